```python
import jax
import jax.numpy as jnp
from jax import lax
import numpy as np

D_MODEL = 2048
BATCH = 2
SEQ = 4096
DEPTH = 2

GRID_W = 64
CTX_LEN = 256

ATT_HEADS = 8
ATT_KV_HEADS = 4
HEAD_DIM = 128
ATT_GROUP = ATT_HEADS // ATT_KV_HEADS
WINDOW = 128
ATT_BLOCK = 128
ROPE_BASE = 10000.0

HG_HEADS = 8
HG_KEY = 128
HG_VAL = (D_MODEL // 2) // HG_HEADS
HG_CHUNK = 64
NORM_EPS = 1e-6

ATT_Q_W = ATT_HEADS * HEAD_DIM
ATT_KV_W = ATT_KV_HEADS * HEAD_DIM
HG_K_W = HG_HEADS * HG_KEY
HG_V_W = HG_HEADS * HG_VAL
MIX_COLS = ('att_q', 'att_k', 'att_v', 'hg_q', 'hg_f_fwd', 'hg_f_bwd', 'hg_i', 'hg_g')
MIX_WIDTHS = (ATT_Q_W, ATT_KV_W, ATT_KV_W, HG_K_W, HG_K_W, HG_K_W, HG_V_W, HG_V_W)
CTX_SIDE_COLS = ('att_k', 'att_v', 'hg_f_fwd', 'hg_f_bwd', 'hg_i')
MIX_IN_W = ATT_Q_W + 2 * ATT_KV_W + 3 * HG_K_W + 2 * HG_V_W
MIX_OUT_IN = ATT_Q_W + HG_V_W

POOL_WINDOWS = (2, 4, 8, 16)
POOL_GROUPS = 4
POOL_CH = D_MODEL // POOL_GROUPS

MOE_GROUPS = 4
MOE_EXPERTS_PER_GROUP = 8
N_EXPERTS = MOE_GROUPS * MOE_EXPERTS_PER_GROUP
MOE_TOP_K = 2
EXPERT_FF = D_MODEL // 4
MOE_BLOCK = 128

LN_EPS = 1e-5
DEEPNORM_ALPHA = (2 * DEPTH) ** 0.25
DEEPNORM_BETA = (8 * DEPTH) ** -0.25

kernel_name = 'hybrid_swa_hgrn2_pool_hmoe_dit'


def _layer_norm(x, g, b):
    xf = x.astype(jnp.float32)
    mu = jnp.mean(xf, axis=-1, keepdims=True)
    var = jnp.mean(jnp.square(xf - mu), axis=-1, keepdims=True)
    y = (xf - mu) * lax.rsqrt(var + LN_EPS) * g.astype(jnp.float32) + b.astype(jnp.float32)
    return y.astype(x.dtype)


def _split_heads(t, n_heads):
    return t.reshape(t.shape[0], t.shape[1], n_heads, -1)


def _project(h, w_in, names):
    starts = dict(zip(MIX_COLS, np.cumsum((0,) + MIX_WIDTHS[:-1]).tolist()))
    widths = dict(zip(MIX_COLS, MIX_WIDTHS))
    if tuple(names) == MIX_COLS:
        w = w_in
    else:
        w = jnp.concatenate([w_in[:, starts[n]:starts[n] + widths[n]] for n in names], axis=1)
    p = h @ w
    cuts = np.cumsum([widths[n] for n in names])[:-1].tolist()
    return dict(zip(names, jnp.split(p, cuts, axis=-1)))


def _axial_rope(t, rows):
    n = t.shape[1]
    half = HEAD_DIM // 2
    n_freq = half // 2
    row = jnp.repeat(jnp.arange(rows), GRID_W).astype(jnp.float32)
    col = (jnp.arange(n) % GRID_W).astype(jnp.float32)
    inv_freq = ROPE_BASE ** (-jnp.arange(n_freq, dtype=jnp.float32) / n_freq)

    def rotate(seg, pos):
        ang = pos[:, None] * inv_freq[None, :]
        cos = jnp.cos(ang)[None, :, None, :]
        sin = jnp.sin(ang)[None, :, None, :]
        a, b = seg[..., :n_freq], seg[..., n_freq:]
        return jnp.concatenate([a * cos - b * sin, b * cos + a * sin], axis=-1)

    tf = t.astype(jnp.float32)
    out = jnp.concatenate([rotate(tf[..., :half], row), rotate(tf[..., half:], col)], axis=-1)
    return out.astype(t.dtype)


def _window_attention(q, k, v, k_ctx, v_ctx, sink):
    b, n = q.shape[0], q.shape[1]
    nb = n // ATT_BLOCK
    n_ctx = k_ctx.shape[1]
    scale = HEAD_DIM ** -0.5
    qb = q.reshape(b, nb, ATT_BLOCK, ATT_KV_HEADS, ATT_GROUP, HEAD_DIM)

    def band(t):
        tp = jnp.pad(t, ((0, 0), (ATT_BLOCK, ATT_BLOCK), (0, 0), (0, 0)))
        tp = tp.reshape(b, nb + 2, ATT_BLOCK, ATT_KV_HEADS, HEAD_DIM)
        return jnp.concatenate([tp[:, :-2], tp[:, 1:-1], tp[:, 2:]], axis=2)

    kb, vb = band(k), band(v)
    q_pos = jnp.arange(n).reshape(nb, ATT_BLOCK)
    k_pos = (jnp.arange(nb) * ATT_BLOCK - ATT_BLOCK)[:, None] + jnp.arange(3 * ATT_BLOCK)[None, :]
    valid = ((jnp.abs(q_pos[:, :, None] - k_pos[:, None, :]) <= WINDOW)
             & (k_pos >= 0)[:, None, :] & (k_pos < n)[:, None, :])
    s_loc = jnp.einsum('bnqhgd,bnkhd->bnhgqk', qb, kb).astype(jnp.float32) * scale
    s_loc = jnp.where(valid[None, :, None, None], s_loc, -jnp.inf)
    s_ctx = jnp.einsum('bnqhgd,bmhd->bnhgqm', qb, k_ctx).astype(jnp.float32) * scale
    s_sink = jnp.broadcast_to(sink.astype(jnp.float32).reshape(1, 1, ATT_KV_HEADS, ATT_GROUP, 1, 1),
                              s_loc.shape[:-1] + (1,))
    p = jax.nn.softmax(jnp.concatenate([s_loc, s_ctx, s_sink], axis=-1), axis=-1).astype(v.dtype)
    nk = 3 * ATT_BLOCK
    o = (jnp.einsum('bnhgqk,bnkhd->bnqhgd', p[..., :nk], vb)
         + jnp.einsum('bnhgqm,bmhd->bnqhgd', p[..., nk:nk + n_ctx], v_ctx))
    return o.reshape(b, n, ATT_HEADS * HEAD_DIM)


def _context_attention(q, k, v, sink):
    b, n_ctx = q.shape[0], q.shape[1]
    qg = q.reshape(b, n_ctx, ATT_KV_HEADS, ATT_GROUP, HEAD_DIM)
    s = jnp.einsum('blhgd,bmhd->bhglm', qg, k).astype(jnp.float32) * HEAD_DIM ** -0.5
    s_sink = jnp.broadcast_to(sink.astype(jnp.float32).reshape(1, ATT_KV_HEADS, ATT_GROUP, 1, 1),
                              s.shape[:-1] + (1,))
    p = jax.nn.softmax(jnp.concatenate([s, s_sink], axis=-1), axis=-1)[..., :n_ctx].astype(v.dtype)
    o = jnp.einsum('bhglm,bmhd->blhgd', p, v)
    return o.reshape(b, n_ctx, ATT_HEADS * HEAD_DIM)


def _hgrn2_gates(f_logit, lb):
    lbh = lb.reshape(HG_HEADS, HG_KEY)
    f = lbh + (1.0 - lbh) * jax.nn.sigmoid(_split_heads(f_logit, HG_HEADS).astype(jnp.float32))
    return 1.0 - f, jnp.log(f)


def _gla_chunked(q, k, v, log_f, s0):
    b, n, h, _ = q.shape
    n_val = v.shape[-1]
    nc = n // HG_CHUNK

    def chunks(t):
        return t.astype(jnp.float32).reshape(b, nc, HG_CHUNK, h, t.shape[-1]).transpose(1, 0, 3, 2, 4)

    incl = jnp.tril(jnp.ones((HG_CHUNK, HG_CHUNK), dtype=bool))[:, :, None]

    def step(state, inp):
        qi, ki, vi, gi = inp
        cum = jnp.cumsum(gi, axis=2)
        cum_end = cum[:, :, -1]
        o_inter = jnp.einsum('bhtk,bhkv->bhtv', qi * jnp.exp(cum), state)
        rel = jnp.exp(jnp.where(incl, cum[:, :, :, None, :] - cum[:, :, None, :, :], -jnp.inf))
        scores = jnp.einsum('bhtk,bhsk,bhtsk->bhts', qi, ki, rel)
        o = o_inter + jnp.einsum('bhts,bhsv->bhtv', scores, vi)
        new_state = (jnp.exp(cum_end)[..., None] * state
                     + jnp.einsum('bhsk,bhsv->bhkv', ki * jnp.exp(cum_end[:, :, None] - cum), vi))
        return new_state, o

    s_fin, o = lax.scan(step, s0.astype(jnp.float32), (chunks(q), chunks(k), chunks(v), chunks(log_f)))
    return o.transpose(1, 0, 3, 2, 4).reshape(b, n, h, n_val), s_fin


def _gla_final_state(k, v, log_f):
    g = log_f.astype(jnp.float32)
    to_end = jnp.flip(jnp.cumsum(jnp.flip(g, axis=1), axis=1), axis=1) - g
    return jnp.einsum('bnhk,bnhv->bhkv', k.astype(jnp.float32) * jnp.exp(to_end), v.astype(jnp.float32))


def _hgrn2_bidir(px, pc, lb_fwd, lb_bwd, ctx_out):
    b = px['hg_i'].shape[0]
    q_x = jax.nn.silu(_split_heads(px['hg_q'], HG_HEADS).astype(jnp.float32))
    v_x = _split_heads(px['hg_i'], HG_HEADS)
    v_c = _split_heads(pc['hg_i'], HG_HEADS)
    q_c = jax.nn.silu(_split_heads(pc['hg_q'], HG_HEADS).astype(jnp.float32)) if ctx_out else None
    o_x, o_c = 0.0, 0.0
    for f_name, lb, reverse in (('hg_f_fwd', lb_fwd, False), ('hg_f_bwd', lb_bwd, True)):
        seq = (lambda t: jnp.flip(t, axis=1)) if reverse else (lambda t: t)
        k_x, g_x = _hgrn2_gates(px[f_name], lb)
        k_c, g_c = _hgrn2_gates(pc[f_name], lb)
        if ctx_out:
            s0 = jnp.zeros((b, HG_HEADS, HG_KEY, HG_VAL), jnp.float32)
            oc, s_ctx = _gla_chunked(seq(q_c), seq(k_c), seq(v_c), seq(g_c), s0)
            o_c = o_c + seq(oc)
        else:
            s_ctx = _gla_final_state(seq(k_c), seq(v_c), seq(g_c))
        ox, _ = _gla_chunked(seq(q_x), seq(k_x), seq(v_x), seq(g_x), s_ctx)
        o_x = o_x + seq(ox)
    return o_x, (o_c if ctx_out else None)


def _gated_rmsnorm(o, gate, gain):
    o = o * lax.rsqrt(jnp.mean(jnp.square(o), axis=-1, keepdims=True) + NORM_EPS)
    o = o.reshape(o.shape[0], o.shape[1], HG_V_W) * gain.astype(jnp.float32)
    return (o * jax.nn.silu(gate.astype(jnp.float32))).astype(gate.dtype)


def _even_mixer(hx, hc, rows, w_in, sink, lb_fwd, lb_bwd, norm_g, w_out, ctx_out):
    px = _project(hx, w_in, MIX_COLS)
    pc = _project(hc, w_in, MIX_COLS if ctx_out else CTX_SIDE_COLS)
    q_x = _axial_rope(_split_heads(px['att_q'], ATT_HEADS), rows)
    k_x = _axial_rope(_split_heads(px['att_k'], ATT_KV_HEADS), rows)
    v_x = _split_heads(px['att_v'], ATT_KV_HEADS)
    k_c = _split_heads(pc['att_k'], ATT_KV_HEADS)
    v_c = _split_heads(pc['att_v'], ATT_KV_HEADS)
    att_x = _window_attention(q_x, k_x, v_x, k_c, v_c, sink)
    o_x, o_c = _hgrn2_bidir(px, pc, lb_fwd, lb_bwd, ctx_out)
    hg_x = _gated_rmsnorm(o_x, px['hg_g'], norm_g)
    y_x = jnp.concatenate([att_x, hg_x.astype(att_x.dtype)], axis=-1) @ w_out
    y_c = None
    if ctx_out:
        att_c = _context_attention(_split_heads(pc['att_q'], ATT_HEADS), k_c, v_c, sink)
        hg_c = _gated_rmsnorm(o_c, pc['hg_g'], norm_g)
        y_c = jnp.concatenate([att_c, hg_c.astype(att_c.dtype)], axis=-1) @ w_out
    return y_x, y_c


def _pool_mixer(h, w_in, w_grp, scale, w_out):
    b, n, _ = h.shape
    u = (h @ w_in).reshape(b, n, POOL_GROUPS, POOL_CH)
    uf = u.astype(jnp.float32)
    pref = jnp.concatenate([jnp.zeros((b, 1, POOL_GROUPS, POOL_CH), jnp.float32),
                            jnp.cumsum(uf, axis=1)], axis=1)
    t = jnp.arange(n)
    means = []
    for gi, w in enumerate(POOL_WINDOWS):
        lo = jnp.clip(t - w // 2, 0, n)
        hi = jnp.clip(t + w - w // 2, 0, n)
        pg = pref[:, :, gi]
        means.append((pg[:, hi] - pg[:, lo]) / (hi - lo).astype(jnp.float32)[None, :, None])
    mixed = (jnp.stack(means, axis=2) - uf).astype(h.dtype)
    y = jnp.einsum('bngc,gcd->bngd', mixed, w_grp).reshape(b, n, D_MODEL) * scale
    return y @ w_out


def _hier_moe(h, w_rg, b_rg, w_re, b_re, w1, w3, w2):
    n_tok, d = h.shape
    g_prob = jax.nn.softmax((h @ w_rg + b_rg).astype(jnp.float32), axis=-1)
    g_val, g_idx = lax.top_k(g_prob, 1)
    e_logits = (h @ w_re + b_re).astype(jnp.float32).reshape(n_tok, MOE_GROUPS, MOE_EXPERTS_PER_GROUP)
    e_logits = jnp.take_along_axis(e_logits, g_idx[:, :, None], axis=1)[:, 0]
    e_val, e_idx = lax.top_k(jax.nn.softmax(e_logits, axis=-1), MOE_TOP_K)
    gate = g_val * e_val / jnp.sum(e_val, axis=-1, keepdims=True)
    expert = g_idx * MOE_EXPERTS_PER_GROUP + e_idx

    n_assign = n_tok * MOE_TOP_K
    flat_e = expert.reshape(n_assign)
    order = jnp.argsort(flat_e)
    e_sorted = flat_e[order]
    counts = jnp.bincount(flat_e, length=N_EXPERTS)
    padded = (counts + MOE_BLOCK - 1) // MOE_BLOCK * MOE_BLOCK
    start = jnp.cumsum(counts) - counts
    pad_end = jnp.cumsum(padded)
    pad_start = pad_end - padded
    dest = pad_start[e_sorted] + jnp.arange(n_assign) - start[e_sorted]
    n_blocks = -(-(n_assign + N_EXPERTS * (MOE_BLOCK - 1)) // MOE_BLOCK)
    n_slots = n_blocks * MOE_BLOCK
    slot_tok = jnp.full((n_slots,), n_tok, jnp.int32).at[dest].set((order // MOE_TOP_K).astype(jnp.int32))
    slot_gate = jnp.zeros((n_slots,), h.dtype).at[dest].set(gate.reshape(n_assign)[order].astype(h.dtype))
    block_expert = jnp.minimum(jnp.searchsorted(pad_end, jnp.arange(n_blocks) * MOE_BLOCK, side='right'),
                               N_EXPERTS - 1)
    h_pad = jnp.concatenate([h, jnp.zeros((1, d), h.dtype)], axis=0)
    xb = h_pad[slot_tok].reshape(n_blocks, MOE_BLOCK, d)

    def expert_block(args):
        xe, e = args
        return (jax.nn.silu(xe @ w1[e]) * (xe @ w3[e])) @ w2[e]

    yb = lax.map(expert_block, (xb, block_expert)).reshape(n_slots, d)
    y = jnp.zeros((n_tok + 1, d), h.dtype).at[slot_tok].add(yb * slot_gate[:, None])
    return y[:n_tok]


def setup_inputs(seed: int = 0) -> dict:
    key = jax.random.key(seed)
    ks = jax.random.split(key, 24)
    d = D_MODEL
    n_even = (DEPTH + 1) // 2
    n_odd = DEPTH // 2

    def nrm(k, shape, s):
        return jax.random.normal(k, shape, jnp.float32) * s

    return {
        'x': nrm(ks[0], (BATCH, SEQ, d), 1.0),
        'c': nrm(ks[1], (BATCH, d), 1.0),
        'ctx': nrm(ks[2], (BATCH, CTX_LEN, d), 1.0),
        'c_ctx': nrm(ks[3], (d,), 1.0),
        'ada_w': nrm(ks[4], (DEPTH, d, 6 * d), 0.5 * d ** -0.5),
        'ada_b': nrm(ks[5], (DEPTH, 6 * d), 0.02),
        'ln_g': 1.0 + nrm(ks[6], (DEPTH, 2, d), 0.02),
        'ln_b': nrm(ks[7], (DEPTH, 2, d), 0.02),
        'mix_w_in': nrm(ks[8], (n_even, d, MIX_IN_W), d ** -0.5),
        'att_sink': nrm(ks[9], (n_even, ATT_HEADS), 0.5),
        'hg_lb': nrm(ks[10], (2, DEPTH + 1, HG_K_W), 0.1),
        'hg_norm_g': 1.0 + nrm(ks[11], (n_even, HG_V_W), 0.02),
        'mix_w_out': nrm(ks[12], (n_even, MIX_OUT_IN, d), MIX_OUT_IN ** -0.5 * DEEPNORM_BETA),
        'pool_w_in': nrm(ks[13], (n_odd, d, d), d ** -0.5),
        'pool_w_grp': nrm(ks[14], (n_odd, POOL_GROUPS, POOL_CH, POOL_CH), POOL_CH ** -0.5),
        'pool_scale': 1.0 + nrm(ks[15], (n_odd, d), 0.1),
        'pool_w_out': nrm(ks[16], (n_odd, d, d), d ** -0.5 * DEEPNORM_BETA),
        'rt_group_w': nrm(ks[17], (DEPTH, d, MOE_GROUPS), d ** -0.5),
        'rt_group_b': nrm(ks[18], (DEPTH, MOE_GROUPS), 0.01),
        'rt_expert_w': nrm(ks[19], (DEPTH, d, N_EXPERTS), d ** -0.5),
        'rt_expert_b': nrm(ks[20], (DEPTH, N_EXPERTS), 0.01),
        'moe_w1': nrm(ks[21], (DEPTH, N_EXPERTS, d, EXPERT_FF), d ** -0.5),
        'moe_w3': nrm(ks[22], (DEPTH, N_EXPERTS, d, EXPERT_FF), d ** -0.5),
        'moe_w2': nrm(ks[23], (DEPTH, N_EXPERTS, EXPERT_FF, d), EXPERT_FF ** -0.5 * DEEPNORM_BETA),
    }


def reference(x, c, ctx, c_ctx, ada_w, ada_b, ln_g, ln_b, mix_w_in, att_sink, hg_lb, hg_norm_g,
              mix_w_out, pool_w_in, pool_w_grp, pool_scale, pool_w_out, rt_group_w, rt_group_b,
              rt_expert_w, rt_expert_b, moe_w1, moe_w3, moe_w2):
    b, n, d = x.shape
    n_ctx = ctx.shape[1]
    rows = n // GRID_W
    lower_bounds = jnp.cumsum(jax.nn.softmax(hg_lb.astype(jnp.float32), axis=1), axis=1)
    silu_c = jax.nn.silu(c)
    silu_cc = jax.nn.silu(c_ctx)
    for l in range(DEPTH):
        even = l % 2 == 0
        ctx_next = any(j % 2 == 0 for j in range(l + 1, DEPTH))
        mod = (silu_c @ ada_w[l] + ada_b[l])[:, None, :]
        sh1, sc1, g1, sh2, sc2, g2 = jnp.split(mod, 6, axis=-1)
        hx = x * (1 + sc1) + sh1
        if even or ctx_next:
            n_cols = 6 * d if ctx_next else 2 * d
            mc = jnp.split(silu_cc @ ada_w[l][:, :n_cols] + ada_b[l][:n_cols], n_cols // d)
            hc = ctx * (1 + mc[1]) + mc[0]
        if even:
            e = l // 2
            y_x, y_c = _even_mixer(hx, hc, rows, mix_w_in[e], att_sink[e], lower_bounds[0, l],
                                   lower_bounds[1, l], hg_norm_g[e], mix_w_out[e], ctx_next)
        else:
            o = l // 2
            y_x = _pool_mixer(hx, pool_w_in[o], pool_w_grp[o], pool_scale[o], pool_w_out[o])
            y_c = _pool_mixer(hc, pool_w_in[o], pool_w_grp[o], pool_scale[o], pool_w_out[o]) if ctx_next else None
        x = _layer_norm(DEEPNORM_ALPHA * x + g1 * y_x, ln_g[l, 0], ln_b[l, 0])
        tokens = (x * (1 + sc2) + sh2).reshape(b * n, d)
        if ctx_next:
            ctx = _layer_norm(DEEPNORM_ALPHA * ctx + mc[2] * y_c, ln_g[l, 0], ln_b[l, 0])
            tokens = jnp.concatenate([tokens, (ctx * (1 + mc[4]) + mc[3]).reshape(b * n_ctx, d)], axis=0)
        y = _hier_moe(tokens, rt_group_w[l], rt_group_b[l], rt_expert_w[l], rt_expert_b[l],
                      moe_w1[l], moe_w3[l], moe_w2[l])
        x = _layer_norm(DEEPNORM_ALPHA * x + g2 * y[:b * n].reshape(b, n, d), ln_g[l, 1], ln_b[l, 1])
        if ctx_next:
            ctx = _layer_norm(DEEPNORM_ALPHA * ctx + mc[5] * y[b * n:].reshape(b, n_ctx, d),
                              ln_g[l, 1], ln_b[l, 1])
    return x
```

```python
import functools

import jax
import jax.numpy as jnp
from jax import lax
from jax.experimental import pallas as pl
from jax.experimental.pallas import tpu as pltpu

F32 = jnp.float32
BF16 = jnp.bfloat16

LANES = 128
VMEM_LIMIT_BYTES = 56 * 1024 * 1024

GRID_W = 64
ATT_HEADS = 8
ATT_KV_HEADS = 4
HEAD_DIM = 128
WINDOW = 128
ATT_BLOCK = 128
ROPE_BASE = 10000.0
HG_HEADS = 8
HG_KEY = 128
HG_CHUNK = 64
HG_SUB = 16
NORM_EPS = 1e-6
POOL_WINDOWS = (2, 4, 8, 16)
POOL_HALO = 8
MOE_GROUPS = 4
MOE_EXPERTS_PER_GROUP = 8
N_EXPERTS = MOE_GROUPS * MOE_EXPERTS_PER_GROUP
MOE_TOP_K = 2
MOE_ROWS = 256
MOE_ROW_ALIGN = 8
LN_EPS = 1e-5
DEPTH = 2
DEEPNORM_ALPHA = (2 * DEPTH) ** 0.25


def _dot(a, b):
    return jnp.dot(a, b, preferred_element_type=F32)


def _dot_nt(a, b):
    return lax.dot_general(a, b, (((1,), (1,)), ((), ())), preferred_element_type=F32)


def _dot_tn(a, b):
    return lax.dot_general(a, b, (((0,), (0,)), ((), ())), preferred_element_type=F32)


def _sigmoid(x):
    return 1.0 / (1.0 + jnp.exp(-x))


def _silu(x):
    return x * _sigmoid(x)


def _params(*sem):
    return pltpu.CompilerParams(dimension_semantics=sem, vmem_limit_bytes=VMEM_LIMIT_BYTES)


def _layer_norm(z, g, b):
    mu = jnp.mean(z, axis=-1, keepdims=True)
    zc = z - mu
    var = jnp.mean(zc * zc, axis=-1, keepdims=True)
    return zc * lax.rsqrt(var + LN_EPS) * g + b


def _ada_kernel(s_ref, w_ref, b_ref, o_ref):
    s = _silu(s_ref[...]).astype(BF16)
    o_ref[0] = _dot(s, w_ref[0].astype(BF16)) + b_ref[0]


def _ada_mod(s, ada_w, ada_b, tn=1024):
    n_l, d, n = ada_w.shape
    return pl.pallas_call(
        _ada_kernel,
        grid=(n_l, n // tn),
        in_specs=[
            pl.BlockSpec((8, d), lambda l, j: (0, 0)),
            pl.BlockSpec((1, d, tn), lambda l, j: (l, 0, j)),
            pl.BlockSpec((1, 1, tn), lambda l, j: (l, 0, j)),
        ],
        out_specs=pl.BlockSpec((1, 8, tn), lambda l, j: (l, 0, j)),
        out_shape=jax.ShapeDtypeStruct((n_l, 8, n), F32),
        compiler_params=_params("parallel", "parallel"),
        name="ada_mod",
    )(s, ada_w, ada_b.reshape(n_l, 1, n))


def _modmm_kernel(x_ref, sc_ref, sh_ref, w_ref, o_ref, xs_ref):
    @pl.when(pl.program_id(1) == 0)
    def _():
        xs_ref[...] = (x_ref[...] * (1.0 + sc_ref[0]) + sh_ref[0]).astype(BF16)

    o_ref[...] = _dot(xs_ref[...], w_ref[...])


def _mod_matmul(x, sc, sh, w, col_map, n_out, tm, tn):
    m, k = x.shape
    rows_per_mod = m // sc.shape[0]
    return pl.pallas_call(
        _modmm_kernel,
        grid=(m // tm, n_out // tn),
        in_specs=[
            pl.BlockSpec((tm, k), lambda i, j: (i, 0)),
            pl.BlockSpec((1, 1, k), lambda i, j: (i * tm // rows_per_mod, 0, 0)),
            pl.BlockSpec((1, 1, k), lambda i, j: (i * tm // rows_per_mod, 0, 0)),
            pl.BlockSpec((k, tn), lambda i, j: (0, col_map(j))),
        ],
        out_specs=pl.BlockSpec((tm, tn), lambda i, j: (i, j)),
        out_shape=jax.ShapeDtypeStruct((m, n_out), F32),
        scratch_shapes=[pltpu.VMEM((tm, k), BF16)],
        compiler_params=_params("parallel", "arbitrary"),
        name="mod_matmul",
    )(x, sc, sh, w)


def _rope(t, cos, sin_signed, first_half):
    partner = jnp.where(first_half, pltpu.roll(t, 96, 1), pltpu.roll(t, 32, 1))
    return t * cos + partner * sin_signed


def _attn_kernel(sink_ref, q_ref, kp_ref, kc_ref, kn_ref, vp_ref, vc_ref, vn_ref, kx_ref, vx_ref,
                 cp_ref, cc_ref, cn_ref, sp_ref, sc_ref, sn_ref, o_ref, *, n_seq):
    kvh = pl.program_id(1)
    n = pl.program_id(2)
    blk = ATT_BLOCK
    lane = lax.broadcasted_iota(jnp.int32, (1, HEAD_DIM), 1)
    first_half = (lane % 64) < 32
    scale = HEAD_DIM ** -0.5

    cos_c, sin_c = cc_ref[...], sc_ref[...]
    qa = _rope(q_ref[:, :HEAD_DIM], cos_c, sin_c, first_half)
    qb = _rope(q_ref[:, HEAD_DIM:], cos_c, sin_c, first_half)
    q2 = (jnp.concatenate([qa, qb], axis=0) * scale).astype(BF16)
    kcat = jnp.concatenate([
        _rope(kp_ref[...], cp_ref[...], sp_ref[...], first_half),
        _rope(kc_ref[...], cos_c, sin_c, first_half),
        _rope(kn_ref[...], cn_ref[...], sn_ref[...], first_half)], axis=0).astype(BF16)
    vcat = jnp.concatenate([vp_ref[...], vc_ref[...], vn_ref[...]], axis=0).astype(BF16)

    s_loc = _dot_nt(q2, kcat)
    s_ctx = _dot_nt(q2, kx_ref[...].astype(BF16))
    row = lax.broadcasted_iota(jnp.int32, s_loc.shape, 0)
    col = lax.broadcasted_iota(jnp.int32, s_loc.shape, 1)
    q_pos = n * blk + (row % blk)
    k_pos = (n - 1) * blk + col
    valid = (jnp.abs(q_pos - k_pos) <= WINDOW) & (k_pos >= 0) & (k_pos < n_seq)
    s_loc = jnp.where(valid, s_loc, -jnp.inf)
    row1 = lax.broadcasted_iota(jnp.int32, (2 * blk, 1), 0)
    sink = jnp.where(row1 < blk, sink_ref[2 * kvh], sink_ref[2 * kvh + 1])
    m = jnp.maximum(jnp.maximum(jnp.max(s_loc, axis=-1, keepdims=True),
                                jnp.max(s_ctx, axis=-1, keepdims=True)), sink)
    p_loc = jnp.exp(s_loc - m)
    p_ctx = jnp.exp(s_ctx - m)
    den = (jnp.sum(p_loc, axis=-1, keepdims=True) + jnp.sum(p_ctx, axis=-1, keepdims=True)
           + jnp.exp(sink - m))
    o = (_dot(p_loc.astype(BF16), vcat) + _dot(p_ctx.astype(BF16), vx_ref[...].astype(BF16))) / den
    o_ref[:, :HEAD_DIM] = o[:blk].astype(o_ref.dtype)
    o_ref[:, HEAD_DIM:] = o[blk:].astype(o_ref.dtype)


def _window_attention(p, pc, sink, cos, sin_signed, batch, n_seq, n_ctx):
    nb = n_seq // ATT_BLOCK
    kcol = ATT_HEADS
    vcol = ATT_HEADS + ATT_KV_HEADS

    def rows(off):
        return lambda b, h, n, off=off: b * nb + jnp.clip(n + off, 0, nb - 1)

    def tab(off):
        return lambda b, h, n, off=off: (jnp.clip(n + off, 0, nb - 1), 0)

    kspec = [pl.BlockSpec((ATT_BLOCK, HEAD_DIM), lambda b, h, n, r=rows(o): (r(b, h, n), kcol + h))
             for o in (-1, 0, 1)]
    vspec = [pl.BlockSpec((ATT_BLOCK, HEAD_DIM), lambda b, h, n, r=rows(o): (r(b, h, n), vcol + h))
             for o in (-1, 0, 1)]
    tspec = [pl.BlockSpec((ATT_BLOCK, HEAD_DIM), tab(o)) for o in (-1, 0, 1)]
    return pl.pallas_call(
        functools.partial(_attn_kernel, n_seq=n_seq),
        grid=(batch, ATT_KV_HEADS, nb),
        in_specs=[pl.BlockSpec(memory_space=pltpu.SMEM),
                  pl.BlockSpec((ATT_BLOCK, 2 * HEAD_DIM), lambda b, h, n: (b * nb + n, h))]
        + kspec + vspec
        + [pl.BlockSpec((n_ctx, HEAD_DIM), lambda b, h, n: (b, h)),
           pl.BlockSpec((n_ctx, HEAD_DIM), lambda b, h, n: (b, ATT_KV_HEADS + h))]
        + tspec + tspec,
        out_specs=pl.BlockSpec((ATT_BLOCK, 2 * HEAD_DIM), lambda b, h, n: (b * nb + n, h)),
        out_shape=jax.ShapeDtypeStruct((batch * n_seq, ATT_HEADS * HEAD_DIM), BF16),
        compiler_params=_params("parallel", "parallel", "parallel"),
        name="window_attention",
    )(sink, p, p, p, p, p, p, p, pc, pc, cos, cos, cos, sin_signed, sin_signed, sin_signed)


def _gla_tile(zf, q_raw, v, lb, st_ref, o_ref, rev):
    c_len = zf.shape[0]
    f = lb + (1.0 - lb) * _sigmoid(zf)
    k = 1.0 - f
    g = jnp.log(f)
    ti = lax.broadcasted_iota(jnp.int32, (c_len, c_len), 0)
    si = lax.broadcasted_iota(jnp.int32, (c_len, c_len), 1)
    tri = jnp.where((si >= ti) if rev else (si <= ti), 1.0, 0.0).astype(BF16)
    g1 = g.astype(BF16)
    r1 = g - g1.astype(F32)
    g2 = r1.astype(BF16)
    g3 = (r1 - g2.astype(F32)).astype(BF16)
    c = _dot(tri, g1) + _dot(tri, g2) + _dot(tri, g3)
    c_end = c[0:1] if rev else c[c_len - 1:c_len]
    k_end = (k * jnp.exp(c_end - c)).astype(BF16)
    dec = jnp.exp(c_end)
    vb = v.astype(BF16)

    if o_ref is not None:
        q = _silu(q_raw)
        q_in = (q * jnp.exp(c)).astype(BF16)
        pairs = []
        size = c_len // 2
        while size >= HG_SUB:
            for lo in range(0, c_len, 2 * size):
                pairs.append((lo, lo + size, lo + 2 * size))
            size //= 2
        scaled = []
        for lo, mid, hi in pairs:
            if rev:
                late, early, bnd = slice(lo, mid), slice(mid, hi), mid
            else:
                late, early, bnd = slice(mid, hi), slice(lo, mid), mid - 1
            cb = c[bnd:bnd + 1]
            q_l = (q[late] * jnp.exp(c[late] - cb)).astype(BF16)
            k_e = (k[early] * jnp.exp(cb - c[early])).astype(BF16)
            scaled.append((late, early, q_l, k_e))
        n_sub = c_len // HG_SUB
        t_idx = lax.broadcasted_iota(jnp.int32, (HG_SUB, 1), 0)
        diag = [[None] * HG_HEADS for _ in range(n_sub)]
        for b in range(n_sub):
            r0 = b * HG_SUB
            qb, cb = q[r0:r0 + HG_SUB], c[r0:r0 + HG_SUB]
            for s in range(HG_SUB):
                row = r0 + s
                ok = (t_idx <= s) if rev else (t_idx >= s)
                w = qb * k[row:row + 1] * jnp.exp(jnp.where(ok, cb - c[row:row + 1], -jnp.inf))
                for h in range(HG_HEADS):
                    sl = slice(h * HG_KEY, (h + 1) * HG_KEY)
                    contrib = jnp.sum(w[:, sl], axis=-1, keepdims=True) * v[row:row + 1, sl]
                    diag[b][h] = contrib if diag[b][h] is None else diag[b][h] + contrib

    for h in range(HG_HEADS):
        sl = slice(h * HG_KEY, (h + 1) * HG_KEY)
        st = st_ref[h]
        if o_ref is not None:
            o_h = _dot_nt(q_in[:, sl], st.astype(BF16))
            parts = [diag[b][h] for b in range(n_sub)]
            for late, early, q_l, k_e in scaled:
                sc = _dot_nt(q_l[:, sl], k_e[:, sl]).astype(BF16)
                add = _dot(sc, vb[early, sl])
                b0 = late.start // HG_SUB
                for j in range((late.stop - late.start) // HG_SUB):
                    parts[b0 + j] = parts[b0 + j] + add[j * HG_SUB:(j + 1) * HG_SUB]
            o_ref[:, sl] = o_h + jnp.concatenate(parts, axis=0)
        st_ref[h] = st * dec[:, sl] + _dot_tn(vb[:, sl], k_end[:, sl])


def _gla_kernel(lb_ref, zf_ref, q_ref, v_ref, zfc_ref, vc_ref, o_ref, st_ref, *, rev, n_ctx_chunks):
    s = pl.program_id(1)

    @pl.when(s == 0)
    def _():
        st_ref[...] = jnp.zeros_like(st_ref)

    x = lb_ref[...]
    e = jnp.exp(x - jnp.max(x, axis=0, keepdims=True))
    lb = e[0:1] / jnp.sum(e, axis=0, keepdims=True)

    @pl.when(s < n_ctx_chunks)
    def _():
        _gla_tile(zfc_ref[...], None, vc_ref[...], lb, st_ref, None, rev)

    @pl.when(s >= n_ctx_chunks)
    def _():
        _gla_tile(zf_ref[...], q_ref[...], v_ref[...], lb, st_ref, o_ref, rev)


def _hgrn2_scan(p, pc, hg_lb, batch, n_seq, n_ctx, rev):
    hk = HG_HEADS * HG_KEY
    nc, ncc = n_seq // HG_CHUNK, n_ctx // HG_CHUNK
    d = 1 if rev else 0

    def lat(b, s):
        j = jnp.maximum(s - ncc, 0)
        return b * nc + (nc - 1 - j if rev else j)

    def ctx(b, s):
        j = jnp.minimum(s, ncc - 1)
        return b * ncc + (ncc - 1 - j if rev else j)

    return pl.pallas_call(
        functools.partial(_gla_kernel, rev=rev, n_ctx_chunks=ncc),
        grid=(batch, ncc + nc),
        in_specs=[
            pl.BlockSpec((None, hg_lb.shape[1], hk), lambda b, s: (d, 0, 0)),
            pl.BlockSpec((HG_CHUNK, hk), lambda b, s: (lat(b, s), 3 + d)),
            pl.BlockSpec((HG_CHUNK, hk), lambda b, s: (lat(b, s), 2)),
            pl.BlockSpec((HG_CHUNK, hk), lambda b, s: (lat(b, s), 5)),
            pl.BlockSpec((HG_CHUNK, hk), lambda b, s: (ctx(b, s), 1 + d)),
            pl.BlockSpec((HG_CHUNK, hk), lambda b, s: (ctx(b, s), 3)),
        ],
        out_specs=pl.BlockSpec((HG_CHUNK, hk), lambda b, s: (lat(b, s), 0)),
        out_shape=jax.ShapeDtypeStruct((batch * n_seq, hk), F32),
        scratch_shapes=[pltpu.VMEM((HG_HEADS, HG_KEY, HG_KEY), F32)],
        compiler_params=_params("parallel", "arbitrary"),
        name="hgrn2_bwd" if rev else "hgrn2_fwd",
    )(hg_lb, p, p, p, pc, pc)


def _route(tok, wr_hi, wr_lo, rb):
    t_hi = tok.astype(BF16)
    t_lo = (tok - t_hi.astype(F32)).astype(BF16)
    lg = _dot(t_hi, wr_hi) + _dot(t_hi, wr_lo) + _dot(t_lo, wr_hi) + rb
    lane = lax.broadcasted_iota(jnp.int32, lg.shape, 1)
    lane_f = lane.astype(F32)
    ninf = -jnp.inf
    gl = jnp.where(lane < MOE_GROUPS, lg, ninf)
    gmax = jnp.max(gl, axis=-1, keepdims=True)
    g_idx = jnp.min(jnp.where(gl == gmax, lane_f, float(LANES)), axis=-1, keepdims=True)
    g_val = 1.0 / jnp.sum(jnp.exp(gl - gmax), axis=-1, keepdims=True)
    e_lane = lane_f - float(MOE_GROUPS)
    lo = g_idx * float(MOE_EXPERTS_PER_GROUP)
    in_grp = (e_lane >= lo) & (e_lane < lo + float(MOE_EXPERTS_PER_GROUP))
    el = jnp.where(in_grp, lg, ninf)
    l1 = jnp.max(el, axis=-1, keepdims=True)
    i1 = jnp.min(jnp.where(el == l1, e_lane, float(LANES)), axis=-1, keepdims=True)
    el2 = jnp.where(e_lane == i1, ninf, el)
    l2 = jnp.max(el2, axis=-1, keepdims=True)
    i2 = jnp.min(jnp.where(el2 == l2, e_lane, float(LANES)), axis=-1, keepdims=True)
    r = jnp.exp(l2 - l1)
    w1 = g_val / (1.0 + r)
    w2 = w1 * r
    eid = jnp.where(lane == 0, i1, jnp.where(lane == 1, i2, 0.0)).astype(jnp.int32)
    gate = jnp.where(lane == 0, w1, jnp.where(lane == 1, w2, 0.0))
    return eid, gate


def _post_mix(y, x_ref, g1_ref, sc2_ref, sh2_ref, lng_ref, lnb_ref, wrh_ref, wrl_ref, rb_ref,
              x1_ref, tok_ref, eid_ref, gate_ref):
    x1 = _layer_norm(DEEPNORM_ALPHA * x_ref[...] + g1_ref[0] * y, lng_ref[...], lnb_ref[...])
    x1_ref[...] = x1
    tok = x1 * (1.0 + sc2_ref[0]) + sh2_ref[0]
    tok_ref[...] = tok
    eid, gate = _route(tok, wrh_ref[...], wrl_ref[...], rb_ref[...])
    eid_ref[...] = eid
    gate_ref[...] = gate


def _even_out_kernel(att_ref, of_ref, ob_ref, gt_ref, ng_ref, wo_ref, *rest):
    o = of_ref[...] + ob_ref[...]
    pieces = []
    for h in range(HG_HEADS):
        oh = o[:, h * HG_KEY:(h + 1) * HG_KEY]
        pieces.append(oh * lax.rsqrt(jnp.mean(oh * oh, axis=-1, keepdims=True) + NORM_EPS))
    hg = (jnp.concatenate(pieces, axis=-1) * ng_ref[...] * _silu(gt_ref[...])).astype(BF16)
    n_att = att_ref.shape[1]
    y = _dot(att_ref[...], wo_ref[:n_att, :]) + _dot(hg, wo_ref[n_att:, :])
    _post_mix(y, *rest)


def _post_specs(d, tm, rows_per_batch):
    def bmap(i):
        return (i * tm // rows_per_batch, 0, 0)

    row = pl.BlockSpec((tm, d), lambda i: (i, 0))
    mod = pl.BlockSpec((1, 1, d), bmap)
    vec = pl.BlockSpec((1, d), lambda i: (0, 0))
    rw = pl.BlockSpec((d, LANES), lambda i: (0, 0))
    in_specs = [row, mod, mod, mod, vec, vec, rw, rw, pl.BlockSpec((1, LANES), lambda i: (0, 0))]
    lane_blk = pl.BlockSpec((tm, LANES), lambda i: (i, 0))
    out_specs = [row, row, lane_blk, lane_blk]
    return in_specs, out_specs


def _post_out_shapes(t, d):
    return [jax.ShapeDtypeStruct((t, d), F32), jax.ShapeDtypeStruct((t, d), F32),
            jax.ShapeDtypeStruct((t, LANES), jnp.int32), jax.ShapeDtypeStruct((t, LANES), F32)]


def _even_out(att, o_f, o_b, p, norm_g, w_out, x, g1, sc2, sh2, lng, lnb, wr_hi, wr_lo, rb, rows_per_batch, tm=256):
    t, d = x.shape
    hv = o_f.shape[1]
    post_in, post_out = _post_specs(d, tm, rows_per_batch)
    return pl.pallas_call(
        _even_out_kernel,
        grid=(t // tm,),
        in_specs=[
            pl.BlockSpec((tm, att.shape[1]), lambda i: (i, 0)),
            pl.BlockSpec((tm, hv), lambda i: (i, 0)),
            pl.BlockSpec((tm, hv), lambda i: (i, 0)),
            pl.BlockSpec((tm, hv), lambda i: (i, 6)),
            pl.BlockSpec((1, hv), lambda i: (0, 0)),
            pl.BlockSpec(w_out.shape, lambda i: (0, 0)),
        ] + post_in,
        out_specs=post_out,
        out_shape=_post_out_shapes(t, d),
        compiler_params=_params("parallel"),
        name="even_out",
    )(att, o_f, o_b, p, norm_g, w_out, x, g1, sc2, sh2, lng, lnb, wr_hi, wr_lo, rb)


def _combine(x_ref, ya_ref, yb_ref, gate_ref, g2_ref, lng_ref, lnb_ref):
    gate = gate_ref[...]
    y = gate[:, 0:1] * ya_ref[...] + gate[:, 1:2] * yb_ref[...]
    return _layer_norm(DEEPNORM_ALPHA * x_ref[...] + g2_ref[0] * y, lng_ref[...], lnb_ref[...])


def _combine_proj_kernel(x_ref, ya_ref, yb_ref, gate_ref, g2_ref, lng_ref, lnb_ref, sc_ref, sh_ref, w_ref,
                         x2_ref, u_ref):
    x2 = _combine(x_ref, ya_ref, yb_ref, gate_ref, g2_ref, lng_ref, lnb_ref)
    x2_ref[...] = x2
    u_ref[...] = _dot((x2 * (1.0 + sc_ref[0]) + sh_ref[0]).astype(BF16), w_ref[...])


def _combine_kernel(x_ref, ya_ref, yb_ref, gate_ref, g2_ref, lng_ref, lnb_ref, x2_ref):
    x2_ref[...] = _combine(x_ref, ya_ref, yb_ref, gate_ref, g2_ref, lng_ref, lnb_ref)


def _combine_call(x1, y2, gate, g2, lng, lnb, rows_per_batch, proj=None, tm=256):
    t, d = x1.shape
    nt = t // tm

    def bmap(i):
        return (i * tm // rows_per_batch, 0, 0)

    row = pl.BlockSpec((tm, d), lambda i: (i, 0))
    mod = pl.BlockSpec((1, 1, d), bmap)
    vec = pl.BlockSpec((1, d), lambda i: (0, 0))
    in_specs = [row, row, pl.BlockSpec((tm, d), lambda i: (nt + i, 0)),
                pl.BlockSpec((tm, LANES), lambda i: (i, 0)), mod, vec, vec]
    args = [x1, y2, y2, gate, g2, lng, lnb]
    if proj is None:
        return pl.pallas_call(
            _combine_kernel, grid=(nt,), in_specs=in_specs, out_specs=row,
            out_shape=jax.ShapeDtypeStruct((t, d), F32),
            compiler_params=_params("parallel"), name="combine_ln")(*args)
    sc, sh, w = proj
    return pl.pallas_call(
        _combine_proj_kernel, grid=(nt,),
        in_specs=in_specs + [mod, mod, pl.BlockSpec(w.shape, lambda i: (0, 0))],
        out_specs=[row, pl.BlockSpec((tm, w.shape[1]), lambda i: (i, 0))],
        out_shape=[jax.ShapeDtypeStruct((t, d), F32), jax.ShapeDtypeStruct((t, w.shape[1]), F32)],
        compiler_params=_params("parallel"), name="combine_ln_proj")(*args, sc, sh, w)


def _pool_out_kernel(up_ref, uc_ref, un_ref, wg_ref, ps_ref, wo_ref, *rest, n_seq):
    tm, d = uc_ref.shape
    n_grp = len(POOL_WINDOWS)
    ch = d // n_grp
    halo = POOL_HALO
    pos0 = (pl.program_id(0) * tm) % n_seq
    e_pos = pos0 - halo + lax.broadcasted_iota(jnp.int32, (tm + 2 * halo, 1), 0)
    e_ok = (e_pos >= 0) & (e_pos < n_seq)
    t_pos = pos0 + lax.broadcasted_iota(jnp.int32, (tm, 1), 0)
    y = None
    for gi, w in enumerate(POOL_WINDOWS):
        cs = slice(gi * ch, (gi + 1) * ch)
        u = uc_ref[:, cs]
        ext = jnp.where(e_ok, jnp.concatenate([up_ref[:, cs], u, un_ref[:, cs]], axis=0), 0.0)
        a, span = ext, 1
        while span < w:
            a = a[:a.shape[0] - span] + a[span:]
            span *= 2
        start = halo - w // 2
        win = a[start:start + tm]
        cnt = (jnp.minimum(t_pos + (w - w // 2), n_seq) - jnp.maximum(t_pos - w // 2, 0)).astype(F32)
        mixed = (win / cnt - u).astype(BF16)
        z = (_dot(mixed, wg_ref[gi]) * ps_ref[:, cs]).astype(BF16)
        part = _dot(z, wo_ref[cs, :])
        y = part if y is None else y + part
    _post_mix(y, *rest)


def _pool_out(u, w_grp, scale, w_out, x, g1, sc2, sh2, lng, lnb, wr_hi, wr_lo, rb, n_seq, tm=256):
    t, d = x.shape
    hb = tm // POOL_HALO
    n_hb = t // POOL_HALO
    post_in, post_out = _post_specs(d, tm, n_seq)
    return pl.pallas_call(
        functools.partial(_pool_out_kernel, n_seq=n_seq),
        grid=(t // tm,),
        in_specs=[
            pl.BlockSpec((POOL_HALO, d), lambda i: (jnp.maximum(i * hb - 1, 0), 0)),
            pl.BlockSpec((tm, d), lambda i: (i, 0)),
            pl.BlockSpec((POOL_HALO, d), lambda i: (jnp.minimum((i + 1) * hb, n_hb - 1), 0)),
            pl.BlockSpec(w_grp.shape, lambda i: (0, 0, 0)),
            pl.BlockSpec((1, d), lambda i: (0, 0)),
            pl.BlockSpec(w_out.shape, lambda i: (0, 0)),
        ] + post_in,
        out_specs=post_out,
        out_shape=_post_out_shapes(t, d),
        compiler_params=_params("parallel"),
        name="pool_out",
    )(u, u, u, w_grp, scale, w_out, x, g1, sc2, sh2, lng, lnb, wr_hi, wr_lo, rb)


def _moe_kernel(be_ref, nv_ref, idx_ref, tok_hbm, w1_ref, w3_ref, w2_ref, y_hbm,
                xbuf, ybuf, w1b, w3b, w2b, gsem, ssem, *, n_tok):
    i = pl.program_id(0)
    nv = pl.multiple_of(nv_ref[i], MOE_ROW_ALIGN)

    @pl.when(i == 0)
    def _():
        xbuf[...] = jnp.zeros_like(xbuf)
        spare = pltpu.make_async_copy(xbuf.at[pl.ds(0, MOE_ROW_ALIGN)],
                                      y_hbm.at[pl.ds(MOE_TOP_K * n_tok, MOE_ROW_ALIGN)], ssem)
        spare.start()
        spare.wait()

    @pl.when(nv > 0)
    def _():
        prev = be_ref[jnp.maximum(i - 1, 0)]

        @pl.when((i == 0) | (be_ref[i] != prev))
        def _():
            w1b[...] = w1_ref[0].astype(BF16)
            w3b[...] = w3_ref[0].astype(BF16)
            w2b[...] = w2_ref[0].astype(BF16)

        def gather(r, carry):
            tok = idx_ref[0, 0, r] & (n_tok - 1)
            pltpu.make_async_copy(tok_hbm.at[pl.ds(tok, 1)], xbuf.at[pl.ds(r, 1)], gsem).start()
            return carry

        lax.fori_loop(0, nv, gather, 0)
        pltpu.make_async_copy(tok_hbm.at[pl.ds(0, nv)], xbuf.at[pl.ds(0, nv)], gsem).wait()

        xb = xbuf[...].astype(BF16)
        h = (_silu(_dot(xb, w1b[...])) * _dot(xb, w3b[...])).astype(BF16)
        ybuf[...] = _dot(h, w2b[...])

        def scatter(r, carry):
            dst = idx_ref[0, 0, r]
            pltpu.make_async_copy(ybuf.at[pl.ds(r, 1)], y_hbm.at[pl.ds(dst, 1)], ssem).start()
            return carry

        lax.fori_loop(0, nv, scatter, 0)
        pltpu.make_async_copy(ybuf.at[pl.ds(0, nv)], y_hbm.at[pl.ds(0, nv)], ssem).wait()


def _moe_dispatch(eid, n_blocks):
    n_tok = eid.shape[0]
    n_assign = n_tok * MOE_TOP_K
    flat_e = eid.reshape(n_assign)
    order = jnp.argsort(flat_e, stable=True).astype(jnp.int32)
    e_sorted = flat_e[order]
    counts = jnp.bincount(flat_e, length=N_EXPERTS).astype(jnp.int32)
    padded = (counts + MOE_ROWS - 1) // MOE_ROWS * MOE_ROWS
    start = jnp.cumsum(counts) - counts
    pad_end = jnp.cumsum(padded)
    pad_start = pad_end - padded
    dest = pad_start[e_sorted] + jnp.arange(n_assign, dtype=jnp.int32) - start[e_sorted]
    dst_row = (order % MOE_TOP_K) * n_tok + order // MOE_TOP_K
    spare = MOE_TOP_K * n_tok + jnp.arange(n_blocks * MOE_ROWS, dtype=jnp.int32) % MOE_ROW_ALIGN
    slot = spare.at[dest].set(dst_row)
    blk0 = jnp.arange(n_blocks, dtype=jnp.int32) * MOE_ROWS
    be = jnp.minimum(jnp.searchsorted(pad_end, blk0, side='right'), N_EXPERTS - 1).astype(jnp.int32)
    nv = jnp.clip(counts[be] - (blk0 - pad_start[be]), 0, MOE_ROWS).astype(jnp.int32)
    nv = jnp.where(blk0 < pad_end[-1], nv, 0)
    nv = (nv + MOE_ROW_ALIGN - 1) // MOE_ROW_ALIGN * MOE_ROW_ALIGN
    return slot.reshape(n_blocks, 1, MOE_ROWS), be, nv


def _moe_experts(tok, eid, w1, w3, w2):
    n_tok, d = tok.shape
    assert n_tok & (n_tok - 1) == 0
    ff = w1.shape[2]
    n_assign = n_tok * MOE_TOP_K
    n_blocks = -(-(n_assign + N_EXPERTS * (MOE_ROWS - 1)) // MOE_ROWS)
    slot, be, nv = _moe_dispatch(eid, n_blocks)
    grid_spec = pltpu.PrefetchScalarGridSpec(
        num_scalar_prefetch=2,
        grid=(n_blocks,),
        in_specs=[
            pl.BlockSpec((1, 1, MOE_ROWS), lambda i, be, nv: (i, 0, 0), memory_space=pltpu.SMEM),
            pl.BlockSpec(memory_space=pl.ANY),
            pl.BlockSpec((1, d, ff), lambda i, be, nv: (be[i], 0, 0)),
            pl.BlockSpec((1, d, ff), lambda i, be, nv: (be[i], 0, 0)),
            pl.BlockSpec((1, ff, d), lambda i, be, nv: (be[i], 0, 0)),
        ],
        out_specs=pl.BlockSpec(memory_space=pl.ANY),
        scratch_shapes=[
            pltpu.VMEM((MOE_ROWS, d), F32), pltpu.VMEM((MOE_ROWS, d), F32),
            pltpu.VMEM((d, ff), BF16), pltpu.VMEM((d, ff), BF16), pltpu.VMEM((ff, d), BF16),
            pltpu.SemaphoreType.DMA(()), pltpu.SemaphoreType.DMA(()),
        ],
    )
    return pl.pallas_call(
        functools.partial(_moe_kernel, n_tok=n_tok),
        grid_spec=grid_spec,
        out_shape=jax.ShapeDtypeStruct((MOE_TOP_K * n_tok + MOE_ROW_ALIGN, d), F32),
        compiler_params=_params("arbitrary"),
        name="moe_experts",
    )(be, nv, slot, tok, w1, w3, w2)


def _rope_tables(n_seq):
    half = HEAD_DIM // 2
    n_freq = half // 2
    t = jnp.arange(n_seq)
    row = (t // GRID_W).astype(F32)
    col = (t % GRID_W).astype(F32)
    inv_freq = ROPE_BASE ** (-jnp.arange(n_freq, dtype=F32) / n_freq)
    ang_r = row[:, None] * inv_freq[None, :]
    ang_c = col[:, None] * inv_freq[None, :]
    cos = jnp.concatenate([jnp.cos(ang_r)] * 2 + [jnp.cos(ang_c)] * 2, axis=-1)
    sin = jnp.concatenate([-jnp.sin(ang_r), jnp.sin(ang_r), -jnp.sin(ang_c), jnp.sin(ang_c)], axis=-1)
    return cos, sin


def _router_weights(w_g, b_g, w_e, b_e):
    d = w_g.shape[0]
    n = w_g.shape[1] + w_e.shape[1]
    wr = jnp.concatenate([w_g, w_e, jnp.zeros((d, LANES - n), F32)], axis=1)
    rb = jnp.concatenate([b_g, b_e, jnp.zeros((LANES - n,), F32)]).reshape(1, LANES)
    hi = wr.astype(BF16)
    lo = (wr - hi.astype(F32)).astype(BF16)
    return hi, lo, rb


def kernel(x, c, ctx, c_ctx, ada_w, ada_b, ln_g, ln_b, mix_w_in, att_sink, hg_lb, hg_norm_g, mix_w_out, pool_w_in, pool_w_grp, pool_scale, pool_w_out, rt_group_w, rt_group_b, rt_expert_w, rt_expert_b, moe_w1, moe_w3, moe_w2):
    b, n, d = x.shape
    n_ctx = ctx.shape[1]
    t = b * n
    xf = x.reshape(t, d)
    ctxf = ctx.reshape(b * n_ctx, d)

    cond = jnp.concatenate([c, c_ctx[None, :], jnp.zeros((8 - b - 1, d), F32)], axis=0)
    mod = _ada_mod(cond, ada_w, ada_b)

    def chunk(l, j, rows=slice(0, b)):
        return mod[l, rows, j * d:(j + 1) * d][:, None, :]

    w_in = mix_w_in[0].astype(BF16)
    p = _mod_matmul(xf, chunk(0, 1), chunk(0, 0), w_in, lambda j: j, w_in.shape[1], tm=1024, tn=512)
    ctx_rows = slice(b, b + 1)
    pc = _mod_matmul(ctxf, chunk(0, 1, ctx_rows), chunk(0, 0, ctx_rows), w_in,
                     lambda j: jnp.where(j < 2, j + 2, j + 4), 4096, tm=b * n_ctx, tn=512)
    cos, sin = _rope_tables(n)
    att = _window_attention(p, pc, att_sink[0], cos, sin, b, n, n_ctx)
    o_f = _hgrn2_scan(p, pc, hg_lb[:, :, :], b, n, n_ctx, rev=False)
    o_b = _hgrn2_scan(p, pc, hg_lb[:, :, :], b, n, n_ctx, rev=True)
    wr_hi, wr_lo, rb = _router_weights(rt_group_w[0], rt_group_b[0], rt_expert_w[0], rt_expert_b[0])
    x1, tok, eid, gate = _even_out(
        att, o_f, o_b, p, hg_norm_g[0][None, :], mix_w_out[0].astype(BF16), xf,
        chunk(0, 2), chunk(0, 4), chunk(0, 3), ln_g[0, 0][None, :], ln_b[0, 0][None, :], wr_hi, wr_lo, rb, n)
    y2 = _moe_experts(tok, eid[:, :MOE_TOP_K], moe_w1[0], moe_w3[0], moe_w2[0])

    x2, u = _combine_call(x1, y2, gate, chunk(0, 5), ln_g[0, 1][None, :], ln_b[0, 1][None, :], n,
                          proj=(chunk(1, 1), chunk(1, 0), pool_w_in[0].astype(BF16)))
    wr_hi, wr_lo, rb = _router_weights(rt_group_w[1], rt_group_b[1], rt_expert_w[1], rt_expert_b[1])
    x3, tok, eid, gate = _pool_out(
        u, pool_w_grp[0].astype(BF16), pool_scale[0][None, :], pool_w_out[0].astype(BF16), x2,
        chunk(1, 2), chunk(1, 4), chunk(1, 3), ln_g[1, 0][None, :], ln_b[1, 0][None, :], wr_hi, wr_lo, rb, n)
    y2 = _moe_experts(tok, eid[:, :MOE_TOP_K], moe_w1[1], moe_w3[1], moe_w2[1])
    out = _combine_call(x3, y2, gate, chunk(1, 5), ln_g[1, 1][None, :], ln_b[1, 1][None, :], n)
    return out.reshape(b, n, d)
```

```python
import functools

import jax
import jax.numpy as jnp
from jax import lax
from jax.experimental import pallas as pl
from jax.experimental.pallas import tpu as pltpu

F32 = jnp.float32
BF16 = jnp.bfloat16

LANES = 128
VMEM_LIMIT_BYTES = 56 * 1024 * 1024
MOE_VMEM_LIMIT_BYTES = 60 * 1024 * 1024

GRID_W = 64
ATT_HEADS = 8
ATT_KV_HEADS = 4
HEAD_DIM = 128
WINDOW = 128
ATT_BLOCK = 128
ROPE_BASE = 10000.0
HG_HEADS = 8
HG_KEY = 128
HG_CHUNK = 64
HG_SUB = 16
NORM_EPS = 1e-6
POOL_WINDOWS = (2, 4, 8, 16)
POOL_HALO = 8
MOE_GROUPS = 4
MOE_EXPERTS_PER_GROUP = 8
N_EXPERTS = MOE_GROUPS * MOE_EXPERTS_PER_GROUP
MOE_TOP_K = 2
MOE_ROWS = 256
MOE_ROW_ALIGN = 8
LN_EPS = 1e-5
DEPTH = 2
DEEPNORM_ALPHA = (2 * DEPTH) ** 0.25


def _dot(a, b):
    return jnp.dot(a, b, preferred_element_type=F32)


def _dot_nt(a, b):
    return lax.dot_general(a, b, (((1,), (1,)), ((), ())), preferred_element_type=F32)


def _dot_tn(a, b):
    return lax.dot_general(a, b, (((0,), (0,)), ((), ())), preferred_element_type=F32)


def _sigmoid(x):
    return 1.0 / (1.0 + jnp.exp(-x))


def _silu(x):
    return x * _sigmoid(x)


def _params(*sem):
    return pltpu.CompilerParams(dimension_semantics=sem, vmem_limit_bytes=VMEM_LIMIT_BYTES)


def _layer_norm(z, g, b):
    mu = jnp.mean(z, axis=-1, keepdims=True)
    zc = z - mu
    var = jnp.mean(zc * zc, axis=-1, keepdims=True)
    return zc * lax.rsqrt(var + LN_EPS) * g + b


def _ada_kernel(s_ref, w_ref, b_ref, o_ref):
    s = _silu(s_ref[...]).astype(BF16)
    o_ref[0] = _dot(s, w_ref[0].astype(BF16)) + b_ref[0]


def _ada_mod(s, ada_w, ada_b, tn=1024):
    n_l, d, n = ada_w.shape
    return pl.pallas_call(
        _ada_kernel,
        grid=(n_l, n // tn),
        in_specs=[
            pl.BlockSpec((8, d), lambda l, j: (0, 0)),
            pl.BlockSpec((1, d, tn), lambda l, j: (l, 0, j)),
            pl.BlockSpec((1, 1, tn), lambda l, j: (l, 0, j)),
        ],
        out_specs=pl.BlockSpec((1, 8, tn), lambda l, j: (l, 0, j)),
        out_shape=jax.ShapeDtypeStruct((n_l, 8, n), F32),
        compiler_params=_params("parallel", "parallel"),
        name="ada_mod",
    )(s, ada_w, ada_b.reshape(n_l, 1, n))


def _modmm_kernel(x_ref, sc_ref, sh_ref, w_ref, o_ref, xs_ref):
    @pl.when(pl.program_id(1) == 0)
    def _():
        xs_ref[...] = (x_ref[...] * (1.0 + sc_ref[0]) + sh_ref[0]).astype(BF16)

    o_ref[...] = _dot(xs_ref[...], w_ref[...])


def _mod_matmul(x, sc, sh, w, col_map, n_out, tm, tn):
    m, k = x.shape
    rows_per_mod = m // sc.shape[0]
    return pl.pallas_call(
        _modmm_kernel,
        grid=(m // tm, n_out // tn),
        in_specs=[
            pl.BlockSpec((tm, k), lambda i, j: (i, 0)),
            pl.BlockSpec((1, 1, k), lambda i, j: (i * tm // rows_per_mod, 0, 0)),
            pl.BlockSpec((1, 1, k), lambda i, j: (i * tm // rows_per_mod, 0, 0)),
            pl.BlockSpec((k, tn), lambda i, j: (0, col_map(j))),
        ],
        out_specs=pl.BlockSpec((tm, tn), lambda i, j: (i, j)),
        out_shape=jax.ShapeDtypeStruct((m, n_out), F32),
        scratch_shapes=[pltpu.VMEM((tm, k), BF16)],
        compiler_params=_params("parallel", "arbitrary"),
        name="mod_matmul",
    )(x, sc, sh, w)


def _rope(t, cos, sin_signed, first_half):
    partner = jnp.where(first_half, pltpu.roll(t, 96, 1), pltpu.roll(t, 32, 1))
    return t * cos + partner * sin_signed


def _attn_kernel(sink_ref, q_ref, kp_ref, kc_ref, kn_ref, vp_ref, vc_ref, vn_ref, kx_ref, vx_ref,
                 cp_ref, cc_ref, cn_ref, sp_ref, sc_ref, sn_ref, o_ref, *, n_seq):
    kvh = pl.program_id(1)
    n = pl.program_id(2)
    blk = ATT_BLOCK
    lane = lax.broadcasted_iota(jnp.int32, (1, HEAD_DIM), 1)
    first_half = (lane % 64) < 32
    scale = HEAD_DIM ** -0.5

    cos_c, sin_c = cc_ref[...], sc_ref[...]
    qa = _rope(q_ref[:, :HEAD_DIM], cos_c, sin_c, first_half)
    qb = _rope(q_ref[:, HEAD_DIM:], cos_c, sin_c, first_half)
    q2 = (jnp.concatenate([qa, qb], axis=0) * scale).astype(BF16)
    kcat = jnp.concatenate([
        _rope(kp_ref[...], cp_ref[...], sp_ref[...], first_half),
        _rope(kc_ref[...], cos_c, sin_c, first_half),
        _rope(kn_ref[...], cn_ref[...], sn_ref[...], first_half)], axis=0).astype(BF16)
    vcat = jnp.concatenate([vp_ref[...], vc_ref[...], vn_ref[...]], axis=0).astype(BF16)

    s_loc = _dot_nt(q2, kcat)
    s_ctx = _dot_nt(q2, kx_ref[...].astype(BF16))
    row = lax.broadcasted_iota(jnp.int32, s_loc.shape, 0)
    col = lax.broadcasted_iota(jnp.int32, s_loc.shape, 1)
    q_pos = n * blk + (row % blk)
    k_pos = (n - 1) * blk + col
    valid = (jnp.abs(q_pos - k_pos) <= WINDOW) & (k_pos >= 0) & (k_pos < n_seq)
    s_loc = jnp.where(valid, s_loc, -jnp.inf)
    row1 = lax.broadcasted_iota(jnp.int32, (2 * blk, 1), 0)
    sink = jnp.where(row1 < blk, sink_ref[2 * kvh], sink_ref[2 * kvh + 1])
    m = jnp.maximum(jnp.maximum(jnp.max(s_loc, axis=-1, keepdims=True),
                                jnp.max(s_ctx, axis=-1, keepdims=True)), sink)
    p_loc = jnp.exp(s_loc - m)
    p_ctx = jnp.exp(s_ctx - m)
    den = (jnp.sum(p_loc, axis=-1, keepdims=True) + jnp.sum(p_ctx, axis=-1, keepdims=True)
           + jnp.exp(sink - m))
    o = (_dot(p_loc.astype(BF16), vcat) + _dot(p_ctx.astype(BF16), vx_ref[...].astype(BF16))) / den
    o_ref[:, :HEAD_DIM] = o[:blk].astype(o_ref.dtype)
    o_ref[:, HEAD_DIM:] = o[blk:].astype(o_ref.dtype)


def _window_attention(p, pc, sink, cos, sin_signed, batch, n_seq, n_ctx):
    nb = n_seq // ATT_BLOCK
    kcol = ATT_HEADS
    vcol = ATT_HEADS + ATT_KV_HEADS

    def rows(off):
        return lambda b, h, n, off=off: b * nb + jnp.clip(n + off, 0, nb - 1)

    def tab(off):
        return lambda b, h, n, off=off: (jnp.clip(n + off, 0, nb - 1), 0)

    kspec = [pl.BlockSpec((ATT_BLOCK, HEAD_DIM), lambda b, h, n, r=rows(o): (r(b, h, n), kcol + h))
             for o in (-1, 0, 1)]
    vspec = [pl.BlockSpec((ATT_BLOCK, HEAD_DIM), lambda b, h, n, r=rows(o): (r(b, h, n), vcol + h))
             for o in (-1, 0, 1)]
    tspec = [pl.BlockSpec((ATT_BLOCK, HEAD_DIM), tab(o)) for o in (-1, 0, 1)]
    return pl.pallas_call(
        functools.partial(_attn_kernel, n_seq=n_seq),
        grid=(batch, ATT_KV_HEADS, nb),
        in_specs=[pl.BlockSpec(memory_space=pltpu.SMEM),
                  pl.BlockSpec((ATT_BLOCK, 2 * HEAD_DIM), lambda b, h, n: (b * nb + n, h))]
        + kspec + vspec
        + [pl.BlockSpec((n_ctx, HEAD_DIM), lambda b, h, n: (b, h)),
           pl.BlockSpec((n_ctx, HEAD_DIM), lambda b, h, n: (b, ATT_KV_HEADS + h))]
        + tspec + tspec,
        out_specs=pl.BlockSpec((ATT_BLOCK, 2 * HEAD_DIM), lambda b, h, n: (b * nb + n, h)),
        out_shape=jax.ShapeDtypeStruct((batch * n_seq, ATT_HEADS * HEAD_DIM), BF16),
        compiler_params=_params("parallel", "parallel", "parallel"),
        name="window_attention",
    )(sink, p, p, p, p, p, p, p, pc, pc, cos, cos, cos, sin_signed, sin_signed, sin_signed)


def _gla_tile(zf, q_raw, v, lb, st_ref, o_ref, rev):
    c_len = zf.shape[0]
    f = lb + (1.0 - lb) * _sigmoid(zf)
    k = 1.0 - f
    g = jnp.log(f)
    ti = lax.broadcasted_iota(jnp.int32, (c_len, c_len), 0)
    si = lax.broadcasted_iota(jnp.int32, (c_len, c_len), 1)
    tri = jnp.where((si >= ti) if rev else (si <= ti), 1.0, 0.0).astype(BF16)
    g1 = g.astype(BF16)
    r1 = g - g1.astype(F32)
    g2 = r1.astype(BF16)
    g3 = (r1 - g2.astype(F32)).astype(BF16)
    c = _dot(tri, g1) + _dot(tri, g2) + _dot(tri, g3)
    c_end = c[0:1] if rev else c[c_len - 1:c_len]
    k_end = (k * jnp.exp(c_end - c)).astype(BF16)
    dec = jnp.exp(c_end)
    vb = v.astype(BF16)

    if o_ref is not None:
        q = _silu(q_raw)
        q_in = (q * jnp.exp(c)).astype(BF16)
        pairs = []
        size = c_len // 2
        while size >= HG_SUB:
            for lo in range(0, c_len, 2 * size):
                pairs.append((lo, lo + size, lo + 2 * size))
            size //= 2
        scaled = []
        for lo, mid, hi in pairs:
            if rev:
                late, early, bnd = slice(lo, mid), slice(mid, hi), mid
            else:
                late, early, bnd = slice(mid, hi), slice(lo, mid), mid - 1
            cb = c[bnd:bnd + 1]
            q_l = (q[late] * jnp.exp(c[late] - cb)).astype(BF16)
            k_e = (k[early] * jnp.exp(cb - c[early])).astype(BF16)
            scaled.append((late, early, q_l, k_e))
        n_sub = c_len // HG_SUB
        t_idx = lax.broadcasted_iota(jnp.int32, (HG_SUB, 1), 0)
        diag = [[None] * HG_HEADS for _ in range(n_sub)]
        for b in range(n_sub):
            r0 = b * HG_SUB
            qb, cb = q[r0:r0 + HG_SUB], c[r0:r0 + HG_SUB]
            for s in range(HG_SUB):
                row = r0 + s
                ok = (t_idx <= s) if rev else (t_idx >= s)
                w = qb * k[row:row + 1] * jnp.exp(jnp.where(ok, cb - c[row:row + 1], -jnp.inf))
                for h in range(HG_HEADS):
                    sl = slice(h * HG_KEY, (h + 1) * HG_KEY)
                    contrib = jnp.sum(w[:, sl], axis=-1, keepdims=True) * v[row:row + 1, sl]
                    diag[b][h] = contrib if diag[b][h] is None else diag[b][h] + contrib

    for h in range(HG_HEADS):
        sl = slice(h * HG_KEY, (h + 1) * HG_KEY)
        st = st_ref[h]
        if o_ref is not None:
            o_h = _dot_nt(q_in[:, sl], st.astype(BF16))
            parts = [diag[b][h] for b in range(n_sub)]
            for late, early, q_l, k_e in scaled:
                sc = _dot_nt(q_l[:, sl], k_e[:, sl]).astype(BF16)
                add = _dot(sc, vb[early, sl])
                b0 = late.start // HG_SUB
                for j in range((late.stop - late.start) // HG_SUB):
                    parts[b0 + j] = parts[b0 + j] + add[j * HG_SUB:(j + 1) * HG_SUB]
            o_ref[:, sl] = o_h + jnp.concatenate(parts, axis=0)
        st_ref[h] = st * dec[:, sl] + _dot_tn(vb[:, sl], k_end[:, sl])


def _gla_kernel(lb_ref, zf_ref, q_ref, v_ref, zfc_ref, vc_ref, o_ref, st_ref, *, rev, n_ctx_chunks):
    s = pl.program_id(1)

    @pl.when(s == 0)
    def _():
        st_ref[...] = jnp.zeros_like(st_ref)

    x = lb_ref[...]
    e = jnp.exp(x - jnp.max(x, axis=0, keepdims=True))
    lb = e[0:1] / jnp.sum(e, axis=0, keepdims=True)

    @pl.when(s < n_ctx_chunks)
    def _():
        _gla_tile(zfc_ref[...], None, vc_ref[...], lb, st_ref, None, rev)

    @pl.when(s >= n_ctx_chunks)
    def _():
        _gla_tile(zf_ref[...], q_ref[...], v_ref[...], lb, st_ref, o_ref, rev)


def _hgrn2_scan(p, pc, hg_lb, batch, n_seq, n_ctx, rev):
    hk = HG_HEADS * HG_KEY
    nc, ncc = n_seq // HG_CHUNK, n_ctx // HG_CHUNK
    d = 1 if rev else 0

    def lat(b, s):
        j = jnp.maximum(s - ncc, 0)
        return b * nc + (nc - 1 - j if rev else j)

    def ctx(b, s):
        j = jnp.minimum(s, ncc - 1)
        return b * ncc + (ncc - 1 - j if rev else j)

    return pl.pallas_call(
        functools.partial(_gla_kernel, rev=rev, n_ctx_chunks=ncc),
        grid=(batch, ncc + nc),
        in_specs=[
            pl.BlockSpec((None, hg_lb.shape[1], hk), lambda b, s: (d, 0, 0)),
            pl.BlockSpec((HG_CHUNK, hk), lambda b, s: (lat(b, s), 3 + d)),
            pl.BlockSpec((HG_CHUNK, hk), lambda b, s: (lat(b, s), 2)),
            pl.BlockSpec((HG_CHUNK, hk), lambda b, s: (lat(b, s), 5)),
            pl.BlockSpec((HG_CHUNK, hk), lambda b, s: (ctx(b, s), 1 + d)),
            pl.BlockSpec((HG_CHUNK, hk), lambda b, s: (ctx(b, s), 3)),
        ],
        out_specs=pl.BlockSpec((HG_CHUNK, hk), lambda b, s: (lat(b, s), 0)),
        out_shape=jax.ShapeDtypeStruct((batch * n_seq, hk), F32),
        scratch_shapes=[pltpu.VMEM((HG_HEADS, HG_KEY, HG_KEY), F32)],
        compiler_params=_params("parallel", "arbitrary"),
        name="hgrn2_bwd" if rev else "hgrn2_fwd",
    )(hg_lb, p, p, p, pc, pc)


def _route(tok, wr_hi, wr_lo, rb):
    t_hi = tok.astype(BF16)
    t_lo = (tok - t_hi.astype(F32)).astype(BF16)
    lg = _dot(t_hi, wr_hi) + _dot(t_hi, wr_lo) + _dot(t_lo, wr_hi) + rb
    lane = lax.broadcasted_iota(jnp.int32, lg.shape, 1)
    lane_f = lane.astype(F32)
    ninf = -jnp.inf
    gl = jnp.where(lane < MOE_GROUPS, lg, ninf)
    gmax = jnp.max(gl, axis=-1, keepdims=True)
    g_idx = jnp.min(jnp.where(gl == gmax, lane_f, float(LANES)), axis=-1, keepdims=True)
    g_val = 1.0 / jnp.sum(jnp.exp(gl - gmax), axis=-1, keepdims=True)
    e_lane = lane_f - float(MOE_GROUPS)
    lo = g_idx * float(MOE_EXPERTS_PER_GROUP)
    in_grp = (e_lane >= lo) & (e_lane < lo + float(MOE_EXPERTS_PER_GROUP))
    el = jnp.where(in_grp, lg, ninf)
    l1 = jnp.max(el, axis=-1, keepdims=True)
    i1 = jnp.min(jnp.where(el == l1, e_lane, float(LANES)), axis=-1, keepdims=True)
    el2 = jnp.where(e_lane == i1, ninf, el)
    l2 = jnp.max(el2, axis=-1, keepdims=True)
    i2 = jnp.min(jnp.where(el2 == l2, e_lane, float(LANES)), axis=-1, keepdims=True)
    r = jnp.exp(l2 - l1)
    w1 = g_val / (1.0 + r)
    w2 = w1 * r
    eid = jnp.where(lane == 0, i1, jnp.where(lane == 1, i2, 0.0)).astype(jnp.int32)
    gate = jnp.where(lane == 0, w1, jnp.where(lane == 1, w2, 0.0))
    return eid, gate


def _post_mix(y, x_ref, g1_ref, sc2_ref, sh2_ref, lng_ref, lnb_ref, wrh_ref, wrl_ref, rb_ref,
              x1_ref, tok_ref, eid_ref, gate_ref):
    x1 = _layer_norm(DEEPNORM_ALPHA * x_ref[...] + g1_ref[0] * y, lng_ref[...], lnb_ref[...])
    x1_ref[...] = x1
    tok = x1 * (1.0 + sc2_ref[0]) + sh2_ref[0]
    tok_ref[...] = tok
    eid, gate = _route(tok, wrh_ref[...], wrl_ref[...], rb_ref[...])
    eid_ref[...] = eid
    gate_ref[...] = gate


def _even_out_kernel(att_ref, of_ref, ob_ref, gt_ref, ng_ref, wo_ref, *rest):
    o = of_ref[...] + ob_ref[...]
    pieces = []
    for h in range(HG_HEADS):
        oh = o[:, h * HG_KEY:(h + 1) * HG_KEY]
        pieces.append(oh * lax.rsqrt(jnp.mean(oh * oh, axis=-1, keepdims=True) + NORM_EPS))
    hg = (jnp.concatenate(pieces, axis=-1) * ng_ref[...] * _silu(gt_ref[...])).astype(BF16)
    n_att = att_ref.shape[1]
    y = _dot(att_ref[...], wo_ref[:n_att, :]) + _dot(hg, wo_ref[n_att:, :])
    _post_mix(y, *rest)


def _post_specs(d, tm, rows_per_batch):
    def bmap(i):
        return (i * tm // rows_per_batch, 0, 0)

    row = pl.BlockSpec((tm, d), lambda i: (i, 0))
    mod = pl.BlockSpec((1, 1, d), bmap)
    vec = pl.BlockSpec((1, d), lambda i: (0, 0))
    rw = pl.BlockSpec((d, LANES), lambda i: (0, 0))
    in_specs = [row, mod, mod, mod, vec, vec, rw, rw, pl.BlockSpec((1, LANES), lambda i: (0, 0))]
    lane_blk = pl.BlockSpec((tm, LANES), lambda i: (i, 0))
    out_specs = [row, row, lane_blk, lane_blk]
    return in_specs, out_specs


def _post_out_shapes(t, d):
    return [jax.ShapeDtypeStruct((t, d), F32), jax.ShapeDtypeStruct((t, d), F32),
            jax.ShapeDtypeStruct((t, LANES), jnp.int32), jax.ShapeDtypeStruct((t, LANES), F32)]


def _even_out(att, o_f, o_b, p, norm_g, w_out, x, g1, sc2, sh2, lng, lnb, wr_hi, wr_lo, rb, rows_per_batch, tm=256):
    t, d = x.shape
    hv = o_f.shape[1]
    post_in, post_out = _post_specs(d, tm, rows_per_batch)
    return pl.pallas_call(
        _even_out_kernel,
        grid=(t // tm,),
        in_specs=[
            pl.BlockSpec((tm, att.shape[1]), lambda i: (i, 0)),
            pl.BlockSpec((tm, hv), lambda i: (i, 0)),
            pl.BlockSpec((tm, hv), lambda i: (i, 0)),
            pl.BlockSpec((tm, hv), lambda i: (i, 6)),
            pl.BlockSpec((1, hv), lambda i: (0, 0)),
            pl.BlockSpec(w_out.shape, lambda i: (0, 0)),
        ] + post_in,
        out_specs=post_out,
        out_shape=_post_out_shapes(t, d),
        compiler_params=_params("parallel"),
        name="even_out",
    )(att, o_f, o_b, p, norm_g, w_out, x, g1, sc2, sh2, lng, lnb, wr_hi, wr_lo, rb)


def _combine(x_ref, ya_ref, yb_ref, gate_ref, g2_ref, lng_ref, lnb_ref):
    gate = gate_ref[...]
    y = gate[:, 0:1] * ya_ref[...] + gate[:, 1:2] * yb_ref[...]
    return _layer_norm(DEEPNORM_ALPHA * x_ref[...] + g2_ref[0] * y, lng_ref[...], lnb_ref[...])


def _combine_proj_kernel(x_ref, ya_ref, yb_ref, gate_ref, g2_ref, lng_ref, lnb_ref, sc_ref, sh_ref, w_ref,
                         x2_ref, u_ref):
    x2 = _combine(x_ref, ya_ref, yb_ref, gate_ref, g2_ref, lng_ref, lnb_ref)
    x2_ref[...] = x2
    u_ref[...] = _dot((x2 * (1.0 + sc_ref[0]) + sh_ref[0]).astype(BF16), w_ref[...])


def _combine_kernel(x_ref, ya_ref, yb_ref, gate_ref, g2_ref, lng_ref, lnb_ref, x2_ref):
    x2_ref[...] = _combine(x_ref, ya_ref, yb_ref, gate_ref, g2_ref, lng_ref, lnb_ref)


def _combine_call(x1, y2, gate, g2, lng, lnb, rows_per_batch, proj=None, tm=256):
    t, d = x1.shape
    nt = t // tm

    def bmap(i):
        return (i * tm // rows_per_batch, 0, 0)

    row = pl.BlockSpec((tm, d), lambda i: (i, 0))
    mod = pl.BlockSpec((1, 1, d), bmap)
    vec = pl.BlockSpec((1, d), lambda i: (0, 0))
    in_specs = [row, row, pl.BlockSpec((tm, d), lambda i: (nt + i, 0)),
                pl.BlockSpec((tm, LANES), lambda i: (i, 0)), mod, vec, vec]
    args = [x1, y2, y2, gate, g2, lng, lnb]
    if proj is None:
        return pl.pallas_call(
            _combine_kernel, grid=(nt,), in_specs=in_specs, out_specs=row,
            out_shape=jax.ShapeDtypeStruct((t, d), F32),
            compiler_params=_params("parallel"), name="combine_ln")(*args)
    sc, sh, w = proj
    return pl.pallas_call(
        _combine_proj_kernel, grid=(nt,),
        in_specs=in_specs + [mod, mod, pl.BlockSpec(w.shape, lambda i: (0, 0))],
        out_specs=[row, pl.BlockSpec((tm, w.shape[1]), lambda i: (i, 0))],
        out_shape=[jax.ShapeDtypeStruct((t, d), F32), jax.ShapeDtypeStruct((t, w.shape[1]), F32)],
        compiler_params=_params("parallel"), name="combine_ln_proj")(*args, sc, sh, w)


def _pool_out_kernel(up_ref, uc_ref, un_ref, wg_ref, ps_ref, wo_ref, *rest, n_seq):
    tm, d = uc_ref.shape
    n_grp = len(POOL_WINDOWS)
    ch = d // n_grp
    halo = POOL_HALO
    pos0 = (pl.program_id(0) * tm) % n_seq
    e_pos = pos0 - halo + lax.broadcasted_iota(jnp.int32, (tm + 2 * halo, 1), 0)
    e_ok = (e_pos >= 0) & (e_pos < n_seq)
    t_pos = pos0 + lax.broadcasted_iota(jnp.int32, (tm, 1), 0)
    y = None
    for gi, w in enumerate(POOL_WINDOWS):
        cs = slice(gi * ch, (gi + 1) * ch)
        u = uc_ref[:, cs]
        ext = jnp.where(e_ok, jnp.concatenate([up_ref[:, cs], u, un_ref[:, cs]], axis=0), 0.0)
        a, span = ext, 1
        while span < w:
            a = a[:a.shape[0] - span] + a[span:]
            span *= 2
        start = halo - w // 2
        win = a[start:start + tm]
        cnt = (jnp.minimum(t_pos + (w - w // 2), n_seq) - jnp.maximum(t_pos - w // 2, 0)).astype(F32)
        mixed = (win / cnt - u).astype(BF16)
        z = (_dot(mixed, wg_ref[gi]) * ps_ref[:, cs]).astype(BF16)
        part = _dot(z, wo_ref[cs, :])
        y = part if y is None else y + part
    _post_mix(y, *rest)


def _pool_out(u, w_grp, scale, w_out, x, g1, sc2, sh2, lng, lnb, wr_hi, wr_lo, rb, n_seq, tm=256):
    t, d = x.shape
    hb = tm // POOL_HALO
    n_hb = t // POOL_HALO
    post_in, post_out = _post_specs(d, tm, n_seq)
    return pl.pallas_call(
        functools.partial(_pool_out_kernel, n_seq=n_seq),
        grid=(t // tm,),
        in_specs=[
            pl.BlockSpec((POOL_HALO, d), lambda i: (jnp.maximum(i * hb - 1, 0), 0)),
            pl.BlockSpec((tm, d), lambda i: (i, 0)),
            pl.BlockSpec((POOL_HALO, d), lambda i: (jnp.minimum((i + 1) * hb, n_hb - 1), 0)),
            pl.BlockSpec(w_grp.shape, lambda i: (0, 0, 0)),
            pl.BlockSpec((1, d), lambda i: (0, 0)),
            pl.BlockSpec(w_out.shape, lambda i: (0, 0)),
        ] + post_in,
        out_specs=post_out,
        out_shape=_post_out_shapes(t, d),
        compiler_params=_params("parallel"),
        name="pool_out",
    )(u, u, u, w_grp, scale, w_out, x, g1, sc2, sh2, lng, lnb, wr_hi, wr_lo, rb)


def _moe_kernel(be_ref, nv_ref, first_ref, ws_ref, nxt_ref, idx_ref, idxn_ref, tok_hbm, w1_hbm, w3_hbm, w2_hbm,
                y_hbm, xbuf, ybuf, wf1, wf3, wf2, w1b, w3b, w2b, gsem, ssem, wsem, *, n_tok, layer, n_blocks):
    i = pl.program_id(0)
    nv = pl.multiple_of(nv_ref[i], MOE_ROW_ALIGN)
    xs = i % 2

    def weight_copies(e, ws):
        return (pltpu.make_async_copy(w1_hbm.at[layer, e], wf1.at[ws], wsem.at[ws]),
                pltpu.make_async_copy(w3_hbm.at[layer, e], wf3.at[ws], wsem.at[ws]),
                pltpu.make_async_copy(w2_hbm.at[layer, e], wf2.at[ws], wsem.at[ws]))

    def gather_start(idx, n_rows, slot):
        def body(g, carry):
            for u in range(MOE_ROW_ALIGN):
                r = g * MOE_ROW_ALIGN + u
                tok = idx[0, 0, r] & (n_tok - 1)
                pltpu.make_async_copy(tok_hbm.at[pl.ds(tok, 1)], xbuf.at[slot, pl.ds(r, 1)], gsem.at[slot]).start()
            return carry

        lax.fori_loop(0, n_rows // MOE_ROW_ALIGN, body, 0)

    def scatter_copy(n_rows):
        return pltpu.make_async_copy(ybuf.at[pl.ds(0, n_rows)], y_hbm.at[pl.ds(0, n_rows)], ssem)

    @pl.when(i == 0)
    def _():
        xbuf[...] = jnp.zeros_like(xbuf)
        spare = pltpu.make_async_copy(xbuf.at[0, pl.ds(0, MOE_ROW_ALIGN)],
                                      y_hbm.at[pl.ds(MOE_TOP_K * n_tok, MOE_ROW_ALIGN)], ssem)
        spare.start()
        spare.wait()
        for cp in weight_copies(be_ref[0], 0):
            cp.start()
        gather_start(idx_ref, nv, 0)

    @pl.when(nv > 0)
    def _():
        ws = ws_ref[i]

        @pl.when(first_ref[i] == 1)
        def _():
            for cp in weight_copies(be_ref[i], ws):
                cp.wait()
            nxt = nxt_ref[i]

            @pl.when(nxt >= 0)
            def _():
                for cp in weight_copies(nxt, 1 - ws):
                    cp.start()

            w1b[...] = wf1[ws].astype(BF16)
            w3b[...] = wf3[ws].astype(BF16)
            w2b[...] = wf2[ws].astype(BF16)

        nv_next = pl.multiple_of(nv_ref[jnp.minimum(i + 1, n_blocks - 1)], MOE_ROW_ALIGN)

        @pl.when((i + 1 < n_blocks) & (nv_next > 0))
        def _():
            gather_start(idxn_ref, nv_next, 1 - xs)

        pltpu.make_async_copy(tok_hbm.at[pl.ds(0, nv)], xbuf.at[xs, pl.ds(0, nv)], gsem.at[xs]).wait()
        xb = xbuf[xs].astype(BF16)
        h = (_silu(_dot(xb, w1b[...])) * _dot(xb, w3b[...])).astype(BF16)

        @pl.when(i > 0)
        def _():
            scatter_copy(pl.multiple_of(nv_ref[jnp.maximum(i - 1, 0)], MOE_ROW_ALIGN)).wait()

        ybuf[...] = _dot(h, w2b[...])

        def scatter(g, carry):
            for u in range(MOE_ROW_ALIGN):
                r = g * MOE_ROW_ALIGN + u
                pltpu.make_async_copy(ybuf.at[pl.ds(r, 1)], y_hbm.at[pl.ds(idx_ref[0, 0, r], 1)], ssem).start()
            return carry

        lax.fori_loop(0, nv // MOE_ROW_ALIGN, scatter, 0)

        @pl.when(i == n_blocks - 1)
        def _():
            scatter_copy(nv).wait()

    @pl.when((nv == 0) & (i > 0))
    def _():
        nv_prev = pl.multiple_of(nv_ref[jnp.maximum(i - 1, 0)], MOE_ROW_ALIGN)

        @pl.when(nv_prev > 0)
        def _():
            scatter_copy(nv_prev).wait()


def _moe_dispatch(eid, n_blocks):
    n_tok = eid.shape[0]
    n_assign = n_tok * MOE_TOP_K
    flat_e = eid.reshape(n_assign)
    order = jnp.argsort(flat_e, stable=True).astype(jnp.int32)
    e_sorted = flat_e[order]
    counts = jnp.bincount(flat_e, length=N_EXPERTS).astype(jnp.int32)
    padded = (counts + MOE_ROWS - 1) // MOE_ROWS * MOE_ROWS
    start = jnp.cumsum(counts) - counts
    pad_end = jnp.cumsum(padded)
    pad_start = pad_end - padded
    dest = pad_start[e_sorted] + jnp.arange(n_assign, dtype=jnp.int32) - start[e_sorted]
    dst_row = (order % MOE_TOP_K) * n_tok + order // MOE_TOP_K
    spare = MOE_TOP_K * n_tok + jnp.arange(n_blocks * MOE_ROWS, dtype=jnp.int32) % MOE_ROW_ALIGN
    slot = spare.at[dest].set(dst_row)
    blk0 = jnp.arange(n_blocks, dtype=jnp.int32) * MOE_ROWS
    be = jnp.minimum(jnp.searchsorted(pad_end, blk0, side='right'), N_EXPERTS - 1).astype(jnp.int32)
    nv = jnp.clip(counts[be] - (blk0 - pad_start[be]), 0, MOE_ROWS).astype(jnp.int32)
    nv = jnp.where(blk0 < pad_end[-1], nv, 0)
    nv = (nv + MOE_ROW_ALIGN - 1) // MOE_ROW_ALIGN * MOE_ROW_ALIGN
    ar = jnp.arange(n_blocks, dtype=jnp.int32)
    first = ((nv > 0) & ((ar == 0) | (be != jnp.roll(be, 1)))).astype(jnp.int32)
    ws = ((jnp.cumsum(first) - 1) % 2).astype(jnp.int32)
    later_first = lax.cummin(jnp.where(first == 1, ar, n_blocks), axis=0, reverse=True)
    nxt_idx = jnp.concatenate([later_first[1:], jnp.full((1,), n_blocks, jnp.int32)])
    nxt = jnp.where(nxt_idx < n_blocks, be[jnp.minimum(nxt_idx, n_blocks - 1)], -1).astype(jnp.int32)
    return slot.reshape(n_blocks, 1, MOE_ROWS), be, nv, first, ws, nxt


def _moe_experts(tok, eid, w1, w3, w2, layer):
    n_tok, d = tok.shape
    assert n_tok & (n_tok - 1) == 0
    ff = w1.shape[3]
    n_assign = n_tok * MOE_TOP_K
    n_blocks = -(-(n_assign + N_EXPERTS * (MOE_ROWS - 1)) // MOE_ROWS)
    slot, be, nv, first, ws, nxt = _moe_dispatch(eid, n_blocks)
    grid_spec = pltpu.PrefetchScalarGridSpec(
        num_scalar_prefetch=5,
        grid=(n_blocks,),
        in_specs=[
            pl.BlockSpec((1, 1, MOE_ROWS), lambda i, *_: (i, 0, 0), memory_space=pltpu.SMEM),
            pl.BlockSpec((1, 1, MOE_ROWS), lambda i, *_: (jnp.minimum(i + 1, n_blocks - 1), 0, 0),
                         memory_space=pltpu.SMEM),
            pl.BlockSpec(memory_space=pl.ANY),
            pl.BlockSpec(memory_space=pl.ANY),
            pl.BlockSpec(memory_space=pl.ANY),
            pl.BlockSpec(memory_space=pl.ANY),
        ],
        out_specs=pl.BlockSpec(memory_space=pl.ANY),
        scratch_shapes=[
            pltpu.VMEM((2, MOE_ROWS, d), F32), pltpu.VMEM((MOE_ROWS, d), F32),
            pltpu.VMEM((2, d, ff), F32), pltpu.VMEM((2, d, ff), F32), pltpu.VMEM((2, ff, d), F32),
            pltpu.VMEM((d, ff), BF16), pltpu.VMEM((d, ff), BF16), pltpu.VMEM((ff, d), BF16),
            pltpu.SemaphoreType.DMA((2,)), pltpu.SemaphoreType.DMA(()), pltpu.SemaphoreType.DMA((2,)),
        ],
    )
    return pl.pallas_call(
        functools.partial(_moe_kernel, n_tok=n_tok, layer=layer, n_blocks=n_blocks),
        grid_spec=grid_spec,
        out_shape=jax.ShapeDtypeStruct((MOE_TOP_K * n_tok + MOE_ROW_ALIGN, d), F32),
        compiler_params=pltpu.CompilerParams(dimension_semantics=("arbitrary",),
                                             vmem_limit_bytes=MOE_VMEM_LIMIT_BYTES),
        name="moe_experts",
    )(be, nv, first, ws, nxt, slot, slot, tok, w1, w3, w2)


def _rope_tables(n_seq):
    half = HEAD_DIM // 2
    n_freq = half // 2
    t = jnp.arange(n_seq)
    row = (t // GRID_W).astype(F32)
    col = (t % GRID_W).astype(F32)
    inv_freq = ROPE_BASE ** (-jnp.arange(n_freq, dtype=F32) / n_freq)
    ang_r = row[:, None] * inv_freq[None, :]
    ang_c = col[:, None] * inv_freq[None, :]
    cos = jnp.concatenate([jnp.cos(ang_r)] * 2 + [jnp.cos(ang_c)] * 2, axis=-1)
    sin = jnp.concatenate([-jnp.sin(ang_r), jnp.sin(ang_r), -jnp.sin(ang_c), jnp.sin(ang_c)], axis=-1)
    return cos, sin


def _router_weights(w_g, b_g, w_e, b_e):
    d = w_g.shape[0]
    n = w_g.shape[1] + w_e.shape[1]
    wr = jnp.concatenate([w_g, w_e, jnp.zeros((d, LANES - n), F32)], axis=1)
    rb = jnp.concatenate([b_g, b_e, jnp.zeros((LANES - n,), F32)]).reshape(1, LANES)
    hi = wr.astype(BF16)
    lo = (wr - hi.astype(F32)).astype(BF16)
    return hi, lo, rb


def kernel(x, c, ctx, c_ctx, ada_w, ada_b, ln_g, ln_b, mix_w_in, att_sink, hg_lb, hg_norm_g, mix_w_out, pool_w_in, pool_w_grp, pool_scale, pool_w_out, rt_group_w, rt_group_b, rt_expert_w, rt_expert_b, moe_w1, moe_w3, moe_w2):
    b, n, d = x.shape
    n_ctx = ctx.shape[1]
    t = b * n
    xf = x.reshape(t, d)
    ctxf = ctx.reshape(b * n_ctx, d)

    cond = jnp.concatenate([c, c_ctx[None, :], jnp.zeros((8 - b - 1, d), F32)], axis=0)
    mod = _ada_mod(cond, ada_w, ada_b)

    def chunk(l, j, rows=slice(0, b)):
        return mod[l, rows, j * d:(j + 1) * d][:, None, :]

    w_in = mix_w_in[0].astype(BF16)
    p = _mod_matmul(xf, chunk(0, 1), chunk(0, 0), w_in, lambda j: j, w_in.shape[1], tm=1024, tn=512)
    ctx_rows = slice(b, b + 1)
    pc = _mod_matmul(ctxf, chunk(0, 1, ctx_rows), chunk(0, 0, ctx_rows), w_in,
                     lambda j: jnp.where(j < 2, j + 2, j + 4), 4096, tm=b * n_ctx, tn=512)
    cos, sin = _rope_tables(n)
    att = _window_attention(p, pc, att_sink[0], cos, sin, b, n, n_ctx)
    o_f = _hgrn2_scan(p, pc, hg_lb[:, :, :], b, n, n_ctx, rev=False)
    o_b = _hgrn2_scan(p, pc, hg_lb[:, :, :], b, n, n_ctx, rev=True)
    wr_hi, wr_lo, rb = _router_weights(rt_group_w[0], rt_group_b[0], rt_expert_w[0], rt_expert_b[0])
    x1, tok, eid, gate = _even_out(
        att, o_f, o_b, p, hg_norm_g[0][None, :], mix_w_out[0].astype(BF16), xf,
        chunk(0, 2), chunk(0, 4), chunk(0, 3), ln_g[0, 0][None, :], ln_b[0, 0][None, :], wr_hi, wr_lo, rb, n)
    y2 = _moe_experts(tok, eid[:, :MOE_TOP_K], moe_w1, moe_w3, moe_w2, 0)

    x2, u = _combine_call(x1, y2, gate, chunk(0, 5), ln_g[0, 1][None, :], ln_b[0, 1][None, :], n,
                          proj=(chunk(1, 1), chunk(1, 0), pool_w_in[0].astype(BF16)))
    wr_hi, wr_lo, rb = _router_weights(rt_group_w[1], rt_group_b[1], rt_expert_w[1], rt_expert_b[1])
    x3, tok, eid, gate = _pool_out(
        u, pool_w_grp[0].astype(BF16), pool_scale[0][None, :], pool_w_out[0].astype(BF16), x2,
        chunk(1, 2), chunk(1, 4), chunk(1, 3), ln_g[1, 0][None, :], ln_b[1, 0][None, :], wr_hi, wr_lo, rb, n)
    y2 = _moe_experts(tok, eid[:, :MOE_TOP_K], moe_w1, moe_w3, moe_w2, 1)
    out = _combine_call(x3, y2, gate, chunk(1, 5), ln_g[1, 1][None, :], ln_b[1, 1][None, :], n)
    return out.reshape(b, n, d)
```

```python
import functools

import jax
import jax.numpy as jnp
from jax import lax
from jax.experimental import pallas as pl
from jax.experimental.pallas import tpu as pltpu

F32 = jnp.float32
BF16 = jnp.bfloat16

LANES = 128
VMEM_LIMIT_BYTES = 56 * 1024 * 1024
MOE_VMEM_LIMIT_BYTES = 60 * 1024 * 1024

GRID_W = 64
ATT_HEADS = 8
ATT_KV_HEADS = 4
ATT_GROUP = ATT_HEADS // ATT_KV_HEADS
HEAD_DIM = 128
WINDOW = 128
ATT_BLOCK = 128
ROPE_BASE = 10000.0
HG_HEADS = 8
HG_KEY = 128
HG_CHUNK = 64
HG_SUB = 16
NORM_EPS = 1e-6
POOL_WINDOWS = (2, 4, 8, 16)
POOL_HALO = 8
MOE_GROUPS = 4
MOE_EXPERTS_PER_GROUP = 8
N_EXPERTS = MOE_GROUPS * MOE_EXPERTS_PER_GROUP
MOE_TOP_K = 2
MOE_ROWS = 256
MOE_ROW_ALIGN = 8
LN_EPS = 1e-5
DEPTH = 2
DEEPNORM_ALPHA = (2 * DEPTH) ** 0.25


def _dot(a, b):
    return jnp.dot(a, b, preferred_element_type=F32)


def _dot_nt(a, b):
    return lax.dot_general(a, b, (((1,), (1,)), ((), ())), preferred_element_type=F32)


def _dot_tn(a, b):
    return lax.dot_general(a, b, (((0,), (0,)), ((), ())), preferred_element_type=F32)


def _sigmoid(x):
    return 1.0 / (1.0 + jnp.exp(-x))


def _silu(x):
    return x * _sigmoid(x)


def _params(*sem):
    return pltpu.CompilerParams(dimension_semantics=sem, vmem_limit_bytes=VMEM_LIMIT_BYTES)


def _layer_norm(z, g, b):
    mu = jnp.mean(z, axis=-1, keepdims=True)
    zc = z - mu
    var = jnp.mean(zc * zc, axis=-1, keepdims=True)
    return zc * lax.rsqrt(var + LN_EPS) * g + b


def _ada_kernel(s_ref, w_ref, b_ref, o_ref):
    s = _silu(s_ref[...]).astype(BF16)
    o_ref[0] = _dot(s, w_ref[0].astype(BF16)) + b_ref[0]


def _ada_mod(s, ada_w, ada_b, tn=1024):
    n_l, d, n = ada_w.shape
    return pl.pallas_call(
        _ada_kernel,
        grid=(n_l, n // tn),
        in_specs=[
            pl.BlockSpec((8, d), lambda l, j: (0, 0)),
            pl.BlockSpec((1, d, tn), lambda l, j: (l, 0, j)),
            pl.BlockSpec((1, 1, tn), lambda l, j: (l, 0, j)),
        ],
        out_specs=pl.BlockSpec((1, 8, tn), lambda l, j: (l, 0, j)),
        out_shape=jax.ShapeDtypeStruct((n_l, 8, n), F32),
        compiler_params=_params("parallel", "parallel"),
        name="ada_mod",
    )(s, ada_w, ada_b.reshape(n_l, 1, n))


def _rope(t, cos, sin_signed, first_half):
    partner = jnp.where(first_half, pltpu.roll(t, 96, 1), pltpu.roll(t, 32, 1))
    return t * cos + partner * sin_signed


def _modmm_kernel(x_ref, sc_ref, sh_ref, w_ref, cos_ref, sin_ref, oa_ref, ob_ref, xs_ref, *,
                  n_q_tiles, n_rope_tiles, n_a_tiles):
    j = pl.program_id(1)

    @pl.when(j == 0)
    def _():
        xs_ref[...] = (x_ref[...] * (1.0 + sc_ref[0]) + sh_ref[0]).astype(BF16)

    acc = _dot(xs_ref[...], w_ref[...])

    if n_rope_tiles:
        @pl.when(j < n_rope_tiles)
        def _():
            lane = lax.broadcasted_iota(jnp.int32, (1, HEAD_DIM), 1)
            first_half = (lane % 64) < 32
            scale = jnp.where(j < n_q_tiles, HEAD_DIM ** -0.5, 1.0)
            cos, sin = cos_ref[...] * scale, sin_ref[...] * scale
            for h in range(acc.shape[1] // HEAD_DIM):
                sl = slice(h * HEAD_DIM, (h + 1) * HEAD_DIM)
                oa_ref[:, sl] = _rope(acc[:, sl], cos, sin, first_half).astype(BF16)

    @pl.when((j >= n_rope_tiles) & (j < n_a_tiles))
    def _():
        oa_ref[...] = acc.astype(BF16)

    @pl.when(j >= n_a_tiles)
    def _():
        ob_ref[...] = acc


def _mod_matmul(x, sc, sh, w, cos, sin, col_map, n_a, n_b, n_q, n_rope, n_seq, tm, tn):
    m, k = x.shape
    rows_per_mod = m // sc.shape[0]
    ta, tb = n_a // tn, n_b // tn
    tab = pl.BlockSpec((tm, HEAD_DIM), lambda i, j: ((i * tm % n_seq) // tm, 0))
    return pl.pallas_call(
        functools.partial(_modmm_kernel, n_q_tiles=n_q // tn, n_rope_tiles=n_rope // tn, n_a_tiles=ta),
        grid=(m // tm, ta + tb),
        in_specs=[
            pl.BlockSpec((tm, k), lambda i, j: (i, 0)),
            pl.BlockSpec((1, 1, k), lambda i, j: (i * tm // rows_per_mod, 0, 0)),
            pl.BlockSpec((1, 1, k), lambda i, j: (i * tm // rows_per_mod, 0, 0)),
            pl.BlockSpec((k, tn), lambda i, j: (0, col_map(j))),
            tab, tab,
        ],
        out_specs=[pl.BlockSpec((tm, tn), lambda i, j: (i, jnp.minimum(j, ta - 1))),
                   pl.BlockSpec((tm, tn), lambda i, j: (i, jnp.maximum(j - ta, 0)))],
        out_shape=[jax.ShapeDtypeStruct((m, n_a), BF16), jax.ShapeDtypeStruct((m, n_b), F32)],
        scratch_shapes=[pltpu.VMEM((tm, k), BF16)],
        compiler_params=_params("parallel", "arbitrary"),
        name="mod_matmul",
    )(x, sc, sh, w, cos, sin)


def _attn_kernel(sink_ref, q_ref, kp_ref, kc_ref, kn_ref, vp_ref, vc_ref, vn_ref, kx_ref, vx_ref,
                 mp_ref, mn_ref, o_ref, *, n_blocks):
    n = pl.program_id(1)
    blk = ATT_BLOCK
    row1 = lax.broadcasted_iota(jnp.int32, (ATT_GROUP * blk, 1), 0)
    has_prev, has_next = n > 0, n < n_blocks - 1
    for h in range(ATT_KV_HEADS):
        kv = slice(h * HEAD_DIM, (h + 1) * HEAD_DIM)
        q2 = jnp.concatenate([q_ref[:, (ATT_GROUP * h + g) * HEAD_DIM:(ATT_GROUP * h + g + 1) * HEAD_DIM]
                              for g in range(ATT_GROUP)], axis=0)
        s_p = jnp.where(has_prev, _dot_nt(q2, kp_ref[:, kv]) + mp_ref[...], -jnp.inf)
        s_c = _dot_nt(q2, kc_ref[:, kv])
        s_n = jnp.where(has_next, _dot_nt(q2, kn_ref[:, kv]) + mn_ref[...], -jnp.inf)
        s_x = _dot_nt(q2, kx_ref[:, kv])
        sink = jnp.where(row1 < blk, sink_ref[ATT_GROUP * h], sink_ref[ATT_GROUP * h + 1])
        m = jnp.maximum(jnp.maximum(jnp.max(s_p, axis=-1, keepdims=True), jnp.max(s_c, axis=-1, keepdims=True)),
                        jnp.maximum(jnp.max(s_n, axis=-1, keepdims=True), jnp.max(s_x, axis=-1, keepdims=True)))
        m = jnp.maximum(m, sink)
        p_p, p_c, p_n, p_x = (jnp.exp(s - m) for s in (s_p, s_c, s_n, s_x))
        den = (jnp.sum(p_p, axis=-1, keepdims=True) + jnp.sum(p_c, axis=-1, keepdims=True)
               + jnp.sum(p_n, axis=-1, keepdims=True) + jnp.sum(p_x, axis=-1, keepdims=True)
               + jnp.exp(sink - m))
        o = (_dot(p_p.astype(BF16), vp_ref[:, kv]) + _dot(p_c.astype(BF16), vc_ref[:, kv])
             + _dot(p_n.astype(BF16), vn_ref[:, kv]) + _dot(p_x.astype(BF16), vx_ref[:, kv])) / den
        for g in range(ATT_GROUP):
            col = (ATT_GROUP * h + g) * HEAD_DIM
            o_ref[:, col:col + HEAD_DIM] = o[g * blk:(g + 1) * blk].astype(o_ref.dtype)


def _window_attention(qkv, kv_ctx, sink, batch, n_seq, n_ctx):
    assert ATT_GROUP == 2 and WINDOW == ATT_BLOCK
    nb = n_seq // ATT_BLOCK
    qw, kw = ATT_HEADS * HEAD_DIM, ATT_KV_HEADS * HEAD_DIM
    kcol, vcol = qw // kw, qw // kw + 1

    def rows(off):
        return lambda b, n, off=off: b * nb + jnp.clip(n + off, 0, nb - 1)

    kspec = [pl.BlockSpec((ATT_BLOCK, kw), lambda b, n, r=rows(o): (r(b, n), kcol)) for o in (-1, 0, 1)]
    vspec = [pl.BlockSpec((ATT_BLOCK, kw), lambda b, n, r=rows(o): (r(b, n), vcol)) for o in (-1, 0, 1)]
    r = jnp.arange(ATT_GROUP * ATT_BLOCK)[:, None] % ATT_BLOCK
    c = jnp.arange(ATT_BLOCK)[None, :]
    mask_prev = jnp.where(c >= r, 0.0, -jnp.inf).astype(F32)
    mask_next = jnp.where(c <= r, 0.0, -jnp.inf).astype(F32)
    mspec = pl.BlockSpec(mask_prev.shape, lambda b, n: (0, 0))
    return pl.pallas_call(
        functools.partial(_attn_kernel, n_blocks=nb),
        grid=(batch, nb),
        in_specs=[pl.BlockSpec(memory_space=pltpu.SMEM),
                  pl.BlockSpec((ATT_BLOCK, qw), lambda b, n: (b * nb + n, 0))]
        + kspec + vspec
        + [pl.BlockSpec((n_ctx, kw), lambda b, n: (b, 0)), pl.BlockSpec((n_ctx, kw), lambda b, n: (b, 1)),
           mspec, mspec],
        out_specs=pl.BlockSpec((ATT_BLOCK, qw), lambda b, n: (b * nb + n, 0)),
        out_shape=jax.ShapeDtypeStruct((batch * n_seq, qw), BF16),
        compiler_params=_params("parallel", "parallel"),
        name="window_attention",
    )(sink, qkv, qkv, qkv, qkv, qkv, qkv, qkv, kv_ctx, kv_ctx, mask_prev, mask_next)


def _gla_tile(zf, q_raw, v, lb, st_ref, o_ref, rev):
    c_len = zf.shape[0]
    f = lb + (1.0 - lb) * _sigmoid(zf)
    k = 1.0 - f
    g = jnp.log(f)
    ti = lax.broadcasted_iota(jnp.int32, (c_len, c_len), 0)
    si = lax.broadcasted_iota(jnp.int32, (c_len, c_len), 1)
    tri = jnp.where((si >= ti) if rev else (si <= ti), 1.0, 0.0).astype(BF16)
    g1 = g.astype(BF16)
    r1 = g - g1.astype(F32)
    g2 = r1.astype(BF16)
    g3 = (r1 - g2.astype(F32)).astype(BF16)
    c = _dot(tri, g1) + _dot(tri, g2) + _dot(tri, g3)
    c_end = c[0:1] if rev else c[c_len - 1:c_len]
    k_end = (k * jnp.exp(c_end - c)).astype(BF16)
    dec = jnp.exp(c_end)
    vb = v.astype(BF16)

    if o_ref is not None:
        q = _silu(q_raw)
        q_in = (q * jnp.exp(c)).astype(BF16)
        pairs = []
        size = c_len // 2
        while size >= HG_SUB:
            for lo in range(0, c_len, 2 * size):
                pairs.append((lo, lo + size, lo + 2 * size))
            size //= 2
        scaled = []
        for lo, mid, hi in pairs:
            if rev:
                late, early, bnd = slice(lo, mid), slice(mid, hi), mid
            else:
                late, early, bnd = slice(mid, hi), slice(lo, mid), mid - 1
            cb = c[bnd:bnd + 1]
            q_l = (q[late] * jnp.exp(c[late] - cb)).astype(BF16)
            k_e = (k[early] * jnp.exp(cb - c[early])).astype(BF16)
            scaled.append((late, early, q_l, k_e))
        n_sub = c_len // HG_SUB
        t_idx = lax.broadcasted_iota(jnp.int32, (HG_SUB, 1), 0)
        diag = [[None] * HG_HEADS for _ in range(n_sub)]
        for b in range(n_sub):
            r0 = b * HG_SUB
            qb, cb = q[r0:r0 + HG_SUB], c[r0:r0 + HG_SUB]
            for s in range(HG_SUB):
                row = r0 + s
                ok = (t_idx <= s) if rev else (t_idx >= s)
                w = qb * k[row:row + 1] * jnp.exp(jnp.where(ok, cb - c[row:row + 1], -jnp.inf))
                for h in range(HG_HEADS):
                    sl = slice(h * HG_KEY, (h + 1) * HG_KEY)
                    contrib = jnp.sum(w[:, sl], axis=-1, keepdims=True) * v[row:row + 1, sl]
                    diag[b][h] = contrib if diag[b][h] is None else diag[b][h] + contrib

    for h in range(HG_HEADS):
        sl = slice(h * HG_KEY, (h + 1) * HG_KEY)
        st = st_ref[h]
        if o_ref is not None:
            o_h = _dot_nt(q_in[:, sl], st.astype(BF16))
            parts = [diag[b][h] for b in range(n_sub)]
            for late, early, q_l, k_e in scaled:
                sc = _dot_nt(q_l[:, sl], k_e[:, sl]).astype(BF16)
                add = _dot(sc, vb[early, sl])
                b0 = late.start // HG_SUB
                for j in range((late.stop - late.start) // HG_SUB):
                    parts[b0 + j] = parts[b0 + j] + add[j * HG_SUB:(j + 1) * HG_SUB]
            o_ref[:, sl] = o_h + jnp.concatenate(parts, axis=0)
        st_ref[h] = st * dec[:, sl] + _dot_tn(vb[:, sl], k_end[:, sl])


def _gla_kernel(lb_ref, zf_ref, q_ref, v_ref, zfc_ref, vc_ref, o_ref, st_ref, *, rev, n_ctx_chunks):
    s = pl.program_id(1)

    @pl.when(s == 0)
    def _():
        st_ref[...] = jnp.zeros_like(st_ref)

    x = lb_ref[...]
    e = jnp.exp(x - jnp.max(x, axis=0, keepdims=True))
    lb = e[0:1] / jnp.sum(e, axis=0, keepdims=True)

    @pl.when(s < n_ctx_chunks)
    def _():
        _gla_tile(zfc_ref[...], None, vc_ref[...], lb, st_ref, None, rev)

    @pl.when(s >= n_ctx_chunks)
    def _():
        _gla_tile(zf_ref[...], q_ref[...], v_ref[...], lb, st_ref, o_ref, rev)


def _hgrn2_scan(p, pc, hg_lb, batch, n_seq, n_ctx, rev):
    hk = HG_HEADS * HG_KEY
    nc, ncc = n_seq // HG_CHUNK, n_ctx // HG_CHUNK
    d = 1 if rev else 0

    def lat(b, s):
        j = jnp.maximum(s - ncc, 0)
        return b * nc + (nc - 1 - j if rev else j)

    def ctx(b, s):
        j = jnp.minimum(s, ncc - 1)
        return b * ncc + (ncc - 1 - j if rev else j)

    return pl.pallas_call(
        functools.partial(_gla_kernel, rev=rev, n_ctx_chunks=ncc),
        grid=(batch, ncc + nc),
        in_specs=[
            pl.BlockSpec((None, hg_lb.shape[1], hk), lambda b, s: (d, 0, 0)),
            pl.BlockSpec((HG_CHUNK, hk), lambda b, s: (lat(b, s), 1 + d)),
            pl.BlockSpec((HG_CHUNK, hk), lambda b, s: (lat(b, s), 0)),
            pl.BlockSpec((HG_CHUNK, hk), lambda b, s: (lat(b, s), 3)),
            pl.BlockSpec((HG_CHUNK, hk), lambda b, s: (ctx(b, s), d)),
            pl.BlockSpec((HG_CHUNK, hk), lambda b, s: (ctx(b, s), 2)),
        ],
        out_specs=pl.BlockSpec((HG_CHUNK, hk), lambda b, s: (lat(b, s), 0)),
        out_shape=jax.ShapeDtypeStruct((batch * n_seq, hk), F32),
        scratch_shapes=[pltpu.VMEM((HG_HEADS, HG_KEY, HG_KEY), F32)],
        compiler_params=_params("parallel", "arbitrary"),
        name="hgrn2_bwd" if rev else "hgrn2_fwd",
    )(hg_lb, p, p, p, pc, pc)


def _route(tok, wr_hi, wr_lo, rb):
    t_hi = tok.astype(BF16)
    t_lo = (tok - t_hi.astype(F32)).astype(BF16)
    lg = _dot(t_hi, wr_hi) + _dot(t_hi, wr_lo) + _dot(t_lo, wr_hi) + rb
    lane = lax.broadcasted_iota(jnp.int32, lg.shape, 1)
    lane_f = lane.astype(F32)
    ninf = -jnp.inf
    gl = jnp.where(lane < MOE_GROUPS, lg, ninf)
    gmax = jnp.max(gl, axis=-1, keepdims=True)
    g_idx = jnp.min(jnp.where(gl == gmax, lane_f, float(LANES)), axis=-1, keepdims=True)
    g_val = 1.0 / jnp.sum(jnp.exp(gl - gmax), axis=-1, keepdims=True)
    e_lane = lane_f - float(MOE_GROUPS)
    lo = g_idx * float(MOE_EXPERTS_PER_GROUP)
    in_grp = (e_lane >= lo) & (e_lane < lo + float(MOE_EXPERTS_PER_GROUP))
    el = jnp.where(in_grp, lg, ninf)
    l1 = jnp.max(el, axis=-1, keepdims=True)
    i1 = jnp.min(jnp.where(el == l1, e_lane, float(LANES)), axis=-1, keepdims=True)
    el2 = jnp.where(e_lane == i1, ninf, el)
    l2 = jnp.max(el2, axis=-1, keepdims=True)
    i2 = jnp.min(jnp.where(el2 == l2, e_lane, float(LANES)), axis=-1, keepdims=True)
    r = jnp.exp(l2 - l1)
    w1 = g_val / (1.0 + r)
    w2 = w1 * r
    eid = jnp.where(lane == 0, i1, jnp.where(lane == 1, i2, 0.0)).astype(jnp.int32)
    gate = jnp.where(lane == 0, w1, jnp.where(lane == 1, w2, 0.0))
    return eid, gate


def _post_mix(y, x_ref, g1_ref, sc2_ref, sh2_ref, lng_ref, lnb_ref, wrh_ref, wrl_ref, rb_ref,
              x1_ref, tok_ref, eid_ref, gate_ref):
    x1 = _layer_norm(DEEPNORM_ALPHA * x_ref[...] + g1_ref[0] * y, lng_ref[...], lnb_ref[...])
    x1_ref[...] = x1
    tok = x1 * (1.0 + sc2_ref[0]) + sh2_ref[0]
    tok_ref[...] = tok
    eid, gate = _route(tok, wrh_ref[...], wrl_ref[...], rb_ref[...])
    eid_ref[...] = eid
    gate_ref[...] = gate


def _even_out_kernel(att_ref, of_ref, ob_ref, gt_ref, ng_ref, wo_ref, *rest):
    o = of_ref[...] + ob_ref[...]
    pieces = []
    for h in range(HG_HEADS):
        oh = o[:, h * HG_KEY:(h + 1) * HG_KEY]
        pieces.append(oh * lax.rsqrt(jnp.mean(oh * oh, axis=-1, keepdims=True) + NORM_EPS))
    hg = (jnp.concatenate(pieces, axis=-1) * ng_ref[...] * _silu(gt_ref[...])).astype(BF16)
    n_att = att_ref.shape[1]
    y = _dot(att_ref[...], wo_ref[:n_att, :]) + _dot(hg, wo_ref[n_att:, :])
    _post_mix(y, *rest)


def _post_specs(d, tm, rows_per_batch):
    def bmap(i):
        return (i * tm // rows_per_batch, 0, 0)

    row = pl.BlockSpec((tm, d), lambda i: (i, 0))
    mod = pl.BlockSpec((1, 1, d), bmap)
    vec = pl.BlockSpec((1, d), lambda i: (0, 0))
    rw = pl.BlockSpec((d, LANES), lambda i: (0, 0))
    in_specs = [row, mod, mod, mod, vec, vec, rw, rw, pl.BlockSpec((1, LANES), lambda i: (0, 0))]
    lane_blk = pl.BlockSpec((tm, LANES), lambda i: (i, 0))
    out_specs = [row, row, lane_blk, lane_blk]
    return in_specs, out_specs


def _post_out_shapes(t, d):
    return [jax.ShapeDtypeStruct((t, d), F32), jax.ShapeDtypeStruct((t, d), F32),
            jax.ShapeDtypeStruct((t, LANES), jnp.int32), jax.ShapeDtypeStruct((t, LANES), F32)]


def _even_out(att, o_f, o_b, p, norm_g, w_out, x, g1, sc2, sh2, lng, lnb, wr_hi, wr_lo, rb, rows_per_batch, tm=256):
    t, d = x.shape
    hv = o_f.shape[1]
    post_in, post_out = _post_specs(d, tm, rows_per_batch)
    return pl.pallas_call(
        _even_out_kernel,
        grid=(t // tm,),
        in_specs=[
            pl.BlockSpec((tm, att.shape[1]), lambda i: (i, 0)),
            pl.BlockSpec((tm, hv), lambda i: (i, 0)),
            pl.BlockSpec((tm, hv), lambda i: (i, 0)),
            pl.BlockSpec((tm, hv), lambda i: (i, 4)),
            pl.BlockSpec((1, hv), lambda i: (0, 0)),
            pl.BlockSpec(w_out.shape, lambda i: (0, 0)),
        ] + post_in,
        out_specs=post_out,
        out_shape=_post_out_shapes(t, d),
        compiler_params=_params("parallel"),
        name="even_out",
    )(att, o_f, o_b, p, norm_g, w_out, x, g1, sc2, sh2, lng, lnb, wr_hi, wr_lo, rb)


def _combine(x_ref, ya_ref, yb_ref, gate_ref, g2_ref, lng_ref, lnb_ref):
    gate = gate_ref[...]
    y = gate[:, 0:1] * ya_ref[...] + gate[:, 1:2] * yb_ref[...]
    return _layer_norm(DEEPNORM_ALPHA * x_ref[...] + g2_ref[0] * y, lng_ref[...], lnb_ref[...])


def _combine_proj_kernel(x_ref, ya_ref, yb_ref, gate_ref, g2_ref, lng_ref, lnb_ref, sc_ref, sh_ref, w_ref,
                         x2_ref, u_ref):
    x2 = _combine(x_ref, ya_ref, yb_ref, gate_ref, g2_ref, lng_ref, lnb_ref)
    x2_ref[...] = x2
    u_ref[...] = _dot((x2 * (1.0 + sc_ref[0]) + sh_ref[0]).astype(BF16), w_ref[...])


def _combine_kernel(x_ref, ya_ref, yb_ref, gate_ref, g2_ref, lng_ref, lnb_ref, x2_ref):
    x2_ref[...] = _combine(x_ref, ya_ref, yb_ref, gate_ref, g2_ref, lng_ref, lnb_ref)


def _combine_call(x1, y2, gate, g2, lng, lnb, rows_per_batch, proj=None, tm=256):
    t, d = x1.shape
    nt = t // tm

    def bmap(i):
        return (i * tm // rows_per_batch, 0, 0)

    row = pl.BlockSpec((tm, d), lambda i: (i, 0))
    mod = pl.BlockSpec((1, 1, d), bmap)
    vec = pl.BlockSpec((1, d), lambda i: (0, 0))
    in_specs = [row, row, pl.BlockSpec((tm, d), lambda i: (nt + i, 0)),
                pl.BlockSpec((tm, LANES), lambda i: (i, 0)), mod, vec, vec]
    args = [x1, y2, y2, gate, g2, lng, lnb]
    if proj is None:
        return pl.pallas_call(
            _combine_kernel, grid=(nt,), in_specs=in_specs, out_specs=row,
            out_shape=jax.ShapeDtypeStruct((t, d), F32),
            compiler_params=_params("parallel"), name="combine_ln")(*args)
    sc, sh, w = proj
    return pl.pallas_call(
        _combine_proj_kernel, grid=(nt,),
        in_specs=in_specs + [mod, mod, pl.BlockSpec(w.shape, lambda i: (0, 0))],
        out_specs=[row, pl.BlockSpec((tm, w.shape[1]), lambda i: (i, 0))],
        out_shape=[jax.ShapeDtypeStruct((t, d), F32), jax.ShapeDtypeStruct((t, w.shape[1]), F32)],
        compiler_params=_params("parallel"), name="combine_ln_proj")(*args, sc, sh, w)


def _pool_out_kernel(up_ref, uc_ref, un_ref, wg_ref, ps_ref, wo_ref, *rest, n_seq):
    tm, d = uc_ref.shape
    n_grp = len(POOL_WINDOWS)
    ch = d // n_grp
    halo = POOL_HALO
    pos0 = (pl.program_id(0) * tm) % n_seq
    e_pos = pos0 - halo + lax.broadcasted_iota(jnp.int32, (tm + 2 * halo, 1), 0)
    e_ok = (e_pos >= 0) & (e_pos < n_seq)
    t_pos = pos0 + lax.broadcasted_iota(jnp.int32, (tm, 1), 0)
    y = None
    for gi, w in enumerate(POOL_WINDOWS):
        cs = slice(gi * ch, (gi + 1) * ch)
        u = uc_ref[:, cs]
        ext = jnp.where(e_ok, jnp.concatenate([up_ref[:, cs], u, un_ref[:, cs]], axis=0), 0.0)
        a, span = ext, 1
        while span < w:
            a = a[:a.shape[0] - span] + a[span:]
            span *= 2
        start = halo - w // 2
        win = a[start:start + tm]
        cnt = (jnp.minimum(t_pos + (w - w // 2), n_seq) - jnp.maximum(t_pos - w // 2, 0)).astype(F32)
        mixed = (win / cnt - u).astype(BF16)
        z = (_dot(mixed, wg_ref[gi]) * ps_ref[:, cs]).astype(BF16)
        part = _dot(z, wo_ref[cs, :])
        y = part if y is None else y + part
    _post_mix(y, *rest)


def _pool_out(u, w_grp, scale, w_out, x, g1, sc2, sh2, lng, lnb, wr_hi, wr_lo, rb, n_seq, tm=256):
    t, d = x.shape
    hb = tm // POOL_HALO
    n_hb = t // POOL_HALO
    post_in, post_out = _post_specs(d, tm, n_seq)
    return pl.pallas_call(
        functools.partial(_pool_out_kernel, n_seq=n_seq),
        grid=(t // tm,),
        in_specs=[
            pl.BlockSpec((POOL_HALO, d), lambda i: (jnp.maximum(i * hb - 1, 0), 0)),
            pl.BlockSpec((tm, d), lambda i: (i, 0)),
            pl.BlockSpec((POOL_HALO, d), lambda i: (jnp.minimum((i + 1) * hb, n_hb - 1), 0)),
            pl.BlockSpec(w_grp.shape, lambda i: (0, 0, 0)),
            pl.BlockSpec((1, d), lambda i: (0, 0)),
            pl.BlockSpec(w_out.shape, lambda i: (0, 0)),
        ] + post_in,
        out_specs=post_out,
        out_shape=_post_out_shapes(t, d),
        compiler_params=_params("parallel"),
        name="pool_out",
    )(u, u, u, w_grp, scale, w_out, x, g1, sc2, sh2, lng, lnb, wr_hi, wr_lo, rb)


def _moe_kernel(be_ref, nv_ref, first_ref, ws_ref, nxt_ref, idx_ref, idxn_ref, tok_hbm, w1_hbm, w3_hbm, w2_hbm,
                y_hbm, xbuf, ybuf, wf1, wf3, wf2, w1b, w3b, w2b, gsem, ssem, wsem, *, n_tok, layer, n_blocks):
    i = pl.program_id(0)
    nv = pl.multiple_of(nv_ref[i], MOE_ROW_ALIGN)
    xs = i % 2

    def weight_copies(e, ws):
        return (pltpu.make_async_copy(w1_hbm.at[layer, e], wf1.at[ws], wsem.at[ws]),
                pltpu.make_async_copy(w3_hbm.at[layer, e], wf3.at[ws], wsem.at[ws]),
                pltpu.make_async_copy(w2_hbm.at[layer, e], wf2.at[ws], wsem.at[ws]))

    def gather_start(idx, n_rows, slot):
        def body(g, carry):
            for u in range(MOE_ROW_ALIGN):
                r = g * MOE_ROW_ALIGN + u
                tok = idx[0, 0, r] & (n_tok - 1)
                pltpu.make_async_copy(tok_hbm.at[pl.ds(tok, 1)], xbuf.at[slot, pl.ds(r, 1)], gsem.at[slot]).start()
            return carry

        lax.fori_loop(0, n_rows // MOE_ROW_ALIGN, body, 0)

    def scatter_copy(n_rows):
        return pltpu.make_async_copy(ybuf.at[pl.ds(0, n_rows)], y_hbm.at[pl.ds(0, n_rows)], ssem)

    @pl.when(i == 0)
    def _():
        xbuf[...] = jnp.zeros_like(xbuf)
        spare = pltpu.make_async_copy(xbuf.at[0, pl.ds(0, MOE_ROW_ALIGN)],
                                      y_hbm.at[pl.ds(MOE_TOP_K * n_tok, MOE_ROW_ALIGN)], ssem)
        spare.start()
        spare.wait()
        for cp in weight_copies(be_ref[0], 0):
            cp.start()
        gather_start(idx_ref, nv, 0)

    @pl.when(nv > 0)
    def _():
        ws = ws_ref[i]

        @pl.when(first_ref[i] == 1)
        def _():
            for cp in weight_copies(be_ref[i], ws):
                cp.wait()
            nxt = nxt_ref[i]

            @pl.when(nxt >= 0)
            def _():
                for cp in weight_copies(nxt, 1 - ws):
                    cp.start()

            w1b[...] = wf1[ws].astype(BF16)
            w3b[...] = wf3[ws].astype(BF16)
            w2b[...] = wf2[ws].astype(BF16)

        nv_next = pl.multiple_of(nv_ref[jnp.minimum(i + 1, n_blocks - 1)], MOE_ROW_ALIGN)

        @pl.when((i + 1 < n_blocks) & (nv_next > 0))
        def _():
            gather_start(idxn_ref, nv_next, 1 - xs)

        pltpu.make_async_copy(tok_hbm.at[pl.ds(0, nv)], xbuf.at[xs, pl.ds(0, nv)], gsem.at[xs]).wait()
        xb = xbuf[xs].astype(BF16)
        h = (_silu(_dot(xb, w1b[...])) * _dot(xb, w3b[...])).astype(BF16)

        @pl.when(i > 0)
        def _():
            scatter_copy(pl.multiple_of(nv_ref[jnp.maximum(i - 1, 0)], MOE_ROW_ALIGN)).wait()

        ybuf[...] = _dot(h, w2b[...])

        def scatter(g, carry):
            for u in range(MOE_ROW_ALIGN):
                r = g * MOE_ROW_ALIGN + u
                pltpu.make_async_copy(ybuf.at[pl.ds(r, 1)], y_hbm.at[pl.ds(idx_ref[0, 0, r], 1)], ssem).start()
            return carry

        lax.fori_loop(0, nv // MOE_ROW_ALIGN, scatter, 0)

        @pl.when(i == n_blocks - 1)
        def _():
            scatter_copy(nv).wait()

    @pl.when((nv == 0) & (i > 0))
    def _():
        nv_prev = pl.multiple_of(nv_ref[jnp.maximum(i - 1, 0)], MOE_ROW_ALIGN)

        @pl.when(nv_prev > 0)
        def _():
            scatter_copy(nv_prev).wait()


def _moe_dispatch(eid, n_blocks):
    n_tok = eid.shape[0]
    n_assign = n_tok * MOE_TOP_K
    flat_e = eid.reshape(n_assign)
    order = jnp.argsort(flat_e, stable=True).astype(jnp.int32)
    e_sorted = flat_e[order]
    counts = jnp.bincount(flat_e, length=N_EXPERTS).astype(jnp.int32)
    padded = (counts + MOE_ROWS - 1) // MOE_ROWS * MOE_ROWS
    start = jnp.cumsum(counts) - counts
    pad_end = jnp.cumsum(padded)
    pad_start = pad_end - padded
    dest = pad_start[e_sorted] + jnp.arange(n_assign, dtype=jnp.int32) - start[e_sorted]
    dst_row = (order % MOE_TOP_K) * n_tok + order // MOE_TOP_K
    spare = MOE_TOP_K * n_tok + jnp.arange(n_blocks * MOE_ROWS, dtype=jnp.int32) % MOE_ROW_ALIGN
    slot = spare.at[dest].set(dst_row)
    blk0 = jnp.arange(n_blocks, dtype=jnp.int32) * MOE_ROWS
    be = jnp.minimum(jnp.searchsorted(pad_end, blk0, side='right'), N_EXPERTS - 1).astype(jnp.int32)
    nv = jnp.clip(counts[be] - (blk0 - pad_start[be]), 0, MOE_ROWS).astype(jnp.int32)
    nv = jnp.where(blk0 < pad_end[-1], nv, 0)
    nv = (nv + MOE_ROW_ALIGN - 1) // MOE_ROW_ALIGN * MOE_ROW_ALIGN
    ar = jnp.arange(n_blocks, dtype=jnp.int32)
    first = ((nv > 0) & ((ar == 0) | (be != jnp.roll(be, 1)))).astype(jnp.int32)
    ws = ((jnp.cumsum(first) - 1) % 2).astype(jnp.int32)
    later_first = lax.cummin(jnp.where(first == 1, ar, n_blocks), axis=0, reverse=True)
    nxt_idx = jnp.concatenate([later_first[1:], jnp.full((1,), n_blocks, jnp.int32)])
    nxt = jnp.where(nxt_idx < n_blocks, be[jnp.minimum(nxt_idx, n_blocks - 1)], -1).astype(jnp.int32)
    return slot.reshape(n_blocks, 1, MOE_ROWS), be, nv, first, ws, nxt


def _moe_experts(tok, eid, w1, w3, w2, layer):
    n_tok, d = tok.shape
    assert n_tok & (n_tok - 1) == 0
    ff = w1.shape[3]
    n_assign = n_tok * MOE_TOP_K
    n_blocks = -(-(n_assign + N_EXPERTS * (MOE_ROWS - 1)) // MOE_ROWS)
    slot, be, nv, first, ws, nxt = _moe_dispatch(eid, n_blocks)
    grid_spec = pltpu.PrefetchScalarGridSpec(
        num_scalar_prefetch=5,
        grid=(n_blocks,),
        in_specs=[
            pl.BlockSpec((1, 1, MOE_ROWS), lambda i, *_: (i, 0, 0), memory_space=pltpu.SMEM),
            pl.BlockSpec((1, 1, MOE_ROWS), lambda i, *_: (jnp.minimum(i + 1, n_blocks - 1), 0, 0),
                         memory_space=pltpu.SMEM),
            pl.BlockSpec(memory_space=pl.ANY),
            pl.BlockSpec(memory_space=pl.ANY),
            pl.BlockSpec(memory_space=pl.ANY),
            pl.BlockSpec(memory_space=pl.ANY),
        ],
        out_specs=pl.BlockSpec(memory_space=pl.ANY),
        scratch_shapes=[
            pltpu.VMEM((2, MOE_ROWS, d), F32), pltpu.VMEM((MOE_ROWS, d), F32),
            pltpu.VMEM((2, d, ff), F32), pltpu.VMEM((2, d, ff), F32), pltpu.VMEM((2, ff, d), F32),
            pltpu.VMEM((d, ff), BF16), pltpu.VMEM((d, ff), BF16), pltpu.VMEM((ff, d), BF16),
            pltpu.SemaphoreType.DMA((2,)), pltpu.SemaphoreType.DMA(()), pltpu.SemaphoreType.DMA((2,)),
        ],
    )
    return pl.pallas_call(
        functools.partial(_moe_kernel, n_tok=n_tok, layer=layer, n_blocks=n_blocks),
        grid_spec=grid_spec,
        out_shape=jax.ShapeDtypeStruct((MOE_TOP_K * n_tok + MOE_ROW_ALIGN, d), F32),
        compiler_params=pltpu.CompilerParams(dimension_semantics=("arbitrary",),
                                             vmem_limit_bytes=MOE_VMEM_LIMIT_BYTES),
        name="moe_experts",
    )(be, nv, first, ws, nxt, slot, slot, tok, w1, w3, w2)


def _rope_tables(n_seq):
    half = HEAD_DIM // 2
    n_freq = half // 2
    t = jnp.arange(n_seq)
    row = (t // GRID_W).astype(F32)
    col = (t % GRID_W).astype(F32)
    inv_freq = ROPE_BASE ** (-jnp.arange(n_freq, dtype=F32) / n_freq)
    ang_r = row[:, None] * inv_freq[None, :]
    ang_c = col[:, None] * inv_freq[None, :]
    cos = jnp.concatenate([jnp.cos(ang_r)] * 2 + [jnp.cos(ang_c)] * 2, axis=-1)
    sin = jnp.concatenate([-jnp.sin(ang_r), jnp.sin(ang_r), -jnp.sin(ang_c), jnp.sin(ang_c)], axis=-1)
    return cos, sin


def _router_weights(w_g, b_g, w_e, b_e):
    d = w_g.shape[0]
    n = w_g.shape[1] + w_e.shape[1]
    wr = jnp.concatenate([w_g, w_e, jnp.zeros((d, LANES - n), F32)], axis=1)
    rb = jnp.concatenate([b_g, b_e, jnp.zeros((LANES - n,), F32)]).reshape(1, LANES)
    hi = wr.astype(BF16)
    lo = (wr - hi.astype(F32)).astype(BF16)
    return hi, lo, rb


def kernel(x, c, ctx, c_ctx, ada_w, ada_b, ln_g, ln_b, mix_w_in, att_sink, hg_lb, hg_norm_g, mix_w_out, pool_w_in, pool_w_grp, pool_scale, pool_w_out, rt_group_w, rt_group_b, rt_expert_w, rt_expert_b, moe_w1, moe_w3, moe_w2):
    b, n, d = x.shape
    n_ctx = ctx.shape[1]
    t = b * n
    xf = x.reshape(t, d)
    ctxf = ctx.reshape(b * n_ctx, d)

    cond = jnp.concatenate([c, c_ctx[None, :], jnp.zeros((8 - b - 1, d), F32)], axis=0)
    mod = _ada_mod(cond, ada_w, ada_b)

    def chunk(l, j, rows=slice(0, b)):
        return mod[l, rows, j * d:(j + 1) * d][:, None, :]

    w_in = mix_w_in[0].astype(BF16)
    cos, sin = _rope_tables(n)
    q_w, kv_w = ATT_HEADS * HEAD_DIM, ATT_KV_HEADS * HEAD_DIM
    n_att = q_w + 2 * kv_w
    qkv, p = _mod_matmul(xf, chunk(0, 1), chunk(0, 0), w_in, cos, sin, lambda j: j, n_att, w_in.shape[1] - n_att,
                         n_q=q_w, n_rope=q_w + kv_w, n_seq=n, tm=1024, tn=512)
    ctx_rows = slice(b, b + 1)
    kv_ctx, pc = _mod_matmul(ctxf, chunk(0, 1, ctx_rows), chunk(0, 0, ctx_rows), w_in, cos, sin,
                             lambda j: jnp.where(j < 2, j + 2, j + 4), 2 * kv_w, 3 * HG_HEADS * HG_KEY,
                             n_q=0, n_rope=0, n_seq=n, tm=b * n_ctx, tn=512)
    att = _window_attention(qkv, kv_ctx, att_sink[0], b, n, n_ctx)
    o_f = _hgrn2_scan(p, pc, hg_lb[:, :, :], b, n, n_ctx, rev=False)
    o_b = _hgrn2_scan(p, pc, hg_lb[:, :, :], b, n, n_ctx, rev=True)
    wr_hi, wr_lo, rb = _router_weights(rt_group_w[0], rt_group_b[0], rt_expert_w[0], rt_expert_b[0])
    x1, tok, eid, gate = _even_out(
        att, o_f, o_b, p, hg_norm_g[0][None, :], mix_w_out[0].astype(BF16), xf,
        chunk(0, 2), chunk(0, 4), chunk(0, 3), ln_g[0, 0][None, :], ln_b[0, 0][None, :], wr_hi, wr_lo, rb, n)
    y2 = _moe_experts(tok, eid[:, :MOE_TOP_K], moe_w1, moe_w3, moe_w2, 0)

    x2, u = _combine_call(x1, y2, gate, chunk(0, 5), ln_g[0, 1][None, :], ln_b[0, 1][None, :], n,
                          proj=(chunk(1, 1), chunk(1, 0), pool_w_in[0].astype(BF16)))
    wr_hi, wr_lo, rb = _router_weights(rt_group_w[1], rt_group_b[1], rt_expert_w[1], rt_expert_b[1])
    x3, tok, eid, gate = _pool_out(
        u, pool_w_grp[0].astype(BF16), pool_scale[0][None, :], pool_w_out[0].astype(BF16), x2,
        chunk(1, 2), chunk(1, 4), chunk(1, 3), ln_g[1, 0][None, :], ln_b[1, 0][None, :], wr_hi, wr_lo, rb, n)
    y2 = _moe_experts(tok, eid[:, :MOE_TOP_K], moe_w1, moe_w3, moe_w2, 1)
    out = _combine_call(x3, y2, gate, chunk(1, 5), ln_g[1, 1][None, :], ln_b[1, 1][None, :], n)
    return out.reshape(b, n, d)
```

```python
import functools

import jax
import jax.numpy as jnp
from jax import lax
from jax.experimental import pallas as pl
from jax.experimental.pallas import tpu as pltpu

F32 = jnp.float32
BF16 = jnp.bfloat16

LANES = 128
VMEM_LIMIT_BYTES = 56 * 1024 * 1024
MOE_VMEM_LIMIT_BYTES = 60 * 1024 * 1024

GRID_W = 64
ATT_HEADS = 8
ATT_KV_HEADS = 4
ATT_GROUP = ATT_HEADS // ATT_KV_HEADS
HEAD_DIM = 128
WINDOW = 128
ATT_BLOCK = 128
ROPE_BASE = 10000.0
HG_HEADS = 8
HG_KEY = 128
HG_CHUNK = 64
HG_SUB = 16
HG_FAST_RANGE = 80.0
NORM_EPS = 1e-6
POOL_WINDOWS = (2, 4, 8, 16)
POOL_HALO = 8
MOE_GROUPS = 4
MOE_EXPERTS_PER_GROUP = 8
N_EXPERTS = MOE_GROUPS * MOE_EXPERTS_PER_GROUP
MOE_TOP_K = 2
MOE_ROWS = 256
MOE_ROW_ALIGN = 8
LN_EPS = 1e-5
DEPTH = 2
DEEPNORM_ALPHA = (2 * DEPTH) ** 0.25


def _dot(a, b):
    return jnp.dot(a, b, preferred_element_type=F32)


def _dot_nt(a, b):
    return lax.dot_general(a, b, (((1,), (1,)), ((), ())), preferred_element_type=F32)


def _dot_tn(a, b):
    return lax.dot_general(a, b, (((0,), (0,)), ((), ())), preferred_element_type=F32)


def _sigmoid(x):
    return 1.0 / (1.0 + jnp.exp(-x))


def _silu(x):
    return x * _sigmoid(x)


def _params(*sem):
    return pltpu.CompilerParams(dimension_semantics=sem, vmem_limit_bytes=VMEM_LIMIT_BYTES)


def _layer_norm(z, g, b):
    mu = jnp.mean(z, axis=-1, keepdims=True)
    zc = z - mu
    var = jnp.mean(zc * zc, axis=-1, keepdims=True)
    return zc * lax.rsqrt(var + LN_EPS) * g + b


def _ada_kernel(s_ref, w_ref, b_ref, o_ref):
    s = _silu(s_ref[...]).astype(BF16)
    o_ref[0] = _dot(s, w_ref[0].astype(BF16)) + b_ref[0]


def _ada_mod(s, ada_w, ada_b, tn=1024):
    n_l, d, n = ada_w.shape
    return pl.pallas_call(
        _ada_kernel,
        grid=(n_l, n // tn),
        in_specs=[
            pl.BlockSpec((8, d), lambda l, j: (0, 0)),
            pl.BlockSpec((1, d, tn), lambda l, j: (l, 0, j)),
            pl.BlockSpec((1, 1, tn), lambda l, j: (l, 0, j)),
        ],
        out_specs=pl.BlockSpec((1, 8, tn), lambda l, j: (l, 0, j)),
        out_shape=jax.ShapeDtypeStruct((n_l, 8, n), F32),
        compiler_params=_params("parallel", "parallel"),
        name="ada_mod",
    )(s, ada_w, ada_b.reshape(n_l, 1, n))


def _rope(t, cos, sin_signed, first_half):
    partner = jnp.where(first_half, pltpu.roll(t, 96, 1), pltpu.roll(t, 32, 1))
    return t * cos + partner * sin_signed


def _modmm_kernel(x_ref, sc_ref, sh_ref, w_ref, cos_ref, sin_ref, oa_ref, ob_ref, xs_ref, *,
                  n_q_tiles, n_rope_tiles, n_a_tiles):
    j = pl.program_id(1)

    @pl.when(j == 0)
    def _():
        xs_ref[...] = (x_ref[...] * (1.0 + sc_ref[0]) + sh_ref[0]).astype(BF16)

    acc = _dot(xs_ref[...], w_ref[...])

    if n_rope_tiles:
        @pl.when(j < n_rope_tiles)
        def _():
            lane = lax.broadcasted_iota(jnp.int32, (1, HEAD_DIM), 1)
            first_half = (lane % 64) < 32
            scale = jnp.where(j < n_q_tiles, HEAD_DIM ** -0.5, 1.0)
            cos, sin = cos_ref[...] * scale, sin_ref[...] * scale
            for h in range(acc.shape[1] // HEAD_DIM):
                sl = slice(h * HEAD_DIM, (h + 1) * HEAD_DIM)
                oa_ref[:, sl] = _rope(acc[:, sl], cos, sin, first_half).astype(BF16)

    @pl.when((j >= n_rope_tiles) & (j < n_a_tiles))
    def _():
        oa_ref[...] = acc.astype(BF16)

    @pl.when(j >= n_a_tiles)
    def _():
        ob_ref[...] = acc


def _mod_matmul(x, sc, sh, w, cos, sin, col_map, n_a, n_b, n_q, n_rope, n_seq, tm, tn):
    m, k = x.shape
    rows_per_mod = m // sc.shape[0]
    ta, tb = n_a // tn, n_b // tn
    tab = pl.BlockSpec((tm, HEAD_DIM), lambda i, j: ((i * tm % n_seq) // tm, 0))
    return pl.pallas_call(
        functools.partial(_modmm_kernel, n_q_tiles=n_q // tn, n_rope_tiles=n_rope // tn, n_a_tiles=ta),
        grid=(m // tm, ta + tb),
        in_specs=[
            pl.BlockSpec((tm, k), lambda i, j: (i, 0)),
            pl.BlockSpec((1, 1, k), lambda i, j: (i * tm // rows_per_mod, 0, 0)),
            pl.BlockSpec((1, 1, k), lambda i, j: (i * tm // rows_per_mod, 0, 0)),
            pl.BlockSpec((k, tn), lambda i, j: (0, col_map(j))),
            tab, tab,
        ],
        out_specs=[pl.BlockSpec((tm, tn), lambda i, j: (i, jnp.minimum(j, ta - 1))),
                   pl.BlockSpec((tm, tn), lambda i, j: (i, jnp.maximum(j - ta, 0)))],
        out_shape=[jax.ShapeDtypeStruct((m, n_a), BF16), jax.ShapeDtypeStruct((m, n_b), F32)],
        scratch_shapes=[pltpu.VMEM((tm, k), BF16)],
        compiler_params=_params("parallel", "arbitrary"),
        name="mod_matmul",
    )(x, sc, sh, w, cos, sin)


def _attn_kernel(sink_ref, q_ref, kp_ref, kc_ref, kn_ref, vp_ref, vc_ref, vn_ref, kx_ref, vx_ref,
                 mp_ref, mn_ref, o_ref, *, n_blocks):
    n = pl.program_id(1)
    blk = ATT_BLOCK
    row1 = lax.broadcasted_iota(jnp.int32, (ATT_GROUP * blk, 1), 0)
    has_prev, has_next = n > 0, n < n_blocks - 1
    for h in range(ATT_KV_HEADS):
        kv = slice(h * HEAD_DIM, (h + 1) * HEAD_DIM)
        q2 = jnp.concatenate([q_ref[:, (ATT_GROUP * h + g) * HEAD_DIM:(ATT_GROUP * h + g + 1) * HEAD_DIM]
                              for g in range(ATT_GROUP)], axis=0)
        s_p = jnp.where(has_prev, _dot_nt(q2, kp_ref[:, kv]) + mp_ref[...], -jnp.inf)
        s_c = _dot_nt(q2, kc_ref[:, kv])
        s_n = jnp.where(has_next, _dot_nt(q2, kn_ref[:, kv]) + mn_ref[...], -jnp.inf)
        s_x = _dot_nt(q2, kx_ref[:, kv])
        sink = jnp.where(row1 < blk, sink_ref[ATT_GROUP * h], sink_ref[ATT_GROUP * h + 1])
        m = jnp.maximum(jnp.maximum(jnp.max(s_p, axis=-1, keepdims=True), jnp.max(s_c, axis=-1, keepdims=True)),
                        jnp.maximum(jnp.max(s_n, axis=-1, keepdims=True), jnp.max(s_x, axis=-1, keepdims=True)))
        m = jnp.maximum(m, sink)
        p_p, p_c, p_n, p_x = (jnp.exp(s - m) for s in (s_p, s_c, s_n, s_x))
        den = (jnp.sum(p_p, axis=-1, keepdims=True) + jnp.sum(p_c, axis=-1, keepdims=True)
               + jnp.sum(p_n, axis=-1, keepdims=True) + jnp.sum(p_x, axis=-1, keepdims=True)
               + jnp.exp(sink - m))
        o = (_dot(p_p.astype(BF16), vp_ref[:, kv]) + _dot(p_c.astype(BF16), vc_ref[:, kv])
             + _dot(p_n.astype(BF16), vn_ref[:, kv]) + _dot(p_x.astype(BF16), vx_ref[:, kv])) / den
        for g in range(ATT_GROUP):
            col = (ATT_GROUP * h + g) * HEAD_DIM
            o_ref[:, col:col + HEAD_DIM] = o[g * blk:(g + 1) * blk].astype(o_ref.dtype)


def _window_attention(qkv, kv_ctx, sink, batch, n_seq, n_ctx):
    assert ATT_GROUP == 2 and WINDOW == ATT_BLOCK
    nb = n_seq // ATT_BLOCK
    qw, kw = ATT_HEADS * HEAD_DIM, ATT_KV_HEADS * HEAD_DIM
    kcol, vcol = qw // kw, qw // kw + 1

    def rows(off):
        return lambda b, n, off=off: b * nb + jnp.clip(n + off, 0, nb - 1)

    kspec = [pl.BlockSpec((ATT_BLOCK, kw), lambda b, n, r=rows(o): (r(b, n), kcol)) for o in (-1, 0, 1)]
    vspec = [pl.BlockSpec((ATT_BLOCK, kw), lambda b, n, r=rows(o): (r(b, n), vcol)) for o in (-1, 0, 1)]
    r = jnp.arange(ATT_GROUP * ATT_BLOCK)[:, None] % ATT_BLOCK
    c = jnp.arange(ATT_BLOCK)[None, :]
    mask_prev = jnp.where(c >= r, 0.0, -jnp.inf).astype(F32)
    mask_next = jnp.where(c <= r, 0.0, -jnp.inf).astype(F32)
    mspec = pl.BlockSpec(mask_prev.shape, lambda b, n: (0, 0))
    return pl.pallas_call(
        functools.partial(_attn_kernel, n_blocks=nb),
        grid=(batch, nb),
        in_specs=[pl.BlockSpec(memory_space=pltpu.SMEM),
                  pl.BlockSpec((ATT_BLOCK, qw), lambda b, n: (b * nb + n, 0))]
        + kspec + vspec
        + [pl.BlockSpec((n_ctx, kw), lambda b, n: (b, 0)), pl.BlockSpec((n_ctx, kw), lambda b, n: (b, 1)),
           mspec, mspec],
        out_specs=pl.BlockSpec((ATT_BLOCK, qw), lambda b, n: (b * nb + n, 0)),
        out_shape=jax.ShapeDtypeStruct((batch * n_seq, qw), BF16),
        compiler_params=_params("parallel", "parallel"),
        name="window_attention",
    )(sink, qkv, qkv, qkv, qkv, qkv, qkv, qkv, kv_ctx, kv_ctx, mask_prev, mask_next)


def _gla_tile(zf, q_raw, v, lb, st_ref, o_ref, rev):
    c_len = zf.shape[0]
    f = lb + (1.0 - lb) * _sigmoid(zf)
    k = 1.0 - f
    g = jnp.log(f)
    ti = lax.broadcasted_iota(jnp.int32, (c_len, c_len), 0)
    si = lax.broadcasted_iota(jnp.int32, (c_len, c_len), 1)
    seen = (si >= ti) if rev else (si <= ti)
    tri = jnp.where(seen, 1.0, 0.0).astype(BF16)
    g1 = g.astype(BF16)
    r1 = g - g1.astype(F32)
    g2 = r1.astype(BF16)
    g3 = (r1 - g2.astype(F32)).astype(BF16)
    c = _dot(tri, g1) + _dot(tri, g2) + _dot(tri, g3)
    c_end = c[0:1] if rev else c[c_len - 1:c_len]
    k_end = (k * jnp.exp(c_end - c)).astype(BF16)
    dec = jnp.exp(c_end)
    vb = v.astype(BF16)

    if o_ref is not None:
        q = _silu(q_raw)
        q_in = (q * jnp.exp(c)).astype(BF16)
        in_range = jnp.min(c_end) >= -HG_FAST_RANGE

        @pl.when(in_range)
        def _():
            k_in = (k * jnp.exp(-c)).astype(BF16)
            for h in range(HG_HEADS):
                sl = slice(h * HG_KEY, (h + 1) * HG_KEY)
                sc = jnp.where(seen, _dot_nt(q_in[:, sl], k_in[:, sl]), 0.0).astype(BF16)
                o_ref[:, sl] = _dot_nt(q_in[:, sl], st_ref[h].astype(BF16)) + _dot(sc, vb[:, sl])

        @pl.when(jnp.logical_not(in_range))
        def _():
            _gla_intra_exact(q, k, v, vb, c, q_in, st_ref, o_ref, rev)

    for h in range(HG_HEADS):
        sl = slice(h * HG_KEY, (h + 1) * HG_KEY)
        st_ref[h] = st_ref[h] * dec[:, sl] + _dot_tn(vb[:, sl], k_end[:, sl])


def _gla_intra_exact(q, k, v, vb, c, q_in, st_ref, o_ref, rev):
    c_len = q.shape[0]
    pairs = []
    size = c_len // 2
    while size >= HG_SUB:
        for lo in range(0, c_len, 2 * size):
            pairs.append((lo, lo + size, lo + 2 * size))
        size //= 2
    scaled = []
    for lo, mid, hi in pairs:
        if rev:
            late, early, bnd = slice(lo, mid), slice(mid, hi), mid
        else:
            late, early, bnd = slice(mid, hi), slice(lo, mid), mid - 1
        cb = c[bnd:bnd + 1]
        q_l = (q[late] * jnp.exp(c[late] - cb)).astype(BF16)
        k_e = (k[early] * jnp.exp(cb - c[early])).astype(BF16)
        scaled.append((late, early, q_l, k_e))
    n_sub = c_len // HG_SUB
    t_idx = lax.broadcasted_iota(jnp.int32, (HG_SUB, 1), 0)
    diag = [[None] * HG_HEADS for _ in range(n_sub)]
    for b in range(n_sub):
        r0 = b * HG_SUB
        qb, cb = q[r0:r0 + HG_SUB], c[r0:r0 + HG_SUB]
        for s in range(HG_SUB):
            row = r0 + s
            ok = (t_idx <= s) if rev else (t_idx >= s)
            w = qb * k[row:row + 1] * jnp.exp(jnp.where(ok, cb - c[row:row + 1], -jnp.inf))
            for h in range(HG_HEADS):
                sl = slice(h * HG_KEY, (h + 1) * HG_KEY)
                contrib = jnp.sum(w[:, sl], axis=-1, keepdims=True) * v[row:row + 1, sl]
                diag[b][h] = contrib if diag[b][h] is None else diag[b][h] + contrib

    for h in range(HG_HEADS):
        sl = slice(h * HG_KEY, (h + 1) * HG_KEY)
        o_h = _dot_nt(q_in[:, sl], st_ref[h].astype(BF16))
        parts = [diag[b][h] for b in range(n_sub)]
        for late, early, q_l, k_e in scaled:
            sc = _dot_nt(q_l[:, sl], k_e[:, sl]).astype(BF16)
            add = _dot(sc, vb[early, sl])
            b0 = late.start // HG_SUB
            for j in range((late.stop - late.start) // HG_SUB):
                parts[b0 + j] = parts[b0 + j] + add[j * HG_SUB:(j + 1) * HG_SUB]
        o_ref[:, sl] = o_h + jnp.concatenate(parts, axis=0)


def _gla_kernel(lb_ref, zf_ref, q_ref, v_ref, zfc_ref, vc_ref, o_ref, st_ref, *, rev, n_ctx_chunks):
    s = pl.program_id(1)

    @pl.when(s == 0)
    def _():
        st_ref[...] = jnp.zeros_like(st_ref)

    x = lb_ref[...]
    e = jnp.exp(x - jnp.max(x, axis=0, keepdims=True))
    lb = e[0:1] / jnp.sum(e, axis=0, keepdims=True)

    @pl.when(s < n_ctx_chunks)
    def _():
        _gla_tile(zfc_ref[...], None, vc_ref[...], lb, st_ref, None, rev)

    @pl.when(s >= n_ctx_chunks)
    def _():
        _gla_tile(zf_ref[...], q_ref[...], v_ref[...], lb, st_ref, o_ref, rev)


def _hgrn2_scan(p, pc, hg_lb, batch, n_seq, n_ctx, rev):
    hk = HG_HEADS * HG_KEY
    nc, ncc = n_seq // HG_CHUNK, n_ctx // HG_CHUNK
    d = 1 if rev else 0

    def lat(b, s):
        j = jnp.maximum(s - ncc, 0)
        return b * nc + (nc - 1 - j if rev else j)

    def ctx(b, s):
        j = jnp.minimum(s, ncc - 1)
        return b * ncc + (ncc - 1 - j if rev else j)

    return pl.pallas_call(
        functools.partial(_gla_kernel, rev=rev, n_ctx_chunks=ncc),
        grid=(batch, ncc + nc),
        in_specs=[
            pl.BlockSpec((None, hg_lb.shape[1], hk), lambda b, s: (d, 0, 0)),
            pl.BlockSpec((HG_CHUNK, hk), lambda b, s: (lat(b, s), 1 + d)),
            pl.BlockSpec((HG_CHUNK, hk), lambda b, s: (lat(b, s), 0)),
            pl.BlockSpec((HG_CHUNK, hk), lambda b, s: (lat(b, s), 3)),
            pl.BlockSpec((HG_CHUNK, hk), lambda b, s: (ctx(b, s), d)),
            pl.BlockSpec((HG_CHUNK, hk), lambda b, s: (ctx(b, s), 2)),
        ],
        out_specs=pl.BlockSpec((HG_CHUNK, hk), lambda b, s: (lat(b, s), 0)),
        out_shape=jax.ShapeDtypeStruct((batch * n_seq, hk), F32),
        scratch_shapes=[pltpu.VMEM((HG_HEADS, HG_KEY, HG_KEY), F32)],
        compiler_params=_params("parallel", "arbitrary"),
        name="hgrn2_bwd" if rev else "hgrn2_fwd",
    )(hg_lb, p, p, p, pc, pc)


def _route(tok, wr_hi, wr_lo, rb):
    t_hi = tok.astype(BF16)
    t_lo = (tok - t_hi.astype(F32)).astype(BF16)
    lg = _dot(t_hi, wr_hi) + _dot(t_hi, wr_lo) + _dot(t_lo, wr_hi) + rb
    lane = lax.broadcasted_iota(jnp.int32, lg.shape, 1)
    lane_f = lane.astype(F32)
    ninf = -jnp.inf
    gl = jnp.where(lane < MOE_GROUPS, lg, ninf)
    gmax = jnp.max(gl, axis=-1, keepdims=True)
    g_idx = jnp.min(jnp.where(gl == gmax, lane_f, float(LANES)), axis=-1, keepdims=True)
    g_val = 1.0 / jnp.sum(jnp.exp(gl - gmax), axis=-1, keepdims=True)
    e_lane = lane_f - float(MOE_GROUPS)
    lo = g_idx * float(MOE_EXPERTS_PER_GROUP)
    in_grp = (e_lane >= lo) & (e_lane < lo + float(MOE_EXPERTS_PER_GROUP))
    el = jnp.where(in_grp, lg, ninf)
    l1 = jnp.max(el, axis=-1, keepdims=True)
    i1 = jnp.min(jnp.where(el == l1, e_lane, float(LANES)), axis=-1, keepdims=True)
    el2 = jnp.where(e_lane == i1, ninf, el)
    l2 = jnp.max(el2, axis=-1, keepdims=True)
    i2 = jnp.min(jnp.where(el2 == l2, e_lane, float(LANES)), axis=-1, keepdims=True)
    r = jnp.exp(l2 - l1)
    w1 = g_val / (1.0 + r)
    w2 = w1 * r
    eid = jnp.where(lane == 0, i1, jnp.where(lane == 1, i2, 0.0)).astype(jnp.int32)
    gate = jnp.where(lane == 0, w1, jnp.where(lane == 1, w2, 0.0))
    return eid, gate


def _post_mix(y, x_ref, g1_ref, sc2_ref, sh2_ref, lng_ref, lnb_ref, wrh_ref, wrl_ref, rb_ref,
              x1_ref, tok_ref, eid_ref, gate_ref):
    x1 = _layer_norm(DEEPNORM_ALPHA * x_ref[...] + g1_ref[0] * y, lng_ref[...], lnb_ref[...])
    x1_ref[...] = x1
    tok = x1 * (1.0 + sc2_ref[0]) + sh2_ref[0]
    tok_ref[...] = tok
    eid, gate = _route(tok, wrh_ref[...], wrl_ref[...], rb_ref[...])
    eid_ref[...] = eid
    gate_ref[...] = gate


def _even_out_kernel(att_ref, of_ref, ob_ref, gt_ref, ng_ref, wo_ref, *rest):
    o = of_ref[...] + ob_ref[...]
    pieces = []
    for h in range(HG_HEADS):
        oh = o[:, h * HG_KEY:(h + 1) * HG_KEY]
        pieces.append(oh * lax.rsqrt(jnp.mean(oh * oh, axis=-1, keepdims=True) + NORM_EPS))
    hg = (jnp.concatenate(pieces, axis=-1) * ng_ref[...] * _silu(gt_ref[...])).astype(BF16)
    n_att = att_ref.shape[1]
    y = _dot(att_ref[...], wo_ref[:n_att, :]) + _dot(hg, wo_ref[n_att:, :])
    _post_mix(y, *rest)


def _post_specs(d, tm, rows_per_batch):
    def bmap(i):
        return (i * tm // rows_per_batch, 0, 0)

    row = pl.BlockSpec((tm, d), lambda i: (i, 0))
    mod = pl.BlockSpec((1, 1, d), bmap)
    vec = pl.BlockSpec((1, d), lambda i: (0, 0))
    rw = pl.BlockSpec((d, LANES), lambda i: (0, 0))
    in_specs = [row, mod, mod, mod, vec, vec, rw, rw, pl.BlockSpec((1, LANES), lambda i: (0, 0))]
    lane_blk = pl.BlockSpec((tm, LANES), lambda i: (i, 0))
    out_specs = [row, row, lane_blk, lane_blk]
    return in_specs, out_specs


def _post_out_shapes(t, d):
    return [jax.ShapeDtypeStruct((t, d), F32), jax.ShapeDtypeStruct((t, d), F32),
            jax.ShapeDtypeStruct((t, LANES), jnp.int32), jax.ShapeDtypeStruct((t, LANES), F32)]


def _even_out(att, o_f, o_b, p, norm_g, w_out, x, g1, sc2, sh2, lng, lnb, wr_hi, wr_lo, rb, rows_per_batch, tm=256):
    t, d = x.shape
    hv = o_f.shape[1]
    post_in, post_out = _post_specs(d, tm, rows_per_batch)
    return pl.pallas_call(
        _even_out_kernel,
        grid=(t // tm,),
        in_specs=[
            pl.BlockSpec((tm, att.shape[1]), lambda i: (i, 0)),
            pl.BlockSpec((tm, hv), lambda i: (i, 0)),
            pl.BlockSpec((tm, hv), lambda i: (i, 0)),
            pl.BlockSpec((tm, hv), lambda i: (i, 4)),
            pl.BlockSpec((1, hv), lambda i: (0, 0)),
            pl.BlockSpec(w_out.shape, lambda i: (0, 0)),
        ] + post_in,
        out_specs=post_out,
        out_shape=_post_out_shapes(t, d),
        compiler_params=_params("parallel"),
        name="even_out",
    )(att, o_f, o_b, p, norm_g, w_out, x, g1, sc2, sh2, lng, lnb, wr_hi, wr_lo, rb)


def _combine(x_ref, ya_ref, yb_ref, gate_ref, g2_ref, lng_ref, lnb_ref):
    gate = gate_ref[...]
    y = gate[:, 0:1] * ya_ref[...] + gate[:, 1:2] * yb_ref[...]
    return _layer_norm(DEEPNORM_ALPHA * x_ref[...] + g2_ref[0] * y, lng_ref[...], lnb_ref[...])


def _combine_proj_kernel(x_ref, ya_ref, yb_ref, gate_ref, g2_ref, lng_ref, lnb_ref, sc_ref, sh_ref, w_ref,
                         x2_ref, u_ref):
    x2 = _combine(x_ref, ya_ref, yb_ref, gate_ref, g2_ref, lng_ref, lnb_ref)
    x2_ref[...] = x2
    u_ref[...] = _dot((x2 * (1.0 + sc_ref[0]) + sh_ref[0]).astype(BF16), w_ref[...])


def _combine_kernel(x_ref, ya_ref, yb_ref, gate_ref, g2_ref, lng_ref, lnb_ref, x2_ref):
    x2_ref[...] = _combine(x_ref, ya_ref, yb_ref, gate_ref, g2_ref, lng_ref, lnb_ref)


def _combine_call(x1, y2, gate, g2, lng, lnb, rows_per_batch, proj=None, tm=256):
    t, d = x1.shape
    nt = t // tm

    def bmap(i):
        return (i * tm // rows_per_batch, 0, 0)

    row = pl.BlockSpec((tm, d), lambda i: (i, 0))
    mod = pl.BlockSpec((1, 1, d), bmap)
    vec = pl.BlockSpec((1, d), lambda i: (0, 0))
    in_specs = [row, row, pl.BlockSpec((tm, d), lambda i: (nt + i, 0)),
                pl.BlockSpec((tm, LANES), lambda i: (i, 0)), mod, vec, vec]
    args = [x1, y2, y2, gate, g2, lng, lnb]
    if proj is None:
        return pl.pallas_call(
            _combine_kernel, grid=(nt,), in_specs=in_specs, out_specs=row,
            out_shape=jax.ShapeDtypeStruct((t, d), F32),
            compiler_params=_params("parallel"), name="combine_ln")(*args)
    sc, sh, w = proj
    return pl.pallas_call(
        _combine_proj_kernel, grid=(nt,),
        in_specs=in_specs + [mod, mod, pl.BlockSpec(w.shape, lambda i: (0, 0))],
        out_specs=[row, pl.BlockSpec((tm, w.shape[1]), lambda i: (i, 0))],
        out_shape=[jax.ShapeDtypeStruct((t, d), F32), jax.ShapeDtypeStruct((t, w.shape[1]), F32)],
        compiler_params=_params("parallel"), name="combine_ln_proj")(*args, sc, sh, w)


def _pool_out_kernel(up_ref, uc_ref, un_ref, wg_ref, ps_ref, wo_ref, *rest, n_seq):
    tm, d = uc_ref.shape
    n_grp = len(POOL_WINDOWS)
    ch = d // n_grp
    halo = POOL_HALO
    pos0 = (pl.program_id(0) * tm) % n_seq
    e_pos = pos0 - halo + lax.broadcasted_iota(jnp.int32, (tm + 2 * halo, 1), 0)
    e_ok = (e_pos >= 0) & (e_pos < n_seq)
    t_pos = pos0 + lax.broadcasted_iota(jnp.int32, (tm, 1), 0)
    y = None
    for gi, w in enumerate(POOL_WINDOWS):
        cs = slice(gi * ch, (gi + 1) * ch)
        u = uc_ref[:, cs]
        ext = jnp.where(e_ok, jnp.concatenate([up_ref[:, cs], u, un_ref[:, cs]], axis=0), 0.0)
        a, span = ext, 1
        while span < w:
            a = a[:a.shape[0] - span] + a[span:]
            span *= 2
        start = halo - w // 2
        win = a[start:start + tm]
        cnt = (jnp.minimum(t_pos + (w - w // 2), n_seq) - jnp.maximum(t_pos - w // 2, 0)).astype(F32)
        mixed = (win / cnt - u).astype(BF16)
        z = (_dot(mixed, wg_ref[gi]) * ps_ref[:, cs]).astype(BF16)
        part = _dot(z, wo_ref[cs, :])
        y = part if y is None else y + part
    _post_mix(y, *rest)


def _pool_out(u, w_grp, scale, w_out, x, g1, sc2, sh2, lng, lnb, wr_hi, wr_lo, rb, n_seq, tm=256):
    t, d = x.shape
    hb = tm // POOL_HALO
    n_hb = t // POOL_HALO
    post_in, post_out = _post_specs(d, tm, n_seq)
    return pl.pallas_call(
        functools.partial(_pool_out_kernel, n_seq=n_seq),
        grid=(t // tm,),
        in_specs=[
            pl.BlockSpec((POOL_HALO, d), lambda i: (jnp.maximum(i * hb - 1, 0), 0)),
            pl.BlockSpec((tm, d), lambda i: (i, 0)),
            pl.BlockSpec((POOL_HALO, d), lambda i: (jnp.minimum((i + 1) * hb, n_hb - 1), 0)),
            pl.BlockSpec(w_grp.shape, lambda i: (0, 0, 0)),
            pl.BlockSpec((1, d), lambda i: (0, 0)),
            pl.BlockSpec(w_out.shape, lambda i: (0, 0)),
        ] + post_in,
        out_specs=post_out,
        out_shape=_post_out_shapes(t, d),
        compiler_params=_params("parallel"),
        name="pool_out",
    )(u, u, u, w_grp, scale, w_out, x, g1, sc2, sh2, lng, lnb, wr_hi, wr_lo, rb)


def _moe_kernel(be_ref, nv_ref, first_ref, ws_ref, nxt_ref, idx_ref, idxn_ref, tok_hbm, w1_hbm, w3_hbm, w2_hbm,
                y_hbm, xbuf, ybuf, wf1, wf3, wf2, w1b, w3b, w2b, gsem, ssem, wsem, *, n_tok, layer, n_blocks):
    i = pl.program_id(0)
    nv = pl.multiple_of(nv_ref[i], MOE_ROW_ALIGN)
    xs = i % 2

    def weight_copies(e, ws):
        return (pltpu.make_async_copy(w1_hbm.at[layer, e], wf1.at[ws], wsem.at[ws]),
                pltpu.make_async_copy(w3_hbm.at[layer, e], wf3.at[ws], wsem.at[ws]),
                pltpu.make_async_copy(w2_hbm.at[layer, e], wf2.at[ws], wsem.at[ws]))

    def gather_start(idx, n_rows, slot):
        def body(g, carry):
            for u in range(MOE_ROW_ALIGN):
                r = g * MOE_ROW_ALIGN + u
                tok = idx[0, 0, r] & (n_tok - 1)
                pltpu.make_async_copy(tok_hbm.at[pl.ds(tok, 1)], xbuf.at[slot, pl.ds(r, 1)], gsem.at[slot]).start()
            return carry

        lax.fori_loop(0, n_rows // MOE_ROW_ALIGN, body, 0)

    def scatter_copy(n_rows):
        return pltpu.make_async_copy(ybuf.at[pl.ds(0, n_rows)], y_hbm.at[pl.ds(0, n_rows)], ssem)

    @pl.when(i == 0)
    def _():
        xbuf[...] = jnp.zeros_like(xbuf)
        spare = pltpu.make_async_copy(xbuf.at[0, pl.ds(0, MOE_ROW_ALIGN)],
                                      y_hbm.at[pl.ds(MOE_TOP_K * n_tok, MOE_ROW_ALIGN)], ssem)
        spare.start()
        spare.wait()
        for cp in weight_copies(be_ref[0], 0):
            cp.start()
        gather_start(idx_ref, nv, 0)

    @pl.when(nv > 0)
    def _():
        ws = ws_ref[i]

        @pl.when(first_ref[i] == 1)
        def _():
            for cp in weight_copies(be_ref[i], ws):
                cp.wait()
            nxt = nxt_ref[i]

            @pl.when(nxt >= 0)
            def _():
                for cp in weight_copies(nxt, 1 - ws):
                    cp.start()

            w1b[...] = wf1[ws].astype(BF16)
            w3b[...] = wf3[ws].astype(BF16)
            w2b[...] = wf2[ws].astype(BF16)

        nv_next = pl.multiple_of(nv_ref[jnp.minimum(i + 1, n_blocks - 1)], MOE_ROW_ALIGN)

        @pl.when((i + 1 < n_blocks) & (nv_next > 0))
        def _():
            gather_start(idxn_ref, nv_next, 1 - xs)

        pltpu.make_async_copy(tok_hbm.at[pl.ds(0, nv)], xbuf.at[xs, pl.ds(0, nv)], gsem.at[xs]).wait()
        xb = xbuf[xs].astype(BF16)
        h = (_silu(_dot(xb, w1b[...])) * _dot(xb, w3b[...])).astype(BF16)

        @pl.when(i > 0)
        def _():
            scatter_copy(pl.multiple_of(nv_ref[jnp.maximum(i - 1, 0)], MOE_ROW_ALIGN)).wait()

        ybuf[...] = _dot(h, w2b[...])

        def scatter(g, carry):
            for u in range(MOE_ROW_ALIGN):
                r = g * MOE_ROW_ALIGN + u
                pltpu.make_async_copy(ybuf.at[pl.ds(r, 1)], y_hbm.at[pl.ds(idx_ref[0, 0, r], 1)], ssem).start()
            return carry

        lax.fori_loop(0, nv // MOE_ROW_ALIGN, scatter, 0)

        @pl.when(i == n_blocks - 1)
        def _():
            scatter_copy(nv).wait()

    @pl.when((nv == 0) & (i > 0))
    def _():
        nv_prev = pl.multiple_of(nv_ref[jnp.maximum(i - 1, 0)], MOE_ROW_ALIGN)

        @pl.when(nv_prev > 0)
        def _():
            scatter_copy(nv_prev).wait()


def _moe_dispatch(eid, n_blocks):
    n_tok = eid.shape[0]
    n_assign = n_tok * MOE_TOP_K
    flat_e = eid.reshape(n_assign)
    order = jnp.argsort(flat_e, stable=True).astype(jnp.int32)
    e_sorted = flat_e[order]
    counts = jnp.bincount(flat_e, length=N_EXPERTS).astype(jnp.int32)
    padded = (counts + MOE_ROWS - 1) // MOE_ROWS * MOE_ROWS
    start = jnp.cumsum(counts) - counts
    pad_end = jnp.cumsum(padded)
    pad_start = pad_end - padded
    dest = pad_start[e_sorted] + jnp.arange(n_assign, dtype=jnp.int32) - start[e_sorted]
    dst_row = (order % MOE_TOP_K) * n_tok + order // MOE_TOP_K
    spare = MOE_TOP_K * n_tok + jnp.arange(n_blocks * MOE_ROWS, dtype=jnp.int32) % MOE_ROW_ALIGN
    slot = spare.at[dest].set(dst_row)
    blk0 = jnp.arange(n_blocks, dtype=jnp.int32) * MOE_ROWS
    be = jnp.minimum(jnp.searchsorted(pad_end, blk0, side='right'), N_EXPERTS - 1).astype(jnp.int32)
    nv = jnp.clip(counts[be] - (blk0 - pad_start[be]), 0, MOE_ROWS).astype(jnp.int32)
    nv = jnp.where(blk0 < pad_end[-1], nv, 0)
    nv = (nv + MOE_ROW_ALIGN - 1) // MOE_ROW_ALIGN * MOE_ROW_ALIGN
    ar = jnp.arange(n_blocks, dtype=jnp.int32)
    first = ((nv > 0) & ((ar == 0) | (be != jnp.roll(be, 1)))).astype(jnp.int32)
    ws = ((jnp.cumsum(first) - 1) % 2).astype(jnp.int32)
    later_first = lax.cummin(jnp.where(first == 1, ar, n_blocks), axis=0, reverse=True)
    nxt_idx = jnp.concatenate([later_first[1:], jnp.full((1,), n_blocks, jnp.int32)])
    nxt = jnp.where(nxt_idx < n_blocks, be[jnp.minimum(nxt_idx, n_blocks - 1)], -1).astype(jnp.int32)
    return slot.reshape(n_blocks, 1, MOE_ROWS), be, nv, first, ws, nxt


def _moe_experts(tok, eid, w1, w3, w2, layer):
    n_tok, d = tok.shape
    assert n_tok & (n_tok - 1) == 0
    ff = w1.shape[3]
    n_assign = n_tok * MOE_TOP_K
    n_blocks = -(-(n_assign + N_EXPERTS * (MOE_ROWS - 1)) // MOE_ROWS)
    slot, be, nv, first, ws, nxt = _moe_dispatch(eid, n_blocks)
    grid_spec = pltpu.PrefetchScalarGridSpec(
        num_scalar_prefetch=5,
        grid=(n_blocks,),
        in_specs=[
            pl.BlockSpec((1, 1, MOE_ROWS), lambda i, *_: (i, 0, 0), memory_space=pltpu.SMEM),
            pl.BlockSpec((1, 1, MOE_ROWS), lambda i, *_: (jnp.minimum(i + 1, n_blocks - 1), 0, 0),
                         memory_space=pltpu.SMEM),
            pl.BlockSpec(memory_space=pl.ANY),
            pl.BlockSpec(memory_space=pl.ANY),
            pl.BlockSpec(memory_space=pl.ANY),
            pl.BlockSpec(memory_space=pl.ANY),
        ],
        out_specs=pl.BlockSpec(memory_space=pl.ANY),
        scratch_shapes=[
            pltpu.VMEM((2, MOE_ROWS, d), F32), pltpu.VMEM((MOE_ROWS, d), F32),
            pltpu.VMEM((2, d, ff), F32), pltpu.VMEM((2, d, ff), F32), pltpu.VMEM((2, ff, d), F32),
            pltpu.VMEM((d, ff), BF16), pltpu.VMEM((d, ff), BF16), pltpu.VMEM((ff, d), BF16),
            pltpu.SemaphoreType.DMA((2,)), pltpu.SemaphoreType.DMA(()), pltpu.SemaphoreType.DMA((2,)),
        ],
    )
    return pl.pallas_call(
        functools.partial(_moe_kernel, n_tok=n_tok, layer=layer, n_blocks=n_blocks),
        grid_spec=grid_spec,
        out_shape=jax.ShapeDtypeStruct((MOE_TOP_K * n_tok + MOE_ROW_ALIGN, d), F32),
        compiler_params=pltpu.CompilerParams(dimension_semantics=("arbitrary",),
                                             vmem_limit_bytes=MOE_VMEM_LIMIT_BYTES),
        name="moe_experts",
    )(be, nv, first, ws, nxt, slot, slot, tok, w1, w3, w2)


def _rope_tables(n_seq):
    half = HEAD_DIM // 2
    n_freq = half // 2
    t = jnp.arange(n_seq)
    row = (t // GRID_W).astype(F32)
    col = (t % GRID_W).astype(F32)
    inv_freq = ROPE_BASE ** (-jnp.arange(n_freq, dtype=F32) / n_freq)
    ang_r = row[:, None] * inv_freq[None, :]
    ang_c = col[:, None] * inv_freq[None, :]
    cos = jnp.concatenate([jnp.cos(ang_r)] * 2 + [jnp.cos(ang_c)] * 2, axis=-1)
    sin = jnp.concatenate([-jnp.sin(ang_r), jnp.sin(ang_r), -jnp.sin(ang_c), jnp.sin(ang_c)], axis=-1)
    return cos, sin


def _router_weights(w_g, b_g, w_e, b_e):
    d = w_g.shape[0]
    n = w_g.shape[1] + w_e.shape[1]
    wr = jnp.concatenate([w_g, w_e, jnp.zeros((d, LANES - n), F32)], axis=1)
    rb = jnp.concatenate([b_g, b_e, jnp.zeros((LANES - n,), F32)]).reshape(1, LANES)
    hi = wr.astype(BF16)
    lo = (wr - hi.astype(F32)).astype(BF16)
    return hi, lo, rb


def kernel(x, c, ctx, c_ctx, ada_w, ada_b, ln_g, ln_b, mix_w_in, att_sink, hg_lb, hg_norm_g, mix_w_out, pool_w_in, pool_w_grp, pool_scale, pool_w_out, rt_group_w, rt_group_b, rt_expert_w, rt_expert_b, moe_w1, moe_w3, moe_w2):
    b, n, d = x.shape
    n_ctx = ctx.shape[1]
    t = b * n
    xf = x.reshape(t, d)
    ctxf = ctx.reshape(b * n_ctx, d)

    cond = jnp.concatenate([c, c_ctx[None, :], jnp.zeros((8 - b - 1, d), F32)], axis=0)
    mod = _ada_mod(cond, ada_w, ada_b)

    def chunk(l, j, rows=slice(0, b)):
        return mod[l, rows, j * d:(j + 1) * d][:, None, :]

    w_in = mix_w_in[0].astype(BF16)
    cos, sin = _rope_tables(n)
    q_w, kv_w = ATT_HEADS * HEAD_DIM, ATT_KV_HEADS * HEAD_DIM
    n_att = q_w + 2 * kv_w
    qkv, p = _mod_matmul(xf, chunk(0, 1), chunk(0, 0), w_in, cos, sin, lambda j: j, n_att, w_in.shape[1] - n_att,
                         n_q=q_w, n_rope=q_w + kv_w, n_seq=n, tm=1024, tn=512)
    ctx_rows = slice(b, b + 1)
    kv_ctx, pc = _mod_matmul(ctxf, chunk(0, 1, ctx_rows), chunk(0, 0, ctx_rows), w_in, cos, sin,
                             lambda j: jnp.where(j < 2, j + 2, j + 4), 2 * kv_w, 3 * HG_HEADS * HG_KEY,
                             n_q=0, n_rope=0, n_seq=n, tm=b * n_ctx, tn=512)
    att = _window_attention(qkv, kv_ctx, att_sink[0], b, n, n_ctx)
    o_f = _hgrn2_scan(p, pc, hg_lb[:, :, :], b, n, n_ctx, rev=False)
    o_b = _hgrn2_scan(p, pc, hg_lb[:, :, :], b, n, n_ctx, rev=True)
    wr_hi, wr_lo, rb = _router_weights(rt_group_w[0], rt_group_b[0], rt_expert_w[0], rt_expert_b[0])
    x1, tok, eid, gate = _even_out(
        att, o_f, o_b, p, hg_norm_g[0][None, :], mix_w_out[0].astype(BF16), xf,
        chunk(0, 2), chunk(0, 4), chunk(0, 3), ln_g[0, 0][None, :], ln_b[0, 0][None, :], wr_hi, wr_lo, rb, n)
    y2 = _moe_experts(tok, eid[:, :MOE_TOP_K], moe_w1, moe_w3, moe_w2, 0)

    x2, u = _combine_call(x1, y2, gate, chunk(0, 5), ln_g[0, 1][None, :], ln_b[0, 1][None, :], n,
                          proj=(chunk(1, 1), chunk(1, 0), pool_w_in[0].astype(BF16)))
    wr_hi, wr_lo, rb = _router_weights(rt_group_w[1], rt_group_b[1], rt_expert_w[1], rt_expert_b[1])
    x3, tok, eid, gate = _pool_out(
        u, pool_w_grp[0].astype(BF16), pool_scale[0][None, :], pool_w_out[0].astype(BF16), x2,
        chunk(1, 2), chunk(1, 4), chunk(1, 3), ln_g[1, 0][None, :], ln_b[1, 0][None, :], wr_hi, wr_lo, rb, n)
    y2 = _moe_experts(tok, eid[:, :MOE_TOP_K], moe_w1, moe_w3, moe_w2, 1)
    out = _combine_call(x3, y2, gate, chunk(1, 5), ln_g[1, 1][None, :], ln_b[1, 1][None, :], n)
    return out.reshape(b, n, d)
```

```python
import functools

import jax
import jax.numpy as jnp
from jax import lax
from jax.experimental import pallas as pl
from jax.experimental.pallas import tpu as pltpu

F32 = jnp.float32
BF16 = jnp.bfloat16

LANES = 128
VMEM_LIMIT_BYTES = 56 * 1024 * 1024
MOE_VMEM_LIMIT_BYTES = 60 * 1024 * 1024

GRID_W = 64
ATT_HEADS = 8
ATT_KV_HEADS = 4
ATT_GROUP = ATT_HEADS // ATT_KV_HEADS
HEAD_DIM = 128
WINDOW = 128
ATT_BLOCK = 128
ROPE_BASE = 10000.0
HG_HEADS = 8
HG_KEY = 128
HG_CHUNK = 64
HG_SUB = 16
HG_FAST_RANGE = 80.0
NORM_EPS = 1e-6
POOL_WINDOWS = (2, 4, 8, 16)
POOL_HALO = 8
MOE_GROUPS = 4
MOE_EXPERTS_PER_GROUP = 8
N_EXPERTS = MOE_GROUPS * MOE_EXPERTS_PER_GROUP
MOE_TOP_K = 2
MOE_ROWS = 256
MOE_ROW_ALIGN = 8
LN_EPS = 1e-5
DEPTH = 2
DEEPNORM_ALPHA = (2 * DEPTH) ** 0.25


def _dot(a, b):
    return jnp.dot(a, b, preferred_element_type=F32)


def _dot_nt(a, b):
    return lax.dot_general(a, b, (((1,), (1,)), ((), ())), preferred_element_type=F32)


def _dot_tn(a, b):
    return lax.dot_general(a, b, (((0,), (0,)), ((), ())), preferred_element_type=F32)


def _sigmoid(x):
    return 1.0 / (1.0 + jnp.exp(-x))


def _silu(x):
    return x * _sigmoid(x)


def _params(*sem):
    return pltpu.CompilerParams(dimension_semantics=sem, vmem_limit_bytes=VMEM_LIMIT_BYTES)


def _layer_norm(z, g, b):
    mu = jnp.mean(z, axis=-1, keepdims=True)
    zc = z - mu
    var = jnp.mean(zc * zc, axis=-1, keepdims=True)
    return zc * lax.rsqrt(var + LN_EPS) * g + b


def _ada_kernel(s_ref, w_ref, b_ref, o_ref):
    s = _silu(s_ref[...]).astype(BF16)
    o_ref[0] = _dot(s, w_ref[0].astype(BF16)) + b_ref[0]


def _ada_mod(s, ada_w, ada_b, tn=1024):
    n_l, d, n = ada_w.shape
    return pl.pallas_call(
        _ada_kernel,
        grid=(n_l, n // tn),
        in_specs=[
            pl.BlockSpec((8, d), lambda l, j: (0, 0)),
            pl.BlockSpec((1, d, tn), lambda l, j: (l, 0, j)),
            pl.BlockSpec((1, 1, tn), lambda l, j: (l, 0, j)),
        ],
        out_specs=pl.BlockSpec((1, 8, tn), lambda l, j: (l, 0, j)),
        out_shape=jax.ShapeDtypeStruct((n_l, 8, n), F32),
        compiler_params=_params("parallel", "parallel"),
        name="ada_mod",
    )(s, ada_w, ada_b.reshape(n_l, 1, n))


def _rope(t, cos, sin_signed, first_half):
    partner = jnp.where(first_half, pltpu.roll(t, 96, 1), pltpu.roll(t, 32, 1))
    return t * cos + partner * sin_signed


def _modmm_kernel(x_ref, sc_ref, sh_ref, w_ref, cos_ref, sin_ref, oa_ref, ob_ref, xs_ref, *,
                  n_q_tiles, n_rope_tiles, n_a_tiles):
    j = pl.program_id(1)

    @pl.when(j == 0)
    def _():
        xs_ref[...] = (x_ref[...] * (1.0 + sc_ref[0]) + sh_ref[0]).astype(BF16)

    acc = _dot(xs_ref[...], w_ref[...])

    if n_rope_tiles:
        @pl.when(j < n_rope_tiles)
        def _():
            lane = lax.broadcasted_iota(jnp.int32, (1, HEAD_DIM), 1)
            first_half = (lane % 64) < 32
            scale = jnp.where(j < n_q_tiles, HEAD_DIM ** -0.5, 1.0)
            cos, sin = cos_ref[...] * scale, sin_ref[...] * scale
            for h in range(acc.shape[1] // HEAD_DIM):
                sl = slice(h * HEAD_DIM, (h + 1) * HEAD_DIM)
                oa_ref[:, sl] = _rope(acc[:, sl], cos, sin, first_half).astype(BF16)

    @pl.when((j >= n_rope_tiles) & (j < n_a_tiles))
    def _():
        oa_ref[...] = acc.astype(BF16)

    @pl.when(j >= n_a_tiles)
    def _():
        ob_ref[...] = acc


def _mod_matmul(x, sc, sh, w, cos, sin, col_map, n_a, n_b, n_q, n_rope, n_seq, tm, tn):
    m, k = x.shape
    rows_per_mod = m // sc.shape[0]
    ta, tb = n_a // tn, n_b // tn
    tab = pl.BlockSpec((tm, HEAD_DIM), lambda i, j: ((i * tm % n_seq) // tm, 0))
    return pl.pallas_call(
        functools.partial(_modmm_kernel, n_q_tiles=n_q // tn, n_rope_tiles=n_rope // tn, n_a_tiles=ta),
        grid=(m // tm, ta + tb),
        in_specs=[
            pl.BlockSpec((tm, k), lambda i, j: (i, 0)),
            pl.BlockSpec((1, 1, k), lambda i, j: (i * tm // rows_per_mod, 0, 0)),
            pl.BlockSpec((1, 1, k), lambda i, j: (i * tm // rows_per_mod, 0, 0)),
            pl.BlockSpec((k, tn), lambda i, j: (0, col_map(j))),
            tab, tab,
        ],
        out_specs=[pl.BlockSpec((tm, tn), lambda i, j: (i, jnp.minimum(j, ta - 1))),
                   pl.BlockSpec((tm, tn), lambda i, j: (i, jnp.maximum(j - ta, 0)))],
        out_shape=[jax.ShapeDtypeStruct((m, n_a), BF16), jax.ShapeDtypeStruct((m, n_b), F32)],
        scratch_shapes=[pltpu.VMEM((tm, k), BF16)],
        compiler_params=_params("parallel", "arbitrary"),
        name="mod_matmul",
    )(x, sc, sh, w, cos, sin)


def _attn_kernel(sink_ref, q_ref, kp_ref, kc_ref, kn_ref, vp_ref, vc_ref, vn_ref, kx_ref, vx_ref,
                 mp_ref, mn_ref, o_ref, *, n_blocks):
    n = pl.program_id(1)
    blk = ATT_BLOCK
    row1 = lax.broadcasted_iota(jnp.int32, (ATT_GROUP * blk, 1), 0)
    has_prev, has_next = n > 0, n < n_blocks - 1
    for h in range(ATT_KV_HEADS):
        kv = slice(h * HEAD_DIM, (h + 1) * HEAD_DIM)
        q2 = jnp.concatenate([q_ref[:, (ATT_GROUP * h + g) * HEAD_DIM:(ATT_GROUP * h + g + 1) * HEAD_DIM]
                              for g in range(ATT_GROUP)], axis=0)
        s_p = jnp.where(has_prev, _dot_nt(q2, kp_ref[:, kv]) + mp_ref[...], -jnp.inf)
        s_c = _dot_nt(q2, kc_ref[:, kv])
        s_n = jnp.where(has_next, _dot_nt(q2, kn_ref[:, kv]) + mn_ref[...], -jnp.inf)
        s_x = _dot_nt(q2, kx_ref[:, kv])
        sink = jnp.where(row1 < blk, sink_ref[ATT_GROUP * h], sink_ref[ATT_GROUP * h + 1])
        m = jnp.maximum(jnp.maximum(jnp.max(s_p, axis=-1, keepdims=True), jnp.max(s_c, axis=-1, keepdims=True)),
                        jnp.maximum(jnp.max(s_n, axis=-1, keepdims=True), jnp.max(s_x, axis=-1, keepdims=True)))
        m = jnp.maximum(m, sink)
        p_p, p_c, p_n, p_x = (jnp.exp(s - m) for s in (s_p, s_c, s_n, s_x))
        den = (jnp.sum(p_p, axis=-1, keepdims=True) + jnp.sum(p_c, axis=-1, keepdims=True)
               + jnp.sum(p_n, axis=-1, keepdims=True) + jnp.sum(p_x, axis=-1, keepdims=True)
               + jnp.exp(sink - m))
        o = (_dot(p_p.astype(BF16), vp_ref[:, kv]) + _dot(p_c.astype(BF16), vc_ref[:, kv])
             + _dot(p_n.astype(BF16), vn_ref[:, kv]) + _dot(p_x.astype(BF16), vx_ref[:, kv])) / den
        for g in range(ATT_GROUP):
            col = (ATT_GROUP * h + g) * HEAD_DIM
            o_ref[:, col:col + HEAD_DIM] = o[g * blk:(g + 1) * blk].astype(o_ref.dtype)


def _window_attention(qkv, kv_ctx, sink, batch, n_seq, n_ctx):
    assert ATT_GROUP == 2 and WINDOW == ATT_BLOCK
    nb = n_seq // ATT_BLOCK
    qw, kw = ATT_HEADS * HEAD_DIM, ATT_KV_HEADS * HEAD_DIM
    kcol, vcol = qw // kw, qw // kw + 1

    def rows(off):
        return lambda b, n, off=off: b * nb + jnp.clip(n + off, 0, nb - 1)

    kspec = [pl.BlockSpec((ATT_BLOCK, kw), lambda b, n, r=rows(o): (r(b, n), kcol)) for o in (-1, 0, 1)]
    vspec = [pl.BlockSpec((ATT_BLOCK, kw), lambda b, n, r=rows(o): (r(b, n), vcol)) for o in (-1, 0, 1)]
    r = jnp.arange(ATT_GROUP * ATT_BLOCK)[:, None] % ATT_BLOCK
    c = jnp.arange(ATT_BLOCK)[None, :]
    mask_prev = jnp.where(c >= r, 0.0, -jnp.inf).astype(F32)
    mask_next = jnp.where(c <= r, 0.0, -jnp.inf).astype(F32)
    mspec = pl.BlockSpec(mask_prev.shape, lambda b, n: (0, 0))
    return pl.pallas_call(
        functools.partial(_attn_kernel, n_blocks=nb),
        grid=(batch, nb),
        in_specs=[pl.BlockSpec(memory_space=pltpu.SMEM),
                  pl.BlockSpec((ATT_BLOCK, qw), lambda b, n: (b * nb + n, 0))]
        + kspec + vspec
        + [pl.BlockSpec((n_ctx, kw), lambda b, n: (b, 0)), pl.BlockSpec((n_ctx, kw), lambda b, n: (b, 1)),
           mspec, mspec],
        out_specs=pl.BlockSpec((ATT_BLOCK, qw), lambda b, n: (b * nb + n, 0)),
        out_shape=jax.ShapeDtypeStruct((batch * n_seq, qw), BF16),
        compiler_params=_params("parallel", "parallel"),
        name="window_attention",
    )(sink, qkv, qkv, qkv, qkv, qkv, qkv, qkv, kv_ctx, kv_ctx, mask_prev, mask_next)


def _gla_tile(zf, q_raw, v, lb, st_ref, o_ref, rev):
    c_len = zf.shape[0]
    f = lb + (1.0 - lb) * _sigmoid(zf)
    k = 1.0 - f
    g = jnp.log(f)
    ti = lax.broadcasted_iota(jnp.int32, (c_len, c_len), 0)
    si = lax.broadcasted_iota(jnp.int32, (c_len, c_len), 1)
    seen = (si >= ti) if rev else (si <= ti)
    tri = jnp.where(seen, 1.0, 0.0).astype(BF16)
    g1 = g.astype(BF16)
    r1 = g - g1.astype(F32)
    g2 = r1.astype(BF16)
    g3 = (r1 - g2.astype(F32)).astype(BF16)
    c = _dot(tri, g1) + _dot(tri, g2) + _dot(tri, g3)
    c_end = c[0:1] if rev else c[c_len - 1:c_len]
    k_end = (k * jnp.exp(c_end - c)).astype(BF16)
    dec = jnp.exp(c_end)
    vb = v.astype(BF16)

    if o_ref is not None:
        q = _silu(q_raw)
        q_in = (q * jnp.exp(c)).astype(BF16)
        in_range = jnp.min(c_end) >= -HG_FAST_RANGE

        @pl.when(in_range)
        def _():
            k_in = (k * jnp.exp(-c)).astype(BF16)
            for h in range(HG_HEADS):
                sl = slice(h * HG_KEY, (h + 1) * HG_KEY)
                sc = jnp.where(seen, _dot_nt(q_in[:, sl], k_in[:, sl]), 0.0).astype(BF16)
                o_ref[:, sl] = _dot_nt(q_in[:, sl], st_ref[h].astype(BF16)) + _dot(sc, vb[:, sl])

        @pl.when(jnp.logical_not(in_range))
        def _():
            _gla_intra_exact(q, k, v, vb, c, q_in, st_ref, o_ref, rev)

    for h in range(HG_HEADS):
        sl = slice(h * HG_KEY, (h + 1) * HG_KEY)
        st_ref[h] = st_ref[h] * dec[:, sl] + _dot_tn(vb[:, sl], k_end[:, sl])


def _gla_intra_exact(q, k, v, vb, c, q_in, st_ref, o_ref, rev):
    c_len = q.shape[0]
    pairs = []
    size = c_len // 2
    while size >= HG_SUB:
        for lo in range(0, c_len, 2 * size):
            pairs.append((lo, lo + size, lo + 2 * size))
        size //= 2
    scaled = []
    for lo, mid, hi in pairs:
        if rev:
            late, early, bnd = slice(lo, mid), slice(mid, hi), mid
        else:
            late, early, bnd = slice(mid, hi), slice(lo, mid), mid - 1
        cb = c[bnd:bnd + 1]
        q_l = (q[late] * jnp.exp(c[late] - cb)).astype(BF16)
        k_e = (k[early] * jnp.exp(cb - c[early])).astype(BF16)
        scaled.append((late, early, q_l, k_e))
    n_sub = c_len // HG_SUB
    t_idx = lax.broadcasted_iota(jnp.int32, (HG_SUB, 1), 0)
    diag = [[None] * HG_HEADS for _ in range(n_sub)]
    for b in range(n_sub):
        r0 = b * HG_SUB
        qb, cb = q[r0:r0 + HG_SUB], c[r0:r0 + HG_SUB]
        for s in range(HG_SUB):
            row = r0 + s
            ok = (t_idx <= s) if rev else (t_idx >= s)
            w = qb * k[row:row + 1] * jnp.exp(jnp.where(ok, cb - c[row:row + 1], -jnp.inf))
            for h in range(HG_HEADS):
                sl = slice(h * HG_KEY, (h + 1) * HG_KEY)
                contrib = jnp.sum(w[:, sl], axis=-1, keepdims=True) * v[row:row + 1, sl]
                diag[b][h] = contrib if diag[b][h] is None else diag[b][h] + contrib

    for h in range(HG_HEADS):
        sl = slice(h * HG_KEY, (h + 1) * HG_KEY)
        o_h = _dot_nt(q_in[:, sl], st_ref[h].astype(BF16))
        parts = [diag[b][h] for b in range(n_sub)]
        for late, early, q_l, k_e in scaled:
            sc = _dot_nt(q_l[:, sl], k_e[:, sl]).astype(BF16)
            add = _dot(sc, vb[early, sl])
            b0 = late.start // HG_SUB
            for j in range((late.stop - late.start) // HG_SUB):
                parts[b0 + j] = parts[b0 + j] + add[j * HG_SUB:(j + 1) * HG_SUB]
        o_ref[:, sl] = o_h + jnp.concatenate(parts, axis=0)


def _gla_kernel(lb_ref, zf_ref, q_ref, v_ref, zfc_ref, vc_ref, o_ref, st_ref, *, rev, n_ctx_chunks):
    s = pl.program_id(1)

    @pl.when(s == 0)
    def _():
        st_ref[...] = jnp.zeros_like(st_ref)

    x = lb_ref[...]
    e = jnp.exp(x - jnp.max(x, axis=0, keepdims=True))
    lb = e[0:1] / jnp.sum(e, axis=0, keepdims=True)

    @pl.when(s < n_ctx_chunks)
    def _():
        _gla_tile(zfc_ref[...], None, vc_ref[...], lb, st_ref, None, rev)

    @pl.when(s >= n_ctx_chunks)
    def _():
        _gla_tile(zf_ref[...], q_ref[...], v_ref[...], lb, st_ref, o_ref, rev)


def _hgrn2_scan(p, pc, hg_lb, batch, n_seq, n_ctx, rev):
    hk = HG_HEADS * HG_KEY
    nc, ncc = n_seq // HG_CHUNK, n_ctx // HG_CHUNK
    d = 1 if rev else 0

    def lat(b, s):
        j = jnp.maximum(s - ncc, 0)
        return b * nc + (nc - 1 - j if rev else j)

    def ctx(b, s):
        j = jnp.minimum(s, ncc - 1)
        return b * ncc + (ncc - 1 - j if rev else j)

    return pl.pallas_call(
        functools.partial(_gla_kernel, rev=rev, n_ctx_chunks=ncc),
        grid=(batch, ncc + nc),
        in_specs=[
            pl.BlockSpec((None, hg_lb.shape[1], hk), lambda b, s: (d, 0, 0)),
            pl.BlockSpec((HG_CHUNK, hk), lambda b, s: (lat(b, s), 1 + d)),
            pl.BlockSpec((HG_CHUNK, hk), lambda b, s: (lat(b, s), 0)),
            pl.BlockSpec((HG_CHUNK, hk), lambda b, s: (lat(b, s), 3)),
            pl.BlockSpec((HG_CHUNK, hk), lambda b, s: (ctx(b, s), d)),
            pl.BlockSpec((HG_CHUNK, hk), lambda b, s: (ctx(b, s), 2)),
        ],
        out_specs=pl.BlockSpec((HG_CHUNK, hk), lambda b, s: (lat(b, s), 0)),
        out_shape=jax.ShapeDtypeStruct((batch * n_seq, hk), F32),
        scratch_shapes=[pltpu.VMEM((HG_HEADS, HG_KEY, HG_KEY), F32)],
        compiler_params=_params("parallel", "arbitrary"),
        name="hgrn2_bwd" if rev else "hgrn2_fwd",
    )(hg_lb, p, p, p, pc, pc)


def _route(tok, wr_hi, wr_lo, rb):
    t_hi = tok.astype(BF16)
    t_lo = (tok - t_hi.astype(F32)).astype(BF16)
    lg = _dot(t_hi, wr_hi) + _dot(t_hi, wr_lo) + _dot(t_lo, wr_hi) + rb
    lane = lax.broadcasted_iota(jnp.int32, lg.shape, 1)
    lane_f = lane.astype(F32)
    ninf = -jnp.inf
    gl = jnp.where(lane < MOE_GROUPS, lg, ninf)
    gmax = jnp.max(gl, axis=-1, keepdims=True)
    g_idx = jnp.min(jnp.where(gl == gmax, lane_f, float(LANES)), axis=-1, keepdims=True)
    g_val = 1.0 / jnp.sum(jnp.exp(gl - gmax), axis=-1, keepdims=True)
    e_lane = lane_f - float(MOE_GROUPS)
    lo = g_idx * float(MOE_EXPERTS_PER_GROUP)
    in_grp = (e_lane >= lo) & (e_lane < lo + float(MOE_EXPERTS_PER_GROUP))
    el = jnp.where(in_grp, lg, ninf)
    l1 = jnp.max(el, axis=-1, keepdims=True)
    i1 = jnp.min(jnp.where(el == l1, e_lane, float(LANES)), axis=-1, keepdims=True)
    el2 = jnp.where(e_lane == i1, ninf, el)
    l2 = jnp.max(el2, axis=-1, keepdims=True)
    i2 = jnp.min(jnp.where(el2 == l2, e_lane, float(LANES)), axis=-1, keepdims=True)
    r = jnp.exp(l2 - l1)
    w1 = g_val / (1.0 + r)
    w2 = w1 * r
    eid = jnp.where(lane == 0, i1, jnp.where(lane == 1, i2, 0.0)).astype(jnp.int32)
    gate = jnp.where(lane == 0, w1, jnp.where(lane == 1, w2, 0.0))
    return eid, gate


def _post_mix(y, x_ref, g1_ref, sc2_ref, sh2_ref, lng_ref, lnb_ref, wrh_ref, wrl_ref, rb_ref,
              x1_ref, tok_ref, eid_ref, gate_ref):
    x1 = _layer_norm(DEEPNORM_ALPHA * x_ref[...] + g1_ref[0] * y, lng_ref[...], lnb_ref[...])
    x1_ref[...] = x1
    tok = x1 * (1.0 + sc2_ref[0]) + sh2_ref[0]
    tok_ref[...] = tok
    eid, gate = _route(tok, wrh_ref[...], wrl_ref[...], rb_ref[...])
    eid_ref[...] = eid
    gate_ref[...] = gate


def _even_out_kernel(att_ref, of_ref, ob_ref, gt_ref, ng_ref, wo_ref, *rest):
    o = of_ref[...] + ob_ref[...]
    pieces = []
    for h in range(HG_HEADS):
        oh = o[:, h * HG_KEY:(h + 1) * HG_KEY]
        pieces.append(oh * lax.rsqrt(jnp.mean(oh * oh, axis=-1, keepdims=True) + NORM_EPS))
    hg = (jnp.concatenate(pieces, axis=-1) * ng_ref[...] * _silu(gt_ref[...])).astype(BF16)
    n_att = att_ref.shape[1]
    y = _dot(att_ref[...], wo_ref[:n_att, :]) + _dot(hg, wo_ref[n_att:, :])
    _post_mix(y, *rest)


def _post_specs(d, tm, rows_per_batch):
    def bmap(i):
        return (i * tm // rows_per_batch, 0, 0)

    row = pl.BlockSpec((tm, d), lambda i: (i, 0))
    mod = pl.BlockSpec((1, 1, d), bmap)
    vec = pl.BlockSpec((1, d), lambda i: (0, 0))
    rw = pl.BlockSpec((d, LANES), lambda i: (0, 0))
    in_specs = [row, mod, mod, mod, vec, vec, rw, rw, pl.BlockSpec((1, LANES), lambda i: (0, 0))]
    lane_blk = pl.BlockSpec((tm, LANES), lambda i: (i, 0))
    out_specs = [row, row, lane_blk, lane_blk]
    return in_specs, out_specs


def _post_out_shapes(t, d):
    return [jax.ShapeDtypeStruct((t, d), F32), jax.ShapeDtypeStruct((t, d), F32),
            jax.ShapeDtypeStruct((t, LANES), jnp.int32), jax.ShapeDtypeStruct((t, LANES), F32)]


def _even_out(att, o_f, o_b, p, norm_g, w_out, x, g1, sc2, sh2, lng, lnb, wr_hi, wr_lo, rb, rows_per_batch, tm=256):
    t, d = x.shape
    hv = o_f.shape[1]
    post_in, post_out = _post_specs(d, tm, rows_per_batch)
    return pl.pallas_call(
        _even_out_kernel,
        grid=(t // tm,),
        in_specs=[
            pl.BlockSpec((tm, att.shape[1]), lambda i: (i, 0)),
            pl.BlockSpec((tm, hv), lambda i: (i, 0)),
            pl.BlockSpec((tm, hv), lambda i: (i, 0)),
            pl.BlockSpec((tm, hv), lambda i: (i, 4)),
            pl.BlockSpec((1, hv), lambda i: (0, 0)),
            pl.BlockSpec(w_out.shape, lambda i: (0, 0)),
        ] + post_in,
        out_specs=post_out,
        out_shape=_post_out_shapes(t, d),
        compiler_params=_params("parallel"),
        name="even_out",
    )(att, o_f, o_b, p, norm_g, w_out, x, g1, sc2, sh2, lng, lnb, wr_hi, wr_lo, rb)


def _combine(x_ref, ya_ref, yb_ref, gate_ref, g2_ref, lng_ref, lnb_ref):
    gate = gate_ref[...]
    y = gate[:, 0:1] * ya_ref[...] + gate[:, 1:2] * yb_ref[...]
    return _layer_norm(DEEPNORM_ALPHA * x_ref[...] + g2_ref[0] * y, lng_ref[...], lnb_ref[...])


def _combine_proj_kernel(x_ref, ya_ref, yb_ref, gate_ref, g2_ref, lng_ref, lnb_ref, sc_ref, sh_ref, w_ref,
                         x2_ref, u_ref):
    x2 = _combine(x_ref, ya_ref, yb_ref, gate_ref, g2_ref, lng_ref, lnb_ref)
    x2_ref[...] = x2
    u_ref[...] = _dot((x2 * (1.0 + sc_ref[0]) + sh_ref[0]).astype(BF16), w_ref[...])


def _combine_kernel(x_ref, ya_ref, yb_ref, gate_ref, g2_ref, lng_ref, lnb_ref, x2_ref):
    x2_ref[...] = _combine(x_ref, ya_ref, yb_ref, gate_ref, g2_ref, lng_ref, lnb_ref)


def _combine_call(x1, y2, gate, g2, lng, lnb, rows_per_batch, proj=None, tm=256):
    t, d = x1.shape
    nt = t // tm

    def bmap(i):
        return (i * tm // rows_per_batch, 0, 0)

    row = pl.BlockSpec((tm, d), lambda i: (i, 0))
    mod = pl.BlockSpec((1, 1, d), bmap)
    vec = pl.BlockSpec((1, d), lambda i: (0, 0))
    in_specs = [row, row, pl.BlockSpec((tm, d), lambda i: (nt + i, 0)),
                pl.BlockSpec((tm, LANES), lambda i: (i, 0)), mod, vec, vec]
    args = [x1, y2, y2, gate, g2, lng, lnb]
    if proj is None:
        return pl.pallas_call(
            _combine_kernel, grid=(nt,), in_specs=in_specs, out_specs=row,
            out_shape=jax.ShapeDtypeStruct((t, d), F32),
            compiler_params=_params("parallel"), name="combine_ln")(*args)
    sc, sh, w = proj
    return pl.pallas_call(
        _combine_proj_kernel, grid=(nt,),
        in_specs=in_specs + [mod, mod, pl.BlockSpec(w.shape, lambda i: (0, 0))],
        out_specs=[row, pl.BlockSpec((tm, w.shape[1]), lambda i: (i, 0))],
        out_shape=[jax.ShapeDtypeStruct((t, d), F32), jax.ShapeDtypeStruct((t, w.shape[1]), F32)],
        compiler_params=_params("parallel"), name="combine_ln_proj")(*args, sc, sh, w)


def _pool_out_kernel(up_ref, uc_ref, un_ref, wg_ref, ps_ref, wo_ref, *rest, n_seq):
    tm, d = uc_ref.shape
    n_grp = len(POOL_WINDOWS)
    ch = d // n_grp
    halo = POOL_HALO
    pos0 = (pl.program_id(0) * tm) % n_seq
    e_pos = pos0 - halo + lax.broadcasted_iota(jnp.int32, (tm + 2 * halo, 1), 0)
    e_ok = (e_pos >= 0) & (e_pos < n_seq)
    t_pos = pos0 + lax.broadcasted_iota(jnp.int32, (tm, 1), 0)
    y = None
    for gi, w in enumerate(POOL_WINDOWS):
        cs = slice(gi * ch, (gi + 1) * ch)
        u = uc_ref[:, cs]
        ext = jnp.where(e_ok, jnp.concatenate([up_ref[:, cs], u, un_ref[:, cs]], axis=0), 0.0)
        a, span = ext, 1
        while span < w:
            a = a[:a.shape[0] - span] + a[span:]
            span *= 2
        start = halo - w // 2
        win = a[start:start + tm]
        cnt = (jnp.minimum(t_pos + (w - w // 2), n_seq) - jnp.maximum(t_pos - w // 2, 0)).astype(F32)
        mixed = (win / cnt - u).astype(BF16)
        z = (_dot(mixed, wg_ref[gi]) * ps_ref[:, cs]).astype(BF16)
        part = _dot(z, wo_ref[cs, :])
        y = part if y is None else y + part
    _post_mix(y, *rest)


def _pool_out(u, w_grp, scale, w_out, x, g1, sc2, sh2, lng, lnb, wr_hi, wr_lo, rb, n_seq, tm=256):
    t, d = x.shape
    hb = tm // POOL_HALO
    n_hb = t // POOL_HALO
    post_in, post_out = _post_specs(d, tm, n_seq)
    return pl.pallas_call(
        functools.partial(_pool_out_kernel, n_seq=n_seq),
        grid=(t // tm,),
        in_specs=[
            pl.BlockSpec((POOL_HALO, d), lambda i: (jnp.maximum(i * hb - 1, 0), 0)),
            pl.BlockSpec((tm, d), lambda i: (i, 0)),
            pl.BlockSpec((POOL_HALO, d), lambda i: (jnp.minimum((i + 1) * hb, n_hb - 1), 0)),
            pl.BlockSpec(w_grp.shape, lambda i: (0, 0, 0)),
            pl.BlockSpec((1, d), lambda i: (0, 0)),
            pl.BlockSpec(w_out.shape, lambda i: (0, 0)),
        ] + post_in,
        out_specs=post_out,
        out_shape=_post_out_shapes(t, d),
        compiler_params=_params("parallel"),
        name="pool_out",
    )(u, u, u, w_grp, scale, w_out, x, g1, sc2, sh2, lng, lnb, wr_hi, wr_lo, rb)


def _moe_kernel(be_ref, nv_ref, first_ref, ws_ref, nxt_ref, idx_ref, idxn_ref, idxp_ref, tok_hbm, w1_hbm, w3_hbm, w2_hbm,
                y_hbm, xbuf, ybuf, xb_ref, wf1, wf3, wf2, w1b, w3b, w2b, gsem, ssem, wsem, *, n_tok, layer, n_blocks):
    i = pl.program_id(0)
    used = nv_ref[jnp.minimum(i, n_blocks - 1)] > 0
    used = used & (i < n_blocks)
    prev_used = (i > 0) & (nv_ref[jnp.maximum(i - 1, 0)] > 0)
    xs = i % 2

    def weight_copies(e, ws):
        return (pltpu.make_async_copy(w1_hbm.at[layer, e], wf1.at[ws], wsem.at[ws]),
                pltpu.make_async_copy(w3_hbm.at[layer, e], wf3.at[ws], wsem.at[ws]),
                pltpu.make_async_copy(w2_hbm.at[layer, e], wf2.at[ws], wsem.at[ws]))

    def gather_start(idx, slot):
        for r in range(MOE_ROWS):
            tok = idx[0, 0, r] & (n_tok - 1)
            pltpu.make_async_copy(tok_hbm.at[pl.ds(tok, 1)], xbuf.at[slot, pl.ds(r, 1)], gsem.at[slot]).start()

    def gather_wait(slot):
        pltpu.make_async_copy(tok_hbm.at[pl.ds(0, MOE_ROWS)], xbuf.at[slot], gsem.at[slot]).wait()

    def scatter_start(idx, slot):
        for r in range(MOE_ROWS):
            pltpu.make_async_copy(ybuf.at[slot, pl.ds(r, 1)], y_hbm.at[pl.ds(idx[0, 0, r], 1)], ssem.at[slot]).start()

    def scatter_wait(slot):
        pltpu.make_async_copy(ybuf.at[slot], y_hbm.at[pl.ds(0, MOE_ROWS)], ssem.at[slot]).wait()

    @pl.when(i == 0)
    def _():
        xbuf[...] = jnp.zeros_like(xbuf)
        ybuf[...] = jnp.zeros_like(ybuf)
        spare0 = pltpu.make_async_copy(
            ybuf.at[0], y_hbm.at[pl.ds(MOE_TOP_K * n_tok, MOE_ROWS)], ssem.at[0])
        spare0.start()
        for cp in weight_copies(be_ref[0], 0):
            cp.start()
        gather_start(idx_ref, 0)

    @pl.when(used)
    def _():
        ws = ws_ref[i]

        @pl.when(first_ref[i] == 1)
        def _():
            for cp in weight_copies(be_ref[i], ws):
                cp.wait()
            nxt = nxt_ref[i]

            @pl.when(nxt >= 0)
            def _():
                for cp in weight_copies(nxt, 1 - ws):
                    cp.start()

            w1b[...] = wf1[ws].astype(BF16)
            w3b[...] = wf3[ws].astype(BF16)
            w2b[...] = wf2[ws].astype(BF16)

        gather_wait(xs)
        xb_ref[...] = xbuf[xs].astype(BF16)
        gather_start(idxn_ref, 1 - xs)
        scatter_start(idxp_ref, 1 - xs)
        xb = xb_ref[...]
        h = (_silu(_dot(xb, w1b[...])) * _dot(xb, w3b[...])).astype(BF16)
        y = _dot(h, w2b[...])
        scatter_wait(xs)
        ybuf[xs] = y

    @pl.when(jnp.logical_not(used) & prev_used)
    def _():
        gather_wait(xs)
        scatter_start(idxp_ref, 1 - xs)
        scatter_wait(1 - xs)
        scatter_wait(xs)


def _moe_dispatch(eid, n_blocks):
    n_tok = eid.shape[0]
    n_assign = n_tok * MOE_TOP_K
    flat_e = eid.reshape(n_assign)
    order = jnp.argsort(flat_e, stable=True).astype(jnp.int32)
    e_sorted = flat_e[order]
    counts = jnp.bincount(flat_e, length=N_EXPERTS).astype(jnp.int32)
    padded = (counts + MOE_ROWS - 1) // MOE_ROWS * MOE_ROWS
    start = jnp.cumsum(counts) - counts
    pad_end = jnp.cumsum(padded)
    pad_start = pad_end - padded
    dest = pad_start[e_sorted] + jnp.arange(n_assign, dtype=jnp.int32) - start[e_sorted]
    dst_row = (order % MOE_TOP_K) * n_tok + order // MOE_TOP_K
    blk = jnp.arange(-1, n_blocks + 1, dtype=jnp.int32)[:, None]
    row = jnp.arange(MOE_ROWS, dtype=jnp.int32)[None, :]
    spare = (MOE_TOP_K * n_tok + (blk % 2) * MOE_ROWS + row).reshape(-1)
    slot = spare.at[MOE_ROWS + dest].set(dst_row)
    blk0 = jnp.arange(n_blocks, dtype=jnp.int32) * MOE_ROWS
    be = jnp.minimum(jnp.searchsorted(pad_end, blk0, side='right'), N_EXPERTS - 1).astype(jnp.int32)
    nv = jnp.clip(counts[be] - (blk0 - pad_start[be]), 0, MOE_ROWS).astype(jnp.int32)
    nv = jnp.where(blk0 < pad_end[-1], nv, 0)
    ar = jnp.arange(n_blocks, dtype=jnp.int32)
    first = ((nv > 0) & ((ar == 0) | (be != jnp.roll(be, 1)))).astype(jnp.int32)
    ws = ((jnp.cumsum(first) - 1) % 2).astype(jnp.int32)
    later_first = lax.cummin(jnp.where(first == 1, ar, n_blocks), axis=0, reverse=True)
    nxt_idx = jnp.concatenate([later_first[1:], jnp.full((1,), n_blocks, jnp.int32)])
    nxt = jnp.where(nxt_idx < n_blocks, be[jnp.minimum(nxt_idx, n_blocks - 1)], -1).astype(jnp.int32)
    return slot.reshape(n_blocks + 2, 1, MOE_ROWS), be, nv, first, ws, nxt


def _moe_experts(tok, eid, w1, w3, w2, layer):
    n_tok, d = tok.shape
    assert n_tok & (n_tok - 1) == 0
    ff = w1.shape[3]
    n_assign = n_tok * MOE_TOP_K
    n_blocks = -(-(n_assign + N_EXPERTS * (MOE_ROWS - 1)) // MOE_ROWS)
    slot, be, nv, first, ws, nxt = _moe_dispatch(eid, n_blocks)
    grid_spec = pltpu.PrefetchScalarGridSpec(
        num_scalar_prefetch=5,
        grid=(n_blocks + 1,),
        in_specs=[
            pl.BlockSpec((1, 1, MOE_ROWS), lambda i, *_: (i + 1, 0, 0), memory_space=pltpu.SMEM),
            pl.BlockSpec((1, 1, MOE_ROWS), lambda i, *_: (jnp.minimum(i + 2, n_blocks + 1), 0, 0),
                         memory_space=pltpu.SMEM),
            pl.BlockSpec((1, 1, MOE_ROWS), lambda i, *_: (i, 0, 0), memory_space=pltpu.SMEM),
            pl.BlockSpec(memory_space=pl.ANY),
            pl.BlockSpec(memory_space=pl.ANY),
            pl.BlockSpec(memory_space=pl.ANY),
            pl.BlockSpec(memory_space=pl.ANY),
        ],
        out_specs=pl.BlockSpec(memory_space=pl.ANY),
        scratch_shapes=[
            pltpu.VMEM((2, MOE_ROWS, d), F32), pltpu.VMEM((2, MOE_ROWS, d), F32), pltpu.VMEM((MOE_ROWS, d), BF16),
            pltpu.VMEM((2, d, ff), F32), pltpu.VMEM((2, d, ff), F32), pltpu.VMEM((2, ff, d), F32),
            pltpu.VMEM((d, ff), BF16), pltpu.VMEM((d, ff), BF16), pltpu.VMEM((ff, d), BF16),
            pltpu.SemaphoreType.DMA((2,)), pltpu.SemaphoreType.DMA((2,)), pltpu.SemaphoreType.DMA((2,)),
        ],
    )
    return pl.pallas_call(
        functools.partial(_moe_kernel, n_tok=n_tok, layer=layer, n_blocks=n_blocks),
        grid_spec=grid_spec,
        out_shape=jax.ShapeDtypeStruct((MOE_TOP_K * n_tok + 2 * MOE_ROWS, d), F32),
        compiler_params=pltpu.CompilerParams(dimension_semantics=("arbitrary",),
                                             vmem_limit_bytes=MOE_VMEM_LIMIT_BYTES),
        name="moe_experts",
    )(be, nv, first, ws, nxt, slot, slot, slot, tok, w1, w3, w2)


def _rope_tables(n_seq):
    half = HEAD_DIM // 2
    n_freq = half // 2
    t = jnp.arange(n_seq)
    row = (t // GRID_W).astype(F32)
    col = (t % GRID_W).astype(F32)
    inv_freq = ROPE_BASE ** (-jnp.arange(n_freq, dtype=F32) / n_freq)
    ang_r = row[:, None] * inv_freq[None, :]
    ang_c = col[:, None] * inv_freq[None, :]
    cos = jnp.concatenate([jnp.cos(ang_r)] * 2 + [jnp.cos(ang_c)] * 2, axis=-1)
    sin = jnp.concatenate([-jnp.sin(ang_r), jnp.sin(ang_r), -jnp.sin(ang_c), jnp.sin(ang_c)], axis=-1)
    return cos, sin


def _router_weights(w_g, b_g, w_e, b_e):
    d = w_g.shape[0]
    n = w_g.shape[1] + w_e.shape[1]
    wr = jnp.concatenate([w_g, w_e, jnp.zeros((d, LANES - n), F32)], axis=1)
    rb = jnp.concatenate([b_g, b_e, jnp.zeros((LANES - n,), F32)]).reshape(1, LANES)
    hi = wr.astype(BF16)
    lo = (wr - hi.astype(F32)).astype(BF16)
    return hi, lo, rb


def kernel(x, c, ctx, c_ctx, ada_w, ada_b, ln_g, ln_b, mix_w_in, att_sink, hg_lb, hg_norm_g, mix_w_out, pool_w_in, pool_w_grp, pool_scale, pool_w_out, rt_group_w, rt_group_b, rt_expert_w, rt_expert_b, moe_w1, moe_w3, moe_w2):
    b, n, d = x.shape
    n_ctx = ctx.shape[1]
    t = b * n
    xf = x.reshape(t, d)
    ctxf = ctx.reshape(b * n_ctx, d)

    cond = jnp.concatenate([c, c_ctx[None, :], jnp.zeros((8 - b - 1, d), F32)], axis=0)
    mod = _ada_mod(cond, ada_w, ada_b)

    def chunk(l, j, rows=slice(0, b)):
        return mod[l, rows, j * d:(j + 1) * d][:, None, :]

    w_in = mix_w_in[0].astype(BF16)
    cos, sin = _rope_tables(n)
    q_w, kv_w = ATT_HEADS * HEAD_DIM, ATT_KV_HEADS * HEAD_DIM
    n_att = q_w + 2 * kv_w
    qkv, p = _mod_matmul(xf, chunk(0, 1), chunk(0, 0), w_in, cos, sin, lambda j: j, n_att, w_in.shape[1] - n_att,
                         n_q=q_w, n_rope=q_w + kv_w, n_seq=n, tm=1024, tn=512)
    ctx_rows = slice(b, b + 1)
    kv_ctx, pc = _mod_matmul(ctxf, chunk(0, 1, ctx_rows), chunk(0, 0, ctx_rows), w_in, cos, sin,
                             lambda j: jnp.where(j < 2, j + 2, j + 4), 2 * kv_w, 3 * HG_HEADS * HG_KEY,
                             n_q=0, n_rope=0, n_seq=n, tm=b * n_ctx, tn=512)
    att = _window_attention(qkv, kv_ctx, att_sink[0], b, n, n_ctx)
    o_f = _hgrn2_scan(p, pc, hg_lb[:, :, :], b, n, n_ctx, rev=False)
    o_b = _hgrn2_scan(p, pc, hg_lb[:, :, :], b, n, n_ctx, rev=True)
    wr_hi, wr_lo, rb = _router_weights(rt_group_w[0], rt_group_b[0], rt_expert_w[0], rt_expert_b[0])
    x1, tok, eid, gate = _even_out(
        att, o_f, o_b, p, hg_norm_g[0][None, :], mix_w_out[0].astype(BF16), xf,
        chunk(0, 2), chunk(0, 4), chunk(0, 3), ln_g[0, 0][None, :], ln_b[0, 0][None, :], wr_hi, wr_lo, rb, n)
    y2 = _moe_experts(tok, eid[:, :MOE_TOP_K], moe_w1, moe_w3, moe_w2, 0)

    x2, u = _combine_call(x1, y2, gate, chunk(0, 5), ln_g[0, 1][None, :], ln_b[0, 1][None, :], n,
                          proj=(chunk(1, 1), chunk(1, 0), pool_w_in[0].astype(BF16)))
    wr_hi, wr_lo, rb = _router_weights(rt_group_w[1], rt_group_b[1], rt_expert_w[1], rt_expert_b[1])
    x3, tok, eid, gate = _pool_out(
        u, pool_w_grp[0].astype(BF16), pool_scale[0][None, :], pool_w_out[0].astype(BF16), x2,
        chunk(1, 2), chunk(1, 4), chunk(1, 3), ln_g[1, 0][None, :], ln_b[1, 0][None, :], wr_hi, wr_lo, rb, n)
    y2 = _moe_experts(tok, eid[:, :MOE_TOP_K], moe_w1, moe_w3, moe_w2, 1)
    out = _combine_call(x3, y2, gate, chunk(1, 5), ln_g[1, 1][None, :], ln_b[1, 1][None, :], n)
    return out.reshape(b, n, d)
```

```python
import functools

import jax
import jax.numpy as jnp
from jax import lax
from jax.experimental import pallas as pl
from jax.experimental.pallas import tpu as pltpu

F32 = jnp.float32
BF16 = jnp.bfloat16

LANES = 128
VMEM_LIMIT_BYTES = 56 * 1024 * 1024
MOE_VMEM_LIMIT_BYTES = 60 * 1024 * 1024

GRID_W = 64
ATT_HEADS = 8
ATT_KV_HEADS = 4
ATT_GROUP = ATT_HEADS // ATT_KV_HEADS
HEAD_DIM = 128
WINDOW = 128
ATT_BLOCK = 128
ROPE_BASE = 10000.0
HG_HEADS = 8
HG_KEY = 128
HG_CHUNK = 64
HG_SUB = 16
HG_FAST_RANGE = 80.0
NORM_EPS = 1e-6
POOL_WINDOWS = (2, 4, 8, 16)
POOL_HALO = 8
MOE_GROUPS = 4
MOE_EXPERTS_PER_GROUP = 8
N_EXPERTS = MOE_GROUPS * MOE_EXPERTS_PER_GROUP
MOE_TOP_K = 2
MOE_ROWS = 256
WEIGHT_DMA_PRIORITY = 1
LN_EPS = 1e-5
DEPTH = 2
DEEPNORM_ALPHA = (2 * DEPTH) ** 0.25


def _dot(a, b):
    return jnp.dot(a, b, preferred_element_type=F32)


def _dot_nt(a, b):
    return lax.dot_general(a, b, (((1,), (1,)), ((), ())), preferred_element_type=F32)


def _dot_tn(a, b):
    return lax.dot_general(a, b, (((0,), (0,)), ((), ())), preferred_element_type=F32)


def _sigmoid(x):
    return 1.0 / (1.0 + jnp.exp(-x))


def _silu(x):
    return x * _sigmoid(x)


def _params(*sem):
    return pltpu.CompilerParams(dimension_semantics=sem, vmem_limit_bytes=VMEM_LIMIT_BYTES)


def _layer_norm(z, g, b):
    mu = jnp.mean(z, axis=-1, keepdims=True)
    zc = z - mu
    var = jnp.mean(zc * zc, axis=-1, keepdims=True)
    return zc * lax.rsqrt(var + LN_EPS) * g + b


def _ada_kernel(s_ref, w_ref, b_ref, o_ref):
    s = _silu(s_ref[...]).astype(BF16)
    o_ref[0] = _dot(s, w_ref[0].astype(BF16)) + b_ref[0]


def _ada_mod(s, ada_w, ada_b, tn=1024):
    n_l, d, n = ada_w.shape
    return pl.pallas_call(
        _ada_kernel,
        grid=(n_l, n // tn),
        in_specs=[
            pl.BlockSpec((8, d), lambda l, j: (0, 0)),
            pl.BlockSpec((1, d, tn), lambda l, j: (l, 0, j)),
            pl.BlockSpec((1, 1, tn), lambda l, j: (l, 0, j)),
        ],
        out_specs=pl.BlockSpec((1, 8, tn), lambda l, j: (l, 0, j)),
        out_shape=jax.ShapeDtypeStruct((n_l, 8, n), F32),
        compiler_params=_params("parallel", "parallel"),
        name="ada_mod",
    )(s, ada_w, ada_b.reshape(n_l, 1, n))


def _rope(t, cos, sin_signed, first_half):
    partner = jnp.where(first_half, pltpu.roll(t, 96, 1), pltpu.roll(t, 32, 1))
    return t * cos + partner * sin_signed


def _modmm_kernel(x_ref, sc_ref, sh_ref, w_ref, cos_ref, sin_ref, oa_ref, ob_ref, xs_ref, *,
                  n_q_tiles, n_rope_tiles, n_a_tiles):
    j = pl.program_id(1)

    @pl.when(j == 0)
    def _():
        xs_ref[...] = (x_ref[...] * (1.0 + sc_ref[0]) + sh_ref[0]).astype(BF16)

    acc = _dot(xs_ref[...], w_ref[...])

    if n_rope_tiles:
        @pl.when(j < n_rope_tiles)
        def _():
            lane = lax.broadcasted_iota(jnp.int32, (1, HEAD_DIM), 1)
            first_half = (lane % 64) < 32
            scale = jnp.where(j < n_q_tiles, HEAD_DIM ** -0.5, 1.0)
            cos, sin = cos_ref[...] * scale, sin_ref[...] * scale
            for h in range(acc.shape[1] // HEAD_DIM):
                sl = slice(h * HEAD_DIM, (h + 1) * HEAD_DIM)
                oa_ref[:, sl] = _rope(acc[:, sl], cos, sin, first_half).astype(BF16)

    @pl.when((j >= n_rope_tiles) & (j < n_a_tiles))
    def _():
        oa_ref[...] = acc.astype(BF16)

    @pl.when(j >= n_a_tiles)
    def _():
        ob_ref[...] = acc


def _mod_matmul(x, sc, sh, w, cos, sin, col_map, n_a, n_b, n_q, n_rope, n_seq, tm, tn):
    m, k = x.shape
    rows_per_mod = m // sc.shape[0]
    ta, tb = n_a // tn, n_b // tn
    tab = pl.BlockSpec((tm, HEAD_DIM), lambda i, j: ((i * tm % n_seq) // tm, 0))
    return pl.pallas_call(
        functools.partial(_modmm_kernel, n_q_tiles=n_q // tn, n_rope_tiles=n_rope // tn, n_a_tiles=ta),
        grid=(m // tm, ta + tb),
        in_specs=[
            pl.BlockSpec((tm, k), lambda i, j: (i, 0)),
            pl.BlockSpec((1, 1, k), lambda i, j: (i * tm // rows_per_mod, 0, 0)),
            pl.BlockSpec((1, 1, k), lambda i, j: (i * tm // rows_per_mod, 0, 0)),
            pl.BlockSpec((k, tn), lambda i, j: (0, col_map(j))),
            tab, tab,
        ],
        out_specs=[pl.BlockSpec((tm, tn), lambda i, j: (i, jnp.minimum(j, ta - 1))),
                   pl.BlockSpec((tm, tn), lambda i, j: (i, jnp.maximum(j - ta, 0)))],
        out_shape=[jax.ShapeDtypeStruct((m, n_a), BF16), jax.ShapeDtypeStruct((m, n_b), F32)],
        scratch_shapes=[pltpu.VMEM((tm, k), BF16)],
        compiler_params=_params("parallel", "arbitrary"),
        name="mod_matmul",
    )(x, sc, sh, w, cos, sin)


def _attn_kernel(sink_ref, q_ref, kp_ref, kc_ref, kn_ref, vp_ref, vc_ref, vn_ref, kx_ref, vx_ref,
                 mp_ref, mn_ref, o_ref, *, n_blocks):
    n = pl.program_id(1)
    blk = ATT_BLOCK
    row1 = lax.broadcasted_iota(jnp.int32, (ATT_GROUP * blk, 1), 0)
    has_prev, has_next = n > 0, n < n_blocks - 1
    for h in range(ATT_KV_HEADS):
        kv = slice(h * HEAD_DIM, (h + 1) * HEAD_DIM)
        q2 = jnp.concatenate([q_ref[:, (ATT_GROUP * h + g) * HEAD_DIM:(ATT_GROUP * h + g + 1) * HEAD_DIM]
                              for g in range(ATT_GROUP)], axis=0)
        s_p = jnp.where(has_prev, _dot_nt(q2, kp_ref[:, kv]) + mp_ref[...], -jnp.inf)
        s_c = _dot_nt(q2, kc_ref[:, kv])
        s_n = jnp.where(has_next, _dot_nt(q2, kn_ref[:, kv]) + mn_ref[...], -jnp.inf)
        s_x = _dot_nt(q2, kx_ref[:, kv])
        sink = jnp.where(row1 < blk, sink_ref[ATT_GROUP * h], sink_ref[ATT_GROUP * h + 1])
        m = jnp.maximum(jnp.maximum(jnp.max(s_p, axis=-1, keepdims=True), jnp.max(s_c, axis=-1, keepdims=True)),
                        jnp.maximum(jnp.max(s_n, axis=-1, keepdims=True), jnp.max(s_x, axis=-1, keepdims=True)))
        m = jnp.maximum(m, sink)
        p_p, p_c, p_n, p_x = (jnp.exp(s - m) for s in (s_p, s_c, s_n, s_x))
        den = (jnp.sum(p_p, axis=-1, keepdims=True) + jnp.sum(p_c, axis=-1, keepdims=True)
               + jnp.sum(p_n, axis=-1, keepdims=True) + jnp.sum(p_x, axis=-1, keepdims=True)
               + jnp.exp(sink - m))
        o = (_dot(p_p.astype(BF16), vp_ref[:, kv]) + _dot(p_c.astype(BF16), vc_ref[:, kv])
             + _dot(p_n.astype(BF16), vn_ref[:, kv]) + _dot(p_x.astype(BF16), vx_ref[:, kv])) / den
        for g in range(ATT_GROUP):
            col = (ATT_GROUP * h + g) * HEAD_DIM
            o_ref[:, col:col + HEAD_DIM] = o[g * blk:(g + 1) * blk].astype(o_ref.dtype)


def _window_attention(qkv, kv_ctx, sink, batch, n_seq, n_ctx):
    assert ATT_GROUP == 2 and WINDOW == ATT_BLOCK
    nb = n_seq // ATT_BLOCK
    qw, kw = ATT_HEADS * HEAD_DIM, ATT_KV_HEADS * HEAD_DIM
    kcol, vcol = qw // kw, qw // kw + 1

    def rows(off):
        return lambda b, n, off=off: b * nb + jnp.clip(n + off, 0, nb - 1)

    kspec = [pl.BlockSpec((ATT_BLOCK, kw), lambda b, n, r=rows(o): (r(b, n), kcol)) for o in (-1, 0, 1)]
    vspec = [pl.BlockSpec((ATT_BLOCK, kw), lambda b, n, r=rows(o): (r(b, n), vcol)) for o in (-1, 0, 1)]
    r = jnp.arange(ATT_GROUP * ATT_BLOCK)[:, None] % ATT_BLOCK
    c = jnp.arange(ATT_BLOCK)[None, :]
    mask_prev = jnp.where(c >= r, 0.0, -jnp.inf).astype(F32)
    mask_next = jnp.where(c <= r, 0.0, -jnp.inf).astype(F32)
    mspec = pl.BlockSpec(mask_prev.shape, lambda b, n: (0, 0))
    return pl.pallas_call(
        functools.partial(_attn_kernel, n_blocks=nb),
        grid=(batch, nb),
        in_specs=[pl.BlockSpec(memory_space=pltpu.SMEM),
                  pl.BlockSpec((ATT_BLOCK, qw), lambda b, n: (b * nb + n, 0))]
        + kspec + vspec
        + [pl.BlockSpec((n_ctx, kw), lambda b, n: (b, 0)), pl.BlockSpec((n_ctx, kw), lambda b, n: (b, 1)),
           mspec, mspec],
        out_specs=pl.BlockSpec((ATT_BLOCK, qw), lambda b, n: (b * nb + n, 0)),
        out_shape=jax.ShapeDtypeStruct((batch * n_seq, qw), BF16),
        compiler_params=_params("parallel", "parallel"),
        name="window_attention",
    )(sink, qkv, qkv, qkv, qkv, qkv, qkv, qkv, kv_ctx, kv_ctx, mask_prev, mask_next)


def _gla_tile(zf, q_raw, v, lb, st_ref, o_ref, rev):
    c_len = zf.shape[0]
    f = lb + (1.0 - lb) * _sigmoid(zf)
    k = 1.0 - f
    g = jnp.log(f)
    ti = lax.broadcasted_iota(jnp.int32, (c_len, c_len), 0)
    si = lax.broadcasted_iota(jnp.int32, (c_len, c_len), 1)
    seen = (si >= ti) if rev else (si <= ti)
    tri = jnp.where(seen, 1.0, 0.0).astype(BF16)
    g1 = g.astype(BF16)
    r1 = g - g1.astype(F32)
    g2 = r1.astype(BF16)
    g3 = (r1 - g2.astype(F32)).astype(BF16)
    c = _dot(tri, g1) + _dot(tri, g2) + _dot(tri, g3)
    c_end = c[0:1] if rev else c[c_len - 1:c_len]
    k_end = (k * jnp.exp(c_end - c)).astype(BF16)
    dec = jnp.exp(c_end)
    vb = v.astype(BF16)

    if o_ref is not None:
        q = _silu(q_raw)
        q_in = (q * jnp.exp(c)).astype(BF16)
        in_range = jnp.min(c_end) >= -HG_FAST_RANGE

        @pl.when(in_range)
        def _():
            k_in = (k * jnp.exp(-c)).astype(BF16)
            for h in range(HG_HEADS):
                sl = slice(h * HG_KEY, (h + 1) * HG_KEY)
                sc = jnp.where(seen, _dot_nt(q_in[:, sl], k_in[:, sl]), 0.0).astype(BF16)
                o_ref[:, sl] = _dot_nt(q_in[:, sl], st_ref[h].astype(BF16)) + _dot(sc, vb[:, sl])

        @pl.when(jnp.logical_not(in_range))
        def _():
            _gla_intra_exact(q, k, v, vb, c, q_in, st_ref, o_ref, rev)

    for h in range(HG_HEADS):
        sl = slice(h * HG_KEY, (h + 1) * HG_KEY)
        st_ref[h] = st_ref[h] * dec[:, sl] + _dot_tn(vb[:, sl], k_end[:, sl])


def _gla_intra_exact(q, k, v, vb, c, q_in, st_ref, o_ref, rev):
    c_len = q.shape[0]
    pairs = []
    size = c_len // 2
    while size >= HG_SUB:
        for lo in range(0, c_len, 2 * size):
            pairs.append((lo, lo + size, lo + 2 * size))
        size //= 2
    scaled = []
    for lo, mid, hi in pairs:
        if rev:
            late, early, bnd = slice(lo, mid), slice(mid, hi), mid
        else:
            late, early, bnd = slice(mid, hi), slice(lo, mid), mid - 1
        cb = c[bnd:bnd + 1]
        q_l = (q[late] * jnp.exp(c[late] - cb)).astype(BF16)
        k_e = (k[early] * jnp.exp(cb - c[early])).astype(BF16)
        scaled.append((late, early, q_l, k_e))
    n_sub = c_len // HG_SUB
    t_idx = lax.broadcasted_iota(jnp.int32, (HG_SUB, 1), 0)
    diag = [[None] * HG_HEADS for _ in range(n_sub)]
    for b in range(n_sub):
        r0 = b * HG_SUB
        qb, cb = q[r0:r0 + HG_SUB], c[r0:r0 + HG_SUB]
        for s in range(HG_SUB):
            row = r0 + s
            ok = (t_idx <= s) if rev else (t_idx >= s)
            w = qb * k[row:row + 1] * jnp.exp(jnp.where(ok, cb - c[row:row + 1], -jnp.inf))
            for h in range(HG_HEADS):
                sl = slice(h * HG_KEY, (h + 1) * HG_KEY)
                contrib = jnp.sum(w[:, sl], axis=-1, keepdims=True) * v[row:row + 1, sl]
                diag[b][h] = contrib if diag[b][h] is None else diag[b][h] + contrib

    for h in range(HG_HEADS):
        sl = slice(h * HG_KEY, (h + 1) * HG_KEY)
        o_h = _dot_nt(q_in[:, sl], st_ref[h].astype(BF16))
        parts = [diag[b][h] for b in range(n_sub)]
        for late, early, q_l, k_e in scaled:
            sc = _dot_nt(q_l[:, sl], k_e[:, sl]).astype(BF16)
            add = _dot(sc, vb[early, sl])
            b0 = late.start // HG_SUB
            for j in range((late.stop - late.start) // HG_SUB):
                parts[b0 + j] = parts[b0 + j] + add[j * HG_SUB:(j + 1) * HG_SUB]
        o_ref[:, sl] = o_h + jnp.concatenate(parts, axis=0)


def _gla_kernel(lb_ref, zf_ref, q_ref, v_ref, zfc_ref, vc_ref, o_ref, st_ref, *, rev, n_ctx_chunks):
    s = pl.program_id(1)

    @pl.when(s == 0)
    def _():
        st_ref[...] = jnp.zeros_like(st_ref)

    x = lb_ref[...]
    e = jnp.exp(x - jnp.max(x, axis=0, keepdims=True))
    lb = e[0:1] / jnp.sum(e, axis=0, keepdims=True)

    @pl.when(s < n_ctx_chunks)
    def _():
        _gla_tile(zfc_ref[...], None, vc_ref[...], lb, st_ref, None, rev)

    @pl.when(s >= n_ctx_chunks)
    def _():
        _gla_tile(zf_ref[...], q_ref[...], v_ref[...], lb, st_ref, o_ref, rev)


def _hgrn2_scan(p, pc, hg_lb, batch, n_seq, n_ctx, rev):
    hk = HG_HEADS * HG_KEY
    nc, ncc = n_seq // HG_CHUNK, n_ctx // HG_CHUNK
    d = 1 if rev else 0

    def lat(b, s):
        j = jnp.maximum(s - ncc, 0)
        return b * nc + (nc - 1 - j if rev else j)

    def ctx(b, s):
        j = jnp.minimum(s, ncc - 1)
        return b * ncc + (ncc - 1 - j if rev else j)

    return pl.pallas_call(
        functools.partial(_gla_kernel, rev=rev, n_ctx_chunks=ncc),
        grid=(batch, ncc + nc),
        in_specs=[
            pl.BlockSpec((None, hg_lb.shape[1], hk), lambda b, s: (d, 0, 0)),
            pl.BlockSpec((HG_CHUNK, hk), lambda b, s: (lat(b, s), 1 + d)),
            pl.BlockSpec((HG_CHUNK, hk), lambda b, s: (lat(b, s), 0)),
            pl.BlockSpec((HG_CHUNK, hk), lambda b, s: (lat(b, s), 3)),
            pl.BlockSpec((HG_CHUNK, hk), lambda b, s: (ctx(b, s), d)),
            pl.BlockSpec((HG_CHUNK, hk), lambda b, s: (ctx(b, s), 2)),
        ],
        out_specs=pl.BlockSpec((HG_CHUNK, hk), lambda b, s: (lat(b, s), 0)),
        out_shape=jax.ShapeDtypeStruct((batch * n_seq, hk), F32),
        scratch_shapes=[pltpu.VMEM((HG_HEADS, HG_KEY, HG_KEY), F32)],
        compiler_params=_params("parallel", "arbitrary"),
        name="hgrn2_bwd" if rev else "hgrn2_fwd",
    )(hg_lb, p, p, p, pc, pc)


def _route(tok, wr_hi, wr_lo, rb):
    t_hi = tok.astype(BF16)
    t_lo = (tok - t_hi.astype(F32)).astype(BF16)
    lg = _dot(t_hi, wr_hi) + _dot(t_hi, wr_lo) + _dot(t_lo, wr_hi) + rb
    lane = lax.broadcasted_iota(jnp.int32, lg.shape, 1)
    lane_f = lane.astype(F32)
    ninf = -jnp.inf
    gl = jnp.where(lane < MOE_GROUPS, lg, ninf)
    gmax = jnp.max(gl, axis=-1, keepdims=True)
    g_idx = jnp.min(jnp.where(gl == gmax, lane_f, float(LANES)), axis=-1, keepdims=True)
    g_val = 1.0 / jnp.sum(jnp.exp(gl - gmax), axis=-1, keepdims=True)
    e_lane = lane_f - float(MOE_GROUPS)
    lo = g_idx * float(MOE_EXPERTS_PER_GROUP)
    in_grp = (e_lane >= lo) & (e_lane < lo + float(MOE_EXPERTS_PER_GROUP))
    el = jnp.where(in_grp, lg, ninf)
    l1 = jnp.max(el, axis=-1, keepdims=True)
    i1 = jnp.min(jnp.where(el == l1, e_lane, float(LANES)), axis=-1, keepdims=True)
    el2 = jnp.where(e_lane == i1, ninf, el)
    l2 = jnp.max(el2, axis=-1, keepdims=True)
    i2 = jnp.min(jnp.where(el2 == l2, e_lane, float(LANES)), axis=-1, keepdims=True)
    r = jnp.exp(l2 - l1)
    w1 = g_val / (1.0 + r)
    w2 = w1 * r
    eid = jnp.where(lane == 0, i1, jnp.where(lane == 1, i2, 0.0)).astype(jnp.int32)
    gate = jnp.where(lane == 0, w1, jnp.where(lane == 1, w2, 0.0))
    return eid, gate


def _post_mix(y, x_ref, g1_ref, sc2_ref, sh2_ref, lng_ref, lnb_ref, wrh_ref, wrl_ref, rb_ref,
              x1_ref, tok_ref, eid_ref, gate_ref):
    x1 = _layer_norm(DEEPNORM_ALPHA * x_ref[...] + g1_ref[0] * y, lng_ref[...], lnb_ref[...])
    x1_ref[...] = x1
    tok = x1 * (1.0 + sc2_ref[0]) + sh2_ref[0]
    tok_ref[...] = tok
    eid, gate = _route(tok, wrh_ref[...], wrl_ref[...], rb_ref[...])
    eid_ref[...] = eid
    gate_ref[...] = gate


def _even_out_kernel(att_ref, of_ref, ob_ref, gt_ref, ng_ref, wo_ref, *rest):
    o = of_ref[...] + ob_ref[...]
    pieces = []
    for h in range(HG_HEADS):
        oh = o[:, h * HG_KEY:(h + 1) * HG_KEY]
        pieces.append(oh * lax.rsqrt(jnp.mean(oh * oh, axis=-1, keepdims=True) + NORM_EPS))
    hg = (jnp.concatenate(pieces, axis=-1) * ng_ref[...] * _silu(gt_ref[...])).astype(BF16)
    n_att = att_ref.shape[1]
    y = _dot(att_ref[...], wo_ref[:n_att, :]) + _dot(hg, wo_ref[n_att:, :])
    _post_mix(y, *rest)


def _post_specs(d, tm, rows_per_batch):
    def bmap(i):
        return (i * tm // rows_per_batch, 0, 0)

    row = pl.BlockSpec((tm, d), lambda i: (i, 0))
    mod = pl.BlockSpec((1, 1, d), bmap)
    vec = pl.BlockSpec((1, d), lambda i: (0, 0))
    rw = pl.BlockSpec((d, LANES), lambda i: (0, 0))
    in_specs = [row, mod, mod, mod, vec, vec, rw, rw, pl.BlockSpec((1, LANES), lambda i: (0, 0))]
    lane_blk = pl.BlockSpec((tm, LANES), lambda i: (i, 0))
    out_specs = [row, row, lane_blk, lane_blk]
    return in_specs, out_specs


def _post_out_shapes(t, d):
    return [jax.ShapeDtypeStruct((t, d), F32), jax.ShapeDtypeStruct((t, d), F32),
            jax.ShapeDtypeStruct((t, LANES), jnp.int32), jax.ShapeDtypeStruct((t, LANES), F32)]


def _even_out(att, o_f, o_b, p, norm_g, w_out, x, g1, sc2, sh2, lng, lnb, wr_hi, wr_lo, rb, rows_per_batch, tm=256):
    t, d = x.shape
    hv = o_f.shape[1]
    post_in, post_out = _post_specs(d, tm, rows_per_batch)
    return pl.pallas_call(
        _even_out_kernel,
        grid=(t // tm,),
        in_specs=[
            pl.BlockSpec((tm, att.shape[1]), lambda i: (i, 0)),
            pl.BlockSpec((tm, hv), lambda i: (i, 0)),
            pl.BlockSpec((tm, hv), lambda i: (i, 0)),
            pl.BlockSpec((tm, hv), lambda i: (i, 4)),
            pl.BlockSpec((1, hv), lambda i: (0, 0)),
            pl.BlockSpec(w_out.shape, lambda i: (0, 0)),
        ] + post_in,
        out_specs=post_out,
        out_shape=_post_out_shapes(t, d),
        compiler_params=_params("parallel"),
        name="even_out",
    )(att, o_f, o_b, p, norm_g, w_out, x, g1, sc2, sh2, lng, lnb, wr_hi, wr_lo, rb)


def _combine(x_ref, ya_ref, yb_ref, gate_ref, g2_ref, lng_ref, lnb_ref):
    gate = gate_ref[...]
    y = gate[:, 0:1] * ya_ref[...] + gate[:, 1:2] * yb_ref[...]
    return _layer_norm(DEEPNORM_ALPHA * x_ref[...] + g2_ref[0] * y, lng_ref[...], lnb_ref[...])


def _combine_proj_kernel(x_ref, ya_ref, yb_ref, gate_ref, g2_ref, lng_ref, lnb_ref, sc_ref, sh_ref, w_ref,
                         x2_ref, u_ref):
    x2 = _combine(x_ref, ya_ref, yb_ref, gate_ref, g2_ref, lng_ref, lnb_ref)
    x2_ref[...] = x2
    u_ref[...] = _dot((x2 * (1.0 + sc_ref[0]) + sh_ref[0]).astype(BF16), w_ref[...])


def _combine_kernel(x_ref, ya_ref, yb_ref, gate_ref, g2_ref, lng_ref, lnb_ref, x2_ref):
    x2_ref[...] = _combine(x_ref, ya_ref, yb_ref, gate_ref, g2_ref, lng_ref, lnb_ref)


def _combine_call(x1, y2, gate, g2, lng, lnb, rows_per_batch, proj=None, tm=256):
    t, d = x1.shape
    nt = t // tm

    def bmap(i):
        return (i * tm // rows_per_batch, 0, 0)

    row = pl.BlockSpec((tm, d), lambda i: (i, 0))
    mod = pl.BlockSpec((1, 1, d), bmap)
    vec = pl.BlockSpec((1, d), lambda i: (0, 0))
    in_specs = [row, row, pl.BlockSpec((tm, d), lambda i: (nt + i, 0)),
                pl.BlockSpec((tm, LANES), lambda i: (i, 0)), mod, vec, vec]
    args = [x1, y2, y2, gate, g2, lng, lnb]
    if proj is None:
        return pl.pallas_call(
            _combine_kernel, grid=(nt,), in_specs=in_specs, out_specs=row,
            out_shape=jax.ShapeDtypeStruct((t, d), F32),
            compiler_params=_params("parallel"), name="combine_ln")(*args)
    sc, sh, w = proj
    return pl.pallas_call(
        _combine_proj_kernel, grid=(nt,),
        in_specs=in_specs + [mod, mod, pl.BlockSpec(w.shape, lambda i: (0, 0))],
        out_specs=[row, pl.BlockSpec((tm, w.shape[1]), lambda i: (i, 0))],
        out_shape=[jax.ShapeDtypeStruct((t, d), F32), jax.ShapeDtypeStruct((t, w.shape[1]), F32)],
        compiler_params=_params("parallel"), name="combine_ln_proj")(*args, sc, sh, w)


def _pool_out_kernel(up_ref, uc_ref, un_ref, wg_ref, ps_ref, wo_ref, *rest, n_seq):
    tm, d = uc_ref.shape
    n_grp = len(POOL_WINDOWS)
    ch = d // n_grp
    halo = POOL_HALO
    pos0 = (pl.program_id(0) * tm) % n_seq
    e_pos = pos0 - halo + lax.broadcasted_iota(jnp.int32, (tm + 2 * halo, 1), 0)
    e_ok = (e_pos >= 0) & (e_pos < n_seq)
    t_pos = pos0 + lax.broadcasted_iota(jnp.int32, (tm, 1), 0)
    y = None
    for gi, w in enumerate(POOL_WINDOWS):
        cs = slice(gi * ch, (gi + 1) * ch)
        u = uc_ref[:, cs]
        ext = jnp.where(e_ok, jnp.concatenate([up_ref[:, cs], u, un_ref[:, cs]], axis=0), 0.0)
        a, span = ext, 1
        while span < w:
            a = a[:a.shape[0] - span] + a[span:]
            span *= 2
        start = halo - w // 2
        win = a[start:start + tm]
        cnt = (jnp.minimum(t_pos + (w - w // 2), n_seq) - jnp.maximum(t_pos - w // 2, 0)).astype(F32)
        mixed = (win / cnt - u).astype(BF16)
        z = (_dot(mixed, wg_ref[gi]) * ps_ref[:, cs]).astype(BF16)
        part = _dot(z, wo_ref[cs, :])
        y = part if y is None else y + part
    _post_mix(y, *rest)


def _pool_out(u, w_grp, scale, w_out, x, g1, sc2, sh2, lng, lnb, wr_hi, wr_lo, rb, n_seq, tm=256):
    t, d = x.shape
    hb = tm // POOL_HALO
    n_hb = t // POOL_HALO
    post_in, post_out = _post_specs(d, tm, n_seq)
    return pl.pallas_call(
        functools.partial(_pool_out_kernel, n_seq=n_seq),
        grid=(t // tm,),
        in_specs=[
            pl.BlockSpec((POOL_HALO, d), lambda i: (jnp.maximum(i * hb - 1, 0), 0)),
            pl.BlockSpec((tm, d), lambda i: (i, 0)),
            pl.BlockSpec((POOL_HALO, d), lambda i: (jnp.minimum((i + 1) * hb, n_hb - 1), 0)),
            pl.BlockSpec(w_grp.shape, lambda i: (0, 0, 0)),
            pl.BlockSpec((1, d), lambda i: (0, 0)),
            pl.BlockSpec(w_out.shape, lambda i: (0, 0)),
        ] + post_in,
        out_specs=post_out,
        out_shape=_post_out_shapes(t, d),
        compiler_params=_params("parallel"),
        name="pool_out",
    )(u, u, u, w_grp, scale, w_out, x, g1, sc2, sh2, lng, lnb, wr_hi, wr_lo, rb)


def _moe_kernel(be_ref, nv_ref, first_ref, ws_ref, nxt_ref, idx_ref, idxn_ref, idxp_ref, tok_hbm, w1_hbm, w3_hbm, w2_hbm,
                y_hbm, xbuf, ybuf, xb_ref, wf1, wf3, wf2, w1b, w3b, w2b, gsem, ssem, wsem, *, n_tok, layer, n_blocks):
    i = pl.program_id(0)
    used = nv_ref[jnp.minimum(i, n_blocks - 1)] > 0
    used = used & (i < n_blocks)
    prev_used = (i > 0) & (nv_ref[jnp.maximum(i - 1, 0)] > 0)
    xs = i % 2

    def weight_copies(e, ws):
        return (pltpu.make_async_copy(w1_hbm.at[layer, e], wf1.at[ws], wsem.at[ws]),
                pltpu.make_async_copy(w3_hbm.at[layer, e], wf3.at[ws], wsem.at[ws]),
                pltpu.make_async_copy(w2_hbm.at[layer, e], wf2.at[ws], wsem.at[ws]))

    def gather_start(idx, slot):
        for r in range(MOE_ROWS):
            tok = idx[0, 0, r] & (n_tok - 1)
            pltpu.make_async_copy(tok_hbm.at[pl.ds(tok, 1)], xbuf.at[slot, pl.ds(r, 1)], gsem.at[slot]).start()

    def gather_wait(slot):
        pltpu.make_async_copy(tok_hbm.at[pl.ds(0, MOE_ROWS)], xbuf.at[slot], gsem.at[slot]).wait()

    def scatter_start(idx, slot):
        for r in range(MOE_ROWS):
            pltpu.make_async_copy(ybuf.at[slot, pl.ds(r, 1)], y_hbm.at[pl.ds(idx[0, 0, r], 1)],
                                  ssem.at[slot]).start(priority=r % 2)

    def scatter_wait(slot):
        pltpu.make_async_copy(ybuf.at[slot], y_hbm.at[pl.ds(0, MOE_ROWS)], ssem.at[slot]).wait()

    @pl.when(i == 0)
    def _():
        xbuf[...] = jnp.zeros_like(xbuf)
        ybuf[...] = jnp.zeros_like(ybuf)
        spare0 = pltpu.make_async_copy(
            ybuf.at[0], y_hbm.at[pl.ds(MOE_TOP_K * n_tok, MOE_ROWS)], ssem.at[0])
        spare0.start()
        for cp in weight_copies(be_ref[0], 0):
            cp.start(priority=WEIGHT_DMA_PRIORITY)
        gather_start(idx_ref, 0)

    @pl.when(used)
    def _():
        ws = ws_ref[i]

        @pl.when(first_ref[i] == 1)
        def _():
            for cp in weight_copies(be_ref[i], ws):
                cp.wait()
            nxt = nxt_ref[i]

            @pl.when(nxt >= 0)
            def _():
                for cp in weight_copies(nxt, 1 - ws):
                    cp.start(priority=WEIGHT_DMA_PRIORITY)

            w1b[...] = wf1[ws].astype(BF16)
            w3b[...] = wf3[ws].astype(BF16)
            w2b[...] = wf2[ws].astype(BF16)

        gather_wait(xs)
        xb_ref[...] = xbuf[xs].astype(BF16)
        gather_start(idxn_ref, 1 - xs)
        scatter_start(idxp_ref, 1 - xs)
        xb = xb_ref[...]
        h = (_silu(_dot(xb, w1b[...])) * _dot(xb, w3b[...])).astype(BF16)
        y = _dot(h, w2b[...])
        scatter_wait(xs)
        ybuf[xs] = y

    @pl.when(jnp.logical_not(used) & prev_used)
    def _():
        gather_wait(xs)
        scatter_start(idxp_ref, 1 - xs)
        scatter_wait(1 - xs)
        scatter_wait(xs)


def _moe_dispatch(eid, n_blocks):
    n_tok = eid.shape[0]
    n_assign = n_tok * MOE_TOP_K
    flat_e = eid.reshape(n_assign)
    order = jnp.argsort(flat_e, stable=True).astype(jnp.int32)
    e_sorted = flat_e[order]
    counts = jnp.bincount(flat_e, length=N_EXPERTS).astype(jnp.int32)
    padded = (counts + MOE_ROWS - 1) // MOE_ROWS * MOE_ROWS
    start = jnp.cumsum(counts) - counts
    pad_end = jnp.cumsum(padded)
    pad_start = pad_end - padded
    dest = pad_start[e_sorted] + jnp.arange(n_assign, dtype=jnp.int32) - start[e_sorted]
    dst_row = (order % MOE_TOP_K) * n_tok + order // MOE_TOP_K
    blk = jnp.arange(-1, n_blocks + 1, dtype=jnp.int32)[:, None]
    row = jnp.arange(MOE_ROWS, dtype=jnp.int32)[None, :]
    spare = (MOE_TOP_K * n_tok + (blk % 2) * MOE_ROWS + row).reshape(-1)
    slot = spare.at[MOE_ROWS + dest].set(dst_row)
    blk0 = jnp.arange(n_blocks, dtype=jnp.int32) * MOE_ROWS
    be = jnp.minimum(jnp.searchsorted(pad_end, blk0, side='right'), N_EXPERTS - 1).astype(jnp.int32)
    nv = jnp.clip(counts[be] - (blk0 - pad_start[be]), 0, MOE_ROWS).astype(jnp.int32)
    nv = jnp.where(blk0 < pad_end[-1], nv, 0)
    ar = jnp.arange(n_blocks, dtype=jnp.int32)
    first = ((nv > 0) & ((ar == 0) | (be != jnp.roll(be, 1)))).astype(jnp.int32)
    ws = ((jnp.cumsum(first) - 1) % 2).astype(jnp.int32)
    later_first = lax.cummin(jnp.where(first == 1, ar, n_blocks), axis=0, reverse=True)
    nxt_idx = jnp.concatenate([later_first[1:], jnp.full((1,), n_blocks, jnp.int32)])
    nxt = jnp.where(nxt_idx < n_blocks, be[jnp.minimum(nxt_idx, n_blocks - 1)], -1).astype(jnp.int32)
    return slot.reshape(n_blocks + 2, 1, MOE_ROWS), be, nv, first, ws, nxt


def _moe_experts(tok, eid, w1, w3, w2, layer):
    n_tok, d = tok.shape
    assert n_tok & (n_tok - 1) == 0
    ff = w1.shape[3]
    n_assign = n_tok * MOE_TOP_K
    n_blocks = -(-(n_assign + N_EXPERTS * (MOE_ROWS - 1)) // MOE_ROWS)
    slot, be, nv, first, ws, nxt = _moe_dispatch(eid, n_blocks)
    grid_spec = pltpu.PrefetchScalarGridSpec(
        num_scalar_prefetch=5,
        grid=(n_blocks + 1,),
        in_specs=[
            pl.BlockSpec((1, 1, MOE_ROWS), lambda i, *_: (i + 1, 0, 0), memory_space=pltpu.SMEM),
            pl.BlockSpec((1, 1, MOE_ROWS), lambda i, *_: (jnp.minimum(i + 2, n_blocks + 1), 0, 0),
                         memory_space=pltpu.SMEM),
            pl.BlockSpec((1, 1, MOE_ROWS), lambda i, *_: (i, 0, 0), memory_space=pltpu.SMEM),
            pl.BlockSpec(memory_space=pl.ANY),
            pl.BlockSpec(memory_space=pl.ANY),
            pl.BlockSpec(memory_space=pl.ANY),
            pl.BlockSpec(memory_space=pl.ANY),
        ],
        out_specs=pl.BlockSpec(memory_space=pl.ANY),
        scratch_shapes=[
            pltpu.VMEM((2, MOE_ROWS, d), F32), pltpu.VMEM((2, MOE_ROWS, d), F32), pltpu.VMEM((MOE_ROWS, d), BF16),
            pltpu.VMEM((2, d, ff), F32), pltpu.VMEM((2, d, ff), F32), pltpu.VMEM((2, ff, d), F32),
            pltpu.VMEM((d, ff), BF16), pltpu.VMEM((d, ff), BF16), pltpu.VMEM((ff, d), BF16),
            pltpu.SemaphoreType.DMA((2,)), pltpu.SemaphoreType.DMA((2,)), pltpu.SemaphoreType.DMA((2,)),
        ],
    )
    return pl.pallas_call(
        functools.partial(_moe_kernel, n_tok=n_tok, layer=layer, n_blocks=n_blocks),
        grid_spec=grid_spec,
        out_shape=jax.ShapeDtypeStruct((MOE_TOP_K * n_tok + 2 * MOE_ROWS, d), F32),
        compiler_params=pltpu.CompilerParams(dimension_semantics=("arbitrary",),
                                             vmem_limit_bytes=MOE_VMEM_LIMIT_BYTES),
        name="moe_experts",
    )(be, nv, first, ws, nxt, slot, slot, slot, tok, w1, w3, w2)


def _rope_tables(n_seq):
    half = HEAD_DIM // 2
    n_freq = half // 2
    t = jnp.arange(n_seq)
    row = (t // GRID_W).astype(F32)
    col = (t % GRID_W).astype(F32)
    inv_freq = ROPE_BASE ** (-jnp.arange(n_freq, dtype=F32) / n_freq)
    ang_r = row[:, None] * inv_freq[None, :]
    ang_c = col[:, None] * inv_freq[None, :]
    cos = jnp.concatenate([jnp.cos(ang_r)] * 2 + [jnp.cos(ang_c)] * 2, axis=-1)
    sin = jnp.concatenate([-jnp.sin(ang_r), jnp.sin(ang_r), -jnp.sin(ang_c), jnp.sin(ang_c)], axis=-1)
    return cos, sin


def _router_weights(w_g, b_g, w_e, b_e):
    d = w_g.shape[0]
    n = w_g.shape[1] + w_e.shape[1]
    wr = jnp.concatenate([w_g, w_e, jnp.zeros((d, LANES - n), F32)], axis=1)
    rb = jnp.concatenate([b_g, b_e, jnp.zeros((LANES - n,), F32)]).reshape(1, LANES)
    hi = wr.astype(BF16)
    lo = (wr - hi.astype(F32)).astype(BF16)
    return hi, lo, rb


def kernel(x, c, ctx, c_ctx, ada_w, ada_b, ln_g, ln_b, mix_w_in, att_sink, hg_lb, hg_norm_g, mix_w_out, pool_w_in, pool_w_grp, pool_scale, pool_w_out, rt_group_w, rt_group_b, rt_expert_w, rt_expert_b, moe_w1, moe_w3, moe_w2):
    b, n, d = x.shape
    n_ctx = ctx.shape[1]
    t = b * n
    xf = x.reshape(t, d)
    ctxf = ctx.reshape(b * n_ctx, d)

    cond = jnp.concatenate([c, c_ctx[None, :], jnp.zeros((8 - b - 1, d), F32)], axis=0)
    mod = _ada_mod(cond, ada_w, ada_b)

    def chunk(l, j, rows=slice(0, b)):
        return mod[l, rows, j * d:(j + 1) * d][:, None, :]

    w_in = mix_w_in[0].astype(BF16)
    cos, sin = _rope_tables(n)
    q_w, kv_w = ATT_HEADS * HEAD_DIM, ATT_KV_HEADS * HEAD_DIM
    n_att = q_w + 2 * kv_w
    qkv, p = _mod_matmul(xf, chunk(0, 1), chunk(0, 0), w_in, cos, sin, lambda j: j, n_att, w_in.shape[1] - n_att,
                         n_q=q_w, n_rope=q_w + kv_w, n_seq=n, tm=1024, tn=512)
    ctx_rows = slice(b, b + 1)
    kv_ctx, pc = _mod_matmul(ctxf, chunk(0, 1, ctx_rows), chunk(0, 0, ctx_rows), w_in, cos, sin,
                             lambda j: jnp.where(j < 2, j + 2, j + 4), 2 * kv_w, 3 * HG_HEADS * HG_KEY,
                             n_q=0, n_rope=0, n_seq=n, tm=b * n_ctx, tn=512)
    att = _window_attention(qkv, kv_ctx, att_sink[0], b, n, n_ctx)
    o_f = _hgrn2_scan(p, pc, hg_lb[:, :, :], b, n, n_ctx, rev=False)
    o_b = _hgrn2_scan(p, pc, hg_lb[:, :, :], b, n, n_ctx, rev=True)
    wr_hi, wr_lo, rb = _router_weights(rt_group_w[0], rt_group_b[0], rt_expert_w[0], rt_expert_b[0])
    x1, tok, eid, gate = _even_out(
        att, o_f, o_b, p, hg_norm_g[0][None, :], mix_w_out[0].astype(BF16), xf,
        chunk(0, 2), chunk(0, 4), chunk(0, 3), ln_g[0, 0][None, :], ln_b[0, 0][None, :], wr_hi, wr_lo, rb, n)
    y2 = _moe_experts(tok, eid[:, :MOE_TOP_K], moe_w1, moe_w3, moe_w2, 0)

    x2, u = _combine_call(x1, y2, gate, chunk(0, 5), ln_g[0, 1][None, :], ln_b[0, 1][None, :], n,
                          proj=(chunk(1, 1), chunk(1, 0), pool_w_in[0].astype(BF16)))
    wr_hi, wr_lo, rb = _router_weights(rt_group_w[1], rt_group_b[1], rt_expert_w[1], rt_expert_b[1])
    x3, tok, eid, gate = _pool_out(
        u, pool_w_grp[0].astype(BF16), pool_scale[0][None, :], pool_w_out[0].astype(BF16), x2,
        chunk(1, 2), chunk(1, 4), chunk(1, 3), ln_g[1, 0][None, :], ln_b[1, 0][None, :], wr_hi, wr_lo, rb, n)
    y2 = _moe_experts(tok, eid[:, :MOE_TOP_K], moe_w1, moe_w3, moe_w2, 1)
    out = _combine_call(x3, y2, gate, chunk(1, 5), ln_g[1, 1][None, :], ln_b[1, 1][None, :], n)
    return out.reshape(b, n, d)
```

```python
import functools

import jax
import jax.numpy as jnp
from jax import lax
from jax.experimental import pallas as pl
from jax.experimental.pallas import tpu as pltpu

F32 = jnp.float32
BF16 = jnp.bfloat16

LANES = 128
VMEM_LIMIT_BYTES = 56 * 1024 * 1024
MOE_VMEM_LIMIT_BYTES = 60 * 1024 * 1024

GRID_W = 64
ATT_HEADS = 8
ATT_KV_HEADS = 4
ATT_GROUP = ATT_HEADS // ATT_KV_HEADS
HEAD_DIM = 128
WINDOW = 128
ATT_BLOCK = 128
ROPE_BASE = 10000.0
HG_HEADS = 8
HG_KEY = 128
HG_CHUNK = 64
HG_SUB = 16
HG_FAST_RANGE = 80.0
NORM_EPS = 1e-6
POOL_WINDOWS = (2, 4, 8, 16)
POOL_HALO = 8
MOE_GROUPS = 4
MOE_EXPERTS_PER_GROUP = 8
N_EXPERTS = MOE_GROUPS * MOE_EXPERTS_PER_GROUP
MOE_TOP_K = 2
MOE_ROWS = 256
WEIGHT_DMA_PRIORITY = 1
LN_EPS = 1e-5
DEPTH = 2
DEEPNORM_ALPHA = (2 * DEPTH) ** 0.25


def _dot(a, b):
    return jnp.dot(a, b, preferred_element_type=F32)


def _dot_nt(a, b):
    return lax.dot_general(a, b, (((1,), (1,)), ((), ())), preferred_element_type=F32)


def _dot_tn(a, b):
    return lax.dot_general(a, b, (((0,), (0,)), ((), ())), preferred_element_type=F32)


def _sigmoid(x):
    return 1.0 / (1.0 + jnp.exp(-x))


def _silu(x):
    return x * _sigmoid(x)


def _params(*sem):
    return pltpu.CompilerParams(dimension_semantics=sem, vmem_limit_bytes=VMEM_LIMIT_BYTES)


def _tiles_from_rows(x):
    n = x.shape[1] // LANES
    return jnp.swapaxes(jnp.stack([x[:, s * LANES:(s + 1) * LANES] for s in range(n)], axis=0), 0, 1)


def _rows_from_tiles(x3):
    xt = jnp.swapaxes(x3, 0, 1)
    return jnp.concatenate([xt[s] for s in range(xt.shape[0])], axis=-1)


def _layer_norm(z, g, b):
    mu = jnp.mean(z, axis=-1, keepdims=True)
    zc = z - mu
    var = jnp.mean(zc * zc, axis=-1, keepdims=True)
    return zc * lax.rsqrt(var + LN_EPS) * g + b


def _ada_kernel(s_ref, w_ref, b_ref, o_ref):
    s = _silu(s_ref[...]).astype(BF16)
    o_ref[0] = _dot(s, w_ref[0].astype(BF16)) + b_ref[0]


def _ada_mod(s, ada_w, ada_b, tn=1024):
    n_l, d, n = ada_w.shape
    return pl.pallas_call(
        _ada_kernel,
        grid=(n_l, n // tn),
        in_specs=[
            pl.BlockSpec((8, d), lambda l, j: (0, 0)),
            pl.BlockSpec((1, d, tn), lambda l, j: (l, 0, j)),
            pl.BlockSpec((1, 1, tn), lambda l, j: (l, 0, j)),
        ],
        out_specs=pl.BlockSpec((1, 8, tn), lambda l, j: (l, 0, j)),
        out_shape=jax.ShapeDtypeStruct((n_l, 8, n), F32),
        compiler_params=_params("parallel", "parallel"),
        name="ada_mod",
    )(s, ada_w, ada_b.reshape(n_l, 1, n))


def _rope(t, cos, sin_signed, first_half):
    partner = jnp.where(first_half, pltpu.roll(t, 96, 1), pltpu.roll(t, 32, 1))
    return t * cos + partner * sin_signed


def _modmm_kernel(x_ref, sc_ref, sh_ref, w_ref, cos_ref, sin_ref, oa_ref, ob_ref, xs_ref, *,
                  n_q_tiles, n_rope_tiles, n_a_tiles):
    j = pl.program_id(1)

    @pl.when(j == 0)
    def _():
        xs_ref[...] = (x_ref[...] * (1.0 + sc_ref[0]) + sh_ref[0]).astype(BF16)

    acc = _dot(xs_ref[...], w_ref[...])

    if n_rope_tiles:
        @pl.when(j < n_rope_tiles)
        def _():
            lane = lax.broadcasted_iota(jnp.int32, (1, HEAD_DIM), 1)
            first_half = (lane % 64) < 32
            scale = jnp.where(j < n_q_tiles, HEAD_DIM ** -0.5, 1.0)
            cos, sin = cos_ref[...] * scale, sin_ref[...] * scale
            for h in range(acc.shape[1] // HEAD_DIM):
                sl = slice(h * HEAD_DIM, (h + 1) * HEAD_DIM)
                oa_ref[:, sl] = _rope(acc[:, sl], cos, sin, first_half).astype(BF16)

    @pl.when((j >= n_rope_tiles) & (j < n_a_tiles))
    def _():
        oa_ref[...] = acc.astype(BF16)

    @pl.when(j >= n_a_tiles)
    def _():
        ob_ref[...] = acc


def _mod_matmul(x, sc, sh, w, cos, sin, col_map, n_a, n_b, n_q, n_rope, n_seq, tm, tn):
    m, k = x.shape
    rows_per_mod = m // sc.shape[0]
    ta, tb = n_a // tn, n_b // tn
    tab = pl.BlockSpec((tm, HEAD_DIM), lambda i, j: ((i * tm % n_seq) // tm, 0))
    return pl.pallas_call(
        functools.partial(_modmm_kernel, n_q_tiles=n_q // tn, n_rope_tiles=n_rope // tn, n_a_tiles=ta),
        grid=(m // tm, ta + tb),
        in_specs=[
            pl.BlockSpec((tm, k), lambda i, j: (i, 0)),
            pl.BlockSpec((1, 1, k), lambda i, j: (i * tm // rows_per_mod, 0, 0)),
            pl.BlockSpec((1, 1, k), lambda i, j: (i * tm // rows_per_mod, 0, 0)),
            pl.BlockSpec((k, tn), lambda i, j: (0, col_map(j))),
            tab, tab,
        ],
        out_specs=[pl.BlockSpec((tm, tn), lambda i, j: (i, jnp.minimum(j, ta - 1))),
                   pl.BlockSpec((tm, tn), lambda i, j: (i, jnp.maximum(j - ta, 0)))],
        out_shape=[jax.ShapeDtypeStruct((m, n_a), BF16), jax.ShapeDtypeStruct((m, n_b), F32)],
        scratch_shapes=[pltpu.VMEM((tm, k), BF16)],
        compiler_params=_params("parallel", "arbitrary"),
        name="mod_matmul",
    )(x, sc, sh, w, cos, sin)


def _attn_kernel(sink_ref, q_ref, kp_ref, kc_ref, kn_ref, vp_ref, vc_ref, vn_ref, kx_ref, vx_ref,
                 mp_ref, mn_ref, o_ref, *, n_blocks):
    n = pl.program_id(1)
    blk = ATT_BLOCK
    row1 = lax.broadcasted_iota(jnp.int32, (ATT_GROUP * blk, 1), 0)
    has_prev, has_next = n > 0, n < n_blocks - 1
    for h in range(ATT_KV_HEADS):
        kv = slice(h * HEAD_DIM, (h + 1) * HEAD_DIM)
        q2 = jnp.concatenate([q_ref[:, (ATT_GROUP * h + g) * HEAD_DIM:(ATT_GROUP * h + g + 1) * HEAD_DIM]
                              for g in range(ATT_GROUP)], axis=0)
        s_p = jnp.where(has_prev, _dot_nt(q2, kp_ref[:, kv]) + mp_ref[...], -jnp.inf)
        s_c = _dot_nt(q2, kc_ref[:, kv])
        s_n = jnp.where(has_next, _dot_nt(q2, kn_ref[:, kv]) + mn_ref[...], -jnp.inf)
        s_x = _dot_nt(q2, kx_ref[:, kv])
        sink = jnp.where(row1 < blk, sink_ref[ATT_GROUP * h], sink_ref[ATT_GROUP * h + 1])
        m = jnp.maximum(jnp.maximum(jnp.max(s_p, axis=-1, keepdims=True), jnp.max(s_c, axis=-1, keepdims=True)),
                        jnp.maximum(jnp.max(s_n, axis=-1, keepdims=True), jnp.max(s_x, axis=-1, keepdims=True)))
        m = jnp.maximum(m, sink)
        p_p, p_c, p_n, p_x = (jnp.exp(s - m) for s in (s_p, s_c, s_n, s_x))
        den = (jnp.sum(p_p, axis=-1, keepdims=True) + jnp.sum(p_c, axis=-1, keepdims=True)
               + jnp.sum(p_n, axis=-1, keepdims=True) + jnp.sum(p_x, axis=-1, keepdims=True)
               + jnp.exp(sink - m))
        o = (_dot(p_p.astype(BF16), vp_ref[:, kv]) + _dot(p_c.astype(BF16), vc_ref[:, kv])
             + _dot(p_n.astype(BF16), vn_ref[:, kv]) + _dot(p_x.astype(BF16), vx_ref[:, kv])) / den
        for g in range(ATT_GROUP):
            col = (ATT_GROUP * h + g) * HEAD_DIM
            o_ref[:, col:col + HEAD_DIM] = o[g * blk:(g + 1) * blk].astype(o_ref.dtype)


def _window_attention(qkv, kv_ctx, sink, batch, n_seq, n_ctx):
    assert ATT_GROUP == 2 and WINDOW == ATT_BLOCK
    nb = n_seq // ATT_BLOCK
    qw, kw = ATT_HEADS * HEAD_DIM, ATT_KV_HEADS * HEAD_DIM
    kcol, vcol = qw // kw, qw // kw + 1

    def rows(off):
        return lambda b, n, off=off: b * nb + jnp.clip(n + off, 0, nb - 1)

    kspec = [pl.BlockSpec((ATT_BLOCK, kw), lambda b, n, r=rows(o): (r(b, n), kcol)) for o in (-1, 0, 1)]
    vspec = [pl.BlockSpec((ATT_BLOCK, kw), lambda b, n, r=rows(o): (r(b, n), vcol)) for o in (-1, 0, 1)]
    r = jnp.arange(ATT_GROUP * ATT_BLOCK)[:, None] % ATT_BLOCK
    c = jnp.arange(ATT_BLOCK)[None, :]
    mask_prev = jnp.where(c >= r, 0.0, -jnp.inf).astype(F32)
    mask_next = jnp.where(c <= r, 0.0, -jnp.inf).astype(F32)
    mspec = pl.BlockSpec(mask_prev.shape, lambda b, n: (0, 0))
    return pl.pallas_call(
        functools.partial(_attn_kernel, n_blocks=nb),
        grid=(batch, nb),
        in_specs=[pl.BlockSpec(memory_space=pltpu.SMEM),
                  pl.BlockSpec((ATT_BLOCK, qw), lambda b, n: (b * nb + n, 0))]
        + kspec + vspec
        + [pl.BlockSpec((n_ctx, kw), lambda b, n: (b, 0)), pl.BlockSpec((n_ctx, kw), lambda b, n: (b, 1)),
           mspec, mspec],
        out_specs=pl.BlockSpec((ATT_BLOCK, qw), lambda b, n: (b * nb + n, 0)),
        out_shape=jax.ShapeDtypeStruct((batch * n_seq, qw), BF16),
        compiler_params=_params("parallel", "parallel"),
        name="window_attention",
    )(sink, qkv, qkv, qkv, qkv, qkv, qkv, qkv, kv_ctx, kv_ctx, mask_prev, mask_next)


def _gla_tile(zf, q_raw, v, lb, st_ref, o_ref, rev):
    c_len = zf.shape[0]
    f = lb + (1.0 - lb) * _sigmoid(zf)
    k = 1.0 - f
    g = jnp.log(f)
    ti = lax.broadcasted_iota(jnp.int32, (c_len, c_len), 0)
    si = lax.broadcasted_iota(jnp.int32, (c_len, c_len), 1)
    seen = (si >= ti) if rev else (si <= ti)
    tri = jnp.where(seen, 1.0, 0.0).astype(BF16)
    g1 = g.astype(BF16)
    r1 = g - g1.astype(F32)
    g2 = r1.astype(BF16)
    g3 = (r1 - g2.astype(F32)).astype(BF16)
    c = _dot(tri, g1) + _dot(tri, g2) + _dot(tri, g3)
    c_end = c[0:1] if rev else c[c_len - 1:c_len]
    k_end = (k * jnp.exp(c_end - c)).astype(BF16)
    dec = jnp.exp(c_end)
    vb = v.astype(BF16)

    if o_ref is not None:
        q = _silu(q_raw)
        q_in = (q * jnp.exp(c)).astype(BF16)
        in_range = jnp.min(c_end) >= -HG_FAST_RANGE

        @pl.when(in_range)
        def _():
            k_in = (k * jnp.exp(-c)).astype(BF16)
            for h in range(HG_HEADS):
                sl = slice(h * HG_KEY, (h + 1) * HG_KEY)
                sc = jnp.where(seen, _dot_nt(q_in[:, sl], k_in[:, sl]), 0.0).astype(BF16)
                o_ref[:, sl] = _dot_nt(q_in[:, sl], st_ref[h].astype(BF16)) + _dot(sc, vb[:, sl])

        @pl.when(jnp.logical_not(in_range))
        def _():
            _gla_intra_exact(q, k, v, vb, c, q_in, st_ref, o_ref, rev)

    for h in range(HG_HEADS):
        sl = slice(h * HG_KEY, (h + 1) * HG_KEY)
        st_ref[h] = st_ref[h] * dec[:, sl] + _dot_tn(vb[:, sl], k_end[:, sl])


def _gla_intra_exact(q, k, v, vb, c, q_in, st_ref, o_ref, rev):
    c_len = q.shape[0]
    pairs = []
    size = c_len // 2
    while size >= HG_SUB:
        for lo in range(0, c_len, 2 * size):
            pairs.append((lo, lo + size, lo + 2 * size))
        size //= 2
    scaled = []
    for lo, mid, hi in pairs:
        if rev:
            late, early, bnd = slice(lo, mid), slice(mid, hi), mid
        else:
            late, early, bnd = slice(mid, hi), slice(lo, mid), mid - 1
        cb = c[bnd:bnd + 1]
        q_l = (q[late] * jnp.exp(c[late] - cb)).astype(BF16)
        k_e = (k[early] * jnp.exp(cb - c[early])).astype(BF16)
        scaled.append((late, early, q_l, k_e))
    n_sub = c_len // HG_SUB
    t_idx = lax.broadcasted_iota(jnp.int32, (HG_SUB, 1), 0)
    diag = [[None] * HG_HEADS for _ in range(n_sub)]
    for b in range(n_sub):
        r0 = b * HG_SUB
        qb, cb = q[r0:r0 + HG_SUB], c[r0:r0 + HG_SUB]
        for s in range(HG_SUB):
            row = r0 + s
            ok = (t_idx <= s) if rev else (t_idx >= s)
            w = qb * k[row:row + 1] * jnp.exp(jnp.where(ok, cb - c[row:row + 1], -jnp.inf))
            for h in range(HG_HEADS):
                sl = slice(h * HG_KEY, (h + 1) * HG_KEY)
                contrib = jnp.sum(w[:, sl], axis=-1, keepdims=True) * v[row:row + 1, sl]
                diag[b][h] = contrib if diag[b][h] is None else diag[b][h] + contrib

    for h in range(HG_HEADS):
        sl = slice(h * HG_KEY, (h + 1) * HG_KEY)
        o_h = _dot_nt(q_in[:, sl], st_ref[h].astype(BF16))
        parts = [diag[b][h] for b in range(n_sub)]
        for late, early, q_l, k_e in scaled:
            sc = _dot_nt(q_l[:, sl], k_e[:, sl]).astype(BF16)
            add = _dot(sc, vb[early, sl])
            b0 = late.start // HG_SUB
            for j in range((late.stop - late.start) // HG_SUB):
                parts[b0 + j] = parts[b0 + j] + add[j * HG_SUB:(j + 1) * HG_SUB]
        o_ref[:, sl] = o_h + jnp.concatenate(parts, axis=0)


def _gla_kernel(lb_ref, zf_ref, q_ref, v_ref, zfc_ref, vc_ref, o_ref, st_ref, *, rev, n_ctx_chunks):
    s = pl.program_id(1)

    @pl.when(s == 0)
    def _():
        st_ref[...] = jnp.zeros_like(st_ref)

    x = lb_ref[...]
    e = jnp.exp(x - jnp.max(x, axis=0, keepdims=True))
    lb = e[0:1] / jnp.sum(e, axis=0, keepdims=True)

    @pl.when(s < n_ctx_chunks)
    def _():
        _gla_tile(zfc_ref[...], None, vc_ref[...], lb, st_ref, None, rev)

    @pl.when(s >= n_ctx_chunks)
    def _():
        _gla_tile(zf_ref[...], q_ref[...], v_ref[...], lb, st_ref, o_ref, rev)


def _hgrn2_scan(p, pc, hg_lb, batch, n_seq, n_ctx, rev):
    hk = HG_HEADS * HG_KEY
    nc, ncc = n_seq // HG_CHUNK, n_ctx // HG_CHUNK
    d = 1 if rev else 0

    def lat(b, s):
        j = jnp.maximum(s - ncc, 0)
        return b * nc + (nc - 1 - j if rev else j)

    def ctx(b, s):
        j = jnp.minimum(s, ncc - 1)
        return b * ncc + (ncc - 1 - j if rev else j)

    return pl.pallas_call(
        functools.partial(_gla_kernel, rev=rev, n_ctx_chunks=ncc),
        grid=(batch, ncc + nc),
        in_specs=[
            pl.BlockSpec((None, hg_lb.shape[1], hk), lambda b, s: (d, 0, 0)),
            pl.BlockSpec((HG_CHUNK, hk), lambda b, s: (lat(b, s), 1 + d)),
            pl.BlockSpec((HG_CHUNK, hk), lambda b, s: (lat(b, s), 0)),
            pl.BlockSpec((HG_CHUNK, hk), lambda b, s: (lat(b, s), 3)),
            pl.BlockSpec((HG_CHUNK, hk), lambda b, s: (ctx(b, s), d)),
            pl.BlockSpec((HG_CHUNK, hk), lambda b, s: (ctx(b, s), 2)),
        ],
        out_specs=pl.BlockSpec((HG_CHUNK, hk), lambda b, s: (lat(b, s), 0)),
        out_shape=jax.ShapeDtypeStruct((batch * n_seq, hk), F32),
        scratch_shapes=[pltpu.VMEM((HG_HEADS, HG_KEY, HG_KEY), F32)],
        compiler_params=_params("parallel", "arbitrary"),
        name="hgrn2_bwd" if rev else "hgrn2_fwd",
    )(hg_lb, p, p, p, pc, pc)


def _route(tok, wr_hi, wr_lo, rb):
    t_hi = tok.astype(BF16)
    t_lo = (tok - t_hi.astype(F32)).astype(BF16)
    lg = _dot(t_hi, wr_hi) + _dot(t_hi, wr_lo) + _dot(t_lo, wr_hi) + rb
    lane = lax.broadcasted_iota(jnp.int32, lg.shape, 1)
    lane_f = lane.astype(F32)
    ninf = -jnp.inf
    gl = jnp.where(lane < MOE_GROUPS, lg, ninf)
    gmax = jnp.max(gl, axis=-1, keepdims=True)
    g_idx = jnp.min(jnp.where(gl == gmax, lane_f, float(LANES)), axis=-1, keepdims=True)
    g_val = 1.0 / jnp.sum(jnp.exp(gl - gmax), axis=-1, keepdims=True)
    e_lane = lane_f - float(MOE_GROUPS)
    lo = g_idx * float(MOE_EXPERTS_PER_GROUP)
    in_grp = (e_lane >= lo) & (e_lane < lo + float(MOE_EXPERTS_PER_GROUP))
    el = jnp.where(in_grp, lg, ninf)
    l1 = jnp.max(el, axis=-1, keepdims=True)
    i1 = jnp.min(jnp.where(el == l1, e_lane, float(LANES)), axis=-1, keepdims=True)
    el2 = jnp.where(e_lane == i1, ninf, el)
    l2 = jnp.max(el2, axis=-1, keepdims=True)
    i2 = jnp.min(jnp.where(el2 == l2, e_lane, float(LANES)), axis=-1, keepdims=True)
    r = jnp.exp(l2 - l1)
    w1 = g_val / (1.0 + r)
    w2 = w1 * r
    eid = jnp.where(lane == 0, i1, jnp.where(lane == 1, i2, 0.0)).astype(jnp.int32)
    gate = jnp.where(lane == 0, w1, jnp.where(lane == 1, w2, 0.0))
    return eid, gate


def _post_mix(y, x_ref, g1_ref, sc2_ref, sh2_ref, lng_ref, lnb_ref, wrh_ref, wrl_ref, rb_ref,
              x1_ref, tok_ref, eid_ref, gate_ref):
    x1 = _layer_norm(DEEPNORM_ALPHA * x_ref[...] + g1_ref[0] * y, lng_ref[...], lnb_ref[...])
    x1_ref[...] = x1
    tok = x1 * (1.0 + sc2_ref[0]) + sh2_ref[0]
    tok_ref[...] = _tiles_from_rows(tok).astype(BF16)
    eid, gate = _route(tok, wrh_ref[...], wrl_ref[...], rb_ref[...])
    eid_ref[...] = eid
    gate_ref[...] = gate


def _even_out_kernel(att_ref, of_ref, ob_ref, gt_ref, ng_ref, wo_ref, *rest):
    o = of_ref[...] + ob_ref[...]
    pieces = []
    for h in range(HG_HEADS):
        oh = o[:, h * HG_KEY:(h + 1) * HG_KEY]
        pieces.append(oh * lax.rsqrt(jnp.mean(oh * oh, axis=-1, keepdims=True) + NORM_EPS))
    hg = (jnp.concatenate(pieces, axis=-1) * ng_ref[...] * _silu(gt_ref[...])).astype(BF16)
    n_att = att_ref.shape[1]
    y = _dot(att_ref[...], wo_ref[:n_att, :]) + _dot(hg, wo_ref[n_att:, :])
    _post_mix(y, *rest)


def _post_specs(d, tm, rows_per_batch):
    def bmap(i):
        return (i * tm // rows_per_batch, 0, 0)

    row = pl.BlockSpec((tm, d), lambda i: (i, 0))
    mod = pl.BlockSpec((1, 1, d), bmap)
    vec = pl.BlockSpec((1, d), lambda i: (0, 0))
    rw = pl.BlockSpec((d, LANES), lambda i: (0, 0))
    in_specs = [row, mod, mod, mod, vec, vec, rw, rw, pl.BlockSpec((1, LANES), lambda i: (0, 0))]
    lane_blk = pl.BlockSpec((tm, LANES), lambda i: (i, 0))
    tiles = pl.BlockSpec((tm, d // LANES, LANES), lambda i: (i, 0, 0))
    out_specs = [row, tiles, lane_blk, lane_blk]
    return in_specs, out_specs


def _post_out_shapes(t, d):
    return [jax.ShapeDtypeStruct((t, d), F32), jax.ShapeDtypeStruct((t, d // LANES, LANES), BF16),
            jax.ShapeDtypeStruct((t, LANES), jnp.int32), jax.ShapeDtypeStruct((t, LANES), F32)]


def _even_out(att, o_f, o_b, p, norm_g, w_out, x, g1, sc2, sh2, lng, lnb, wr_hi, wr_lo, rb, rows_per_batch, tm=256):
    t, d = x.shape
    hv = o_f.shape[1]
    post_in, post_out = _post_specs(d, tm, rows_per_batch)
    return pl.pallas_call(
        _even_out_kernel,
        grid=(t // tm,),
        in_specs=[
            pl.BlockSpec((tm, att.shape[1]), lambda i: (i, 0)),
            pl.BlockSpec((tm, hv), lambda i: (i, 0)),
            pl.BlockSpec((tm, hv), lambda i: (i, 0)),
            pl.BlockSpec((tm, hv), lambda i: (i, 4)),
            pl.BlockSpec((1, hv), lambda i: (0, 0)),
            pl.BlockSpec(w_out.shape, lambda i: (0, 0)),
        ] + post_in,
        out_specs=post_out,
        out_shape=_post_out_shapes(t, d),
        compiler_params=_params("parallel"),
        name="even_out",
    )(att, o_f, o_b, p, norm_g, w_out, x, g1, sc2, sh2, lng, lnb, wr_hi, wr_lo, rb)


def _combine(x_ref, ya_ref, yb_ref, gate_ref, g2_ref, lng_ref, lnb_ref):
    gate = gate_ref[...]
    y = gate[:, 0:1] * _rows_from_tiles(ya_ref[...]) + gate[:, 1:2] * _rows_from_tiles(yb_ref[...])
    return _layer_norm(DEEPNORM_ALPHA * x_ref[...] + g2_ref[0] * y, lng_ref[...], lnb_ref[...])


def _combine_proj_kernel(x_ref, ya_ref, yb_ref, gate_ref, g2_ref, lng_ref, lnb_ref, sc_ref, sh_ref, w_ref,
                         x2_ref, u_ref):
    x2 = _combine(x_ref, ya_ref, yb_ref, gate_ref, g2_ref, lng_ref, lnb_ref)
    x2_ref[...] = x2
    u_ref[...] = _dot((x2 * (1.0 + sc_ref[0]) + sh_ref[0]).astype(BF16), w_ref[...])


def _combine_kernel(x_ref, ya_ref, yb_ref, gate_ref, g2_ref, lng_ref, lnb_ref, x2_ref):
    x2_ref[...] = _combine(x_ref, ya_ref, yb_ref, gate_ref, g2_ref, lng_ref, lnb_ref)


def _combine_call(x1, y2, gate, g2, lng, lnb, rows_per_batch, proj=None, tm=256):
    t, d = x1.shape
    nt = t // tm

    def bmap(i):
        return (i * tm // rows_per_batch, 0, 0)

    row = pl.BlockSpec((tm, d), lambda i: (i, 0))
    mod = pl.BlockSpec((1, 1, d), bmap)
    vec = pl.BlockSpec((1, d), lambda i: (0, 0))
    in_specs = [row, pl.BlockSpec((tm, d // LANES, LANES), lambda i: (i, 0, 0)),
                pl.BlockSpec((tm, d // LANES, LANES), lambda i: (nt + i, 0, 0)),
                pl.BlockSpec((tm, LANES), lambda i: (i, 0)), mod, vec, vec]
    args = [x1, y2, y2, gate, g2, lng, lnb]
    if proj is None:
        return pl.pallas_call(
            _combine_kernel, grid=(nt,), in_specs=in_specs, out_specs=row,
            out_shape=jax.ShapeDtypeStruct((t, d), F32),
            compiler_params=_params("parallel"), name="combine_ln")(*args)
    sc, sh, w = proj
    return pl.pallas_call(
        _combine_proj_kernel, grid=(nt,),
        in_specs=in_specs + [mod, mod, pl.BlockSpec(w.shape, lambda i: (0, 0))],
        out_specs=[row, pl.BlockSpec((tm, w.shape[1]), lambda i: (i, 0))],
        out_shape=[jax.ShapeDtypeStruct((t, d), F32), jax.ShapeDtypeStruct((t, w.shape[1]), F32)],
        compiler_params=_params("parallel"), name="combine_ln_proj")(*args, sc, sh, w)


def _pool_out_kernel(up_ref, uc_ref, un_ref, wg_ref, ps_ref, wo_ref, *rest, n_seq):
    tm, d = uc_ref.shape
    n_grp = len(POOL_WINDOWS)
    ch = d // n_grp
    halo = POOL_HALO
    pos0 = (pl.program_id(0) * tm) % n_seq
    e_pos = pos0 - halo + lax.broadcasted_iota(jnp.int32, (tm + 2 * halo, 1), 0)
    e_ok = (e_pos >= 0) & (e_pos < n_seq)
    t_pos = pos0 + lax.broadcasted_iota(jnp.int32, (tm, 1), 0)
    y = None
    for gi, w in enumerate(POOL_WINDOWS):
        cs = slice(gi * ch, (gi + 1) * ch)
        u = uc_ref[:, cs]
        ext = jnp.where(e_ok, jnp.concatenate([up_ref[:, cs], u, un_ref[:, cs]], axis=0), 0.0)
        a, span = ext, 1
        while span < w:
            a = a[:a.shape[0] - span] + a[span:]
            span *= 2
        start = halo - w // 2
        win = a[start:start + tm]
        cnt = (jnp.minimum(t_pos + (w - w // 2), n_seq) - jnp.maximum(t_pos - w // 2, 0)).astype(F32)
        mixed = (win / cnt - u).astype(BF16)
        z = (_dot(mixed, wg_ref[gi]) * ps_ref[:, cs]).astype(BF16)
        part = _dot(z, wo_ref[cs, :])
        y = part if y is None else y + part
    _post_mix(y, *rest)


def _pool_out(u, w_grp, scale, w_out, x, g1, sc2, sh2, lng, lnb, wr_hi, wr_lo, rb, n_seq, tm=256):
    t, d = x.shape
    hb = tm // POOL_HALO
    n_hb = t // POOL_HALO
    post_in, post_out = _post_specs(d, tm, n_seq)
    return pl.pallas_call(
        functools.partial(_pool_out_kernel, n_seq=n_seq),
        grid=(t // tm,),
        in_specs=[
            pl.BlockSpec((POOL_HALO, d), lambda i: (jnp.maximum(i * hb - 1, 0), 0)),
            pl.BlockSpec((tm, d), lambda i: (i, 0)),
            pl.BlockSpec((POOL_HALO, d), lambda i: (jnp.minimum((i + 1) * hb, n_hb - 1), 0)),
            pl.BlockSpec(w_grp.shape, lambda i: (0, 0, 0)),
            pl.BlockSpec((1, d), lambda i: (0, 0)),
            pl.BlockSpec(w_out.shape, lambda i: (0, 0)),
        ] + post_in,
        out_specs=post_out,
        out_shape=_post_out_shapes(t, d),
        compiler_params=_params("parallel"),
        name="pool_out",
    )(u, u, u, w_grp, scale, w_out, x, g1, sc2, sh2, lng, lnb, wr_hi, wr_lo, rb)


def _moe_kernel(be_ref, nv_ref, first_ref, ws_ref, nxt_ref, idx_ref, idxn_ref, idxp_ref, tok_hbm, w1_hbm, w3_hbm, w2_hbm,
                y_hbm, xbuf, ybuf, xb_ref, wf1, wf3, wf2, w1b, w3b, w2b, gsem, ssem, wsem, *, n_tok, layer, n_blocks):
    i = pl.program_id(0)
    used = nv_ref[jnp.minimum(i, n_blocks - 1)] > 0
    used = used & (i < n_blocks)
    prev_used = (i > 0) & (nv_ref[jnp.maximum(i - 1, 0)] > 0)
    xs = i % 2

    def weight_copies(e, ws):
        return (pltpu.make_async_copy(w1_hbm.at[layer, e], wf1.at[ws], wsem.at[ws]),
                pltpu.make_async_copy(w3_hbm.at[layer, e], wf3.at[ws], wsem.at[ws]),
                pltpu.make_async_copy(w2_hbm.at[layer, e], wf2.at[ws], wsem.at[ws]))

    def gather_start(idx, slot):
        for r in range(MOE_ROWS):
            tok = idx[0, 0, r] & (n_tok - 1)
            pltpu.make_async_copy(tok_hbm.at[tok], xbuf.at[slot, r], gsem.at[slot]).start()

    def gather_wait(slot):
        pltpu.make_async_copy(tok_hbm.at[pl.ds(0, MOE_ROWS)], xbuf.at[slot], gsem.at[slot]).wait()

    def scatter_start(idx, slot):
        for r in range(MOE_ROWS):
            pltpu.make_async_copy(ybuf.at[slot, r], y_hbm.at[idx[0, 0, r]], ssem.at[slot]).start(priority=r % 2)

    def scatter_wait(slot):
        pltpu.make_async_copy(ybuf.at[slot], y_hbm.at[pl.ds(0, MOE_ROWS)], ssem.at[slot]).wait()

    @pl.when(i == 0)
    def _():
        xbuf[...] = jnp.zeros_like(xbuf)
        ybuf[...] = jnp.zeros_like(ybuf)
        spare0 = pltpu.make_async_copy(
            ybuf.at[0], y_hbm.at[pl.ds(MOE_TOP_K * n_tok, MOE_ROWS)], ssem.at[0])
        spare0.start()
        for cp in weight_copies(be_ref[0], 0):
            cp.start(priority=WEIGHT_DMA_PRIORITY)
        gather_start(idx_ref, 0)

    @pl.when(used)
    def _():
        ws = ws_ref[i]

        @pl.when(first_ref[i] == 1)
        def _():
            for cp in weight_copies(be_ref[i], ws):
                cp.wait()
            nxt = nxt_ref[i]

            @pl.when(nxt >= 0)
            def _():
                for cp in weight_copies(nxt, 1 - ws):
                    cp.start(priority=WEIGHT_DMA_PRIORITY)

            w1b[...] = wf1[ws].astype(BF16)
            w3b[...] = wf3[ws].astype(BF16)
            w2b[...] = wf2[ws].astype(BF16)

        gather_wait(xs)
        xb_ref[...] = _rows_from_tiles(xbuf[xs].astype(F32)).astype(BF16)
        gather_start(idxn_ref, 1 - xs)
        scatter_start(idxp_ref, 1 - xs)
        xb = xb_ref[...]
        h = (_silu(_dot(xb, w1b[...])) * _dot(xb, w3b[...])).astype(BF16)
        y = _tiles_from_rows(_dot(h, w2b[...]))
        scatter_wait(xs)
        ybuf[xs] = y

    @pl.when(jnp.logical_not(used) & prev_used)
    def _():
        gather_wait(xs)
        scatter_start(idxp_ref, 1 - xs)
        scatter_wait(1 - xs)
        scatter_wait(xs)


def _moe_dispatch(eid, n_blocks):
    n_tok = eid.shape[0]
    n_assign = n_tok * MOE_TOP_K
    flat_e = eid.reshape(n_assign)
    order = jnp.argsort(flat_e, stable=True).astype(jnp.int32)
    e_sorted = flat_e[order]
    counts = jnp.bincount(flat_e, length=N_EXPERTS).astype(jnp.int32)
    padded = (counts + MOE_ROWS - 1) // MOE_ROWS * MOE_ROWS
    start = jnp.cumsum(counts) - counts
    pad_end = jnp.cumsum(padded)
    pad_start = pad_end - padded
    dest = pad_start[e_sorted] + jnp.arange(n_assign, dtype=jnp.int32) - start[e_sorted]
    dst_row = (order % MOE_TOP_K) * n_tok + order // MOE_TOP_K
    blk = jnp.arange(-1, n_blocks + 1, dtype=jnp.int32)[:, None]
    row = jnp.arange(MOE_ROWS, dtype=jnp.int32)[None, :]
    spare = (MOE_TOP_K * n_tok + (blk % 2) * MOE_ROWS + row).reshape(-1)
    slot = spare.at[MOE_ROWS + dest].set(dst_row)
    blk0 = jnp.arange(n_blocks, dtype=jnp.int32) * MOE_ROWS
    be = jnp.minimum(jnp.searchsorted(pad_end, blk0, side='right'), N_EXPERTS - 1).astype(jnp.int32)
    nv = jnp.clip(counts[be] - (blk0 - pad_start[be]), 0, MOE_ROWS).astype(jnp.int32)
    nv = jnp.where(blk0 < pad_end[-1], nv, 0)
    ar = jnp.arange(n_blocks, dtype=jnp.int32)
    first = ((nv > 0) & ((ar == 0) | (be != jnp.roll(be, 1)))).astype(jnp.int32)
    ws = ((jnp.cumsum(first) - 1) % 2).astype(jnp.int32)
    later_first = lax.cummin(jnp.where(first == 1, ar, n_blocks), axis=0, reverse=True)
    nxt_idx = jnp.concatenate([later_first[1:], jnp.full((1,), n_blocks, jnp.int32)])
    nxt = jnp.where(nxt_idx < n_blocks, be[jnp.minimum(nxt_idx, n_blocks - 1)], -1).astype(jnp.int32)
    return slot.reshape(n_blocks + 2, 1, MOE_ROWS), be, nv, first, ws, nxt


def _moe_experts(tok, eid, w1, w3, w2, layer):
    n_tok, n_sub, _ = tok.shape
    d = n_sub * LANES
    assert n_tok & (n_tok - 1) == 0
    ff = w1.shape[3]
    n_assign = n_tok * MOE_TOP_K
    n_blocks = -(-(n_assign + N_EXPERTS * (MOE_ROWS - 1)) // MOE_ROWS)
    slot, be, nv, first, ws, nxt = _moe_dispatch(eid, n_blocks)
    grid_spec = pltpu.PrefetchScalarGridSpec(
        num_scalar_prefetch=5,
        grid=(n_blocks + 1,),
        in_specs=[
            pl.BlockSpec((1, 1, MOE_ROWS), lambda i, *_: (i + 1, 0, 0), memory_space=pltpu.SMEM),
            pl.BlockSpec((1, 1, MOE_ROWS), lambda i, *_: (jnp.minimum(i + 2, n_blocks + 1), 0, 0),
                         memory_space=pltpu.SMEM),
            pl.BlockSpec((1, 1, MOE_ROWS), lambda i, *_: (i, 0, 0), memory_space=pltpu.SMEM),
            pl.BlockSpec(memory_space=pl.ANY),
            pl.BlockSpec(memory_space=pl.ANY),
            pl.BlockSpec(memory_space=pl.ANY),
            pl.BlockSpec(memory_space=pl.ANY),
        ],
        out_specs=pl.BlockSpec(memory_space=pl.ANY),
        scratch_shapes=[
            pltpu.VMEM((2, MOE_ROWS, n_sub, LANES), BF16), pltpu.VMEM((2, MOE_ROWS, n_sub, LANES), F32),
            pltpu.VMEM((MOE_ROWS, d), BF16),
            pltpu.VMEM((2, d, ff), F32), pltpu.VMEM((2, d, ff), F32), pltpu.VMEM((2, ff, d), F32),
            pltpu.VMEM((d, ff), BF16), pltpu.VMEM((d, ff), BF16), pltpu.VMEM((ff, d), BF16),
            pltpu.SemaphoreType.DMA((2,)), pltpu.SemaphoreType.DMA((2,)), pltpu.SemaphoreType.DMA((2,)),
        ],
    )
    return pl.pallas_call(
        functools.partial(_moe_kernel, n_tok=n_tok, layer=layer, n_blocks=n_blocks),
        grid_spec=grid_spec,
        out_shape=jax.ShapeDtypeStruct((MOE_TOP_K * n_tok + 2 * MOE_ROWS, n_sub, LANES), F32),
        compiler_params=pltpu.CompilerParams(dimension_semantics=("arbitrary",),
                                             vmem_limit_bytes=MOE_VMEM_LIMIT_BYTES),
        name="moe_experts",
    )(be, nv, first, ws, nxt, slot, slot, slot, tok, w1, w3, w2)


def _rope_tables(n_seq):
    half = HEAD_DIM // 2
    n_freq = half // 2
    t = jnp.arange(n_seq)
    row = (t // GRID_W).astype(F32)
    col = (t % GRID_W).astype(F32)
    inv_freq = ROPE_BASE ** (-jnp.arange(n_freq, dtype=F32) / n_freq)
    ang_r = row[:, None] * inv_freq[None, :]
    ang_c = col[:, None] * inv_freq[None, :]
    cos = jnp.concatenate([jnp.cos(ang_r)] * 2 + [jnp.cos(ang_c)] * 2, axis=-1)
    sin = jnp.concatenate([-jnp.sin(ang_r), jnp.sin(ang_r), -jnp.sin(ang_c), jnp.sin(ang_c)], axis=-1)
    return cos, sin


def _router_weights(w_g, b_g, w_e, b_e):
    d = w_g.shape[0]
    n = w_g.shape[1] + w_e.shape[1]
    wr = jnp.concatenate([w_g, w_e, jnp.zeros((d, LANES - n), F32)], axis=1)
    rb = jnp.concatenate([b_g, b_e, jnp.zeros((LANES - n,), F32)]).reshape(1, LANES)
    hi = wr.astype(BF16)
    lo = (wr - hi.astype(F32)).astype(BF16)
    return hi, lo, rb


def kernel(x, c, ctx, c_ctx, ada_w, ada_b, ln_g, ln_b, mix_w_in, att_sink, hg_lb, hg_norm_g, mix_w_out, pool_w_in, pool_w_grp, pool_scale, pool_w_out, rt_group_w, rt_group_b, rt_expert_w, rt_expert_b, moe_w1, moe_w3, moe_w2):
    b, n, d = x.shape
    n_ctx = ctx.shape[1]
    t = b * n
    xf = x.reshape(t, d)
    ctxf = ctx.reshape(b * n_ctx, d)

    cond = jnp.concatenate([c, c_ctx[None, :], jnp.zeros((8 - b - 1, d), F32)], axis=0)
    mod = _ada_mod(cond, ada_w, ada_b)

    def chunk(l, j, rows=slice(0, b)):
        return mod[l, rows, j * d:(j + 1) * d][:, None, :]

    w_in = mix_w_in[0].astype(BF16)
    cos, sin = _rope_tables(n)
    q_w, kv_w = ATT_HEADS * HEAD_DIM, ATT_KV_HEADS * HEAD_DIM
    n_att = q_w + 2 * kv_w
    qkv, p = _mod_matmul(xf, chunk(0, 1), chunk(0, 0), w_in, cos, sin, lambda j: j, n_att, w_in.shape[1] - n_att,
                         n_q=q_w, n_rope=q_w + kv_w, n_seq=n, tm=1024, tn=512)
    ctx_rows = slice(b, b + 1)
    kv_ctx, pc = _mod_matmul(ctxf, chunk(0, 1, ctx_rows), chunk(0, 0, ctx_rows), w_in, cos, sin,
                             lambda j: jnp.where(j < 2, j + 2, j + 4), 2 * kv_w, 3 * HG_HEADS * HG_KEY,
                             n_q=0, n_rope=0, n_seq=n, tm=b * n_ctx, tn=512)
    att = _window_attention(qkv, kv_ctx, att_sink[0], b, n, n_ctx)
    o_f = _hgrn2_scan(p, pc, hg_lb[:, :, :], b, n, n_ctx, rev=False)
    o_b = _hgrn2_scan(p, pc, hg_lb[:, :, :], b, n, n_ctx, rev=True)
    wr_hi, wr_lo, rb = _router_weights(rt_group_w[0], rt_group_b[0], rt_expert_w[0], rt_expert_b[0])
    x1, tok, eid, gate = _even_out(
        att, o_f, o_b, p, hg_norm_g[0][None, :], mix_w_out[0].astype(BF16), xf,
        chunk(0, 2), chunk(0, 4), chunk(0, 3), ln_g[0, 0][None, :], ln_b[0, 0][None, :], wr_hi, wr_lo, rb, n)
    y2 = _moe_experts(tok, eid[:, :MOE_TOP_K], moe_w1, moe_w3, moe_w2, 0)

    x2, u = _combine_call(x1, y2, gate, chunk(0, 5), ln_g[0, 1][None, :], ln_b[0, 1][None, :], n,
                          proj=(chunk(1, 1), chunk(1, 0), pool_w_in[0].astype(BF16)))
    wr_hi, wr_lo, rb = _router_weights(rt_group_w[1], rt_group_b[1], rt_expert_w[1], rt_expert_b[1])
    x3, tok, eid, gate = _pool_out(
        u, pool_w_grp[0].astype(BF16), pool_scale[0][None, :], pool_w_out[0].astype(BF16), x2,
        chunk(1, 2), chunk(1, 4), chunk(1, 3), ln_g[1, 0][None, :], ln_b[1, 0][None, :], wr_hi, wr_lo, rb, n)
    y2 = _moe_experts(tok, eid[:, :MOE_TOP_K], moe_w1, moe_w3, moe_w2, 1)
    out = _combine_call(x3, y2, gate, chunk(1, 5), ln_g[1, 1][None, :], ln_b[1, 1][None, :], n)
    return out.reshape(b, n, d)
```

```python
import functools

import jax
import jax.numpy as jnp
from jax import lax
from jax.experimental import pallas as pl
from jax.experimental.pallas import tpu as pltpu

F32 = jnp.float32
BF16 = jnp.bfloat16

LANES = 128
VMEM_LIMIT_BYTES = 56 * 1024 * 1024
MOE_VMEM_LIMIT_BYTES = 60 * 1024 * 1024

GRID_W = 64
ATT_HEADS = 8
ATT_KV_HEADS = 4
ATT_GROUP = ATT_HEADS // ATT_KV_HEADS
HEAD_DIM = 128
WINDOW = 128
ATT_BLOCK = 128
ROPE_BASE = 10000.0
HG_HEADS = 8
HG_KEY = 128
HG_CHUNK = 64
HG_SUB = 16
HG_FAST_RANGE = 80.0
NORM_EPS = 1e-6
POOL_WINDOWS = (2, 4, 8, 16)
POOL_HALO = 8
MOE_GROUPS = 4
MOE_EXPERTS_PER_GROUP = 8
N_EXPERTS = MOE_GROUPS * MOE_EXPERTS_PER_GROUP
MOE_TOP_K = 2
MOE_ROWS = 256
MOE_ID_ROWS = 8
SMEM_1D_TILE = 1024
WEIGHT_DMA_PRIORITY = 1
LN_EPS = 1e-5
DEPTH = 2
DEEPNORM_ALPHA = (2 * DEPTH) ** 0.25


def _dot(a, b):
    return jnp.dot(a, b, preferred_element_type=F32)


def _dot_nt(a, b):
    return lax.dot_general(a, b, (((1,), (1,)), ((), ())), preferred_element_type=F32)


def _dot_tn(a, b):
    return lax.dot_general(a, b, (((0,), (0,)), ((), ())), preferred_element_type=F32)


def _sigmoid(x):
    return 1.0 / (1.0 + jnp.exp(-x))


def _silu(x):
    return x * _sigmoid(x)


def _params(*sem):
    return pltpu.CompilerParams(dimension_semantics=sem, vmem_limit_bytes=VMEM_LIMIT_BYTES)


def _tiles_from_rows(x):
    n = x.shape[1] // LANES
    return jnp.swapaxes(jnp.stack([x[:, s * LANES:(s + 1) * LANES] for s in range(n)], axis=0), 0, 1)


def _rows_from_tiles(x3):
    xt = jnp.swapaxes(x3, 0, 1)
    return jnp.concatenate([xt[s] for s in range(xt.shape[0])], axis=-1)


def _layer_norm(z, g, b):
    mu = jnp.mean(z, axis=-1, keepdims=True)
    zc = z - mu
    var = jnp.mean(zc * zc, axis=-1, keepdims=True)
    return zc * lax.rsqrt(var + LN_EPS) * g + b


def _ada_kernel(s_ref, w_ref, b_ref, o_ref):
    s = _silu(s_ref[...]).astype(BF16)
    o_ref[0] = _dot(s, w_ref[0].astype(BF16)) + b_ref[0]


def _ada_mod(s, ada_w, ada_b, tn=1024):
    n_l, d, n = ada_w.shape
    return pl.pallas_call(
        _ada_kernel,
        grid=(n_l, n // tn),
        in_specs=[
            pl.BlockSpec((8, d), lambda l, j: (0, 0)),
            pl.BlockSpec((1, d, tn), lambda l, j: (l, 0, j)),
            pl.BlockSpec((1, 1, tn), lambda l, j: (l, 0, j)),
        ],
        out_specs=pl.BlockSpec((1, 8, tn), lambda l, j: (l, 0, j)),
        out_shape=jax.ShapeDtypeStruct((n_l, 8, n), F32),
        compiler_params=_params("parallel", "parallel"),
        name="ada_mod",
    )(s, ada_w, ada_b.reshape(n_l, 1, n))


def _rope(t, cos, sin_signed, first_half):
    partner = jnp.where(first_half, pltpu.roll(t, 96, 1), pltpu.roll(t, 32, 1))
    return t * cos + partner * sin_signed


def _modmm_kernel(x_ref, sc_ref, sh_ref, w_ref, cos_ref, sin_ref, oa_ref, ob_ref, xs_ref, *,
                  n_q_tiles, n_rope_tiles, n_a_tiles):
    j = pl.program_id(1)

    @pl.when(j == 0)
    def _():
        xs_ref[...] = (x_ref[...] * (1.0 + sc_ref[0]) + sh_ref[0]).astype(BF16)

    acc = _dot(xs_ref[...], w_ref[...])

    if n_rope_tiles:
        @pl.when(j < n_rope_tiles)
        def _():
            lane = lax.broadcasted_iota(jnp.int32, (1, HEAD_DIM), 1)
            first_half = (lane % 64) < 32
            scale = jnp.where(j < n_q_tiles, HEAD_DIM ** -0.5, 1.0)
            cos, sin = cos_ref[...] * scale, sin_ref[...] * scale
            for h in range(acc.shape[1] // HEAD_DIM):
                sl = slice(h * HEAD_DIM, (h + 1) * HEAD_DIM)
                oa_ref[:, sl] = _rope(acc[:, sl], cos, sin, first_half).astype(BF16)

    @pl.when((j >= n_rope_tiles) & (j < n_a_tiles))
    def _():
        oa_ref[...] = acc.astype(BF16)

    @pl.when(j >= n_a_tiles)
    def _():
        ob_ref[...] = acc


def _mod_matmul(x, sc, sh, w, cos, sin, col_map, n_a, n_b, n_q, n_rope, n_seq, tm, tn):
    m, k = x.shape
    rows_per_mod = m // sc.shape[0]
    ta, tb = n_a // tn, n_b // tn
    tab = pl.BlockSpec((tm, HEAD_DIM), lambda i, j: ((i * tm % n_seq) // tm, 0))
    return pl.pallas_call(
        functools.partial(_modmm_kernel, n_q_tiles=n_q // tn, n_rope_tiles=n_rope // tn, n_a_tiles=ta),
        grid=(m // tm, ta + tb),
        in_specs=[
            pl.BlockSpec((tm, k), lambda i, j: (i, 0)),
            pl.BlockSpec((1, 1, k), lambda i, j: (i * tm // rows_per_mod, 0, 0)),
            pl.BlockSpec((1, 1, k), lambda i, j: (i * tm // rows_per_mod, 0, 0)),
            pl.BlockSpec((k, tn), lambda i, j: (0, col_map(j))),
            tab, tab,
        ],
        out_specs=[pl.BlockSpec((tm, tn), lambda i, j: (i, jnp.minimum(j, ta - 1))),
                   pl.BlockSpec((tm, tn), lambda i, j: (i, jnp.maximum(j - ta, 0)))],
        out_shape=[jax.ShapeDtypeStruct((m, n_a), BF16), jax.ShapeDtypeStruct((m, n_b), F32)],
        scratch_shapes=[pltpu.VMEM((tm, k), BF16)],
        compiler_params=_params("parallel", "arbitrary"),
        name="mod_matmul",
    )(x, sc, sh, w, cos, sin)


def _attn_kernel(sink_ref, q_ref, kp_ref, kc_ref, kn_ref, vp_ref, vc_ref, vn_ref, kx_ref, vx_ref,
                 mp_ref, mn_ref, o_ref, *, n_blocks):
    n = pl.program_id(1)
    blk = ATT_BLOCK
    row1 = lax.broadcasted_iota(jnp.int32, (ATT_GROUP * blk, 1), 0)
    has_prev, has_next = n > 0, n < n_blocks - 1
    for h in range(ATT_KV_HEADS):
        kv = slice(h * HEAD_DIM, (h + 1) * HEAD_DIM)
        q2 = jnp.concatenate([q_ref[:, (ATT_GROUP * h + g) * HEAD_DIM:(ATT_GROUP * h + g + 1) * HEAD_DIM]
                              for g in range(ATT_GROUP)], axis=0)
        s_p = jnp.where(has_prev, _dot_nt(q2, kp_ref[:, kv]) + mp_ref[...], -jnp.inf)
        s_c = _dot_nt(q2, kc_ref[:, kv])
        s_n = jnp.where(has_next, _dot_nt(q2, kn_ref[:, kv]) + mn_ref[...], -jnp.inf)
        s_x = _dot_nt(q2, kx_ref[:, kv])
        sink = jnp.where(row1 < blk, sink_ref[ATT_GROUP * h], sink_ref[ATT_GROUP * h + 1])
        m = jnp.maximum(jnp.maximum(jnp.max(s_p, axis=-1, keepdims=True), jnp.max(s_c, axis=-1, keepdims=True)),
                        jnp.maximum(jnp.max(s_n, axis=-1, keepdims=True), jnp.max(s_x, axis=-1, keepdims=True)))
        m = jnp.maximum(m, sink)
        p_p, p_c, p_n, p_x = (jnp.exp(s - m) for s in (s_p, s_c, s_n, s_x))
        den = (jnp.sum(p_p, axis=-1, keepdims=True) + jnp.sum(p_c, axis=-1, keepdims=True)
               + jnp.sum(p_n, axis=-1, keepdims=True) + jnp.sum(p_x, axis=-1, keepdims=True)
               + jnp.exp(sink - m))
        o = (_dot(p_p.astype(BF16), vp_ref[:, kv]) + _dot(p_c.astype(BF16), vc_ref[:, kv])
             + _dot(p_n.astype(BF16), vn_ref[:, kv]) + _dot(p_x.astype(BF16), vx_ref[:, kv])) / den
        for g in range(ATT_GROUP):
            col = (ATT_GROUP * h + g) * HEAD_DIM
            o_ref[:, col:col + HEAD_DIM] = o[g * blk:(g + 1) * blk].astype(o_ref.dtype)


def _window_attention(qkv, kv_ctx, sink, batch, n_seq, n_ctx):
    assert ATT_GROUP == 2 and WINDOW == ATT_BLOCK
    nb = n_seq // ATT_BLOCK
    qw, kw = ATT_HEADS * HEAD_DIM, ATT_KV_HEADS * HEAD_DIM
    kcol, vcol = qw // kw, qw // kw + 1

    def rows(off):
        return lambda b, n, off=off: b * nb + jnp.clip(n + off, 0, nb - 1)

    kspec = [pl.BlockSpec((ATT_BLOCK, kw), lambda b, n, r=rows(o): (r(b, n), kcol)) for o in (-1, 0, 1)]
    vspec = [pl.BlockSpec((ATT_BLOCK, kw), lambda b, n, r=rows(o): (r(b, n), vcol)) for o in (-1, 0, 1)]
    r = jnp.arange(ATT_GROUP * ATT_BLOCK)[:, None] % ATT_BLOCK
    c = jnp.arange(ATT_BLOCK)[None, :]
    mask_prev = jnp.where(c >= r, 0.0, -jnp.inf).astype(F32)
    mask_next = jnp.where(c <= r, 0.0, -jnp.inf).astype(F32)
    mspec = pl.BlockSpec(mask_prev.shape, lambda b, n: (0, 0))
    return pl.pallas_call(
        functools.partial(_attn_kernel, n_blocks=nb),
        grid=(batch, nb),
        in_specs=[pl.BlockSpec(memory_space=pltpu.SMEM),
                  pl.BlockSpec((ATT_BLOCK, qw), lambda b, n: (b * nb + n, 0))]
        + kspec + vspec
        + [pl.BlockSpec((n_ctx, kw), lambda b, n: (b, 0)), pl.BlockSpec((n_ctx, kw), lambda b, n: (b, 1)),
           mspec, mspec],
        out_specs=pl.BlockSpec((ATT_BLOCK, qw), lambda b, n: (b * nb + n, 0)),
        out_shape=jax.ShapeDtypeStruct((batch * n_seq, qw), BF16),
        compiler_params=_params("parallel", "parallel"),
        name="window_attention",
    )(sink, qkv, qkv, qkv, qkv, qkv, qkv, qkv, kv_ctx, kv_ctx, mask_prev, mask_next)


def _gla_tile(zf, q_raw, v, lb, st_ref, o_ref, rev):
    c_len = zf.shape[0]
    f = lb + (1.0 - lb) * _sigmoid(zf)
    k = 1.0 - f
    g = jnp.log(f)
    ti = lax.broadcasted_iota(jnp.int32, (c_len, c_len), 0)
    si = lax.broadcasted_iota(jnp.int32, (c_len, c_len), 1)
    seen = (si >= ti) if rev else (si <= ti)
    tri = jnp.where(seen, 1.0, 0.0).astype(BF16)
    g1 = g.astype(BF16)
    r1 = g - g1.astype(F32)
    g2 = r1.astype(BF16)
    g3 = (r1 - g2.astype(F32)).astype(BF16)
    c = _dot(tri, g1) + _dot(tri, g2) + _dot(tri, g3)
    c_end = c[0:1] if rev else c[c_len - 1:c_len]
    k_end = (k * jnp.exp(c_end - c)).astype(BF16)
    dec = jnp.exp(c_end)
    vb = v.astype(BF16)

    if o_ref is not None:
        q = _silu(q_raw)
        q_in = (q * jnp.exp(c)).astype(BF16)
        in_range = jnp.min(c_end) >= -HG_FAST_RANGE

        @pl.when(in_range)
        def _():
            k_in = (k * jnp.exp(-c)).astype(BF16)
            for h in range(HG_HEADS):
                sl = slice(h * HG_KEY, (h + 1) * HG_KEY)
                sc = jnp.where(seen, _dot_nt(q_in[:, sl], k_in[:, sl]), 0.0).astype(BF16)
                o_ref[:, sl] = _dot_nt(q_in[:, sl], st_ref[h].astype(BF16)) + _dot(sc, vb[:, sl])

        @pl.when(jnp.logical_not(in_range))
        def _():
            _gla_intra_exact(q, k, v, vb, c, q_in, st_ref, o_ref, rev)

    for h in range(HG_HEADS):
        sl = slice(h * HG_KEY, (h + 1) * HG_KEY)
        st_ref[h] = st_ref[h] * dec[:, sl] + _dot_tn(vb[:, sl], k_end[:, sl])


def _gla_intra_exact(q, k, v, vb, c, q_in, st_ref, o_ref, rev):
    c_len = q.shape[0]
    pairs = []
    size = c_len // 2
    while size >= HG_SUB:
        for lo in range(0, c_len, 2 * size):
            pairs.append((lo, lo + size, lo + 2 * size))
        size //= 2
    scaled = []
    for lo, mid, hi in pairs:
        if rev:
            late, early, bnd = slice(lo, mid), slice(mid, hi), mid
        else:
            late, early, bnd = slice(mid, hi), slice(lo, mid), mid - 1
        cb = c[bnd:bnd + 1]
        q_l = (q[late] * jnp.exp(c[late] - cb)).astype(BF16)
        k_e = (k[early] * jnp.exp(cb - c[early])).astype(BF16)
        scaled.append((late, early, q_l, k_e))
    n_sub = c_len // HG_SUB
    t_idx = lax.broadcasted_iota(jnp.int32, (HG_SUB, 1), 0)
    diag = [[None] * HG_HEADS for _ in range(n_sub)]
    for b in range(n_sub):
        r0 = b * HG_SUB
        qb, cb = q[r0:r0 + HG_SUB], c[r0:r0 + HG_SUB]
        for s in range(HG_SUB):
            row = r0 + s
            ok = (t_idx <= s) if rev else (t_idx >= s)
            w = qb * k[row:row + 1] * jnp.exp(jnp.where(ok, cb - c[row:row + 1], -jnp.inf))
            for h in range(HG_HEADS):
                sl = slice(h * HG_KEY, (h + 1) * HG_KEY)
                contrib = jnp.sum(w[:, sl], axis=-1, keepdims=True) * v[row:row + 1, sl]
                diag[b][h] = contrib if diag[b][h] is None else diag[b][h] + contrib

    for h in range(HG_HEADS):
        sl = slice(h * HG_KEY, (h + 1) * HG_KEY)
        o_h = _dot_nt(q_in[:, sl], st_ref[h].astype(BF16))
        parts = [diag[b][h] for b in range(n_sub)]
        for late, early, q_l, k_e in scaled:
            sc = _dot_nt(q_l[:, sl], k_e[:, sl]).astype(BF16)
            add = _dot(sc, vb[early, sl])
            b0 = late.start // HG_SUB
            for j in range((late.stop - late.start) // HG_SUB):
                parts[b0 + j] = parts[b0 + j] + add[j * HG_SUB:(j + 1) * HG_SUB]
        o_ref[:, sl] = o_h + jnp.concatenate(parts, axis=0)


def _gla_kernel(lb_ref, zf_ref, q_ref, v_ref, zfc_ref, vc_ref, o_ref, st_ref, *, rev, n_ctx_chunks):
    s = pl.program_id(1)

    @pl.when(s == 0)
    def _():
        st_ref[...] = jnp.zeros_like(st_ref)

    x = lb_ref[...]
    e = jnp.exp(x - jnp.max(x, axis=0, keepdims=True))
    lb = e[0:1] / jnp.sum(e, axis=0, keepdims=True)

    @pl.when(s < n_ctx_chunks)
    def _():
        _gla_tile(zfc_ref[...], None, vc_ref[...], lb, st_ref, None, rev)

    @pl.when(s >= n_ctx_chunks)
    def _():
        _gla_tile(zf_ref[...], q_ref[...], v_ref[...], lb, st_ref, o_ref, rev)


def _hgrn2_scan(p, pc, hg_lb, batch, n_seq, n_ctx, rev):
    hk = HG_HEADS * HG_KEY
    nc, ncc = n_seq // HG_CHUNK, n_ctx // HG_CHUNK
    d = 1 if rev else 0

    def lat(b, s):
        j = jnp.maximum(s - ncc, 0)
        return b * nc + (nc - 1 - j if rev else j)

    def ctx(b, s):
        j = jnp.minimum(s, ncc - 1)
        return b * ncc + (ncc - 1 - j if rev else j)

    return pl.pallas_call(
        functools.partial(_gla_kernel, rev=rev, n_ctx_chunks=ncc),
        grid=(batch, ncc + nc),
        in_specs=[
            pl.BlockSpec((None, hg_lb.shape[1], hk), lambda b, s: (d, 0, 0)),
            pl.BlockSpec((HG_CHUNK, hk), lambda b, s: (lat(b, s), 1 + d)),
            pl.BlockSpec((HG_CHUNK, hk), lambda b, s: (lat(b, s), 0)),
            pl.BlockSpec((HG_CHUNK, hk), lambda b, s: (lat(b, s), 3)),
            pl.BlockSpec((HG_CHUNK, hk), lambda b, s: (ctx(b, s), d)),
            pl.BlockSpec((HG_CHUNK, hk), lambda b, s: (ctx(b, s), 2)),
        ],
        out_specs=pl.BlockSpec((HG_CHUNK, hk), lambda b, s: (lat(b, s), 0)),
        out_shape=jax.ShapeDtypeStruct((batch * n_seq, hk), F32),
        scratch_shapes=[pltpu.VMEM((HG_HEADS, HG_KEY, HG_KEY), F32)],
        compiler_params=_params("parallel", "arbitrary"),
        name="hgrn2_bwd" if rev else "hgrn2_fwd",
    )(hg_lb, p, p, p, pc, pc)


def _route(tok, wr_hi, wr_lo, rb):
    t_hi = tok.astype(BF16)
    t_lo = (tok - t_hi.astype(F32)).astype(BF16)
    lg = _dot(t_hi, wr_hi) + _dot(t_hi, wr_lo) + _dot(t_lo, wr_hi) + rb
    lane = lax.broadcasted_iota(jnp.int32, lg.shape, 1)
    lane_f = lane.astype(F32)
    ninf = -jnp.inf
    gl = jnp.where(lane < MOE_GROUPS, lg, ninf)
    gmax = jnp.max(gl, axis=-1, keepdims=True)
    g_idx = jnp.min(jnp.where(gl == gmax, lane_f, float(LANES)), axis=-1, keepdims=True)
    g_val = 1.0 / jnp.sum(jnp.exp(gl - gmax), axis=-1, keepdims=True)
    e_lane = lane_f - float(MOE_GROUPS)
    lo = g_idx * float(MOE_EXPERTS_PER_GROUP)
    in_grp = (e_lane >= lo) & (e_lane < lo + float(MOE_EXPERTS_PER_GROUP))
    el = jnp.where(in_grp, lg, ninf)
    l1 = jnp.max(el, axis=-1, keepdims=True)
    i1 = jnp.min(jnp.where(el == l1, e_lane, float(LANES)), axis=-1, keepdims=True)
    el2 = jnp.where(e_lane == i1, ninf, el)
    l2 = jnp.max(el2, axis=-1, keepdims=True)
    i2 = jnp.min(jnp.where(el2 == l2, e_lane, float(LANES)), axis=-1, keepdims=True)
    r = jnp.exp(l2 - l1)
    w1 = g_val / (1.0 + r)
    w2 = w1 * r
    eid = jnp.where(lane == 0, i1, jnp.where(lane == 1, i2, 0.0))
    gate = jnp.where(lane == 0, w1, jnp.where(lane == 1, w2, 0.0))
    eid_t = jnp.transpose(eid)[:MOE_ID_ROWS].astype(jnp.int32)
    return eid_t, gate


def _post_mix(y, x_ref, g1_ref, sc2_ref, sh2_ref, lng_ref, lnb_ref, wrh_ref, wrl_ref, rb_ref,
              x1_ref, tok_ref, eid_ref, gate_ref):
    x1 = _layer_norm(DEEPNORM_ALPHA * x_ref[...] + g1_ref[0] * y, lng_ref[...], lnb_ref[...])
    x1_ref[...] = x1
    tok = x1 * (1.0 + sc2_ref[0]) + sh2_ref[0]
    tok_ref[...] = _tiles_from_rows(tok).astype(BF16)
    eid, gate = _route(tok, wrh_ref[...], wrl_ref[...], rb_ref[...])
    eid_ref[...] = eid
    gate_ref[...] = gate


def _even_out_kernel(att_ref, of_ref, ob_ref, gt_ref, ng_ref, wo_ref, *rest):
    o = of_ref[...] + ob_ref[...]
    pieces = []
    for h in range(HG_HEADS):
        oh = o[:, h * HG_KEY:(h + 1) * HG_KEY]
        pieces.append(oh * lax.rsqrt(jnp.mean(oh * oh, axis=-1, keepdims=True) + NORM_EPS))
    hg = (jnp.concatenate(pieces, axis=-1) * ng_ref[...] * _silu(gt_ref[...])).astype(BF16)
    n_att = att_ref.shape[1]
    y = _dot(att_ref[...], wo_ref[:n_att, :]) + _dot(hg, wo_ref[n_att:, :])
    _post_mix(y, *rest)


def _post_specs(d, tm, rows_per_batch):
    def bmap(i):
        return (i * tm // rows_per_batch, 0, 0)

    row = pl.BlockSpec((tm, d), lambda i: (i, 0))
    mod = pl.BlockSpec((1, 1, d), bmap)
    vec = pl.BlockSpec((1, d), lambda i: (0, 0))
    rw = pl.BlockSpec((d, LANES), lambda i: (0, 0))
    in_specs = [row, mod, mod, mod, vec, vec, rw, rw, pl.BlockSpec((1, LANES), lambda i: (0, 0))]
    lane_blk = pl.BlockSpec((tm, LANES), lambda i: (i, 0))
    tiles = pl.BlockSpec((tm, d // LANES, LANES), lambda i: (i, 0, 0))
    out_specs = [row, tiles, pl.BlockSpec((MOE_ID_ROWS, tm), lambda i: (0, i)), lane_blk]
    return in_specs, out_specs


def _post_out_shapes(t, d):
    return [jax.ShapeDtypeStruct((t, d), F32), jax.ShapeDtypeStruct((t, d // LANES, LANES), BF16),
            jax.ShapeDtypeStruct((MOE_ID_ROWS, t), jnp.int32), jax.ShapeDtypeStruct((t, LANES), F32)]


def _even_out(att, o_f, o_b, p, norm_g, w_out, x, g1, sc2, sh2, lng, lnb, wr_hi, wr_lo, rb, rows_per_batch, tm=256):
    t, d = x.shape
    hv = o_f.shape[1]
    post_in, post_out = _post_specs(d, tm, rows_per_batch)
    return pl.pallas_call(
        _even_out_kernel,
        grid=(t // tm,),
        in_specs=[
            pl.BlockSpec((tm, att.shape[1]), lambda i: (i, 0)),
            pl.BlockSpec((tm, hv), lambda i: (i, 0)),
            pl.BlockSpec((tm, hv), lambda i: (i, 0)),
            pl.BlockSpec((tm, hv), lambda i: (i, 4)),
            pl.BlockSpec((1, hv), lambda i: (0, 0)),
            pl.BlockSpec(w_out.shape, lambda i: (0, 0)),
        ] + post_in,
        out_specs=post_out,
        out_shape=_post_out_shapes(t, d),
        compiler_params=_params("parallel"),
        name="even_out",
    )(att, o_f, o_b, p, norm_g, w_out, x, g1, sc2, sh2, lng, lnb, wr_hi, wr_lo, rb)


def _combine(x_ref, ya_ref, yb_ref, gate_ref, g2_ref, lng_ref, lnb_ref):
    gate = gate_ref[...]
    y = gate[:, 0:1] * _rows_from_tiles(ya_ref[...]) + gate[:, 1:2] * _rows_from_tiles(yb_ref[...])
    return _layer_norm(DEEPNORM_ALPHA * x_ref[...] + g2_ref[0] * y, lng_ref[...], lnb_ref[...])


def _combine_proj_kernel(x_ref, ya_ref, yb_ref, gate_ref, g2_ref, lng_ref, lnb_ref, sc_ref, sh_ref, w_ref,
                         x2_ref, u_ref):
    x2 = _combine(x_ref, ya_ref, yb_ref, gate_ref, g2_ref, lng_ref, lnb_ref)
    x2_ref[...] = x2
    u_ref[...] = _dot((x2 * (1.0 + sc_ref[0]) + sh_ref[0]).astype(BF16), w_ref[...])


def _combine_kernel(x_ref, ya_ref, yb_ref, gate_ref, g2_ref, lng_ref, lnb_ref, x2_ref):
    x2_ref[...] = _combine(x_ref, ya_ref, yb_ref, gate_ref, g2_ref, lng_ref, lnb_ref)


def _combine_call(x1, y2, gate, g2, lng, lnb, rows_per_batch, proj=None, tm=256):
    t, d = x1.shape
    nt = t // tm

    def bmap(i):
        return (i * tm // rows_per_batch, 0, 0)

    row = pl.BlockSpec((tm, d), lambda i: (i, 0))
    mod = pl.BlockSpec((1, 1, d), bmap)
    vec = pl.BlockSpec((1, d), lambda i: (0, 0))
    in_specs = [row, pl.BlockSpec((tm, d // LANES, LANES), lambda i: (i, 0, 0)),
                pl.BlockSpec((tm, d // LANES, LANES), lambda i: (nt + i, 0, 0)),
                pl.BlockSpec((tm, LANES), lambda i: (i, 0)), mod, vec, vec]
    args = [x1, y2, y2, gate, g2, lng, lnb]
    if proj is None:
        return pl.pallas_call(
            _combine_kernel, grid=(nt,), in_specs=in_specs, out_specs=row,
            out_shape=jax.ShapeDtypeStruct((t, d), F32),
            compiler_params=_params("parallel"), name="combine_ln")(*args)
    sc, sh, w = proj
    return pl.pallas_call(
        _combine_proj_kernel, grid=(nt,),
        in_specs=in_specs + [mod, mod, pl.BlockSpec(w.shape, lambda i: (0, 0))],
        out_specs=[row, pl.BlockSpec((tm, w.shape[1]), lambda i: (i, 0))],
        out_shape=[jax.ShapeDtypeStruct((t, d), F32), jax.ShapeDtypeStruct((t, w.shape[1]), F32)],
        compiler_params=_params("parallel"), name="combine_ln_proj")(*args, sc, sh, w)


def _pool_out_kernel(up_ref, uc_ref, un_ref, wg_ref, ps_ref, wo_ref, *rest, n_seq):
    tm, d = uc_ref.shape
    n_grp = len(POOL_WINDOWS)
    ch = d // n_grp
    halo = POOL_HALO
    pos0 = (pl.program_id(0) * tm) % n_seq
    e_pos = pos0 - halo + lax.broadcasted_iota(jnp.int32, (tm + 2 * halo, 1), 0)
    e_ok = (e_pos >= 0) & (e_pos < n_seq)
    t_pos = pos0 + lax.broadcasted_iota(jnp.int32, (tm, 1), 0)
    y = None
    for gi, w in enumerate(POOL_WINDOWS):
        cs = slice(gi * ch, (gi + 1) * ch)
        u = uc_ref[:, cs]
        ext = jnp.where(e_ok, jnp.concatenate([up_ref[:, cs], u, un_ref[:, cs]], axis=0), 0.0)
        a, span = ext, 1
        while span < w:
            a = a[:a.shape[0] - span] + a[span:]
            span *= 2
        start = halo - w // 2
        win = a[start:start + tm]
        cnt = (jnp.minimum(t_pos + (w - w // 2), n_seq) - jnp.maximum(t_pos - w // 2, 0)).astype(F32)
        mixed = (win / cnt - u).astype(BF16)
        z = (_dot(mixed, wg_ref[gi]) * ps_ref[:, cs]).astype(BF16)
        part = _dot(z, wo_ref[cs, :])
        y = part if y is None else y + part
    _post_mix(y, *rest)


def _pool_out(u, w_grp, scale, w_out, x, g1, sc2, sh2, lng, lnb, wr_hi, wr_lo, rb, n_seq, tm=256):
    t, d = x.shape
    hb = tm // POOL_HALO
    n_hb = t // POOL_HALO
    post_in, post_out = _post_specs(d, tm, n_seq)
    return pl.pallas_call(
        functools.partial(_pool_out_kernel, n_seq=n_seq),
        grid=(t // tm,),
        in_specs=[
            pl.BlockSpec((POOL_HALO, d), lambda i: (jnp.maximum(i * hb - 1, 0), 0)),
            pl.BlockSpec((tm, d), lambda i: (i, 0)),
            pl.BlockSpec((POOL_HALO, d), lambda i: (jnp.minimum((i + 1) * hb, n_hb - 1), 0)),
            pl.BlockSpec(w_grp.shape, lambda i: (0, 0, 0)),
            pl.BlockSpec((1, d), lambda i: (0, 0)),
            pl.BlockSpec(w_out.shape, lambda i: (0, 0)),
        ] + post_in,
        out_specs=post_out,
        out_shape=_post_out_shapes(t, d),
        compiler_params=_params("parallel"),
        name="pool_out",
    )(u, u, u, w_grp, scale, w_out, x, g1, sc2, sh2, lng, lnb, wr_hi, wr_lo, rb)


def _moe_kernel(be_ref, nv_ref, first_ref, ws_ref, nxt_ref, idx_ref, idxn_ref, idxp_ref, tok_hbm, w1_hbm, w3_hbm, w2_hbm,
                y_hbm, xbuf, ybuf, xb_ref, wf1, wf3, wf2, w1b, w3b, w2b, gsem, ssem, wsem, *, n_tok, layer, n_blocks):
    i = pl.program_id(0)
    used = nv_ref[jnp.minimum(i, n_blocks - 1)] > 0
    used = used & (i < n_blocks)
    prev_used = (i > 0) & (nv_ref[jnp.maximum(i - 1, 0)] > 0)
    xs = i % 2

    def weight_copies(e, ws):
        return (pltpu.make_async_copy(w1_hbm.at[layer, e], wf1.at[ws], wsem.at[ws]),
                pltpu.make_async_copy(w3_hbm.at[layer, e], wf3.at[ws], wsem.at[ws]),
                pltpu.make_async_copy(w2_hbm.at[layer, e], wf2.at[ws], wsem.at[ws]))

    def gather_start(idx, slot):
        for r in range(MOE_ROWS):
            tok = idx[0, 0, r] & (n_tok - 1)
            pltpu.make_async_copy(tok_hbm.at[tok], xbuf.at[slot, r], gsem.at[slot]).start()

    def gather_wait(slot):
        pltpu.make_async_copy(tok_hbm.at[pl.ds(0, MOE_ROWS)], xbuf.at[slot], gsem.at[slot]).wait()

    def scatter_start(idx, slot):
        for r in range(MOE_ROWS):
            pltpu.make_async_copy(ybuf.at[slot, r], y_hbm.at[idx[0, 0, r]], ssem.at[slot]).start(priority=r % 2)

    def scatter_wait(slot):
        pltpu.make_async_copy(ybuf.at[slot], y_hbm.at[pl.ds(0, MOE_ROWS)], ssem.at[slot]).wait()

    @pl.when(i == 0)
    def _():
        xbuf[...] = jnp.zeros_like(xbuf)
        ybuf[...] = jnp.zeros_like(ybuf)
        spare0 = pltpu.make_async_copy(
            ybuf.at[0], y_hbm.at[pl.ds(MOE_TOP_K * n_tok, MOE_ROWS)], ssem.at[0])
        spare0.start()
        for cp in weight_copies(be_ref[0], 0):
            cp.start(priority=WEIGHT_DMA_PRIORITY)
        gather_start(idx_ref, 0)

    @pl.when(used)
    def _():
        ws = ws_ref[i]

        @pl.when(first_ref[i] == 1)
        def _():
            for cp in weight_copies(be_ref[i], ws):
                cp.wait()
            nxt = nxt_ref[i]

            @pl.when(nxt >= 0)
            def _():
                for cp in weight_copies(nxt, 1 - ws):
                    cp.start(priority=WEIGHT_DMA_PRIORITY)

            w1b[...] = wf1[ws].astype(BF16)
            w3b[...] = wf3[ws].astype(BF16)
            w2b[...] = wf2[ws].astype(BF16)

        gather_wait(xs)
        xb_ref[...] = _rows_from_tiles(xbuf[xs].astype(F32)).astype(BF16)
        gather_start(idxn_ref, 1 - xs)
        scatter_start(idxp_ref, 1 - xs)
        xb = xb_ref[...]
        h = (_silu(_dot(xb, w1b[...])) * _dot(xb, w3b[...])).astype(BF16)
        y = _tiles_from_rows(_dot(h, w2b[...]))
        scatter_wait(xs)
        ybuf[xs] = y

    @pl.when(jnp.logical_not(used) & prev_used)
    def _():
        gather_wait(xs)
        scatter_start(idxp_ref, 1 - xs)
        scatter_wait(1 - xs)
        scatter_wait(xs)


def _moe_dispatch(eid_t, n_blocks):
    n_tok = eid_t.shape[1]
    n_slots = (n_blocks + 2) * MOE_ROWS
    slot_rows = -(-n_slots // SMEM_1D_TILE)
    assert n_blocks <= LANES and n_tok % SMEM_1D_TILE == 0 and SMEM_1D_TILE % MOE_ROWS == 0
    slot, meta = pl.pallas_call(
        functools.partial(_dispatch_kernel, n_tok=n_tok),
        in_specs=[pl.BlockSpec(memory_space=pltpu.VMEM)],
        out_specs=[pl.BlockSpec(memory_space=pltpu.SMEM), pl.BlockSpec(memory_space=pltpu.VMEM)],
        out_shape=[jax.ShapeDtypeStruct((slot_rows * SMEM_1D_TILE,), jnp.int32),
                   jax.ShapeDtypeStruct((MOE_ID_ROWS, LANES), jnp.int32)],
        scratch_shapes=[pltpu.VMEM((MOE_TOP_K * n_tok,), jnp.int32),
                        pltpu.VMEM((slot_rows * SMEM_1D_TILE,), jnp.int32),
                        pltpu.SMEM((MOE_TOP_K * n_tok,), jnp.int32),
                        pltpu.SemaphoreType.DMA(())],
        name="moe_dispatch",
    )(eid_t)
    return (slot[:n_slots].reshape(n_blocks + 2, 1, MOE_ROWS),) + tuple(meta[r, :n_blocks] for r in range(5))


def _dispatch_kernel(eid_ref, slot_ref, meta_ref, dest_vmem, init_vmem, dest_smem, sem, *, n_tok):
    tile = MOE_ROWS
    n_tiles = n_tok // tile
    sub = lax.broadcasted_iota(jnp.int32, (N_EXPERTS, tile), 0)
    si = lax.broadcasted_iota(jnp.int32, (tile, tile), 0)
    ti = lax.broadcasted_iota(jnp.int32, (tile, tile), 1)
    before = jnp.where(si < ti, 1.0, 0.0).astype(BF16)

    def one_hots(j):
        ids = eid_ref[:, j * tile:(j + 1) * tile]
        return [jnp.where(sub == ids[k:k + 1], 1.0, 0.0) for k in range(MOE_TOP_K)]

    carry = jnp.zeros((N_EXPERTS, 1), F32)
    ranks = []
    for j in range(n_tiles):
        oh = one_hots(j)
        both = oh[0] + oh[1]
        seen = carry + _dot(both.astype(BF16), before)
        ranks.append([jnp.sum(seen * o, axis=0, keepdims=True) for o in oh])
        carry = carry + jnp.sum(both, axis=1, keepdims=True)

    counts = carry
    nblk = jnp.floor((counts + float(MOE_ROWS - 1)) * (1.0 / MOE_ROWS))
    ei = lax.broadcasted_iota(jnp.int32, (N_EXPERTS, N_EXPERTS), 0)
    ej = lax.broadcasted_iota(jnp.int32, (N_EXPERTS, N_EXPERTS), 1)
    lower = jnp.where(ej < ei, 1.0, 0.0).astype(BF16)
    first_blk = _dot(lower, jnp.broadcast_to(nblk, (N_EXPERTS, LANES)).astype(BF16))[:, 0:1]
    first_slot = first_blk * float(MOE_ROWS)

    per_row = SMEM_1D_TILE // tile
    for k in range(MOE_TOP_K):
        for q in range(n_tiles // per_row):
            parts = []
            for j in range(q * per_row, (q + 1) * per_row):
                parts.append(jnp.sum(first_slot * one_hots(j)[k], axis=0, keepdims=True) + ranks[j][k])
            dest = jnp.concatenate(parts, axis=1).astype(jnp.int32)
            dest_vmem[pl.ds(k * n_tok + q * SMEM_1D_TILE, SMEM_1D_TILE)] = dest.reshape(SMEM_1D_TILE)

    lane = lax.broadcasted_iota(jnp.int32, (1, SMEM_1D_TILE), 1)
    for q in range(init_vmem.shape[0] // SMEM_1D_TILE):
        pos = q * SMEM_1D_TILE + lane
        spare = MOE_TOP_K * n_tok + ((pos // MOE_ROWS + 1) % 2) * MOE_ROWS + pos % MOE_ROWS
        init_vmem[pl.ds(q * SMEM_1D_TILE, SMEM_1D_TILE)] = spare.reshape(SMEM_1D_TILE)
    copies = [pltpu.make_async_copy(dest_vmem, dest_smem, sem), pltpu.make_async_copy(init_vmem, slot_ref, sem)]
    for cp in copies:
        cp.start()
    for cp in copies:
        cp.wait()

    def place(t, carry_):
        for k in range(MOE_TOP_K):
            slot_ref[dest_smem[k * n_tok + t] + MOE_ROWS] = k * n_tok + t
        return carry_

    lax.fori_loop(0, n_tok, place, 0, unroll=8)

    b = lax.broadcasted_iota(jnp.int32, (N_EXPERTS, LANES), 1).astype(F32)
    e_col = lax.broadcasted_iota(jnp.int32, (N_EXPERTS, LANES), 0).astype(F32)
    b_row = b[0:1]
    be = jnp.minimum(jnp.sum(jnp.where(first_blk + nblk <= b, 1.0, 0.0), axis=0, keepdims=True), N_EXPERTS - 1.0)
    mine = e_col == be
    cnt_b = jnp.sum(jnp.where(mine, counts, 0.0), axis=0, keepdims=True)
    start_b = jnp.sum(jnp.where(mine, first_blk, 0.0), axis=0, keepdims=True)
    nv = jnp.clip(cnt_b - (b_row - start_b) * MOE_ROWS, 0.0, float(MOE_ROWS))
    nv = jnp.where(b_row < jnp.sum(nblk, axis=0, keepdims=True), nv, 0.0)
    first = jnp.where((nv > 0) & ((b_row == 0) | (be != pltpu.roll(be, 1, 1))), 1.0, 0.0)
    li = lax.broadcasted_iota(jnp.int32, (LANES, LANES), 0)
    lj = lax.broadcasted_iota(jnp.int32, (LANES, LANES), 1)
    upto = jnp.where(li <= lj, 1.0, 0.0).astype(BF16)
    run = _dot(jnp.broadcast_to(first, (MOE_ID_ROWS, LANES)).astype(BF16), upto)[0:1] - 1.0
    ws = run - 2.0 * jnp.floor(run * 0.5)
    later = jnp.min(jnp.where((e_col > be) & (counts > 0), e_col, float(LANES)), axis=0, keepdims=True)
    nxt = jnp.where(later >= float(N_EXPERTS), -1.0, later)
    rows = [be, nv, first, ws, nxt] + [jnp.zeros_like(be)] * (MOE_ID_ROWS - 5)
    meta_ref[...] = jnp.concatenate(rows, axis=0).astype(jnp.int32)


def _moe_experts(tok, eid, w1, w3, w2, layer):
    n_tok, n_sub, _ = tok.shape
    d = n_sub * LANES
    assert n_tok & (n_tok - 1) == 0
    ff = w1.shape[3]
    n_assign = n_tok * MOE_TOP_K
    n_blocks = -(-(n_assign + N_EXPERTS * (MOE_ROWS - 1)) // MOE_ROWS)
    slot, be, nv, first, ws, nxt = _moe_dispatch(eid, n_blocks)
    grid_spec = pltpu.PrefetchScalarGridSpec(
        num_scalar_prefetch=5,
        grid=(n_blocks + 1,),
        in_specs=[
            pl.BlockSpec((1, 1, MOE_ROWS), lambda i, *_: (i + 1, 0, 0), memory_space=pltpu.SMEM),
            pl.BlockSpec((1, 1, MOE_ROWS), lambda i, *_: (jnp.minimum(i + 2, n_blocks + 1), 0, 0),
                         memory_space=pltpu.SMEM),
            pl.BlockSpec((1, 1, MOE_ROWS), lambda i, *_: (i, 0, 0), memory_space=pltpu.SMEM),
            pl.BlockSpec(memory_space=pl.ANY),
            pl.BlockSpec(memory_space=pl.ANY),
            pl.BlockSpec(memory_space=pl.ANY),
            pl.BlockSpec(memory_space=pl.ANY),
        ],
        out_specs=pl.BlockSpec(memory_space=pl.ANY),
        scratch_shapes=[
            pltpu.VMEM((2, MOE_ROWS, n_sub, LANES), BF16), pltpu.VMEM((2, MOE_ROWS, n_sub, LANES), F32),
            pltpu.VMEM((MOE_ROWS, d), BF16),
            pltpu.VMEM((2, d, ff), F32), pltpu.VMEM((2, d, ff), F32), pltpu.VMEM((2, ff, d), F32),
            pltpu.VMEM((d, ff), BF16), pltpu.VMEM((d, ff), BF16), pltpu.VMEM((ff, d), BF16),
            pltpu.SemaphoreType.DMA((2,)), pltpu.SemaphoreType.DMA((2,)), pltpu.SemaphoreType.DMA((2,)),
        ],
    )
    return pl.pallas_call(
        functools.partial(_moe_kernel, n_tok=n_tok, layer=layer, n_blocks=n_blocks),
        grid_spec=grid_spec,
        out_shape=jax.ShapeDtypeStruct((MOE_TOP_K * n_tok + 2 * MOE_ROWS, n_sub, LANES), F32),
        compiler_params=pltpu.CompilerParams(dimension_semantics=("arbitrary",),
                                             vmem_limit_bytes=MOE_VMEM_LIMIT_BYTES),
        name="moe_experts",
    )(be, nv, first, ws, nxt, slot, slot, slot, tok, w1, w3, w2)


def _rope_tables(n_seq):
    half = HEAD_DIM // 2
    n_freq = half // 2
    t = jnp.arange(n_seq)
    row = (t // GRID_W).astype(F32)
    col = (t % GRID_W).astype(F32)
    inv_freq = ROPE_BASE ** (-jnp.arange(n_freq, dtype=F32) / n_freq)
    ang_r = row[:, None] * inv_freq[None, :]
    ang_c = col[:, None] * inv_freq[None, :]
    cos = jnp.concatenate([jnp.cos(ang_r)] * 2 + [jnp.cos(ang_c)] * 2, axis=-1)
    sin = jnp.concatenate([-jnp.sin(ang_r), jnp.sin(ang_r), -jnp.sin(ang_c), jnp.sin(ang_c)], axis=-1)
    return cos, sin


def _router_weights(w_g, b_g, w_e, b_e):
    d = w_g.shape[0]
    n = w_g.shape[1] + w_e.shape[1]
    wr = jnp.concatenate([w_g, w_e, jnp.zeros((d, LANES - n), F32)], axis=1)
    rb = jnp.concatenate([b_g, b_e, jnp.zeros((LANES - n,), F32)]).reshape(1, LANES)
    hi = wr.astype(BF16)
    lo = (wr - hi.astype(F32)).astype(BF16)
    return hi, lo, rb


def kernel(x, c, ctx, c_ctx, ada_w, ada_b, ln_g, ln_b, mix_w_in, att_sink, hg_lb, hg_norm_g, mix_w_out, pool_w_in, pool_w_grp, pool_scale, pool_w_out, rt_group_w, rt_group_b, rt_expert_w, rt_expert_b, moe_w1, moe_w3, moe_w2):
    b, n, d = x.shape
    n_ctx = ctx.shape[1]
    t = b * n
    xf = x.reshape(t, d)
    ctxf = ctx.reshape(b * n_ctx, d)

    cond = jnp.concatenate([c, c_ctx[None, :], jnp.zeros((8 - b - 1, d), F32)], axis=0)
    mod = _ada_mod(cond, ada_w, ada_b)

    def chunk(l, j, rows=slice(0, b)):
        return mod[l, rows, j * d:(j + 1) * d][:, None, :]

    w_in = mix_w_in[0].astype(BF16)
    cos, sin = _rope_tables(n)
    q_w, kv_w = ATT_HEADS * HEAD_DIM, ATT_KV_HEADS * HEAD_DIM
    n_att = q_w + 2 * kv_w
    qkv, p = _mod_matmul(xf, chunk(0, 1), chunk(0, 0), w_in, cos, sin, lambda j: j, n_att, w_in.shape[1] - n_att,
                         n_q=q_w, n_rope=q_w + kv_w, n_seq=n, tm=1024, tn=512)
    ctx_rows = slice(b, b + 1)
    kv_ctx, pc = _mod_matmul(ctxf, chunk(0, 1, ctx_rows), chunk(0, 0, ctx_rows), w_in, cos, sin,
                             lambda j: jnp.where(j < 2, j + 2, j + 4), 2 * kv_w, 3 * HG_HEADS * HG_KEY,
                             n_q=0, n_rope=0, n_seq=n, tm=b * n_ctx, tn=512)
    att = _window_attention(qkv, kv_ctx, att_sink[0], b, n, n_ctx)
    o_f = _hgrn2_scan(p, pc, hg_lb[:, :, :], b, n, n_ctx, rev=False)
    o_b = _hgrn2_scan(p, pc, hg_lb[:, :, :], b, n, n_ctx, rev=True)
    wr_hi, wr_lo, rb = _router_weights(rt_group_w[0], rt_group_b[0], rt_expert_w[0], rt_expert_b[0])
    x1, tok, eid, gate = _even_out(
        att, o_f, o_b, p, hg_norm_g[0][None, :], mix_w_out[0].astype(BF16), xf,
        chunk(0, 2), chunk(0, 4), chunk(0, 3), ln_g[0, 0][None, :], ln_b[0, 0][None, :], wr_hi, wr_lo, rb, n)
    y2 = _moe_experts(tok, eid, moe_w1, moe_w3, moe_w2, 0)

    x2, u = _combine_call(x1, y2, gate, chunk(0, 5), ln_g[0, 1][None, :], ln_b[0, 1][None, :], n,
                          proj=(chunk(1, 1), chunk(1, 0), pool_w_in[0].astype(BF16)))
    wr_hi, wr_lo, rb = _router_weights(rt_group_w[1], rt_group_b[1], rt_expert_w[1], rt_expert_b[1])
    x3, tok, eid, gate = _pool_out(
        u, pool_w_grp[0].astype(BF16), pool_scale[0][None, :], pool_w_out[0].astype(BF16), x2,
        chunk(1, 2), chunk(1, 4), chunk(1, 3), ln_g[1, 0][None, :], ln_b[1, 0][None, :], wr_hi, wr_lo, rb, n)
    y2 = _moe_experts(tok, eid, moe_w1, moe_w3, moe_w2, 1)
    out = _combine_call(x3, y2, gate, chunk(1, 5), ln_g[1, 1][None, :], ln_b[1, 1][None, :], n)
    return out.reshape(b, n, d)
```

```python
import functools

import jax
import jax.numpy as jnp
from jax import lax
from jax.experimental import pallas as pl
from jax.experimental.pallas import tpu as pltpu

F32 = jnp.float32
BF16 = jnp.bfloat16

LANES = 128
VMEM_LIMIT_BYTES = 56 * 1024 * 1024
MOE_VMEM_LIMIT_BYTES = 60 * 1024 * 1024

GRID_W = 64
ATT_HEADS = 8
ATT_KV_HEADS = 4
ATT_GROUP = ATT_HEADS // ATT_KV_HEADS
HEAD_DIM = 128
WINDOW = 128
ATT_BLOCK = 128
ROPE_BASE = 10000.0
HG_HEADS = 8
HG_KEY = 128
HG_CHUNK = 64
HG_SUB = 16
HG_FAST_RANGE = 80.0
NORM_EPS = 1e-6
POOL_WINDOWS = (2, 4, 8, 16)
POOL_HALO = 8
MOE_GROUPS = 4
MOE_EXPERTS_PER_GROUP = 8
N_EXPERTS = MOE_GROUPS * MOE_EXPERTS_PER_GROUP
MOE_TOP_K = 2
MOE_ROWS = 256
PROJ_TN = 1024
MOE_ID_ROWS = 8
SMEM_1D_TILE = 1024
WEIGHT_DMA_PRIORITY = 1
LN_EPS = 1e-5
DEPTH = 2
DEEPNORM_ALPHA = (2 * DEPTH) ** 0.25


def _dot(a, b):
    return jnp.dot(a, b, preferred_element_type=F32)


def _dot_nt(a, b):
    return lax.dot_general(a, b, (((1,), (1,)), ((), ())), preferred_element_type=F32)


def _dot_tn(a, b):
    return lax.dot_general(a, b, (((0,), (0,)), ((), ())), preferred_element_type=F32)


def _sigmoid(x):
    return 1.0 / (1.0 + jnp.exp(-x))


def _silu(x):
    return x * _sigmoid(x)


def _params(*sem):
    return pltpu.CompilerParams(dimension_semantics=sem, vmem_limit_bytes=VMEM_LIMIT_BYTES)


def _tiles_from_rows(x):
    n = x.shape[1] // LANES
    return jnp.swapaxes(jnp.stack([x[:, s * LANES:(s + 1) * LANES] for s in range(n)], axis=0), 0, 1)


def _rows_from_tiles(x3):
    xt = jnp.swapaxes(x3, 0, 1)
    return jnp.concatenate([xt[s] for s in range(xt.shape[0])], axis=-1)


def _layer_norm(z, g, b):
    mu = jnp.mean(z, axis=-1, keepdims=True)
    zc = z - mu
    var = jnp.mean(zc * zc, axis=-1, keepdims=True)
    return zc * lax.rsqrt(var + LN_EPS) * g + b


def _ada_kernel(s_ref, w_ref, b_ref, o_ref):
    s = _silu(s_ref[...]).astype(BF16)
    o_ref[0] = _dot(s, w_ref[0].astype(BF16)) + b_ref[0]


def _ada_mod(s, ada_w, ada_b, tn=1024):
    n_l, d, n = ada_w.shape
    return pl.pallas_call(
        _ada_kernel,
        grid=(n_l, n // tn),
        in_specs=[
            pl.BlockSpec((8, d), lambda l, j: (0, 0)),
            pl.BlockSpec((1, d, tn), lambda l, j: (l, 0, j)),
            pl.BlockSpec((1, 1, tn), lambda l, j: (l, 0, j)),
        ],
        out_specs=pl.BlockSpec((1, 8, tn), lambda l, j: (l, 0, j)),
        out_shape=jax.ShapeDtypeStruct((n_l, 8, n), F32),
        compiler_params=_params("parallel", "parallel"),
        name="ada_mod",
    )(s, ada_w, ada_b.reshape(n_l, 1, n))


def _rope(t, cos, sin_signed, first_half):
    partner = jnp.where(first_half, pltpu.roll(t, 96, 1), pltpu.roll(t, 32, 1))
    return t * cos + partner * sin_signed


def _modmm_kernel(x_ref, sc_ref, sh_ref, w_ref, cos_ref, sin_ref, oa_ref, ob_ref, xs_ref, *,
                  n_q, n_rope, n_a_tiles):
    j = pl.program_id(1)

    @pl.when(j == 0)
    def _():
        xs_ref[...] = (x_ref[...] * (1.0 + sc_ref[0]) + sh_ref[0]).astype(BF16)

    acc = _dot(xs_ref[...], w_ref[...])
    tn = acc.shape[1]

    for jt in range(n_a_tiles):
        @pl.when(j == jt)
        def _(jt=jt):
            lane = lax.broadcasted_iota(jnp.int32, (1, HEAD_DIM), 1)
            first_half = (lane % 64) < 32
            for h in range(tn // HEAD_DIM):
                sl = slice(h * HEAD_DIM, (h + 1) * HEAD_DIM)
                col = jt * tn + h * HEAD_DIM
                if col < n_rope:
                    scale = HEAD_DIM ** -0.5 if col < n_q else 1.0
                    oa_ref[:, sl] = _rope(acc[:, sl], cos_ref[...] * scale, sin_ref[...] * scale,
                                          first_half).astype(BF16)
                else:
                    oa_ref[:, sl] = acc[:, sl].astype(BF16)

    @pl.when(j >= n_a_tiles)
    def _():
        ob_ref[...] = acc


def _mod_matmul(x, sc, sh, w, cos, sin, col_map, n_a, n_b, n_q, n_rope, n_seq, tm, tn):
    m, k = x.shape
    rows_per_mod = m // sc.shape[0]
    ta, tb = n_a // tn, n_b // tn
    tab = pl.BlockSpec((tm, HEAD_DIM), lambda i, j: ((i * tm % n_seq) // tm, 0))
    return pl.pallas_call(
        functools.partial(_modmm_kernel, n_q=n_q, n_rope=n_rope, n_a_tiles=ta),
        grid=(m // tm, ta + tb),
        in_specs=[
            pl.BlockSpec((tm, k), lambda i, j: (i, 0)),
            pl.BlockSpec((1, 1, k), lambda i, j: (i * tm // rows_per_mod, 0, 0)),
            pl.BlockSpec((1, 1, k), lambda i, j: (i * tm // rows_per_mod, 0, 0)),
            pl.BlockSpec((k, tn), lambda i, j: (0, col_map(j))),
            tab, tab,
        ],
        out_specs=[pl.BlockSpec((tm, tn), lambda i, j: (i, jnp.minimum(j, ta - 1))),
                   pl.BlockSpec((tm, tn), lambda i, j: (i, jnp.maximum(j - ta, 0)))],
        out_shape=[jax.ShapeDtypeStruct((m, n_a), BF16), jax.ShapeDtypeStruct((m, n_b), F32)],
        scratch_shapes=[pltpu.VMEM((tm, k), BF16)],
        compiler_params=_params("parallel", "arbitrary"),
        name="mod_matmul",
    )(x, sc, sh, w, cos, sin)


def _attn_kernel(sink_ref, q_ref, kp_ref, kc_ref, kn_ref, vp_ref, vc_ref, vn_ref, kx_ref, vx_ref,
                 mp_ref, mn_ref, o_ref, *, n_blocks):
    n = pl.program_id(1)
    blk = ATT_BLOCK
    row1 = lax.broadcasted_iota(jnp.int32, (ATT_GROUP * blk, 1), 0)
    has_prev, has_next = n > 0, n < n_blocks - 1
    for h in range(ATT_KV_HEADS):
        kv = slice(h * HEAD_DIM, (h + 1) * HEAD_DIM)
        q2 = jnp.concatenate([q_ref[:, (ATT_GROUP * h + g) * HEAD_DIM:(ATT_GROUP * h + g + 1) * HEAD_DIM]
                              for g in range(ATT_GROUP)], axis=0)
        s_p = jnp.where(has_prev, _dot_nt(q2, kp_ref[:, kv]) + mp_ref[...], -jnp.inf)
        s_c = _dot_nt(q2, kc_ref[:, kv])
        s_n = jnp.where(has_next, _dot_nt(q2, kn_ref[:, kv]) + mn_ref[...], -jnp.inf)
        s_x = _dot_nt(q2, kx_ref[:, kv])
        sink = jnp.where(row1 < blk, sink_ref[ATT_GROUP * h], sink_ref[ATT_GROUP * h + 1])
        m = jnp.maximum(jnp.maximum(jnp.max(s_p, axis=-1, keepdims=True), jnp.max(s_c, axis=-1, keepdims=True)),
                        jnp.maximum(jnp.max(s_n, axis=-1, keepdims=True), jnp.max(s_x, axis=-1, keepdims=True)))
        m = jnp.maximum(m, sink)
        p_p, p_c, p_n, p_x = (jnp.exp(s - m) for s in (s_p, s_c, s_n, s_x))
        den = (jnp.sum(p_p, axis=-1, keepdims=True) + jnp.sum(p_c, axis=-1, keepdims=True)
               + jnp.sum(p_n, axis=-1, keepdims=True) + jnp.sum(p_x, axis=-1, keepdims=True)
               + jnp.exp(sink - m))
        o = (_dot(p_p.astype(BF16), vp_ref[:, kv]) + _dot(p_c.astype(BF16), vc_ref[:, kv])
             + _dot(p_n.astype(BF16), vn_ref[:, kv]) + _dot(p_x.astype(BF16), vx_ref[:, kv])) / den
        for g in range(ATT_GROUP):
            col = (ATT_GROUP * h + g) * HEAD_DIM
            o_ref[:, col:col + HEAD_DIM] = o[g * blk:(g + 1) * blk].astype(o_ref.dtype)


def _window_attention(qkv, kv_ctx, sink, batch, n_seq, n_ctx):
    assert ATT_GROUP == 2 and WINDOW == ATT_BLOCK
    nb = n_seq // ATT_BLOCK
    qw, kw = ATT_HEADS * HEAD_DIM, ATT_KV_HEADS * HEAD_DIM
    kcol, vcol = qw // kw, qw // kw + 1

    def rows(off):
        return lambda b, n, off=off: b * nb + jnp.clip(n + off, 0, nb - 1)

    kspec = [pl.BlockSpec((ATT_BLOCK, kw), lambda b, n, r=rows(o): (r(b, n), kcol)) for o in (-1, 0, 1)]
    vspec = [pl.BlockSpec((ATT_BLOCK, kw), lambda b, n, r=rows(o): (r(b, n), vcol)) for o in (-1, 0, 1)]
    r = jnp.arange(ATT_GROUP * ATT_BLOCK)[:, None] % ATT_BLOCK
    c = jnp.arange(ATT_BLOCK)[None, :]
    mask_prev = jnp.where(c >= r, 0.0, -jnp.inf).astype(F32)
    mask_next = jnp.where(c <= r, 0.0, -jnp.inf).astype(F32)
    mspec = pl.BlockSpec(mask_prev.shape, lambda b, n: (0, 0))
    return pl.pallas_call(
        functools.partial(_attn_kernel, n_blocks=nb),
        grid=(batch, nb),
        in_specs=[pl.BlockSpec(memory_space=pltpu.SMEM),
                  pl.BlockSpec((ATT_BLOCK, qw), lambda b, n: (b * nb + n, 0))]
        + kspec + vspec
        + [pl.BlockSpec((n_ctx, kw), lambda b, n: (b, 0)), pl.BlockSpec((n_ctx, kw), lambda b, n: (b, 1)),
           mspec, mspec],
        out_specs=pl.BlockSpec((ATT_BLOCK, qw), lambda b, n: (b * nb + n, 0)),
        out_shape=jax.ShapeDtypeStruct((batch * n_seq, qw), BF16),
        compiler_params=_params("parallel", "parallel"),
        name="window_attention",
    )(sink, qkv, qkv, qkv, qkv, qkv, qkv, qkv, kv_ctx, kv_ctx, mask_prev, mask_next)


def _gla_tile(zf, q_raw, v, lb, st_ref, o_ref, rev):
    c_len = zf.shape[0]
    f = lb + (1.0 - lb) * _sigmoid(zf)
    k = 1.0 - f
    g = jnp.log(f)
    ti = lax.broadcasted_iota(jnp.int32, (c_len, c_len), 0)
    si = lax.broadcasted_iota(jnp.int32, (c_len, c_len), 1)
    seen = (si >= ti) if rev else (si <= ti)
    tri = jnp.where(seen, 1.0, 0.0).astype(BF16)
    g1 = g.astype(BF16)
    r1 = g - g1.astype(F32)
    g2 = r1.astype(BF16)
    g3 = (r1 - g2.astype(F32)).astype(BF16)
    c = _dot(tri, g1) + _dot(tri, g2) + _dot(tri, g3)
    c_end = c[0:1] if rev else c[c_len - 1:c_len]
    k_end = (k * jnp.exp(c_end - c)).astype(BF16)
    dec = jnp.exp(c_end)
    vb = v.astype(BF16)

    if o_ref is not None:
        q = _silu(q_raw)
        q_in = (q * jnp.exp(c)).astype(BF16)
        in_range = jnp.min(c_end) >= -HG_FAST_RANGE

        @pl.when(in_range)
        def _():
            k_in = (k * jnp.exp(-c)).astype(BF16)
            for h in range(HG_HEADS):
                sl = slice(h * HG_KEY, (h + 1) * HG_KEY)
                sc = jnp.where(seen, _dot_nt(q_in[:, sl], k_in[:, sl]), 0.0).astype(BF16)
                o_ref[:, sl] = _dot_nt(q_in[:, sl], st_ref[h].astype(BF16)) + _dot(sc, vb[:, sl])

        @pl.when(jnp.logical_not(in_range))
        def _():
            _gla_intra_exact(q, k, v, vb, c, q_in, st_ref, o_ref, rev)

    for h in range(HG_HEADS):
        sl = slice(h * HG_KEY, (h + 1) * HG_KEY)
        st_ref[h] = st_ref[h] * dec[:, sl] + _dot_tn(vb[:, sl], k_end[:, sl])


def _gla_intra_exact(q, k, v, vb, c, q_in, st_ref, o_ref, rev):
    c_len = q.shape[0]
    pairs = []
    size = c_len // 2
    while size >= HG_SUB:
        for lo in range(0, c_len, 2 * size):
            pairs.append((lo, lo + size, lo + 2 * size))
        size //= 2
    scaled = []
    for lo, mid, hi in pairs:
        if rev:
            late, early, bnd = slice(lo, mid), slice(mid, hi), mid
        else:
            late, early, bnd = slice(mid, hi), slice(lo, mid), mid - 1
        cb = c[bnd:bnd + 1]
        q_l = (q[late] * jnp.exp(c[late] - cb)).astype(BF16)
        k_e = (k[early] * jnp.exp(cb - c[early])).astype(BF16)
        scaled.append((late, early, q_l, k_e))
    n_sub = c_len // HG_SUB
    t_idx = lax.broadcasted_iota(jnp.int32, (HG_SUB, 1), 0)
    diag = [[None] * HG_HEADS for _ in range(n_sub)]
    for b in range(n_sub):
        r0 = b * HG_SUB
        qb, cb = q[r0:r0 + HG_SUB], c[r0:r0 + HG_SUB]
        for s in range(HG_SUB):
            row = r0 + s
            ok = (t_idx <= s) if rev else (t_idx >= s)
            w = qb * k[row:row + 1] * jnp.exp(jnp.where(ok, cb - c[row:row + 1], -jnp.inf))
            for h in range(HG_HEADS):
                sl = slice(h * HG_KEY, (h + 1) * HG_KEY)
                contrib = jnp.sum(w[:, sl], axis=-1, keepdims=True) * v[row:row + 1, sl]
                diag[b][h] = contrib if diag[b][h] is None else diag[b][h] + contrib

    for h in range(HG_HEADS):
        sl = slice(h * HG_KEY, (h + 1) * HG_KEY)
        o_h = _dot_nt(q_in[:, sl], st_ref[h].astype(BF16))
        parts = [diag[b][h] for b in range(n_sub)]
        for late, early, q_l, k_e in scaled:
            sc = _dot_nt(q_l[:, sl], k_e[:, sl]).astype(BF16)
            add = _dot(sc, vb[early, sl])
            b0 = late.start // HG_SUB
            for j in range((late.stop - late.start) // HG_SUB):
                parts[b0 + j] = parts[b0 + j] + add[j * HG_SUB:(j + 1) * HG_SUB]
        o_ref[:, sl] = o_h + jnp.concatenate(parts, axis=0)


def _gla_kernel(lb_ref, zf_ref, q_ref, v_ref, zfc_ref, vc_ref, o_ref, st_ref, *, rev, n_ctx_chunks):
    s = pl.program_id(1)

    @pl.when(s == 0)
    def _():
        st_ref[...] = jnp.zeros_like(st_ref)

    x = lb_ref[...]
    e = jnp.exp(x - jnp.max(x, axis=0, keepdims=True))
    lb = e[0:1] / jnp.sum(e, axis=0, keepdims=True)

    @pl.when(s < n_ctx_chunks)
    def _():
        _gla_tile(zfc_ref[...], None, vc_ref[...], lb, st_ref, None, rev)

    @pl.when(s >= n_ctx_chunks)
    def _():
        _gla_tile(zf_ref[...], q_ref[...], v_ref[...], lb, st_ref, o_ref, rev)


def _hgrn2_scan(p, pc, hg_lb, batch, n_seq, n_ctx, rev):
    hk = HG_HEADS * HG_KEY
    nc, ncc = n_seq // HG_CHUNK, n_ctx // HG_CHUNK
    d = 1 if rev else 0

    def lat(b, s):
        j = jnp.maximum(s - ncc, 0)
        return b * nc + (nc - 1 - j if rev else j)

    def ctx(b, s):
        j = jnp.minimum(s, ncc - 1)
        return b * ncc + (ncc - 1 - j if rev else j)

    return pl.pallas_call(
        functools.partial(_gla_kernel, rev=rev, n_ctx_chunks=ncc),
        grid=(batch, ncc + nc),
        in_specs=[
            pl.BlockSpec((None, hg_lb.shape[1], hk), lambda b, s: (d, 0, 0)),
            pl.BlockSpec((HG_CHUNK, hk), lambda b, s: (lat(b, s), 1 + d)),
            pl.BlockSpec((HG_CHUNK, hk), lambda b, s: (lat(b, s), 0)),
            pl.BlockSpec((HG_CHUNK, hk), lambda b, s: (lat(b, s), 3)),
            pl.BlockSpec((HG_CHUNK, hk), lambda b, s: (ctx(b, s), d)),
            pl.BlockSpec((HG_CHUNK, hk), lambda b, s: (ctx(b, s), 2)),
        ],
        out_specs=pl.BlockSpec((HG_CHUNK, hk), lambda b, s: (lat(b, s), 0)),
        out_shape=jax.ShapeDtypeStruct((batch * n_seq, hk), F32),
        scratch_shapes=[pltpu.VMEM((HG_HEADS, HG_KEY, HG_KEY), F32)],
        compiler_params=_params("parallel", "arbitrary"),
        name="hgrn2_bwd" if rev else "hgrn2_fwd",
    )(hg_lb, p, p, p, pc, pc)


def _route(tok, wr_hi, wr_lo, rb):
    t_hi = tok.astype(BF16)
    t_lo = (tok - t_hi.astype(F32)).astype(BF16)
    lg = _dot(t_hi, wr_hi) + _dot(t_hi, wr_lo) + _dot(t_lo, wr_hi) + rb
    lane = lax.broadcasted_iota(jnp.int32, lg.shape, 1)
    lane_f = lane.astype(F32)
    ninf = -jnp.inf
    gl = jnp.where(lane < MOE_GROUPS, lg, ninf)
    gmax = jnp.max(gl, axis=-1, keepdims=True)
    g_idx = jnp.min(jnp.where(gl == gmax, lane_f, float(LANES)), axis=-1, keepdims=True)
    g_val = 1.0 / jnp.sum(jnp.exp(gl - gmax), axis=-1, keepdims=True)
    e_lane = lane_f - float(MOE_GROUPS)
    lo = g_idx * float(MOE_EXPERTS_PER_GROUP)
    in_grp = (e_lane >= lo) & (e_lane < lo + float(MOE_EXPERTS_PER_GROUP))
    el = jnp.where(in_grp, lg, ninf)
    l1 = jnp.max(el, axis=-1, keepdims=True)
    i1 = jnp.min(jnp.where(el == l1, e_lane, float(LANES)), axis=-1, keepdims=True)
    el2 = jnp.where(e_lane == i1, ninf, el)
    l2 = jnp.max(el2, axis=-1, keepdims=True)
    i2 = jnp.min(jnp.where(el2 == l2, e_lane, float(LANES)), axis=-1, keepdims=True)
    r = jnp.exp(l2 - l1)
    w1 = g_val / (1.0 + r)
    w2 = w1 * r
    eid = jnp.where(lane == 0, i1, jnp.where(lane == 1, i2, 0.0))
    gate = jnp.where(lane == 0, w1, jnp.where(lane == 1, w2, 0.0))
    eid_t = jnp.transpose(eid)[:MOE_ID_ROWS].astype(jnp.int32)
    return eid_t, gate


def _post_mix(y, x_ref, g1_ref, sc2_ref, sh2_ref, lng_ref, lnb_ref, wrh_ref, wrl_ref, rb_ref,
              x1_ref, tok_ref, eid_ref, gate_ref):
    x1 = _layer_norm(DEEPNORM_ALPHA * x_ref[...] + g1_ref[0] * y, lng_ref[...], lnb_ref[...])
    x1_ref[...] = x1
    tok = x1 * (1.0 + sc2_ref[0]) + sh2_ref[0]
    tok_ref[...] = _tiles_from_rows(tok).astype(BF16)
    eid, gate = _route(tok, wrh_ref[...], wrl_ref[...], rb_ref[...])
    eid_ref[...] = eid
    gate_ref[...] = gate


def _even_out_kernel(att_ref, of_ref, ob_ref, gt_ref, ng_ref, wo_ref, *rest):
    o = of_ref[...] + ob_ref[...]
    pieces = []
    for h in range(HG_HEADS):
        oh = o[:, h * HG_KEY:(h + 1) * HG_KEY]
        pieces.append(oh * lax.rsqrt(jnp.mean(oh * oh, axis=-1, keepdims=True) + NORM_EPS))
    hg = (jnp.concatenate(pieces, axis=-1) * ng_ref[...] * _silu(gt_ref[...])).astype(BF16)
    n_att = att_ref.shape[1]
    y = _dot(att_ref[...], wo_ref[:n_att, :]) + _dot(hg, wo_ref[n_att:, :])
    _post_mix(y, *rest)


def _post_specs(d, tm, rows_per_batch):
    def bmap(i):
        return (i * tm // rows_per_batch, 0, 0)

    row = pl.BlockSpec((tm, d), lambda i: (i, 0))
    mod = pl.BlockSpec((1, 1, d), bmap)
    vec = pl.BlockSpec((1, d), lambda i: (0, 0))
    rw = pl.BlockSpec((d, LANES), lambda i: (0, 0))
    in_specs = [row, mod, mod, mod, vec, vec, rw, rw, pl.BlockSpec((1, LANES), lambda i: (0, 0))]
    lane_blk = pl.BlockSpec((tm, LANES), lambda i: (i, 0))
    tiles = pl.BlockSpec((tm, d // LANES, LANES), lambda i: (i, 0, 0))
    out_specs = [row, tiles, pl.BlockSpec((MOE_ID_ROWS, tm), lambda i: (0, i)), lane_blk]
    return in_specs, out_specs


def _post_out_shapes(t, d):
    return [jax.ShapeDtypeStruct((t, d), F32), jax.ShapeDtypeStruct((t, d // LANES, LANES), BF16),
            jax.ShapeDtypeStruct((MOE_ID_ROWS, t), jnp.int32), jax.ShapeDtypeStruct((t, LANES), F32)]


def _even_out(att, o_f, o_b, p, norm_g, w_out, x, g1, sc2, sh2, lng, lnb, wr_hi, wr_lo, rb, rows_per_batch, tm=256):
    t, d = x.shape
    hv = o_f.shape[1]
    post_in, post_out = _post_specs(d, tm, rows_per_batch)
    return pl.pallas_call(
        _even_out_kernel,
        grid=(t // tm,),
        in_specs=[
            pl.BlockSpec((tm, att.shape[1]), lambda i: (i, 0)),
            pl.BlockSpec((tm, hv), lambda i: (i, 0)),
            pl.BlockSpec((tm, hv), lambda i: (i, 0)),
            pl.BlockSpec((tm, hv), lambda i: (i, 4)),
            pl.BlockSpec((1, hv), lambda i: (0, 0)),
            pl.BlockSpec(w_out.shape, lambda i: (0, 0)),
        ] + post_in,
        out_specs=post_out,
        out_shape=_post_out_shapes(t, d),
        compiler_params=_params("parallel"),
        name="even_out",
    )(att, o_f, o_b, p, norm_g, w_out, x, g1, sc2, sh2, lng, lnb, wr_hi, wr_lo, rb)


def _combine(x_ref, ya_ref, yb_ref, gate_ref, g2_ref, lng_ref, lnb_ref):
    gate = gate_ref[...]
    y = (gate[:, 0:1] * _rows_from_tiles(ya_ref[...].astype(F32))
         + gate[:, 1:2] * _rows_from_tiles(yb_ref[...].astype(F32)))
    return _layer_norm(DEEPNORM_ALPHA * x_ref[...] + g2_ref[0] * y, lng_ref[...], lnb_ref[...])


def _combine_proj_kernel(x_ref, ya_ref, yb_ref, gate_ref, g2_ref, lng_ref, lnb_ref, sc_ref, sh_ref, w_ref,
                         x2_ref, u_ref):
    x2 = _combine(x_ref, ya_ref, yb_ref, gate_ref, g2_ref, lng_ref, lnb_ref)
    x2_ref[...] = x2
    u_ref[...] = _dot((x2 * (1.0 + sc_ref[0]) + sh_ref[0]).astype(BF16), w_ref[...])


def _combine_kernel(x_ref, ya_ref, yb_ref, gate_ref, g2_ref, lng_ref, lnb_ref, x2_ref):
    x2_ref[...] = _combine(x_ref, ya_ref, yb_ref, gate_ref, g2_ref, lng_ref, lnb_ref)


def _combine_call(x1, y2, gate, g2, lng, lnb, rows_per_batch, proj=None, tm=256):
    t, d = x1.shape
    nt = t // tm

    def bmap(i):
        return (i * tm // rows_per_batch, 0, 0)

    row = pl.BlockSpec((tm, d), lambda i: (i, 0))
    mod = pl.BlockSpec((1, 1, d), bmap)
    vec = pl.BlockSpec((1, d), lambda i: (0, 0))
    in_specs = [row, pl.BlockSpec((tm, d // LANES, LANES), lambda i: (i, 0, 0)),
                pl.BlockSpec((tm, d // LANES, LANES), lambda i: (nt + i, 0, 0)),
                pl.BlockSpec((tm, LANES), lambda i: (i, 0)), mod, vec, vec]
    args = [x1, y2, y2, gate, g2, lng, lnb]
    if proj is None:
        return pl.pallas_call(
            _combine_kernel, grid=(nt,), in_specs=in_specs, out_specs=row,
            out_shape=jax.ShapeDtypeStruct((t, d), F32),
            compiler_params=_params("parallel"), name="combine_ln")(*args)
    sc, sh, w = proj
    return pl.pallas_call(
        _combine_proj_kernel, grid=(nt,),
        in_specs=in_specs + [mod, mod, pl.BlockSpec(w.shape, lambda i: (0, 0))],
        out_specs=[row, pl.BlockSpec((tm, w.shape[1]), lambda i: (i, 0))],
        out_shape=[jax.ShapeDtypeStruct((t, d), F32), jax.ShapeDtypeStruct((t, w.shape[1]), F32)],
        compiler_params=_params("parallel"), name="combine_ln_proj")(*args, sc, sh, w)


def _pool_out_kernel(up_ref, uc_ref, un_ref, wg_ref, ps_ref, wo_ref, *rest, n_seq):
    tm, d = uc_ref.shape
    n_grp = len(POOL_WINDOWS)
    ch = d // n_grp
    halo = POOL_HALO
    pos0 = (pl.program_id(0) * tm) % n_seq
    e_pos = pos0 - halo + lax.broadcasted_iota(jnp.int32, (tm + 2 * halo, 1), 0)
    e_ok = (e_pos >= 0) & (e_pos < n_seq)
    t_pos = pos0 + lax.broadcasted_iota(jnp.int32, (tm, 1), 0)
    y = None
    for gi, w in enumerate(POOL_WINDOWS):
        cs = slice(gi * ch, (gi + 1) * ch)
        u = uc_ref[:, cs]
        ext = jnp.where(e_ok, jnp.concatenate([up_ref[:, cs], u, un_ref[:, cs]], axis=0), 0.0)
        a, span = ext, 1
        while span < w:
            a = a[:a.shape[0] - span] + a[span:]
            span *= 2
        start = halo - w // 2
        win = a[start:start + tm]
        cnt = (jnp.minimum(t_pos + (w - w // 2), n_seq) - jnp.maximum(t_pos - w // 2, 0)).astype(F32)
        mixed = (win / cnt - u).astype(BF16)
        z = (_dot(mixed, wg_ref[gi]) * ps_ref[:, cs]).astype(BF16)
        part = _dot(z, wo_ref[cs, :])
        y = part if y is None else y + part
    _post_mix(y, *rest)


def _pool_out(u, w_grp, scale, w_out, x, g1, sc2, sh2, lng, lnb, wr_hi, wr_lo, rb, n_seq, tm=256):
    t, d = x.shape
    hb = tm // POOL_HALO
    n_hb = t // POOL_HALO
    post_in, post_out = _post_specs(d, tm, n_seq)
    return pl.pallas_call(
        functools.partial(_pool_out_kernel, n_seq=n_seq),
        grid=(t // tm,),
        in_specs=[
            pl.BlockSpec((POOL_HALO, d), lambda i: (jnp.maximum(i * hb - 1, 0), 0)),
            pl.BlockSpec((tm, d), lambda i: (i, 0)),
            pl.BlockSpec((POOL_HALO, d), lambda i: (jnp.minimum((i + 1) * hb, n_hb - 1), 0)),
            pl.BlockSpec(w_grp.shape, lambda i: (0, 0, 0)),
            pl.BlockSpec((1, d), lambda i: (0, 0)),
            pl.BlockSpec(w_out.shape, lambda i: (0, 0)),
        ] + post_in,
        out_specs=post_out,
        out_shape=_post_out_shapes(t, d),
        compiler_params=_params("parallel"),
        name="pool_out",
    )(u, u, u, w_grp, scale, w_out, x, g1, sc2, sh2, lng, lnb, wr_hi, wr_lo, rb)


def _moe_kernel(be_ref, nv_ref, first_ref, ws_ref, nxt_ref, idx_ref, idxn_ref, idxp_ref, tok_hbm, w1_hbm, w3_hbm, w2_hbm,
                y_hbm, xbuf, ybuf, xb_ref, wf1, wf3, wf2, w1b, w3b, w2b, gsem, ssem, wsem, *, n_tok, layer, n_blocks):
    i = pl.program_id(0)
    used = nv_ref[jnp.minimum(i, n_blocks - 1)] > 0
    used = used & (i < n_blocks)
    prev_used = (i > 0) & (nv_ref[jnp.maximum(i - 1, 0)] > 0)
    xs = i % 2

    def weight_copies(e, ws):
        return (pltpu.make_async_copy(w1_hbm.at[layer, e], wf1.at[ws], wsem.at[ws]),
                pltpu.make_async_copy(w3_hbm.at[layer, e], wf3.at[ws], wsem.at[ws]),
                pltpu.make_async_copy(w2_hbm.at[layer, e], wf2.at[ws], wsem.at[ws]))

    def gather_start(idx, slot):
        for r in range(MOE_ROWS):
            tok = idx[0, 0, r] & (n_tok - 1)
            pltpu.make_async_copy(tok_hbm.at[tok], xbuf.at[slot, r], gsem.at[slot]).start()

    def gather_wait(slot):
        pltpu.make_async_copy(tok_hbm.at[pl.ds(0, MOE_ROWS)], xbuf.at[slot], gsem.at[slot]).wait()

    def scatter_start(idx, slot):
        for r in range(MOE_ROWS):
            pltpu.make_async_copy(ybuf.at[slot, r], y_hbm.at[idx[0, 0, r]], ssem.at[slot]).start(priority=r % 2)

    def scatter_wait(slot):
        pltpu.make_async_copy(ybuf.at[slot], y_hbm.at[pl.ds(0, MOE_ROWS)], ssem.at[slot]).wait()

    @pl.when(i == 0)
    def _():
        xbuf[...] = jnp.zeros_like(xbuf)
        ybuf[...] = jnp.zeros_like(ybuf)
        spare0 = pltpu.make_async_copy(
            ybuf.at[0], y_hbm.at[pl.ds(MOE_TOP_K * n_tok, MOE_ROWS)], ssem.at[0])
        spare0.start()
        for cp in weight_copies(be_ref[0], 0):
            cp.start(priority=WEIGHT_DMA_PRIORITY)
        gather_start(idx_ref, 0)

    @pl.when(used)
    def _():
        ws = ws_ref[i]

        @pl.when(first_ref[i] == 1)
        def _():
            for cp in weight_copies(be_ref[i], ws):
                cp.wait()
            nxt = nxt_ref[i]

            @pl.when(nxt >= 0)
            def _():
                for cp in weight_copies(nxt, 1 - ws):
                    cp.start(priority=WEIGHT_DMA_PRIORITY)

            w1b[...] = wf1[ws].astype(BF16)
            w3b[...] = wf3[ws].astype(BF16)
            w2b[...] = wf2[ws].astype(BF16)

        gather_wait(xs)
        xb_ref[...] = _rows_from_tiles(xbuf[xs].astype(F32)).astype(BF16)
        gather_start(idxn_ref, 1 - xs)
        scatter_start(idxp_ref, 1 - xs)
        xb = xb_ref[...]
        h = (_silu(_dot(xb, w1b[...])) * _dot(xb, w3b[...])).astype(BF16)
        y = _tiles_from_rows(_dot(h, w2b[...])).astype(BF16)
        scatter_wait(xs)
        ybuf[xs] = y

    @pl.when(jnp.logical_not(used) & prev_used)
    def _():
        gather_wait(xs)
        scatter_start(idxp_ref, 1 - xs)
        scatter_wait(1 - xs)
        scatter_wait(xs)


def _moe_dispatch(eid_t, n_blocks):
    n_tok = eid_t.shape[1]
    n_slots = (n_blocks + 2) * MOE_ROWS
    slot_rows = -(-n_slots // SMEM_1D_TILE)
    assert n_blocks <= LANES and n_tok % SMEM_1D_TILE == 0 and SMEM_1D_TILE % MOE_ROWS == 0
    slot, meta = pl.pallas_call(
        functools.partial(_dispatch_kernel, n_tok=n_tok),
        in_specs=[pl.BlockSpec(memory_space=pltpu.VMEM)],
        out_specs=[pl.BlockSpec(memory_space=pltpu.SMEM), pl.BlockSpec(memory_space=pltpu.VMEM)],
        out_shape=[jax.ShapeDtypeStruct((slot_rows * SMEM_1D_TILE,), jnp.int32),
                   jax.ShapeDtypeStruct((MOE_ID_ROWS, LANES), jnp.int32)],
        scratch_shapes=[pltpu.VMEM((MOE_TOP_K * n_tok,), jnp.int32),
                        pltpu.VMEM((slot_rows * SMEM_1D_TILE,), jnp.int32),
                        pltpu.SMEM((MOE_TOP_K * n_tok,), jnp.int32),
                        pltpu.SemaphoreType.DMA(())],
        name="moe_dispatch",
    )(eid_t)
    return (slot[:n_slots].reshape(n_blocks + 2, 1, MOE_ROWS),) + tuple(meta[r, :n_blocks] for r in range(5))


def _dispatch_kernel(eid_ref, slot_ref, meta_ref, dest_vmem, init_vmem, dest_smem, sem, *, n_tok):
    tile = MOE_ROWS
    n_tiles = n_tok // tile
    sub = lax.broadcasted_iota(jnp.int32, (N_EXPERTS, tile), 0)
    si = lax.broadcasted_iota(jnp.int32, (tile, tile), 0)
    ti = lax.broadcasted_iota(jnp.int32, (tile, tile), 1)
    before = jnp.where(si < ti, 1.0, 0.0).astype(BF16)

    def one_hots(j):
        ids = eid_ref[:, j * tile:(j + 1) * tile]
        return [jnp.where(sub == ids[k:k + 1], 1.0, 0.0) for k in range(MOE_TOP_K)]

    carry = jnp.zeros((N_EXPERTS, 1), F32)
    ranks = []
    for j in range(n_tiles):
        oh = one_hots(j)
        both = oh[0] + oh[1]
        seen = carry + _dot(both.astype(BF16), before)
        ranks.append([jnp.sum(seen * o, axis=0, keepdims=True) for o in oh])
        carry = carry + jnp.sum(both, axis=1, keepdims=True)

    counts = carry
    nblk = jnp.floor((counts + float(MOE_ROWS - 1)) * (1.0 / MOE_ROWS))
    ei = lax.broadcasted_iota(jnp.int32, (N_EXPERTS, N_EXPERTS), 0)
    ej = lax.broadcasted_iota(jnp.int32, (N_EXPERTS, N_EXPERTS), 1)
    lower = jnp.where(ej < ei, 1.0, 0.0).astype(BF16)
    first_blk = _dot(lower, jnp.broadcast_to(nblk, (N_EXPERTS, LANES)).astype(BF16))[:, 0:1]
    first_slot = first_blk * float(MOE_ROWS)

    per_row = SMEM_1D_TILE // tile
    for k in range(MOE_TOP_K):
        for q in range(n_tiles // per_row):
            parts = []
            for j in range(q * per_row, (q + 1) * per_row):
                parts.append(jnp.sum(first_slot * one_hots(j)[k], axis=0, keepdims=True) + ranks[j][k])
            dest = jnp.concatenate(parts, axis=1).astype(jnp.int32)
            dest_vmem[pl.ds(k * n_tok + q * SMEM_1D_TILE, SMEM_1D_TILE)] = dest.reshape(SMEM_1D_TILE)

    lane = lax.broadcasted_iota(jnp.int32, (1, SMEM_1D_TILE), 1)
    for q in range(init_vmem.shape[0] // SMEM_1D_TILE):
        pos = q * SMEM_1D_TILE + lane
        spare = MOE_TOP_K * n_tok + ((pos // MOE_ROWS + 1) % 2) * MOE_ROWS + pos % MOE_ROWS
        init_vmem[pl.ds(q * SMEM_1D_TILE, SMEM_1D_TILE)] = spare.reshape(SMEM_1D_TILE)
    copies = [pltpu.make_async_copy(dest_vmem, dest_smem, sem), pltpu.make_async_copy(init_vmem, slot_ref, sem)]
    for cp in copies:
        cp.start()
    for cp in copies:
        cp.wait()

    def place(t, carry_):
        for k in range(MOE_TOP_K):
            slot_ref[dest_smem[k * n_tok + t] + MOE_ROWS] = k * n_tok + t
        return carry_

    lax.fori_loop(0, n_tok, place, 0, unroll=8)

    b = lax.broadcasted_iota(jnp.int32, (N_EXPERTS, LANES), 1).astype(F32)
    e_col = lax.broadcasted_iota(jnp.int32, (N_EXPERTS, LANES), 0).astype(F32)
    b_row = b[0:1]
    be = jnp.minimum(jnp.sum(jnp.where(first_blk + nblk <= b, 1.0, 0.0), axis=0, keepdims=True), N_EXPERTS - 1.0)
    mine = e_col == be
    cnt_b = jnp.sum(jnp.where(mine, counts, 0.0), axis=0, keepdims=True)
    start_b = jnp.sum(jnp.where(mine, first_blk, 0.0), axis=0, keepdims=True)
    nv = jnp.clip(cnt_b - (b_row - start_b) * MOE_ROWS, 0.0, float(MOE_ROWS))
    nv = jnp.where(b_row < jnp.sum(nblk, axis=0, keepdims=True), nv, 0.0)
    first = jnp.where((nv > 0) & ((b_row == 0) | (be != pltpu.roll(be, 1, 1))), 1.0, 0.0)
    li = lax.broadcasted_iota(jnp.int32, (LANES, LANES), 0)
    lj = lax.broadcasted_iota(jnp.int32, (LANES, LANES), 1)
    upto = jnp.where(li <= lj, 1.0, 0.0).astype(BF16)
    run = _dot(jnp.broadcast_to(first, (MOE_ID_ROWS, LANES)).astype(BF16), upto)[0:1] - 1.0
    ws = run - 2.0 * jnp.floor(run * 0.5)
    later = jnp.min(jnp.where((e_col > be) & (counts > 0), e_col, float(LANES)), axis=0, keepdims=True)
    nxt = jnp.where(later >= float(N_EXPERTS), -1.0, later)
    rows = [be, nv, first, ws, nxt] + [jnp.zeros_like(be)] * (MOE_ID_ROWS - 5)
    meta_ref[...] = jnp.concatenate(rows, axis=0).astype(jnp.int32)


def _moe_experts(tok, eid, w1, w3, w2, layer):
    n_tok, n_sub, _ = tok.shape
    d = n_sub * LANES
    assert n_tok & (n_tok - 1) == 0
    ff = w1.shape[3]
    n_assign = n_tok * MOE_TOP_K
    n_blocks = -(-(n_assign + N_EXPERTS * (MOE_ROWS - 1)) // MOE_ROWS)
    slot, be, nv, first, ws, nxt = _moe_dispatch(eid, n_blocks)
    grid_spec = pltpu.PrefetchScalarGridSpec(
        num_scalar_prefetch=5,
        grid=(n_blocks + 1,),
        in_specs=[
            pl.BlockSpec((1, 1, MOE_ROWS), lambda i, *_: (i + 1, 0, 0), memory_space=pltpu.SMEM),
            pl.BlockSpec((1, 1, MOE_ROWS), lambda i, *_: (jnp.minimum(i + 2, n_blocks + 1), 0, 0),
                         memory_space=pltpu.SMEM),
            pl.BlockSpec((1, 1, MOE_ROWS), lambda i, *_: (i, 0, 0), memory_space=pltpu.SMEM),
            pl.BlockSpec(memory_space=pl.ANY),
            pl.BlockSpec(memory_space=pl.ANY),
            pl.BlockSpec(memory_space=pl.ANY),
            pl.BlockSpec(memory_space=pl.ANY),
        ],
        out_specs=pl.BlockSpec(memory_space=pl.ANY),
        scratch_shapes=[
            pltpu.VMEM((2, MOE_ROWS, n_sub, LANES), BF16), pltpu.VMEM((2, MOE_ROWS, n_sub, LANES), BF16),
            pltpu.VMEM((MOE_ROWS, d), BF16),
            pltpu.VMEM((2, d, ff), F32), pltpu.VMEM((2, d, ff), F32), pltpu.VMEM((2, ff, d), F32),
            pltpu.VMEM((d, ff), BF16), pltpu.VMEM((d, ff), BF16), pltpu.VMEM((ff, d), BF16),
            pltpu.SemaphoreType.DMA((2,)), pltpu.SemaphoreType.DMA((2,)), pltpu.SemaphoreType.DMA((2,)),
        ],
    )
    return pl.pallas_call(
        functools.partial(_moe_kernel, n_tok=n_tok, layer=layer, n_blocks=n_blocks),
        grid_spec=grid_spec,
        out_shape=jax.ShapeDtypeStruct((MOE_TOP_K * n_tok + 2 * MOE_ROWS, n_sub, LANES), BF16),
        compiler_params=pltpu.CompilerParams(dimension_semantics=("arbitrary",),
                                             vmem_limit_bytes=MOE_VMEM_LIMIT_BYTES),
        name="moe_experts",
    )(be, nv, first, ws, nxt, slot, slot, slot, tok, w1, w3, w2)


def _rope_tables(n_seq):
    half = HEAD_DIM // 2
    n_freq = half // 2
    t = jnp.arange(n_seq)
    row = (t // GRID_W).astype(F32)
    col = (t % GRID_W).astype(F32)
    inv_freq = ROPE_BASE ** (-jnp.arange(n_freq, dtype=F32) / n_freq)
    ang_r = row[:, None] * inv_freq[None, :]
    ang_c = col[:, None] * inv_freq[None, :]
    cos = jnp.concatenate([jnp.cos(ang_r)] * 2 + [jnp.cos(ang_c)] * 2, axis=-1)
    sin = jnp.concatenate([-jnp.sin(ang_r), jnp.sin(ang_r), -jnp.sin(ang_c), jnp.sin(ang_c)], axis=-1)
    return cos, sin


def _router_weights(w_g, b_g, w_e, b_e):
    d = w_g.shape[0]
    n = w_g.shape[1] + w_e.shape[1]
    wr = jnp.concatenate([w_g, w_e, jnp.zeros((d, LANES - n), F32)], axis=1)
    rb = jnp.concatenate([b_g, b_e, jnp.zeros((LANES - n,), F32)]).reshape(1, LANES)
    hi = wr.astype(BF16)
    lo = (wr - hi.astype(F32)).astype(BF16)
    return hi, lo, rb


def kernel(x, c, ctx, c_ctx, ada_w, ada_b, ln_g, ln_b, mix_w_in, att_sink, hg_lb, hg_norm_g, mix_w_out, pool_w_in, pool_w_grp, pool_scale, pool_w_out, rt_group_w, rt_group_b, rt_expert_w, rt_expert_b, moe_w1, moe_w3, moe_w2):
    b, n, d = x.shape
    n_ctx = ctx.shape[1]
    t = b * n
    xf = x.reshape(t, d)
    ctxf = ctx.reshape(b * n_ctx, d)

    cond = jnp.concatenate([c, c_ctx[None, :], jnp.zeros((8 - b - 1, d), F32)], axis=0)
    mod = _ada_mod(cond, ada_w, ada_b)

    def chunk(l, j, rows=slice(0, b)):
        return mod[l, rows, j * d:(j + 1) * d][:, None, :]

    w_in = mix_w_in[0].astype(BF16)
    cos, sin = _rope_tables(n)
    q_w, kv_w = ATT_HEADS * HEAD_DIM, ATT_KV_HEADS * HEAD_DIM
    n_att = q_w + 2 * kv_w
    qkv, p = _mod_matmul(xf, chunk(0, 1), chunk(0, 0), w_in, cos, sin, lambda j: j, n_att, w_in.shape[1] - n_att,
                         n_q=q_w, n_rope=q_w + kv_w, n_seq=n, tm=1024, tn=PROJ_TN)
    ctx_rows = slice(b, b + 1)
    kv_ctx, pc = _mod_matmul(ctxf, chunk(0, 1, ctx_rows), chunk(0, 0, ctx_rows), w_in, cos, sin,
                             lambda j: jnp.where(j < 1, j + 1, j + 2), 2 * kv_w, 3 * HG_HEADS * HG_KEY,
                             n_q=0, n_rope=0, n_seq=n, tm=b * n_ctx, tn=PROJ_TN)
    att = _window_attention(qkv, kv_ctx, att_sink[0], b, n, n_ctx)
    o_f = _hgrn2_scan(p, pc, hg_lb[:, :, :], b, n, n_ctx, rev=False)
    o_b = _hgrn2_scan(p, pc, hg_lb[:, :, :], b, n, n_ctx, rev=True)
    wr_hi, wr_lo, rb = _router_weights(rt_group_w[0], rt_group_b[0], rt_expert_w[0], rt_expert_b[0])
    x1, tok, eid, gate = _even_out(
        att, o_f, o_b, p, hg_norm_g[0][None, :], mix_w_out[0].astype(BF16), xf,
        chunk(0, 2), chunk(0, 4), chunk(0, 3), ln_g[0, 0][None, :], ln_b[0, 0][None, :], wr_hi, wr_lo, rb, n)
    y2 = _moe_experts(tok, eid, moe_w1, moe_w3, moe_w2, 0)

    x2, u = _combine_call(x1, y2, gate, chunk(0, 5), ln_g[0, 1][None, :], ln_b[0, 1][None, :], n,
                          proj=(chunk(1, 1), chunk(1, 0), pool_w_in[0].astype(BF16)))
    wr_hi, wr_lo, rb = _router_weights(rt_group_w[1], rt_group_b[1], rt_expert_w[1], rt_expert_b[1])
    x3, tok, eid, gate = _pool_out(
        u, pool_w_grp[0].astype(BF16), pool_scale[0][None, :], pool_w_out[0].astype(BF16), x2,
        chunk(1, 2), chunk(1, 4), chunk(1, 3), ln_g[1, 0][None, :], ln_b[1, 0][None, :], wr_hi, wr_lo, rb, n)
    y2 = _moe_experts(tok, eid, moe_w1, moe_w3, moe_w2, 1)
    out = _combine_call(x3, y2, gate, chunk(1, 5), ln_g[1, 1][None, :], ln_b[1, 1][None, :], n)
    return out.reshape(b, n, d)
```

```python
import functools

import jax
import jax.numpy as jnp
from jax import lax
from jax.experimental import pallas as pl
from jax.experimental.pallas import tpu as pltpu

F32 = jnp.float32
BF16 = jnp.bfloat16

LANES = 128
MXU_TILE = 256
VMEM_LIMIT_BYTES = 56 * 1024 * 1024
MOE_VMEM_LIMIT_BYTES = 60 * 1024 * 1024

GRID_W = 64
ATT_HEADS = 8
ATT_KV_HEADS = 4
ATT_GROUP = ATT_HEADS // ATT_KV_HEADS
HEAD_DIM = 128
WINDOW = 128
ATT_BLOCK = 128
ROPE_BASE = 10000.0
HG_HEADS = 8
HG_KEY = 128
HG_CHUNK = 64
HG_STEP_CHUNKS = 4
HG_SUB = 16
HG_FAST_RANGE = 80.0
NORM_EPS = 1e-6
POOL_WINDOWS = (2, 4, 8, 16)
POOL_HALO = 8
MOE_GROUPS = 4
MOE_EXPERTS_PER_GROUP = 8
N_EXPERTS = MOE_GROUPS * MOE_EXPERTS_PER_GROUP
MOE_TOP_K = 2
MOE_ROWS = 256
PROJ_TN = 1024
MOE_ID_ROWS = 8
SMEM_1D_TILE = 1024
WEIGHT_DMA_PRIORITY = 1
LN_EPS = 1e-5
DEPTH = 2
DEEPNORM_ALPHA = (2 * DEPTH) ** 0.25


def _dot(a, b):
    return jnp.dot(a, b, preferred_element_type=F32)


def _dot_nt(a, b):
    return lax.dot_general(a, b, (((1,), (1,)), ((), ())), preferred_element_type=F32)


def _dot_tn(a, b):
    return lax.dot_general(a, b, (((0,), (0,)), ((), ())), preferred_element_type=F32)


def _sigmoid(x):
    return 1.0 / (1.0 + jnp.exp(-x))


def _silu(x):
    return x * _sigmoid(x)


def _params(*sem):
    return pltpu.CompilerParams(dimension_semantics=sem, vmem_limit_bytes=VMEM_LIMIT_BYTES)


def _tiles_from_rows(x):
    n = x.shape[1] // LANES
    return jnp.swapaxes(jnp.stack([x[:, s * LANES:(s + 1) * LANES] for s in range(n)], axis=0), 0, 1)


def _rows_from_tiles(x3):
    xt = jnp.swapaxes(x3, 0, 1)
    return jnp.concatenate([xt[s] for s in range(xt.shape[0])], axis=-1)


def _layer_norm(z, g, b):
    mu = jnp.mean(z, axis=-1, keepdims=True)
    zc = z - mu
    var = jnp.mean(zc * zc, axis=-1, keepdims=True)
    return zc * lax.rsqrt(var + LN_EPS) * g + b


def _ada_kernel(s_ref, w_ref, b_ref, o_ref):
    s = _silu(s_ref[...]).astype(BF16)
    o_ref[0] = _dot(s, w_ref[0].astype(BF16)) + b_ref[0]


def _ada_mod(s, ada_w, ada_b, tn=1024):
    n_l, d, n = ada_w.shape
    return pl.pallas_call(
        _ada_kernel,
        grid=(n_l, n // tn),
        in_specs=[
            pl.BlockSpec((8, d), lambda l, j: (0, 0)),
            pl.BlockSpec((1, d, tn), lambda l, j: (l, 0, j)),
            pl.BlockSpec((1, 1, tn), lambda l, j: (l, 0, j)),
        ],
        out_specs=pl.BlockSpec((1, 8, tn), lambda l, j: (l, 0, j)),
        out_shape=jax.ShapeDtypeStruct((n_l, 8, n), F32),
        compiler_params=_params("parallel", "parallel"),
        name="ada_mod",
    )(s, ada_w, ada_b.reshape(n_l, 1, n))


def _rope(t, cos, sin_signed, first_half):
    partner = jnp.where(first_half, pltpu.roll(t, 96, 1), pltpu.roll(t, 32, 1))
    return t * cos + partner * sin_signed


def _modmm_kernel(x_ref, sc_ref, sh_ref, w_ref, cos_ref, sin_ref, oa_ref, ob_ref, xs_ref, *,
                  n_q, n_rope, n_a_tiles):
    j = pl.program_id(1)

    @pl.when(j == 0)
    def _():
        xs_ref[...] = (x_ref[...] * (1.0 + sc_ref[0]) + sh_ref[0]).astype(BF16)

    acc = _dot(xs_ref[...], w_ref[...])
    tn = acc.shape[1]

    for jt in range(n_a_tiles):
        @pl.when(j == jt)
        def _(jt=jt):
            lane = lax.broadcasted_iota(jnp.int32, (1, HEAD_DIM), 1)
            first_half = (lane % 64) < 32
            for h in range(tn // HEAD_DIM):
                sl = slice(h * HEAD_DIM, (h + 1) * HEAD_DIM)
                col = jt * tn + h * HEAD_DIM
                if col < n_rope:
                    scale = HEAD_DIM ** -0.5 if col < n_q else 1.0
                    oa_ref[:, sl] = _rope(acc[:, sl], cos_ref[...] * scale, sin_ref[...] * scale,
                                          first_half).astype(BF16)
                else:
                    oa_ref[:, sl] = acc[:, sl].astype(BF16)

    @pl.when(j >= n_a_tiles)
    def _():
        ob_ref[...] = acc


def _mod_matmul(x, sc, sh, w, cos, sin, col_map, n_a, n_b, n_q, n_rope, n_seq, tm, tn):
    m, k = x.shape
    rows_per_mod = m // sc.shape[0]
    ta, tb = n_a // tn, n_b // tn
    tab = pl.BlockSpec((tm, HEAD_DIM), lambda i, j: ((i * tm % n_seq) // tm, 0))
    return pl.pallas_call(
        functools.partial(_modmm_kernel, n_q=n_q, n_rope=n_rope, n_a_tiles=ta),
        grid=(m // tm, ta + tb),
        in_specs=[
            pl.BlockSpec((tm, k), lambda i, j: (i, 0)),
            pl.BlockSpec((1, 1, k), lambda i, j: (i * tm // rows_per_mod, 0, 0)),
            pl.BlockSpec((1, 1, k), lambda i, j: (i * tm // rows_per_mod, 0, 0)),
            pl.BlockSpec((k, tn), lambda i, j: (0, col_map(j))),
            tab, tab,
        ],
        out_specs=[pl.BlockSpec((tm, tn), lambda i, j: (i, jnp.minimum(j, ta - 1))),
                   pl.BlockSpec((tm, tn), lambda i, j: (i, jnp.maximum(j - ta, 0)))],
        out_shape=[jax.ShapeDtypeStruct((m, n_a), BF16), jax.ShapeDtypeStruct((m, n_b), F32)],
        scratch_shapes=[pltpu.VMEM((tm, k), BF16)],
        compiler_params=_params("parallel", "arbitrary"),
        name="mod_matmul",
    )(x, sc, sh, w, cos, sin)


def _attn_kernel(sink_ref, q_ref, kp_ref, kc_ref, kn_ref, vp_ref, vc_ref, vn_ref, kx_ref, vx_ref,
                 mp_ref, mn_ref, o_ref, *, n_blocks):
    n = pl.program_id(1)
    blk = ATT_BLOCK
    row1 = lax.broadcasted_iota(jnp.int32, (ATT_GROUP * blk, 1), 0)
    has_prev, has_next = n > 0, n < n_blocks - 1
    for h in range(ATT_KV_HEADS):
        kv = slice(h * HEAD_DIM, (h + 1) * HEAD_DIM)
        q2 = jnp.concatenate([q_ref[:, (ATT_GROUP * h + g) * HEAD_DIM:(ATT_GROUP * h + g + 1) * HEAD_DIM]
                              for g in range(ATT_GROUP)], axis=0)
        s_p = jnp.where(has_prev, _dot_nt(q2, kp_ref[:, kv]) + mp_ref[...], -jnp.inf)
        s_c = _dot_nt(q2, kc_ref[:, kv])
        s_n = jnp.where(has_next, _dot_nt(q2, kn_ref[:, kv]) + mn_ref[...], -jnp.inf)
        s_x = _dot_nt(q2, kx_ref[:, kv])
        sink = jnp.where(row1 < blk, sink_ref[ATT_GROUP * h], sink_ref[ATT_GROUP * h + 1])
        m = jnp.maximum(jnp.maximum(jnp.max(s_p, axis=-1, keepdims=True), jnp.max(s_c, axis=-1, keepdims=True)),
                        jnp.maximum(jnp.max(s_n, axis=-1, keepdims=True), jnp.max(s_x, axis=-1, keepdims=True)))
        m = jnp.maximum(m, sink)
        p_p, p_c, p_n, p_x = (jnp.exp(s - m) for s in (s_p, s_c, s_n, s_x))
        den = (jnp.sum(p_p, axis=-1, keepdims=True) + jnp.sum(p_c, axis=-1, keepdims=True)
               + jnp.sum(p_n, axis=-1, keepdims=True) + jnp.sum(p_x, axis=-1, keepdims=True)
               + jnp.exp(sink - m))
        o = (_dot(p_p.astype(BF16), vp_ref[:, kv]) + _dot(p_c.astype(BF16), vc_ref[:, kv])
             + _dot(p_n.astype(BF16), vn_ref[:, kv]) + _dot(p_x.astype(BF16), vx_ref[:, kv])) / den
        for g in range(ATT_GROUP):
            col = (ATT_GROUP * h + g) * HEAD_DIM
            o_ref[:, col:col + HEAD_DIM] = o[g * blk:(g + 1) * blk].astype(o_ref.dtype)


def _window_attention(qkv, kv_ctx, sink, batch, n_seq, n_ctx):
    assert ATT_GROUP == 2 and WINDOW == ATT_BLOCK
    nb = n_seq // ATT_BLOCK
    qw, kw = ATT_HEADS * HEAD_DIM, ATT_KV_HEADS * HEAD_DIM
    kcol, vcol = qw // kw, qw // kw + 1

    def rows(off):
        return lambda b, n, off=off: b * nb + jnp.clip(n + off, 0, nb - 1)

    kspec = [pl.BlockSpec((ATT_BLOCK, kw), lambda b, n, r=rows(o): (r(b, n), kcol)) for o in (-1, 0, 1)]
    vspec = [pl.BlockSpec((ATT_BLOCK, kw), lambda b, n, r=rows(o): (r(b, n), vcol)) for o in (-1, 0, 1)]
    r = jnp.arange(ATT_GROUP * ATT_BLOCK)[:, None] % ATT_BLOCK
    c = jnp.arange(ATT_BLOCK)[None, :]
    mask_prev = jnp.where(c >= r, 0.0, -jnp.inf).astype(F32)
    mask_next = jnp.where(c <= r, 0.0, -jnp.inf).astype(F32)
    mspec = pl.BlockSpec(mask_prev.shape, lambda b, n: (0, 0))
    return pl.pallas_call(
        functools.partial(_attn_kernel, n_blocks=nb),
        grid=(batch, nb),
        in_specs=[pl.BlockSpec(memory_space=pltpu.SMEM),
                  pl.BlockSpec((ATT_BLOCK, qw), lambda b, n: (b * nb + n, 0))]
        + kspec + vspec
        + [pl.BlockSpec((n_ctx, kw), lambda b, n: (b, 0)), pl.BlockSpec((n_ctx, kw), lambda b, n: (b, 1)),
           mspec, mspec],
        out_specs=pl.BlockSpec((ATT_BLOCK, qw), lambda b, n: (b * nb + n, 0)),
        out_shape=jax.ShapeDtypeStruct((batch * n_seq, qw), BF16),
        compiler_params=_params("parallel", "parallel"),
        name="window_attention",
    )(sink, qkv, qkv, qkv, qkv, qkv, qkv, qkv, kv_ctx, kv_ctx, mask_prev, mask_next)


def _gla_step(zf, q_raw, v, lb, st_ref, o_ref, rev):
    c_len = HG_CHUNK
    n_rows = zf.shape[0]
    n_sub = n_rows // c_len
    shift = c_len.bit_length() - 1
    order = range(n_sub - 1, -1, -1) if rev else range(n_sub)

    def head(h):
        return slice(h * HG_KEY, (h + 1) * HG_KEY)

    def chunk(i):
        return slice(i * c_len, (i + 1) * c_len)

    def seen(n):
        ri = lax.broadcasted_iota(jnp.int32, (n, n), 0)
        ci = lax.broadcasted_iota(jnp.int32, (n, n), 1)
        return ((ri >> shift) == (ci >> shift)) & ((ci >= ri) if rev else (ci <= ri))

    f = lb + (1.0 - lb) * _sigmoid(zf)
    k = 1.0 - f
    g = jnp.log(f)
    tri = jnp.where(seen(n_rows), 1.0, 0.0).astype(BF16)
    g1 = g.astype(BF16)
    r1 = g - g1.astype(F32)
    g2 = r1.astype(BF16)
    g3 = (r1 - g2.astype(F32)).astype(BF16)
    c = _dot(tri, g1) + _dot(tri, g2) + _dot(tri, g3)
    c_end = [c[i * c_len:i * c_len + 1] if rev else c[(i + 1) * c_len - 1:(i + 1) * c_len] for i in range(n_sub)]
    c_end_rows = jnp.concatenate([jnp.broadcast_to(ce, (c_len, ce.shape[1])) for ce in c_end], axis=0)
    k_end = (k * jnp.exp(c_end_rows - c)).astype(BF16)
    dec = [jnp.exp(ce) for ce in c_end]
    vb = v.astype(BF16)

    def advance(states, i):
        new = []
        for p in range(0, HG_HEADS, 2):
            lanes = slice(p * HG_KEY, (p + 2) * HG_KEY)
            inc = _dot_tn(vb[chunk(i), lanes], k_end[chunk(i), lanes])
            for j in range(2):
                blk = slice(j * HG_KEY, (j + 1) * HG_KEY)
                new.append(states[p + j] * dec[i][:, head(p + j)] + inc[blk, blk])
        return new

    if o_ref is None:
        states = [st_ref[h] for h in range(HG_HEADS)]
        for i in order:
            states = advance(states, i)
        for h in range(HG_HEADS):
            st_ref[h] = states[h]
        return

    q = _silu(q_raw)
    q_in = (q * jnp.exp(c)).astype(BF16)
    lowest = functools.reduce(jnp.minimum, c_end)
    in_range = jnp.min(lowest) >= -HG_FAST_RANGE

    def stack(x, i, h0, n):
        return jnp.concatenate([x[chunk(i), head(h)] for h in range(h0, h0 + n)], axis=0)

    @pl.when(in_range)
    def _():
        k_in = (k * jnp.exp(-c)).astype(BF16)
        n_qk = MXU_TILE // c_len
        same = seen(n_qk * c_len)
        intra = {}
        for i in order:
            for h0 in range(0, HG_HEADS, n_qk):
                sc = jnp.where(same, _dot_nt(stack(q_in, i, h0, n_qk), stack(k_in, i, h0, n_qk)), 0.0)
                pv = _dot(sc.astype(BF16), stack(vb, i, h0, n_qk))
                for j in range(n_qk):
                    intra[i, h0 + j] = pv[j * c_len:(j + 1) * c_len]
        states = [st_ref[h] for h in range(HG_HEADS)]
        for i in order:
            out = []
            for p in range(0, HG_HEADS, 2):
                st_pair = jnp.concatenate([states[p].astype(BF16), states[p + 1].astype(BF16)], axis=0)
                inter = _dot_nt(stack(q_in, i, p, 2), st_pair)
                for j in range(2):
                    out.append(inter[j * c_len:(j + 1) * c_len, j * HG_KEY:(j + 1) * HG_KEY] + intra[i, p + j])
            o_ref[chunk(i), :] = jnp.concatenate(out, axis=1)
            states = advance(states, i)
        for h in range(HG_HEADS):
            st_ref[h] = states[h]

    @pl.when(jnp.logical_not(in_range))
    def _():
        for i in order:
            r = chunk(i)
            _gla_intra_exact(q[r], k[r], v[r], vb[r], c[r], q_in[r], st_ref, o_ref, r, rev)
            states = advance([st_ref[h] for h in range(HG_HEADS)], i)
            for h in range(HG_HEADS):
                st_ref[h] = states[h]


def _gla_intra_exact(q, k, v, vb, c, q_in, st_ref, o_ref, rows, rev):
    c_len = q.shape[0]
    pairs = []
    size = c_len // 2
    while size >= HG_SUB:
        for lo in range(0, c_len, 2 * size):
            pairs.append((lo, lo + size, lo + 2 * size))
        size //= 2
    scaled = []
    for lo, mid, hi in pairs:
        if rev:
            late, early, bnd = slice(lo, mid), slice(mid, hi), mid
        else:
            late, early, bnd = slice(mid, hi), slice(lo, mid), mid - 1
        cb = c[bnd:bnd + 1]
        q_l = (q[late] * jnp.exp(c[late] - cb)).astype(BF16)
        k_e = (k[early] * jnp.exp(cb - c[early])).astype(BF16)
        scaled.append((late, early, q_l, k_e))
    n_sub = c_len // HG_SUB
    t_idx = lax.broadcasted_iota(jnp.int32, (HG_SUB, 1), 0)
    diag = [[None] * HG_HEADS for _ in range(n_sub)]
    for b in range(n_sub):
        r0 = b * HG_SUB
        qb, cb = q[r0:r0 + HG_SUB], c[r0:r0 + HG_SUB]
        for s in range(HG_SUB):
            row = r0 + s
            ok = (t_idx <= s) if rev else (t_idx >= s)
            w = qb * k[row:row + 1] * jnp.exp(jnp.where(ok, cb - c[row:row + 1], -jnp.inf))
            for h in range(HG_HEADS):
                sl = slice(h * HG_KEY, (h + 1) * HG_KEY)
                contrib = jnp.sum(w[:, sl], axis=-1, keepdims=True) * v[row:row + 1, sl]
                diag[b][h] = contrib if diag[b][h] is None else diag[b][h] + contrib

    for h in range(HG_HEADS):
        sl = slice(h * HG_KEY, (h + 1) * HG_KEY)
        o_h = _dot_nt(q_in[:, sl], st_ref[h].astype(BF16))
        parts = [diag[b][h] for b in range(n_sub)]
        for late, early, q_l, k_e in scaled:
            sc = _dot_nt(q_l[:, sl], k_e[:, sl]).astype(BF16)
            add = _dot(sc, vb[early, sl])
            b0 = late.start // HG_SUB
            for j in range((late.stop - late.start) // HG_SUB):
                parts[b0 + j] = parts[b0 + j] + add[j * HG_SUB:(j + 1) * HG_SUB]
        o_ref[rows, sl] = o_h + jnp.concatenate(parts, axis=0)


def _gla_kernel(lb_ref, zf_ref, q_ref, v_ref, zfc_ref, vc_ref, o_ref, st_ref, *, rev, n_ctx_steps):
    s = pl.program_id(1)

    @pl.when(s == 0)
    def _():
        st_ref[...] = jnp.zeros_like(st_ref)

    x = lb_ref[...]
    e = jnp.exp(x - jnp.max(x, axis=0, keepdims=True))
    lb = e[0:1] / jnp.sum(e, axis=0, keepdims=True)

    @pl.when(s < n_ctx_steps)
    def _():
        _gla_step(zfc_ref[...], None, vc_ref[...], lb, st_ref, None, rev)

    @pl.when(s >= n_ctx_steps)
    def _():
        _gla_step(zf_ref[...], q_ref[...], v_ref[...], lb, st_ref, o_ref, rev)


def _hgrn2_scan(p, pc, hg_lb, batch, n_seq, n_ctx, rev):
    hk = HG_HEADS * HG_KEY
    rows = HG_CHUNK * HG_STEP_CHUNKS
    assert n_seq % rows == 0 and n_ctx % rows == 0
    nc, ncc = n_seq // rows, n_ctx // rows
    d = 1 if rev else 0

    def lat(b, s):
        j = jnp.maximum(s - ncc, 0)
        return b * nc + (nc - 1 - j if rev else j)

    def ctx(b, s):
        j = jnp.minimum(s, ncc - 1)
        return b * ncc + (ncc - 1 - j if rev else j)

    return pl.pallas_call(
        functools.partial(_gla_kernel, rev=rev, n_ctx_steps=ncc),
        grid=(batch, ncc + nc),
        in_specs=[
            pl.BlockSpec((None, hg_lb.shape[1], hk), lambda b, s: (d, 0, 0)),
            pl.BlockSpec((rows, hk), lambda b, s: (lat(b, s), 1 + d)),
            pl.BlockSpec((rows, hk), lambda b, s: (lat(b, s), 0)),
            pl.BlockSpec((rows, hk), lambda b, s: (lat(b, s), 3)),
            pl.BlockSpec((rows, hk), lambda b, s: (ctx(b, s), d)),
            pl.BlockSpec((rows, hk), lambda b, s: (ctx(b, s), 2)),
        ],
        out_specs=pl.BlockSpec((rows, hk), lambda b, s: (lat(b, s), 0)),
        out_shape=jax.ShapeDtypeStruct((batch * n_seq, hk), F32),
        scratch_shapes=[pltpu.VMEM((HG_HEADS, HG_KEY, HG_KEY), F32)],
        compiler_params=_params("parallel", "arbitrary"),
        name="hgrn2_bwd" if rev else "hgrn2_fwd",
    )(hg_lb, p, p, p, pc, pc)


def _route(tok, wr_hi, wr_lo, rb):
    t_hi = tok.astype(BF16)
    t_lo = (tok - t_hi.astype(F32)).astype(BF16)
    lg = _dot(t_hi, wr_hi) + _dot(t_hi, wr_lo) + _dot(t_lo, wr_hi) + rb
    lane = lax.broadcasted_iota(jnp.int32, lg.shape, 1)
    lane_f = lane.astype(F32)
    ninf = -jnp.inf
    gl = jnp.where(lane < MOE_GROUPS, lg, ninf)
    gmax = jnp.max(gl, axis=-1, keepdims=True)
    g_idx = jnp.min(jnp.where(gl == gmax, lane_f, float(LANES)), axis=-1, keepdims=True)
    g_val = 1.0 / jnp.sum(jnp.exp(gl - gmax), axis=-1, keepdims=True)
    e_lane = lane_f - float(MOE_GROUPS)
    lo = g_idx * float(MOE_EXPERTS_PER_GROUP)
    in_grp = (e_lane >= lo) & (e_lane < lo + float(MOE_EXPERTS_PER_GROUP))
    el = jnp.where(in_grp, lg, ninf)
    l1 = jnp.max(el, axis=-1, keepdims=True)
    i1 = jnp.min(jnp.where(el == l1, e_lane, float(LANES)), axis=-1, keepdims=True)
    el2 = jnp.where(e_lane == i1, ninf, el)
    l2 = jnp.max(el2, axis=-1, keepdims=True)
    i2 = jnp.min(jnp.where(el2 == l2, e_lane, float(LANES)), axis=-1, keepdims=True)
    r = jnp.exp(l2 - l1)
    w1 = g_val / (1.0 + r)
    w2 = w1 * r
    eid = jnp.where(lane == 0, i1, jnp.where(lane == 1, i2, 0.0))
    gate = jnp.where(lane == 0, w1, jnp.where(lane == 1, w2, 0.0))
    eid_t = jnp.transpose(eid)[:MOE_ID_ROWS].astype(jnp.int32)
    return eid_t, gate


def _post_mix(y, x_ref, g1_ref, sc2_ref, sh2_ref, lng_ref, lnb_ref, wrh_ref, wrl_ref, rb_ref,
              x1_ref, tok_ref, eid_ref, gate_ref):
    x1 = _layer_norm(DEEPNORM_ALPHA * x_ref[...] + g1_ref[0] * y, lng_ref[...], lnb_ref[...])
    x1_ref[...] = x1
    tok = x1 * (1.0 + sc2_ref[0]) + sh2_ref[0]
    tok_ref[...] = _tiles_from_rows(tok).astype(BF16)
    eid, gate = _route(tok, wrh_ref[...], wrl_ref[...], rb_ref[...])
    eid_ref[...] = eid
    gate_ref[...] = gate


def _even_out_kernel(att_ref, of_ref, ob_ref, gt_ref, ng_ref, wo_ref, *rest):
    o = of_ref[...] + ob_ref[...]
    pieces = []
    for h in range(HG_HEADS):
        oh = o[:, h * HG_KEY:(h + 1) * HG_KEY]
        pieces.append(oh * lax.rsqrt(jnp.mean(oh * oh, axis=-1, keepdims=True) + NORM_EPS))
    hg = (jnp.concatenate(pieces, axis=-1) * ng_ref[...] * _silu(gt_ref[...])).astype(BF16)
    n_att = att_ref.shape[1]
    y = _dot(att_ref[...], wo_ref[:n_att, :]) + _dot(hg, wo_ref[n_att:, :])
    _post_mix(y, *rest)


def _post_specs(d, tm, rows_per_batch):
    def bmap(i):
        return (i * tm // rows_per_batch, 0, 0)

    row = pl.BlockSpec((tm, d), lambda i: (i, 0))
    mod = pl.BlockSpec((1, 1, d), bmap)
    vec = pl.BlockSpec((1, d), lambda i: (0, 0))
    rw = pl.BlockSpec((d, LANES), lambda i: (0, 0))
    in_specs = [row, mod, mod, mod, vec, vec, rw, rw, pl.BlockSpec((1, LANES), lambda i: (0, 0))]
    lane_blk = pl.BlockSpec((tm, LANES), lambda i: (i, 0))
    tiles = pl.BlockSpec((tm, d // LANES, LANES), lambda i: (i, 0, 0))
    out_specs = [row, tiles, pl.BlockSpec((MOE_ID_ROWS, tm), lambda i: (0, i)), lane_blk]
    return in_specs, out_specs


def _post_out_shapes(t, d):
    return [jax.ShapeDtypeStruct((t, d), F32), jax.ShapeDtypeStruct((t, d // LANES, LANES), BF16),
            jax.ShapeDtypeStruct((MOE_ID_ROWS, t), jnp.int32), jax.ShapeDtypeStruct((t, LANES), F32)]


def _even_out(att, o_f, o_b, p, norm_g, w_out, x, g1, sc2, sh2, lng, lnb, wr_hi, wr_lo, rb, rows_per_batch, tm=256):
    t, d = x.shape
    hv = o_f.shape[1]
    post_in, post_out = _post_specs(d, tm, rows_per_batch)
    return pl.pallas_call(
        _even_out_kernel,
        grid=(t // tm,),
        in_specs=[
            pl.BlockSpec((tm, att.shape[1]), lambda i: (i, 0)),
            pl.BlockSpec((tm, hv), lambda i: (i, 0)),
            pl.BlockSpec((tm, hv), lambda i: (i, 0)),
            pl.BlockSpec((tm, hv), lambda i: (i, 4)),
            pl.BlockSpec((1, hv), lambda i: (0, 0)),
            pl.BlockSpec(w_out.shape, lambda i: (0, 0)),
        ] + post_in,
        out_specs=post_out,
        out_shape=_post_out_shapes(t, d),
        compiler_params=_params("parallel"),
        name="even_out",
    )(att, o_f, o_b, p, norm_g, w_out, x, g1, sc2, sh2, lng, lnb, wr_hi, wr_lo, rb)


def _combine(x_ref, ya_ref, yb_ref, gate_ref, g2_ref, lng_ref, lnb_ref):
    gate = gate_ref[...]
    y = (gate[:, 0:1] * _rows_from_tiles(ya_ref[...].astype(F32))
         + gate[:, 1:2] * _rows_from_tiles(yb_ref[...].astype(F32)))
    return _layer_norm(DEEPNORM_ALPHA * x_ref[...] + g2_ref[0] * y, lng_ref[...], lnb_ref[...])


def _combine_proj_kernel(x_ref, ya_ref, yb_ref, gate_ref, g2_ref, lng_ref, lnb_ref, sc_ref, sh_ref, w_ref,
                         x2_ref, u_ref):
    x2 = _combine(x_ref, ya_ref, yb_ref, gate_ref, g2_ref, lng_ref, lnb_ref)
    x2_ref[...] = x2
    u_ref[...] = _dot((x2 * (1.0 + sc_ref[0]) + sh_ref[0]).astype(BF16), w_ref[...])


def _combine_kernel(x_ref, ya_ref, yb_ref, gate_ref, g2_ref, lng_ref, lnb_ref, x2_ref):
    x2_ref[...] = _combine(x_ref, ya_ref, yb_ref, gate_ref, g2_ref, lng_ref, lnb_ref)


def _combine_call(x1, y2, gate, g2, lng, lnb, rows_per_batch, proj=None, tm=256):
    t, d = x1.shape
    nt = t // tm

    def bmap(i):
        return (i * tm // rows_per_batch, 0, 0)

    row = pl.BlockSpec((tm, d), lambda i: (i, 0))
    mod = pl.BlockSpec((1, 1, d), bmap)
    vec = pl.BlockSpec((1, d), lambda i: (0, 0))
    in_specs = [row, pl.BlockSpec((tm, d // LANES, LANES), lambda i: (i, 0, 0)),
                pl.BlockSpec((tm, d // LANES, LANES), lambda i: (nt + i, 0, 0)),
                pl.BlockSpec((tm, LANES), lambda i: (i, 0)), mod, vec, vec]
    args = [x1, y2, y2, gate, g2, lng, lnb]
    if proj is None:
        return pl.pallas_call(
            _combine_kernel, grid=(nt,), in_specs=in_specs, out_specs=row,
            out_shape=jax.ShapeDtypeStruct((t, d), F32),
            compiler_params=_params("parallel"), name="combine_ln")(*args)
    sc, sh, w = proj
    return pl.pallas_call(
        _combine_proj_kernel, grid=(nt,),
        in_specs=in_specs + [mod, mod, pl.BlockSpec(w.shape, lambda i: (0, 0))],
        out_specs=[row, pl.BlockSpec((tm, w.shape[1]), lambda i: (i, 0))],
        out_shape=[jax.ShapeDtypeStruct((t, d), F32), jax.ShapeDtypeStruct((t, w.shape[1]), F32)],
        compiler_params=_params("parallel"), name="combine_ln_proj")(*args, sc, sh, w)


def _pool_out_kernel(up_ref, uc_ref, un_ref, wg_ref, ps_ref, wo_ref, *rest, n_seq):
    tm, d = uc_ref.shape
    n_grp = len(POOL_WINDOWS)
    ch = d // n_grp
    halo = POOL_HALO
    pos0 = (pl.program_id(0) * tm) % n_seq
    e_pos = pos0 - halo + lax.broadcasted_iota(jnp.int32, (tm + 2 * halo, 1), 0)
    e_ok = (e_pos >= 0) & (e_pos < n_seq)
    t_pos = pos0 + lax.broadcasted_iota(jnp.int32, (tm, 1), 0)
    y = None
    for gi, w in enumerate(POOL_WINDOWS):
        cs = slice(gi * ch, (gi + 1) * ch)
        u = uc_ref[:, cs]
        ext = jnp.where(e_ok, jnp.concatenate([up_ref[:, cs], u, un_ref[:, cs]], axis=0), 0.0)
        a, span = ext, 1
        while span < w:
            a = a[:a.shape[0] - span] + a[span:]
            span *= 2
        start = halo - w // 2
        win = a[start:start + tm]
        cnt = (jnp.minimum(t_pos + (w - w // 2), n_seq) - jnp.maximum(t_pos - w // 2, 0)).astype(F32)
        mixed = (win / cnt - u).astype(BF16)
        z = (_dot(mixed, wg_ref[gi]) * ps_ref[:, cs]).astype(BF16)
        part = _dot(z, wo_ref[cs, :])
        y = part if y is None else y + part
    _post_mix(y, *rest)


def _pool_out(u, w_grp, scale, w_out, x, g1, sc2, sh2, lng, lnb, wr_hi, wr_lo, rb, n_seq, tm=256):
    t, d = x.shape
    hb = tm // POOL_HALO
    n_hb = t // POOL_HALO
    post_in, post_out = _post_specs(d, tm, n_seq)
    return pl.pallas_call(
        functools.partial(_pool_out_kernel, n_seq=n_seq),
        grid=(t // tm,),
        in_specs=[
            pl.BlockSpec((POOL_HALO, d), lambda i: (jnp.maximum(i * hb - 1, 0), 0)),
            pl.BlockSpec((tm, d), lambda i: (i, 0)),
            pl.BlockSpec((POOL_HALO, d), lambda i: (jnp.minimum((i + 1) * hb, n_hb - 1), 0)),
            pl.BlockSpec(w_grp.shape, lambda i: (0, 0, 0)),
            pl.BlockSpec((1, d), lambda i: (0, 0)),
            pl.BlockSpec(w_out.shape, lambda i: (0, 0)),
        ] + post_in,
        out_specs=post_out,
        out_shape=_post_out_shapes(t, d),
        compiler_params=_params("parallel"),
        name="pool_out",
    )(u, u, u, w_grp, scale, w_out, x, g1, sc2, sh2, lng, lnb, wr_hi, wr_lo, rb)


def _moe_kernel(be_ref, nv_ref, first_ref, ws_ref, nxt_ref, idx_ref, idxn_ref, idxp_ref, tok_hbm, w1_hbm, w3_hbm, w2_hbm,
                y_hbm, xbuf, ybuf, xb_ref, wf1, wf3, wf2, w1b, w3b, w2b, gsem, ssem, wsem, *, n_tok, layer, n_blocks):
    i = pl.program_id(0)
    used = nv_ref[jnp.minimum(i, n_blocks - 1)] > 0
    used = used & (i < n_blocks)
    prev_used = (i > 0) & (nv_ref[jnp.maximum(i - 1, 0)] > 0)
    xs = i % 2

    def weight_copies(e, ws):
        return (pltpu.make_async_copy(w1_hbm.at[layer, e], wf1.at[ws], wsem.at[ws]),
                pltpu.make_async_copy(w3_hbm.at[layer, e], wf3.at[ws], wsem.at[ws]),
                pltpu.make_async_copy(w2_hbm.at[layer, e], wf2.at[ws], wsem.at[ws]))

    def gather_start(idx, slot):
        for r in range(MOE_ROWS):
            tok = idx[0, 0, r] & (n_tok - 1)
            pltpu.make_async_copy(tok_hbm.at[tok], xbuf.at[slot, r], gsem.at[slot]).start()

    def gather_wait(slot):
        pltpu.make_async_copy(tok_hbm.at[pl.ds(0, MOE_ROWS)], xbuf.at[slot], gsem.at[slot]).wait()

    def scatter_start(idx, slot):
        for r in range(MOE_ROWS):
            pltpu.make_async_copy(ybuf.at[slot, r], y_hbm.at[idx[0, 0, r]], ssem.at[slot]).start(priority=r % 2)

    def scatter_wait(slot):
        pltpu.make_async_copy(ybuf.at[slot], y_hbm.at[pl.ds(0, MOE_ROWS)], ssem.at[slot]).wait()

    @pl.when(i == 0)
    def _():
        xbuf[...] = jnp.zeros_like(xbuf)
        ybuf[...] = jnp.zeros_like(ybuf)
        spare0 = pltpu.make_async_copy(
            ybuf.at[0], y_hbm.at[pl.ds(MOE_TOP_K * n_tok, MOE_ROWS)], ssem.at[0])
        spare0.start()
        for cp in weight_copies(be_ref[0], 0):
            cp.start(priority=WEIGHT_DMA_PRIORITY)
        gather_start(idx_ref, 0)

    @pl.when(used)
    def _():
        ws = ws_ref[i]

        @pl.when(first_ref[i] == 1)
        def _():
            for cp in weight_copies(be_ref[i], ws):
                cp.wait()
            nxt = nxt_ref[i]

            @pl.when(nxt >= 0)
            def _():
                for cp in weight_copies(nxt, 1 - ws):
                    cp.start(priority=WEIGHT_DMA_PRIORITY)

            w1b[...] = wf1[ws].astype(BF16)
            w3b[...] = wf3[ws].astype(BF16)
            w2b[...] = wf2[ws].astype(BF16)

        gather_wait(xs)
        xb_ref[...] = _rows_from_tiles(xbuf[xs].astype(F32)).astype(BF16)
        gather_start(idxn_ref, 1 - xs)
        scatter_start(idxp_ref, 1 - xs)
        xb = xb_ref[...]
        h = (_silu(_dot(xb, w1b[...])) * _dot(xb, w3b[...])).astype(BF16)
        y = _tiles_from_rows(_dot(h, w2b[...])).astype(BF16)
        scatter_wait(xs)
        ybuf[xs] = y

    @pl.when(jnp.logical_not(used) & prev_used)
    def _():
        gather_wait(xs)
        scatter_start(idxp_ref, 1 - xs)
        scatter_wait(1 - xs)
        scatter_wait(xs)


def _moe_dispatch(eid_t, n_blocks):
    n_tok = eid_t.shape[1]
    n_slots = (n_blocks + 2) * MOE_ROWS
    slot_rows = -(-n_slots // SMEM_1D_TILE)
    assert n_blocks <= LANES and n_tok % SMEM_1D_TILE == 0 and SMEM_1D_TILE % MOE_ROWS == 0
    slot, meta = pl.pallas_call(
        functools.partial(_dispatch_kernel, n_tok=n_tok),
        in_specs=[pl.BlockSpec(memory_space=pltpu.VMEM)],
        out_specs=[pl.BlockSpec(memory_space=pltpu.SMEM), pl.BlockSpec(memory_space=pltpu.VMEM)],
        out_shape=[jax.ShapeDtypeStruct((slot_rows * SMEM_1D_TILE,), jnp.int32),
                   jax.ShapeDtypeStruct((MOE_ID_ROWS, LANES), jnp.int32)],
        scratch_shapes=[pltpu.VMEM((MOE_TOP_K * n_tok,), jnp.int32),
                        pltpu.VMEM((slot_rows * SMEM_1D_TILE,), jnp.int32),
                        pltpu.SMEM((MOE_TOP_K * n_tok,), jnp.int32),
                        pltpu.SemaphoreType.DMA(())],
        name="moe_dispatch",
    )(eid_t)
    return (slot[:n_slots].reshape(n_blocks + 2, 1, MOE_ROWS),) + tuple(meta[r, :n_blocks] for r in range(5))


def _dispatch_kernel(eid_ref, slot_ref, meta_ref, dest_vmem, init_vmem, dest_smem, sem, *, n_tok):
    tile = MOE_ROWS
    n_tiles = n_tok // tile
    sub = lax.broadcasted_iota(jnp.int32, (N_EXPERTS, tile), 0)
    si = lax.broadcasted_iota(jnp.int32, (tile, tile), 0)
    ti = lax.broadcasted_iota(jnp.int32, (tile, tile), 1)
    before = jnp.where(si < ti, 1.0, 0.0).astype(BF16)

    def one_hots(j):
        ids = eid_ref[:, j * tile:(j + 1) * tile]
        return [jnp.where(sub == ids[k:k + 1], 1.0, 0.0) for k in range(MOE_TOP_K)]

    carry = jnp.zeros((N_EXPERTS, 1), F32)
    ranks = []
    for j in range(n_tiles):
        oh = one_hots(j)
        both = oh[0] + oh[1]
        seen = carry + _dot(both.astype(BF16), before)
        ranks.append([jnp.sum(seen * o, axis=0, keepdims=True) for o in oh])
        carry = carry + jnp.sum(both, axis=1, keepdims=True)

    counts = carry
    nblk = jnp.floor((counts + float(MOE_ROWS - 1)) * (1.0 / MOE_ROWS))
    ei = lax.broadcasted_iota(jnp.int32, (N_EXPERTS, N_EXPERTS), 0)
    ej = lax.broadcasted_iota(jnp.int32, (N_EXPERTS, N_EXPERTS), 1)
    lower = jnp.where(ej < ei, 1.0, 0.0).astype(BF16)
    first_blk = _dot(lower, jnp.broadcast_to(nblk, (N_EXPERTS, LANES)).astype(BF16))[:, 0:1]
    first_slot = first_blk * float(MOE_ROWS)

    per_row = SMEM_1D_TILE // tile
    for k in range(MOE_TOP_K):
        for q in range(n_tiles // per_row):
            parts = []
            for j in range(q * per_row, (q + 1) * per_row):
                parts.append(jnp.sum(first_slot * one_hots(j)[k], axis=0, keepdims=True) + ranks[j][k])
            dest = jnp.concatenate(parts, axis=1).astype(jnp.int32)
            dest_vmem[pl.ds(k * n_tok + q * SMEM_1D_TILE, SMEM_1D_TILE)] = dest.reshape(SMEM_1D_TILE)

    lane = lax.broadcasted_iota(jnp.int32, (1, SMEM_1D_TILE), 1)
    for q in range(init_vmem.shape[0] // SMEM_1D_TILE):
        pos = q * SMEM_1D_TILE + lane
        spare = MOE_TOP_K * n_tok + ((pos // MOE_ROWS + 1) % 2) * MOE_ROWS + pos % MOE_ROWS
        init_vmem[pl.ds(q * SMEM_1D_TILE, SMEM_1D_TILE)] = spare.reshape(SMEM_1D_TILE)
    copies = [pltpu.make_async_copy(dest_vmem, dest_smem, sem), pltpu.make_async_copy(init_vmem, slot_ref, sem)]
    for cp in copies:
        cp.start()
    for cp in copies:
        cp.wait()

    def place(t, carry_):
        for k in range(MOE_TOP_K):
            slot_ref[dest_smem[k * n_tok + t] + MOE_ROWS] = k * n_tok + t
        return carry_

    lax.fori_loop(0, n_tok, place, 0, unroll=8)

    b = lax.broadcasted_iota(jnp.int32, (N_EXPERTS, LANES), 1).astype(F32)
    e_col = lax.broadcasted_iota(jnp.int32, (N_EXPERTS, LANES), 0).astype(F32)
    b_row = b[0:1]
    be = jnp.minimum(jnp.sum(jnp.where(first_blk + nblk <= b, 1.0, 0.0), axis=0, keepdims=True), N_EXPERTS - 1.0)
    mine = e_col == be
    cnt_b = jnp.sum(jnp.where(mine, counts, 0.0), axis=0, keepdims=True)
    start_b = jnp.sum(jnp.where(mine, first_blk, 0.0), axis=0, keepdims=True)
    nv = jnp.clip(cnt_b - (b_row - start_b) * MOE_ROWS, 0.0, float(MOE_ROWS))
    nv = jnp.where(b_row < jnp.sum(nblk, axis=0, keepdims=True), nv, 0.0)
    first = jnp.where((nv > 0) & ((b_row == 0) | (be != pltpu.roll(be, 1, 1))), 1.0, 0.0)
    li = lax.broadcasted_iota(jnp.int32, (LANES, LANES), 0)
    lj = lax.broadcasted_iota(jnp.int32, (LANES, LANES), 1)
    upto = jnp.where(li <= lj, 1.0, 0.0).astype(BF16)
    run = _dot(jnp.broadcast_to(first, (MOE_ID_ROWS, LANES)).astype(BF16), upto)[0:1] - 1.0
    ws = run - 2.0 * jnp.floor(run * 0.5)
    later = jnp.min(jnp.where((e_col > be) & (counts > 0), e_col, float(LANES)), axis=0, keepdims=True)
    nxt = jnp.where(later >= float(N_EXPERTS), -1.0, later)
    rows = [be, nv, first, ws, nxt] + [jnp.zeros_like(be)] * (MOE_ID_ROWS - 5)
    meta_ref[...] = jnp.concatenate(rows, axis=0).astype(jnp.int32)


def _moe_experts(tok, eid, w1, w3, w2, layer):
    n_tok, n_sub, _ = tok.shape
    d = n_sub * LANES
    assert n_tok & (n_tok - 1) == 0
    ff = w1.shape[3]
    n_assign = n_tok * MOE_TOP_K
    n_blocks = -(-(n_assign + N_EXPERTS * (MOE_ROWS - 1)) // MOE_ROWS)
    slot, be, nv, first, ws, nxt = _moe_dispatch(eid, n_blocks)
    grid_spec = pltpu.PrefetchScalarGridSpec(
        num_scalar_prefetch=5,
        grid=(n_blocks + 1,),
        in_specs=[
            pl.BlockSpec((1, 1, MOE_ROWS), lambda i, *_: (i + 1, 0, 0), memory_space=pltpu.SMEM),
            pl.BlockSpec((1, 1, MOE_ROWS), lambda i, *_: (jnp.minimum(i + 2, n_blocks + 1), 0, 0),
                         memory_space=pltpu.SMEM),
            pl.BlockSpec((1, 1, MOE_ROWS), lambda i, *_: (i, 0, 0), memory_space=pltpu.SMEM),
            pl.BlockSpec(memory_space=pl.ANY),
            pl.BlockSpec(memory_space=pl.ANY),
            pl.BlockSpec(memory_space=pl.ANY),
            pl.BlockSpec(memory_space=pl.ANY),
        ],
        out_specs=pl.BlockSpec(memory_space=pl.ANY),
        scratch_shapes=[
            pltpu.VMEM((2, MOE_ROWS, n_sub, LANES), BF16), pltpu.VMEM((2, MOE_ROWS, n_sub, LANES), BF16),
            pltpu.VMEM((MOE_ROWS, d), BF16),
            pltpu.VMEM((2, d, ff), F32), pltpu.VMEM((2, d, ff), F32), pltpu.VMEM((2, ff, d), F32),
            pltpu.VMEM((d, ff), BF16), pltpu.VMEM((d, ff), BF16), pltpu.VMEM((ff, d), BF16),
            pltpu.SemaphoreType.DMA((2,)), pltpu.SemaphoreType.DMA((2,)), pltpu.SemaphoreType.DMA((2,)),
        ],
    )
    return pl.pallas_call(
        functools.partial(_moe_kernel, n_tok=n_tok, layer=layer, n_blocks=n_blocks),
        grid_spec=grid_spec,
        out_shape=jax.ShapeDtypeStruct((MOE_TOP_K * n_tok + 2 * MOE_ROWS, n_sub, LANES), BF16),
        compiler_params=pltpu.CompilerParams(dimension_semantics=("arbitrary",),
                                             vmem_limit_bytes=MOE_VMEM_LIMIT_BYTES),
        name="moe_experts",
    )(be, nv, first, ws, nxt, slot, slot, slot, tok, w1, w3, w2)


def _rope_tables(n_seq):
    half = HEAD_DIM // 2
    n_freq = half // 2
    t = jnp.arange(n_seq)
    row = (t // GRID_W).astype(F32)
    col = (t % GRID_W).astype(F32)
    inv_freq = ROPE_BASE ** (-jnp.arange(n_freq, dtype=F32) / n_freq)
    ang_r = row[:, None] * inv_freq[None, :]
    ang_c = col[:, None] * inv_freq[None, :]
    cos = jnp.concatenate([jnp.cos(ang_r)] * 2 + [jnp.cos(ang_c)] * 2, axis=-1)
    sin = jnp.concatenate([-jnp.sin(ang_r), jnp.sin(ang_r), -jnp.sin(ang_c), jnp.sin(ang_c)], axis=-1)
    return cos, sin


def _router_weights(w_g, b_g, w_e, b_e):
    d = w_g.shape[0]
    n = w_g.shape[1] + w_e.shape[1]
    wr = jnp.concatenate([w_g, w_e, jnp.zeros((d, LANES - n), F32)], axis=1)
    rb = jnp.concatenate([b_g, b_e, jnp.zeros((LANES - n,), F32)]).reshape(1, LANES)
    hi = wr.astype(BF16)
    lo = (wr - hi.astype(F32)).astype(BF16)
    return hi, lo, rb


def kernel(x, c, ctx, c_ctx, ada_w, ada_b, ln_g, ln_b, mix_w_in, att_sink, hg_lb, hg_norm_g, mix_w_out, pool_w_in, pool_w_grp, pool_scale, pool_w_out, rt_group_w, rt_group_b, rt_expert_w, rt_expert_b, moe_w1, moe_w3, moe_w2):
    b, n, d = x.shape
    n_ctx = ctx.shape[1]
    t = b * n
    xf = x.reshape(t, d)
    ctxf = ctx.reshape(b * n_ctx, d)

    cond = jnp.concatenate([c, c_ctx[None, :], jnp.zeros((8 - b - 1, d), F32)], axis=0)
    mod = _ada_mod(cond, ada_w, ada_b)

    def chunk(l, j, rows=slice(0, b)):
        return mod[l, rows, j * d:(j + 1) * d][:, None, :]

    w_in = mix_w_in[0].astype(BF16)
    cos, sin = _rope_tables(n)
    q_w, kv_w = ATT_HEADS * HEAD_DIM, ATT_KV_HEADS * HEAD_DIM
    n_att = q_w + 2 * kv_w
    qkv, p = _mod_matmul(xf, chunk(0, 1), chunk(0, 0), w_in, cos, sin, lambda j: j, n_att, w_in.shape[1] - n_att,
                         n_q=q_w, n_rope=q_w + kv_w, n_seq=n, tm=1024, tn=PROJ_TN)
    ctx_rows = slice(b, b + 1)
    kv_ctx, pc = _mod_matmul(ctxf, chunk(0, 1, ctx_rows), chunk(0, 0, ctx_rows), w_in, cos, sin,
                             lambda j: jnp.where(j < 1, j + 1, j + 2), 2 * kv_w, 3 * HG_HEADS * HG_KEY,
                             n_q=0, n_rope=0, n_seq=n, tm=b * n_ctx, tn=PROJ_TN)
    att = _window_attention(qkv, kv_ctx, att_sink[0], b, n, n_ctx)
    o_f = _hgrn2_scan(p, pc, hg_lb[:, :, :], b, n, n_ctx, rev=False)
    o_b = _hgrn2_scan(p, pc, hg_lb[:, :, :], b, n, n_ctx, rev=True)
    wr_hi, wr_lo, rb = _router_weights(rt_group_w[0], rt_group_b[0], rt_expert_w[0], rt_expert_b[0])
    x1, tok, eid, gate = _even_out(
        att, o_f, o_b, p, hg_norm_g[0][None, :], mix_w_out[0].astype(BF16), xf,
        chunk(0, 2), chunk(0, 4), chunk(0, 3), ln_g[0, 0][None, :], ln_b[0, 0][None, :], wr_hi, wr_lo, rb, n)
    y2 = _moe_experts(tok, eid, moe_w1, moe_w3, moe_w2, 0)

    x2, u = _combine_call(x1, y2, gate, chunk(0, 5), ln_g[0, 1][None, :], ln_b[0, 1][None, :], n,
                          proj=(chunk(1, 1), chunk(1, 0), pool_w_in[0].astype(BF16)))
    wr_hi, wr_lo, rb = _router_weights(rt_group_w[1], rt_group_b[1], rt_expert_w[1], rt_expert_b[1])
    x3, tok, eid, gate = _pool_out(
        u, pool_w_grp[0].astype(BF16), pool_scale[0][None, :], pool_w_out[0].astype(BF16), x2,
        chunk(1, 2), chunk(1, 4), chunk(1, 3), ln_g[1, 0][None, :], ln_b[1, 0][None, :], wr_hi, wr_lo, rb, n)
    y2 = _moe_experts(tok, eid, moe_w1, moe_w3, moe_w2, 1)
    out = _combine_call(x3, y2, gate, chunk(1, 5), ln_g[1, 1][None, :], ln_b[1, 1][None, :], n)
    return out.reshape(b, n, d)
```

```python
import functools

import jax
import jax.numpy as jnp
from jax import lax
from jax.experimental import pallas as pl
from jax.experimental.pallas import tpu as pltpu

F32 = jnp.float32
BF16 = jnp.bfloat16

LANES = 128
MXU_TILE = 256
VMEM_LIMIT_BYTES = 56 * 1024 * 1024
MOE_VMEM_LIMIT_BYTES = 60 * 1024 * 1024

GRID_W = 64
ATT_HEADS = 8
ATT_KV_HEADS = 4
ATT_GROUP = ATT_HEADS // ATT_KV_HEADS
HEAD_DIM = 128
WINDOW = 128
ATT_BLOCK = 128
ROPE_BASE = 10000.0
HG_HEADS = 8
HG_KEY = 128
HG_CHUNK = 64
HG_STEP_CHUNKS = 4
HG_SUB = 16
HG_FAST_RANGE = 80.0
NORM_EPS = 1e-6
POOL_WINDOWS = (2, 4, 8, 16)
POOL_HALO = 8
MOE_GROUPS = 4
MOE_EXPERTS_PER_GROUP = 8
N_EXPERTS = MOE_GROUPS * MOE_EXPERTS_PER_GROUP
MOE_TOP_K = 2
MOE_ROWS = 256
PROJ_TN = 1024
MOE_ID_ROWS = 8
SMEM_1D_TILE = 1024
WEIGHT_DMA_PRIORITY = 1
LN_EPS = 1e-5
DEPTH = 2
DEEPNORM_ALPHA = (2 * DEPTH) ** 0.25


def _dot(a, b):
    return jnp.dot(a, b, preferred_element_type=F32)


def _dot_nt(a, b):
    return lax.dot_general(a, b, (((1,), (1,)), ((), ())), preferred_element_type=F32)


def _dot_tn(a, b):
    return lax.dot_general(a, b, (((0,), (0,)), ((), ())), preferred_element_type=F32)


def _sigmoid(x):
    return 1.0 / (1.0 + jnp.exp(-x))


def _silu(x):
    return x * _sigmoid(x)


def _params(*sem):
    return pltpu.CompilerParams(dimension_semantics=sem, vmem_limit_bytes=VMEM_LIMIT_BYTES)


def _tiles_from_rows(x):
    n = x.shape[1] // LANES
    return jnp.swapaxes(jnp.stack([x[:, s * LANES:(s + 1) * LANES] for s in range(n)], axis=0), 0, 1)


def _rows_from_tiles(x3):
    xt = jnp.swapaxes(x3, 0, 1)
    return jnp.concatenate([xt[s] for s in range(xt.shape[0])], axis=-1)


def _layer_norm(z, g, b):
    mu = jnp.mean(z, axis=-1, keepdims=True)
    zc = z - mu
    var = jnp.mean(zc * zc, axis=-1, keepdims=True)
    return zc * lax.rsqrt(var + LN_EPS) * g + b


def _ada_kernel(s_ref, w_ref, b_ref, o_ref):
    s = _silu(s_ref[...]).astype(BF16)
    o_ref[0] = _dot(s, w_ref[0].astype(BF16)) + b_ref[0]


def _ada_mod(s, ada_w, ada_b, tn=1024):
    n_l, d, n = ada_w.shape
    return pl.pallas_call(
        _ada_kernel,
        grid=(n_l, n // tn),
        in_specs=[
            pl.BlockSpec((8, d), lambda l, j: (0, 0)),
            pl.BlockSpec((1, d, tn), lambda l, j: (l, 0, j)),
            pl.BlockSpec((1, 1, tn), lambda l, j: (l, 0, j)),
        ],
        out_specs=pl.BlockSpec((1, 8, tn), lambda l, j: (l, 0, j)),
        out_shape=jax.ShapeDtypeStruct((n_l, 8, n), F32),
        compiler_params=_params("parallel", "parallel"),
        name="ada_mod",
    )(s, ada_w, ada_b.reshape(n_l, 1, n))


def _rope(t, cos, sin_signed, first_half):
    partner = jnp.where(first_half, pltpu.roll(t, 96, 1), pltpu.roll(t, 32, 1))
    return t * cos + partner * sin_signed


def _modmm_kernel(x_ref, sc_ref, sh_ref, w_ref, cos_ref, sin_ref, oa_ref, ob_ref, xs_ref, *,
                  n_q, n_rope, n_a_tiles):
    j = pl.program_id(1)

    @pl.when(j == 0)
    def _():
        xs_ref[...] = (x_ref[...] * (1.0 + sc_ref[0]) + sh_ref[0]).astype(BF16)

    acc = _dot(xs_ref[...], w_ref[...])
    tn = acc.shape[1]

    for jt in range(n_a_tiles):
        @pl.when(j == jt)
        def _(jt=jt):
            lane = lax.broadcasted_iota(jnp.int32, (1, HEAD_DIM), 1)
            first_half = (lane % 64) < 32
            for h in range(tn // HEAD_DIM):
                sl = slice(h * HEAD_DIM, (h + 1) * HEAD_DIM)
                col = jt * tn + h * HEAD_DIM
                if col < n_rope:
                    scale = HEAD_DIM ** -0.5 if col < n_q else 1.0
                    oa_ref[:, sl] = _rope(acc[:, sl], cos_ref[...] * scale, sin_ref[...] * scale,
                                          first_half).astype(BF16)
                else:
                    oa_ref[:, sl] = acc[:, sl].astype(BF16)

    @pl.when(j >= n_a_tiles)
    def _():
        ob_ref[...] = acc


def _mod_matmul(x, sc, sh, w, cos, sin, col_map, n_a, n_b, n_q, n_rope, n_seq, tm, tn):
    m, k = x.shape
    rows_per_mod = m // sc.shape[0]
    ta, tb = n_a // tn, n_b // tn
    tab = pl.BlockSpec((tm, HEAD_DIM), lambda i, j: ((i * tm % n_seq) // tm, 0))
    return pl.pallas_call(
        functools.partial(_modmm_kernel, n_q=n_q, n_rope=n_rope, n_a_tiles=ta),
        grid=(m // tm, ta + tb),
        in_specs=[
            pl.BlockSpec((tm, k), lambda i, j: (i, 0)),
            pl.BlockSpec((1, 1, k), lambda i, j: (i * tm // rows_per_mod, 0, 0)),
            pl.BlockSpec((1, 1, k), lambda i, j: (i * tm // rows_per_mod, 0, 0)),
            pl.BlockSpec((k, tn), lambda i, j: (0, col_map(j))),
            tab, tab,
        ],
        out_specs=[pl.BlockSpec((tm, tn), lambda i, j: (i, jnp.minimum(j, ta - 1))),
                   pl.BlockSpec((tm, tn), lambda i, j: (i, jnp.maximum(j - ta, 0)))],
        out_shape=[jax.ShapeDtypeStruct((m, n_a), BF16), jax.ShapeDtypeStruct((m, n_b), F32)],
        scratch_shapes=[pltpu.VMEM((tm, k), BF16)],
        compiler_params=_params("parallel", "arbitrary"),
        name="mod_matmul",
    )(x, sc, sh, w, cos, sin)


def _attn_kernel(sink_ref, q_ref, kp_ref, kc_ref, kn_ref, vp_ref, vc_ref, vn_ref, kx_ref, vx_ref,
                 mp_ref, mn_ref, o_ref, *, n_blocks):
    n = pl.program_id(1)
    blk = ATT_BLOCK
    row1 = lax.broadcasted_iota(jnp.int32, (ATT_GROUP * blk, 1), 0)
    has_prev, has_next = n > 0, n < n_blocks - 1
    for h in range(ATT_KV_HEADS):
        kv = slice(h * HEAD_DIM, (h + 1) * HEAD_DIM)
        q2 = jnp.concatenate([q_ref[:, (ATT_GROUP * h + g) * HEAD_DIM:(ATT_GROUP * h + g + 1) * HEAD_DIM]
                              for g in range(ATT_GROUP)], axis=0)
        s_p = jnp.where(has_prev, _dot_nt(q2, kp_ref[:, kv]) + mp_ref[...], -jnp.inf)
        s_c = _dot_nt(q2, kc_ref[:, kv])
        s_n = jnp.where(has_next, _dot_nt(q2, kn_ref[:, kv]) + mn_ref[...], -jnp.inf)
        s_x = _dot_nt(q2, kx_ref[:, kv])
        sink = jnp.where(row1 < blk, sink_ref[ATT_GROUP * h], sink_ref[ATT_GROUP * h + 1])
        m = jnp.maximum(jnp.maximum(jnp.max(s_p, axis=-1, keepdims=True), jnp.max(s_c, axis=-1, keepdims=True)),
                        jnp.maximum(jnp.max(s_n, axis=-1, keepdims=True), jnp.max(s_x, axis=-1, keepdims=True)))
        m = jnp.maximum(m, sink)
        p_p, p_c, p_n, p_x = (jnp.exp(s - m) for s in (s_p, s_c, s_n, s_x))
        den = (jnp.sum(p_p, axis=-1, keepdims=True) + jnp.sum(p_c, axis=-1, keepdims=True)
               + jnp.sum(p_n, axis=-1, keepdims=True) + jnp.sum(p_x, axis=-1, keepdims=True)
               + jnp.exp(sink - m))
        o = (_dot(p_p.astype(BF16), vp_ref[:, kv]) + _dot(p_c.astype(BF16), vc_ref[:, kv])
             + _dot(p_n.astype(BF16), vn_ref[:, kv]) + _dot(p_x.astype(BF16), vx_ref[:, kv])) / den
        for g in range(ATT_GROUP):
            col = (ATT_GROUP * h + g) * HEAD_DIM
            o_ref[:, col:col + HEAD_DIM] = o[g * blk:(g + 1) * blk].astype(o_ref.dtype)


def _window_attention(qkv, kv_ctx, sink, batch, n_seq, n_ctx):
    assert ATT_GROUP == 2 and WINDOW == ATT_BLOCK
    nb = n_seq // ATT_BLOCK
    qw, kw = ATT_HEADS * HEAD_DIM, ATT_KV_HEADS * HEAD_DIM
    kcol, vcol = qw // kw, qw // kw + 1

    def rows(off):
        return lambda b, n, off=off: b * nb + jnp.clip(n + off, 0, nb - 1)

    kspec = [pl.BlockSpec((ATT_BLOCK, kw), lambda b, n, r=rows(o): (r(b, n), kcol)) for o in (-1, 0, 1)]
    vspec = [pl.BlockSpec((ATT_BLOCK, kw), lambda b, n, r=rows(o): (r(b, n), vcol)) for o in (-1, 0, 1)]
    r = jnp.arange(ATT_GROUP * ATT_BLOCK)[:, None] % ATT_BLOCK
    c = jnp.arange(ATT_BLOCK)[None, :]
    mask_prev = jnp.where(c >= r, 0.0, -jnp.inf).astype(F32)
    mask_next = jnp.where(c <= r, 0.0, -jnp.inf).astype(F32)
    mspec = pl.BlockSpec(mask_prev.shape, lambda b, n: (0, 0))
    return pl.pallas_call(
        functools.partial(_attn_kernel, n_blocks=nb),
        grid=(batch, nb),
        in_specs=[pl.BlockSpec(memory_space=pltpu.SMEM),
                  pl.BlockSpec((ATT_BLOCK, qw), lambda b, n: (b * nb + n, 0))]
        + kspec + vspec
        + [pl.BlockSpec((n_ctx, kw), lambda b, n: (b, 0)), pl.BlockSpec((n_ctx, kw), lambda b, n: (b, 1)),
           mspec, mspec],
        out_specs=pl.BlockSpec((ATT_BLOCK, qw), lambda b, n: (b * nb + n, 0)),
        out_shape=jax.ShapeDtypeStruct((batch * n_seq, qw), BF16),
        compiler_params=_params("parallel", "parallel"),
        name="window_attention",
    )(sink, qkv, qkv, qkv, qkv, qkv, qkv, qkv, kv_ctx, kv_ctx, mask_prev, mask_next)


def _gla_step(zf, q_raw, v, lb, st_ref, o_ref, rev):
    c_len = HG_CHUNK
    n_rows = zf.shape[0]
    n_sub = n_rows // c_len
    shift = c_len.bit_length() - 1
    order = range(n_sub - 1, -1, -1) if rev else range(n_sub)

    def head(h):
        return slice(h * HG_KEY, (h + 1) * HG_KEY)

    def chunk(i):
        return slice(i * c_len, (i + 1) * c_len)

    def seen(n):
        ri = lax.broadcasted_iota(jnp.int32, (n, n), 0)
        ci = lax.broadcasted_iota(jnp.int32, (n, n), 1)
        return ((ri >> shift) == (ci >> shift)) & ((ci >= ri) if rev else (ci <= ri))

    f = lb + (1.0 - lb) * _sigmoid(zf)
    k = 1.0 - f
    g = jnp.log(f)
    tri = jnp.where(seen(n_rows), 1.0, 0.0).astype(BF16)
    g1 = g.astype(BF16)
    r1 = g - g1.astype(F32)
    g2 = r1.astype(BF16)
    g3 = (r1 - g2.astype(F32)).astype(BF16)
    c = _dot(tri, g1) + _dot(tri, g2) + _dot(tri, g3)
    c_end = [c[i * c_len:i * c_len + 1] if rev else c[(i + 1) * c_len - 1:(i + 1) * c_len] for i in range(n_sub)]
    c_end_rows = jnp.concatenate([jnp.broadcast_to(ce, (c_len, ce.shape[1])) for ce in c_end], axis=0)
    k_end = (k * jnp.exp(c_end_rows - c)).astype(BF16)
    dec = [jnp.exp(ce) for ce in c_end]
    vb = v.astype(BF16)

    def advance(states, i):
        new = []
        for p in range(0, HG_HEADS, 2):
            lanes = slice(p * HG_KEY, (p + 2) * HG_KEY)
            inc = _dot_tn(vb[chunk(i), lanes], k_end[chunk(i), lanes])
            for j in range(2):
                blk = slice(j * HG_KEY, (j + 1) * HG_KEY)
                new.append(states[p + j] * dec[i][:, head(p + j)] + inc[blk, blk])
        return new

    if o_ref is None:
        states = [st_ref[h] for h in range(HG_HEADS)]
        for i in order:
            states = advance(states, i)
        for h in range(HG_HEADS):
            st_ref[h] = states[h]
        return

    q = _silu(q_raw.astype(F32))
    q_in = (q * jnp.exp(c)).astype(BF16)
    lowest = functools.reduce(jnp.minimum, c_end)
    in_range = jnp.min(lowest) >= -HG_FAST_RANGE

    def stack(x, i, h0, n):
        return jnp.concatenate([x[chunk(i), head(h)] for h in range(h0, h0 + n)], axis=0)

    @pl.when(in_range)
    def _():
        k_in = (k * jnp.exp(-c)).astype(BF16)
        n_qk = MXU_TILE // c_len
        same = seen(n_qk * c_len)
        intra = {}
        for i in order:
            for h0 in range(0, HG_HEADS, n_qk):
                sc = jnp.where(same, _dot_nt(stack(q_in, i, h0, n_qk), stack(k_in, i, h0, n_qk)), 0.0)
                pv = _dot(sc.astype(BF16), stack(vb, i, h0, n_qk))
                for j in range(n_qk):
                    intra[i, h0 + j] = pv[j * c_len:(j + 1) * c_len]
        states = [st_ref[h] for h in range(HG_HEADS)]
        for i in order:
            out = []
            for p in range(0, HG_HEADS, 2):
                st_pair = jnp.concatenate([states[p].astype(BF16), states[p + 1].astype(BF16)], axis=0)
                inter = _dot_nt(stack(q_in, i, p, 2), st_pair)
                for j in range(2):
                    out.append(inter[j * c_len:(j + 1) * c_len, j * HG_KEY:(j + 1) * HG_KEY] + intra[i, p + j])
            o_ref[chunk(i), :] = jnp.concatenate(out, axis=1)
            states = advance(states, i)
        for h in range(HG_HEADS):
            st_ref[h] = states[h]

    @pl.when(jnp.logical_not(in_range))
    def _():
        for i in order:
            r = chunk(i)
            _gla_intra_exact(q[r], k[r], v[r].astype(F32), vb[r], c[r], q_in[r], st_ref, o_ref, r, rev)
            states = advance([st_ref[h] for h in range(HG_HEADS)], i)
            for h in range(HG_HEADS):
                st_ref[h] = states[h]


def _gla_intra_exact(q, k, v, vb, c, q_in, st_ref, o_ref, rows, rev):
    c_len = q.shape[0]
    pairs = []
    size = c_len // 2
    while size >= HG_SUB:
        for lo in range(0, c_len, 2 * size):
            pairs.append((lo, lo + size, lo + 2 * size))
        size //= 2
    scaled = []
    for lo, mid, hi in pairs:
        if rev:
            late, early, bnd = slice(lo, mid), slice(mid, hi), mid
        else:
            late, early, bnd = slice(mid, hi), slice(lo, mid), mid - 1
        cb = c[bnd:bnd + 1]
        q_l = (q[late] * jnp.exp(c[late] - cb)).astype(BF16)
        k_e = (k[early] * jnp.exp(cb - c[early])).astype(BF16)
        scaled.append((late, early, q_l, k_e))
    n_sub = c_len // HG_SUB
    t_idx = lax.broadcasted_iota(jnp.int32, (HG_SUB, 1), 0)
    diag = [[None] * HG_HEADS for _ in range(n_sub)]
    for b in range(n_sub):
        r0 = b * HG_SUB
        qb, cb = q[r0:r0 + HG_SUB], c[r0:r0 + HG_SUB]
        for s in range(HG_SUB):
            row = r0 + s
            ok = (t_idx <= s) if rev else (t_idx >= s)
            w = qb * k[row:row + 1] * jnp.exp(jnp.where(ok, cb - c[row:row + 1], -jnp.inf))
            for h in range(HG_HEADS):
                sl = slice(h * HG_KEY, (h + 1) * HG_KEY)
                contrib = jnp.sum(w[:, sl], axis=-1, keepdims=True) * v[row:row + 1, sl]
                diag[b][h] = contrib if diag[b][h] is None else diag[b][h] + contrib

    for h in range(HG_HEADS):
        sl = slice(h * HG_KEY, (h + 1) * HG_KEY)
        o_h = _dot_nt(q_in[:, sl], st_ref[h].astype(BF16))
        parts = [diag[b][h] for b in range(n_sub)]
        for late, early, q_l, k_e in scaled:
            sc = _dot_nt(q_l[:, sl], k_e[:, sl]).astype(BF16)
            add = _dot(sc, vb[early, sl])
            b0 = late.start // HG_SUB
            for j in range((late.stop - late.start) // HG_SUB):
                parts[b0 + j] = parts[b0 + j] + add[j * HG_SUB:(j + 1) * HG_SUB]
        o_ref[rows, sl] = o_h + jnp.concatenate(parts, axis=0)


def _gla_kernel(lb_ref, zf_ref, q_ref, v_ref, zfc_ref, vc_ref, o_ref, st_ref, *, rev, n_ctx_steps):
    s = pl.program_id(1)

    @pl.when(s == 0)
    def _():
        st_ref[...] = jnp.zeros_like(st_ref)

    x = lb_ref[...]
    e = jnp.exp(x - jnp.max(x, axis=0, keepdims=True))
    lb = e[0:1] / jnp.sum(e, axis=0, keepdims=True)

    @pl.when(s < n_ctx_steps)
    def _():
        _gla_step(zfc_ref[...], None, vc_ref[...], lb, st_ref, None, rev)

    @pl.when(s >= n_ctx_steps)
    def _():
        _gla_step(zf_ref[...], q_ref[...], v_ref[...], lb, st_ref, o_ref, rev)


def _hgrn2_scan(a_lat, f_lat, a_ctx, f_ctx, hg_lb, batch, n_seq, n_ctx, rev):
    hk = HG_HEADS * HG_KEY
    rows = HG_CHUNK * HG_STEP_CHUNKS
    assert n_seq % rows == 0 and n_ctx % rows == 0
    nc, ncc = n_seq // rows, n_ctx // rows
    d = 1 if rev else 0

    def lat(b, s):
        j = jnp.maximum(s - ncc, 0)
        return b * nc + (nc - 1 - j if rev else j)

    def ctx(b, s):
        j = jnp.minimum(s, ncc - 1)
        return b * ncc + (ncc - 1 - j if rev else j)

    return pl.pallas_call(
        functools.partial(_gla_kernel, rev=rev, n_ctx_steps=ncc),
        grid=(batch, ncc + nc),
        in_specs=[
            pl.BlockSpec((None, hg_lb.shape[1], hk), lambda b, s: (d, 0, 0)),
            pl.BlockSpec((rows, hk), lambda b, s: (lat(b, s), d)),
            pl.BlockSpec((rows, hk), lambda b, s: (lat(b, s), 2)),
            pl.BlockSpec((rows, hk), lambda b, s: (lat(b, s), 3)),
            pl.BlockSpec((rows, hk), lambda b, s: (ctx(b, s), d)),
            pl.BlockSpec((rows, hk), lambda b, s: (ctx(b, s), 1)),
        ],
        out_specs=pl.BlockSpec((rows, hk), lambda b, s: (lat(b, s), 0)),
        out_shape=jax.ShapeDtypeStruct((batch * n_seq, hk), F32),
        scratch_shapes=[pltpu.VMEM((HG_HEADS, HG_KEY, HG_KEY), F32)],
        compiler_params=_params("parallel", "arbitrary"),
        name="hgrn2_bwd" if rev else "hgrn2_fwd",
    )(hg_lb, f_lat, a_lat, a_lat, f_ctx, a_ctx)


def _route(tok, wr_hi, wr_lo, rb):
    t_hi = tok.astype(BF16)
    t_lo = (tok - t_hi.astype(F32)).astype(BF16)
    lg = _dot(t_hi, wr_hi) + _dot(t_hi, wr_lo) + _dot(t_lo, wr_hi) + rb
    lane = lax.broadcasted_iota(jnp.int32, lg.shape, 1)
    lane_f = lane.astype(F32)
    ninf = -jnp.inf
    gl = jnp.where(lane < MOE_GROUPS, lg, ninf)
    gmax = jnp.max(gl, axis=-1, keepdims=True)
    g_idx = jnp.min(jnp.where(gl == gmax, lane_f, float(LANES)), axis=-1, keepdims=True)
    g_val = 1.0 / jnp.sum(jnp.exp(gl - gmax), axis=-1, keepdims=True)
    e_lane = lane_f - float(MOE_GROUPS)
    lo = g_idx * float(MOE_EXPERTS_PER_GROUP)
    in_grp = (e_lane >= lo) & (e_lane < lo + float(MOE_EXPERTS_PER_GROUP))
    el = jnp.where(in_grp, lg, ninf)
    l1 = jnp.max(el, axis=-1, keepdims=True)
    i1 = jnp.min(jnp.where(el == l1, e_lane, float(LANES)), axis=-1, keepdims=True)
    el2 = jnp.where(e_lane == i1, ninf, el)
    l2 = jnp.max(el2, axis=-1, keepdims=True)
    i2 = jnp.min(jnp.where(el2 == l2, e_lane, float(LANES)), axis=-1, keepdims=True)
    r = jnp.exp(l2 - l1)
    w1 = g_val / (1.0 + r)
    w2 = w1 * r
    eid = jnp.where(lane == 0, i1, jnp.where(lane == 1, i2, 0.0))
    gate = jnp.where(lane == 0, w1, jnp.where(lane == 1, w2, 0.0))
    eid_t = jnp.transpose(eid)[:MOE_ID_ROWS].astype(jnp.int32)
    return eid_t, gate


def _post_mix(y, x_ref, g1_ref, sc2_ref, sh2_ref, lng_ref, lnb_ref, wrh_ref, wrl_ref, rb_ref,
              x1_ref, tok_ref, eid_ref, gate_ref):
    x1 = _layer_norm(DEEPNORM_ALPHA * x_ref[...] + g1_ref[0] * y, lng_ref[...], lnb_ref[...])
    x1_ref[...] = x1
    tok = x1 * (1.0 + sc2_ref[0]) + sh2_ref[0]
    tok_ref[...] = _tiles_from_rows(tok).astype(BF16)
    eid, gate = _route(tok, wrh_ref[...], wrl_ref[...], rb_ref[...])
    eid_ref[...] = eid
    gate_ref[...] = gate


def _even_out_kernel(att_ref, of_ref, ob_ref, gt_ref, ng_ref, wo_ref, *rest):
    o = of_ref[...] + ob_ref[...]
    pieces = []
    for h in range(HG_HEADS):
        oh = o[:, h * HG_KEY:(h + 1) * HG_KEY]
        pieces.append(oh * lax.rsqrt(jnp.mean(oh * oh, axis=-1, keepdims=True) + NORM_EPS))
    hg = (jnp.concatenate(pieces, axis=-1) * ng_ref[...] * _silu(gt_ref[...].astype(F32))).astype(BF16)
    y = _dot(jnp.concatenate([att_ref[...], hg], axis=-1), wo_ref[...])
    _post_mix(y, *rest)


def _post_specs(d, tm, rows_per_batch):
    def bmap(i):
        return (i * tm // rows_per_batch, 0, 0)

    row = pl.BlockSpec((tm, d), lambda i: (i, 0))
    mod = pl.BlockSpec((1, 1, d), bmap)
    vec = pl.BlockSpec((1, d), lambda i: (0, 0))
    rw = pl.BlockSpec((d, LANES), lambda i: (0, 0))
    in_specs = [row, mod, mod, mod, vec, vec, rw, rw, pl.BlockSpec((1, LANES), lambda i: (0, 0))]
    lane_blk = pl.BlockSpec((tm, LANES), lambda i: (i, 0))
    tiles = pl.BlockSpec((tm, d // LANES, LANES), lambda i: (i, 0, 0))
    out_specs = [row, tiles, pl.BlockSpec((MOE_ID_ROWS, tm), lambda i: (0, i)), lane_blk]
    return in_specs, out_specs


def _post_out_shapes(t, d):
    return [jax.ShapeDtypeStruct((t, d), F32), jax.ShapeDtypeStruct((t, d // LANES, LANES), BF16),
            jax.ShapeDtypeStruct((MOE_ID_ROWS, t), jnp.int32), jax.ShapeDtypeStruct((t, LANES), F32)]


def _even_out(att, o_f, o_b, p, norm_g, w_out, x, g1, sc2, sh2, lng, lnb, wr_hi, wr_lo, rb, rows_per_batch, tm=256):
    t, d = x.shape
    hv = o_f.shape[1]
    post_in, post_out = _post_specs(d, tm, rows_per_batch)
    return pl.pallas_call(
        _even_out_kernel,
        grid=(t // tm,),
        in_specs=[
            pl.BlockSpec((tm, att.shape[1]), lambda i: (i, 0)),
            pl.BlockSpec((tm, hv), lambda i: (i, 0)),
            pl.BlockSpec((tm, hv), lambda i: (i, 0)),
            pl.BlockSpec((tm, hv), lambda i: (i, 4)),
            pl.BlockSpec((1, hv), lambda i: (0, 0)),
            pl.BlockSpec(w_out.shape, lambda i: (0, 0)),
        ] + post_in,
        out_specs=post_out,
        out_shape=_post_out_shapes(t, d),
        compiler_params=_params("parallel"),
        name="even_out",
    )(att, o_f, o_b, p, norm_g, w_out, x, g1, sc2, sh2, lng, lnb, wr_hi, wr_lo, rb)


def _combine(x_ref, ya_ref, yb_ref, gate_ref, g2_ref, lng_ref, lnb_ref):
    gate = gate_ref[...]
    y = (gate[:, 0:1] * _rows_from_tiles(ya_ref[...].astype(F32))
         + gate[:, 1:2] * _rows_from_tiles(yb_ref[...].astype(F32)))
    return _layer_norm(DEEPNORM_ALPHA * x_ref[...] + g2_ref[0] * y, lng_ref[...], lnb_ref[...])


def _combine_proj_kernel(x_ref, ya_ref, yb_ref, gate_ref, g2_ref, lng_ref, lnb_ref, sc_ref, sh_ref, w_ref,
                         x2_ref, u_ref):
    x2 = _combine(x_ref, ya_ref, yb_ref, gate_ref, g2_ref, lng_ref, lnb_ref)
    x2_ref[...] = x2
    u_ref[...] = _dot((x2 * (1.0 + sc_ref[0]) + sh_ref[0]).astype(BF16), w_ref[...])


def _combine_kernel(x_ref, ya_ref, yb_ref, gate_ref, g2_ref, lng_ref, lnb_ref, x2_ref):
    x2_ref[...] = _combine(x_ref, ya_ref, yb_ref, gate_ref, g2_ref, lng_ref, lnb_ref)


def _combine_call(x1, y2, gate, g2, lng, lnb, rows_per_batch, proj=None, tm=256):
    t, d = x1.shape
    nt = t // tm

    def bmap(i):
        return (i * tm // rows_per_batch, 0, 0)

    row = pl.BlockSpec((tm, d), lambda i: (i, 0))
    mod = pl.BlockSpec((1, 1, d), bmap)
    vec = pl.BlockSpec((1, d), lambda i: (0, 0))
    in_specs = [row, pl.BlockSpec((tm, d // LANES, LANES), lambda i: (i, 0, 0)),
                pl.BlockSpec((tm, d // LANES, LANES), lambda i: (nt + i, 0, 0)),
                pl.BlockSpec((tm, LANES), lambda i: (i, 0)), mod, vec, vec]
    args = [x1, y2, y2, gate, g2, lng, lnb]
    if proj is None:
        return pl.pallas_call(
            _combine_kernel, grid=(nt,), in_specs=in_specs, out_specs=row,
            out_shape=jax.ShapeDtypeStruct((t, d), F32),
            compiler_params=_params("parallel"), name="combine_ln")(*args)
    sc, sh, w = proj
    return pl.pallas_call(
        _combine_proj_kernel, grid=(nt,),
        in_specs=in_specs + [mod, mod, pl.BlockSpec(w.shape, lambda i: (0, 0))],
        out_specs=[row, pl.BlockSpec((tm, w.shape[1]), lambda i: (i, 0))],
        out_shape=[jax.ShapeDtypeStruct((t, d), F32), jax.ShapeDtypeStruct((t, w.shape[1]), F32)],
        compiler_params=_params("parallel"), name="combine_ln_proj")(*args, sc, sh, w)


def _pool_out_kernel(up_ref, uc_ref, un_ref, wg_ref, ps_ref, wo_ref, *rest, n_seq):
    tm, d = uc_ref.shape
    n_grp = len(POOL_WINDOWS)
    ch = d // n_grp
    halo = POOL_HALO
    pos0 = (pl.program_id(0) * tm) % n_seq
    e_pos = pos0 - halo + lax.broadcasted_iota(jnp.int32, (tm + 2 * halo, 1), 0)
    e_ok = (e_pos >= 0) & (e_pos < n_seq)
    t_pos = pos0 + lax.broadcasted_iota(jnp.int32, (tm, 1), 0)
    z = []
    for gi, w in enumerate(POOL_WINDOWS):
        cs = slice(gi * ch, (gi + 1) * ch)
        u = uc_ref[:, cs]
        ext = jnp.where(e_ok, jnp.concatenate([up_ref[:, cs], u, un_ref[:, cs]], axis=0), 0.0)
        a, span = ext, 1
        while span < w:
            a = a[:a.shape[0] - span] + a[span:]
            span *= 2
        start = halo - w // 2
        win = a[start:start + tm]
        cnt = (jnp.minimum(t_pos + (w - w // 2), n_seq) - jnp.maximum(t_pos - w // 2, 0)).astype(F32)
        mixed = (win / cnt - u).astype(BF16)
        z.append((_dot(mixed, wg_ref[gi]) * ps_ref[:, cs]).astype(BF16))
    _post_mix(_dot(jnp.concatenate(z, axis=-1), wo_ref[...]), *rest)


def _pool_out(u, w_grp, scale, w_out, x, g1, sc2, sh2, lng, lnb, wr_hi, wr_lo, rb, n_seq, tm=256):
    t, d = x.shape
    hb = tm // POOL_HALO
    n_hb = t // POOL_HALO
    post_in, post_out = _post_specs(d, tm, n_seq)
    return pl.pallas_call(
        functools.partial(_pool_out_kernel, n_seq=n_seq),
        grid=(t // tm,),
        in_specs=[
            pl.BlockSpec((POOL_HALO, d), lambda i: (jnp.maximum(i * hb - 1, 0), 0)),
            pl.BlockSpec((tm, d), lambda i: (i, 0)),
            pl.BlockSpec((POOL_HALO, d), lambda i: (jnp.minimum((i + 1) * hb, n_hb - 1), 0)),
            pl.BlockSpec(w_grp.shape, lambda i: (0, 0, 0)),
            pl.BlockSpec((1, d), lambda i: (0, 0)),
            pl.BlockSpec(w_out.shape, lambda i: (0, 0)),
        ] + post_in,
        out_specs=post_out,
        out_shape=_post_out_shapes(t, d),
        compiler_params=_params("parallel"),
        name="pool_out",
    )(u, u, u, w_grp, scale, w_out, x, g1, sc2, sh2, lng, lnb, wr_hi, wr_lo, rb)


def _moe_kernel(be_ref, nv_ref, first_ref, ws_ref, nxt_ref, idx_ref, idxn_ref, idxp_ref, tok_hbm, w1_hbm, w3_hbm, w2_hbm,
                y_hbm, xbuf, ybuf, xb_ref, wf1, wf3, wf2, w1b, w3b, w2b, gsem, ssem, wsem, *, n_tok, layer, n_blocks):
    i = pl.program_id(0)
    used = nv_ref[jnp.minimum(i, n_blocks - 1)] > 0
    used = used & (i < n_blocks)
    prev_used = (i > 0) & (nv_ref[jnp.maximum(i - 1, 0)] > 0)
    xs = i % 2

    def weight_copies(e, ws):
        return (pltpu.make_async_copy(w1_hbm.at[layer, e], wf1.at[ws], wsem.at[ws]),
                pltpu.make_async_copy(w3_hbm.at[layer, e], wf3.at[ws], wsem.at[ws]),
                pltpu.make_async_copy(w2_hbm.at[layer, e], wf2.at[ws], wsem.at[ws]))

    def gather_start(idx, slot):
        for r in range(MOE_ROWS):
            tok = idx[0, 0, r] & (n_tok - 1)
            pltpu.make_async_copy(tok_hbm.at[tok], xbuf.at[slot, r], gsem.at[slot]).start()

    def gather_wait(slot):
        pltpu.make_async_copy(tok_hbm.at[pl.ds(0, MOE_ROWS)], xbuf.at[slot], gsem.at[slot]).wait()

    def scatter_start(idx, slot):
        for r in range(MOE_ROWS):
            pltpu.make_async_copy(ybuf.at[slot, r], y_hbm.at[idx[0, 0, r]], ssem.at[slot]).start(priority=r % 2)

    def scatter_wait(slot):
        pltpu.make_async_copy(ybuf.at[slot], y_hbm.at[pl.ds(0, MOE_ROWS)], ssem.at[slot]).wait()

    @pl.when(i == 0)
    def _():
        xbuf[...] = jnp.zeros_like(xbuf)
        ybuf[...] = jnp.zeros_like(ybuf)
        spare0 = pltpu.make_async_copy(
            ybuf.at[0], y_hbm.at[pl.ds(MOE_TOP_K * n_tok, MOE_ROWS)], ssem.at[0])
        spare0.start()
        for cp in weight_copies(be_ref[0], 0):
            cp.start(priority=WEIGHT_DMA_PRIORITY)
        gather_start(idx_ref, 0)

    @pl.when(used)
    def _():
        ws = ws_ref[i]

        @pl.when(first_ref[i] == 1)
        def _():
            for cp in weight_copies(be_ref[i], ws):
                cp.wait()
            nxt = nxt_ref[i]

            @pl.when(nxt >= 0)
            def _():
                for cp in weight_copies(nxt, 1 - ws):
                    cp.start(priority=WEIGHT_DMA_PRIORITY)

            w1b[...] = wf1[ws].astype(BF16)
            w3b[...] = wf3[ws].astype(BF16)
            w2b[...] = wf2[ws].astype(BF16)

        gather_wait(xs)
        xb_ref[...] = _rows_from_tiles(xbuf[xs].astype(F32)).astype(BF16)
        gather_start(idxn_ref, 1 - xs)
        scatter_start(idxp_ref, 1 - xs)
        xb = xb_ref[...]
        h = (_silu(_dot(xb, w1b[...])) * _dot(xb, w3b[...])).astype(BF16)
        y = _tiles_from_rows(_dot(h, w2b[...])).astype(BF16)
        scatter_wait(xs)
        ybuf[xs] = y

    @pl.when(jnp.logical_not(used) & prev_used)
    def _():
        gather_wait(xs)
        scatter_start(idxp_ref, 1 - xs)
        scatter_wait(1 - xs)
        scatter_wait(xs)


def _moe_dispatch(eid_t, n_blocks):
    n_tok = eid_t.shape[1]
    n_slots = (n_blocks + 2) * MOE_ROWS
    slot_rows = -(-n_slots // SMEM_1D_TILE)
    assert n_blocks <= LANES and n_tok % SMEM_1D_TILE == 0 and SMEM_1D_TILE % MOE_ROWS == 0
    slot, meta = pl.pallas_call(
        functools.partial(_dispatch_kernel, n_tok=n_tok),
        in_specs=[pl.BlockSpec(memory_space=pltpu.VMEM)],
        out_specs=[pl.BlockSpec(memory_space=pltpu.SMEM), pl.BlockSpec(memory_space=pltpu.VMEM)],
        out_shape=[jax.ShapeDtypeStruct((slot_rows * SMEM_1D_TILE,), jnp.int32),
                   jax.ShapeDtypeStruct((MOE_ID_ROWS, LANES), jnp.int32)],
        scratch_shapes=[pltpu.VMEM((MOE_TOP_K * n_tok,), jnp.int32),
                        pltpu.VMEM((slot_rows * SMEM_1D_TILE,), jnp.int32),
                        pltpu.SMEM((MOE_TOP_K * n_tok,), jnp.int32),
                        pltpu.SemaphoreType.DMA(())],
        name="moe_dispatch",
    )(eid_t)
    return (slot[:n_slots].reshape(n_blocks + 2, 1, MOE_ROWS),) + tuple(meta[r, :n_blocks] for r in range(5))


def _dispatch_kernel(eid_ref, slot_ref, meta_ref, dest_vmem, init_vmem, dest_smem, sem, *, n_tok):
    tile = MOE_ROWS
    n_tiles = n_tok // tile
    sub = lax.broadcasted_iota(jnp.int32, (N_EXPERTS, tile), 0)
    si = lax.broadcasted_iota(jnp.int32, (tile, tile), 0)
    ti = lax.broadcasted_iota(jnp.int32, (tile, tile), 1)
    before = jnp.where(si < ti, 1.0, 0.0).astype(BF16)

    def one_hots(j):
        ids = eid_ref[:, j * tile:(j + 1) * tile]
        return [jnp.where(sub == ids[k:k + 1], 1.0, 0.0) for k in range(MOE_TOP_K)]

    carry = jnp.zeros((N_EXPERTS, 1), F32)
    ranks = []
    for j in range(n_tiles):
        oh = one_hots(j)
        both = oh[0] + oh[1]
        seen = carry + _dot(both.astype(BF16), before)
        ranks.append([jnp.sum(seen * o, axis=0, keepdims=True) for o in oh])
        carry = carry + jnp.sum(both, axis=1, keepdims=True)

    counts = carry
    nblk = jnp.floor((counts + float(MOE_ROWS - 1)) * (1.0 / MOE_ROWS))
    ei = lax.broadcasted_iota(jnp.int32, (N_EXPERTS, N_EXPERTS), 0)
    ej = lax.broadcasted_iota(jnp.int32, (N_EXPERTS, N_EXPERTS), 1)
    lower = jnp.where(ej < ei, 1.0, 0.0).astype(BF16)
    first_blk = _dot(lower, jnp.broadcast_to(nblk, (N_EXPERTS, LANES)).astype(BF16))[:, 0:1]
    first_slot = first_blk * float(MOE_ROWS)

    per_row = SMEM_1D_TILE // tile
    for k in range(MOE_TOP_K):
        for q in range(n_tiles // per_row):
            parts = []
            for j in range(q * per_row, (q + 1) * per_row):
                parts.append(jnp.sum(first_slot * one_hots(j)[k], axis=0, keepdims=True) + ranks[j][k])
            dest = jnp.concatenate(parts, axis=1).astype(jnp.int32)
            dest_vmem[pl.ds(k * n_tok + q * SMEM_1D_TILE, SMEM_1D_TILE)] = dest.reshape(SMEM_1D_TILE)

    lane = lax.broadcasted_iota(jnp.int32, (1, SMEM_1D_TILE), 1)
    for q in range(init_vmem.shape[0] // SMEM_1D_TILE):
        pos = q * SMEM_1D_TILE + lane
        spare = MOE_TOP_K * n_tok + ((pos // MOE_ROWS + 1) % 2) * MOE_ROWS + pos % MOE_ROWS
        init_vmem[pl.ds(q * SMEM_1D_TILE, SMEM_1D_TILE)] = spare.reshape(SMEM_1D_TILE)
    copies = [pltpu.make_async_copy(dest_vmem, dest_smem, sem), pltpu.make_async_copy(init_vmem, slot_ref, sem)]
    for cp in copies:
        cp.start()
    for cp in copies:
        cp.wait()

    def place(t, carry_):
        for k in range(MOE_TOP_K):
            slot_ref[dest_smem[k * n_tok + t] + MOE_ROWS] = k * n_tok + t
        return carry_

    lax.fori_loop(0, n_tok, place, 0, unroll=8)

    b = lax.broadcasted_iota(jnp.int32, (N_EXPERTS, LANES), 1).astype(F32)
    e_col = lax.broadcasted_iota(jnp.int32, (N_EXPERTS, LANES), 0).astype(F32)
    b_row = b[0:1]
    be = jnp.minimum(jnp.sum(jnp.where(first_blk + nblk <= b, 1.0, 0.0), axis=0, keepdims=True), N_EXPERTS - 1.0)
    mine = e_col == be
    cnt_b = jnp.sum(jnp.where(mine, counts, 0.0), axis=0, keepdims=True)
    start_b = jnp.sum(jnp.where(mine, first_blk, 0.0), axis=0, keepdims=True)
    nv = jnp.clip(cnt_b - (b_row - start_b) * MOE_ROWS, 0.0, float(MOE_ROWS))
    nv = jnp.where(b_row < jnp.sum(nblk, axis=0, keepdims=True), nv, 0.0)
    first = jnp.where((nv > 0) & ((b_row == 0) | (be != pltpu.roll(be, 1, 1))), 1.0, 0.0)
    li = lax.broadcasted_iota(jnp.int32, (LANES, LANES), 0)
    lj = lax.broadcasted_iota(jnp.int32, (LANES, LANES), 1)
    upto = jnp.where(li <= lj, 1.0, 0.0).astype(BF16)
    run = _dot(jnp.broadcast_to(first, (MOE_ID_ROWS, LANES)).astype(BF16), upto)[0:1] - 1.0
    ws = run - 2.0 * jnp.floor(run * 0.5)
    later = jnp.min(jnp.where((e_col > be) & (counts > 0), e_col, float(LANES)), axis=0, keepdims=True)
    nxt = jnp.where(later >= float(N_EXPERTS), -1.0, later)
    rows = [be, nv, first, ws, nxt] + [jnp.zeros_like(be)] * (MOE_ID_ROWS - 5)
    meta_ref[...] = jnp.concatenate(rows, axis=0).astype(jnp.int32)


def _moe_experts(tok, eid, w1, w3, w2, layer):
    n_tok, n_sub, _ = tok.shape
    d = n_sub * LANES
    assert n_tok & (n_tok - 1) == 0
    ff = w1.shape[3]
    n_assign = n_tok * MOE_TOP_K
    n_blocks = -(-(n_assign + N_EXPERTS * (MOE_ROWS - 1)) // MOE_ROWS)
    slot, be, nv, first, ws, nxt = _moe_dispatch(eid, n_blocks)
    grid_spec = pltpu.PrefetchScalarGridSpec(
        num_scalar_prefetch=5,
        grid=(n_blocks + 1,),
        in_specs=[
            pl.BlockSpec((1, 1, MOE_ROWS), lambda i, *_: (i + 1, 0, 0), memory_space=pltpu.SMEM),
            pl.BlockSpec((1, 1, MOE_ROWS), lambda i, *_: (jnp.minimum(i + 2, n_blocks + 1), 0, 0),
                         memory_space=pltpu.SMEM),
            pl.BlockSpec((1, 1, MOE_ROWS), lambda i, *_: (i, 0, 0), memory_space=pltpu.SMEM),
            pl.BlockSpec(memory_space=pl.ANY),
            pl.BlockSpec(memory_space=pl.ANY),
            pl.BlockSpec(memory_space=pl.ANY),
            pl.BlockSpec(memory_space=pl.ANY),
        ],
        out_specs=pl.BlockSpec(memory_space=pl.ANY),
        scratch_shapes=[
            pltpu.VMEM((2, MOE_ROWS, n_sub, LANES), BF16), pltpu.VMEM((2, MOE_ROWS, n_sub, LANES), BF16),
            pltpu.VMEM((MOE_ROWS, d), BF16),
            pltpu.VMEM((2, d, ff), F32), pltpu.VMEM((2, d, ff), F32), pltpu.VMEM((2, ff, d), F32),
            pltpu.VMEM((d, ff), BF16), pltpu.VMEM((d, ff), BF16), pltpu.VMEM((ff, d), BF16),
            pltpu.SemaphoreType.DMA((2,)), pltpu.SemaphoreType.DMA((2,)), pltpu.SemaphoreType.DMA((2,)),
        ],
    )
    return pl.pallas_call(
        functools.partial(_moe_kernel, n_tok=n_tok, layer=layer, n_blocks=n_blocks),
        grid_spec=grid_spec,
        out_shape=jax.ShapeDtypeStruct((MOE_TOP_K * n_tok + 2 * MOE_ROWS, n_sub, LANES), BF16),
        compiler_params=pltpu.CompilerParams(dimension_semantics=("arbitrary",),
                                             vmem_limit_bytes=MOE_VMEM_LIMIT_BYTES),
        name="moe_experts",
    )(be, nv, first, ws, nxt, slot, slot, slot, tok, w1, w3, w2)


def _rope_tables(n_seq):
    half = HEAD_DIM // 2
    n_freq = half // 2
    t = jnp.arange(n_seq)
    row = (t // GRID_W).astype(F32)
    col = (t % GRID_W).astype(F32)
    inv_freq = ROPE_BASE ** (-jnp.arange(n_freq, dtype=F32) / n_freq)
    ang_r = row[:, None] * inv_freq[None, :]
    ang_c = col[:, None] * inv_freq[None, :]
    cos = jnp.concatenate([jnp.cos(ang_r)] * 2 + [jnp.cos(ang_c)] * 2, axis=-1)
    sin = jnp.concatenate([-jnp.sin(ang_r), jnp.sin(ang_r), -jnp.sin(ang_c), jnp.sin(ang_c)], axis=-1)
    return cos, sin


def _router_weights(w_g, b_g, w_e, b_e):
    d = w_g.shape[0]
    n = w_g.shape[1] + w_e.shape[1]
    wr = jnp.concatenate([w_g, w_e, jnp.zeros((d, LANES - n), F32)], axis=1)
    rb = jnp.concatenate([b_g, b_e, jnp.zeros((LANES - n,), F32)]).reshape(1, LANES)
    hi = wr.astype(BF16)
    lo = (wr - hi.astype(F32)).astype(BF16)
    return hi, lo, rb


def kernel(x, c, ctx, c_ctx, ada_w, ada_b, ln_g, ln_b, mix_w_in, att_sink, hg_lb, hg_norm_g, mix_w_out, pool_w_in, pool_w_grp, pool_scale, pool_w_out, rt_group_w, rt_group_b, rt_expert_w, rt_expert_b, moe_w1, moe_w3, moe_w2):
    b, n, d = x.shape
    n_ctx = ctx.shape[1]
    t = b * n
    xf = x.reshape(t, d)
    ctxf = ctx.reshape(b * n_ctx, d)

    cond = jnp.concatenate([c, c_ctx[None, :], jnp.zeros((8 - b - 1, d), F32)], axis=0)
    mod = _ada_mod(cond, ada_w, ada_b)

    def chunk(l, j, rows=slice(0, b)):
        return mod[l, rows, j * d:(j + 1) * d][:, None, :]

    w_in = mix_w_in[0].astype(BF16)
    cos, sin = _rope_tables(n)
    q_w, kv_w = ATT_HEADS * HEAD_DIM, ATT_KV_HEADS * HEAD_DIM
    n_att = q_w + 2 * kv_w
    hk = HG_HEADS * HG_KEY
    assert n_att == 2 * PROJ_TN and hk == PROJ_TN
    a_lat, f_lat = _mod_matmul(xf, chunk(0, 1), chunk(0, 0), w_in, cos, sin,
                               lambda j: jnp.where(j < 3, j, jnp.where(j < 5, j + 2, j - 2)), n_att + 3 * hk, 2 * hk,
                               n_q=q_w, n_rope=q_w + kv_w, n_seq=n, tm=1024, tn=PROJ_TN)
    ctx_rows = slice(b, b + 1)
    a_ctx, f_ctx = _mod_matmul(ctxf, chunk(0, 1, ctx_rows), chunk(0, 0, ctx_rows), w_in, cos, sin,
                               lambda j: jnp.where(j < 1, 1, jnp.where(j < 2, 5, j + 1)), 2 * kv_w + hk, 2 * hk,
                               n_q=0, n_rope=0, n_seq=n, tm=b * n_ctx, tn=PROJ_TN)
    att = _window_attention(a_lat, a_ctx, att_sink[0], b, n, n_ctx)
    o_f = _hgrn2_scan(a_lat, f_lat, a_ctx, f_ctx, hg_lb, b, n, n_ctx, rev=False)
    o_b = _hgrn2_scan(a_lat, f_lat, a_ctx, f_ctx, hg_lb, b, n, n_ctx, rev=True)
    wr_hi, wr_lo, rb = _router_weights(rt_group_w[0], rt_group_b[0], rt_expert_w[0], rt_expert_b[0])
    x1, tok, eid, gate = _even_out(
        att, o_f, o_b, a_lat, hg_norm_g[0][None, :], mix_w_out[0].astype(BF16), xf,
        chunk(0, 2), chunk(0, 4), chunk(0, 3), ln_g[0, 0][None, :], ln_b[0, 0][None, :], wr_hi, wr_lo, rb, n)
    y2 = _moe_experts(tok, eid, moe_w1, moe_w3, moe_w2, 0)

    x2, u = _combine_call(x1, y2, gate, chunk(0, 5), ln_g[0, 1][None, :], ln_b[0, 1][None, :], n,
                          proj=(chunk(1, 1), chunk(1, 0), pool_w_in[0].astype(BF16)))
    wr_hi, wr_lo, rb = _router_weights(rt_group_w[1], rt_group_b[1], rt_expert_w[1], rt_expert_b[1])
    x3, tok, eid, gate = _pool_out(
        u, pool_w_grp[0].astype(BF16), pool_scale[0][None, :], pool_w_out[0].astype(BF16), x2,
        chunk(1, 2), chunk(1, 4), chunk(1, 3), ln_g[1, 0][None, :], ln_b[1, 0][None, :], wr_hi, wr_lo, rb, n)
    y2 = _moe_experts(tok, eid, moe_w1, moe_w3, moe_w2, 1)
    out = _combine_call(x3, y2, gate, chunk(1, 5), ln_g[1, 1][None, :], ln_b[1, 1][None, :], n)
    return out.reshape(b, n, d)
```

```python
import functools

import jax
import jax.numpy as jnp
from jax import lax
from jax.experimental import pallas as pl
from jax.experimental.pallas import tpu as pltpu

F32 = jnp.float32
BF16 = jnp.bfloat16

LANES = 128
MXU_TILE = 256
VMEM_LIMIT_BYTES = 56 * 1024 * 1024
MOE_VMEM_LIMIT_BYTES = 60 * 1024 * 1024

GRID_W = 64
ATT_HEADS = 8
ATT_KV_HEADS = 4
ATT_GROUP = ATT_HEADS // ATT_KV_HEADS
HEAD_DIM = 128
WINDOW = 128
ATT_BLOCK = 128
ROPE_BASE = 10000.0
HG_HEADS = 8
HG_KEY = 128
HG_CHUNK = 64
HG_STEP_CHUNKS = 4
HG_SUB = 16
HG_FAST_RANGE = 80.0
NORM_EPS = 1e-6
POOL_WINDOWS = (2, 4, 8, 16)
POOL_HALO = 8
MOE_GROUPS = 4
MOE_EXPERTS_PER_GROUP = 8
N_EXPERTS = MOE_GROUPS * MOE_EXPERTS_PER_GROUP
MOE_TOP_K = 2
MOE_ROWS = 256
ROW_TILE = 512
SUB_ROWS = 256
PROJ_TN = 1024
MOE_ID_ROWS = 8
SMEM_1D_TILE = 1024
WEIGHT_DMA_PRIORITY = 1
LN_EPS = 1e-5
DEPTH = 2
DEEPNORM_ALPHA = (2 * DEPTH) ** 0.25


def _dot(a, b):
    return jnp.dot(a, b, preferred_element_type=F32)


def _dot_nt(a, b):
    return lax.dot_general(a, b, (((1,), (1,)), ((), ())), preferred_element_type=F32)


def _dot_tn(a, b):
    return lax.dot_general(a, b, (((0,), (0,)), ((), ())), preferred_element_type=F32)


def _sigmoid(x):
    return 1.0 / (1.0 + jnp.exp(-x))


def _silu(x):
    return x * _sigmoid(x)


def _params(*sem):
    return pltpu.CompilerParams(dimension_semantics=sem, vmem_limit_bytes=VMEM_LIMIT_BYTES)


def _tiles_from_rows(x):
    n = x.shape[1] // LANES
    return jnp.swapaxes(jnp.stack([x[:, s * LANES:(s + 1) * LANES] for s in range(n)], axis=0), 0, 1)


def _rows_from_tiles(x3):
    xt = jnp.swapaxes(x3, 0, 1)
    return jnp.concatenate([xt[s] for s in range(xt.shape[0])], axis=-1)


def _layer_norm(z, g, b):
    mu = jnp.mean(z, axis=-1, keepdims=True)
    zc = z - mu
    var = jnp.mean(zc * zc, axis=-1, keepdims=True)
    return zc * lax.rsqrt(var + LN_EPS) * g + b


def _ada_kernel(s_ref, w_ref, b_ref, o_ref):
    s = _silu(s_ref[...]).astype(BF16)
    o_ref[0] = _dot(s, w_ref[0].astype(BF16)) + b_ref[0]


def _ada_mod(s, ada_w, ada_b, tn=1024):
    n_l, d, n = ada_w.shape
    return pl.pallas_call(
        _ada_kernel,
        grid=(n_l, n // tn),
        in_specs=[
            pl.BlockSpec((8, d), lambda l, j: (0, 0)),
            pl.BlockSpec((1, d, tn), lambda l, j: (l, 0, j)),
            pl.BlockSpec((1, 1, tn), lambda l, j: (l, 0, j)),
        ],
        out_specs=pl.BlockSpec((1, 8, tn), lambda l, j: (l, 0, j)),
        out_shape=jax.ShapeDtypeStruct((n_l, 8, n), F32),
        compiler_params=_params("parallel", "parallel"),
        name="ada_mod",
    )(s, ada_w, ada_b.reshape(n_l, 1, n))


def _rope(t, cos, sin_signed, first_half):
    partner = jnp.where(first_half, pltpu.roll(t, 96, 1), pltpu.roll(t, 32, 1))
    return t * cos + partner * sin_signed


def _modmm_kernel(x_ref, sc_ref, sh_ref, w_ref, cos_ref, sin_ref, oa_ref, ob_ref, xs_ref, *,
                  n_q, n_rope, n_a_tiles):
    j = pl.program_id(1)

    @pl.when(j == 0)
    def _():
        xs_ref[...] = (x_ref[...] * (1.0 + sc_ref[0]) + sh_ref[0]).astype(BF16)

    acc = _dot(xs_ref[...], w_ref[...])
    tn = acc.shape[1]

    for jt in range(n_a_tiles):
        @pl.when(j == jt)
        def _(jt=jt):
            lane = lax.broadcasted_iota(jnp.int32, (1, HEAD_DIM), 1)
            first_half = (lane % 64) < 32
            for h in range(tn // HEAD_DIM):
                sl = slice(h * HEAD_DIM, (h + 1) * HEAD_DIM)
                col = jt * tn + h * HEAD_DIM
                if col < n_rope:
                    scale = HEAD_DIM ** -0.5 if col < n_q else 1.0
                    oa_ref[:, sl] = _rope(acc[:, sl], cos_ref[...] * scale, sin_ref[...] * scale,
                                          first_half).astype(BF16)
                else:
                    oa_ref[:, sl] = acc[:, sl].astype(BF16)

    @pl.when(j >= n_a_tiles)
    def _():
        ob_ref[...] = acc


def _mod_matmul(x, sc, sh, w, cos, sin, col_map, n_a, n_b, n_q, n_rope, n_seq, tm, tn):
    m, k = x.shape
    rows_per_mod = m // sc.shape[0]
    ta, tb = n_a // tn, n_b // tn
    tab = pl.BlockSpec((tm, HEAD_DIM), lambda i, j: ((i * tm % n_seq) // tm, 0))
    return pl.pallas_call(
        functools.partial(_modmm_kernel, n_q=n_q, n_rope=n_rope, n_a_tiles=ta),
        grid=(m // tm, ta + tb),
        in_specs=[
            pl.BlockSpec((tm, k), lambda i, j: (i, 0)),
            pl.BlockSpec((1, 1, k), lambda i, j: (i * tm // rows_per_mod, 0, 0)),
            pl.BlockSpec((1, 1, k), lambda i, j: (i * tm // rows_per_mod, 0, 0)),
            pl.BlockSpec((k, tn), lambda i, j: (0, col_map(j))),
            tab, tab,
        ],
        out_specs=[pl.BlockSpec((tm, tn), lambda i, j: (i, jnp.minimum(j, ta - 1))),
                   pl.BlockSpec((tm, tn), lambda i, j: (i, jnp.maximum(j - ta, 0)))],
        out_shape=[jax.ShapeDtypeStruct((m, n_a), BF16), jax.ShapeDtypeStruct((m, n_b), F32)],
        scratch_shapes=[pltpu.VMEM((tm, k), BF16)],
        compiler_params=_params("parallel", "arbitrary"),
        name="mod_matmul",
    )(x, sc, sh, w, cos, sin)


def _attn_kernel(sink_ref, q_ref, kp_ref, kc_ref, kn_ref, vp_ref, vc_ref, vn_ref, kx_ref, vx_ref,
                 mp_ref, mn_ref, o_ref, *, n_blocks):
    n = pl.program_id(1)
    blk = ATT_BLOCK
    row1 = lax.broadcasted_iota(jnp.int32, (ATT_GROUP * blk, 1), 0)
    has_prev, has_next = n > 0, n < n_blocks - 1
    for h in range(ATT_KV_HEADS):
        kv = slice(h * HEAD_DIM, (h + 1) * HEAD_DIM)
        q2 = jnp.concatenate([q_ref[:, (ATT_GROUP * h + g) * HEAD_DIM:(ATT_GROUP * h + g + 1) * HEAD_DIM]
                              for g in range(ATT_GROUP)], axis=0)
        s_p = jnp.where(has_prev, _dot_nt(q2, kp_ref[:, kv]) + mp_ref[...], -jnp.inf)
        s_c = _dot_nt(q2, kc_ref[:, kv])
        s_n = jnp.where(has_next, _dot_nt(q2, kn_ref[:, kv]) + mn_ref[...], -jnp.inf)
        s_x = _dot_nt(q2, kx_ref[:, kv])
        sink = jnp.where(row1 < blk, sink_ref[ATT_GROUP * h], sink_ref[ATT_GROUP * h + 1])
        m = jnp.maximum(jnp.maximum(jnp.max(s_p, axis=-1, keepdims=True), jnp.max(s_c, axis=-1, keepdims=True)),
                        jnp.maximum(jnp.max(s_n, axis=-1, keepdims=True), jnp.max(s_x, axis=-1, keepdims=True)))
        m = jnp.maximum(m, sink)
        p_p, p_c, p_n, p_x = (jnp.exp(s - m) for s in (s_p, s_c, s_n, s_x))
        den = (jnp.sum(p_p, axis=-1, keepdims=True) + jnp.sum(p_c, axis=-1, keepdims=True)
               + jnp.sum(p_n, axis=-1, keepdims=True) + jnp.sum(p_x, axis=-1, keepdims=True)
               + jnp.exp(sink - m))
        o = (_dot(p_p.astype(BF16), vp_ref[:, kv]) + _dot(p_c.astype(BF16), vc_ref[:, kv])
             + _dot(p_n.astype(BF16), vn_ref[:, kv]) + _dot(p_x.astype(BF16), vx_ref[:, kv])) / den
        for g in range(ATT_GROUP):
            col = (ATT_GROUP * h + g) * HEAD_DIM
            o_ref[:, col:col + HEAD_DIM] = o[g * blk:(g + 1) * blk].astype(o_ref.dtype)


def _window_attention(qkv, kv_ctx, sink, batch, n_seq, n_ctx):
    assert ATT_GROUP == 2 and WINDOW == ATT_BLOCK
    nb = n_seq // ATT_BLOCK
    qw, kw = ATT_HEADS * HEAD_DIM, ATT_KV_HEADS * HEAD_DIM
    kcol, vcol = qw // kw, qw // kw + 1

    def rows(off):
        return lambda b, n, off=off: b * nb + jnp.clip(n + off, 0, nb - 1)

    kspec = [pl.BlockSpec((ATT_BLOCK, kw), lambda b, n, r=rows(o): (r(b, n), kcol)) for o in (-1, 0, 1)]
    vspec = [pl.BlockSpec((ATT_BLOCK, kw), lambda b, n, r=rows(o): (r(b, n), vcol)) for o in (-1, 0, 1)]
    r = jnp.arange(ATT_GROUP * ATT_BLOCK)[:, None] % ATT_BLOCK
    c = jnp.arange(ATT_BLOCK)[None, :]
    mask_prev = jnp.where(c >= r, 0.0, -jnp.inf).astype(F32)
    mask_next = jnp.where(c <= r, 0.0, -jnp.inf).astype(F32)
    mspec = pl.BlockSpec(mask_prev.shape, lambda b, n: (0, 0))
    return pl.pallas_call(
        functools.partial(_attn_kernel, n_blocks=nb),
        grid=(batch, nb),
        in_specs=[pl.BlockSpec(memory_space=pltpu.SMEM),
                  pl.BlockSpec((ATT_BLOCK, qw), lambda b, n: (b * nb + n, 0))]
        + kspec + vspec
        + [pl.BlockSpec((n_ctx, kw), lambda b, n: (b, 0)), pl.BlockSpec((n_ctx, kw), lambda b, n: (b, 1)),
           mspec, mspec],
        out_specs=pl.BlockSpec((ATT_BLOCK, qw), lambda b, n: (b * nb + n, 0)),
        out_shape=jax.ShapeDtypeStruct((batch * n_seq, qw), BF16),
        compiler_params=_params("parallel", "parallel"),
        name="window_attention",
    )(sink, qkv, qkv, qkv, qkv, qkv, qkv, qkv, kv_ctx, kv_ctx, mask_prev, mask_next)


def _gla_step(zf, q_raw, v, lb, st_ref, o_ref, rev):
    c_len = HG_CHUNK
    n_rows = zf.shape[0]
    n_sub = n_rows // c_len
    shift = c_len.bit_length() - 1
    order = range(n_sub - 1, -1, -1) if rev else range(n_sub)

    def head(h):
        return slice(h * HG_KEY, (h + 1) * HG_KEY)

    def chunk(i):
        return slice(i * c_len, (i + 1) * c_len)

    def seen(n):
        ri = lax.broadcasted_iota(jnp.int32, (n, n), 0)
        ci = lax.broadcasted_iota(jnp.int32, (n, n), 1)
        return ((ri >> shift) == (ci >> shift)) & ((ci >= ri) if rev else (ci <= ri))

    f = lb + (1.0 - lb) * _sigmoid(zf)
    k = 1.0 - f
    g = jnp.log(f)
    tri = jnp.where(seen(n_rows), 1.0, 0.0).astype(BF16)
    g1 = g.astype(BF16)
    r1 = g - g1.astype(F32)
    g2 = r1.astype(BF16)
    g3 = (r1 - g2.astype(F32)).astype(BF16)
    c = _dot(tri, g1) + _dot(tri, g2) + _dot(tri, g3)
    c_end = [c[i * c_len:i * c_len + 1] if rev else c[(i + 1) * c_len - 1:(i + 1) * c_len] for i in range(n_sub)]
    c_end_rows = jnp.concatenate([jnp.broadcast_to(ce, (c_len, ce.shape[1])) for ce in c_end], axis=0)
    k_end = (k * jnp.exp(c_end_rows - c)).astype(BF16)
    dec = [jnp.exp(ce) for ce in c_end]
    vb = v.astype(BF16)

    def advance(states, i):
        new = []
        for p in range(0, HG_HEADS, 2):
            lanes = slice(p * HG_KEY, (p + 2) * HG_KEY)
            inc = _dot_tn(vb[chunk(i), lanes], k_end[chunk(i), lanes])
            for j in range(2):
                blk = slice(j * HG_KEY, (j + 1) * HG_KEY)
                new.append(states[p + j] * dec[i][:, head(p + j)] + inc[blk, blk])
        return new

    if o_ref is None:
        states = [st_ref[h] for h in range(HG_HEADS)]
        for i in order:
            states = advance(states, i)
        for h in range(HG_HEADS):
            st_ref[h] = states[h]
        return

    q = _silu(q_raw.astype(F32))
    q_in = (q * jnp.exp(c)).astype(BF16)
    lowest = functools.reduce(jnp.minimum, c_end)
    in_range = jnp.min(lowest) >= -HG_FAST_RANGE

    def stack(x, i, h0, n):
        return jnp.concatenate([x[chunk(i), head(h)] for h in range(h0, h0 + n)], axis=0)

    @pl.when(in_range)
    def _():
        k_in = (k * jnp.exp(-c)).astype(BF16)
        n_qk = MXU_TILE // c_len
        same = seen(n_qk * c_len)
        intra = {}
        for i in order:
            for h0 in range(0, HG_HEADS, n_qk):
                sc = jnp.where(same, _dot_nt(stack(q_in, i, h0, n_qk), stack(k_in, i, h0, n_qk)), 0.0)
                pv = _dot(sc.astype(BF16), stack(vb, i, h0, n_qk))
                for j in range(n_qk):
                    intra[i, h0 + j] = pv[j * c_len:(j + 1) * c_len]
        states = [st_ref[h] for h in range(HG_HEADS)]
        for i in order:
            out = []
            for p in range(0, HG_HEADS, 2):
                st_pair = jnp.concatenate([states[p].astype(BF16), states[p + 1].astype(BF16)], axis=0)
                inter = _dot_nt(stack(q_in, i, p, 2), st_pair)
                for j in range(2):
                    out.append(inter[j * c_len:(j + 1) * c_len, j * HG_KEY:(j + 1) * HG_KEY] + intra[i, p + j])
            o_ref[chunk(i), :] = jnp.concatenate(out, axis=1)
            states = advance(states, i)
        for h in range(HG_HEADS):
            st_ref[h] = states[h]

    @pl.when(jnp.logical_not(in_range))
    def _():
        for i in order:
            r = chunk(i)
            _gla_intra_exact(q[r], k[r], v[r].astype(F32), vb[r], c[r], q_in[r], st_ref, o_ref, r, rev)
            states = advance([st_ref[h] for h in range(HG_HEADS)], i)
            for h in range(HG_HEADS):
                st_ref[h] = states[h]


def _gla_intra_exact(q, k, v, vb, c, q_in, st_ref, o_ref, rows, rev):
    c_len = q.shape[0]
    pairs = []
    size = c_len // 2
    while size >= HG_SUB:
        for lo in range(0, c_len, 2 * size):
            pairs.append((lo, lo + size, lo + 2 * size))
        size //= 2
    scaled = []
    for lo, mid, hi in pairs:
        if rev:
            late, early, bnd = slice(lo, mid), slice(mid, hi), mid
        else:
            late, early, bnd = slice(mid, hi), slice(lo, mid), mid - 1
        cb = c[bnd:bnd + 1]
        q_l = (q[late] * jnp.exp(c[late] - cb)).astype(BF16)
        k_e = (k[early] * jnp.exp(cb - c[early])).astype(BF16)
        scaled.append((late, early, q_l, k_e))
    n_sub = c_len // HG_SUB
    t_idx = lax.broadcasted_iota(jnp.int32, (HG_SUB, 1), 0)
    diag = [[None] * HG_HEADS for _ in range(n_sub)]
    for b in range(n_sub):
        r0 = b * HG_SUB
        qb, cb = q[r0:r0 + HG_SUB], c[r0:r0 + HG_SUB]
        for s in range(HG_SUB):
            row = r0 + s
            ok = (t_idx <= s) if rev else (t_idx >= s)
            w = qb * k[row:row + 1] * jnp.exp(jnp.where(ok, cb - c[row:row + 1], -jnp.inf))
            for h in range(HG_HEADS):
                sl = slice(h * HG_KEY, (h + 1) * HG_KEY)
                contrib = jnp.sum(w[:, sl], axis=-1, keepdims=True) * v[row:row + 1, sl]
                diag[b][h] = contrib if diag[b][h] is None else diag[b][h] + contrib

    for h in range(HG_HEADS):
        sl = slice(h * HG_KEY, (h + 1) * HG_KEY)
        o_h = _dot_nt(q_in[:, sl], st_ref[h].astype(BF16))
        parts = [diag[b][h] for b in range(n_sub)]
        for late, early, q_l, k_e in scaled:
            sc = _dot_nt(q_l[:, sl], k_e[:, sl]).astype(BF16)
            add = _dot(sc, vb[early, sl])
            b0 = late.start // HG_SUB
            for j in range((late.stop - late.start) // HG_SUB):
                parts[b0 + j] = parts[b0 + j] + add[j * HG_SUB:(j + 1) * HG_SUB]
        o_ref[rows, sl] = o_h + jnp.concatenate(parts, axis=0)


def _gla_kernel(lb_ref, zf_ref, q_ref, v_ref, zfc_ref, vc_ref, o_ref, st_ref, *, rev, n_ctx_steps):
    s = pl.program_id(1)

    @pl.when(s == 0)
    def _():
        st_ref[...] = jnp.zeros_like(st_ref)

    x = lb_ref[...]
    e = jnp.exp(x - jnp.max(x, axis=0, keepdims=True))
    lb = e[0:1] / jnp.sum(e, axis=0, keepdims=True)

    @pl.when(s < n_ctx_steps)
    def _():
        _gla_step(zfc_ref[...], None, vc_ref[...], lb, st_ref, None, rev)

    @pl.when(s >= n_ctx_steps)
    def _():
        _gla_step(zf_ref[...], q_ref[...], v_ref[...], lb, st_ref, o_ref, rev)


def _hgrn2_scan(a_lat, f_lat, a_ctx, f_ctx, hg_lb, batch, n_seq, n_ctx, rev):
    hk = HG_HEADS * HG_KEY
    rows = HG_CHUNK * HG_STEP_CHUNKS
    assert n_seq % rows == 0 and n_ctx % rows == 0
    nc, ncc = n_seq // rows, n_ctx // rows
    d = 1 if rev else 0

    def lat(b, s):
        j = jnp.maximum(s - ncc, 0)
        return b * nc + (nc - 1 - j if rev else j)

    def ctx(b, s):
        j = jnp.minimum(s, ncc - 1)
        return b * ncc + (ncc - 1 - j if rev else j)

    return pl.pallas_call(
        functools.partial(_gla_kernel, rev=rev, n_ctx_steps=ncc),
        grid=(batch, ncc + nc),
        in_specs=[
            pl.BlockSpec((None, hg_lb.shape[1], hk), lambda b, s: (d, 0, 0)),
            pl.BlockSpec((rows, hk), lambda b, s: (lat(b, s), d)),
            pl.BlockSpec((rows, hk), lambda b, s: (lat(b, s), 2)),
            pl.BlockSpec((rows, hk), lambda b, s: (lat(b, s), 3)),
            pl.BlockSpec((rows, hk), lambda b, s: (ctx(b, s), d)),
            pl.BlockSpec((rows, hk), lambda b, s: (ctx(b, s), 1)),
        ],
        out_specs=pl.BlockSpec((rows, hk), lambda b, s: (lat(b, s), 0)),
        out_shape=jax.ShapeDtypeStruct((batch * n_seq, hk), F32),
        scratch_shapes=[pltpu.VMEM((HG_HEADS, HG_KEY, HG_KEY), F32)],
        compiler_params=_params("parallel", "arbitrary"),
        name="hgrn2_bwd" if rev else "hgrn2_fwd",
    )(hg_lb, f_lat, a_lat, a_lat, f_ctx, a_ctx)


def _route(tok, wr_hi, wr_lo, rb):
    t_hi = tok.astype(BF16)
    t_lo = (tok - t_hi.astype(F32)).astype(BF16)
    lg = _dot(t_hi, wr_hi) + _dot(t_hi, wr_lo) + _dot(t_lo, wr_hi) + rb
    lane = lax.broadcasted_iota(jnp.int32, lg.shape, 1)
    lane_f = lane.astype(F32)
    ninf = -jnp.inf
    gl = jnp.where(lane < MOE_GROUPS, lg, ninf)
    gmax = jnp.max(gl, axis=-1, keepdims=True)
    g_idx = jnp.min(jnp.where(gl == gmax, lane_f, float(LANES)), axis=-1, keepdims=True)
    g_val = 1.0 / jnp.sum(jnp.exp(gl - gmax), axis=-1, keepdims=True)
    e_lane = lane_f - float(MOE_GROUPS)
    lo = g_idx * float(MOE_EXPERTS_PER_GROUP)
    in_grp = (e_lane >= lo) & (e_lane < lo + float(MOE_EXPERTS_PER_GROUP))
    el = jnp.where(in_grp, lg, ninf)
    l1 = jnp.max(el, axis=-1, keepdims=True)
    i1 = jnp.min(jnp.where(el == l1, e_lane, float(LANES)), axis=-1, keepdims=True)
    el2 = jnp.where(e_lane == i1, ninf, el)
    l2 = jnp.max(el2, axis=-1, keepdims=True)
    i2 = jnp.min(jnp.where(el2 == l2, e_lane, float(LANES)), axis=-1, keepdims=True)
    r = jnp.exp(l2 - l1)
    w1 = g_val / (1.0 + r)
    w2 = w1 * r
    eid = jnp.where(lane == 0, i1, jnp.where(lane == 1, i2, 0.0))
    gate = jnp.where(lane == 0, w1, jnp.where(lane == 1, w2, 0.0))
    eid_t = jnp.transpose(eid)[:MOE_ID_ROWS].astype(jnp.int32)
    return eid_t, gate


def _post_mix(y, x_ref, g1_ref, sc2_ref, sh2_ref, lng_ref, lnb_ref, wrh_ref, wrl_ref, rb_ref,
              x1_ref, tok_ref, eid_ref, gate_ref, r):
    x1 = _layer_norm(DEEPNORM_ALPHA * x_ref[r] + g1_ref[0] * y, lng_ref[...], lnb_ref[...])
    x1_ref[r] = x1
    tok = x1 * (1.0 + sc2_ref[0]) + sh2_ref[0]
    tok_ref[r] = _tiles_from_rows(tok).astype(BF16)
    eid, gate = _route(tok, wrh_ref[...], wrl_ref[...], rb_ref[...])
    eid_ref[:, r] = eid
    gate_ref[r] = gate


def _sub_tiles(n_rows):
    return [slice(s, s + SUB_ROWS) for s in range(0, n_rows, SUB_ROWS)]


def _even_out_kernel(att_ref, of_ref, ob_ref, gt_ref, ng_ref, wo_ref, *rest):
    tiles = _sub_tiles(att_ref.shape[0])
    lhs = []
    for r in tiles:
        o = of_ref[r] + ob_ref[r]
        pieces = []
        for h in range(HG_HEADS):
            oh = o[:, h * HG_KEY:(h + 1) * HG_KEY]
            pieces.append(oh * lax.rsqrt(jnp.mean(oh * oh, axis=-1, keepdims=True) + NORM_EPS))
        hg = (jnp.concatenate(pieces, axis=-1) * ng_ref[...] * _silu(gt_ref[r].astype(F32))).astype(BF16)
        lhs.append(jnp.concatenate([att_ref[r], hg], axis=-1))
    ys = [_dot(a, wo_ref[...]) for a in lhs]
    for r, y in zip(tiles, ys):
        _post_mix(y, *rest, r)


def _post_specs(d, tm, rows_per_batch):
    def bmap(i):
        return (i * tm // rows_per_batch, 0, 0)

    row = pl.BlockSpec((tm, d), lambda i: (i, 0))
    mod = pl.BlockSpec((1, 1, d), bmap)
    vec = pl.BlockSpec((1, d), lambda i: (0, 0))
    rw = pl.BlockSpec((d, LANES), lambda i: (0, 0))
    in_specs = [row, mod, mod, mod, vec, vec, rw, rw, pl.BlockSpec((1, LANES), lambda i: (0, 0))]
    lane_blk = pl.BlockSpec((tm, LANES), lambda i: (i, 0))
    tiles = pl.BlockSpec((tm, d // LANES, LANES), lambda i: (i, 0, 0))
    out_specs = [row, tiles, pl.BlockSpec((MOE_ID_ROWS, tm), lambda i: (0, i)), lane_blk]
    return in_specs, out_specs


def _post_out_shapes(t, d):
    return [jax.ShapeDtypeStruct((t, d), F32), jax.ShapeDtypeStruct((t, d // LANES, LANES), BF16),
            jax.ShapeDtypeStruct((MOE_ID_ROWS, t), jnp.int32), jax.ShapeDtypeStruct((t, LANES), F32)]


def _resident(shape):
    return pl.BlockSpec(shape, lambda i: (0,) * len(shape), pipeline_mode=pl.Buffered(1))


def _even_out(att, o_f, o_b, p, norm_g, w_out, x, g1, sc2, sh2, lng, lnb, wr_hi, wr_lo, rb, rows_per_batch,
              tm=ROW_TILE):
    t, d = x.shape
    hv = o_f.shape[1]
    post_in, post_out = _post_specs(d, tm, rows_per_batch)
    return pl.pallas_call(
        _even_out_kernel,
        grid=(t // tm,),
        in_specs=[
            pl.BlockSpec((tm, att.shape[1]), lambda i: (i, 0)),
            pl.BlockSpec((tm, hv), lambda i: (i, 0)),
            pl.BlockSpec((tm, hv), lambda i: (i, 0)),
            pl.BlockSpec((tm, hv), lambda i: (i, 4)),
            pl.BlockSpec((1, hv), lambda i: (0, 0)),
            _resident(w_out.shape),
        ] + post_in,
        out_specs=post_out,
        out_shape=_post_out_shapes(t, d),
        compiler_params=_params("parallel"),
        name="even_out",
    )(att, o_f, o_b, p, norm_g, w_out, x, g1, sc2, sh2, lng, lnb, wr_hi, wr_lo, rb)


def _combine(x_ref, ya_ref, yb_ref, gate_ref, g2_ref, lng_ref, lnb_ref, r):
    gate = gate_ref[r]
    y = (gate[:, 0:1] * _rows_from_tiles(ya_ref[r].astype(F32))
         + gate[:, 1:2] * _rows_from_tiles(yb_ref[r].astype(F32)))
    return _layer_norm(DEEPNORM_ALPHA * x_ref[r] + g2_ref[0] * y, lng_ref[...], lnb_ref[...])


def _combine_proj_kernel(x_ref, ya_ref, yb_ref, gate_ref, g2_ref, lng_ref, lnb_ref, sc_ref, sh_ref, w_ref,
                         x2_ref, u_ref):
    tiles = _sub_tiles(x_ref.shape[0])
    lhs = []
    for r in tiles:
        x2 = _combine(x_ref, ya_ref, yb_ref, gate_ref, g2_ref, lng_ref, lnb_ref, r)
        x2_ref[r] = x2
        lhs.append((x2 * (1.0 + sc_ref[0]) + sh_ref[0]).astype(BF16))
    for r, a in zip(tiles, lhs):
        u_ref[r] = _dot(a, w_ref[...])


def _combine_kernel(x_ref, ya_ref, yb_ref, gate_ref, g2_ref, lng_ref, lnb_ref, x2_ref):
    for r in _sub_tiles(x_ref.shape[0]):
        x2_ref[r] = _combine(x_ref, ya_ref, yb_ref, gate_ref, g2_ref, lng_ref, lnb_ref, r)


def _combine_call(x1, y2, gate, g2, lng, lnb, rows_per_batch, proj=None, tm=ROW_TILE):
    t, d = x1.shape
    nt = t // tm

    def bmap(i):
        return (i * tm // rows_per_batch, 0, 0)

    row = pl.BlockSpec((tm, d), lambda i: (i, 0))
    mod = pl.BlockSpec((1, 1, d), bmap)
    vec = pl.BlockSpec((1, d), lambda i: (0, 0))
    in_specs = [row, pl.BlockSpec((tm, d // LANES, LANES), lambda i: (i, 0, 0)),
                pl.BlockSpec((tm, d // LANES, LANES), lambda i: (nt + i, 0, 0)),
                pl.BlockSpec((tm, LANES), lambda i: (i, 0)), mod, vec, vec]
    args = [x1, y2, y2, gate, g2, lng, lnb]
    if proj is None:
        return pl.pallas_call(
            _combine_kernel, grid=(nt,), in_specs=in_specs, out_specs=row,
            out_shape=jax.ShapeDtypeStruct((t, d), F32),
            compiler_params=_params("parallel"), name="combine_ln")(*args)
    sc, sh, w = proj
    return pl.pallas_call(
        _combine_proj_kernel, grid=(nt,),
        in_specs=in_specs + [mod, mod, _resident(w.shape)],
        out_specs=[row, pl.BlockSpec((tm, w.shape[1]), lambda i: (i, 0))],
        out_shape=[jax.ShapeDtypeStruct((t, d), F32), jax.ShapeDtypeStruct((t, w.shape[1]), F32)],
        compiler_params=_params("parallel"), name="combine_ln_proj")(*args, sc, sh, w)


def _pool_out_kernel(up_ref, uc_ref, un_ref, wg_ref, ps_ref, wo_ref, *rest, n_seq):
    tm, d = uc_ref.shape
    n_grp = len(POOL_WINDOWS)
    ch = d // n_grp
    halo = POOL_HALO
    tiles = _sub_tiles(tm)
    lhs = []
    for r in tiles:
        n_r = r.stop - r.start
        pos0 = (pl.program_id(0) * tm + r.start) % n_seq
        e_pos = pos0 - halo + lax.broadcasted_iota(jnp.int32, (n_r + 2 * halo, 1), 0)
        e_ok = (e_pos >= 0) & (e_pos < n_seq)
        t_pos = pos0 + lax.broadcasted_iota(jnp.int32, (n_r, 1), 0)
        z = []
        for gi, w in enumerate(POOL_WINDOWS):
            cs = slice(gi * ch, (gi + 1) * ch)
            u = uc_ref[r, cs]
            before = up_ref[:, cs] if r.start == 0 else uc_ref[r.start - halo:r.start, cs]
            after = un_ref[:, cs] if r.stop == tm else uc_ref[r.stop:r.stop + halo, cs]
            ext = jnp.where(e_ok, jnp.concatenate([before, u, after], axis=0), 0.0)
            a, span = ext, 1
            while span < w:
                a = a[:a.shape[0] - span] + a[span:]
                span *= 2
            start = halo - w // 2
            win = a[start:start + n_r]
            cnt = (jnp.minimum(t_pos + (w - w // 2), n_seq) - jnp.maximum(t_pos - w // 2, 0)).astype(F32)
            mixed = (win / cnt - u).astype(BF16)
            z.append((_dot(mixed, wg_ref[gi]) * ps_ref[:, cs]).astype(BF16))
        lhs.append(jnp.concatenate(z, axis=-1))
    ys = [_dot(a, wo_ref[...]) for a in lhs]
    for r, y in zip(tiles, ys):
        _post_mix(y, *rest, r)


def _pool_out(u, w_grp, scale, w_out, x, g1, sc2, sh2, lng, lnb, wr_hi, wr_lo, rb, n_seq, tm=ROW_TILE):
    t, d = x.shape
    hb = tm // POOL_HALO
    n_hb = t // POOL_HALO
    post_in, post_out = _post_specs(d, tm, n_seq)
    return pl.pallas_call(
        functools.partial(_pool_out_kernel, n_seq=n_seq),
        grid=(t // tm,),
        in_specs=[
            pl.BlockSpec((POOL_HALO, d), lambda i: (jnp.maximum(i * hb - 1, 0), 0)),
            pl.BlockSpec((tm, d), lambda i: (i, 0)),
            pl.BlockSpec((POOL_HALO, d), lambda i: (jnp.minimum((i + 1) * hb, n_hb - 1), 0)),
            _resident(w_grp.shape),
            pl.BlockSpec((1, d), lambda i: (0, 0)),
            _resident(w_out.shape),
        ] + post_in,
        out_specs=post_out,
        out_shape=_post_out_shapes(t, d),
        compiler_params=_params("parallel"),
        name="pool_out",
    )(u, u, u, w_grp, scale, w_out, x, g1, sc2, sh2, lng, lnb, wr_hi, wr_lo, rb)


def _moe_kernel(be_ref, nv_ref, first_ref, ws_ref, nxt_ref, idx_ref, idxn_ref, idxp_ref, tok_hbm, w1_hbm, w3_hbm, w2_hbm,
                y_hbm, xbuf, ybuf, xb_ref, wf1, wf3, wf2, w1b, w3b, w2b, gsem, ssem, wsem, *, n_tok, layer, n_blocks):
    i = pl.program_id(0)
    used = nv_ref[jnp.minimum(i, n_blocks - 1)] > 0
    used = used & (i < n_blocks)
    prev_used = (i > 0) & (nv_ref[jnp.maximum(i - 1, 0)] > 0)
    xs = i % 2

    def weight_copies(e, ws):
        return (pltpu.make_async_copy(w1_hbm.at[layer, e], wf1.at[ws], wsem.at[ws]),
                pltpu.make_async_copy(w3_hbm.at[layer, e], wf3.at[ws], wsem.at[ws]),
                pltpu.make_async_copy(w2_hbm.at[layer, e], wf2.at[ws], wsem.at[ws]))

    def gather_start(idx, slot):
        for r in range(MOE_ROWS):
            tok = idx[0, 0, r] & (n_tok - 1)
            pltpu.make_async_copy(tok_hbm.at[tok], xbuf.at[slot, r], gsem.at[slot]).start()

    def gather_wait(slot):
        pltpu.make_async_copy(tok_hbm.at[pl.ds(0, MOE_ROWS)], xbuf.at[slot], gsem.at[slot]).wait()

    def scatter_start(idx, slot):
        for r in range(MOE_ROWS):
            pltpu.make_async_copy(ybuf.at[slot, r], y_hbm.at[idx[0, 0, r]], ssem.at[slot]).start(priority=r % 2)

    def scatter_wait(slot):
        pltpu.make_async_copy(ybuf.at[slot], y_hbm.at[pl.ds(0, MOE_ROWS)], ssem.at[slot]).wait()

    @pl.when(i == 0)
    def _():
        xbuf[...] = jnp.zeros_like(xbuf)
        ybuf[...] = jnp.zeros_like(ybuf)
        spare0 = pltpu.make_async_copy(
            ybuf.at[0], y_hbm.at[pl.ds(MOE_TOP_K * n_tok, MOE_ROWS)], ssem.at[0])
        spare0.start()
        for cp in weight_copies(be_ref[0], 0):
            cp.start(priority=WEIGHT_DMA_PRIORITY)
        gather_start(idx_ref, 0)

    @pl.when(used)
    def _():
        ws = ws_ref[i]

        @pl.when(first_ref[i] == 1)
        def _():
            for cp in weight_copies(be_ref[i], ws):
                cp.wait()
            nxt = nxt_ref[i]

            @pl.when(nxt >= 0)
            def _():
                for cp in weight_copies(nxt, 1 - ws):
                    cp.start(priority=WEIGHT_DMA_PRIORITY)

            w1b[...] = wf1[ws].astype(BF16)
            w3b[...] = wf3[ws].astype(BF16)
            w2b[...] = wf2[ws].astype(BF16)

        gather_wait(xs)
        xb_ref[...] = _rows_from_tiles(xbuf[xs].astype(F32)).astype(BF16)
        gather_start(idxn_ref, 1 - xs)
        scatter_start(idxp_ref, 1 - xs)
        xb = xb_ref[...]
        h = (_silu(_dot(xb, w1b[...])) * _dot(xb, w3b[...])).astype(BF16)
        y = _tiles_from_rows(_dot(h, w2b[...])).astype(BF16)
        scatter_wait(xs)
        ybuf[xs] = y

    @pl.when(jnp.logical_not(used) & prev_used)
    def _():
        gather_wait(xs)
        scatter_start(idxp_ref, 1 - xs)
        scatter_wait(1 - xs)
        scatter_wait(xs)


def _moe_dispatch(eid_t, n_blocks):
    n_tok = eid_t.shape[1]
    n_slots = (n_blocks + 2) * MOE_ROWS
    slot_rows = -(-n_slots // SMEM_1D_TILE)
    assert n_blocks <= LANES and n_tok % SMEM_1D_TILE == 0 and SMEM_1D_TILE % MOE_ROWS == 0
    slot, meta = pl.pallas_call(
        functools.partial(_dispatch_kernel, n_tok=n_tok),
        in_specs=[pl.BlockSpec(memory_space=pltpu.VMEM)],
        out_specs=[pl.BlockSpec(memory_space=pltpu.SMEM), pl.BlockSpec(memory_space=pltpu.VMEM)],
        out_shape=[jax.ShapeDtypeStruct((slot_rows * SMEM_1D_TILE,), jnp.int32),
                   jax.ShapeDtypeStruct((MOE_ID_ROWS, LANES), jnp.int32)],
        scratch_shapes=[pltpu.VMEM((MOE_TOP_K * n_tok,), jnp.int32),
                        pltpu.VMEM((slot_rows * SMEM_1D_TILE,), jnp.int32),
                        pltpu.SMEM((MOE_TOP_K * n_tok,), jnp.int32),
                        pltpu.SemaphoreType.DMA(())],
        name="moe_dispatch",
    )(eid_t)
    return (slot[:n_slots].reshape(n_blocks + 2, 1, MOE_ROWS),) + tuple(meta[r, :n_blocks] for r in range(5))


def _dispatch_kernel(eid_ref, slot_ref, meta_ref, dest_vmem, init_vmem, dest_smem, sem, *, n_tok):
    tile = MOE_ROWS
    n_tiles = n_tok // tile
    sub = lax.broadcasted_iota(jnp.int32, (N_EXPERTS, tile), 0)
    si = lax.broadcasted_iota(jnp.int32, (tile, tile), 0)
    ti = lax.broadcasted_iota(jnp.int32, (tile, tile), 1)
    before = jnp.where(si < ti, 1.0, 0.0).astype(BF16)

    def one_hots(j):
        ids = eid_ref[:, j * tile:(j + 1) * tile]
        return [jnp.where(sub == ids[k:k + 1], 1.0, 0.0) for k in range(MOE_TOP_K)]

    carry = jnp.zeros((N_EXPERTS, 1), F32)
    ranks = []
    for j in range(n_tiles):
        oh = one_hots(j)
        both = oh[0] + oh[1]
        seen = carry + _dot(both.astype(BF16), before)
        ranks.append([jnp.sum(seen * o, axis=0, keepdims=True) for o in oh])
        carry = carry + jnp.sum(both, axis=1, keepdims=True)

    counts = carry
    nblk = jnp.floor((counts + float(MOE_ROWS - 1)) * (1.0 / MOE_ROWS))
    ei = lax.broadcasted_iota(jnp.int32, (N_EXPERTS, N_EXPERTS), 0)
    ej = lax.broadcasted_iota(jnp.int32, (N_EXPERTS, N_EXPERTS), 1)
    lower = jnp.where(ej < ei, 1.0, 0.0).astype(BF16)
    first_blk = _dot(lower, jnp.broadcast_to(nblk, (N_EXPERTS, LANES)).astype(BF16))[:, 0:1]
    first_slot = first_blk * float(MOE_ROWS)

    per_row = SMEM_1D_TILE // tile
    for k in range(MOE_TOP_K):
        for q in range(n_tiles // per_row):
            parts = []
            for j in range(q * per_row, (q + 1) * per_row):
                parts.append(jnp.sum(first_slot * one_hots(j)[k], axis=0, keepdims=True) + ranks[j][k])
            dest = jnp.concatenate(parts, axis=1).astype(jnp.int32)
            dest_vmem[pl.ds(k * n_tok + q * SMEM_1D_TILE, SMEM_1D_TILE)] = dest.reshape(SMEM_1D_TILE)

    lane = lax.broadcasted_iota(jnp.int32, (1, SMEM_1D_TILE), 1)
    for q in range(init_vmem.shape[0] // SMEM_1D_TILE):
        pos = q * SMEM_1D_TILE + lane
        spare = MOE_TOP_K * n_tok + ((pos // MOE_ROWS + 1) % 2) * MOE_ROWS + pos % MOE_ROWS
        init_vmem[pl.ds(q * SMEM_1D_TILE, SMEM_1D_TILE)] = spare.reshape(SMEM_1D_TILE)
    copies = [pltpu.make_async_copy(dest_vmem, dest_smem, sem), pltpu.make_async_copy(init_vmem, slot_ref, sem)]
    for cp in copies:
        cp.start()
    for cp in copies:
        cp.wait()

    def place(t, carry_):
        for k in range(MOE_TOP_K):
            slot_ref[dest_smem[k * n_tok + t] + MOE_ROWS] = k * n_tok + t
        return carry_

    lax.fori_loop(0, n_tok, place, 0, unroll=8)

    b = lax.broadcasted_iota(jnp.int32, (N_EXPERTS, LANES), 1).astype(F32)
    e_col = lax.broadcasted_iota(jnp.int32, (N_EXPERTS, LANES), 0).astype(F32)
    b_row = b[0:1]
    be = jnp.minimum(jnp.sum(jnp.where(first_blk + nblk <= b, 1.0, 0.0), axis=0, keepdims=True), N_EXPERTS - 1.0)
    mine = e_col == be
    cnt_b = jnp.sum(jnp.where(mine, counts, 0.0), axis=0, keepdims=True)
    start_b = jnp.sum(jnp.where(mine, first_blk, 0.0), axis=0, keepdims=True)
    nv = jnp.clip(cnt_b - (b_row - start_b) * MOE_ROWS, 0.0, float(MOE_ROWS))
    nv = jnp.where(b_row < jnp.sum(nblk, axis=0, keepdims=True), nv, 0.0)
    first = jnp.where((nv > 0) & ((b_row == 0) | (be != pltpu.roll(be, 1, 1))), 1.0, 0.0)
    li = lax.broadcasted_iota(jnp.int32, (LANES, LANES), 0)
    lj = lax.broadcasted_iota(jnp.int32, (LANES, LANES), 1)
    upto = jnp.where(li <= lj, 1.0, 0.0).astype(BF16)
    run = _dot(jnp.broadcast_to(first, (MOE_ID_ROWS, LANES)).astype(BF16), upto)[0:1] - 1.0
    ws = run - 2.0 * jnp.floor(run * 0.5)
    later = jnp.min(jnp.where((e_col > be) & (counts > 0), e_col, float(LANES)), axis=0, keepdims=True)
    nxt = jnp.where(later >= float(N_EXPERTS), -1.0, later)
    rows = [be, nv, first, ws, nxt] + [jnp.zeros_like(be)] * (MOE_ID_ROWS - 5)
    meta_ref[...] = jnp.concatenate(rows, axis=0).astype(jnp.int32)


def _moe_experts(tok, eid, w1, w3, w2, layer):
    n_tok, n_sub, _ = tok.shape
    d = n_sub * LANES
    assert n_tok & (n_tok - 1) == 0
    ff = w1.shape[3]
    n_assign = n_tok * MOE_TOP_K
    n_blocks = -(-(n_assign + N_EXPERTS * (MOE_ROWS - 1)) // MOE_ROWS)
    slot, be, nv, first, ws, nxt = _moe_dispatch(eid, n_blocks)
    grid_spec = pltpu.PrefetchScalarGridSpec(
        num_scalar_prefetch=5,
        grid=(n_blocks + 1,),
        in_specs=[
            pl.BlockSpec((1, 1, MOE_ROWS), lambda i, *_: (i + 1, 0, 0), memory_space=pltpu.SMEM),
            pl.BlockSpec((1, 1, MOE_ROWS), lambda i, *_: (jnp.minimum(i + 2, n_blocks + 1), 0, 0),
                         memory_space=pltpu.SMEM),
            pl.BlockSpec((1, 1, MOE_ROWS), lambda i, *_: (i, 0, 0), memory_space=pltpu.SMEM),
            pl.BlockSpec(memory_space=pl.ANY),
            pl.BlockSpec(memory_space=pl.ANY),
            pl.BlockSpec(memory_space=pl.ANY),
            pl.BlockSpec(memory_space=pl.ANY),
        ],
        out_specs=pl.BlockSpec(memory_space=pl.ANY),
        scratch_shapes=[
            pltpu.VMEM((2, MOE_ROWS, n_sub, LANES), BF16), pltpu.VMEM((2, MOE_ROWS, n_sub, LANES), BF16),
            pltpu.VMEM((MOE_ROWS, d), BF16),
            pltpu.VMEM((2, d, ff), F32), pltpu.VMEM((2, d, ff), F32), pltpu.VMEM((2, ff, d), F32),
            pltpu.VMEM((d, ff), BF16), pltpu.VMEM((d, ff), BF16), pltpu.VMEM((ff, d), BF16),
            pltpu.SemaphoreType.DMA((2,)), pltpu.SemaphoreType.DMA((2,)), pltpu.SemaphoreType.DMA((2,)),
        ],
    )
    return pl.pallas_call(
        functools.partial(_moe_kernel, n_tok=n_tok, layer=layer, n_blocks=n_blocks),
        grid_spec=grid_spec,
        out_shape=jax.ShapeDtypeStruct((MOE_TOP_K * n_tok + 2 * MOE_ROWS, n_sub, LANES), BF16),
        compiler_params=pltpu.CompilerParams(dimension_semantics=("arbitrary",),
                                             vmem_limit_bytes=MOE_VMEM_LIMIT_BYTES),
        name="moe_experts",
    )(be, nv, first, ws, nxt, slot, slot, slot, tok, w1, w3, w2)


def _rope_tables(n_seq):
    half = HEAD_DIM // 2
    n_freq = half // 2
    t = jnp.arange(n_seq)
    row = (t // GRID_W).astype(F32)
    col = (t % GRID_W).astype(F32)
    inv_freq = ROPE_BASE ** (-jnp.arange(n_freq, dtype=F32) / n_freq)
    ang_r = row[:, None] * inv_freq[None, :]
    ang_c = col[:, None] * inv_freq[None, :]
    cos = jnp.concatenate([jnp.cos(ang_r)] * 2 + [jnp.cos(ang_c)] * 2, axis=-1)
    sin = jnp.concatenate([-jnp.sin(ang_r), jnp.sin(ang_r), -jnp.sin(ang_c), jnp.sin(ang_c)], axis=-1)
    return cos, sin


def _router_weights(w_g, b_g, w_e, b_e):
    d = w_g.shape[0]
    n = w_g.shape[1] + w_e.shape[1]
    wr = jnp.concatenate([w_g, w_e, jnp.zeros((d, LANES - n), F32)], axis=1)
    rb = jnp.concatenate([b_g, b_e, jnp.zeros((LANES - n,), F32)]).reshape(1, LANES)
    hi = wr.astype(BF16)
    lo = (wr - hi.astype(F32)).astype(BF16)
    return hi, lo, rb


def kernel(x, c, ctx, c_ctx, ada_w, ada_b, ln_g, ln_b, mix_w_in, att_sink, hg_lb, hg_norm_g, mix_w_out, pool_w_in, pool_w_grp, pool_scale, pool_w_out, rt_group_w, rt_group_b, rt_expert_w, rt_expert_b, moe_w1, moe_w3, moe_w2):
    b, n, d = x.shape
    n_ctx = ctx.shape[1]
    t = b * n
    xf = x.reshape(t, d)
    ctxf = ctx.reshape(b * n_ctx, d)

    cond = jnp.concatenate([c, c_ctx[None, :], jnp.zeros((8 - b - 1, d), F32)], axis=0)
    mod = _ada_mod(cond, ada_w, ada_b)

    def chunk(l, j, rows=slice(0, b)):
        return mod[l, rows, j * d:(j + 1) * d][:, None, :]

    w_in = mix_w_in[0].astype(BF16)
    cos, sin = _rope_tables(n)
    q_w, kv_w = ATT_HEADS * HEAD_DIM, ATT_KV_HEADS * HEAD_DIM
    n_att = q_w + 2 * kv_w
    hk = HG_HEADS * HG_KEY
    assert n_att == 2 * PROJ_TN and hk == PROJ_TN
    a_lat, f_lat = _mod_matmul(xf, chunk(0, 1), chunk(0, 0), w_in, cos, sin,
                               lambda j: jnp.where(j < 3, j, jnp.where(j < 5, j + 2, j - 2)), n_att + 3 * hk, 2 * hk,
                               n_q=q_w, n_rope=q_w + kv_w, n_seq=n, tm=1024, tn=PROJ_TN)
    ctx_rows = slice(b, b + 1)
    a_ctx, f_ctx = _mod_matmul(ctxf, chunk(0, 1, ctx_rows), chunk(0, 0, ctx_rows), w_in, cos, sin,
                               lambda j: jnp.where(j < 1, 1, jnp.where(j < 2, 5, j + 1)), 2 * kv_w + hk, 2 * hk,
                               n_q=0, n_rope=0, n_seq=n, tm=b * n_ctx, tn=PROJ_TN)
    att = _window_attention(a_lat, a_ctx, att_sink[0], b, n, n_ctx)
    o_f = _hgrn2_scan(a_lat, f_lat, a_ctx, f_ctx, hg_lb, b, n, n_ctx, rev=False)
    o_b = _hgrn2_scan(a_lat, f_lat, a_ctx, f_ctx, hg_lb, b, n, n_ctx, rev=True)
    wr_hi, wr_lo, rb = _router_weights(rt_group_w[0], rt_group_b[0], rt_expert_w[0], rt_expert_b[0])
    x1, tok, eid, gate = _even_out(
        att, o_f, o_b, a_lat, hg_norm_g[0][None, :], mix_w_out[0].astype(BF16), xf,
        chunk(0, 2), chunk(0, 4), chunk(0, 3), ln_g[0, 0][None, :], ln_b[0, 0][None, :], wr_hi, wr_lo, rb, n)
    y2 = _moe_experts(tok, eid, moe_w1, moe_w3, moe_w2, 0)

    x2, u = _combine_call(x1, y2, gate, chunk(0, 5), ln_g[0, 1][None, :], ln_b[0, 1][None, :], n,
                          proj=(chunk(1, 1), chunk(1, 0), pool_w_in[0].astype(BF16)))
    wr_hi, wr_lo, rb = _router_weights(rt_group_w[1], rt_group_b[1], rt_expert_w[1], rt_expert_b[1])
    x3, tok, eid, gate = _pool_out(
        u, pool_w_grp[0].astype(BF16), pool_scale[0][None, :], pool_w_out[0].astype(BF16), x2,
        chunk(1, 2), chunk(1, 4), chunk(1, 3), ln_g[1, 0][None, :], ln_b[1, 0][None, :], wr_hi, wr_lo, rb, n)
    y2 = _moe_experts(tok, eid, moe_w1, moe_w3, moe_w2, 1)
    out = _combine_call(x3, y2, gate, chunk(1, 5), ln_g[1, 1][None, :], ln_b[1, 1][None, :], n)
    return out.reshape(b, n, d)
```

```python
import functools

import jax
import jax.numpy as jnp
from jax import lax
from jax.experimental import pallas as pl
from jax.experimental.pallas import tpu as pltpu

F32 = jnp.float32
BF16 = jnp.bfloat16

LANES = 128
MXU_TILE = 256
VMEM_LIMIT_BYTES = 56 * 1024 * 1024
MOE_VMEM_LIMIT_BYTES = 60 * 1024 * 1024

GRID_W = 64
ATT_HEADS = 8
ATT_KV_HEADS = 4
ATT_GROUP = ATT_HEADS // ATT_KV_HEADS
HEAD_DIM = 128
WINDOW = 128
ATT_BLOCK = 128
ROPE_BASE = 10000.0
HG_HEADS = 8
HG_KEY = 128
HG_CHUNK = 64
HG_STEP_CHUNKS = 4
HG_SUB = 16
HG_FAST_RANGE = 80.0
NORM_EPS = 1e-6
POOL_WINDOWS = (2, 4, 8, 16)
POOL_HALO = 8
MOE_GROUPS = 4
MOE_EXPERTS_PER_GROUP = 8
N_EXPERTS = MOE_GROUPS * MOE_EXPERTS_PER_GROUP
MOE_TOP_K = 2
MOE_ROWS = 256
ROW_TILE = 512
SUB_ROWS = 256
PROJ_TN = 1024
MOE_ID_ROWS = 8
SMEM_1D_TILE = 1024
WEIGHT_DMA_PRIORITY = 1
LN_EPS = 1e-5
DEPTH = 2
DEEPNORM_ALPHA = (2 * DEPTH) ** 0.25


def _dot(a, b):
    return jnp.dot(a, b, preferred_element_type=F32)


def _dot_nt(a, b):
    return lax.dot_general(a, b, (((1,), (1,)), ((), ())), preferred_element_type=F32)


def _dot_tn(a, b):
    return lax.dot_general(a, b, (((0,), (0,)), ((), ())), preferred_element_type=F32)


def _sigmoid(x):
    return 1.0 / (1.0 + jnp.exp(-x))


def _silu(x):
    return x * _sigmoid(x)


def _params(*sem):
    return pltpu.CompilerParams(dimension_semantics=sem, vmem_limit_bytes=VMEM_LIMIT_BYTES)


def _tiles_from_rows(x):
    n = x.shape[1] // LANES
    return jnp.swapaxes(jnp.stack([x[:, s * LANES:(s + 1) * LANES] for s in range(n)], axis=0), 0, 1)


def _rows_from_tiles(x3):
    xt = jnp.swapaxes(x3, 0, 1)
    return jnp.concatenate([xt[s] for s in range(xt.shape[0])], axis=-1)


def _layer_norm(z, g, b):
    mu = jnp.mean(z, axis=-1, keepdims=True)
    zc = z - mu
    var = jnp.mean(zc * zc, axis=-1, keepdims=True)
    return zc * lax.rsqrt(var + LN_EPS) * g + b


def _ada_kernel(s_ref, w_ref, b_ref, o_ref):
    s = _silu(s_ref[...]).astype(BF16)
    o_ref[0] = _dot(s, w_ref[0].astype(BF16)) + b_ref[0]


def _ada_mod(s, ada_w, ada_b, tn=1024):
    n_l, d, n = ada_w.shape
    return pl.pallas_call(
        _ada_kernel,
        grid=(n_l, n // tn),
        in_specs=[
            pl.BlockSpec((8, d), lambda l, j: (0, 0)),
            pl.BlockSpec((1, d, tn), lambda l, j: (l, 0, j)),
            pl.BlockSpec((1, 1, tn), lambda l, j: (l, 0, j)),
        ],
        out_specs=pl.BlockSpec((1, 8, tn), lambda l, j: (l, 0, j)),
        out_shape=jax.ShapeDtypeStruct((n_l, 8, n), F32),
        compiler_params=_params("parallel", "parallel"),
        name="ada_mod",
    )(s, ada_w, ada_b.reshape(n_l, 1, n))


def _rope(t, cos, sin_signed, first_half):
    partner = jnp.where(first_half, pltpu.roll(t, 96, 1), pltpu.roll(t, 32, 1))
    return t * cos + partner * sin_signed


def _modmm_kernel(x_ref, sc_ref, sh_ref, w_ref, cos_ref, sin_ref, oa_ref, ob_ref, xs_ref, *,
                  n_q, n_rope, n_a_tiles):
    j = pl.program_id(1)

    @pl.when(j == 0)
    def _():
        xs_ref[...] = (x_ref[...] * (1.0 + sc_ref[0]) + sh_ref[0]).astype(BF16)

    acc = _dot(xs_ref[...], w_ref[...])
    tn = acc.shape[1]

    for jt in range(n_a_tiles):
        @pl.when(j == jt)
        def _(jt=jt):
            lane = lax.broadcasted_iota(jnp.int32, (1, HEAD_DIM), 1)
            first_half = (lane % 64) < 32
            for h in range(tn // HEAD_DIM):
                sl = slice(h * HEAD_DIM, (h + 1) * HEAD_DIM)
                col = jt * tn + h * HEAD_DIM
                if col < n_rope:
                    scale = HEAD_DIM ** -0.5 if col < n_q else 1.0
                    oa_ref[:, sl] = _rope(acc[:, sl], cos_ref[...] * scale, sin_ref[...] * scale,
                                          first_half).astype(BF16)
                else:
                    oa_ref[:, sl] = acc[:, sl].astype(BF16)

    @pl.when(j >= n_a_tiles)
    def _():
        ob_ref[...] = acc


def _mod_matmul(x, sc, sh, w, cos, sin, col_map, n_a, n_b, n_q, n_rope, n_seq, tm, tn):
    m, k = x.shape
    rows_per_mod = m // sc.shape[0]
    ta, tb = n_a // tn, n_b // tn
    tab = pl.BlockSpec((tm, HEAD_DIM), lambda i, j: ((i * tm % n_seq) // tm, 0))
    return pl.pallas_call(
        functools.partial(_modmm_kernel, n_q=n_q, n_rope=n_rope, n_a_tiles=ta),
        grid=(m // tm, ta + tb),
        in_specs=[
            pl.BlockSpec((tm, k), lambda i, j: (i, 0)),
            pl.BlockSpec((1, 1, k), lambda i, j: (i * tm // rows_per_mod, 0, 0)),
            pl.BlockSpec((1, 1, k), lambda i, j: (i * tm // rows_per_mod, 0, 0)),
            pl.BlockSpec((k, tn), lambda i, j: (0, col_map(j))),
            tab, tab,
        ],
        out_specs=[pl.BlockSpec((tm, tn), lambda i, j: (i, jnp.minimum(j, ta - 1))),
                   pl.BlockSpec((tm, tn), lambda i, j: (i, jnp.maximum(j - ta, 0)))],
        out_shape=[jax.ShapeDtypeStruct((m, n_a), BF16), jax.ShapeDtypeStruct((m, n_b), F32)],
        scratch_shapes=[pltpu.VMEM((tm, k), BF16)],
        compiler_params=_params("parallel", "arbitrary"),
        name="mod_matmul",
    )(x, sc, sh, w, cos, sin)


def _attn_kernel(sink_ref, q_ref, kp_ref, kc_ref, kn_ref, vp_ref, vc_ref, vn_ref, kx_ref, vx_ref,
                 mp_ref, mn_ref, o_ref, *, n_blocks):
    n = pl.program_id(1)
    blk = ATT_BLOCK
    row1 = lax.broadcasted_iota(jnp.int32, (ATT_GROUP * blk, 1), 0)
    has_prev, has_next = n > 0, n < n_blocks - 1
    heads = range(ATT_KV_HEADS)

    def kv(h):
        return slice(h * HEAD_DIM, (h + 1) * HEAD_DIM)

    scores = []
    for h in heads:
        q2 = jnp.concatenate([q_ref[:, (ATT_GROUP * h + g) * HEAD_DIM:(ATT_GROUP * h + g + 1) * HEAD_DIM]
                              for g in range(ATT_GROUP)], axis=0)
        k_all = jnp.concatenate([kp_ref[:, kv(h)], kc_ref[:, kv(h)], kn_ref[:, kv(h)], kx_ref[:, kv(h)]], axis=0)
        scores.append(_dot_nt(q2, k_all))
    probs, dens = [], []
    for h in heads:
        s = scores[h]
        parts = [jnp.where(has_prev, s[:, :blk] + mp_ref[...], -jnp.inf), s[:, blk:2 * blk],
                 jnp.where(has_next, s[:, 2 * blk:3 * blk] + mn_ref[...], -jnp.inf)]
        parts += [s[:, c:c + blk] for c in range(3 * blk, s.shape[1], blk)]
        sink = jnp.where(row1 < blk, sink_ref[ATT_GROUP * h], sink_ref[ATT_GROUP * h + 1])
        m = jnp.maximum(jnp.max(functools.reduce(jnp.maximum, parts), axis=-1, keepdims=True), sink)
        p = [jnp.exp(x - m) for x in parts]
        dens.append(jnp.sum(functools.reduce(jnp.add, p), axis=-1, keepdims=True) + jnp.exp(sink - m))
        probs.append(jnp.concatenate(p, axis=-1).astype(BF16))
    for h in heads:
        v_all = jnp.concatenate([vp_ref[:, kv(h)], vc_ref[:, kv(h)], vn_ref[:, kv(h)], vx_ref[:, kv(h)]], axis=0)
        o = _dot(probs[h], v_all) / dens[h]
        for g in range(ATT_GROUP):
            col = (ATT_GROUP * h + g) * HEAD_DIM
            o_ref[:, col:col + HEAD_DIM] = o[g * blk:(g + 1) * blk].astype(o_ref.dtype)


def _window_attention(qkv, kv_ctx, sink, batch, n_seq, n_ctx):
    assert ATT_GROUP == 2 and WINDOW == ATT_BLOCK
    nb = n_seq // ATT_BLOCK
    qw, kw = ATT_HEADS * HEAD_DIM, ATT_KV_HEADS * HEAD_DIM
    kcol, vcol = qw // kw, qw // kw + 1

    def rows(off):
        return lambda b, n, off=off: b * nb + jnp.clip(n + off, 0, nb - 1)

    kspec = [pl.BlockSpec((ATT_BLOCK, kw), lambda b, n, r=rows(o): (r(b, n), kcol)) for o in (-1, 0, 1)]
    vspec = [pl.BlockSpec((ATT_BLOCK, kw), lambda b, n, r=rows(o): (r(b, n), vcol)) for o in (-1, 0, 1)]
    r = jnp.arange(ATT_GROUP * ATT_BLOCK)[:, None] % ATT_BLOCK
    c = jnp.arange(ATT_BLOCK)[None, :]
    mask_prev = jnp.where(c >= r, 0.0, -jnp.inf).astype(F32)
    mask_next = jnp.where(c <= r, 0.0, -jnp.inf).astype(F32)
    mspec = pl.BlockSpec(mask_prev.shape, lambda b, n: (0, 0))
    return pl.pallas_call(
        functools.partial(_attn_kernel, n_blocks=nb),
        grid=(batch, nb),
        in_specs=[pl.BlockSpec(memory_space=pltpu.SMEM),
                  pl.BlockSpec((ATT_BLOCK, qw), lambda b, n: (b * nb + n, 0))]
        + kspec + vspec
        + [pl.BlockSpec((n_ctx, kw), lambda b, n: (b, 0)), pl.BlockSpec((n_ctx, kw), lambda b, n: (b, 1)),
           mspec, mspec],
        out_specs=pl.BlockSpec((ATT_BLOCK, qw), lambda b, n: (b * nb + n, 0)),
        out_shape=jax.ShapeDtypeStruct((batch * n_seq, qw), BF16),
        compiler_params=_params("parallel", "parallel"),
        name="window_attention",
    )(sink, qkv, qkv, qkv, qkv, qkv, qkv, qkv, kv_ctx, kv_ctx, mask_prev, mask_next)


def _gla_step(zf, q_raw, v, lb, st_ref, o_ref, rev):
    c_len = HG_CHUNK
    n_rows = zf.shape[0]
    n_sub = n_rows // c_len
    shift = c_len.bit_length() - 1
    order = range(n_sub - 1, -1, -1) if rev else range(n_sub)

    def head(h):
        return slice(h * HG_KEY, (h + 1) * HG_KEY)

    def chunk(i):
        return slice(i * c_len, (i + 1) * c_len)

    def seen(n):
        ri = lax.broadcasted_iota(jnp.int32, (n, n), 0)
        ci = lax.broadcasted_iota(jnp.int32, (n, n), 1)
        return ((ri >> shift) == (ci >> shift)) & ((ci >= ri) if rev else (ci <= ri))

    f = lb + (1.0 - lb) * _sigmoid(zf)
    k = 1.0 - f
    g = jnp.log(f)
    tri = jnp.where(seen(n_rows), 1.0, 0.0).astype(BF16)
    g1 = g.astype(BF16)
    r1 = g - g1.astype(F32)
    g2 = r1.astype(BF16)
    g3 = (r1 - g2.astype(F32)).astype(BF16)
    c = _dot(tri, g1) + _dot(tri, g2) + _dot(tri, g3)
    c_end = [c[i * c_len:i * c_len + 1] if rev else c[(i + 1) * c_len - 1:(i + 1) * c_len] for i in range(n_sub)]
    c_end_rows = jnp.concatenate([jnp.broadcast_to(ce, (c_len, ce.shape[1])) for ce in c_end], axis=0)
    k_end = (k * jnp.exp(c_end_rows - c)).astype(BF16)
    dec = [jnp.exp(ce) for ce in c_end]
    vb = v.astype(BF16)

    def advance(states, i):
        new = []
        for p in range(0, HG_HEADS, 2):
            lanes = slice(p * HG_KEY, (p + 2) * HG_KEY)
            inc = _dot_tn(vb[chunk(i), lanes], k_end[chunk(i), lanes])
            for j in range(2):
                blk = slice(j * HG_KEY, (j + 1) * HG_KEY)
                new.append(states[p + j] * dec[i][:, head(p + j)] + inc[blk, blk])
        return new

    if o_ref is None:
        states = [st_ref[h] for h in range(HG_HEADS)]
        for i in order:
            states = advance(states, i)
        for h in range(HG_HEADS):
            st_ref[h] = states[h]
        return

    q = _silu(q_raw.astype(F32))
    q_in = (q * jnp.exp(c)).astype(BF16)
    lowest = functools.reduce(jnp.minimum, c_end)
    in_range = jnp.min(lowest) >= -HG_FAST_RANGE

    def stack(x, i, h0, n):
        return jnp.concatenate([x[chunk(i), head(h)] for h in range(h0, h0 + n)], axis=0)

    @pl.when(in_range)
    def _():
        k_in = (k * jnp.exp(-c)).astype(BF16)
        n_qk = MXU_TILE // c_len
        same = seen(n_qk * c_len)
        groups = [(i, h0) for i in order for h0 in range(0, HG_HEADS, n_qk)]
        raw = [_dot_nt(stack(q_in, i, h0, n_qk), stack(k_in, i, h0, n_qk)) for i, h0 in groups]
        masked = [jnp.where(same, s, 0.0).astype(BF16) for s in raw]
        intra = {}
        for (i, h0), sc in zip(groups, masked):
            pv = _dot(sc, stack(vb, i, h0, n_qk))
            for j in range(n_qk):
                intra[i, h0 + j] = pv[j * c_len:(j + 1) * c_len]
        states = [st_ref[h] for h in range(HG_HEADS)]
        for i in order:
            out = []
            for p in range(0, HG_HEADS, 2):
                st_pair = jnp.concatenate([states[p].astype(BF16), states[p + 1].astype(BF16)], axis=0)
                inter = _dot_nt(stack(q_in, i, p, 2), st_pair)
                for j in range(2):
                    out.append(inter[j * c_len:(j + 1) * c_len, j * HG_KEY:(j + 1) * HG_KEY] + intra[i, p + j])
            o_ref[chunk(i), :] = jnp.concatenate(out, axis=1)
            states = advance(states, i)
        for h in range(HG_HEADS):
            st_ref[h] = states[h]

    @pl.when(jnp.logical_not(in_range))
    def _():
        for i in order:
            r = chunk(i)
            _gla_intra_exact(q[r], k[r], v[r].astype(F32), vb[r], c[r], q_in[r], st_ref, o_ref, r, rev)
            states = advance([st_ref[h] for h in range(HG_HEADS)], i)
            for h in range(HG_HEADS):
                st_ref[h] = states[h]


def _gla_intra_exact(q, k, v, vb, c, q_in, st_ref, o_ref, rows, rev):
    c_len = q.shape[0]
    pairs = []
    size = c_len // 2
    while size >= HG_SUB:
        for lo in range(0, c_len, 2 * size):
            pairs.append((lo, lo + size, lo + 2 * size))
        size //= 2
    scaled = []
    for lo, mid, hi in pairs:
        if rev:
            late, early, bnd = slice(lo, mid), slice(mid, hi), mid
        else:
            late, early, bnd = slice(mid, hi), slice(lo, mid), mid - 1
        cb = c[bnd:bnd + 1]
        q_l = (q[late] * jnp.exp(c[late] - cb)).astype(BF16)
        k_e = (k[early] * jnp.exp(cb - c[early])).astype(BF16)
        scaled.append((late, early, q_l, k_e))
    n_sub = c_len // HG_SUB
    t_idx = lax.broadcasted_iota(jnp.int32, (HG_SUB, 1), 0)
    diag = [[None] * HG_HEADS for _ in range(n_sub)]
    for b in range(n_sub):
        r0 = b * HG_SUB
        qb, cb = q[r0:r0 + HG_SUB], c[r0:r0 + HG_SUB]
        for s in range(HG_SUB):
            row = r0 + s
            ok = (t_idx <= s) if rev else (t_idx >= s)
            w = qb * k[row:row + 1] * jnp.exp(jnp.where(ok, cb - c[row:row + 1], -jnp.inf))
            for h in range(HG_HEADS):
                sl = slice(h * HG_KEY, (h + 1) * HG_KEY)
                contrib = jnp.sum(w[:, sl], axis=-1, keepdims=True) * v[row:row + 1, sl]
                diag[b][h] = contrib if diag[b][h] is None else diag[b][h] + contrib

    for h in range(HG_HEADS):
        sl = slice(h * HG_KEY, (h + 1) * HG_KEY)
        o_h = _dot_nt(q_in[:, sl], st_ref[h].astype(BF16))
        parts = [diag[b][h] for b in range(n_sub)]
        for late, early, q_l, k_e in scaled:
            sc = _dot_nt(q_l[:, sl], k_e[:, sl]).astype(BF16)
            add = _dot(sc, vb[early, sl])
            b0 = late.start // HG_SUB
            for j in range((late.stop - late.start) // HG_SUB):
                parts[b0 + j] = parts[b0 + j] + add[j * HG_SUB:(j + 1) * HG_SUB]
        o_ref[rows, sl] = o_h + jnp.concatenate(parts, axis=0)


def _gla_kernel(lb_ref, zf_ref, q_ref, v_ref, zfc_ref, vc_ref, o_ref, st_ref, *, rev, n_ctx_steps):
    s = pl.program_id(1)

    @pl.when(s == 0)
    def _():
        st_ref[...] = jnp.zeros_like(st_ref)

    x = lb_ref[...]
    e = jnp.exp(x - jnp.max(x, axis=0, keepdims=True))
    lb = e[0:1] / jnp.sum(e, axis=0, keepdims=True)

    @pl.when(s < n_ctx_steps)
    def _():
        _gla_step(zfc_ref[...], None, vc_ref[...], lb, st_ref, None, rev)

    @pl.when(s >= n_ctx_steps)
    def _():
        _gla_step(zf_ref[...], q_ref[...], v_ref[...], lb, st_ref, o_ref, rev)


def _hgrn2_scan(a_lat, f_lat, a_ctx, f_ctx, hg_lb, batch, n_seq, n_ctx, rev):
    hk = HG_HEADS * HG_KEY
    rows = HG_CHUNK * HG_STEP_CHUNKS
    assert n_seq % rows == 0 and n_ctx % rows == 0
    nc, ncc = n_seq // rows, n_ctx // rows
    d = 1 if rev else 0

    def lat(b, s):
        j = jnp.maximum(s - ncc, 0)
        return b * nc + (nc - 1 - j if rev else j)

    def ctx(b, s):
        j = jnp.minimum(s, ncc - 1)
        return b * ncc + (ncc - 1 - j if rev else j)

    return pl.pallas_call(
        functools.partial(_gla_kernel, rev=rev, n_ctx_steps=ncc),
        grid=(batch, ncc + nc),
        in_specs=[
            pl.BlockSpec((None, hg_lb.shape[1], hk), lambda b, s: (d, 0, 0)),
            pl.BlockSpec((rows, hk), lambda b, s: (lat(b, s), d)),
            pl.BlockSpec((rows, hk), lambda b, s: (lat(b, s), 2)),
            pl.BlockSpec((rows, hk), lambda b, s: (lat(b, s), 3)),
            pl.BlockSpec((rows, hk), lambda b, s: (ctx(b, s), d)),
            pl.BlockSpec((rows, hk), lambda b, s: (ctx(b, s), 1)),
        ],
        out_specs=pl.BlockSpec((rows, hk), lambda b, s: (lat(b, s), 0)),
        out_shape=jax.ShapeDtypeStruct((batch * n_seq, hk), F32),
        scratch_shapes=[pltpu.VMEM((HG_HEADS, HG_KEY, HG_KEY), F32)],
        compiler_params=_params("parallel", "arbitrary"),
        name="hgrn2_bwd" if rev else "hgrn2_fwd",
    )(hg_lb, f_lat, a_lat, a_lat, f_ctx, a_ctx)


def _route(tok, wr_hi, wr_lo, rb):
    t_hi = tok.astype(BF16)
    t_lo = (tok - t_hi.astype(F32)).astype(BF16)
    lg = _dot(t_hi, wr_hi) + _dot(t_hi, wr_lo) + _dot(t_lo, wr_hi) + rb
    lane = lax.broadcasted_iota(jnp.int32, lg.shape, 1)
    lane_f = lane.astype(F32)
    ninf = -jnp.inf
    gl = jnp.where(lane < MOE_GROUPS, lg, ninf)
    gmax = jnp.max(gl, axis=-1, keepdims=True)
    g_idx = jnp.min(jnp.where(gl == gmax, lane_f, float(LANES)), axis=-1, keepdims=True)
    g_val = 1.0 / jnp.sum(jnp.exp(gl - gmax), axis=-1, keepdims=True)
    e_lane = lane_f - float(MOE_GROUPS)
    lo = g_idx * float(MOE_EXPERTS_PER_GROUP)
    in_grp = (e_lane >= lo) & (e_lane < lo + float(MOE_EXPERTS_PER_GROUP))
    el = jnp.where(in_grp, lg, ninf)
    l1 = jnp.max(el, axis=-1, keepdims=True)
    i1 = jnp.min(jnp.where(el == l1, e_lane, float(LANES)), axis=-1, keepdims=True)
    el2 = jnp.where(e_lane == i1, ninf, el)
    l2 = jnp.max(el2, axis=-1, keepdims=True)
    i2 = jnp.min(jnp.where(el2 == l2, e_lane, float(LANES)), axis=-1, keepdims=True)
    r = jnp.exp(l2 - l1)
    w1 = g_val / (1.0 + r)
    w2 = w1 * r
    eid = jnp.where(lane == 0, i1, jnp.where(lane == 1, i2, 0.0))
    gate = jnp.where(lane == 0, w1, jnp.where(lane == 1, w2, 0.0))
    eid_t = jnp.transpose(eid)[:MOE_ID_ROWS].astype(jnp.int32)
    return eid_t, gate


def _post_mix(y, x_ref, g1_ref, sc2_ref, sh2_ref, lng_ref, lnb_ref, wrh_ref, wrl_ref, rb_ref,
              x1_ref, tok_ref, eid_ref, gate_ref, r):
    x1 = _layer_norm(DEEPNORM_ALPHA * x_ref[r] + g1_ref[0] * y, lng_ref[...], lnb_ref[...])
    x1_ref[r] = x1
    tok = x1 * (1.0 + sc2_ref[0]) + sh2_ref[0]
    tok_ref[r] = _tiles_from_rows(tok).astype(BF16)
    eid, gate = _route(tok, wrh_ref[...], wrl_ref[...], rb_ref[...])
    eid_ref[:, r] = eid
    gate_ref[r] = gate


def _sub_tiles(n_rows):
    return [slice(s, s + SUB_ROWS) for s in range(0, n_rows, SUB_ROWS)]


def _even_out_kernel(att_ref, of_ref, ob_ref, gt_ref, ng_ref, wo_ref, *rest):
    tiles = _sub_tiles(att_ref.shape[0])
    lhs = []
    for r in tiles:
        o = of_ref[r] + ob_ref[r]
        pieces = []
        for h in range(HG_HEADS):
            oh = o[:, h * HG_KEY:(h + 1) * HG_KEY]
            pieces.append(oh * lax.rsqrt(jnp.mean(oh * oh, axis=-1, keepdims=True) + NORM_EPS))
        hg = (jnp.concatenate(pieces, axis=-1) * ng_ref[...] * _silu(gt_ref[r].astype(F32))).astype(BF16)
        lhs.append(jnp.concatenate([att_ref[r], hg], axis=-1))
    ys = [_dot(a, wo_ref[...]) for a in lhs]
    for r, y in zip(tiles, ys):
        _post_mix(y, *rest, r)


def _post_specs(d, tm, rows_per_batch):
    def bmap(i):
        return (i * tm // rows_per_batch, 0, 0)

    row = pl.BlockSpec((tm, d), lambda i: (i, 0))
    mod = pl.BlockSpec((1, 1, d), bmap)
    vec = pl.BlockSpec((1, d), lambda i: (0, 0))
    rw = pl.BlockSpec((d, LANES), lambda i: (0, 0))
    in_specs = [row, mod, mod, mod, vec, vec, rw, rw, pl.BlockSpec((1, LANES), lambda i: (0, 0))]
    lane_blk = pl.BlockSpec((tm, LANES), lambda i: (i, 0))
    tiles = pl.BlockSpec((tm, d // LANES, LANES), lambda i: (i, 0, 0))
    out_specs = [row, tiles, pl.BlockSpec((MOE_ID_ROWS, tm), lambda i: (0, i)), lane_blk]
    return in_specs, out_specs


def _post_out_shapes(t, d):
    return [jax.ShapeDtypeStruct((t, d), F32), jax.ShapeDtypeStruct((t, d // LANES, LANES), BF16),
            jax.ShapeDtypeStruct((MOE_ID_ROWS, t), jnp.int32), jax.ShapeDtypeStruct((t, LANES), F32)]


def _resident(shape):
    return pl.BlockSpec(shape, lambda i: (0,) * len(shape), pipeline_mode=pl.Buffered(1))


def _even_out(att, o_f, o_b, p, norm_g, w_out, x, g1, sc2, sh2, lng, lnb, wr_hi, wr_lo, rb, rows_per_batch,
              tm=ROW_TILE):
    t, d = x.shape
    hv = o_f.shape[1]
    post_in, post_out = _post_specs(d, tm, rows_per_batch)
    return pl.pallas_call(
        _even_out_kernel,
        grid=(t // tm,),
        in_specs=[
            pl.BlockSpec((tm, att.shape[1]), lambda i: (i, 0)),
            pl.BlockSpec((tm, hv), lambda i: (i, 0)),
            pl.BlockSpec((tm, hv), lambda i: (i, 0)),
            pl.BlockSpec((tm, hv), lambda i: (i, 4)),
            pl.BlockSpec((1, hv), lambda i: (0, 0)),
            _resident(w_out.shape),
        ] + post_in,
        out_specs=post_out,
        out_shape=_post_out_shapes(t, d),
        compiler_params=_params("parallel"),
        name="even_out",
    )(att, o_f, o_b, p, norm_g, w_out, x, g1, sc2, sh2, lng, lnb, wr_hi, wr_lo, rb)


def _combine(x_ref, ya_ref, yb_ref, gate_ref, g2_ref, lng_ref, lnb_ref, r):
    gate = gate_ref[r]
    y = (gate[:, 0:1] * _rows_from_tiles(ya_ref[r].astype(F32))
         + gate[:, 1:2] * _rows_from_tiles(yb_ref[r].astype(F32)))
    return _layer_norm(DEEPNORM_ALPHA * x_ref[r] + g2_ref[0] * y, lng_ref[...], lnb_ref[...])


def _combine_proj_kernel(x_ref, ya_ref, yb_ref, gate_ref, g2_ref, lng_ref, lnb_ref, sc_ref, sh_ref, w_ref,
                         x2_ref, u_ref):
    tiles = _sub_tiles(x_ref.shape[0])
    lhs = []
    for r in tiles:
        x2 = _combine(x_ref, ya_ref, yb_ref, gate_ref, g2_ref, lng_ref, lnb_ref, r)
        x2_ref[r] = x2
        lhs.append((x2 * (1.0 + sc_ref[0]) + sh_ref[0]).astype(BF16))
    for r, a in zip(tiles, lhs):
        u_ref[r] = _dot(a, w_ref[...])


def _combine_kernel(x_ref, ya_ref, yb_ref, gate_ref, g2_ref, lng_ref, lnb_ref, x2_ref):
    for r in _sub_tiles(x_ref.shape[0]):
        x2_ref[r] = _combine(x_ref, ya_ref, yb_ref, gate_ref, g2_ref, lng_ref, lnb_ref, r)


def _combine_call(x1, y2, gate, g2, lng, lnb, rows_per_batch, proj=None, tm=ROW_TILE):
    t, d = x1.shape
    nt = t // tm

    def bmap(i):
        return (i * tm // rows_per_batch, 0, 0)

    row = pl.BlockSpec((tm, d), lambda i: (i, 0))
    mod = pl.BlockSpec((1, 1, d), bmap)
    vec = pl.BlockSpec((1, d), lambda i: (0, 0))
    in_specs = [row, pl.BlockSpec((tm, d // LANES, LANES), lambda i: (i, 0, 0)),
                pl.BlockSpec((tm, d // LANES, LANES), lambda i: (nt + i, 0, 0)),
                pl.BlockSpec((tm, LANES), lambda i: (i, 0)), mod, vec, vec]
    args = [x1, y2, y2, gate, g2, lng, lnb]
    if proj is None:
        return pl.pallas_call(
            _combine_kernel, grid=(nt,), in_specs=in_specs, out_specs=row,
            out_shape=jax.ShapeDtypeStruct((t, d), F32),
            compiler_params=_params("parallel"), name="combine_ln")(*args)
    sc, sh, w = proj
    return pl.pallas_call(
        _combine_proj_kernel, grid=(nt,),
        in_specs=in_specs + [mod, mod, _resident(w.shape)],
        out_specs=[row, pl.BlockSpec((tm, w.shape[1]), lambda i: (i, 0))],
        out_shape=[jax.ShapeDtypeStruct((t, d), F32), jax.ShapeDtypeStruct((t, w.shape[1]), F32)],
        compiler_params=_params("parallel"), name="combine_ln_proj")(*args, sc, sh, w)


def _pool_out_kernel(up_ref, uc_ref, un_ref, wg_ref, ps_ref, wo_ref, *rest, n_seq):
    tm, d = uc_ref.shape
    n_grp = len(POOL_WINDOWS)
    ch = d // n_grp
    halo = POOL_HALO
    tiles = _sub_tiles(tm)
    lhs = []
    for r in tiles:
        n_r = r.stop - r.start
        pos0 = (pl.program_id(0) * tm + r.start) % n_seq
        e_pos = pos0 - halo + lax.broadcasted_iota(jnp.int32, (n_r + 2 * halo, 1), 0)
        e_ok = (e_pos >= 0) & (e_pos < n_seq)
        t_pos = pos0 + lax.broadcasted_iota(jnp.int32, (n_r, 1), 0)
        z = []
        for gi, w in enumerate(POOL_WINDOWS):
            cs = slice(gi * ch, (gi + 1) * ch)
            u = uc_ref[r, cs]
            before = up_ref[:, cs] if r.start == 0 else uc_ref[r.start - halo:r.start, cs]
            after = un_ref[:, cs] if r.stop == tm else uc_ref[r.stop:r.stop + halo, cs]
            ext = jnp.where(e_ok, jnp.concatenate([before, u, after], axis=0), 0.0)
            a, span = ext, 1
            while span < w:
                a = a[:a.shape[0] - span] + a[span:]
                span *= 2
            start = halo - w // 2
            win = a[start:start + n_r]
            cnt = (jnp.minimum(t_pos + (w - w // 2), n_seq) - jnp.maximum(t_pos - w // 2, 0)).astype(F32)
            mixed = (win / cnt - u).astype(BF16)
            z.append((_dot(mixed, wg_ref[gi]) * ps_ref[:, cs]).astype(BF16))
        lhs.append(jnp.concatenate(z, axis=-1))
    ys = [_dot(a, wo_ref[...]) for a in lhs]
    for r, y in zip(tiles, ys):
        _post_mix(y, *rest, r)


def _pool_out(u, w_grp, scale, w_out, x, g1, sc2, sh2, lng, lnb, wr_hi, wr_lo, rb, n_seq, tm=ROW_TILE):
    t, d = x.shape
    hb = tm // POOL_HALO
    n_hb = t // POOL_HALO
    post_in, post_out = _post_specs(d, tm, n_seq)
    return pl.pallas_call(
        functools.partial(_pool_out_kernel, n_seq=n_seq),
        grid=(t // tm,),
        in_specs=[
            pl.BlockSpec((POOL_HALO, d), lambda i: (jnp.maximum(i * hb - 1, 0), 0)),
            pl.BlockSpec((tm, d), lambda i: (i, 0)),
            pl.BlockSpec((POOL_HALO, d), lambda i: (jnp.minimum((i + 1) * hb, n_hb - 1), 0)),
            _resident(w_grp.shape),
            pl.BlockSpec((1, d), lambda i: (0, 0)),
            _resident(w_out.shape),
        ] + post_in,
        out_specs=post_out,
        out_shape=_post_out_shapes(t, d),
        compiler_params=_params("parallel"),
        name="pool_out",
    )(u, u, u, w_grp, scale, w_out, x, g1, sc2, sh2, lng, lnb, wr_hi, wr_lo, rb)


def _moe_kernel(be_ref, nv_ref, first_ref, ws_ref, nxt_ref, idx_ref, idxn_ref, idxp_ref, tok_hbm, w1_hbm, w3_hbm, w2_hbm,
                y_hbm, xbuf, ybuf, xb_ref, wf1, wf3, wf2, w1b, w3b, w2b, gsem, ssem, wsem, *, n_tok, layer, n_blocks):
    i = pl.program_id(0)
    used = nv_ref[jnp.minimum(i, n_blocks - 1)] > 0
    used = used & (i < n_blocks)
    prev_used = (i > 0) & (nv_ref[jnp.maximum(i - 1, 0)] > 0)
    xs = i % 2

    def weight_copies(e, ws):
        return (pltpu.make_async_copy(w1_hbm.at[layer, e], wf1.at[ws], wsem.at[ws]),
                pltpu.make_async_copy(w3_hbm.at[layer, e], wf3.at[ws], wsem.at[ws]),
                pltpu.make_async_copy(w2_hbm.at[layer, e], wf2.at[ws], wsem.at[ws]))

    def gather_start(idx, slot):
        for r in range(MOE_ROWS):
            tok = idx[0, 0, r] & (n_tok - 1)
            pltpu.make_async_copy(tok_hbm.at[tok], xbuf.at[slot, r], gsem.at[slot]).start()

    def gather_wait(slot):
        pltpu.make_async_copy(tok_hbm.at[pl.ds(0, MOE_ROWS)], xbuf.at[slot], gsem.at[slot]).wait()

    def scatter_start(idx, slot):
        for r in range(MOE_ROWS):
            pltpu.make_async_copy(ybuf.at[slot, r], y_hbm.at[idx[0, 0, r]], ssem.at[slot]).start(priority=r % 2)

    def scatter_wait(slot):
        pltpu.make_async_copy(ybuf.at[slot], y_hbm.at[pl.ds(0, MOE_ROWS)], ssem.at[slot]).wait()

    @pl.when(i == 0)
    def _():
        xbuf[...] = jnp.zeros_like(xbuf)
        ybuf[...] = jnp.zeros_like(ybuf)
        spare0 = pltpu.make_async_copy(
            ybuf.at[0], y_hbm.at[pl.ds(MOE_TOP_K * n_tok, MOE_ROWS)], ssem.at[0])
        spare0.start()
        for cp in weight_copies(be_ref[0], 0):
            cp.start(priority=WEIGHT_DMA_PRIORITY)
        gather_start(idx_ref, 0)

    @pl.when(used)
    def _():
        ws = ws_ref[i]

        @pl.when(first_ref[i] == 1)
        def _():
            for cp in weight_copies(be_ref[i], ws):
                cp.wait()
            nxt = nxt_ref[i]

            @pl.when(nxt >= 0)
            def _():
                for cp in weight_copies(nxt, 1 - ws):
                    cp.start(priority=WEIGHT_DMA_PRIORITY)

            w1b[...] = wf1[ws].astype(BF16)
            w3b[...] = wf3[ws].astype(BF16)
            w2b[...] = wf2[ws].astype(BF16)

        gather_wait(xs)
        xb_ref[...] = _rows_from_tiles(xbuf[xs].astype(F32)).astype(BF16)
        gather_start(idxn_ref, 1 - xs)
        scatter_start(idxp_ref, 1 - xs)
        xb = xb_ref[...]
        h = (_silu(_dot(xb, w1b[...])) * _dot(xb, w3b[...])).astype(BF16)
        y = _tiles_from_rows(_dot(h, w2b[...])).astype(BF16)
        scatter_wait(xs)
        ybuf[xs] = y

    @pl.when(jnp.logical_not(used) & prev_used)
    def _():
        gather_wait(xs)
        scatter_start(idxp_ref, 1 - xs)
        scatter_wait(1 - xs)
        scatter_wait(xs)


def _moe_dispatch(eid_t, n_blocks):
    n_tok = eid_t.shape[1]
    n_slots = (n_blocks + 2) * MOE_ROWS
    slot_rows = -(-n_slots // SMEM_1D_TILE)
    assert n_blocks <= LANES and n_tok % SMEM_1D_TILE == 0 and SMEM_1D_TILE % MOE_ROWS == 0
    slot, meta = pl.pallas_call(
        functools.partial(_dispatch_kernel, n_tok=n_tok),
        in_specs=[pl.BlockSpec(memory_space=pltpu.VMEM)],
        out_specs=[pl.BlockSpec(memory_space=pltpu.SMEM), pl.BlockSpec(memory_space=pltpu.VMEM)],
        out_shape=[jax.ShapeDtypeStruct((slot_rows * SMEM_1D_TILE,), jnp.int32),
                   jax.ShapeDtypeStruct((MOE_ID_ROWS, LANES), jnp.int32)],
        scratch_shapes=[pltpu.VMEM((MOE_TOP_K * n_tok,), jnp.int32),
                        pltpu.VMEM((slot_rows * SMEM_1D_TILE,), jnp.int32),
                        pltpu.SMEM((MOE_TOP_K * n_tok,), jnp.int32),
                        pltpu.SemaphoreType.DMA(())],
        name="moe_dispatch",
    )(eid_t)
    return (slot[:n_slots].reshape(n_blocks + 2, 1, MOE_ROWS),) + tuple(meta[r, :n_blocks] for r in range(5))


def _dispatch_kernel(eid_ref, slot_ref, meta_ref, dest_vmem, init_vmem, dest_smem, sem, *, n_tok):
    tile = MOE_ROWS
    n_tiles = n_tok // tile
    sub = lax.broadcasted_iota(jnp.int32, (N_EXPERTS, tile), 0)
    si = lax.broadcasted_iota(jnp.int32, (tile, tile), 0)
    ti = lax.broadcasted_iota(jnp.int32, (tile, tile), 1)
    before = jnp.where(si < ti, 1.0, 0.0).astype(BF16)

    def one_hots(j):
        ids = eid_ref[:, j * tile:(j + 1) * tile]
        return [jnp.where(sub == ids[k:k + 1], 1.0, 0.0) for k in range(MOE_TOP_K)]

    carry = jnp.zeros((N_EXPERTS, 1), F32)
    ranks = []
    for j in range(n_tiles):
        oh = one_hots(j)
        both = oh[0] + oh[1]
        seen = carry + _dot(both.astype(BF16), before)
        ranks.append([jnp.sum(seen * o, axis=0, keepdims=True) for o in oh])
        carry = carry + jnp.sum(both, axis=1, keepdims=True)

    counts = carry
    nblk = jnp.floor((counts + float(MOE_ROWS - 1)) * (1.0 / MOE_ROWS))
    ei = lax.broadcasted_iota(jnp.int32, (N_EXPERTS, N_EXPERTS), 0)
    ej = lax.broadcasted_iota(jnp.int32, (N_EXPERTS, N_EXPERTS), 1)
    lower = jnp.where(ej < ei, 1.0, 0.0).astype(BF16)
    first_blk = _dot(lower, jnp.broadcast_to(nblk, (N_EXPERTS, LANES)).astype(BF16))[:, 0:1]
    first_slot = first_blk * float(MOE_ROWS)

    per_row = SMEM_1D_TILE // tile
    for k in range(MOE_TOP_K):
        for q in range(n_tiles // per_row):
            parts = []
            for j in range(q * per_row, (q + 1) * per_row):
                parts.append(jnp.sum(first_slot * one_hots(j)[k], axis=0, keepdims=True) + ranks[j][k])
            dest = jnp.concatenate(parts, axis=1).astype(jnp.int32)
            dest_vmem[pl.ds(k * n_tok + q * SMEM_1D_TILE, SMEM_1D_TILE)] = dest.reshape(SMEM_1D_TILE)

    lane = lax.broadcasted_iota(jnp.int32, (1, SMEM_1D_TILE), 1)
    for q in range(init_vmem.shape[0] // SMEM_1D_TILE):
        pos = q * SMEM_1D_TILE + lane
        spare = MOE_TOP_K * n_tok + ((pos // MOE_ROWS + 1) % 2) * MOE_ROWS + pos % MOE_ROWS
        init_vmem[pl.ds(q * SMEM_1D_TILE, SMEM_1D_TILE)] = spare.reshape(SMEM_1D_TILE)
    copies = [pltpu.make_async_copy(dest_vmem, dest_smem, sem), pltpu.make_async_copy(init_vmem, slot_ref, sem)]
    for cp in copies:
        cp.start()
    for cp in copies:
        cp.wait()

    def place(t, carry_):
        for k in range(MOE_TOP_K):
            slot_ref[dest_smem[k * n_tok + t] + MOE_ROWS] = k * n_tok + t
        return carry_

    lax.fori_loop(0, n_tok, place, 0, unroll=8)

    b = lax.broadcasted_iota(jnp.int32, (N_EXPERTS, LANES), 1).astype(F32)
    e_col = lax.broadcasted_iota(jnp.int32, (N_EXPERTS, LANES), 0).astype(F32)
    b_row = b[0:1]
    be = jnp.minimum(jnp.sum(jnp.where(first_blk + nblk <= b, 1.0, 0.0), axis=0, keepdims=True), N_EXPERTS - 1.0)
    mine = e_col == be
    cnt_b = jnp.sum(jnp.where(mine, counts, 0.0), axis=0, keepdims=True)
    start_b = jnp.sum(jnp.where(mine, first_blk, 0.0), axis=0, keepdims=True)
    nv = jnp.clip(cnt_b - (b_row - start_b) * MOE_ROWS, 0.0, float(MOE_ROWS))
    nv = jnp.where(b_row < jnp.sum(nblk, axis=0, keepdims=True), nv, 0.0)
    first = jnp.where((nv > 0) & ((b_row == 0) | (be != pltpu.roll(be, 1, 1))), 1.0, 0.0)
    li = lax.broadcasted_iota(jnp.int32, (LANES, LANES), 0)
    lj = lax.broadcasted_iota(jnp.int32, (LANES, LANES), 1)
    upto = jnp.where(li <= lj, 1.0, 0.0).astype(BF16)
    run = _dot(jnp.broadcast_to(first, (MOE_ID_ROWS, LANES)).astype(BF16), upto)[0:1] - 1.0
    ws = run - 2.0 * jnp.floor(run * 0.5)
    later = jnp.min(jnp.where((e_col > be) & (counts > 0), e_col, float(LANES)), axis=0, keepdims=True)
    nxt = jnp.where(later >= float(N_EXPERTS), -1.0, later)
    rows = [be, nv, first, ws, nxt] + [jnp.zeros_like(be)] * (MOE_ID_ROWS - 5)
    meta_ref[...] = jnp.concatenate(rows, axis=0).astype(jnp.int32)


def _moe_experts(tok, eid, w1, w3, w2, layer):
    n_tok, n_sub, _ = tok.shape
    d = n_sub * LANES
    assert n_tok & (n_tok - 1) == 0
    ff = w1.shape[3]
    n_assign = n_tok * MOE_TOP_K
    n_blocks = -(-(n_assign + N_EXPERTS * (MOE_ROWS - 1)) // MOE_ROWS)
    slot, be, nv, first, ws, nxt = _moe_dispatch(eid, n_blocks)
    grid_spec = pltpu.PrefetchScalarGridSpec(
        num_scalar_prefetch=5,
        grid=(n_blocks + 1,),
        in_specs=[
            pl.BlockSpec((1, 1, MOE_ROWS), lambda i, *_: (i + 1, 0, 0), memory_space=pltpu.SMEM),
            pl.BlockSpec((1, 1, MOE_ROWS), lambda i, *_: (jnp.minimum(i + 2, n_blocks + 1), 0, 0),
                         memory_space=pltpu.SMEM),
            pl.BlockSpec((1, 1, MOE_ROWS), lambda i, *_: (i, 0, 0), memory_space=pltpu.SMEM),
            pl.BlockSpec(memory_space=pl.ANY),
            pl.BlockSpec(memory_space=pl.ANY),
            pl.BlockSpec(memory_space=pl.ANY),
            pl.BlockSpec(memory_space=pl.ANY),
        ],
        out_specs=pl.BlockSpec(memory_space=pl.ANY),
        scratch_shapes=[
            pltpu.VMEM((2, MOE_ROWS, n_sub, LANES), BF16), pltpu.VMEM((2, MOE_ROWS, n_sub, LANES), BF16),
            pltpu.VMEM((MOE_ROWS, d), BF16),
            pltpu.VMEM((2, d, ff), F32), pltpu.VMEM((2, d, ff), F32), pltpu.VMEM((2, ff, d), F32),
            pltpu.VMEM((d, ff), BF16), pltpu.VMEM((d, ff), BF16), pltpu.VMEM((ff, d), BF16),
            pltpu.SemaphoreType.DMA((2,)), pltpu.SemaphoreType.DMA((2,)), pltpu.SemaphoreType.DMA((2,)),
        ],
    )
    return pl.pallas_call(
        functools.partial(_moe_kernel, n_tok=n_tok, layer=layer, n_blocks=n_blocks),
        grid_spec=grid_spec,
        out_shape=jax.ShapeDtypeStruct((MOE_TOP_K * n_tok + 2 * MOE_ROWS, n_sub, LANES), BF16),
        compiler_params=pltpu.CompilerParams(dimension_semantics=("arbitrary",),
                                             vmem_limit_bytes=MOE_VMEM_LIMIT_BYTES),
        name="moe_experts",
    )(be, nv, first, ws, nxt, slot, slot, slot, tok, w1, w3, w2)


def _rope_tables(n_seq):
    half = HEAD_DIM // 2
    n_freq = half // 2
    t = jnp.arange(n_seq)
    row = (t // GRID_W).astype(F32)
    col = (t % GRID_W).astype(F32)
    inv_freq = ROPE_BASE ** (-jnp.arange(n_freq, dtype=F32) / n_freq)
    ang_r = row[:, None] * inv_freq[None, :]
    ang_c = col[:, None] * inv_freq[None, :]
    cos = jnp.concatenate([jnp.cos(ang_r)] * 2 + [jnp.cos(ang_c)] * 2, axis=-1)
    sin = jnp.concatenate([-jnp.sin(ang_r), jnp.sin(ang_r), -jnp.sin(ang_c), jnp.sin(ang_c)], axis=-1)
    return cos, sin


def _router_weights(w_g, b_g, w_e, b_e):
    d = w_g.shape[0]
    n = w_g.shape[1] + w_e.shape[1]
    wr = jnp.concatenate([w_g, w_e, jnp.zeros((d, LANES - n), F32)], axis=1)
    rb = jnp.concatenate([b_g, b_e, jnp.zeros((LANES - n,), F32)]).reshape(1, LANES)
    hi = wr.astype(BF16)
    lo = (wr - hi.astype(F32)).astype(BF16)
    return hi, lo, rb


def kernel(x, c, ctx, c_ctx, ada_w, ada_b, ln_g, ln_b, mix_w_in, att_sink, hg_lb, hg_norm_g, mix_w_out, pool_w_in, pool_w_grp, pool_scale, pool_w_out, rt_group_w, rt_group_b, rt_expert_w, rt_expert_b, moe_w1, moe_w3, moe_w2):
    b, n, d = x.shape
    n_ctx = ctx.shape[1]
    t = b * n
    xf = x.reshape(t, d)
    ctxf = ctx.reshape(b * n_ctx, d)

    cond = jnp.concatenate([c, c_ctx[None, :], jnp.zeros((8 - b - 1, d), F32)], axis=0)
    mod = _ada_mod(cond, ada_w, ada_b)

    def chunk(l, j, rows=slice(0, b)):
        return mod[l, rows, j * d:(j + 1) * d][:, None, :]

    w_in = mix_w_in[0].astype(BF16)
    cos, sin = _rope_tables(n)
    q_w, kv_w = ATT_HEADS * HEAD_DIM, ATT_KV_HEADS * HEAD_DIM
    n_att = q_w + 2 * kv_w
    hk = HG_HEADS * HG_KEY
    assert n_att == 2 * PROJ_TN and hk == PROJ_TN
    a_lat, f_lat = _mod_matmul(xf, chunk(0, 1), chunk(0, 0), w_in, cos, sin,
                               lambda j: jnp.where(j < 3, j, jnp.where(j < 5, j + 2, j - 2)), n_att + 3 * hk, 2 * hk,
                               n_q=q_w, n_rope=q_w + kv_w, n_seq=n, tm=1024, tn=PROJ_TN)
    ctx_rows = slice(b, b + 1)
    a_ctx, f_ctx = _mod_matmul(ctxf, chunk(0, 1, ctx_rows), chunk(0, 0, ctx_rows), w_in, cos, sin,
                               lambda j: jnp.where(j < 1, 1, jnp.where(j < 2, 5, j + 1)), 2 * kv_w + hk, 2 * hk,
                               n_q=0, n_rope=0, n_seq=n, tm=b * n_ctx, tn=PROJ_TN)
    att = _window_attention(a_lat, a_ctx, att_sink[0], b, n, n_ctx)
    o_f = _hgrn2_scan(a_lat, f_lat, a_ctx, f_ctx, hg_lb, b, n, n_ctx, rev=False)
    o_b = _hgrn2_scan(a_lat, f_lat, a_ctx, f_ctx, hg_lb, b, n, n_ctx, rev=True)
    wr_hi, wr_lo, rb = _router_weights(rt_group_w[0], rt_group_b[0], rt_expert_w[0], rt_expert_b[0])
    x1, tok, eid, gate = _even_out(
        att, o_f, o_b, a_lat, hg_norm_g[0][None, :], mix_w_out[0].astype(BF16), xf,
        chunk(0, 2), chunk(0, 4), chunk(0, 3), ln_g[0, 0][None, :], ln_b[0, 0][None, :], wr_hi, wr_lo, rb, n)
    y2 = _moe_experts(tok, eid, moe_w1, moe_w3, moe_w2, 0)

    x2, u = _combine_call(x1, y2, gate, chunk(0, 5), ln_g[0, 1][None, :], ln_b[0, 1][None, :], n,
                          proj=(chunk(1, 1), chunk(1, 0), pool_w_in[0].astype(BF16)))
    wr_hi, wr_lo, rb = _router_weights(rt_group_w[1], rt_group_b[1], rt_expert_w[1], rt_expert_b[1])
    x3, tok, eid, gate = _pool_out(
        u, pool_w_grp[0].astype(BF16), pool_scale[0][None, :], pool_w_out[0].astype(BF16), x2,
        chunk(1, 2), chunk(1, 4), chunk(1, 3), ln_g[1, 0][None, :], ln_b[1, 0][None, :], wr_hi, wr_lo, rb, n)
    y2 = _moe_experts(tok, eid, moe_w1, moe_w3, moe_w2, 1)
    out = _combine_call(x3, y2, gate, chunk(1, 5), ln_g[1, 1][None, :], ln_b[1, 1][None, :], n)
    return out.reshape(b, n, d)
```

```python
import functools

import jax
import jax.numpy as jnp
from jax import lax
from jax.experimental import pallas as pl
from jax.experimental.pallas import tpu as pltpu

F32 = jnp.float32
BF16 = jnp.bfloat16

LANES = 128
MXU_TILE = 256
VMEM_LIMIT_BYTES = 56 * 1024 * 1024
MOE_VMEM_LIMIT_BYTES = 60 * 1024 * 1024

GRID_W = 64
ATT_HEADS = 8
ATT_KV_HEADS = 4
ATT_GROUP = ATT_HEADS // ATT_KV_HEADS
HEAD_DIM = 128
WINDOW = 128
ATT_BLOCK = 128
ROPE_BASE = 10000.0
HG_HEADS = 8
HG_KEY = 128
HG_CHUNK = 64
HG_STEP_CHUNKS = 4
HG_SUB = 16
HG_FAST_RANGE = 80.0
NORM_EPS = 1e-6
POOL_WINDOWS = (2, 4, 8, 16)
POOL_HALO = 8
MOE_GROUPS = 4
MOE_EXPERTS_PER_GROUP = 8
N_EXPERTS = MOE_GROUPS * MOE_EXPERTS_PER_GROUP
MOE_TOP_K = 2
MOE_ROWS = 256
ROW_TILE = 512
SUB_ROWS = 256
PROJ_TN = 1024
MOE_ID_ROWS = 8
SMEM_1D_TILE = 1024
WEIGHT_DMA_PRIORITY = 1
LN_EPS = 1e-5
DEPTH = 2
DEEPNORM_ALPHA = (2 * DEPTH) ** 0.25


def _dot(a, b):
    return jnp.dot(a, b, preferred_element_type=F32)


def _dot_nt(a, b):
    return lax.dot_general(a, b, (((1,), (1,)), ((), ())), preferred_element_type=F32)


def _dot_tn(a, b):
    return lax.dot_general(a, b, (((0,), (0,)), ((), ())), preferred_element_type=F32)


def _sigmoid(x):
    return 1.0 / (1.0 + jnp.exp(-x))


def _silu(x):
    return x * _sigmoid(x)


def _params(*sem):
    return pltpu.CompilerParams(dimension_semantics=sem, vmem_limit_bytes=VMEM_LIMIT_BYTES)


def _tiles_from_rows(x):
    n = x.shape[1] // LANES
    return jnp.swapaxes(jnp.stack([x[:, s * LANES:(s + 1) * LANES] for s in range(n)], axis=0), 0, 1)


def _rows_from_tiles(x3):
    xt = jnp.swapaxes(x3, 0, 1)
    return jnp.concatenate([xt[s] for s in range(xt.shape[0])], axis=-1)


def _layer_norm(z, g, b):
    mu = jnp.mean(z, axis=-1, keepdims=True)
    zc = z - mu
    var = jnp.mean(zc * zc, axis=-1, keepdims=True)
    return zc * lax.rsqrt(var + LN_EPS) * g + b


def _ada_kernel(s_ref, w_ref, b_ref, o_ref):
    s = _silu(s_ref[...]).astype(BF16)
    o_ref[0] = _dot(s, w_ref[0].astype(BF16)) + b_ref[0]


def _ada_mod(s, ada_w, ada_b, tn=1024):
    n_l, d, n = ada_w.shape
    return pl.pallas_call(
        _ada_kernel,
        grid=(n_l, n // tn),
        in_specs=[
            pl.BlockSpec((8, d), lambda l, j: (0, 0)),
            pl.BlockSpec((1, d, tn), lambda l, j: (l, 0, j)),
            pl.BlockSpec((1, 1, tn), lambda l, j: (l, 0, j)),
        ],
        out_specs=pl.BlockSpec((1, 8, tn), lambda l, j: (l, 0, j)),
        out_shape=jax.ShapeDtypeStruct((n_l, 8, n), F32),
        compiler_params=_params("parallel", "parallel"),
        name="ada_mod",
    )(s, ada_w, ada_b.reshape(n_l, 1, n))


def _rope(t, cos, sin_signed, first_half):
    partner = jnp.where(first_half, pltpu.roll(t, 96, 1), pltpu.roll(t, 32, 1))
    return t * cos + partner * sin_signed


def _modmm_kernel(x_ref, sc_ref, sh_ref, w_ref, cos_ref, sin_ref, oa_ref, ob_ref, xs_ref, *,
                  n_q, n_rope, n_a_tiles):
    j = pl.program_id(1)
    tm, tn = x_ref.shape[0], w_ref.shape[1]
    halves = [slice(0, tm // 2), slice(tm // 2, tm)]

    def dots():
        return [_dot(xs_ref[r], w_ref[...]) for r in halves]

    def store_bf16(jt, r, acc):
        lane = lax.broadcasted_iota(jnp.int32, (1, HEAD_DIM), 1)
        first_half = (lane % 64) < 32
        for h in range(tn // HEAD_DIM):
            sl = slice(h * HEAD_DIM, (h + 1) * HEAD_DIM)
            col = jt * tn + h * HEAD_DIM
            if col < n_rope:
                scale = HEAD_DIM ** -0.5 if col < n_q else 1.0
                oa_ref[r, sl] = _rope(acc[:, sl], cos_ref[r] * scale, sin_ref[r] * scale, first_half).astype(BF16)
            else:
                oa_ref[r, sl] = acc[:, sl].astype(BF16)

    for jt in range(n_a_tiles):
        @pl.when(j == jt)
        def _(jt=jt):
            if jt == 0:
                for r in halves:
                    xs_ref[r] = (x_ref[r] * (1.0 + sc_ref[0]) + sh_ref[0]).astype(BF16)
            for r, acc in zip(halves, dots()):
                store_bf16(jt, r, acc)

    @pl.when(j >= n_a_tiles)
    def _():
        for r, acc in zip(halves, dots()):
            ob_ref[r] = acc


def _mod_matmul(x, sc, sh, w, cos, sin, col_map, n_a, n_b, n_q, n_rope, n_seq, tm, tn):
    m, k = x.shape
    rows_per_mod = m // sc.shape[0]
    ta, tb = n_a // tn, n_b // tn
    tab = pl.BlockSpec((tm, HEAD_DIM), lambda i, j: ((i * tm % n_seq) // tm, 0))
    return pl.pallas_call(
        functools.partial(_modmm_kernel, n_q=n_q, n_rope=n_rope, n_a_tiles=ta),
        grid=(m // tm, ta + tb),
        in_specs=[
            pl.BlockSpec((tm, k), lambda i, j: (i, 0)),
            pl.BlockSpec((1, 1, k), lambda i, j: (i * tm // rows_per_mod, 0, 0)),
            pl.BlockSpec((1, 1, k), lambda i, j: (i * tm // rows_per_mod, 0, 0)),
            pl.BlockSpec((k, tn), lambda i, j: (0, col_map(j))),
            tab, tab,
        ],
        out_specs=[pl.BlockSpec((tm, tn), lambda i, j: (i, jnp.minimum(j, ta - 1))),
                   pl.BlockSpec((tm, tn), lambda i, j: (i, jnp.maximum(j - ta, 0)))],
        out_shape=[jax.ShapeDtypeStruct((m, n_a), BF16), jax.ShapeDtypeStruct((m, n_b), F32)],
        scratch_shapes=[pltpu.VMEM((tm, k), BF16)],
        compiler_params=_params("parallel", "arbitrary"),
        name="mod_matmul",
    )(x, sc, sh, w, cos, sin)


def _attn_kernel(sink_ref, q_ref, kp_ref, kc_ref, kn_ref, vp_ref, vc_ref, vn_ref, kx_ref, vx_ref,
                 mp_ref, mn_ref, o_ref, *, n_blocks):
    n = pl.program_id(1)
    blk = ATT_BLOCK
    row1 = lax.broadcasted_iota(jnp.int32, (ATT_GROUP * blk, 1), 0)
    has_prev, has_next = n > 0, n < n_blocks - 1
    heads = range(ATT_KV_HEADS)

    def kv(h):
        return slice(h * HEAD_DIM, (h + 1) * HEAD_DIM)

    scores = []
    for h in heads:
        q2 = jnp.concatenate([q_ref[:, (ATT_GROUP * h + g) * HEAD_DIM:(ATT_GROUP * h + g + 1) * HEAD_DIM]
                              for g in range(ATT_GROUP)], axis=0)
        k_all = jnp.concatenate([kp_ref[:, kv(h)], kc_ref[:, kv(h)], kn_ref[:, kv(h)], kx_ref[:, kv(h)]], axis=0)
        scores.append(_dot_nt(q2, k_all))
    probs, dens = [], []
    for h in heads:
        s = scores[h]
        parts = [jnp.where(has_prev, s[:, :blk] + mp_ref[...], -jnp.inf), s[:, blk:2 * blk],
                 jnp.where(has_next, s[:, 2 * blk:3 * blk] + mn_ref[...], -jnp.inf)]
        parts += [s[:, c:c + blk] for c in range(3 * blk, s.shape[1], blk)]
        sink = jnp.where(row1 < blk, sink_ref[ATT_GROUP * h], sink_ref[ATT_GROUP * h + 1])
        m = jnp.maximum(jnp.max(functools.reduce(jnp.maximum, parts), axis=-1, keepdims=True), sink)
        p = [jnp.exp(x - m) for x in parts]
        dens.append(jnp.sum(functools.reduce(jnp.add, p), axis=-1, keepdims=True) + jnp.exp(sink - m))
        probs.append(jnp.concatenate(p, axis=-1).astype(BF16))
    for h in heads:
        v_all = jnp.concatenate([vp_ref[:, kv(h)], vc_ref[:, kv(h)], vn_ref[:, kv(h)], vx_ref[:, kv(h)]], axis=0)
        o = _dot(probs[h], v_all) / dens[h]
        for g in range(ATT_GROUP):
            col = (ATT_GROUP * h + g) * HEAD_DIM
            o_ref[:, col:col + HEAD_DIM] = o[g * blk:(g + 1) * blk].astype(o_ref.dtype)


def _window_attention(qkv, kv_ctx, sink, batch, n_seq, n_ctx):
    assert ATT_GROUP == 2 and WINDOW == ATT_BLOCK
    nb = n_seq // ATT_BLOCK
    qw, kw = ATT_HEADS * HEAD_DIM, ATT_KV_HEADS * HEAD_DIM
    kcol, vcol = qw // kw, qw // kw + 1

    def rows(off):
        return lambda b, n, off=off: b * nb + jnp.clip(n + off, 0, nb - 1)

    kspec = [pl.BlockSpec((ATT_BLOCK, kw), lambda b, n, r=rows(o): (r(b, n), kcol)) for o in (-1, 0, 1)]
    vspec = [pl.BlockSpec((ATT_BLOCK, kw), lambda b, n, r=rows(o): (r(b, n), vcol)) for o in (-1, 0, 1)]
    r = jnp.arange(ATT_GROUP * ATT_BLOCK)[:, None] % ATT_BLOCK
    c = jnp.arange(ATT_BLOCK)[None, :]
    mask_prev = jnp.where(c >= r, 0.0, -jnp.inf).astype(F32)
    mask_next = jnp.where(c <= r, 0.0, -jnp.inf).astype(F32)
    mspec = pl.BlockSpec(mask_prev.shape, lambda b, n: (0, 0))
    return pl.pallas_call(
        functools.partial(_attn_kernel, n_blocks=nb),
        grid=(batch, nb),
        in_specs=[pl.BlockSpec(memory_space=pltpu.SMEM),
                  pl.BlockSpec((ATT_BLOCK, qw), lambda b, n: (b * nb + n, 0))]
        + kspec + vspec
        + [pl.BlockSpec((n_ctx, kw), lambda b, n: (b, 0)), pl.BlockSpec((n_ctx, kw), lambda b, n: (b, 1)),
           mspec, mspec],
        out_specs=pl.BlockSpec((ATT_BLOCK, qw), lambda b, n: (b * nb + n, 0)),
        out_shape=jax.ShapeDtypeStruct((batch * n_seq, qw), BF16),
        compiler_params=_params("parallel", "parallel"),
        name="window_attention",
    )(sink, qkv, qkv, qkv, qkv, qkv, qkv, qkv, kv_ctx, kv_ctx, mask_prev, mask_next)


def _gla_step(zf, q_raw, v, lb, st_ref, o_ref, rev):
    c_len = HG_CHUNK
    n_rows = zf.shape[0]
    n_sub = n_rows // c_len
    shift = c_len.bit_length() - 1
    order = range(n_sub - 1, -1, -1) if rev else range(n_sub)

    def head(h):
        return slice(h * HG_KEY, (h + 1) * HG_KEY)

    def chunk(i):
        return slice(i * c_len, (i + 1) * c_len)

    def seen(n):
        ri = lax.broadcasted_iota(jnp.int32, (n, n), 0)
        ci = lax.broadcasted_iota(jnp.int32, (n, n), 1)
        return ((ri >> shift) == (ci >> shift)) & ((ci >= ri) if rev else (ci <= ri))

    f = lb + (1.0 - lb) * _sigmoid(zf)
    k = 1.0 - f
    g = jnp.log(f)
    tri = jnp.where(seen(n_rows), 1.0, 0.0).astype(BF16)
    g1 = g.astype(BF16)
    r1 = g - g1.astype(F32)
    g2 = r1.astype(BF16)
    g3 = (r1 - g2.astype(F32)).astype(BF16)
    c = _dot(tri, g1) + _dot(tri, g2) + _dot(tri, g3)
    c_end = [c[i * c_len:i * c_len + 1] if rev else c[(i + 1) * c_len - 1:(i + 1) * c_len] for i in range(n_sub)]
    c_end_rows = jnp.concatenate([jnp.broadcast_to(ce, (c_len, ce.shape[1])) for ce in c_end], axis=0)
    k_end = (k * jnp.exp(c_end_rows - c)).astype(BF16)
    dec = [jnp.exp(ce) for ce in c_end]
    vb = v.astype(BF16)

    def advance(states, i):
        new = []
        for p in range(0, HG_HEADS, 2):
            lanes = slice(p * HG_KEY, (p + 2) * HG_KEY)
            inc = _dot_tn(vb[chunk(i), lanes], k_end[chunk(i), lanes])
            for j in range(2):
                blk = slice(j * HG_KEY, (j + 1) * HG_KEY)
                new.append(states[p + j] * dec[i][:, head(p + j)] + inc[blk, blk])
        return new

    if o_ref is None:
        states = [st_ref[h] for h in range(HG_HEADS)]
        for i in order:
            states = advance(states, i)
        for h in range(HG_HEADS):
            st_ref[h] = states[h]
        return

    q = _silu(q_raw.astype(F32))
    q_in = (q * jnp.exp(c)).astype(BF16)
    lowest = functools.reduce(jnp.minimum, c_end)
    in_range = jnp.min(lowest) >= -HG_FAST_RANGE

    def stack(x, i, h0, n):
        return jnp.concatenate([x[chunk(i), head(h)] for h in range(h0, h0 + n)], axis=0)

    @pl.when(in_range)
    def _():
        k_in = (k * jnp.exp(-c)).astype(BF16)
        n_qk = MXU_TILE // c_len
        same = seen(n_qk * c_len)
        groups = [(i, h0) for i in order for h0 in range(0, HG_HEADS, n_qk)]
        raw = [_dot_nt(stack(q_in, i, h0, n_qk), stack(k_in, i, h0, n_qk)) for i, h0 in groups]
        masked = [jnp.where(same, s, 0.0).astype(BF16) for s in raw]
        intra = {}
        for (i, h0), sc in zip(groups, masked):
            pv = _dot(sc, stack(vb, i, h0, n_qk))
            for j in range(n_qk):
                intra[i, h0 + j] = pv[j * c_len:(j + 1) * c_len]
        states = [st_ref[h] for h in range(HG_HEADS)]
        for i in order:
            out = []
            for p in range(0, HG_HEADS, 2):
                st_pair = jnp.concatenate([states[p].astype(BF16), states[p + 1].astype(BF16)], axis=0)
                inter = _dot_nt(stack(q_in, i, p, 2), st_pair)
                for j in range(2):
                    out.append(inter[j * c_len:(j + 1) * c_len, j * HG_KEY:(j + 1) * HG_KEY] + intra[i, p + j])
            o_ref[chunk(i), :] = jnp.concatenate(out, axis=1)
            states = advance(states, i)
        for h in range(HG_HEADS):
            st_ref[h] = states[h]

    @pl.when(jnp.logical_not(in_range))
    def _():
        for i in order:
            r = chunk(i)
            _gla_intra_exact(q[r], k[r], v[r].astype(F32), vb[r], c[r], q_in[r], st_ref, o_ref, r, rev)
            states = advance([st_ref[h] for h in range(HG_HEADS)], i)
            for h in range(HG_HEADS):
                st_ref[h] = states[h]


def _gla_intra_exact(q, k, v, vb, c, q_in, st_ref, o_ref, rows, rev):
    c_len = q.shape[0]
    pairs = []
    size = c_len // 2
    while size >= HG_SUB:
        for lo in range(0, c_len, 2 * size):
            pairs.append((lo, lo + size, lo + 2 * size))
        size //= 2
    scaled = []
    for lo, mid, hi in pairs:
        if rev:
            late, early, bnd = slice(lo, mid), slice(mid, hi), mid
        else:
            late, early, bnd = slice(mid, hi), slice(lo, mid), mid - 1
        cb = c[bnd:bnd + 1]
        q_l = (q[late] * jnp.exp(c[late] - cb)).astype(BF16)
        k_e = (k[early] * jnp.exp(cb - c[early])).astype(BF16)
        scaled.append((late, early, q_l, k_e))
    n_sub = c_len // HG_SUB
    t_idx = lax.broadcasted_iota(jnp.int32, (HG_SUB, 1), 0)
    diag = [[None] * HG_HEADS for _ in range(n_sub)]
    for b in range(n_sub):
        r0 = b * HG_SUB
        qb, cb = q[r0:r0 + HG_SUB], c[r0:r0 + HG_SUB]
        for s in range(HG_SUB):
            row = r0 + s
            ok = (t_idx <= s) if rev else (t_idx >= s)
            w = qb * k[row:row + 1] * jnp.exp(jnp.where(ok, cb - c[row:row + 1], -jnp.inf))
            for h in range(HG_HEADS):
                sl = slice(h * HG_KEY, (h + 1) * HG_KEY)
                contrib = jnp.sum(w[:, sl], axis=-1, keepdims=True) * v[row:row + 1, sl]
                diag[b][h] = contrib if diag[b][h] is None else diag[b][h] + contrib

    for h in range(HG_HEADS):
        sl = slice(h * HG_KEY, (h + 1) * HG_KEY)
        o_h = _dot_nt(q_in[:, sl], st_ref[h].astype(BF16))
        parts = [diag[b][h] for b in range(n_sub)]
        for late, early, q_l, k_e in scaled:
            sc = _dot_nt(q_l[:, sl], k_e[:, sl]).astype(BF16)
            add = _dot(sc, vb[early, sl])
            b0 = late.start // HG_SUB
            for j in range((late.stop - late.start) // HG_SUB):
                parts[b0 + j] = parts[b0 + j] + add[j * HG_SUB:(j + 1) * HG_SUB]
        o_ref[rows, sl] = o_h + jnp.concatenate(parts, axis=0)


def _gla_kernel(lb_ref, zf_ref, q_ref, v_ref, zfc_ref, vc_ref, o_ref, st_ref, *, rev, n_ctx_steps):
    s = pl.program_id(1)

    @pl.when(s == 0)
    def _():
        st_ref[...] = jnp.zeros_like(st_ref)

    x = lb_ref[...]
    e = jnp.exp(x - jnp.max(x, axis=0, keepdims=True))
    lb = e[0:1] / jnp.sum(e, axis=0, keepdims=True)

    @pl.when(s < n_ctx_steps)
    def _():
        _gla_step(zfc_ref[...], None, vc_ref[...], lb, st_ref, None, rev)

    @pl.when(s >= n_ctx_steps)
    def _():
        _gla_step(zf_ref[...], q_ref[...], v_ref[...], lb, st_ref, o_ref, rev)


def _hgrn2_scan(a_lat, f_lat, a_ctx, f_ctx, hg_lb, batch, n_seq, n_ctx, rev):
    hk = HG_HEADS * HG_KEY
    rows = HG_CHUNK * HG_STEP_CHUNKS
    assert n_seq % rows == 0 and n_ctx % rows == 0
    nc, ncc = n_seq // rows, n_ctx // rows
    d = 1 if rev else 0

    def lat(b, s):
        j = jnp.maximum(s - ncc, 0)
        return b * nc + (nc - 1 - j if rev else j)

    def ctx(b, s):
        j = jnp.minimum(s, ncc - 1)
        return b * ncc + (ncc - 1 - j if rev else j)

    return pl.pallas_call(
        functools.partial(_gla_kernel, rev=rev, n_ctx_steps=ncc),
        grid=(batch, ncc + nc),
        in_specs=[
            pl.BlockSpec((None, hg_lb.shape[1], hk), lambda b, s: (d, 0, 0)),
            pl.BlockSpec((rows, hk), lambda b, s: (lat(b, s), d)),
            pl.BlockSpec((rows, hk), lambda b, s: (lat(b, s), 2)),
            pl.BlockSpec((rows, hk), lambda b, s: (lat(b, s), 3)),
            pl.BlockSpec((rows, hk), lambda b, s: (ctx(b, s), d)),
            pl.BlockSpec((rows, hk), lambda b, s: (ctx(b, s), 1)),
        ],
        out_specs=pl.BlockSpec((rows, hk), lambda b, s: (lat(b, s), 0)),
        out_shape=jax.ShapeDtypeStruct((batch * n_seq, hk), F32),
        scratch_shapes=[pltpu.VMEM((HG_HEADS, HG_KEY, HG_KEY), F32)],
        compiler_params=_params("parallel", "arbitrary"),
        name="hgrn2_bwd" if rev else "hgrn2_fwd",
    )(hg_lb, f_lat, a_lat, a_lat, f_ctx, a_ctx)


def _route(tok, wr_hi, wr_lo, rb):
    t_hi = tok.astype(BF16)
    t_lo = (tok - t_hi.astype(F32)).astype(BF16)
    lg = _dot(t_hi, wr_hi) + _dot(t_hi, wr_lo) + _dot(t_lo, wr_hi) + rb
    lane = lax.broadcasted_iota(jnp.int32, lg.shape, 1)
    lane_f = lane.astype(F32)
    ninf = -jnp.inf
    gl = jnp.where(lane < MOE_GROUPS, lg, ninf)
    gmax = jnp.max(gl, axis=-1, keepdims=True)
    g_idx = jnp.min(jnp.where(gl == gmax, lane_f, float(LANES)), axis=-1, keepdims=True)
    g_val = 1.0 / jnp.sum(jnp.exp(gl - gmax), axis=-1, keepdims=True)
    e_lane = lane_f - float(MOE_GROUPS)
    lo = g_idx * float(MOE_EXPERTS_PER_GROUP)
    in_grp = (e_lane >= lo) & (e_lane < lo + float(MOE_EXPERTS_PER_GROUP))
    el = jnp.where(in_grp, lg, ninf)
    l1 = jnp.max(el, axis=-1, keepdims=True)
    i1 = jnp.min(jnp.where(el == l1, e_lane, float(LANES)), axis=-1, keepdims=True)
    el2 = jnp.where(e_lane == i1, ninf, el)
    l2 = jnp.max(el2, axis=-1, keepdims=True)
    i2 = jnp.min(jnp.where(el2 == l2, e_lane, float(LANES)), axis=-1, keepdims=True)
    r = jnp.exp(l2 - l1)
    w1 = g_val / (1.0 + r)
    w2 = w1 * r
    eid = jnp.where(lane == 0, i1, jnp.where(lane == 1, i2, 0.0))
    gate = jnp.where(lane == 0, w1, jnp.where(lane == 1, w2, 0.0))
    eid_t = jnp.transpose(eid)[:MOE_ID_ROWS].astype(jnp.int32)
    return eid_t, gate


def _post_mix(y, x_ref, g1_ref, sc2_ref, sh2_ref, lng_ref, lnb_ref, wrh_ref, wrl_ref, rb_ref,
              x1_ref, tok_ref, eid_ref, gate_ref, r):
    x1 = _layer_norm(DEEPNORM_ALPHA * x_ref[r] + g1_ref[0] * y, lng_ref[...], lnb_ref[...])
    x1_ref[r] = x1
    tok = x1 * (1.0 + sc2_ref[0]) + sh2_ref[0]
    tok_ref[r] = _tiles_from_rows(tok).astype(BF16)
    eid, gate = _route(tok, wrh_ref[...], wrl_ref[...], rb_ref[...])
    eid_ref[:, r] = eid
    gate_ref[r] = gate


def _sub_tiles(n_rows):
    return [slice(s, s + SUB_ROWS) for s in range(0, n_rows, SUB_ROWS)]


def _even_out_kernel(att_ref, of_ref, ob_ref, gt_ref, ng_ref, wo_ref, *rest):
    tiles = _sub_tiles(att_ref.shape[0])
    lhs = []
    for r in tiles:
        o = of_ref[r] + ob_ref[r]
        pieces = []
        for h in range(HG_HEADS):
            oh = o[:, h * HG_KEY:(h + 1) * HG_KEY]
            pieces.append(oh * lax.rsqrt(jnp.mean(oh * oh, axis=-1, keepdims=True) + NORM_EPS))
        hg = (jnp.concatenate(pieces, axis=-1) * ng_ref[...] * _silu(gt_ref[r].astype(F32))).astype(BF16)
        lhs.append(jnp.concatenate([att_ref[r], hg], axis=-1))
    ys = [_dot(a, wo_ref[...]) for a in lhs]
    for r, y in zip(tiles, ys):
        _post_mix(y, *rest, r)


def _post_specs(d, tm, rows_per_batch):
    def bmap(i):
        return (i * tm // rows_per_batch, 0, 0)

    row = pl.BlockSpec((tm, d), lambda i: (i, 0))
    mod = pl.BlockSpec((1, 1, d), bmap)
    vec = pl.BlockSpec((1, d), lambda i: (0, 0))
    rw = pl.BlockSpec((d, LANES), lambda i: (0, 0))
    in_specs = [row, mod, mod, mod, vec, vec, rw, rw, pl.BlockSpec((1, LANES), lambda i: (0, 0))]
    lane_blk = pl.BlockSpec((tm, LANES), lambda i: (i, 0))
    tiles = pl.BlockSpec((tm, d // LANES, LANES), lambda i: (i, 0, 0))
    out_specs = [row, tiles, pl.BlockSpec((MOE_ID_ROWS, tm), lambda i: (0, i)), lane_blk]
    return in_specs, out_specs


def _post_out_shapes(t, d):
    return [jax.ShapeDtypeStruct((t, d), F32), jax.ShapeDtypeStruct((t, d // LANES, LANES), BF16),
            jax.ShapeDtypeStruct((MOE_ID_ROWS, t), jnp.int32), jax.ShapeDtypeStruct((t, LANES), F32)]


def _resident(shape):
    return pl.BlockSpec(shape, lambda i: (0,) * len(shape), pipeline_mode=pl.Buffered(1))


def _even_out(att, o_f, o_b, p, norm_g, w_out, x, g1, sc2, sh2, lng, lnb, wr_hi, wr_lo, rb, rows_per_batch,
              tm=ROW_TILE):
    t, d = x.shape
    hv = o_f.shape[1]
    post_in, post_out = _post_specs(d, tm, rows_per_batch)
    return pl.pallas_call(
        _even_out_kernel,
        grid=(t // tm,),
        in_specs=[
            pl.BlockSpec((tm, att.shape[1]), lambda i: (i, 0)),
            pl.BlockSpec((tm, hv), lambda i: (i, 0)),
            pl.BlockSpec((tm, hv), lambda i: (i, 0)),
            pl.BlockSpec((tm, hv), lambda i: (i, 4)),
            pl.BlockSpec((1, hv), lambda i: (0, 0)),
            _resident(w_out.shape),
        ] + post_in,
        out_specs=post_out,
        out_shape=_post_out_shapes(t, d),
        compiler_params=_params("parallel"),
        name="even_out",
    )(att, o_f, o_b, p, norm_g, w_out, x, g1, sc2, sh2, lng, lnb, wr_hi, wr_lo, rb)


def _combine(x_ref, ya_ref, yb_ref, gate_ref, g2_ref, lng_ref, lnb_ref, r):
    gate = gate_ref[r]
    y = (gate[:, 0:1] * _rows_from_tiles(ya_ref[r].astype(F32))
         + gate[:, 1:2] * _rows_from_tiles(yb_ref[r].astype(F32)))
    return _layer_norm(DEEPNORM_ALPHA * x_ref[r] + g2_ref[0] * y, lng_ref[...], lnb_ref[...])


def _combine_proj_kernel(x_ref, ya_ref, yb_ref, gate_ref, g2_ref, lng_ref, lnb_ref, sc_ref, sh_ref, w_ref,
                         x2_ref, u_ref):
    tiles = _sub_tiles(x_ref.shape[0])
    lhs = []
    for r in tiles:
        x2 = _combine(x_ref, ya_ref, yb_ref, gate_ref, g2_ref, lng_ref, lnb_ref, r)
        x2_ref[r] = x2
        lhs.append((x2 * (1.0 + sc_ref[0]) + sh_ref[0]).astype(BF16))
    for r, a in zip(tiles, lhs):
        u_ref[r] = _dot(a, w_ref[...])


def _combine_kernel(x_ref, ya_ref, yb_ref, gate_ref, g2_ref, lng_ref, lnb_ref, x2_ref):
    for r in _sub_tiles(x_ref.shape[0]):
        x2_ref[r] = _combine(x_ref, ya_ref, yb_ref, gate_ref, g2_ref, lng_ref, lnb_ref, r)


def _combine_call(x1, y2, gate, g2, lng, lnb, rows_per_batch, proj=None, tm=ROW_TILE):
    t, d = x1.shape
    nt = t // tm

    def bmap(i):
        return (i * tm // rows_per_batch, 0, 0)

    row = pl.BlockSpec((tm, d), lambda i: (i, 0))
    mod = pl.BlockSpec((1, 1, d), bmap)
    vec = pl.BlockSpec((1, d), lambda i: (0, 0))
    in_specs = [row, pl.BlockSpec((tm, d // LANES, LANES), lambda i: (i, 0, 0)),
                pl.BlockSpec((tm, d // LANES, LANES), lambda i: (nt + i, 0, 0)),
                pl.BlockSpec((tm, LANES), lambda i: (i, 0)), mod, vec, vec]
    args = [x1, y2, y2, gate, g2, lng, lnb]
    if proj is None:
        return pl.pallas_call(
            _combine_kernel, grid=(nt,), in_specs=in_specs, out_specs=row,
            out_shape=jax.ShapeDtypeStruct((t, d), F32),
            compiler_params=_params("parallel"), name="combine_ln")(*args)
    sc, sh, w = proj
    return pl.pallas_call(
        _combine_proj_kernel, grid=(nt,),
        in_specs=in_specs + [mod, mod, _resident(w.shape)],
        out_specs=[row, pl.BlockSpec((tm, w.shape[1]), lambda i: (i, 0))],
        out_shape=[jax.ShapeDtypeStruct((t, d), F32), jax.ShapeDtypeStruct((t, w.shape[1]), F32)],
        compiler_params=_params("parallel"), name="combine_ln_proj")(*args, sc, sh, w)


def _pool_out_kernel(up_ref, uc_ref, un_ref, wg_ref, ps_ref, wo_ref, *rest, n_seq):
    tm, d = uc_ref.shape
    n_grp = len(POOL_WINDOWS)
    ch = d // n_grp
    halo = POOL_HALO
    tiles = _sub_tiles(tm)
    lhs = []
    for r in tiles:
        n_r = r.stop - r.start
        pos0 = (pl.program_id(0) * tm + r.start) % n_seq
        e_pos = pos0 - halo + lax.broadcasted_iota(jnp.int32, (n_r + 2 * halo, 1), 0)
        e_ok = (e_pos >= 0) & (e_pos < n_seq)
        t_pos = pos0 + lax.broadcasted_iota(jnp.int32, (n_r, 1), 0)
        z = []
        for gi, w in enumerate(POOL_WINDOWS):
            cs = slice(gi * ch, (gi + 1) * ch)
            u = uc_ref[r, cs]
            before = up_ref[:, cs] if r.start == 0 else uc_ref[r.start - halo:r.start, cs]
            after = un_ref[:, cs] if r.stop == tm else uc_ref[r.stop:r.stop + halo, cs]
            ext = jnp.where(e_ok, jnp.concatenate([before, u, after], axis=0), 0.0)
            a, span = ext, 1
            while span < w:
                a = a[:a.shape[0] - span] + a[span:]
                span *= 2
            start = halo - w // 2
            win = a[start:start + n_r]
            cnt = (jnp.minimum(t_pos + (w - w // 2), n_seq) - jnp.maximum(t_pos - w // 2, 0)).astype(F32)
            mixed = (win / cnt - u).astype(BF16)
            z.append((_dot(mixed, wg_ref[gi]) * ps_ref[:, cs]).astype(BF16))
        lhs.append(jnp.concatenate(z, axis=-1))
    ys = [_dot(a, wo_ref[...]) for a in lhs]
    for r, y in zip(tiles, ys):
        _post_mix(y, *rest, r)


def _pool_out(u, w_grp, scale, w_out, x, g1, sc2, sh2, lng, lnb, wr_hi, wr_lo, rb, n_seq, tm=ROW_TILE):
    t, d = x.shape
    hb = tm // POOL_HALO
    n_hb = t // POOL_HALO
    post_in, post_out = _post_specs(d, tm, n_seq)
    return pl.pallas_call(
        functools.partial(_pool_out_kernel, n_seq=n_seq),
        grid=(t // tm,),
        in_specs=[
            pl.BlockSpec((POOL_HALO, d), lambda i: (jnp.maximum(i * hb - 1, 0), 0)),
            pl.BlockSpec((tm, d), lambda i: (i, 0)),
            pl.BlockSpec((POOL_HALO, d), lambda i: (jnp.minimum((i + 1) * hb, n_hb - 1), 0)),
            _resident(w_grp.shape),
            pl.BlockSpec((1, d), lambda i: (0, 0)),
            _resident(w_out.shape),
        ] + post_in,
        out_specs=post_out,
        out_shape=_post_out_shapes(t, d),
        compiler_params=_params("parallel"),
        name="pool_out",
    )(u, u, u, w_grp, scale, w_out, x, g1, sc2, sh2, lng, lnb, wr_hi, wr_lo, rb)


def _moe_kernel(be_ref, nv_ref, first_ref, ws_ref, nxt_ref, idx_ref, idxn_ref, idxp_ref, tok_hbm, w1_hbm, w3_hbm, w2_hbm,
                y_hbm, xbuf, ybuf, xb_ref, wf1, wf3, wf2, w1b, w3b, w2b, gsem, ssem, wsem, *, n_tok, layer, n_blocks):
    i = pl.program_id(0)
    used = nv_ref[jnp.minimum(i, n_blocks - 1)] > 0
    used = used & (i < n_blocks)
    prev_used = (i > 0) & (nv_ref[jnp.maximum(i - 1, 0)] > 0)
    xs = i % 2

    def weight_copies(e, ws):
        return (pltpu.make_async_copy(w1_hbm.at[layer, e], wf1.at[ws], wsem.at[ws]),
                pltpu.make_async_copy(w3_hbm.at[layer, e], wf3.at[ws], wsem.at[ws]),
                pltpu.make_async_copy(w2_hbm.at[layer, e], wf2.at[ws], wsem.at[ws]))

    def gather_start(idx, slot):
        for r in range(MOE_ROWS):
            tok = idx[0, 0, r] & (n_tok - 1)
            pltpu.make_async_copy(tok_hbm.at[tok], xbuf.at[slot, r], gsem.at[slot]).start()

    def gather_wait(slot):
        pltpu.make_async_copy(tok_hbm.at[pl.ds(0, MOE_ROWS)], xbuf.at[slot], gsem.at[slot]).wait()

    def scatter_start(idx, slot):
        for r in range(MOE_ROWS):
            pltpu.make_async_copy(ybuf.at[slot, r], y_hbm.at[idx[0, 0, r]], ssem.at[slot]).start(priority=r % 2)

    def scatter_wait(slot):
        pltpu.make_async_copy(ybuf.at[slot], y_hbm.at[pl.ds(0, MOE_ROWS)], ssem.at[slot]).wait()

    @pl.when(i == 0)
    def _():
        xbuf[...] = jnp.zeros_like(xbuf)
        ybuf[...] = jnp.zeros_like(ybuf)
        spare0 = pltpu.make_async_copy(
            ybuf.at[0], y_hbm.at[pl.ds(MOE_TOP_K * n_tok, MOE_ROWS)], ssem.at[0])
        spare0.start()
        for cp in weight_copies(be_ref[0], 0):
            cp.start(priority=WEIGHT_DMA_PRIORITY)
        gather_start(idx_ref, 0)

    @pl.when(used)
    def _():
        ws = ws_ref[i]

        @pl.when(first_ref[i] == 1)
        def _():
            for cp in weight_copies(be_ref[i], ws):
                cp.wait()
            nxt = nxt_ref[i]

            @pl.when(nxt >= 0)
            def _():
                for cp in weight_copies(nxt, 1 - ws):
                    cp.start(priority=WEIGHT_DMA_PRIORITY)

            w1b[...] = wf1[ws].astype(BF16)
            w3b[...] = wf3[ws].astype(BF16)
            w2b[...] = wf2[ws].astype(BF16)

        gather_wait(xs)
        xb_ref[...] = _rows_from_tiles(xbuf[xs].astype(F32)).astype(BF16)
        gather_start(idxn_ref, 1 - xs)
        scatter_start(idxp_ref, 1 - xs)
        xb = xb_ref[...]
        h = (_silu(_dot(xb, w1b[...])) * _dot(xb, w3b[...])).astype(BF16)
        y = _tiles_from_rows(_dot(h, w2b[...])).astype(BF16)
        scatter_wait(xs)
        ybuf[xs] = y

    @pl.when(jnp.logical_not(used) & prev_used)
    def _():
        gather_wait(xs)
        scatter_start(idxp_ref, 1 - xs)
        scatter_wait(1 - xs)
        scatter_wait(xs)


def _moe_dispatch(eid_t, n_blocks):
    n_tok = eid_t.shape[1]
    n_slots = (n_blocks + 2) * MOE_ROWS
    slot_rows = -(-n_slots // SMEM_1D_TILE)
    assert n_blocks <= LANES and n_tok % SMEM_1D_TILE == 0 and SMEM_1D_TILE % MOE_ROWS == 0
    slot, meta = pl.pallas_call(
        functools.partial(_dispatch_kernel, n_tok=n_tok),
        in_specs=[pl.BlockSpec(memory_space=pltpu.VMEM)],
        out_specs=[pl.BlockSpec(memory_space=pltpu.SMEM), pl.BlockSpec(memory_space=pltpu.VMEM)],
        out_shape=[jax.ShapeDtypeStruct((slot_rows * SMEM_1D_TILE,), jnp.int32),
                   jax.ShapeDtypeStruct((MOE_ID_ROWS, LANES), jnp.int32)],
        scratch_shapes=[pltpu.VMEM((MOE_TOP_K * n_tok,), jnp.int32),
                        pltpu.VMEM((slot_rows * SMEM_1D_TILE,), jnp.int32),
                        pltpu.SMEM((MOE_TOP_K * n_tok,), jnp.int32),
                        pltpu.SemaphoreType.DMA(())],
        name="moe_dispatch",
    )(eid_t)
    return (slot[:n_slots].reshape(n_blocks + 2, 1, MOE_ROWS),) + tuple(meta[r, :n_blocks] for r in range(5))


def _dispatch_kernel(eid_ref, slot_ref, meta_ref, dest_vmem, init_vmem, dest_smem, sem, *, n_tok):
    tile = MOE_ROWS
    n_tiles = n_tok // tile
    sub = lax.broadcasted_iota(jnp.int32, (N_EXPERTS, tile), 0)
    si = lax.broadcasted_iota(jnp.int32, (tile, tile), 0)
    ti = lax.broadcasted_iota(jnp.int32, (tile, tile), 1)
    before = jnp.where(si < ti, 1.0, 0.0).astype(BF16)

    def one_hots(j):
        ids = eid_ref[:, j * tile:(j + 1) * tile]
        return [jnp.where(sub == ids[k:k + 1], 1.0, 0.0) for k in range(MOE_TOP_K)]

    carry = jnp.zeros((N_EXPERTS, 1), F32)
    ranks = []
    for j in range(n_tiles):
        oh = one_hots(j)
        both = oh[0] + oh[1]
        seen = carry + _dot(both.astype(BF16), before)
        ranks.append([jnp.sum(seen * o, axis=0, keepdims=True) for o in oh])
        carry = carry + jnp.sum(both, axis=1, keepdims=True)

    counts = carry
    nblk = jnp.floor((counts + float(MOE_ROWS - 1)) * (1.0 / MOE_ROWS))
    ei = lax.broadcasted_iota(jnp.int32, (N_EXPERTS, N_EXPERTS), 0)
    ej = lax.broadcasted_iota(jnp.int32, (N_EXPERTS, N_EXPERTS), 1)
    lower = jnp.where(ej < ei, 1.0, 0.0).astype(BF16)
    first_blk = _dot(lower, jnp.broadcast_to(nblk, (N_EXPERTS, LANES)).astype(BF16))[:, 0:1]
    first_slot = first_blk * float(MOE_ROWS)

    per_row = SMEM_1D_TILE // tile
    for k in range(MOE_TOP_K):
        for q in range(n_tiles // per_row):
            parts = []
            for j in range(q * per_row, (q + 1) * per_row):
                parts.append(jnp.sum(first_slot * one_hots(j)[k], axis=0, keepdims=True) + ranks[j][k])
            dest = jnp.concatenate(parts, axis=1).astype(jnp.int32)
            dest_vmem[pl.ds(k * n_tok + q * SMEM_1D_TILE, SMEM_1D_TILE)] = dest.reshape(SMEM_1D_TILE)

    lane = lax.broadcasted_iota(jnp.int32, (1, SMEM_1D_TILE), 1)
    for q in range(init_vmem.shape[0] // SMEM_1D_TILE):
        pos = q * SMEM_1D_TILE + lane
        spare = MOE_TOP_K * n_tok + ((pos // MOE_ROWS + 1) % 2) * MOE_ROWS + pos % MOE_ROWS
        init_vmem[pl.ds(q * SMEM_1D_TILE, SMEM_1D_TILE)] = spare.reshape(SMEM_1D_TILE)
    copies = [pltpu.make_async_copy(dest_vmem, dest_smem, sem), pltpu.make_async_copy(init_vmem, slot_ref, sem)]
    for cp in copies:
        cp.start()
    for cp in copies:
        cp.wait()

    def place(t, carry_):
        for k in range(MOE_TOP_K):
            slot_ref[dest_smem[k * n_tok + t] + MOE_ROWS] = k * n_tok + t
        return carry_

    lax.fori_loop(0, n_tok, place, 0, unroll=8)

    b = lax.broadcasted_iota(jnp.int32, (N_EXPERTS, LANES), 1).astype(F32)
    e_col = lax.broadcasted_iota(jnp.int32, (N_EXPERTS, LANES), 0).astype(F32)
    b_row = b[0:1]
    be = jnp.minimum(jnp.sum(jnp.where(first_blk + nblk <= b, 1.0, 0.0), axis=0, keepdims=True), N_EXPERTS - 1.0)
    mine = e_col == be
    cnt_b = jnp.sum(jnp.where(mine, counts, 0.0), axis=0, keepdims=True)
    start_b = jnp.sum(jnp.where(mine, first_blk, 0.0), axis=0, keepdims=True)
    nv = jnp.clip(cnt_b - (b_row - start_b) * MOE_ROWS, 0.0, float(MOE_ROWS))
    nv = jnp.where(b_row < jnp.sum(nblk, axis=0, keepdims=True), nv, 0.0)
    first = jnp.where((nv > 0) & ((b_row == 0) | (be != pltpu.roll(be, 1, 1))), 1.0, 0.0)
    li = lax.broadcasted_iota(jnp.int32, (LANES, LANES), 0)
    lj = lax.broadcasted_iota(jnp.int32, (LANES, LANES), 1)
    upto = jnp.where(li <= lj, 1.0, 0.0).astype(BF16)
    run = _dot(jnp.broadcast_to(first, (MOE_ID_ROWS, LANES)).astype(BF16), upto)[0:1] - 1.0
    ws = run - 2.0 * jnp.floor(run * 0.5)
    later = jnp.min(jnp.where((e_col > be) & (counts > 0), e_col, float(LANES)), axis=0, keepdims=True)
    nxt = jnp.where(later >= float(N_EXPERTS), -1.0, later)
    rows = [be, nv, first, ws, nxt] + [jnp.zeros_like(be)] * (MOE_ID_ROWS - 5)
    meta_ref[...] = jnp.concatenate(rows, axis=0).astype(jnp.int32)


def _moe_experts(tok, eid, w1, w3, w2, layer):
    n_tok, n_sub, _ = tok.shape
    d = n_sub * LANES
    assert n_tok & (n_tok - 1) == 0
    ff = w1.shape[3]
    n_assign = n_tok * MOE_TOP_K
    n_blocks = -(-(n_assign + N_EXPERTS * (MOE_ROWS - 1)) // MOE_ROWS)
    slot, be, nv, first, ws, nxt = _moe_dispatch(eid, n_blocks)
    grid_spec = pltpu.PrefetchScalarGridSpec(
        num_scalar_prefetch=5,
        grid=(n_blocks + 1,),
        in_specs=[
            pl.BlockSpec((1, 1, MOE_ROWS), lambda i, *_: (i + 1, 0, 0), memory_space=pltpu.SMEM),
            pl.BlockSpec((1, 1, MOE_ROWS), lambda i, *_: (jnp.minimum(i + 2, n_blocks + 1), 0, 0),
                         memory_space=pltpu.SMEM),
            pl.BlockSpec((1, 1, MOE_ROWS), lambda i, *_: (i, 0, 0), memory_space=pltpu.SMEM),
            pl.BlockSpec(memory_space=pl.ANY),
            pl.BlockSpec(memory_space=pl.ANY),
            pl.BlockSpec(memory_space=pl.ANY),
            pl.BlockSpec(memory_space=pl.ANY),
        ],
        out_specs=pl.BlockSpec(memory_space=pl.ANY),
        scratch_shapes=[
            pltpu.VMEM((2, MOE_ROWS, n_sub, LANES), BF16), pltpu.VMEM((2, MOE_ROWS, n_sub, LANES), BF16),
            pltpu.VMEM((MOE_ROWS, d), BF16),
            pltpu.VMEM((2, d, ff), F32), pltpu.VMEM((2, d, ff), F32), pltpu.VMEM((2, ff, d), F32),
            pltpu.VMEM((d, ff), BF16), pltpu.VMEM((d, ff), BF16), pltpu.VMEM((ff, d), BF16),
            pltpu.SemaphoreType.DMA((2,)), pltpu.SemaphoreType.DMA((2,)), pltpu.SemaphoreType.DMA((2,)),
        ],
    )
    return pl.pallas_call(
        functools.partial(_moe_kernel, n_tok=n_tok, layer=layer, n_blocks=n_blocks),
        grid_spec=grid_spec,
        out_shape=jax.ShapeDtypeStruct((MOE_TOP_K * n_tok + 2 * MOE_ROWS, n_sub, LANES), BF16),
        compiler_params=pltpu.CompilerParams(dimension_semantics=("arbitrary",),
                                             vmem_limit_bytes=MOE_VMEM_LIMIT_BYTES),
        name="moe_experts",
    )(be, nv, first, ws, nxt, slot, slot, slot, tok, w1, w3, w2)


def _rope_tables(n_seq):
    half = HEAD_DIM // 2
    n_freq = half // 2
    t = jnp.arange(n_seq)
    row = (t // GRID_W).astype(F32)
    col = (t % GRID_W).astype(F32)
    inv_freq = ROPE_BASE ** (-jnp.arange(n_freq, dtype=F32) / n_freq)
    ang_r = row[:, None] * inv_freq[None, :]
    ang_c = col[:, None] * inv_freq[None, :]
    cos = jnp.concatenate([jnp.cos(ang_r)] * 2 + [jnp.cos(ang_c)] * 2, axis=-1)
    sin = jnp.concatenate([-jnp.sin(ang_r), jnp.sin(ang_r), -jnp.sin(ang_c), jnp.sin(ang_c)], axis=-1)
    return cos, sin


def _router_weights(w_g, b_g, w_e, b_e):
    d = w_g.shape[0]
    n = w_g.shape[1] + w_e.shape[1]
    wr = jnp.concatenate([w_g, w_e, jnp.zeros((d, LANES - n), F32)], axis=1)
    rb = jnp.concatenate([b_g, b_e, jnp.zeros((LANES - n,), F32)]).reshape(1, LANES)
    hi = wr.astype(BF16)
    lo = (wr - hi.astype(F32)).astype(BF16)
    return hi, lo, rb


def kernel(x, c, ctx, c_ctx, ada_w, ada_b, ln_g, ln_b, mix_w_in, att_sink, hg_lb, hg_norm_g, mix_w_out, pool_w_in, pool_w_grp, pool_scale, pool_w_out, rt_group_w, rt_group_b, rt_expert_w, rt_expert_b, moe_w1, moe_w3, moe_w2):
    b, n, d = x.shape
    n_ctx = ctx.shape[1]
    t = b * n
    xf = x.reshape(t, d)
    ctxf = ctx.reshape(b * n_ctx, d)

    cond = jnp.concatenate([c, c_ctx[None, :], jnp.zeros((8 - b - 1, d), F32)], axis=0)
    mod = _ada_mod(cond, ada_w, ada_b)

    def chunk(l, j, rows=slice(0, b)):
        return mod[l, rows, j * d:(j + 1) * d][:, None, :]

    w_in = mix_w_in[0].astype(BF16)
    cos, sin = _rope_tables(n)
    q_w, kv_w = ATT_HEADS * HEAD_DIM, ATT_KV_HEADS * HEAD_DIM
    n_att = q_w + 2 * kv_w
    hk = HG_HEADS * HG_KEY
    assert n_att == 2 * PROJ_TN and hk == PROJ_TN
    a_lat, f_lat = _mod_matmul(xf, chunk(0, 1), chunk(0, 0), w_in, cos, sin,
                               lambda j: jnp.where(j < 3, j, jnp.where(j < 5, j + 2, j - 2)), n_att + 3 * hk, 2 * hk,
                               n_q=q_w, n_rope=q_w + kv_w, n_seq=n, tm=1024, tn=PROJ_TN)
    ctx_rows = slice(b, b + 1)
    a_ctx, f_ctx = _mod_matmul(ctxf, chunk(0, 1, ctx_rows), chunk(0, 0, ctx_rows), w_in, cos, sin,
                               lambda j: jnp.where(j < 1, 1, jnp.where(j < 2, 5, j + 1)), 2 * kv_w + hk, 2 * hk,
                               n_q=0, n_rope=0, n_seq=n, tm=b * n_ctx, tn=PROJ_TN)
    att = _window_attention(a_lat, a_ctx, att_sink[0], b, n, n_ctx)
    o_f = _hgrn2_scan(a_lat, f_lat, a_ctx, f_ctx, hg_lb, b, n, n_ctx, rev=False)
    o_b = _hgrn2_scan(a_lat, f_lat, a_ctx, f_ctx, hg_lb, b, n, n_ctx, rev=True)
    wr_hi, wr_lo, rb = _router_weights(rt_group_w[0], rt_group_b[0], rt_expert_w[0], rt_expert_b[0])
    x1, tok, eid, gate = _even_out(
        att, o_f, o_b, a_lat, hg_norm_g[0][None, :], mix_w_out[0].astype(BF16), xf,
        chunk(0, 2), chunk(0, 4), chunk(0, 3), ln_g[0, 0][None, :], ln_b[0, 0][None, :], wr_hi, wr_lo, rb, n)
    y2 = _moe_experts(tok, eid, moe_w1, moe_w3, moe_w2, 0)

    x2, u = _combine_call(x1, y2, gate, chunk(0, 5), ln_g[0, 1][None, :], ln_b[0, 1][None, :], n,
                          proj=(chunk(1, 1), chunk(1, 0), pool_w_in[0].astype(BF16)))
    wr_hi, wr_lo, rb = _router_weights(rt_group_w[1], rt_group_b[1], rt_expert_w[1], rt_expert_b[1])
    x3, tok, eid, gate = _pool_out(
        u, pool_w_grp[0].astype(BF16), pool_scale[0][None, :], pool_w_out[0].astype(BF16), x2,
        chunk(1, 2), chunk(1, 4), chunk(1, 3), ln_g[1, 0][None, :], ln_b[1, 0][None, :], wr_hi, wr_lo, rb, n)
    y2 = _moe_experts(tok, eid, moe_w1, moe_w3, moe_w2, 1)
    out = _combine_call(x3, y2, gate, chunk(1, 5), ln_g[1, 1][None, :], ln_b[1, 1][None, :], n)
    return out.reshape(b, n, d)
```

```python
import functools

import jax
import jax.numpy as jnp
from jax import lax
from jax.experimental import pallas as pl
from jax.experimental.pallas import tpu as pltpu

F32 = jnp.float32
BF16 = jnp.bfloat16

LANES = 128
MXU_TILE = 256
VMEM_LIMIT_BYTES = 56 * 1024 * 1024
MOE_VMEM_LIMIT_BYTES = 60 * 1024 * 1024

GRID_W = 64
ATT_HEADS = 8
ATT_KV_HEADS = 4
ATT_GROUP = ATT_HEADS // ATT_KV_HEADS
HEAD_DIM = 128
WINDOW = 128
ATT_BLOCK = 128
ROPE_BASE = 10000.0
HG_HEADS = 8
HG_KEY = 128
HG_CHUNK = 64
HG_STEP_CHUNKS = 4
HG_SUB = 16
HG_FAST_RANGE = 80.0
NORM_EPS = 1e-6
POOL_WINDOWS = (2, 4, 8, 16)
POOL_HALO = 8
MOE_GROUPS = 4
MOE_EXPERTS_PER_GROUP = 8
N_EXPERTS = MOE_GROUPS * MOE_EXPERTS_PER_GROUP
MOE_TOP_K = 2
MOE_ROWS = 256
ROW_TILE = 512
SUB_ROWS = 256
PROJ_TN = 1024
MOE_ID_ROWS = 8
SMEM_1D_TILE = 1024
WEIGHT_DMA_PRIORITY = 1
LN_EPS = 1e-5
DEPTH = 2
DEEPNORM_ALPHA = (2 * DEPTH) ** 0.25


def _dot(a, b):
    return jnp.dot(a, b, preferred_element_type=F32)


def _dot_nt(a, b):
    return lax.dot_general(a, b, (((1,), (1,)), ((), ())), preferred_element_type=F32)


def _dot_tn(a, b):
    return lax.dot_general(a, b, (((0,), (0,)), ((), ())), preferred_element_type=F32)


def _sigmoid(x):
    return 1.0 / (1.0 + jnp.exp(-x))


def _silu(x):
    return x * _sigmoid(x)


def _params(*sem):
    return pltpu.CompilerParams(dimension_semantics=sem, vmem_limit_bytes=VMEM_LIMIT_BYTES)


def _tiles_from_rows(x):
    n = x.shape[1] // LANES
    return jnp.swapaxes(jnp.stack([x[:, s * LANES:(s + 1) * LANES] for s in range(n)], axis=0), 0, 1)


def _rows_from_tiles(x3):
    xt = jnp.swapaxes(x3, 0, 1)
    return jnp.concatenate([xt[s] for s in range(xt.shape[0])], axis=-1)


def _layer_norm(z, g, b):
    mu = jnp.mean(z, axis=-1, keepdims=True)
    zc = z - mu
    var = jnp.mean(zc * zc, axis=-1, keepdims=True)
    return zc * lax.rsqrt(var + LN_EPS) * g + b


def _ada_kernel(s_ref, w_ref, b_ref, o_ref):
    s = _silu(s_ref[...]).astype(BF16)
    o_ref[0] = _dot(s, w_ref[0].astype(BF16)) + b_ref[0]


def _ada_mod(s, ada_w, ada_b, tn=1024):
    n_l, d, n = ada_w.shape
    return pl.pallas_call(
        _ada_kernel,
        grid=(n_l, n // tn),
        in_specs=[
            pl.BlockSpec((8, d), lambda l, j: (0, 0)),
            pl.BlockSpec((1, d, tn), lambda l, j: (l, 0, j)),
            pl.BlockSpec((1, 1, tn), lambda l, j: (l, 0, j)),
        ],
        out_specs=pl.BlockSpec((1, 8, tn), lambda l, j: (l, 0, j)),
        out_shape=jax.ShapeDtypeStruct((n_l, 8, n), F32),
        compiler_params=_params("parallel", "parallel"),
        name="ada_mod",
    )(s, ada_w, ada_b.reshape(n_l, 1, n))


def _rope(t, cos, sin_signed, first_half):
    partner = jnp.where(first_half, pltpu.roll(t, 96, 1), pltpu.roll(t, 32, 1))
    return t * cos + partner * sin_signed


def _modmm_kernel(x_ref, sc_ref, sh_ref, w_ref, cos_ref, sin_ref, oa_ref, ob_ref, xs_ref, *,
                  n_q, n_rope, n_a_tiles):
    j = pl.program_id(1)
    tm, tn = x_ref.shape[0], w_ref.shape[1]
    halves = [slice(0, tm // 2), slice(tm // 2, tm)]

    def dots():
        return [_dot(xs_ref[r], w_ref[...]) for r in halves]

    def store_bf16(jt, r, acc):
        lane = lax.broadcasted_iota(jnp.int32, (1, HEAD_DIM), 1)
        first_half = (lane % 64) < 32
        for h in range(tn // HEAD_DIM):
            sl = slice(h * HEAD_DIM, (h + 1) * HEAD_DIM)
            col = jt * tn + h * HEAD_DIM
            if col < n_rope:
                scale = HEAD_DIM ** -0.5 if col < n_q else 1.0
                oa_ref[r, sl] = _rope(acc[:, sl], cos_ref[r] * scale, sin_ref[r] * scale, first_half).astype(BF16)
            else:
                oa_ref[r, sl] = acc[:, sl].astype(BF16)

    for jt in range(n_a_tiles):
        @pl.when(j == jt)
        def _(jt=jt):
            if jt == 0:
                for r in halves:
                    xs_ref[r] = (x_ref[r] * (1.0 + sc_ref[0]) + sh_ref[0]).astype(BF16)
            for r, acc in zip(halves, dots()):
                store_bf16(jt, r, acc)

    @pl.when(j >= n_a_tiles)
    def _():
        for r, acc in zip(halves, dots()):
            ob_ref[r] = acc


def _mod_matmul(x, sc, sh, w, cos, sin, col_map, n_a, n_b, n_q, n_rope, n_seq, tm, tn):
    m, k = x.shape
    rows_per_mod = m // sc.shape[0]
    ta, tb = n_a // tn, n_b // tn
    tab = pl.BlockSpec((tm, HEAD_DIM), lambda i, j: ((i * tm % n_seq) // tm, 0))
    return pl.pallas_call(
        functools.partial(_modmm_kernel, n_q=n_q, n_rope=n_rope, n_a_tiles=ta),
        grid=(m // tm, ta + tb),
        in_specs=[
            pl.BlockSpec((tm, k), lambda i, j: (i, 0)),
            pl.BlockSpec((1, 1, k), lambda i, j: (i * tm // rows_per_mod, 0, 0)),
            pl.BlockSpec((1, 1, k), lambda i, j: (i * tm // rows_per_mod, 0, 0)),
            pl.BlockSpec((k, tn), lambda i, j: (0, col_map(j))),
            tab, tab,
        ],
        out_specs=[pl.BlockSpec((tm, tn), lambda i, j: (i, jnp.minimum(j, ta - 1))),
                   pl.BlockSpec((tm, tn), lambda i, j: (i, jnp.maximum(j - ta, 0)))],
        out_shape=[jax.ShapeDtypeStruct((m, n_a), BF16), jax.ShapeDtypeStruct((m, n_b), F32)],
        scratch_shapes=[pltpu.VMEM((tm, k), BF16)],
        compiler_params=_params("parallel", "arbitrary"),
        name="mod_matmul",
    )(x, sc, sh, w, cos, sin)


def _attn_kernel(sink_ref, q_ref, kp_ref, kc_ref, kn_ref, vp_ref, vc_ref, vn_ref, kx_ref, vx_ref,
                 mp_ref, mn_ref, o_ref, *, n_blocks):
    s_idx = pl.program_id(1)
    blk = ATT_BLOCK
    row1 = lax.broadcasted_iota(jnp.int32, (ATT_GROUP * blk, 1), 0)
    lo, hi = slice(0, blk), slice(blk, 2 * blk)
    subs = [(lo, (kp_ref, vp_ref, lo), (kc_ref, vc_ref, lo), (kc_ref, vc_ref, hi), s_idx > 0, True),
            (hi, (kc_ref, vc_ref, lo), (kc_ref, vc_ref, hi), (kn_ref, vn_ref, lo), True, s_idx < n_blocks // 2 - 1)]
    jobs = [(sub, h) for sub in subs for h in range(ATT_KV_HEADS)]

    def kv(h):
        return slice(h * HEAD_DIM, (h + 1) * HEAD_DIM)

    def gather(sub, h, which):
        blocks = [t[which][t[2], kv(h)] for t in sub[1:4]]
        return jnp.concatenate(blocks + [(kx_ref, vx_ref)[which][:, kv(h)]], axis=0)

    scores = []
    for sub, h in jobs:
        q2 = jnp.concatenate([q_ref[sub[0], (ATT_GROUP * h + g) * HEAD_DIM:(ATT_GROUP * h + g + 1) * HEAD_DIM]
                              for g in range(ATT_GROUP)], axis=0)
        scores.append(_dot_nt(q2, gather(sub, h, 0)))
    probs, dens = [], []
    for (sub, h), s in zip(jobs, scores):
        has_prev, has_next = sub[4], sub[5]
        parts = [jnp.where(has_prev, s[:, :blk] + mp_ref[...], -jnp.inf), s[:, blk:2 * blk],
                 jnp.where(has_next, s[:, 2 * blk:3 * blk] + mn_ref[...], -jnp.inf)]
        parts += [s[:, c:c + blk] for c in range(3 * blk, s.shape[1], blk)]
        sink = jnp.where(row1 < blk, sink_ref[ATT_GROUP * h], sink_ref[ATT_GROUP * h + 1])
        m = jnp.maximum(jnp.max(functools.reduce(jnp.maximum, parts), axis=-1, keepdims=True), sink)
        p = [jnp.exp(x - m) for x in parts]
        dens.append(jnp.sum(functools.reduce(jnp.add, p), axis=-1, keepdims=True) + jnp.exp(sink - m))
        probs.append(jnp.concatenate(p, axis=-1).astype(BF16))
    for (sub, h), p, den in zip(jobs, probs, dens):
        o = _dot(p, gather(sub, h, 1)) / den
        for g in range(ATT_GROUP):
            col = (ATT_GROUP * h + g) * HEAD_DIM
            o_ref[sub[0], col:col + HEAD_DIM] = o[g * blk:(g + 1) * blk].astype(o_ref.dtype)


def _window_attention(qkv, kv_ctx, sink, batch, n_seq, n_ctx):
    assert ATT_GROUP == 2 and WINDOW == ATT_BLOCK
    nb = n_seq // ATT_BLOCK
    qw, kw = ATT_HEADS * HEAD_DIM, ATT_KV_HEADS * HEAD_DIM
    kcol, vcol = qw // kw, qw // kw + 1

    assert nb % 2 == 0
    pair = 2 * ATT_BLOCK

    def before(b, s):
        return b * nb + jnp.maximum(2 * s - 1, 0)

    def after(b, s):
        return b * nb + jnp.minimum(2 * s + 2, nb - 1)

    def own(b, s):
        return b * (nb // 2) + s

    def band(col):
        return [pl.BlockSpec((ATT_BLOCK, kw), lambda b, s: (before(b, s), col)),
                pl.BlockSpec((pair, kw), lambda b, s: (own(b, s), col)),
                pl.BlockSpec((ATT_BLOCK, kw), lambda b, s: (after(b, s), col))]

    kspec, vspec = band(kcol), band(vcol)
    r = jnp.arange(ATT_GROUP * ATT_BLOCK)[:, None] % ATT_BLOCK
    c = jnp.arange(ATT_BLOCK)[None, :]
    mask_prev = jnp.where(c >= r, 0.0, -jnp.inf).astype(F32)
    mask_next = jnp.where(c <= r, 0.0, -jnp.inf).astype(F32)
    mspec = pl.BlockSpec(mask_prev.shape, lambda b, s: (0, 0))
    return pl.pallas_call(
        functools.partial(_attn_kernel, n_blocks=nb),
        grid=(batch, nb // 2),
        in_specs=[pl.BlockSpec(memory_space=pltpu.SMEM),
                  pl.BlockSpec((pair, qw), lambda b, s: (own(b, s), 0))]
        + kspec + vspec
        + [pl.BlockSpec((n_ctx, kw), lambda b, s: (b, 0)), pl.BlockSpec((n_ctx, kw), lambda b, s: (b, 1)),
           mspec, mspec],
        out_specs=pl.BlockSpec((pair, qw), lambda b, s: (own(b, s), 0)),
        out_shape=jax.ShapeDtypeStruct((batch * n_seq, qw), BF16),
        compiler_params=_params("parallel", "parallel"),
        name="window_attention",
    )(sink, qkv, qkv, qkv, qkv, qkv, qkv, qkv, kv_ctx, kv_ctx, mask_prev, mask_next)


def _gla_step(zf, q_raw, v, lb, st_ref, o_ref, rev):
    c_len = HG_CHUNK
    n_rows = zf.shape[0]
    n_sub = n_rows // c_len
    shift = c_len.bit_length() - 1
    order = range(n_sub - 1, -1, -1) if rev else range(n_sub)

    def head(h):
        return slice(h * HG_KEY, (h + 1) * HG_KEY)

    def chunk(i):
        return slice(i * c_len, (i + 1) * c_len)

    def seen(n):
        ri = lax.broadcasted_iota(jnp.int32, (n, n), 0)
        ci = lax.broadcasted_iota(jnp.int32, (n, n), 1)
        return ((ri >> shift) == (ci >> shift)) & ((ci >= ri) if rev else (ci <= ri))

    f = lb + (1.0 - lb) * _sigmoid(zf)
    k = 1.0 - f
    g = jnp.log(f)
    tri = jnp.where(seen(n_rows), 1.0, 0.0).astype(BF16)
    g1 = g.astype(BF16)
    r1 = g - g1.astype(F32)
    g2 = r1.astype(BF16)
    g3 = (r1 - g2.astype(F32)).astype(BF16)
    c = _dot(tri, g1) + _dot(tri, g2) + _dot(tri, g3)
    c_end = [c[i * c_len:i * c_len + 1] if rev else c[(i + 1) * c_len - 1:(i + 1) * c_len] for i in range(n_sub)]
    c_end_rows = jnp.concatenate([jnp.broadcast_to(ce, (c_len, ce.shape[1])) for ce in c_end], axis=0)
    k_end = (k * jnp.exp(c_end_rows - c)).astype(BF16)
    dec = [jnp.exp(ce) for ce in c_end]
    vb = v.astype(BF16)

    def advance(states, i):
        new = []
        for p in range(0, HG_HEADS, 2):
            lanes = slice(p * HG_KEY, (p + 2) * HG_KEY)
            inc = _dot_tn(vb[chunk(i), lanes], k_end[chunk(i), lanes])
            for j in range(2):
                blk = slice(j * HG_KEY, (j + 1) * HG_KEY)
                new.append(states[p + j] * dec[i][:, head(p + j)] + inc[blk, blk])
        return new

    if o_ref is None:
        states = [st_ref[h] for h in range(HG_HEADS)]
        for i in order:
            states = advance(states, i)
        for h in range(HG_HEADS):
            st_ref[h] = states[h]
        return

    q = _silu(q_raw.astype(F32))
    q_in = (q * jnp.exp(c)).astype(BF16)
    lowest = functools.reduce(jnp.minimum, c_end)
    in_range = jnp.min(lowest) >= -HG_FAST_RANGE

    def stack(x, i, h0, n):
        return jnp.concatenate([x[chunk(i), head(h)] for h in range(h0, h0 + n)], axis=0)

    @pl.when(in_range)
    def _():
        k_in = (k * jnp.exp(-c)).astype(BF16)
        n_qk = MXU_TILE // c_len
        same = seen(n_qk * c_len)
        groups = [(i, h0) for i in order for h0 in range(0, HG_HEADS, n_qk)]
        raw = [_dot_nt(stack(q_in, i, h0, n_qk), stack(k_in, i, h0, n_qk)) for i, h0 in groups]
        masked = [jnp.where(same, s, 0.0).astype(BF16) for s in raw]
        intra = {}
        for (i, h0), sc in zip(groups, masked):
            pv = _dot(sc, stack(vb, i, h0, n_qk))
            for j in range(n_qk):
                intra[i, h0 + j] = pv[j * c_len:(j + 1) * c_len]
        states = [st_ref[h] for h in range(HG_HEADS)]
        for i in order:
            out = []
            for p in range(0, HG_HEADS, 2):
                st_pair = jnp.concatenate([states[p].astype(BF16), states[p + 1].astype(BF16)], axis=0)
                inter = _dot_nt(stack(q_in, i, p, 2), st_pair)
                for j in range(2):
                    out.append(inter[j * c_len:(j + 1) * c_len, j * HG_KEY:(j + 1) * HG_KEY] + intra[i, p + j])
            o_ref[chunk(i), :] = jnp.concatenate(out, axis=1)
            states = advance(states, i)
        for h in range(HG_HEADS):
            st_ref[h] = states[h]

    @pl.when(jnp.logical_not(in_range))
    def _():
        for i in order:
            r = chunk(i)
            _gla_intra_exact(q[r], k[r], v[r].astype(F32), vb[r], c[r], q_in[r], st_ref, o_ref, r, rev)
            states = advance([st_ref[h] for h in range(HG_HEADS)], i)
            for h in range(HG_HEADS):
                st_ref[h] = states[h]


def _gla_intra_exact(q, k, v, vb, c, q_in, st_ref, o_ref, rows, rev):
    c_len = q.shape[0]
    pairs = []
    size = c_len // 2
    while size >= HG_SUB:
        for lo in range(0, c_len, 2 * size):
            pairs.append((lo, lo + size, lo + 2 * size))
        size //= 2
    scaled = []
    for lo, mid, hi in pairs:
        if rev:
            late, early, bnd = slice(lo, mid), slice(mid, hi), mid
        else:
            late, early, bnd = slice(mid, hi), slice(lo, mid), mid - 1
        cb = c[bnd:bnd + 1]
        q_l = (q[late] * jnp.exp(c[late] - cb)).astype(BF16)
        k_e = (k[early] * jnp.exp(cb - c[early])).astype(BF16)
        scaled.append((late, early, q_l, k_e))
    n_sub = c_len // HG_SUB
    t_idx = lax.broadcasted_iota(jnp.int32, (HG_SUB, 1), 0)
    diag = [[None] * HG_HEADS for _ in range(n_sub)]
    for b in range(n_sub):
        r0 = b * HG_SUB
        qb, cb = q[r0:r0 + HG_SUB], c[r0:r0 + HG_SUB]
        for s in range(HG_SUB):
            row = r0 + s
            ok = (t_idx <= s) if rev else (t_idx >= s)
            w = qb * k[row:row + 1] * jnp.exp(jnp.where(ok, cb - c[row:row + 1], -jnp.inf))
            for h in range(HG_HEADS):
                sl = slice(h * HG_KEY, (h + 1) * HG_KEY)
                contrib = jnp.sum(w[:, sl], axis=-1, keepdims=True) * v[row:row + 1, sl]
                diag[b][h] = contrib if diag[b][h] is None else diag[b][h] + contrib

    for h in range(HG_HEADS):
        sl = slice(h * HG_KEY, (h + 1) * HG_KEY)
        o_h = _dot_nt(q_in[:, sl], st_ref[h].astype(BF16))
        parts = [diag[b][h] for b in range(n_sub)]
        for late, early, q_l, k_e in scaled:
            sc = _dot_nt(q_l[:, sl], k_e[:, sl]).astype(BF16)
            add = _dot(sc, vb[early, sl])
            b0 = late.start // HG_SUB
            for j in range((late.stop - late.start) // HG_SUB):
                parts[b0 + j] = parts[b0 + j] + add[j * HG_SUB:(j + 1) * HG_SUB]
        o_ref[rows, sl] = o_h + jnp.concatenate(parts, axis=0)


def _gla_kernel(lb_ref, zf_ref, q_ref, v_ref, zfc_ref, vc_ref, o_ref, st_ref, *, rev, n_ctx_steps):
    s = pl.program_id(1)

    @pl.when(s == 0)
    def _():
        st_ref[...] = jnp.zeros_like(st_ref)

    x = lb_ref[...]
    e = jnp.exp(x - jnp.max(x, axis=0, keepdims=True))
    lb = e[0:1] / jnp.sum(e, axis=0, keepdims=True)

    @pl.when(s < n_ctx_steps)
    def _():
        _gla_step(zfc_ref[...], None, vc_ref[...], lb, st_ref, None, rev)

    @pl.when(s >= n_ctx_steps)
    def _():
        _gla_step(zf_ref[...], q_ref[...], v_ref[...], lb, st_ref, o_ref, rev)


def _hgrn2_scan(a_lat, f_lat, a_ctx, f_ctx, hg_lb, batch, n_seq, n_ctx, rev):
    hk = HG_HEADS * HG_KEY
    rows = HG_CHUNK * HG_STEP_CHUNKS
    assert n_seq % rows == 0 and n_ctx % rows == 0
    nc, ncc = n_seq // rows, n_ctx // rows
    d = 1 if rev else 0

    def lat(b, s):
        j = jnp.maximum(s - ncc, 0)
        return b * nc + (nc - 1 - j if rev else j)

    def ctx(b, s):
        j = jnp.minimum(s, ncc - 1)
        return b * ncc + (ncc - 1 - j if rev else j)

    return pl.pallas_call(
        functools.partial(_gla_kernel, rev=rev, n_ctx_steps=ncc),
        grid=(batch, ncc + nc),
        in_specs=[
            pl.BlockSpec((None, hg_lb.shape[1], hk), lambda b, s: (d, 0, 0)),
            pl.BlockSpec((rows, hk), lambda b, s: (lat(b, s), d)),
            pl.BlockSpec((rows, hk), lambda b, s: (lat(b, s), 2)),
            pl.BlockSpec((rows, hk), lambda b, s: (lat(b, s), 3)),
            pl.BlockSpec((rows, hk), lambda b, s: (ctx(b, s), d)),
            pl.BlockSpec((rows, hk), lambda b, s: (ctx(b, s), 1)),
        ],
        out_specs=pl.BlockSpec((rows, hk), lambda b, s: (lat(b, s), 0)),
        out_shape=jax.ShapeDtypeStruct((batch * n_seq, hk), F32),
        scratch_shapes=[pltpu.VMEM((HG_HEADS, HG_KEY, HG_KEY), F32)],
        compiler_params=_params("parallel", "arbitrary"),
        name="hgrn2_bwd" if rev else "hgrn2_fwd",
    )(hg_lb, f_lat, a_lat, a_lat, f_ctx, a_ctx)


def _route(tok, wr_ref, rb):
    t_hi = tok.astype(BF16)
    t_lo = (tok - t_hi.astype(F32)).astype(BF16)
    hi_both = _dot(t_hi, wr_ref[...])
    lg = hi_both[:, :LANES] + hi_both[:, LANES:] + _dot(t_lo, wr_ref[:, :LANES]) + rb
    lane = lax.broadcasted_iota(jnp.int32, lg.shape, 1)
    lane_f = lane.astype(F32)
    ninf = -jnp.inf
    gl = jnp.where(lane < MOE_GROUPS, lg, ninf)
    gmax = jnp.max(gl, axis=-1, keepdims=True)
    g_idx = jnp.min(jnp.where(gl == gmax, lane_f, float(LANES)), axis=-1, keepdims=True)
    g_val = 1.0 / jnp.sum(jnp.exp(gl - gmax), axis=-1, keepdims=True)
    e_lane = lane_f - float(MOE_GROUPS)
    lo = g_idx * float(MOE_EXPERTS_PER_GROUP)
    in_grp = (e_lane >= lo) & (e_lane < lo + float(MOE_EXPERTS_PER_GROUP))
    el = jnp.where(in_grp, lg, ninf)
    l1 = jnp.max(el, axis=-1, keepdims=True)
    i1 = jnp.min(jnp.where(el == l1, e_lane, float(LANES)), axis=-1, keepdims=True)
    el2 = jnp.where(e_lane == i1, ninf, el)
    l2 = jnp.max(el2, axis=-1, keepdims=True)
    i2 = jnp.min(jnp.where(el2 == l2, e_lane, float(LANES)), axis=-1, keepdims=True)
    r = jnp.exp(l2 - l1)
    w1 = g_val / (1.0 + r)
    w2 = w1 * r
    eid = jnp.where(lane == 0, i1, jnp.where(lane == 1, i2, 0.0))
    gate = jnp.where(lane == 0, w1, jnp.where(lane == 1, w2, 0.0))
    eid_t = jnp.transpose(eid)[:MOE_ID_ROWS].astype(jnp.int32)
    return eid_t, gate


def _post_mix(y, x_ref, g1_ref, sc2_ref, sh2_ref, lng_ref, lnb_ref, wr_ref, rb_ref,
              x1_ref, tok_ref, eid_ref, gate_ref, r):
    x1 = _layer_norm(DEEPNORM_ALPHA * x_ref[r] + g1_ref[0] * y, lng_ref[...], lnb_ref[...])
    x1_ref[r] = x1
    tok = x1 * (1.0 + sc2_ref[0]) + sh2_ref[0]
    tok_ref[r] = _tiles_from_rows(tok).astype(BF16)
    eid, gate = _route(tok, wr_ref, rb_ref[...])
    eid_ref[:, r] = eid
    gate_ref[r] = gate


def _sub_tiles(n_rows):
    return [slice(s, s + SUB_ROWS) for s in range(0, n_rows, SUB_ROWS)]


def _even_out_kernel(att_ref, of_ref, ob_ref, gt_ref, ng_ref, wo_ref, *rest):
    tiles = _sub_tiles(att_ref.shape[0])
    lhs = []
    for r in tiles:
        o = of_ref[r] + ob_ref[r]
        pieces = []
        for h in range(HG_HEADS):
            oh = o[:, h * HG_KEY:(h + 1) * HG_KEY]
            pieces.append(oh * lax.rsqrt(jnp.mean(oh * oh, axis=-1, keepdims=True) + NORM_EPS))
        hg = (jnp.concatenate(pieces, axis=-1) * ng_ref[...] * _silu(gt_ref[r].astype(F32))).astype(BF16)
        lhs.append(jnp.concatenate([att_ref[r], hg], axis=-1))
    ys = [_dot(a, wo_ref[...]) for a in lhs]
    for r, y in zip(tiles, ys):
        _post_mix(y, *rest, r)


def _post_specs(d, tm, rows_per_batch):
    def bmap(i):
        return (i * tm // rows_per_batch, 0, 0)

    row = pl.BlockSpec((tm, d), lambda i: (i, 0))
    mod = pl.BlockSpec((1, 1, d), bmap)
    vec = pl.BlockSpec((1, d), lambda i: (0, 0))
    rw = pl.BlockSpec((d, 2 * LANES), lambda i: (0, 0))
    in_specs = [row, mod, mod, mod, vec, vec, rw, pl.BlockSpec((1, LANES), lambda i: (0, 0))]
    lane_blk = pl.BlockSpec((tm, LANES), lambda i: (i, 0))
    tiles = pl.BlockSpec((tm, d // LANES, LANES), lambda i: (i, 0, 0))
    out_specs = [row, tiles, pl.BlockSpec((MOE_ID_ROWS, tm), lambda i: (0, i)), lane_blk]
    return in_specs, out_specs


def _post_out_shapes(t, d):
    return [jax.ShapeDtypeStruct((t, d), F32), jax.ShapeDtypeStruct((t, d // LANES, LANES), BF16),
            jax.ShapeDtypeStruct((MOE_ID_ROWS, t), jnp.int32), jax.ShapeDtypeStruct((t, LANES), F32)]


def _resident(shape):
    return pl.BlockSpec(shape, lambda i: (0,) * len(shape), pipeline_mode=pl.Buffered(1))


def _even_out(att, o_f, o_b, p, norm_g, w_out, x, g1, sc2, sh2, lng, lnb, wr, rb, rows_per_batch,
              tm=ROW_TILE):
    t, d = x.shape
    hv = o_f.shape[1]
    post_in, post_out = _post_specs(d, tm, rows_per_batch)
    return pl.pallas_call(
        _even_out_kernel,
        grid=(t // tm,),
        in_specs=[
            pl.BlockSpec((tm, att.shape[1]), lambda i: (i, 0)),
            pl.BlockSpec((tm, hv), lambda i: (i, 0)),
            pl.BlockSpec((tm, hv), lambda i: (i, 0)),
            pl.BlockSpec((tm, hv), lambda i: (i, 4)),
            pl.BlockSpec((1, hv), lambda i: (0, 0)),
            _resident(w_out.shape),
        ] + post_in,
        out_specs=post_out,
        out_shape=_post_out_shapes(t, d),
        compiler_params=_params("parallel"),
        name="even_out",
    )(att, o_f, o_b, p, norm_g, w_out, x, g1, sc2, sh2, lng, lnb, wr, rb)


def _combine(x_ref, ya_ref, yb_ref, gate_ref, g2_ref, lng_ref, lnb_ref, r):
    gate = gate_ref[r]
    y = (gate[:, 0:1] * _rows_from_tiles(ya_ref[r].astype(F32))
         + gate[:, 1:2] * _rows_from_tiles(yb_ref[r].astype(F32)))
    return _layer_norm(DEEPNORM_ALPHA * x_ref[r] + g2_ref[0] * y, lng_ref[...], lnb_ref[...])


def _combine_proj_kernel(x_ref, ya_ref, yb_ref, gate_ref, g2_ref, lng_ref, lnb_ref, sc_ref, sh_ref, w_ref,
                         x2_ref, u_ref):
    tiles = _sub_tiles(x_ref.shape[0])
    lhs = []
    for r in tiles:
        x2 = _combine(x_ref, ya_ref, yb_ref, gate_ref, g2_ref, lng_ref, lnb_ref, r)
        x2_ref[r] = x2
        lhs.append((x2 * (1.0 + sc_ref[0]) + sh_ref[0]).astype(BF16))
    for r, a in zip(tiles, lhs):
        u_ref[r] = _dot(a, w_ref[...])


def _combine_kernel(x_ref, ya_ref, yb_ref, gate_ref, g2_ref, lng_ref, lnb_ref, x2_ref):
    for r in _sub_tiles(x_ref.shape[0]):
        x2_ref[r] = _combine(x_ref, ya_ref, yb_ref, gate_ref, g2_ref, lng_ref, lnb_ref, r)


def _combine_call(x1, y2, gate, g2, lng, lnb, rows_per_batch, proj=None, tm=ROW_TILE):
    t, d = x1.shape
    nt = t // tm

    def bmap(i):
        return (i * tm // rows_per_batch, 0, 0)

    row = pl.BlockSpec((tm, d), lambda i: (i, 0))
    mod = pl.BlockSpec((1, 1, d), bmap)
    vec = pl.BlockSpec((1, d), lambda i: (0, 0))
    in_specs = [row, pl.BlockSpec((tm, d // LANES, LANES), lambda i: (i, 0, 0)),
                pl.BlockSpec((tm, d // LANES, LANES), lambda i: (nt + i, 0, 0)),
                pl.BlockSpec((tm, LANES), lambda i: (i, 0)), mod, vec, vec]
    args = [x1, y2, y2, gate, g2, lng, lnb]
    if proj is None:
        return pl.pallas_call(
            _combine_kernel, grid=(nt,), in_specs=in_specs, out_specs=row,
            out_shape=jax.ShapeDtypeStruct((t, d), F32),
            compiler_params=_params("parallel"), name="combine_ln")(*args)
    sc, sh, w = proj
    return pl.pallas_call(
        _combine_proj_kernel, grid=(nt,),
        in_specs=in_specs + [mod, mod, _resident(w.shape)],
        out_specs=[row, pl.BlockSpec((tm, w.shape[1]), lambda i: (i, 0))],
        out_shape=[jax.ShapeDtypeStruct((t, d), F32), jax.ShapeDtypeStruct((t, w.shape[1]), F32)],
        compiler_params=_params("parallel"), name="combine_ln_proj")(*args, sc, sh, w)


def _pool_out_kernel(up_ref, uc_ref, un_ref, wg_ref, ps_ref, wo_ref, *rest, n_seq):
    tm, d = uc_ref.shape
    n_grp = len(POOL_WINDOWS)
    ch = d // n_grp
    halo = POOL_HALO
    tiles = _sub_tiles(tm)
    lhs = []
    for r in tiles:
        n_r = r.stop - r.start
        pos0 = (pl.program_id(0) * tm + r.start) % n_seq
        e_pos = pos0 - halo + lax.broadcasted_iota(jnp.int32, (n_r + 2 * halo, 1), 0)
        e_ok = (e_pos >= 0) & (e_pos < n_seq)
        t_pos = pos0 + lax.broadcasted_iota(jnp.int32, (n_r, 1), 0)
        z = []
        for gi, w in enumerate(POOL_WINDOWS):
            cs = slice(gi * ch, (gi + 1) * ch)
            u = uc_ref[r, cs]
            before = up_ref[:, cs] if r.start == 0 else uc_ref[r.start - halo:r.start, cs]
            after = un_ref[:, cs] if r.stop == tm else uc_ref[r.stop:r.stop + halo, cs]
            ext = jnp.where(e_ok, jnp.concatenate([before, u, after], axis=0), 0.0)
            a, span = ext, 1
            while span < w:
                a = a[:a.shape[0] - span] + a[span:]
                span *= 2
            start = halo - w // 2
            win = a[start:start + n_r]
            cnt = (jnp.minimum(t_pos + (w - w // 2), n_seq) - jnp.maximum(t_pos - w // 2, 0)).astype(F32)
            mixed = (win / cnt - u).astype(BF16)
            z.append((_dot(mixed, wg_ref[gi]) * ps_ref[:, cs]).astype(BF16))
        lhs.append(jnp.concatenate(z, axis=-1))
    ys = [_dot(a, wo_ref[...]) for a in lhs]
    for r, y in zip(tiles, ys):
        _post_mix(y, *rest, r)


def _pool_out(u, w_grp, scale, w_out, x, g1, sc2, sh2, lng, lnb, wr, rb, n_seq, tm=ROW_TILE):
    t, d = x.shape
    hb = tm // POOL_HALO
    n_hb = t // POOL_HALO
    post_in, post_out = _post_specs(d, tm, n_seq)
    return pl.pallas_call(
        functools.partial(_pool_out_kernel, n_seq=n_seq),
        grid=(t // tm,),
        in_specs=[
            pl.BlockSpec((POOL_HALO, d), lambda i: (jnp.maximum(i * hb - 1, 0), 0)),
            pl.BlockSpec((tm, d), lambda i: (i, 0)),
            pl.BlockSpec((POOL_HALO, d), lambda i: (jnp.minimum((i + 1) * hb, n_hb - 1), 0)),
            _resident(w_grp.shape),
            pl.BlockSpec((1, d), lambda i: (0, 0)),
            _resident(w_out.shape),
        ] + post_in,
        out_specs=post_out,
        out_shape=_post_out_shapes(t, d),
        compiler_params=_params("parallel"),
        name="pool_out",
    )(u, u, u, w_grp, scale, w_out, x, g1, sc2, sh2, lng, lnb, wr, rb)


def _moe_kernel(be_ref, nv_ref, first_ref, ws_ref, nxt_ref, idx_ref, idxn_ref, idxp_ref, tok_hbm, w1_hbm, w3_hbm, w2_hbm,
                y_hbm, xbuf, ybuf, xb_ref, wf1, wf3, wf2, w1b, w3b, w2b, gsem, ssem, wsem, *, n_tok, layer, n_blocks):
    i = pl.program_id(0)
    used = nv_ref[jnp.minimum(i, n_blocks - 1)] > 0
    used = used & (i < n_blocks)
    prev_used = (i > 0) & (nv_ref[jnp.maximum(i - 1, 0)] > 0)
    xs = i % 2

    def weight_copies(e, ws):
        return (pltpu.make_async_copy(w1_hbm.at[layer, e], wf1.at[ws], wsem.at[ws]),
                pltpu.make_async_copy(w3_hbm.at[layer, e], wf3.at[ws], wsem.at[ws]),
                pltpu.make_async_copy(w2_hbm.at[layer, e], wf2.at[ws], wsem.at[ws]))

    def gather_start(idx, slot):
        for r in range(MOE_ROWS):
            tok = idx[0, 0, r] & (n_tok - 1)
            pltpu.make_async_copy(tok_hbm.at[tok], xbuf.at[slot, r], gsem.at[slot]).start()

    def gather_wait(slot):
        pltpu.make_async_copy(tok_hbm.at[pl.ds(0, MOE_ROWS)], xbuf.at[slot], gsem.at[slot]).wait()

    def scatter_start(idx, slot):
        for r in range(MOE_ROWS):
            pltpu.make_async_copy(ybuf.at[slot, r], y_hbm.at[idx[0, 0, r]], ssem.at[slot]).start(priority=r % 2)

    def scatter_wait(slot):
        pltpu.make_async_copy(ybuf.at[slot], y_hbm.at[pl.ds(0, MOE_ROWS)], ssem.at[slot]).wait()

    @pl.when(i == 0)
    def _():
        xbuf[...] = jnp.zeros_like(xbuf)
        ybuf[...] = jnp.zeros_like(ybuf)
        spare0 = pltpu.make_async_copy(
            ybuf.at[0], y_hbm.at[pl.ds(MOE_TOP_K * n_tok, MOE_ROWS)], ssem.at[0])
        spare0.start()
        for cp in weight_copies(be_ref[0], 0):
            cp.start(priority=WEIGHT_DMA_PRIORITY)
        gather_start(idx_ref, 0)

    @pl.when(used)
    def _():
        ws = ws_ref[i]

        @pl.when(first_ref[i] == 1)
        def _():
            for cp in weight_copies(be_ref[i], ws):
                cp.wait()
            nxt = nxt_ref[i]

            @pl.when(nxt >= 0)
            def _():
                for cp in weight_copies(nxt, 1 - ws):
                    cp.start(priority=WEIGHT_DMA_PRIORITY)

            w1b[...] = wf1[ws].astype(BF16)
            w3b[...] = wf3[ws].astype(BF16)
            w2b[...] = wf2[ws].astype(BF16)

        gather_wait(xs)
        xb_ref[...] = _rows_from_tiles(xbuf[xs].astype(F32)).astype(BF16)
        gather_start(idxn_ref, 1 - xs)
        scatter_start(idxp_ref, 1 - xs)
        xb = xb_ref[...]
        h = (_silu(_dot(xb, w1b[...])) * _dot(xb, w3b[...])).astype(BF16)
        y = _tiles_from_rows(_dot(h, w2b[...])).astype(BF16)
        scatter_wait(xs)
        ybuf[xs] = y

    @pl.when(jnp.logical_not(used) & prev_used)
    def _():
        gather_wait(xs)
        scatter_start(idxp_ref, 1 - xs)
        scatter_wait(1 - xs)
        scatter_wait(xs)


def _moe_dispatch(eid_t, n_blocks):
    n_tok = eid_t.shape[1]
    n_slots = (n_blocks + 2) * MOE_ROWS
    slot_rows = -(-n_slots // SMEM_1D_TILE)
    assert n_blocks <= LANES and n_tok % SMEM_1D_TILE == 0 and SMEM_1D_TILE % MOE_ROWS == 0
    slot, meta = pl.pallas_call(
        functools.partial(_dispatch_kernel, n_tok=n_tok),
        in_specs=[pl.BlockSpec(memory_space=pltpu.VMEM)],
        out_specs=[pl.BlockSpec(memory_space=pltpu.SMEM), pl.BlockSpec(memory_space=pltpu.VMEM)],
        out_shape=[jax.ShapeDtypeStruct((slot_rows * SMEM_1D_TILE,), jnp.int32),
                   jax.ShapeDtypeStruct((MOE_ID_ROWS, LANES), jnp.int32)],
        scratch_shapes=[pltpu.VMEM((MOE_TOP_K * n_tok,), jnp.int32),
                        pltpu.VMEM((slot_rows * SMEM_1D_TILE,), jnp.int32),
                        pltpu.SMEM((MOE_TOP_K * n_tok,), jnp.int32),
                        pltpu.SemaphoreType.DMA(())],
        name="moe_dispatch",
    )(eid_t)
    return (slot[:n_slots].reshape(n_blocks + 2, 1, MOE_ROWS),) + tuple(meta[r, :n_blocks] for r in range(5))


def _dispatch_kernel(eid_ref, slot_ref, meta_ref, dest_vmem, init_vmem, dest_smem, sem, *, n_tok):
    tile = MOE_ROWS
    n_tiles = n_tok // tile
    sub = lax.broadcasted_iota(jnp.int32, (N_EXPERTS, tile), 0)
    si = lax.broadcasted_iota(jnp.int32, (tile, tile), 0)
    ti = lax.broadcasted_iota(jnp.int32, (tile, tile), 1)
    before = jnp.where(si < ti, 1.0, 0.0).astype(BF16)

    def one_hots(j):
        ids = eid_ref[:, j * tile:(j + 1) * tile]
        return [jnp.where(sub == ids[k:k + 1], 1.0, 0.0) for k in range(MOE_TOP_K)]

    carry = jnp.zeros((N_EXPERTS, 1), F32)
    ranks = []
    for j in range(n_tiles):
        oh = one_hots(j)
        both = oh[0] + oh[1]
        seen = carry + _dot(both.astype(BF16), before)
        ranks.append([jnp.sum(seen * o, axis=0, keepdims=True) for o in oh])
        carry = carry + jnp.sum(both, axis=1, keepdims=True)

    counts = carry
    nblk = jnp.floor((counts + float(MOE_ROWS - 1)) * (1.0 / MOE_ROWS))
    ei = lax.broadcasted_iota(jnp.int32, (N_EXPERTS, N_EXPERTS), 0)
    ej = lax.broadcasted_iota(jnp.int32, (N_EXPERTS, N_EXPERTS), 1)
    lower = jnp.where(ej < ei, 1.0, 0.0).astype(BF16)
    first_blk = _dot(lower, jnp.broadcast_to(nblk, (N_EXPERTS, LANES)).astype(BF16))[:, 0:1]
    first_slot = first_blk * float(MOE_ROWS)

    per_row = SMEM_1D_TILE // tile
    for k in range(MOE_TOP_K):
        for q in range(n_tiles // per_row):
            parts = []
            for j in range(q * per_row, (q + 1) * per_row):
                parts.append(jnp.sum(first_slot * one_hots(j)[k], axis=0, keepdims=True) + ranks[j][k])
            dest = jnp.concatenate(parts, axis=1).astype(jnp.int32)
            dest_vmem[pl.ds(k * n_tok + q * SMEM_1D_TILE, SMEM_1D_TILE)] = dest.reshape(SMEM_1D_TILE)

    lane = lax.broadcasted_iota(jnp.int32, (1, SMEM_1D_TILE), 1)
    for q in range(init_vmem.shape[0] // SMEM_1D_TILE):
        pos = q * SMEM_1D_TILE + lane
        spare = MOE_TOP_K * n_tok + ((pos // MOE_ROWS + 1) % 2) * MOE_ROWS + pos % MOE_ROWS
        init_vmem[pl.ds(q * SMEM_1D_TILE, SMEM_1D_TILE)] = spare.reshape(SMEM_1D_TILE)
    copies = [pltpu.make_async_copy(dest_vmem, dest_smem, sem), pltpu.make_async_copy(init_vmem, slot_ref, sem)]
    for cp in copies:
        cp.start()
    for cp in copies:
        cp.wait()

    def place(t, carry_):
        for k in range(MOE_TOP_K):
            slot_ref[dest_smem[k * n_tok + t] + MOE_ROWS] = k * n_tok + t
        return carry_

    lax.fori_loop(0, n_tok, place, 0, unroll=8)

    b = lax.broadcasted_iota(jnp.int32, (N_EXPERTS, LANES), 1).astype(F32)
    e_col = lax.broadcasted_iota(jnp.int32, (N_EXPERTS, LANES), 0).astype(F32)
    b_row = b[0:1]
    be = jnp.minimum(jnp.sum(jnp.where(first_blk + nblk <= b, 1.0, 0.0), axis=0, keepdims=True), N_EXPERTS - 1.0)
    mine = e_col == be
    cnt_b = jnp.sum(jnp.where(mine, counts, 0.0), axis=0, keepdims=True)
    start_b = jnp.sum(jnp.where(mine, first_blk, 0.0), axis=0, keepdims=True)
    nv = jnp.clip(cnt_b - (b_row - start_b) * MOE_ROWS, 0.0, float(MOE_ROWS))
    nv = jnp.where(b_row < jnp.sum(nblk, axis=0, keepdims=True), nv, 0.0)
    first = jnp.where((nv > 0) & ((b_row == 0) | (be != pltpu.roll(be, 1, 1))), 1.0, 0.0)
    li = lax.broadcasted_iota(jnp.int32, (LANES, LANES), 0)
    lj = lax.broadcasted_iota(jnp.int32, (LANES, LANES), 1)
    upto = jnp.where(li <= lj, 1.0, 0.0).astype(BF16)
    run = _dot(jnp.broadcast_to(first, (MOE_ID_ROWS, LANES)).astype(BF16), upto)[0:1] - 1.0
    ws = run - 2.0 * jnp.floor(run * 0.5)
    later = jnp.min(jnp.where((e_col > be) & (counts > 0), e_col, float(LANES)), axis=0, keepdims=True)
    nxt = jnp.where(later >= float(N_EXPERTS), -1.0, later)
    rows = [be, nv, first, ws, nxt] + [jnp.zeros_like(be)] * (MOE_ID_ROWS - 5)
    meta_ref[...] = jnp.concatenate(rows, axis=0).astype(jnp.int32)


def _moe_experts(tok, eid, w1, w3, w2, layer):
    n_tok, n_sub, _ = tok.shape
    d = n_sub * LANES
    assert n_tok & (n_tok - 1) == 0
    ff = w1.shape[3]
    n_assign = n_tok * MOE_TOP_K
    n_blocks = -(-(n_assign + N_EXPERTS * (MOE_ROWS - 1)) // MOE_ROWS)
    slot, be, nv, first, ws, nxt = _moe_dispatch(eid, n_blocks)
    grid_spec = pltpu.PrefetchScalarGridSpec(
        num_scalar_prefetch=5,
        grid=(n_blocks + 1,),
        in_specs=[
            pl.BlockSpec((1, 1, MOE_ROWS), lambda i, *_: (i + 1, 0, 0), memory_space=pltpu.SMEM),
            pl.BlockSpec((1, 1, MOE_ROWS), lambda i, *_: (jnp.minimum(i + 2, n_blocks + 1), 0, 0),
                         memory_space=pltpu.SMEM),
            pl.BlockSpec((1, 1, MOE_ROWS), lambda i, *_: (i, 0, 0), memory_space=pltpu.SMEM),
            pl.BlockSpec(memory_space=pl.ANY),
            pl.BlockSpec(memory_space=pl.ANY),
            pl.BlockSpec(memory_space=pl.ANY),
            pl.BlockSpec(memory_space=pl.ANY),
        ],
        out_specs=pl.BlockSpec(memory_space=pl.ANY),
        scratch_shapes=[
            pltpu.VMEM((2, MOE_ROWS, n_sub, LANES), BF16), pltpu.VMEM((2, MOE_ROWS, n_sub, LANES), BF16),
            pltpu.VMEM((MOE_ROWS, d), BF16),
            pltpu.VMEM((2, d, ff), F32), pltpu.VMEM((2, d, ff), F32), pltpu.VMEM((2, ff, d), F32),
            pltpu.VMEM((d, ff), BF16), pltpu.VMEM((d, ff), BF16), pltpu.VMEM((ff, d), BF16),
            pltpu.SemaphoreType.DMA((2,)), pltpu.SemaphoreType.DMA((2,)), pltpu.SemaphoreType.DMA((2,)),
        ],
    )
    return pl.pallas_call(
        functools.partial(_moe_kernel, n_tok=n_tok, layer=layer, n_blocks=n_blocks),
        grid_spec=grid_spec,
        out_shape=jax.ShapeDtypeStruct((MOE_TOP_K * n_tok + 2 * MOE_ROWS, n_sub, LANES), BF16),
        compiler_params=pltpu.CompilerParams(dimension_semantics=("arbitrary",),
                                             vmem_limit_bytes=MOE_VMEM_LIMIT_BYTES),
        name="moe_experts",
    )(be, nv, first, ws, nxt, slot, slot, slot, tok, w1, w3, w2)


def _rope_tables(n_seq):
    half = HEAD_DIM // 2
    n_freq = half // 2
    t = jnp.arange(n_seq)
    row = (t // GRID_W).astype(F32)
    col = (t % GRID_W).astype(F32)
    inv_freq = ROPE_BASE ** (-jnp.arange(n_freq, dtype=F32) / n_freq)
    ang_r = row[:, None] * inv_freq[None, :]
    ang_c = col[:, None] * inv_freq[None, :]
    cos = jnp.concatenate([jnp.cos(ang_r)] * 2 + [jnp.cos(ang_c)] * 2, axis=-1)
    sin = jnp.concatenate([-jnp.sin(ang_r), jnp.sin(ang_r), -jnp.sin(ang_c), jnp.sin(ang_c)], axis=-1)
    return cos, sin


def _router_weights(w_g, b_g, w_e, b_e):
    d = w_g.shape[0]
    n = w_g.shape[1] + w_e.shape[1]
    wr = jnp.concatenate([w_g, w_e, jnp.zeros((d, LANES - n), F32)], axis=1)
    rb = jnp.concatenate([b_g, b_e, jnp.zeros((LANES - n,), F32)]).reshape(1, LANES)
    hi = wr.astype(BF16)
    lo = (wr - hi.astype(F32)).astype(BF16)
    return jnp.concatenate([hi, lo], axis=1), rb


def kernel(x, c, ctx, c_ctx, ada_w, ada_b, ln_g, ln_b, mix_w_in, att_sink, hg_lb, hg_norm_g, mix_w_out, pool_w_in, pool_w_grp, pool_scale, pool_w_out, rt_group_w, rt_group_b, rt_expert_w, rt_expert_b, moe_w1, moe_w3, moe_w2):
    b, n, d = x.shape
    n_ctx = ctx.shape[1]
    t = b * n
    xf = x.reshape(t, d)
    ctxf = ctx.reshape(b * n_ctx, d)

    cond = jnp.concatenate([c, c_ctx[None, :], jnp.zeros((8 - b - 1, d), F32)], axis=0)
    mod = _ada_mod(cond, ada_w, ada_b)

    def chunk(l, j, rows=slice(0, b)):
        return mod[l, rows, j * d:(j + 1) * d][:, None, :]

    w_in = mix_w_in[0].astype(BF16)
    cos, sin = _rope_tables(n)
    q_w, kv_w = ATT_HEADS * HEAD_DIM, ATT_KV_HEADS * HEAD_DIM
    n_att = q_w + 2 * kv_w
    hk = HG_HEADS * HG_KEY
    assert n_att == 2 * PROJ_TN and hk == PROJ_TN
    a_lat, f_lat = _mod_matmul(xf, chunk(0, 1), chunk(0, 0), w_in, cos, sin,
                               lambda j: jnp.where(j < 3, j, jnp.where(j < 5, j + 2, j - 2)), n_att + 3 * hk, 2 * hk,
                               n_q=q_w, n_rope=q_w + kv_w, n_seq=n, tm=1024, tn=PROJ_TN)
    ctx_rows = slice(b, b + 1)
    a_ctx, f_ctx = _mod_matmul(ctxf, chunk(0, 1, ctx_rows), chunk(0, 0, ctx_rows), w_in, cos, sin,
                               lambda j: jnp.where(j < 1, 1, jnp.where(j < 2, 5, j + 1)), 2 * kv_w + hk, 2 * hk,
                               n_q=0, n_rope=0, n_seq=n, tm=b * n_ctx, tn=PROJ_TN)
    att = _window_attention(a_lat, a_ctx, att_sink[0], b, n, n_ctx)
    o_f = _hgrn2_scan(a_lat, f_lat, a_ctx, f_ctx, hg_lb, b, n, n_ctx, rev=False)
    o_b = _hgrn2_scan(a_lat, f_lat, a_ctx, f_ctx, hg_lb, b, n, n_ctx, rev=True)
    wr, rb = _router_weights(rt_group_w[0], rt_group_b[0], rt_expert_w[0], rt_expert_b[0])
    x1, tok, eid, gate = _even_out(
        att, o_f, o_b, a_lat, hg_norm_g[0][None, :], mix_w_out[0].astype(BF16), xf,
        chunk(0, 2), chunk(0, 4), chunk(0, 3), ln_g[0, 0][None, :], ln_b[0, 0][None, :], wr, rb, n)
    y2 = _moe_experts(tok, eid, moe_w1, moe_w3, moe_w2, 0)

    x2, u = _combine_call(x1, y2, gate, chunk(0, 5), ln_g[0, 1][None, :], ln_b[0, 1][None, :], n,
                          proj=(chunk(1, 1), chunk(1, 0), pool_w_in[0].astype(BF16)))
    wr, rb = _router_weights(rt_group_w[1], rt_group_b[1], rt_expert_w[1], rt_expert_b[1])
    x3, tok, eid, gate = _pool_out(
        u, pool_w_grp[0].astype(BF16), pool_scale[0][None, :], pool_w_out[0].astype(BF16), x2,
        chunk(1, 2), chunk(1, 4), chunk(1, 3), ln_g[1, 0][None, :], ln_b[1, 0][None, :], wr, rb, n)
    y2 = _moe_experts(tok, eid, moe_w1, moe_w3, moe_w2, 1)
    out = _combine_call(x3, y2, gate, chunk(1, 5), ln_g[1, 1][None, :], ln_b[1, 1][None, :], n)
    return out.reshape(b, n, d)
```

```python
import functools

import jax
import jax.numpy as jnp
from jax import lax
from jax.experimental import pallas as pl
from jax.experimental.pallas import tpu as pltpu

F32 = jnp.float32
BF16 = jnp.bfloat16

LANES = 128
MXU_TILE = 256
VMEM_LIMIT_BYTES = 56 * 1024 * 1024
MOE_VMEM_LIMIT_BYTES = 60 * 1024 * 1024

GRID_W = 64
ATT_HEADS = 8
ATT_KV_HEADS = 4
ATT_GROUP = ATT_HEADS // ATT_KV_HEADS
HEAD_DIM = 128
WINDOW = 128
ATT_BLOCK = 128
ROPE_BASE = 10000.0
HG_HEADS = 8
HG_KEY = 128
HG_CHUNK = 64
HG_STEP_CHUNKS = 4
HG_SUB = 16
HG_FAST_RANGE = 80.0
NORM_EPS = 1e-6
POOL_WINDOWS = (2, 4, 8, 16)
POOL_HALO = 8
MOE_GROUPS = 4
MOE_EXPERTS_PER_GROUP = 8
N_EXPERTS = MOE_GROUPS * MOE_EXPERTS_PER_GROUP
MOE_TOP_K = 2
MOE_ROWS = 256
ROW_TILE = 512
SUB_ROWS = 256
PROJ_TN = 1024
MOE_ID_ROWS = 8
SMEM_1D_TILE = 1024
WEIGHT_DMA_PRIORITY = 1
LN_EPS = 1e-5
DEPTH = 2
DEEPNORM_ALPHA = (2 * DEPTH) ** 0.25


def _dot(a, b):
    return jnp.dot(a, b, preferred_element_type=F32)


def _dot_nt(a, b):
    return lax.dot_general(a, b, (((1,), (1,)), ((), ())), preferred_element_type=F32)


def _dot_tn(a, b):
    return lax.dot_general(a, b, (((0,), (0,)), ((), ())), preferred_element_type=F32)


def _sigmoid(x):
    return 1.0 / (1.0 + jnp.exp(-x))


def _silu(x):
    return x * _sigmoid(x)


def _params(*sem):
    return pltpu.CompilerParams(dimension_semantics=sem, vmem_limit_bytes=VMEM_LIMIT_BYTES)


def _tiles_from_rows(x):
    n = x.shape[1] // LANES
    return jnp.swapaxes(jnp.stack([x[:, s * LANES:(s + 1) * LANES] for s in range(n)], axis=0), 0, 1)


def _rows_from_tiles(x3):
    xt = jnp.swapaxes(x3, 0, 1)
    return jnp.concatenate([xt[s] for s in range(xt.shape[0])], axis=-1)


def _layer_norm(z, g, b):
    mu = jnp.mean(z, axis=-1, keepdims=True)
    zc = z - mu
    var = jnp.mean(zc * zc, axis=-1, keepdims=True)
    return zc * lax.rsqrt(var + LN_EPS) * g + b


def _ada_kernel(s_ref, w_ref, b_ref, o_ref):
    s = _silu(s_ref[...]).astype(BF16)
    o_ref[0] = _dot(s, w_ref[0].astype(BF16)) + b_ref[0]


def _ada_mod(s, ada_w, ada_b, tn=1024):
    n_l, d, n = ada_w.shape
    return pl.pallas_call(
        _ada_kernel,
        grid=(n_l, n // tn),
        in_specs=[
            pl.BlockSpec((8, d), lambda l, j: (0, 0)),
            pl.BlockSpec((1, d, tn), lambda l, j: (l, 0, j)),
            pl.BlockSpec((1, 1, tn), lambda l, j: (l, 0, j)),
        ],
        out_specs=pl.BlockSpec((1, 8, tn), lambda l, j: (l, 0, j)),
        out_shape=jax.ShapeDtypeStruct((n_l, 8, n), F32),
        compiler_params=_params("parallel", "parallel"),
        name="ada_mod",
    )(s, ada_w, ada_b.reshape(n_l, 1, n))


def _rope(t, cos, sin_signed, first_half):
    partner = jnp.where(first_half, pltpu.roll(t, 96, 1), pltpu.roll(t, 32, 1))
    return t * cos + partner * sin_signed


def _modmm_kernel(x_ref, sc_ref, sh_ref, w_ref, cos_ref, sin_ref, oa_ref, ob_ref, xs_ref, *,
                  n_q, n_rope, n_a_tiles):
    j = pl.program_id(1)
    tm, tn = x_ref.shape[0], w_ref.shape[1]
    halves = [slice(0, tm // 2), slice(tm // 2, tm)]

    def dots():
        return [_dot(xs_ref[r], w_ref[...]) for r in halves]

    def store_bf16(jt, r, acc):
        lane = lax.broadcasted_iota(jnp.int32, (1, HEAD_DIM), 1)
        first_half = (lane % 64) < 32
        for h in range(tn // HEAD_DIM):
            sl = slice(h * HEAD_DIM, (h + 1) * HEAD_DIM)
            col = jt * tn + h * HEAD_DIM
            if col < n_rope:
                scale = HEAD_DIM ** -0.5 if col < n_q else 1.0
                oa_ref[r, sl] = _rope(acc[:, sl], cos_ref[r] * scale, sin_ref[r] * scale, first_half).astype(BF16)
            else:
                oa_ref[r, sl] = acc[:, sl].astype(BF16)

    for jt in range(n_a_tiles):
        @pl.when(j == jt)
        def _(jt=jt):
            if jt == 0:
                for r in halves:
                    xs_ref[r] = (x_ref[r] * (1.0 + sc_ref[0]) + sh_ref[0]).astype(BF16)
            for r, acc in zip(halves, dots()):
                store_bf16(jt, r, acc)

    @pl.when(j >= n_a_tiles)
    def _():
        for r, acc in zip(halves, dots()):
            ob_ref[r] = acc


def _mod_matmul(x, sc, sh, w, cos, sin, col_map, n_a, n_b, n_q, n_rope, n_seq, tm, tn):
    m, k = x.shape
    rows_per_mod = m // sc.shape[0]
    ta, tb = n_a // tn, n_b // tn
    tab = pl.BlockSpec((tm, HEAD_DIM), lambda i, j: ((i * tm % n_seq) // tm, 0))
    return pl.pallas_call(
        functools.partial(_modmm_kernel, n_q=n_q, n_rope=n_rope, n_a_tiles=ta),
        grid=(m // tm, ta + tb),
        in_specs=[
            pl.BlockSpec((tm, k), lambda i, j: (i, 0)),
            pl.BlockSpec((1, 1, k), lambda i, j: (i * tm // rows_per_mod, 0, 0)),
            pl.BlockSpec((1, 1, k), lambda i, j: (i * tm // rows_per_mod, 0, 0)),
            pl.BlockSpec((k, tn), lambda i, j: (0, col_map(j))),
            tab, tab,
        ],
        out_specs=[pl.BlockSpec((tm, tn), lambda i, j: (i, jnp.minimum(j, ta - 1))),
                   pl.BlockSpec((tm, tn), lambda i, j: (i, jnp.maximum(j - ta, 0)))],
        out_shape=[jax.ShapeDtypeStruct((m, n_a), BF16), jax.ShapeDtypeStruct((m, n_b), F32)],
        scratch_shapes=[pltpu.VMEM((tm, k), BF16)],
        compiler_params=_params("parallel", "arbitrary"),
        name="mod_matmul",
    )(x, sc, sh, w, cos, sin)


def _attn_kernel(sink_ref, q_ref, kp_ref, kc_ref, kn_ref, vp_ref, vc_ref, vn_ref, kx_ref, vx_ref,
                 mp_ref, mn_ref, o_ref, *, n_blocks):
    s_idx = pl.program_id(1)
    blk = ATT_BLOCK
    row1 = lax.broadcasted_iota(jnp.int32, (ATT_GROUP * blk, 1), 0)
    lo, hi = slice(0, blk), slice(blk, 2 * blk)
    subs = [(lo, (kp_ref, vp_ref, lo), (kc_ref, vc_ref, lo), (kc_ref, vc_ref, hi), s_idx > 0, True),
            (hi, (kc_ref, vc_ref, lo), (kc_ref, vc_ref, hi), (kn_ref, vn_ref, lo), True, s_idx < n_blocks // 2 - 1)]
    jobs = [(sub, h) for sub in subs for h in range(ATT_KV_HEADS)]

    def kv(h):
        return slice(h * HEAD_DIM, (h + 1) * HEAD_DIM)

    def gather(sub, h, which):
        blocks = [t[which][t[2], kv(h)] for t in sub[1:4]]
        return jnp.concatenate(blocks + [(kx_ref, vx_ref)[which][:, kv(h)]], axis=0)

    scores = []
    for sub, h in jobs:
        q2 = jnp.concatenate([q_ref[sub[0], (ATT_GROUP * h + g) * HEAD_DIM:(ATT_GROUP * h + g + 1) * HEAD_DIM]
                              for g in range(ATT_GROUP)], axis=0)
        scores.append(_dot_nt(q2, gather(sub, h, 0)))
    probs, dens = [], []
    for (sub, h), s in zip(jobs, scores):
        has_prev, has_next = sub[4], sub[5]
        parts = [jnp.where(has_prev, s[:, :blk] + mp_ref[...], -jnp.inf), s[:, blk:2 * blk],
                 jnp.where(has_next, s[:, 2 * blk:3 * blk] + mn_ref[...], -jnp.inf)]
        parts += [s[:, c:c + blk] for c in range(3 * blk, s.shape[1], blk)]
        sink = jnp.where(row1 < blk, sink_ref[ATT_GROUP * h], sink_ref[ATT_GROUP * h + 1])
        m = jnp.maximum(jnp.max(functools.reduce(jnp.maximum, parts), axis=-1, keepdims=True), sink)
        p = [jnp.exp(x - m) for x in parts]
        dens.append(jnp.sum(functools.reduce(jnp.add, p), axis=-1, keepdims=True) + jnp.exp(sink - m))
        probs.append(jnp.concatenate(p, axis=-1).astype(BF16))
    for (sub, h), p, den in zip(jobs, probs, dens):
        o = _dot(p, gather(sub, h, 1)) / den
        for g in range(ATT_GROUP):
            col = (ATT_GROUP * h + g) * HEAD_DIM
            o_ref[sub[0], col:col + HEAD_DIM] = o[g * blk:(g + 1) * blk].astype(o_ref.dtype)


def _window_attention(qkv, kv_ctx, sink, batch, n_seq, n_ctx):
    assert ATT_GROUP == 2 and WINDOW == ATT_BLOCK
    nb = n_seq // ATT_BLOCK
    qw, kw = ATT_HEADS * HEAD_DIM, ATT_KV_HEADS * HEAD_DIM
    kcol, vcol = qw // kw, qw // kw + 1

    assert nb % 2 == 0
    pair = 2 * ATT_BLOCK

    def before(b, s):
        return b * nb + jnp.maximum(2 * s - 1, 0)

    def after(b, s):
        return b * nb + jnp.minimum(2 * s + 2, nb - 1)

    def own(b, s):
        return b * (nb // 2) + s

    def band(col):
        return [pl.BlockSpec((ATT_BLOCK, kw), lambda b, s: (before(b, s), col)),
                pl.BlockSpec((pair, kw), lambda b, s: (own(b, s), col)),
                pl.BlockSpec((ATT_BLOCK, kw), lambda b, s: (after(b, s), col))]

    kspec, vspec = band(kcol), band(vcol)
    r = jnp.arange(ATT_GROUP * ATT_BLOCK)[:, None] % ATT_BLOCK
    c = jnp.arange(ATT_BLOCK)[None, :]
    mask_prev = jnp.where(c >= r, 0.0, -jnp.inf).astype(F32)
    mask_next = jnp.where(c <= r, 0.0, -jnp.inf).astype(F32)
    mspec = pl.BlockSpec(mask_prev.shape, lambda b, s: (0, 0))
    return pl.pallas_call(
        functools.partial(_attn_kernel, n_blocks=nb),
        grid=(batch, nb // 2),
        in_specs=[pl.BlockSpec(memory_space=pltpu.SMEM),
                  pl.BlockSpec((pair, qw), lambda b, s: (own(b, s), 0))]
        + kspec + vspec
        + [pl.BlockSpec((n_ctx, kw), lambda b, s: (b, 0)), pl.BlockSpec((n_ctx, kw), lambda b, s: (b, 1)),
           mspec, mspec],
        out_specs=pl.BlockSpec((pair, qw), lambda b, s: (own(b, s), 0)),
        out_shape=jax.ShapeDtypeStruct((batch * n_seq, qw), BF16),
        compiler_params=_params("parallel", "parallel"),
        name="window_attention",
    )(sink, qkv, qkv, qkv, qkv, qkv, qkv, qkv, kv_ctx, kv_ctx, mask_prev, mask_next)


def _gla_step(zf, q_raw, v, lb, st_ref, o_ref, rev):
    c_len = HG_CHUNK
    n_rows = zf.shape[0]
    n_sub = n_rows // c_len
    shift = c_len.bit_length() - 1
    order = range(n_sub - 1, -1, -1) if rev else range(n_sub)

    def head(h):
        return slice(h * HG_KEY, (h + 1) * HG_KEY)

    def chunk(i):
        return slice(i * c_len, (i + 1) * c_len)

    def seen(n):
        ri = lax.broadcasted_iota(jnp.int32, (n, n), 0)
        ci = lax.broadcasted_iota(jnp.int32, (n, n), 1)
        return ((ri >> shift) == (ci >> shift)) & ((ci >= ri) if rev else (ci <= ri))

    f = lb + (1.0 - lb) * _sigmoid(zf)
    k = 1.0 - f
    g = jnp.log(f)
    tri = jnp.where(seen(n_rows), 1.0, 0.0).astype(BF16)
    g1 = g.astype(BF16)
    r1 = g - g1.astype(F32)
    g2 = r1.astype(BF16)
    g3 = (r1 - g2.astype(F32)).astype(BF16)
    c = _dot(tri, g1) + _dot(tri, g2) + _dot(tri, g3)
    c_end = [c[i * c_len:i * c_len + 1] if rev else c[(i + 1) * c_len - 1:(i + 1) * c_len] for i in range(n_sub)]
    c_end_rows = jnp.concatenate([jnp.broadcast_to(ce, (c_len, ce.shape[1])) for ce in c_end], axis=0)
    k_end = (k * jnp.exp(c_end_rows - c)).astype(BF16)
    dec = [jnp.exp(ce) for ce in c_end]
    vb = v.astype(BF16)

    def advance(states, i):
        new = []
        for p in range(0, HG_HEADS, 2):
            lanes = slice(p * HG_KEY, (p + 2) * HG_KEY)
            inc = _dot_tn(vb[chunk(i), lanes], k_end[chunk(i), lanes])
            for j in range(2):
                blk = slice(j * HG_KEY, (j + 1) * HG_KEY)
                new.append(states[p + j] * dec[i][:, head(p + j)] + inc[blk, blk])
        return new

    if o_ref is None:
        states = [st_ref[h] for h in range(HG_HEADS)]
        for i in order:
            states = advance(states, i)
        for h in range(HG_HEADS):
            st_ref[h] = states[h]
        return

    q = _silu(q_raw.astype(F32))
    q_in = (q * jnp.exp(c)).astype(BF16)
    lowest = functools.reduce(jnp.minimum, c_end)

    def stack(x, i, h0, n):
        return jnp.concatenate([x[chunk(i), head(h)] for h in range(h0, h0 + n)], axis=0)

    n_qk = MXU_TILE // c_len
    groups = [(i, h0) for i in order for h0 in range(0, HG_HEADS, n_qk)]
    work = {}

    def fast_scores():
        k_in = (k * jnp.exp(-c)).astype(BF16)
        work['raw'] = [_dot_nt(stack(q_in, i, h0, n_qk), stack(k_in, i, h0, n_qk)) for i, h0 in groups]

    def fast_mask():
        same = seen(n_qk * c_len)
        work['masked'] = [jnp.where(same, s, 0.0).astype(BF16) for s in work['raw']]

    def fast_values():
        intra = {}
        for (i, h0), sc in zip(groups, work['masked']):
            pv = _dot(sc, stack(vb, i, h0, n_qk))
            for j in range(n_qk):
                intra[i, h0 + j] = pv[j * c_len:(j + 1) * c_len]
        work['intra'] = intra

    def fast_chain():
        states = [st_ref[h] for h in range(HG_HEADS)]
        for i in order:
            out = []
            for p in range(0, HG_HEADS, 2):
                st_pair = jnp.concatenate([states[p].astype(BF16), states[p + 1].astype(BF16)], axis=0)
                inter = _dot_nt(stack(q_in, i, p, 2), st_pair)
                for j in range(2):
                    out.append(inter[j * c_len:(j + 1) * c_len, j * HG_KEY:(j + 1) * HG_KEY]
                               + work['intra'][i, p + j])
            o_ref[chunk(i), :] = jnp.concatenate(out, axis=1)
            states = advance(states, i)
        for h in range(HG_HEADS):
            st_ref[h] = states[h]

    def exact():
        for i in order:
            r = chunk(i)
            _gla_intra_exact(q[r], k[r], v[r].astype(F32), vb[r], c[r], q_in[r], st_ref, o_ref, r, rev)
            states = advance([st_ref[h] for h in range(HG_HEADS)], i)
            for h in range(HG_HEADS):
                st_ref[h] = states[h]

    return lowest, (fast_scores, fast_mask, fast_values, fast_chain), exact


def _gla_run(dirs):
    parts = [_gla_step(*d) for d in dirs]
    if parts[0] is None:
        return
    lowest = functools.reduce(jnp.minimum, [p[0] for p in parts])
    in_range = jnp.min(lowest) >= -HG_FAST_RANGE

    @pl.when(in_range)
    def _():
        for phase in zip(*[p[1] for p in parts]):
            for fn in phase:
                fn()

    @pl.when(jnp.logical_not(in_range))
    def _():
        for p in parts:
            p[2]()


def _gla_intra_exact(q, k, v, vb, c, q_in, st_ref, o_ref, rows, rev):
    c_len = q.shape[0]
    pairs = []
    size = c_len // 2
    while size >= HG_SUB:
        for lo in range(0, c_len, 2 * size):
            pairs.append((lo, lo + size, lo + 2 * size))
        size //= 2
    scaled = []
    for lo, mid, hi in pairs:
        if rev:
            late, early, bnd = slice(lo, mid), slice(mid, hi), mid
        else:
            late, early, bnd = slice(mid, hi), slice(lo, mid), mid - 1
        cb = c[bnd:bnd + 1]
        q_l = (q[late] * jnp.exp(c[late] - cb)).astype(BF16)
        k_e = (k[early] * jnp.exp(cb - c[early])).astype(BF16)
        scaled.append((late, early, q_l, k_e))
    n_sub = c_len // HG_SUB
    t_idx = lax.broadcasted_iota(jnp.int32, (HG_SUB, 1), 0)
    diag = [[None] * HG_HEADS for _ in range(n_sub)]
    for b in range(n_sub):
        r0 = b * HG_SUB
        qb, cb = q[r0:r0 + HG_SUB], c[r0:r0 + HG_SUB]
        for s in range(HG_SUB):
            row = r0 + s
            ok = (t_idx <= s) if rev else (t_idx >= s)
            w = qb * k[row:row + 1] * jnp.exp(jnp.where(ok, cb - c[row:row + 1], -jnp.inf))
            for h in range(HG_HEADS):
                sl = slice(h * HG_KEY, (h + 1) * HG_KEY)
                contrib = jnp.sum(w[:, sl], axis=-1, keepdims=True) * v[row:row + 1, sl]
                diag[b][h] = contrib if diag[b][h] is None else diag[b][h] + contrib

    for h in range(HG_HEADS):
        sl = slice(h * HG_KEY, (h + 1) * HG_KEY)
        o_h = _dot_nt(q_in[:, sl], st_ref[h].astype(BF16))
        parts = [diag[b][h] for b in range(n_sub)]
        for late, early, q_l, k_e in scaled:
            sc = _dot_nt(q_l[:, sl], k_e[:, sl]).astype(BF16)
            add = _dot(sc, vb[early, sl])
            b0 = late.start // HG_SUB
            for j in range((late.stop - late.start) // HG_SUB):
                parts[b0 + j] = parts[b0 + j] + add[j * HG_SUB:(j + 1) * HG_SUB]
        o_ref[rows, sl] = o_h + jnp.concatenate(parts, axis=0)


def _gla_kernel(lb_ref, *refs, n_ctx_steps):
    lat = [refs[0:3], refs[3:6]]
    ctx = [refs[6:8], refs[8:10]]
    outs, states = refs[10:12], refs[12:14]
    s = pl.program_id(1)

    @pl.when(s == 0)
    def _():
        for st_ref in states:
            st_ref[...] = jnp.zeros_like(st_ref)

    lbs = []
    for d in range(2):
        x = lb_ref[d]
        e = jnp.exp(x - jnp.max(x, axis=0, keepdims=True))
        lbs.append(e[0:1] / jnp.sum(e, axis=0, keepdims=True))

    @pl.when(s < n_ctx_steps)
    def _():
        _gla_run([(ctx[d][0][...], None, ctx[d][1][...], lbs[d], states[d], None, d == 1) for d in range(2)])

    @pl.when(s >= n_ctx_steps)
    def _():
        _gla_run([(lat[d][0][...], lat[d][1][...], lat[d][2][...], lbs[d], states[d], outs[d], d == 1)
                  for d in range(2)])


def _hgrn2_scan(a_lat, f_lat, a_ctx, f_ctx, hg_lb, batch, n_seq, n_ctx):
    hk = HG_HEADS * HG_KEY
    rows = HG_CHUNK * HG_STEP_CHUNKS
    assert n_seq % rows == 0 and n_ctx % rows == 0
    nc, ncc = n_seq // rows, n_ctx // rows

    def lat(rev):
        def index(b, s):
            j = jnp.maximum(s - ncc, 0)
            return b * nc + (nc - 1 - j if rev else j)
        return index

    def ctx(rev):
        def index(b, s):
            j = jnp.minimum(s, ncc - 1)
            return b * ncc + (ncc - 1 - j if rev else j)
        return index

    def blk(index, col):
        return pl.BlockSpec((rows, hk), lambda b, s: (index(b, s), col))

    lat_specs, ctx_specs, out_specs = [], [], []
    for d, rev in enumerate((False, True)):
        lat_specs += [blk(lat(rev), d), blk(lat(rev), 2), blk(lat(rev), 3)]
        ctx_specs += [blk(ctx(rev), d), blk(ctx(rev), 1)]
        out_specs.append(blk(lat(rev), 0))
    return pl.pallas_call(
        functools.partial(_gla_kernel, n_ctx_steps=ncc),
        grid=(batch, ncc + nc),
        in_specs=[pl.BlockSpec((2, hg_lb.shape[1], hk), lambda b, s: (0, 0, 0))] + lat_specs + ctx_specs,
        out_specs=out_specs,
        out_shape=[jax.ShapeDtypeStruct((batch * n_seq, hk), F32)] * 2,
        scratch_shapes=[pltpu.VMEM((HG_HEADS, HG_KEY, HG_KEY), F32)] * 2,
        compiler_params=_params("parallel", "arbitrary"),
        name="hgrn2",
    )(hg_lb, f_lat, a_lat, a_lat, f_lat, a_lat, a_lat, f_ctx, a_ctx, f_ctx, a_ctx)


def _route(tok, wr_ref, rb):
    t_hi = tok.astype(BF16)
    t_lo = (tok - t_hi.astype(F32)).astype(BF16)
    hi_both = _dot(t_hi, wr_ref[...])
    lg = hi_both[:, :LANES] + hi_both[:, LANES:] + _dot(t_lo, wr_ref[:, :LANES]) + rb
    lane = lax.broadcasted_iota(jnp.int32, lg.shape, 1)
    lane_f = lane.astype(F32)
    ninf = -jnp.inf
    gl = jnp.where(lane < MOE_GROUPS, lg, ninf)
    gmax = jnp.max(gl, axis=-1, keepdims=True)
    g_idx = jnp.min(jnp.where(gl == gmax, lane_f, float(LANES)), axis=-1, keepdims=True)
    g_val = 1.0 / jnp.sum(jnp.exp(gl - gmax), axis=-1, keepdims=True)
    e_lane = lane_f - float(MOE_GROUPS)
    lo = g_idx * float(MOE_EXPERTS_PER_GROUP)
    in_grp = (e_lane >= lo) & (e_lane < lo + float(MOE_EXPERTS_PER_GROUP))
    el = jnp.where(in_grp, lg, ninf)
    l1 = jnp.max(el, axis=-1, keepdims=True)
    i1 = jnp.min(jnp.where(el == l1, e_lane, float(LANES)), axis=-1, keepdims=True)
    el2 = jnp.where(e_lane == i1, ninf, el)
    l2 = jnp.max(el2, axis=-1, keepdims=True)
    i2 = jnp.min(jnp.where(el2 == l2, e_lane, float(LANES)), axis=-1, keepdims=True)
    r = jnp.exp(l2 - l1)
    w1 = g_val / (1.0 + r)
    w2 = w1 * r
    eid = jnp.where(lane == 0, i1, jnp.where(lane == 1, i2, 0.0))
    gate = jnp.where(lane == 0, w1, jnp.where(lane == 1, w2, 0.0))
    eid_t = jnp.transpose(eid)[:MOE_ID_ROWS].astype(jnp.int32)
    return eid_t, gate


def _post_mix(y, x_ref, g1_ref, sc2_ref, sh2_ref, lng_ref, lnb_ref, wr_ref, rb_ref,
              x1_ref, tok_ref, eid_ref, gate_ref, r):
    x1 = _layer_norm(DEEPNORM_ALPHA * x_ref[r] + g1_ref[0] * y, lng_ref[...], lnb_ref[...])
    x1_ref[r] = x1
    tok = x1 * (1.0 + sc2_ref[0]) + sh2_ref[0]
    tok_ref[r] = _tiles_from_rows(tok).astype(BF16)
    eid, gate = _route(tok, wr_ref, rb_ref[...])
    eid_ref[:, r] = eid
    gate_ref[r] = gate


def _sub_tiles(n_rows):
    return [slice(s, s + SUB_ROWS) for s in range(0, n_rows, SUB_ROWS)]


def _even_out_kernel(att_ref, of_ref, ob_ref, gt_ref, ng_ref, wo_ref, *rest):
    tiles = _sub_tiles(att_ref.shape[0])
    lhs = []
    for r in tiles:
        o = of_ref[r] + ob_ref[r]
        pieces = []
        for h in range(HG_HEADS):
            oh = o[:, h * HG_KEY:(h + 1) * HG_KEY]
            pieces.append(oh * lax.rsqrt(jnp.mean(oh * oh, axis=-1, keepdims=True) + NORM_EPS))
        hg = (jnp.concatenate(pieces, axis=-1) * ng_ref[...] * _silu(gt_ref[r].astype(F32))).astype(BF16)
        lhs.append(jnp.concatenate([att_ref[r], hg], axis=-1))
    ys = [_dot(a, wo_ref[...]) for a in lhs]
    for r, y in zip(tiles, ys):
        _post_mix(y, *rest, r)


def _post_specs(d, tm, rows_per_batch):
    def bmap(i):
        return (i * tm // rows_per_batch, 0, 0)

    row = pl.BlockSpec((tm, d), lambda i: (i, 0))
    mod = pl.BlockSpec((1, 1, d), bmap)
    vec = pl.BlockSpec((1, d), lambda i: (0, 0))
    rw = pl.BlockSpec((d, 2 * LANES), lambda i: (0, 0))
    in_specs = [row, mod, mod, mod, vec, vec, rw, pl.BlockSpec((1, LANES), lambda i: (0, 0))]
    lane_blk = pl.BlockSpec((tm, LANES), lambda i: (i, 0))
    tiles = pl.BlockSpec((tm, d // LANES, LANES), lambda i: (i, 0, 0))
    out_specs = [row, tiles, pl.BlockSpec((MOE_ID_ROWS, tm), lambda i: (0, i)), lane_blk]
    return in_specs, out_specs


def _post_out_shapes(t, d):
    return [jax.ShapeDtypeStruct((t, d), F32), jax.ShapeDtypeStruct((t, d // LANES, LANES), BF16),
            jax.ShapeDtypeStruct((MOE_ID_ROWS, t), jnp.int32), jax.ShapeDtypeStruct((t, LANES), F32)]


def _resident(shape):
    return pl.BlockSpec(shape, lambda i: (0,) * len(shape), pipeline_mode=pl.Buffered(1))


def _even_out(att, o_f, o_b, p, norm_g, w_out, x, g1, sc2, sh2, lng, lnb, wr, rb, rows_per_batch,
              tm=ROW_TILE):
    t, d = x.shape
    hv = o_f.shape[1]
    post_in, post_out = _post_specs(d, tm, rows_per_batch)
    return pl.pallas_call(
        _even_out_kernel,
        grid=(t // tm,),
        in_specs=[
            pl.BlockSpec((tm, att.shape[1]), lambda i: (i, 0)),
            pl.BlockSpec((tm, hv), lambda i: (i, 0)),
            pl.BlockSpec((tm, hv), lambda i: (i, 0)),
            pl.BlockSpec((tm, hv), lambda i: (i, 4)),
            pl.BlockSpec((1, hv), lambda i: (0, 0)),
            _resident(w_out.shape),
        ] + post_in,
        out_specs=post_out,
        out_shape=_post_out_shapes(t, d),
        compiler_params=_params("parallel"),
        name="even_out",
    )(att, o_f, o_b, p, norm_g, w_out, x, g1, sc2, sh2, lng, lnb, wr, rb)


def _combine(x_ref, ya_ref, yb_ref, gate_ref, g2_ref, lng_ref, lnb_ref, r):
    gate = gate_ref[r]
    y = (gate[:, 0:1] * _rows_from_tiles(ya_ref[r].astype(F32))
         + gate[:, 1:2] * _rows_from_tiles(yb_ref[r].astype(F32)))
    return _layer_norm(DEEPNORM_ALPHA * x_ref[r] + g2_ref[0] * y, lng_ref[...], lnb_ref[...])


def _combine_proj_kernel(x_ref, ya_ref, yb_ref, gate_ref, g2_ref, lng_ref, lnb_ref, sc_ref, sh_ref, w_ref,
                         x2_ref, u_ref):
    tiles = _sub_tiles(x_ref.shape[0])
    lhs = []
    for r in tiles:
        x2 = _combine(x_ref, ya_ref, yb_ref, gate_ref, g2_ref, lng_ref, lnb_ref, r)
        x2_ref[r] = x2
        lhs.append((x2 * (1.0 + sc_ref[0]) + sh_ref[0]).astype(BF16))
    for r, a in zip(tiles, lhs):
        u_ref[r] = _dot(a, w_ref[...])


def _combine_kernel(x_ref, ya_ref, yb_ref, gate_ref, g2_ref, lng_ref, lnb_ref, x2_ref):
    for r in _sub_tiles(x_ref.shape[0]):
        x2_ref[r] = _combine(x_ref, ya_ref, yb_ref, gate_ref, g2_ref, lng_ref, lnb_ref, r)


def _combine_call(x1, y2, gate, g2, lng, lnb, rows_per_batch, proj=None, tm=ROW_TILE):
    t, d = x1.shape
    nt = t // tm

    def bmap(i):
        return (i * tm // rows_per_batch, 0, 0)

    row = pl.BlockSpec((tm, d), lambda i: (i, 0))
    mod = pl.BlockSpec((1, 1, d), bmap)
    vec = pl.BlockSpec((1, d), lambda i: (0, 0))
    in_specs = [row, pl.BlockSpec((tm, d // LANES, LANES), lambda i: (i, 0, 0)),
                pl.BlockSpec((tm, d // LANES, LANES), lambda i: (nt + i, 0, 0)),
                pl.BlockSpec((tm, LANES), lambda i: (i, 0)), mod, vec, vec]
    args = [x1, y2, y2, gate, g2, lng, lnb]
    if proj is None:
        return pl.pallas_call(
            _combine_kernel, grid=(nt,), in_specs=in_specs, out_specs=row,
            out_shape=jax.ShapeDtypeStruct((t, d), F32),
            compiler_params=_params("parallel"), name="combine_ln")(*args)
    sc, sh, w = proj
    return pl.pallas_call(
        _combine_proj_kernel, grid=(nt,),
        in_specs=in_specs + [mod, mod, _resident(w.shape)],
        out_specs=[row, pl.BlockSpec((tm, w.shape[1]), lambda i: (i, 0))],
        out_shape=[jax.ShapeDtypeStruct((t, d), F32), jax.ShapeDtypeStruct((t, w.shape[1]), F32)],
        compiler_params=_params("parallel"), name="combine_ln_proj")(*args, sc, sh, w)


def _pool_out_kernel(up_ref, uc_ref, un_ref, wg_ref, ps_ref, wo_ref, *rest, n_seq):
    tm, d = uc_ref.shape
    n_grp = len(POOL_WINDOWS)
    ch = d // n_grp
    halo = POOL_HALO
    tiles = _sub_tiles(tm)
    lhs = []
    for r in tiles:
        n_r = r.stop - r.start
        pos0 = (pl.program_id(0) * tm + r.start) % n_seq
        e_pos = pos0 - halo + lax.broadcasted_iota(jnp.int32, (n_r + 2 * halo, 1), 0)
        e_ok = (e_pos >= 0) & (e_pos < n_seq)
        t_pos = pos0 + lax.broadcasted_iota(jnp.int32, (n_r, 1), 0)
        z = []
        for gi, w in enumerate(POOL_WINDOWS):
            cs = slice(gi * ch, (gi + 1) * ch)
            u = uc_ref[r, cs]
            before = up_ref[:, cs] if r.start == 0 else uc_ref[r.start - halo:r.start, cs]
            after = un_ref[:, cs] if r.stop == tm else uc_ref[r.stop:r.stop + halo, cs]
            ext = jnp.where(e_ok, jnp.concatenate([before, u, after], axis=0), 0.0)
            a, span = ext, 1
            while span < w:
                a = a[:a.shape[0] - span] + a[span:]
                span *= 2
            start = halo - w // 2
            win = a[start:start + n_r]
            cnt = (jnp.minimum(t_pos + (w - w // 2), n_seq) - jnp.maximum(t_pos - w // 2, 0)).astype(F32)
            mixed = (win / cnt - u).astype(BF16)
            z.append((_dot(mixed, wg_ref[gi]) * ps_ref[:, cs]).astype(BF16))
        lhs.append(jnp.concatenate(z, axis=-1))
    ys = [_dot(a, wo_ref[...]) for a in lhs]
    for r, y in zip(tiles, ys):
        _post_mix(y, *rest, r)


def _pool_out(u, w_grp, scale, w_out, x, g1, sc2, sh2, lng, lnb, wr, rb, n_seq, tm=ROW_TILE):
    t, d = x.shape
    hb = tm // POOL_HALO
    n_hb = t // POOL_HALO
    post_in, post_out = _post_specs(d, tm, n_seq)
    return pl.pallas_call(
        functools.partial(_pool_out_kernel, n_seq=n_seq),
        grid=(t // tm,),
        in_specs=[
            pl.BlockSpec((POOL_HALO, d), lambda i: (jnp.maximum(i * hb - 1, 0), 0)),
            pl.BlockSpec((tm, d), lambda i: (i, 0)),
            pl.BlockSpec((POOL_HALO, d), lambda i: (jnp.minimum((i + 1) * hb, n_hb - 1), 0)),
            _resident(w_grp.shape),
            pl.BlockSpec((1, d), lambda i: (0, 0)),
            _resident(w_out.shape),
        ] + post_in,
        out_specs=post_out,
        out_shape=_post_out_shapes(t, d),
        compiler_params=_params("parallel"),
        name="pool_out",
    )(u, u, u, w_grp, scale, w_out, x, g1, sc2, sh2, lng, lnb, wr, rb)


def _moe_kernel(be_ref, nv_ref, first_ref, ws_ref, nxt_ref, idx_ref, idxn_ref, idxp_ref, tok_hbm, w1_hbm, w3_hbm, w2_hbm,
                y_hbm, xbuf, ybuf, xb_ref, wf1, wf3, wf2, w1b, w3b, w2b, gsem, ssem, wsem, *, n_tok, layer, n_blocks):
    i = pl.program_id(0)
    used = nv_ref[jnp.minimum(i, n_blocks - 1)] > 0
    used = used & (i < n_blocks)
    prev_used = (i > 0) & (nv_ref[jnp.maximum(i - 1, 0)] > 0)
    xs = i % 2

    def weight_copies(e, ws):
        return (pltpu.make_async_copy(w1_hbm.at[layer, e], wf1.at[ws], wsem.at[ws]),
                pltpu.make_async_copy(w3_hbm.at[layer, e], wf3.at[ws], wsem.at[ws]),
                pltpu.make_async_copy(w2_hbm.at[layer, e], wf2.at[ws], wsem.at[ws]))

    def gather_start(idx, slot):
        for r in range(MOE_ROWS):
            tok = idx[0, 0, r] & (n_tok - 1)
            pltpu.make_async_copy(tok_hbm.at[tok], xbuf.at[slot, r], gsem.at[slot]).start()

    def gather_wait(slot):
        pltpu.make_async_copy(tok_hbm.at[pl.ds(0, MOE_ROWS)], xbuf.at[slot], gsem.at[slot]).wait()

    def scatter_start(idx, slot):
        for r in range(MOE_ROWS):
            pltpu.make_async_copy(ybuf.at[slot, r], y_hbm.at[idx[0, 0, r]], ssem.at[slot]).start(priority=r % 2)

    def scatter_wait(slot):
        pltpu.make_async_copy(ybuf.at[slot], y_hbm.at[pl.ds(0, MOE_ROWS)], ssem.at[slot]).wait()

    @pl.when(i == 0)
    def _():
        xbuf[...] = jnp.zeros_like(xbuf)
        ybuf[...] = jnp.zeros_like(ybuf)
        spare0 = pltpu.make_async_copy(
            ybuf.at[0], y_hbm.at[pl.ds(MOE_TOP_K * n_tok, MOE_ROWS)], ssem.at[0])
        spare0.start()
        for cp in weight_copies(be_ref[0], 0):
            cp.start(priority=WEIGHT_DMA_PRIORITY)
        gather_start(idx_ref, 0)

    @pl.when(used)
    def _():
        ws = ws_ref[i]

        @pl.when(first_ref[i] == 1)
        def _():
            for cp in weight_copies(be_ref[i], ws):
                cp.wait()
            nxt = nxt_ref[i]

            @pl.when(nxt >= 0)
            def _():
                for cp in weight_copies(nxt, 1 - ws):
                    cp.start(priority=WEIGHT_DMA_PRIORITY)

            w1b[...] = wf1[ws].astype(BF16)
            w3b[...] = wf3[ws].astype(BF16)
            w2b[...] = wf2[ws].astype(BF16)

        gather_wait(xs)
        xb_ref[...] = _rows_from_tiles(xbuf[xs].astype(F32)).astype(BF16)
        gather_start(idxn_ref, 1 - xs)
        scatter_start(idxp_ref, 1 - xs)
        xb = xb_ref[...]
        h = (_silu(_dot(xb, w1b[...])) * _dot(xb, w3b[...])).astype(BF16)
        y = _tiles_from_rows(_dot(h, w2b[...])).astype(BF16)
        scatter_wait(xs)
        ybuf[xs] = y

    @pl.when(jnp.logical_not(used) & prev_used)
    def _():
        gather_wait(xs)
        scatter_start(idxp_ref, 1 - xs)
        scatter_wait(1 - xs)
        scatter_wait(xs)


def _moe_dispatch(eid_t, n_blocks):
    n_tok = eid_t.shape[1]
    n_slots = (n_blocks + 2) * MOE_ROWS
    slot_rows = -(-n_slots // SMEM_1D_TILE)
    assert n_blocks <= LANES and n_tok % SMEM_1D_TILE == 0 and SMEM_1D_TILE % MOE_ROWS == 0
    slot, meta = pl.pallas_call(
        functools.partial(_dispatch_kernel, n_tok=n_tok),
        in_specs=[pl.BlockSpec(memory_space=pltpu.VMEM)],
        out_specs=[pl.BlockSpec(memory_space=pltpu.SMEM), pl.BlockSpec(memory_space=pltpu.VMEM)],
        out_shape=[jax.ShapeDtypeStruct((slot_rows * SMEM_1D_TILE,), jnp.int32),
                   jax.ShapeDtypeStruct((MOE_ID_ROWS, LANES), jnp.int32)],
        scratch_shapes=[pltpu.VMEM((MOE_TOP_K * n_tok,), jnp.int32),
                        pltpu.VMEM((slot_rows * SMEM_1D_TILE,), jnp.int32),
                        pltpu.SMEM((MOE_TOP_K * n_tok,), jnp.int32),
                        pltpu.SemaphoreType.DMA(())],
        name="moe_dispatch",
    )(eid_t)
    return (slot[:n_slots].reshape(n_blocks + 2, 1, MOE_ROWS),) + tuple(meta[r, :n_blocks] for r in range(5))


def _dispatch_kernel(eid_ref, slot_ref, meta_ref, dest_vmem, init_vmem, dest_smem, sem, *, n_tok):
    tile = MOE_ROWS
    n_tiles = n_tok // tile
    sub = lax.broadcasted_iota(jnp.int32, (N_EXPERTS, tile), 0)
    si = lax.broadcasted_iota(jnp.int32, (tile, tile), 0)
    ti = lax.broadcasted_iota(jnp.int32, (tile, tile), 1)
    before = jnp.where(si < ti, 1.0, 0.0).astype(BF16)

    def one_hots(j):
        ids = eid_ref[:, j * tile:(j + 1) * tile]
        return [jnp.where(sub == ids[k:k + 1], 1.0, 0.0) for k in range(MOE_TOP_K)]

    carry = jnp.zeros((N_EXPERTS, 1), F32)
    ranks = []
    for j in range(n_tiles):
        oh = one_hots(j)
        both = oh[0] + oh[1]
        seen = carry + _dot(both.astype(BF16), before)
        ranks.append([jnp.sum(seen * o, axis=0, keepdims=True) for o in oh])
        carry = carry + jnp.sum(both, axis=1, keepdims=True)

    counts = carry
    nblk = jnp.floor((counts + float(MOE_ROWS - 1)) * (1.0 / MOE_ROWS))
    ei = lax.broadcasted_iota(jnp.int32, (N_EXPERTS, N_EXPERTS), 0)
    ej = lax.broadcasted_iota(jnp.int32, (N_EXPERTS, N_EXPERTS), 1)
    lower = jnp.where(ej < ei, 1.0, 0.0).astype(BF16)
    first_blk = _dot(lower, jnp.broadcast_to(nblk, (N_EXPERTS, LANES)).astype(BF16))[:, 0:1]
    first_slot = first_blk * float(MOE_ROWS)

    per_row = SMEM_1D_TILE // tile
    for k in range(MOE_TOP_K):
        for q in range(n_tiles // per_row):
            parts = []
            for j in range(q * per_row, (q + 1) * per_row):
                parts.append(jnp.sum(first_slot * one_hots(j)[k], axis=0, keepdims=True) + ranks[j][k])
            dest = jnp.concatenate(parts, axis=1).astype(jnp.int32)
            dest_vmem[pl.ds(k * n_tok + q * SMEM_1D_TILE, SMEM_1D_TILE)] = dest.reshape(SMEM_1D_TILE)

    lane = lax.broadcasted_iota(jnp.int32, (1, SMEM_1D_TILE), 1)
    for q in range(init_vmem.shape[0] // SMEM_1D_TILE):
        pos = q * SMEM_1D_TILE + lane
        spare = MOE_TOP_K * n_tok + ((pos // MOE_ROWS + 1) % 2) * MOE_ROWS + pos % MOE_ROWS
        init_vmem[pl.ds(q * SMEM_1D_TILE, SMEM_1D_TILE)] = spare.reshape(SMEM_1D_TILE)
    copies = [pltpu.make_async_copy(dest_vmem, dest_smem, sem), pltpu.make_async_copy(init_vmem, slot_ref, sem)]
    for cp in copies:
        cp.start()
    for cp in copies:
        cp.wait()

    def place(t, carry_):
        for k in range(MOE_TOP_K):
            slot_ref[dest_smem[k * n_tok + t] + MOE_ROWS] = k * n_tok + t
        return carry_

    lax.fori_loop(0, n_tok, place, 0, unroll=8)

    b = lax.broadcasted_iota(jnp.int32, (N_EXPERTS, LANES), 1).astype(F32)
    e_col = lax.broadcasted_iota(jnp.int32, (N_EXPERTS, LANES), 0).astype(F32)
    b_row = b[0:1]
    be = jnp.minimum(jnp.sum(jnp.where(first_blk + nblk <= b, 1.0, 0.0), axis=0, keepdims=True), N_EXPERTS - 1.0)
    mine = e_col == be
    cnt_b = jnp.sum(jnp.where(mine, counts, 0.0), axis=0, keepdims=True)
    start_b = jnp.sum(jnp.where(mine, first_blk, 0.0), axis=0, keepdims=True)
    nv = jnp.clip(cnt_b - (b_row - start_b) * MOE_ROWS, 0.0, float(MOE_ROWS))
    nv = jnp.where(b_row < jnp.sum(nblk, axis=0, keepdims=True), nv, 0.0)
    first = jnp.where((nv > 0) & ((b_row == 0) | (be != pltpu.roll(be, 1, 1))), 1.0, 0.0)
    li = lax.broadcasted_iota(jnp.int32, (LANES, LANES), 0)
    lj = lax.broadcasted_iota(jnp.int32, (LANES, LANES), 1)
    upto = jnp.where(li <= lj, 1.0, 0.0).astype(BF16)
    run = _dot(jnp.broadcast_to(first, (MOE_ID_ROWS, LANES)).astype(BF16), upto)[0:1] - 1.0
    ws = run - 2.0 * jnp.floor(run * 0.5)
    later = jnp.min(jnp.where((e_col > be) & (counts > 0), e_col, float(LANES)), axis=0, keepdims=True)
    nxt = jnp.where(later >= float(N_EXPERTS), -1.0, later)
    rows = [be, nv, first, ws, nxt] + [jnp.zeros_like(be)] * (MOE_ID_ROWS - 5)
    meta_ref[...] = jnp.concatenate(rows, axis=0).astype(jnp.int32)


def _moe_experts(tok, eid, w1, w3, w2, layer):
    n_tok, n_sub, _ = tok.shape
    d = n_sub * LANES
    assert n_tok & (n_tok - 1) == 0
    ff = w1.shape[3]
    n_assign = n_tok * MOE_TOP_K
    n_blocks = -(-(n_assign + N_EXPERTS * (MOE_ROWS - 1)) // MOE_ROWS)
    slot, be, nv, first, ws, nxt = _moe_dispatch(eid, n_blocks)
    grid_spec = pltpu.PrefetchScalarGridSpec(
        num_scalar_prefetch=5,
        grid=(n_blocks + 1,),
        in_specs=[
            pl.BlockSpec((1, 1, MOE_ROWS), lambda i, *_: (i + 1, 0, 0), memory_space=pltpu.SMEM),
            pl.BlockSpec((1, 1, MOE_ROWS), lambda i, *_: (jnp.minimum(i + 2, n_blocks + 1), 0, 0),
                         memory_space=pltpu.SMEM),
            pl.BlockSpec((1, 1, MOE_ROWS), lambda i, *_: (i, 0, 0), memory_space=pltpu.SMEM),
            pl.BlockSpec(memory_space=pl.ANY),
            pl.BlockSpec(memory_space=pl.ANY),
            pl.BlockSpec(memory_space=pl.ANY),
            pl.BlockSpec(memory_space=pl.ANY),
        ],
        out_specs=pl.BlockSpec(memory_space=pl.ANY),
        scratch_shapes=[
            pltpu.VMEM((2, MOE_ROWS, n_sub, LANES), BF16), pltpu.VMEM((2, MOE_ROWS, n_sub, LANES), BF16),
            pltpu.VMEM((MOE_ROWS, d), BF16),
            pltpu.VMEM((2, d, ff), F32), pltpu.VMEM((2, d, ff), F32), pltpu.VMEM((2, ff, d), F32),
            pltpu.VMEM((d, ff), BF16), pltpu.VMEM((d, ff), BF16), pltpu.VMEM((ff, d), BF16),
            pltpu.SemaphoreType.DMA((2,)), pltpu.SemaphoreType.DMA((2,)), pltpu.SemaphoreType.DMA((2,)),
        ],
    )
    return pl.pallas_call(
        functools.partial(_moe_kernel, n_tok=n_tok, layer=layer, n_blocks=n_blocks),
        grid_spec=grid_spec,
        out_shape=jax.ShapeDtypeStruct((MOE_TOP_K * n_tok + 2 * MOE_ROWS, n_sub, LANES), BF16),
        compiler_params=pltpu.CompilerParams(dimension_semantics=("arbitrary",),
                                             vmem_limit_bytes=MOE_VMEM_LIMIT_BYTES),
        name="moe_experts",
    )(be, nv, first, ws, nxt, slot, slot, slot, tok, w1, w3, w2)


def _rope_tables(n_seq):
    half = HEAD_DIM // 2
    n_freq = half // 2
    t = jnp.arange(n_seq)
    row = (t // GRID_W).astype(F32)
    col = (t % GRID_W).astype(F32)
    inv_freq = ROPE_BASE ** (-jnp.arange(n_freq, dtype=F32) / n_freq)
    ang_r = row[:, None] * inv_freq[None, :]
    ang_c = col[:, None] * inv_freq[None, :]
    cos = jnp.concatenate([jnp.cos(ang_r)] * 2 + [jnp.cos(ang_c)] * 2, axis=-1)
    sin = jnp.concatenate([-jnp.sin(ang_r), jnp.sin(ang_r), -jnp.sin(ang_c), jnp.sin(ang_c)], axis=-1)
    return cos, sin


def _router_weights(w_g, b_g, w_e, b_e):
    d = w_g.shape[0]
    n = w_g.shape[1] + w_e.shape[1]
    wr = jnp.concatenate([w_g, w_e, jnp.zeros((d, LANES - n), F32)], axis=1)
    rb = jnp.concatenate([b_g, b_e, jnp.zeros((LANES - n,), F32)]).reshape(1, LANES)
    hi = wr.astype(BF16)
    lo = (wr - hi.astype(F32)).astype(BF16)
    return jnp.concatenate([hi, lo], axis=1), rb


def kernel(x, c, ctx, c_ctx, ada_w, ada_b, ln_g, ln_b, mix_w_in, att_sink, hg_lb, hg_norm_g, mix_w_out, pool_w_in, pool_w_grp, pool_scale, pool_w_out, rt_group_w, rt_group_b, rt_expert_w, rt_expert_b, moe_w1, moe_w3, moe_w2):
    b, n, d = x.shape
    n_ctx = ctx.shape[1]
    t = b * n
    xf = x.reshape(t, d)
    ctxf = ctx.reshape(b * n_ctx, d)

    cond = jnp.concatenate([c, c_ctx[None, :], jnp.zeros((8 - b - 1, d), F32)], axis=0)
    mod = _ada_mod(cond, ada_w, ada_b)

    def chunk(l, j, rows=slice(0, b)):
        return mod[l, rows, j * d:(j + 1) * d][:, None, :]

    w_in = mix_w_in[0].astype(BF16)
    cos, sin = _rope_tables(n)
    q_w, kv_w = ATT_HEADS * HEAD_DIM, ATT_KV_HEADS * HEAD_DIM
    n_att = q_w + 2 * kv_w
    hk = HG_HEADS * HG_KEY
    assert n_att == 2 * PROJ_TN and hk == PROJ_TN
    a_lat, f_lat = _mod_matmul(xf, chunk(0, 1), chunk(0, 0), w_in, cos, sin,
                               lambda j: jnp.where(j < 3, j, jnp.where(j < 5, j + 2, j - 2)), n_att + 3 * hk, 2 * hk,
                               n_q=q_w, n_rope=q_w + kv_w, n_seq=n, tm=1024, tn=PROJ_TN)
    ctx_rows = slice(b, b + 1)
    a_ctx, f_ctx = _mod_matmul(ctxf, chunk(0, 1, ctx_rows), chunk(0, 0, ctx_rows), w_in, cos, sin,
                               lambda j: jnp.where(j < 1, 1, jnp.where(j < 2, 5, j + 1)), 2 * kv_w + hk, 2 * hk,
                               n_q=0, n_rope=0, n_seq=n, tm=b * n_ctx, tn=PROJ_TN)
    att = _window_attention(a_lat, a_ctx, att_sink[0], b, n, n_ctx)
    o_f, o_b = _hgrn2_scan(a_lat, f_lat, a_ctx, f_ctx, hg_lb, b, n, n_ctx)
    wr, rb = _router_weights(rt_group_w[0], rt_group_b[0], rt_expert_w[0], rt_expert_b[0])
    x1, tok, eid, gate = _even_out(
        att, o_f, o_b, a_lat, hg_norm_g[0][None, :], mix_w_out[0].astype(BF16), xf,
        chunk(0, 2), chunk(0, 4), chunk(0, 3), ln_g[0, 0][None, :], ln_b[0, 0][None, :], wr, rb, n)
    y2 = _moe_experts(tok, eid, moe_w1, moe_w3, moe_w2, 0)

    x2, u = _combine_call(x1, y2, gate, chunk(0, 5), ln_g[0, 1][None, :], ln_b[0, 1][None, :], n,
                          proj=(chunk(1, 1), chunk(1, 0), pool_w_in[0].astype(BF16)))
    wr, rb = _router_weights(rt_group_w[1], rt_group_b[1], rt_expert_w[1], rt_expert_b[1])
    x3, tok, eid, gate = _pool_out(
        u, pool_w_grp[0].astype(BF16), pool_scale[0][None, :], pool_w_out[0].astype(BF16), x2,
        chunk(1, 2), chunk(1, 4), chunk(1, 3), ln_g[1, 0][None, :], ln_b[1, 0][None, :], wr, rb, n)
    y2 = _moe_experts(tok, eid, moe_w1, moe_w3, moe_w2, 1)
    out = _combine_call(x3, y2, gate, chunk(1, 5), ln_g[1, 1][None, :], ln_b[1, 1][None, :], n)
    return out.reshape(b, n, d)
```

```python
import functools

import jax
import jax.numpy as jnp
from jax import lax
from jax.experimental import pallas as pl
from jax.experimental.pallas import tpu as pltpu

F32 = jnp.float32
BF16 = jnp.bfloat16

LANES = 128
MXU_TILE = 256
VMEM_LIMIT_BYTES = 56 * 1024 * 1024
MOE_VMEM_LIMIT_BYTES = 60 * 1024 * 1024

GRID_W = 64
ATT_HEADS = 8
ATT_KV_HEADS = 4
ATT_GROUP = ATT_HEADS // ATT_KV_HEADS
HEAD_DIM = 128
WINDOW = 128
ATT_BLOCK = 128
ROPE_BASE = 10000.0
HG_HEADS = 8
HG_KEY = 128
HG_CHUNK = 64
HG_STEP_CHUNKS = 4
HG_SUB = 16
HG_FAST_RANGE = 80.0
NORM_EPS = 1e-6
POOL_WINDOWS = (2, 4, 8, 16)
POOL_HALO = 8
MOE_GROUPS = 4
MOE_EXPERTS_PER_GROUP = 8
N_EXPERTS = MOE_GROUPS * MOE_EXPERTS_PER_GROUP
MOE_TOP_K = 2
MOE_ROWS = 256
ROW_TILE = 512
SUB_ROWS = 256
PROJ_TN = 1024
MOE_ID_ROWS = 8
SMEM_1D_TILE = 1024
WEIGHT_DMA_PRIORITY = 1
LN_EPS = 1e-5
DEPTH = 2
DEEPNORM_ALPHA = (2 * DEPTH) ** 0.25


def _dot(a, b):
    return jnp.dot(a, b, preferred_element_type=F32)


def _dot_nt(a, b):
    return lax.dot_general(a, b, (((1,), (1,)), ((), ())), preferred_element_type=F32)


def _dot_tn(a, b):
    return lax.dot_general(a, b, (((0,), (0,)), ((), ())), preferred_element_type=F32)


def _sigmoid(x):
    return 1.0 / (1.0 + jnp.exp(-x))


def _silu(x):
    return x * _sigmoid(x)


def _params(*sem):
    return pltpu.CompilerParams(dimension_semantics=sem, vmem_limit_bytes=VMEM_LIMIT_BYTES)


def _tiles_from_rows(x):
    n = x.shape[1] // LANES
    return jnp.swapaxes(jnp.stack([x[:, s * LANES:(s + 1) * LANES] for s in range(n)], axis=0), 0, 1)


def _rows_from_tiles(x3):
    xt = jnp.swapaxes(x3, 0, 1)
    return jnp.concatenate([xt[s] for s in range(xt.shape[0])], axis=-1)


def _layer_norm(z, g, b):
    mu = jnp.mean(z, axis=-1, keepdims=True)
    zc = z - mu
    var = jnp.mean(zc * zc, axis=-1, keepdims=True)
    return zc * lax.rsqrt(var + LN_EPS) * g + b


def _ada_kernel(s_ref, w_ref, b_ref, o_ref):
    s = _silu(s_ref[...]).astype(BF16)
    o_ref[0] = _dot(s, w_ref[0].astype(BF16)) + b_ref[0]


def _ada_mod(s, ada_w, ada_b, tn=1024):
    n_l, d, n = ada_w.shape
    return pl.pallas_call(
        _ada_kernel,
        grid=(n_l, n // tn),
        in_specs=[
            pl.BlockSpec((8, d), lambda l, j: (0, 0)),
            pl.BlockSpec((1, d, tn), lambda l, j: (l, 0, j)),
            pl.BlockSpec((1, 1, tn), lambda l, j: (l, 0, j)),
        ],
        out_specs=pl.BlockSpec((1, 8, tn), lambda l, j: (l, 0, j)),
        out_shape=jax.ShapeDtypeStruct((n_l, 8, n), F32),
        compiler_params=_params("parallel", "parallel"),
        name="ada_mod",
    )(s, ada_w, ada_b.reshape(n_l, 1, n))


def _rope(t, cos, sin_signed, first_half):
    partner = jnp.where(first_half, pltpu.roll(t, 96, 1), pltpu.roll(t, 32, 1))
    return t * cos + partner * sin_signed


def _modmm_kernel(x_ref, sc_ref, sh_ref, w_ref, cos_ref, sin_ref, oa_ref, ob_ref, xs_ref, *,
                  n_q, n_rope, n_a_tiles):
    j = pl.program_id(1)
    tm, tn = x_ref.shape[0], w_ref.shape[1]
    halves = [slice(0, tm // 2), slice(tm // 2, tm)]

    def dots():
        return [_dot(xs_ref[r], w_ref[...]) for r in halves]

    def store_bf16(jt, r, acc):
        lane = lax.broadcasted_iota(jnp.int32, (1, HEAD_DIM), 1)
        first_half = (lane % 64) < 32
        for h in range(tn // HEAD_DIM):
            sl = slice(h * HEAD_DIM, (h + 1) * HEAD_DIM)
            col = jt * tn + h * HEAD_DIM
            if col < n_rope:
                scale = HEAD_DIM ** -0.5 if col < n_q else 1.0
                oa_ref[r, sl] = _rope(acc[:, sl], cos_ref[r] * scale, sin_ref[r] * scale, first_half).astype(BF16)
            else:
                oa_ref[r, sl] = acc[:, sl].astype(BF16)

    for jt in range(n_a_tiles):
        @pl.when(j == jt)
        def _(jt=jt):
            if jt == 0:
                for r in halves:
                    xs_ref[r] = (x_ref[r] * (1.0 + sc_ref[0]) + sh_ref[0]).astype(BF16)
            for r, acc in zip(halves, dots()):
                store_bf16(jt, r, acc)

    @pl.when(j >= n_a_tiles)
    def _():
        for r, acc in zip(halves, dots()):
            ob_ref[r] = acc


def _mod_matmul(x, sc, sh, w, cos, sin, col_map, n_a, n_b, n_q, n_rope, n_seq, tm, tn):
    m, k = x.shape
    rows_per_mod = m // sc.shape[0]
    ta, tb = n_a // tn, n_b // tn
    tab = pl.BlockSpec((tm, HEAD_DIM), lambda i, j: ((i * tm % n_seq) // tm, 0))
    return pl.pallas_call(
        functools.partial(_modmm_kernel, n_q=n_q, n_rope=n_rope, n_a_tiles=ta),
        grid=(m // tm, ta + tb),
        in_specs=[
            pl.BlockSpec((tm, k), lambda i, j: (i, 0)),
            pl.BlockSpec((1, 1, k), lambda i, j: (i * tm // rows_per_mod, 0, 0)),
            pl.BlockSpec((1, 1, k), lambda i, j: (i * tm // rows_per_mod, 0, 0)),
            pl.BlockSpec((k, tn), lambda i, j: (0, col_map(j))),
            tab, tab,
        ],
        out_specs=[pl.BlockSpec((tm, tn), lambda i, j: (i, jnp.minimum(j, ta - 1))),
                   pl.BlockSpec((tm, tn), lambda i, j: (i, jnp.maximum(j - ta, 0)))],
        out_shape=[jax.ShapeDtypeStruct((m, n_a), BF16), jax.ShapeDtypeStruct((m, n_b), F32)],
        scratch_shapes=[pltpu.VMEM((tm, k), BF16)],
        compiler_params=_params("parallel", "arbitrary"),
        name="mod_matmul",
    )(x, sc, sh, w, cos, sin)


def _attn_kernel(sink_ref, q_ref, kp_ref, kc_ref, kn_ref, vp_ref, vc_ref, vn_ref, kx_ref, vx_ref,
                 mp_ref, mn_ref, o_ref, *, n_blocks):
    s_idx = pl.program_id(1)
    blk = ATT_BLOCK
    row1 = lax.broadcasted_iota(jnp.int32, (ATT_GROUP * blk, 1), 0)
    lo, hi = slice(0, blk), slice(blk, 2 * blk)
    subs = [(lo, (kp_ref, vp_ref, lo), (kc_ref, vc_ref, lo), (kc_ref, vc_ref, hi), s_idx > 0, True),
            (hi, (kc_ref, vc_ref, lo), (kc_ref, vc_ref, hi), (kn_ref, vn_ref, lo), True, s_idx < n_blocks // 2 - 1)]
    jobs = [(sub, h) for sub in subs for h in range(ATT_KV_HEADS)]

    def kv(h):
        return slice(h * HEAD_DIM, (h + 1) * HEAD_DIM)

    def gather(sub, h, which):
        blocks = [t[which][t[2], kv(h)] for t in sub[1:4]]
        return jnp.concatenate(blocks + [(kx_ref, vx_ref)[which][:, kv(h)]], axis=0)

    scores = []
    for sub, h in jobs:
        q2 = jnp.concatenate([q_ref[sub[0], (ATT_GROUP * h + g) * HEAD_DIM:(ATT_GROUP * h + g + 1) * HEAD_DIM]
                              for g in range(ATT_GROUP)], axis=0)
        scores.append(_dot_nt(q2, gather(sub, h, 0)))
    probs, dens = [], []
    for (sub, h), s in zip(jobs, scores):
        has_prev, has_next = sub[4], sub[5]
        parts = [jnp.where(has_prev, s[:, :blk] + mp_ref[...], -jnp.inf), s[:, blk:2 * blk],
                 jnp.where(has_next, s[:, 2 * blk:3 * blk] + mn_ref[...], -jnp.inf)]
        parts += [s[:, c:c + blk] for c in range(3 * blk, s.shape[1], blk)]
        sink = jnp.where(row1 < blk, sink_ref[ATT_GROUP * h], sink_ref[ATT_GROUP * h + 1])
        m = jnp.maximum(jnp.max(functools.reduce(jnp.maximum, parts), axis=-1, keepdims=True), sink)
        p = [jnp.exp(x - m) for x in parts]
        dens.append(jnp.sum(functools.reduce(jnp.add, p), axis=-1, keepdims=True) + jnp.exp(sink - m))
        probs.append(jnp.concatenate(p, axis=-1).astype(BF16))
    for (sub, h), p, den in zip(jobs, probs, dens):
        o = _dot(p, gather(sub, h, 1)) / den
        for g in range(ATT_GROUP):
            col = (ATT_GROUP * h + g) * HEAD_DIM
            o_ref[sub[0], col:col + HEAD_DIM] = o[g * blk:(g + 1) * blk].astype(o_ref.dtype)


def _window_attention(qkv, kv_ctx, sink, batch, n_seq, n_ctx):
    assert ATT_GROUP == 2 and WINDOW == ATT_BLOCK
    nb = n_seq // ATT_BLOCK
    qw, kw = ATT_HEADS * HEAD_DIM, ATT_KV_HEADS * HEAD_DIM
    kcol, vcol = qw // kw, qw // kw + 1

    assert nb % 2 == 0
    pair = 2 * ATT_BLOCK

    def before(b, s):
        return b * nb + jnp.maximum(2 * s - 1, 0)

    def after(b, s):
        return b * nb + jnp.minimum(2 * s + 2, nb - 1)

    def own(b, s):
        return b * (nb // 2) + s

    def band(col):
        return [pl.BlockSpec((ATT_BLOCK, kw), lambda b, s: (before(b, s), col)),
                pl.BlockSpec((pair, kw), lambda b, s: (own(b, s), col)),
                pl.BlockSpec((ATT_BLOCK, kw), lambda b, s: (after(b, s), col))]

    kspec, vspec = band(kcol), band(vcol)
    r = jnp.arange(ATT_GROUP * ATT_BLOCK)[:, None] % ATT_BLOCK
    c = jnp.arange(ATT_BLOCK)[None, :]
    mask_prev = jnp.where(c >= r, 0.0, -jnp.inf).astype(F32)
    mask_next = jnp.where(c <= r, 0.0, -jnp.inf).astype(F32)
    mspec = pl.BlockSpec(mask_prev.shape, lambda b, s: (0, 0))
    return pl.pallas_call(
        functools.partial(_attn_kernel, n_blocks=nb),
        grid=(batch, nb // 2),
        in_specs=[pl.BlockSpec(memory_space=pltpu.SMEM),
                  pl.BlockSpec((pair, qw), lambda b, s: (own(b, s), 0))]
        + kspec + vspec
        + [pl.BlockSpec((n_ctx, kw), lambda b, s: (b, 0)), pl.BlockSpec((n_ctx, kw), lambda b, s: (b, 1)),
           mspec, mspec],
        out_specs=pl.BlockSpec((pair, qw), lambda b, s: (own(b, s), 0)),
        out_shape=jax.ShapeDtypeStruct((batch * n_seq, qw), BF16),
        compiler_params=_params("parallel", "parallel"),
        name="window_attention",
    )(sink, qkv, qkv, qkv, qkv, qkv, qkv, qkv, kv_ctx, kv_ctx, mask_prev, mask_next)


def _gla_step(zf, q_raw, v, lb, st_ref, o_ref, rev):
    c_len = HG_CHUNK
    n_rows = zf.shape[0]
    n_sub = n_rows // c_len
    shift = c_len.bit_length() - 1
    order = range(n_sub - 1, -1, -1) if rev else range(n_sub)

    def head(h):
        return slice(h * HG_KEY, (h + 1) * HG_KEY)

    def chunk(i):
        return slice(i * c_len, (i + 1) * c_len)

    def seen(n):
        ri = lax.broadcasted_iota(jnp.int32, (n, n), 0)
        ci = lax.broadcasted_iota(jnp.int32, (n, n), 1)
        return ((ri >> shift) == (ci >> shift)) & ((ci >= ri) if rev else (ci <= ri))

    f = lb + (1.0 - lb) * _sigmoid(zf)
    k = 1.0 - f
    g = jnp.log(f)
    tri = jnp.where(seen(n_rows), 1.0, 0.0).astype(BF16)
    g1 = g.astype(BF16)
    r1 = g - g1.astype(F32)
    g2 = r1.astype(BF16)
    g3 = (r1 - g2.astype(F32)).astype(BF16)
    c = _dot(tri, g1) + _dot(tri, g2) + _dot(tri, g3)
    c_end = [c[i * c_len:i * c_len + 1] if rev else c[(i + 1) * c_len - 1:(i + 1) * c_len] for i in range(n_sub)]
    c_end_rows = jnp.concatenate([jnp.broadcast_to(ce, (c_len, ce.shape[1])) for ce in c_end], axis=0)
    k_end = (k * jnp.exp(c_end_rows - c)).astype(BF16)
    dec = [jnp.exp(ce) for ce in c_end]
    vb = v.astype(BF16)

    def advance(states, i):
        new = []
        for p in range(0, HG_HEADS, 2):
            lanes = slice(p * HG_KEY, (p + 2) * HG_KEY)
            inc = _dot_tn(vb[chunk(i), lanes], k_end[chunk(i), lanes])
            for j in range(2):
                blk = slice(j * HG_KEY, (j + 1) * HG_KEY)
                new.append(states[p + j] * dec[i][:, head(p + j)] + inc[blk, blk])
        return new

    if o_ref is None:
        states = [st_ref[h] for h in range(HG_HEADS)]
        for i in order:
            states = advance(states, i)
        for h in range(HG_HEADS):
            st_ref[h] = states[h]
        return

    q = _silu(q_raw.astype(F32))
    q_in = (q * jnp.exp(c)).astype(BF16)
    lowest = functools.reduce(jnp.minimum, c_end)

    def stack(x, i, h0, n):
        return jnp.concatenate([x[chunk(i), head(h)] for h in range(h0, h0 + n)], axis=0)

    n_qk = MXU_TILE // c_len
    groups = [(i, h0) for i in order for h0 in range(0, HG_HEADS, n_qk)]
    work = {}

    def fast_scores():
        k_in = (k * jnp.exp(-c)).astype(BF16)
        work['raw'] = [_dot_nt(stack(q_in, i, h0, n_qk), stack(k_in, i, h0, n_qk)) for i, h0 in groups]

    def fast_mask():
        same = seen(n_qk * c_len)
        work['masked'] = [jnp.where(same, s, 0.0).astype(BF16) for s in work['raw']]

    def fast_values():
        intra = {}
        for (i, h0), sc in zip(groups, work['masked']):
            pv = _dot(sc, stack(vb, i, h0, n_qk))
            for j in range(n_qk):
                intra[i, h0 + j] = pv[j * c_len:(j + 1) * c_len]
        work['intra'] = intra

    def fast_chain():
        states = [st_ref[h] for h in range(HG_HEADS)]
        for i in order:
            out = []
            for p in range(0, HG_HEADS, 2):
                st_pair = jnp.concatenate([states[p].astype(BF16), states[p + 1].astype(BF16)], axis=0)
                inter = _dot_nt(stack(q_in, i, p, 2), st_pair)
                for j in range(2):
                    out.append(inter[j * c_len:(j + 1) * c_len, j * HG_KEY:(j + 1) * HG_KEY]
                               + work['intra'][i, p + j])
            o_ref[chunk(i), :] = jnp.concatenate(out, axis=1)
            states = advance(states, i)
        for h in range(HG_HEADS):
            st_ref[h] = states[h]

    def exact():
        for i in order:
            r = chunk(i)
            _gla_intra_exact(q[r], k[r], v[r].astype(F32), vb[r], c[r], q_in[r], st_ref, o_ref, r, rev)
            states = advance([st_ref[h] for h in range(HG_HEADS)], i)
            for h in range(HG_HEADS):
                st_ref[h] = states[h]

    return lowest, (fast_scores, fast_mask, fast_values, fast_chain), exact


def _gla_run(dirs):
    parts = [_gla_step(*d) for d in dirs]
    if parts[0] is None:
        return
    lowest = functools.reduce(jnp.minimum, [p[0] for p in parts])
    in_range = jnp.min(lowest) >= -HG_FAST_RANGE

    @pl.when(in_range)
    def _():
        for phase in zip(*[p[1] for p in parts]):
            for fn in phase:
                fn()

    @pl.when(jnp.logical_not(in_range))
    def _():
        for p in parts:
            p[2]()


def _gla_intra_exact(q, k, v, vb, c, q_in, st_ref, o_ref, rows, rev):
    c_len = q.shape[0]
    pairs = []
    size = c_len // 2
    while size >= HG_SUB:
        for lo in range(0, c_len, 2 * size):
            pairs.append((lo, lo + size, lo + 2 * size))
        size //= 2
    scaled = []
    for lo, mid, hi in pairs:
        if rev:
            late, early, bnd = slice(lo, mid), slice(mid, hi), mid
        else:
            late, early, bnd = slice(mid, hi), slice(lo, mid), mid - 1
        cb = c[bnd:bnd + 1]
        q_l = (q[late] * jnp.exp(c[late] - cb)).astype(BF16)
        k_e = (k[early] * jnp.exp(cb - c[early])).astype(BF16)
        scaled.append((late, early, q_l, k_e))
    n_sub = c_len // HG_SUB
    t_idx = lax.broadcasted_iota(jnp.int32, (HG_SUB, 1), 0)
    diag = [[None] * HG_HEADS for _ in range(n_sub)]
    for b in range(n_sub):
        r0 = b * HG_SUB
        qb, cb = q[r0:r0 + HG_SUB], c[r0:r0 + HG_SUB]
        for s in range(HG_SUB):
            row = r0 + s
            ok = (t_idx <= s) if rev else (t_idx >= s)
            w = qb * k[row:row + 1] * jnp.exp(jnp.where(ok, cb - c[row:row + 1], -jnp.inf))
            for h in range(HG_HEADS):
                sl = slice(h * HG_KEY, (h + 1) * HG_KEY)
                contrib = jnp.sum(w[:, sl], axis=-1, keepdims=True) * v[row:row + 1, sl]
                diag[b][h] = contrib if diag[b][h] is None else diag[b][h] + contrib

    for h in range(HG_HEADS):
        sl = slice(h * HG_KEY, (h + 1) * HG_KEY)
        o_h = _dot_nt(q_in[:, sl], st_ref[h].astype(BF16))
        parts = [diag[b][h] for b in range(n_sub)]
        for late, early, q_l, k_e in scaled:
            sc = _dot_nt(q_l[:, sl], k_e[:, sl]).astype(BF16)
            add = _dot(sc, vb[early, sl])
            b0 = late.start // HG_SUB
            for j in range((late.stop - late.start) // HG_SUB):
                parts[b0 + j] = parts[b0 + j] + add[j * HG_SUB:(j + 1) * HG_SUB]
        o_ref[rows, sl] = o_h + jnp.concatenate(parts, axis=0)


def _gla_kernel(lb_ref, *refs, n_ctx_steps):
    lat = [refs[0:3], refs[3:6]]
    ctx = [refs[6:8], refs[8:10]]
    outs, states = refs[10:12], refs[12:14]
    s = pl.program_id(1)

    @pl.when(s == 0)
    def _():
        for st_ref in states:
            st_ref[...] = jnp.zeros_like(st_ref)

    lbs = []
    for d in range(2):
        x = lb_ref[d]
        e = jnp.exp(x - jnp.max(x, axis=0, keepdims=True))
        lbs.append(e[0:1] / jnp.sum(e, axis=0, keepdims=True))

    @pl.when(s < n_ctx_steps)
    def _():
        _gla_run([(ctx[d][0][...], None, ctx[d][1][...], lbs[d], states[d], None, d == 1) for d in range(2)])

    @pl.when(s >= n_ctx_steps)
    def _():
        _gla_run([(lat[d][0][...], lat[d][1][...], lat[d][2][...], lbs[d], states[d], outs[d], d == 1)
                  for d in range(2)])


def _hgrn2_scan(a_lat, f_lat, a_ctx, f_ctx, hg_lb, batch, n_seq, n_ctx):
    hk = HG_HEADS * HG_KEY
    rows = HG_CHUNK * HG_STEP_CHUNKS
    assert n_seq % rows == 0 and n_ctx % rows == 0
    nc, ncc = n_seq // rows, n_ctx // rows

    def lat(rev):
        def index(b, s):
            j = jnp.maximum(s - ncc, 0)
            return b * nc + (nc - 1 - j if rev else j)
        return index

    def ctx(rev):
        def index(b, s):
            j = jnp.minimum(s, ncc - 1)
            return b * ncc + (ncc - 1 - j if rev else j)
        return index

    def blk(index, col):
        return pl.BlockSpec((rows, hk), lambda b, s: (index(b, s), col))

    lat_specs, ctx_specs, out_specs = [], [], []
    for d, rev in enumerate((False, True)):
        lat_specs += [blk(lat(rev), d), blk(lat(rev), 2), blk(lat(rev), 3)]
        ctx_specs += [blk(ctx(rev), d), blk(ctx(rev), 1)]
        out_specs.append(blk(lat(rev), 0))
    return pl.pallas_call(
        functools.partial(_gla_kernel, n_ctx_steps=ncc),
        grid=(batch, ncc + nc),
        in_specs=[pl.BlockSpec((2, hg_lb.shape[1], hk), lambda b, s: (0, 0, 0))] + lat_specs + ctx_specs,
        out_specs=out_specs,
        out_shape=[jax.ShapeDtypeStruct((batch * n_seq, hk), F32)] * 2,
        scratch_shapes=[pltpu.VMEM((HG_HEADS, HG_KEY, HG_KEY), F32)] * 2,
        compiler_params=_params("parallel", "arbitrary"),
        name="hgrn2",
    )(hg_lb, f_lat, a_lat, a_lat, f_lat, a_lat, a_lat, f_ctx, a_ctx, f_ctx, a_ctx)


def _route(tok, wr_ref, rb):
    t_hi = tok.astype(BF16)
    t_lo = (tok - t_hi.astype(F32)).astype(BF16)
    hi_both = _dot(t_hi, wr_ref[...])
    lg = hi_both[:, :LANES] + hi_both[:, LANES:] + _dot(t_lo, wr_ref[:, :LANES]) + rb
    lane = lax.broadcasted_iota(jnp.int32, lg.shape, 1)
    lane_f = lane.astype(F32)
    ninf = -jnp.inf
    gl = jnp.where(lane < MOE_GROUPS, lg, ninf)
    gmax = jnp.max(gl, axis=-1, keepdims=True)
    g_idx = jnp.min(jnp.where(gl == gmax, lane_f, float(LANES)), axis=-1, keepdims=True)
    g_val = 1.0 / jnp.sum(jnp.exp(gl - gmax), axis=-1, keepdims=True)
    e_lane = lane_f - float(MOE_GROUPS)
    lo = g_idx * float(MOE_EXPERTS_PER_GROUP)
    in_grp = (e_lane >= lo) & (e_lane < lo + float(MOE_EXPERTS_PER_GROUP))
    el = jnp.where(in_grp, lg, ninf)
    l1 = jnp.max(el, axis=-1, keepdims=True)
    i1 = jnp.min(jnp.where(el == l1, e_lane, float(LANES)), axis=-1, keepdims=True)
    el2 = jnp.where(e_lane == i1, ninf, el)
    l2 = jnp.max(el2, axis=-1, keepdims=True)
    i2 = jnp.min(jnp.where(el2 == l2, e_lane, float(LANES)), axis=-1, keepdims=True)
    r = jnp.exp(l2 - l1)
    w1 = g_val / (1.0 + r)
    w2 = w1 * r
    eid = jnp.where(lane == 0, i1, jnp.where(lane == 1, i2, 0.0))
    gate = jnp.where(lane == 0, w1, jnp.where(lane == 1, w2, 0.0))
    eid_t = jnp.transpose(eid)[:MOE_ID_ROWS].astype(jnp.int32)
    return eid_t, gate


def _post_mix(y, x_ref, g1_ref, sc2_ref, sh2_ref, lng_ref, lnb_ref, wr_ref, rb_ref,
              x1_ref, tok_ref, eid_ref, gate_ref, r):
    x1 = _layer_norm(DEEPNORM_ALPHA * x_ref[r] + g1_ref[0] * y, lng_ref[...], lnb_ref[...])
    x1_ref[r] = x1
    tok = x1 * (1.0 + sc2_ref[0]) + sh2_ref[0]
    tok_ref[r] = _tiles_from_rows(tok).astype(BF16)
    eid, gate = _route(tok, wr_ref, rb_ref[...])
    eid_ref[:, r] = eid
    gate_ref[r] = gate


def _sub_tiles(n_rows):
    return [slice(s, s + SUB_ROWS) for s in range(0, n_rows, SUB_ROWS)]


def _even_out_kernel(att_ref, of_ref, ob_ref, gt_ref, ng_ref, wo_ref, *rest):
    tiles = _sub_tiles(att_ref.shape[0])
    lhs = []
    for r in tiles:
        o = of_ref[r] + ob_ref[r]
        pieces = []
        for h in range(HG_HEADS):
            oh = o[:, h * HG_KEY:(h + 1) * HG_KEY]
            pieces.append(oh * lax.rsqrt(jnp.mean(oh * oh, axis=-1, keepdims=True) + NORM_EPS))
        hg = (jnp.concatenate(pieces, axis=-1) * ng_ref[...] * _silu(gt_ref[r].astype(F32))).astype(BF16)
        lhs.append(jnp.concatenate([att_ref[r], hg], axis=-1))
    ys = [_dot(a, wo_ref[...]) for a in lhs]
    for r, y in zip(tiles, ys):
        _post_mix(y, *rest, r)


def _post_specs(d, tm, rows_per_batch):
    def bmap(i):
        return (i * tm // rows_per_batch, 0, 0)

    row = pl.BlockSpec((tm, d), lambda i: (i, 0))
    mod = pl.BlockSpec((1, 1, d), bmap)
    vec = pl.BlockSpec((1, d), lambda i: (0, 0))
    rw = pl.BlockSpec((d, 2 * LANES), lambda i: (0, 0))
    in_specs = [row, mod, mod, mod, vec, vec, rw, pl.BlockSpec((1, LANES), lambda i: (0, 0))]
    lane_blk = pl.BlockSpec((tm, LANES), lambda i: (i, 0))
    tiles = pl.BlockSpec((tm, d // LANES, LANES), lambda i: (i, 0, 0))
    out_specs = [row, tiles, pl.BlockSpec((MOE_ID_ROWS, tm), lambda i: (0, i)), lane_blk]
    return in_specs, out_specs


def _post_out_shapes(t, d):
    return [jax.ShapeDtypeStruct((t, d), F32), jax.ShapeDtypeStruct((t, d // LANES, LANES), BF16),
            jax.ShapeDtypeStruct((MOE_ID_ROWS, t), jnp.int32), jax.ShapeDtypeStruct((t, LANES), F32)]


def _resident(shape):
    return pl.BlockSpec(shape, lambda i: (0,) * len(shape), pipeline_mode=pl.Buffered(1))


def _even_out(att, o_f, o_b, p, norm_g, w_out, x, g1, sc2, sh2, lng, lnb, wr, rb, rows_per_batch,
              tm=ROW_TILE):
    t, d = x.shape
    hv = o_f.shape[1]
    post_in, post_out = _post_specs(d, tm, rows_per_batch)
    return pl.pallas_call(
        _even_out_kernel,
        grid=(t // tm,),
        in_specs=[
            pl.BlockSpec((tm, att.shape[1]), lambda i: (i, 0)),
            pl.BlockSpec((tm, hv), lambda i: (i, 0)),
            pl.BlockSpec((tm, hv), lambda i: (i, 0)),
            pl.BlockSpec((tm, hv), lambda i: (i, 4)),
            pl.BlockSpec((1, hv), lambda i: (0, 0)),
            _resident(w_out.shape),
        ] + post_in,
        out_specs=post_out,
        out_shape=_post_out_shapes(t, d),
        compiler_params=_params("parallel"),
        name="even_out",
    )(att, o_f, o_b, p, norm_g, w_out, x, g1, sc2, sh2, lng, lnb, wr, rb)


def _combine(x_ref, ya_ref, yb_ref, gate_ref, g2_ref, lng_ref, lnb_ref, r):
    gate = gate_ref[r]
    y = (gate[:, 0:1] * _rows_from_tiles(ya_ref[r].astype(F32))
         + gate[:, 1:2] * _rows_from_tiles(yb_ref[r].astype(F32)))
    return _layer_norm(DEEPNORM_ALPHA * x_ref[r] + g2_ref[0] * y, lng_ref[...], lnb_ref[...])


def _combine_proj_kernel(x_ref, ya_ref, yb_ref, gate_ref, g2_ref, lng_ref, lnb_ref, sc_ref, sh_ref, w_ref,
                         x2_ref, u_ref):
    tiles = _sub_tiles(x_ref.shape[0])
    lhs = []
    for r in tiles:
        x2 = _combine(x_ref, ya_ref, yb_ref, gate_ref, g2_ref, lng_ref, lnb_ref, r)
        x2_ref[r] = x2
        lhs.append((x2 * (1.0 + sc_ref[0]) + sh_ref[0]).astype(BF16))
    for r, a in zip(tiles, lhs):
        u_ref[r] = _dot(a, w_ref[...])


def _combine_kernel(x_ref, ya_ref, yb_ref, gate_ref, g2_ref, lng_ref, lnb_ref, x2_ref):
    for r in _sub_tiles(x_ref.shape[0]):
        x2_ref[r] = _combine(x_ref, ya_ref, yb_ref, gate_ref, g2_ref, lng_ref, lnb_ref, r)


def _combine_call(x1, y2, gate, g2, lng, lnb, rows_per_batch, proj=None, tm=ROW_TILE):
    t, d = x1.shape
    nt = t // tm

    def bmap(i):
        return (i * tm // rows_per_batch, 0, 0)

    row = pl.BlockSpec((tm, d), lambda i: (i, 0))
    mod = pl.BlockSpec((1, 1, d), bmap)
    vec = pl.BlockSpec((1, d), lambda i: (0, 0))
    in_specs = [row, pl.BlockSpec((tm, d // LANES, LANES), lambda i: (i, 0, 0)),
                pl.BlockSpec((tm, d // LANES, LANES), lambda i: (nt + i, 0, 0)),
                pl.BlockSpec((tm, LANES), lambda i: (i, 0)), mod, vec, vec]
    args = [x1, y2, y2, gate, g2, lng, lnb]
    if proj is None:
        return pl.pallas_call(
            _combine_kernel, grid=(nt,), in_specs=in_specs, out_specs=row,
            out_shape=jax.ShapeDtypeStruct((t, d), F32),
            compiler_params=_params("parallel"), name="combine_ln")(*args)
    sc, sh, w = proj
    return pl.pallas_call(
        _combine_proj_kernel, grid=(nt,),
        in_specs=in_specs + [mod, mod, _resident(w.shape)],
        out_specs=[row, pl.BlockSpec((tm, w.shape[1]), lambda i: (i, 0))],
        out_shape=[jax.ShapeDtypeStruct((t, d), F32), jax.ShapeDtypeStruct((t, w.shape[1]), F32)],
        compiler_params=_params("parallel"), name="combine_ln_proj")(*args, sc, sh, w)


def _pool_out_kernel(up_ref, uc_ref, un_ref, wg_ref, ps_ref, wo_ref, *rest, n_seq):
    tm, d = uc_ref.shape
    n_grp = len(POOL_WINDOWS)
    ch = d // n_grp
    halo = POOL_HALO
    tiles = _sub_tiles(tm)
    lhs = []
    for r in tiles:
        n_r = r.stop - r.start
        pos0 = (pl.program_id(0) * tm + r.start) % n_seq
        e_pos = pos0 - halo + lax.broadcasted_iota(jnp.int32, (n_r + 2 * halo, 1), 0)
        e_ok = (e_pos >= 0) & (e_pos < n_seq)
        t_pos = pos0 + lax.broadcasted_iota(jnp.int32, (n_r, 1), 0)
        z = []
        for gi, w in enumerate(POOL_WINDOWS):
            cs = slice(gi * ch, (gi + 1) * ch)
            u = uc_ref[r, cs]
            before = up_ref[:, cs] if r.start == 0 else uc_ref[r.start - halo:r.start, cs]
            after = un_ref[:, cs] if r.stop == tm else uc_ref[r.stop:r.stop + halo, cs]
            ext = jnp.where(e_ok, jnp.concatenate([before, u, after], axis=0), 0.0)
            a, span = ext, 1
            while span < w:
                a = a[:a.shape[0] - span] + a[span:]
                span *= 2
            start = halo - w // 2
            win = a[start:start + n_r]
            cnt = (jnp.minimum(t_pos + (w - w // 2), n_seq) - jnp.maximum(t_pos - w // 2, 0)).astype(F32)
            mixed = (win / cnt - u).astype(BF16)
            z.append((_dot(mixed, wg_ref[gi]) * ps_ref[:, cs]).astype(BF16))
        lhs.append(jnp.concatenate(z, axis=-1))
    ys = [_dot(a, wo_ref[...]) for a in lhs]
    for r, y in zip(tiles, ys):
        _post_mix(y, *rest, r)


def _pool_out(u, w_grp, scale, w_out, x, g1, sc2, sh2, lng, lnb, wr, rb, n_seq, tm=ROW_TILE):
    t, d = x.shape
    hb = tm // POOL_HALO
    n_hb = t // POOL_HALO
    post_in, post_out = _post_specs(d, tm, n_seq)
    return pl.pallas_call(
        functools.partial(_pool_out_kernel, n_seq=n_seq),
        grid=(t // tm,),
        in_specs=[
            pl.BlockSpec((POOL_HALO, d), lambda i: (jnp.maximum(i * hb - 1, 0), 0)),
            pl.BlockSpec((tm, d), lambda i: (i, 0)),
            pl.BlockSpec((POOL_HALO, d), lambda i: (jnp.minimum((i + 1) * hb, n_hb - 1), 0)),
            _resident(w_grp.shape),
            pl.BlockSpec((1, d), lambda i: (0, 0)),
            _resident(w_out.shape),
        ] + post_in,
        out_specs=post_out,
        out_shape=_post_out_shapes(t, d),
        compiler_params=_params("parallel"),
        name="pool_out",
    )(u, u, u, w_grp, scale, w_out, x, g1, sc2, sh2, lng, lnb, wr, rb)


def _moe_kernel(be_ref, nv_ref, first_ref, ws_ref, nxt_ref, idx_ref, idxn_ref, idxp_ref, tok_hbm, w1_hbm, w3_hbm, w2_hbm,
                y_hbm, xbuf, ybuf, xb_ref, wf1, wf3, wf2, w1b, w3b, w2b, gsem, ssem, wsem, *, n_tok, layer, n_blocks):
    i = pl.program_id(0)
    used = nv_ref[jnp.minimum(i, n_blocks - 1)] > 0
    used = used & (i < n_blocks)
    prev_used = (i > 0) & (nv_ref[jnp.maximum(i - 1, 0)] > 0)
    xs = i % 2

    def weight_copies(e, ws):
        return (pltpu.make_async_copy(w1_hbm.at[layer, e], wf1.at[ws], wsem.at[ws]),
                pltpu.make_async_copy(w3_hbm.at[layer, e], wf3.at[ws], wsem.at[ws]),
                pltpu.make_async_copy(w2_hbm.at[layer, e], wf2.at[ws], wsem.at[ws]))

    def gather_start(idx, slot):
        for r in range(MOE_ROWS):
            tok = idx[0, 0, r] & (n_tok - 1)
            pltpu.make_async_copy(tok_hbm.at[tok], xbuf.at[slot, r], gsem.at[slot]).start()

    def gather_wait(slot):
        pltpu.make_async_copy(tok_hbm.at[pl.ds(0, MOE_ROWS)], xbuf.at[slot], gsem.at[slot]).wait()

    def scatter_start(idx, slot):
        for r in range(MOE_ROWS):
            pltpu.make_async_copy(ybuf.at[slot, r], y_hbm.at[idx[0, 0, r]], ssem.at[slot]).start(priority=r % 2)

    def scatter_wait(slot):
        pltpu.make_async_copy(ybuf.at[slot], y_hbm.at[pl.ds(0, MOE_ROWS)], ssem.at[slot]).wait()

    @pl.when(i == 0)
    def _():
        xbuf[...] = jnp.zeros_like(xbuf)
        ybuf[...] = jnp.zeros_like(ybuf)
        spare0 = pltpu.make_async_copy(
            ybuf.at[0], y_hbm.at[pl.ds(MOE_TOP_K * n_tok, MOE_ROWS)], ssem.at[0])
        spare0.start()
        for cp in weight_copies(be_ref[0], 0):
            cp.start(priority=WEIGHT_DMA_PRIORITY)
        gather_start(idx_ref, 0)

    @pl.when(used)
    def _():
        ws = ws_ref[i]

        @pl.when(first_ref[i] == 1)
        def _():
            for cp in weight_copies(be_ref[i], ws):
                cp.wait()
            nxt = nxt_ref[i]

            @pl.when(nxt >= 0)
            def _():
                for cp in weight_copies(nxt, 1 - ws):
                    cp.start(priority=WEIGHT_DMA_PRIORITY)

            w1b[...] = wf1[ws].astype(BF16)
            w3b[...] = wf3[ws].astype(BF16)
            w2b[...] = wf2[ws].astype(BF16)

        def compute(n_rows):
            rows = slice(0, n_rows)
            gather_wait(xs)
            xb_ref[rows] = _rows_from_tiles(xbuf[xs, rows].astype(F32)).astype(BF16)
            gather_start(idxn_ref, 1 - xs)
            scatter_start(idxp_ref, 1 - xs)
            xb = xb_ref[rows]
            h = (_silu(_dot(xb, w1b[...])) * _dot(xb, w3b[...])).astype(BF16)
            y = _tiles_from_rows(_dot(h, w2b[...])).astype(BF16)
            scatter_wait(xs)
            ybuf[xs, rows] = y

        half = MOE_ROWS // 2
        nv = nv_ref[i]

        @pl.when(nv > half)
        def _():
            compute(MOE_ROWS)

        @pl.when(nv <= half)
        def _():
            compute(half)

    @pl.when(jnp.logical_not(used) & prev_used)
    def _():
        gather_wait(xs)
        scatter_start(idxp_ref, 1 - xs)
        scatter_wait(1 - xs)
        scatter_wait(xs)


def _moe_dispatch(eid_t, n_blocks):
    n_tok = eid_t.shape[1]
    n_slots = (n_blocks + 2) * MOE_ROWS
    slot_rows = -(-n_slots // SMEM_1D_TILE)
    assert n_blocks <= LANES and n_tok % SMEM_1D_TILE == 0 and SMEM_1D_TILE % MOE_ROWS == 0
    slot, meta = pl.pallas_call(
        functools.partial(_dispatch_kernel, n_tok=n_tok),
        in_specs=[pl.BlockSpec(memory_space=pltpu.VMEM)],
        out_specs=[pl.BlockSpec(memory_space=pltpu.SMEM), pl.BlockSpec(memory_space=pltpu.VMEM)],
        out_shape=[jax.ShapeDtypeStruct((slot_rows * SMEM_1D_TILE,), jnp.int32),
                   jax.ShapeDtypeStruct((MOE_ID_ROWS, LANES), jnp.int32)],
        scratch_shapes=[pltpu.VMEM((MOE_TOP_K * n_tok,), jnp.int32),
                        pltpu.VMEM((slot_rows * SMEM_1D_TILE,), jnp.int32),
                        pltpu.SMEM((MOE_TOP_K * n_tok,), jnp.int32),
                        pltpu.SemaphoreType.DMA(())],
        name="moe_dispatch",
    )(eid_t)
    return (slot[:n_slots].reshape(n_blocks + 2, 1, MOE_ROWS),) + tuple(meta[r, :n_blocks] for r in range(5))


def _dispatch_kernel(eid_ref, slot_ref, meta_ref, dest_vmem, init_vmem, dest_smem, sem, *, n_tok):
    tile = MOE_ROWS
    n_tiles = n_tok // tile
    sub = lax.broadcasted_iota(jnp.int32, (N_EXPERTS, tile), 0)
    si = lax.broadcasted_iota(jnp.int32, (tile, tile), 0)
    ti = lax.broadcasted_iota(jnp.int32, (tile, tile), 1)
    before = jnp.where(si < ti, 1.0, 0.0).astype(BF16)

    def one_hots(j):
        ids = eid_ref[:, j * tile:(j + 1) * tile]
        return [jnp.where(sub == ids[k:k + 1], 1.0, 0.0) for k in range(MOE_TOP_K)]

    carry = jnp.zeros((N_EXPERTS, 1), F32)
    ranks = []
    for j in range(n_tiles):
        oh = one_hots(j)
        both = oh[0] + oh[1]
        seen = carry + _dot(both.astype(BF16), before)
        ranks.append([jnp.sum(seen * o, axis=0, keepdims=True) for o in oh])
        carry = carry + jnp.sum(both, axis=1, keepdims=True)

    counts = carry
    nblk = jnp.floor((counts + float(MOE_ROWS - 1)) * (1.0 / MOE_ROWS))
    ei = lax.broadcasted_iota(jnp.int32, (N_EXPERTS, N_EXPERTS), 0)
    ej = lax.broadcasted_iota(jnp.int32, (N_EXPERTS, N_EXPERTS), 1)
    lower = jnp.where(ej < ei, 1.0, 0.0).astype(BF16)
    first_blk = _dot(lower, jnp.broadcast_to(nblk, (N_EXPERTS, LANES)).astype(BF16))[:, 0:1]
    first_slot = first_blk * float(MOE_ROWS)

    per_row = SMEM_1D_TILE // tile
    for k in range(MOE_TOP_K):
        for q in range(n_tiles // per_row):
            parts = []
            for j in range(q * per_row, (q + 1) * per_row):
                parts.append(jnp.sum(first_slot * one_hots(j)[k], axis=0, keepdims=True) + ranks[j][k])
            dest = jnp.concatenate(parts, axis=1).astype(jnp.int32)
            dest_vmem[pl.ds(k * n_tok + q * SMEM_1D_TILE, SMEM_1D_TILE)] = dest.reshape(SMEM_1D_TILE)

    lane = lax.broadcasted_iota(jnp.int32, (1, SMEM_1D_TILE), 1)
    for q in range(init_vmem.shape[0] // SMEM_1D_TILE):
        pos = q * SMEM_1D_TILE + lane
        spare = MOE_TOP_K * n_tok + ((pos // MOE_ROWS + 1) % 2) * MOE_ROWS + pos % MOE_ROWS
        init_vmem[pl.ds(q * SMEM_1D_TILE, SMEM_1D_TILE)] = spare.reshape(SMEM_1D_TILE)
    copies = [pltpu.make_async_copy(dest_vmem, dest_smem, sem), pltpu.make_async_copy(init_vmem, slot_ref, sem)]
    for cp in copies:
        cp.start()
    for cp in copies:
        cp.wait()

    def place(t, carry_):
        for k in range(MOE_TOP_K):
            slot_ref[dest_smem[k * n_tok + t] + MOE_ROWS] = k * n_tok + t
        return carry_

    lax.fori_loop(0, n_tok, place, 0, unroll=8)

    b = lax.broadcasted_iota(jnp.int32, (N_EXPERTS, LANES), 1).astype(F32)
    e_col = lax.broadcasted_iota(jnp.int32, (N_EXPERTS, LANES), 0).astype(F32)
    b_row = b[0:1]
    be = jnp.minimum(jnp.sum(jnp.where(first_blk + nblk <= b, 1.0, 0.0), axis=0, keepdims=True), N_EXPERTS - 1.0)
    mine = e_col == be
    cnt_b = jnp.sum(jnp.where(mine, counts, 0.0), axis=0, keepdims=True)
    start_b = jnp.sum(jnp.where(mine, first_blk, 0.0), axis=0, keepdims=True)
    nv = jnp.clip(cnt_b - (b_row - start_b) * MOE_ROWS, 0.0, float(MOE_ROWS))
    nv = jnp.where(b_row < jnp.sum(nblk, axis=0, keepdims=True), nv, 0.0)
    first = jnp.where((nv > 0) & ((b_row == 0) | (be != pltpu.roll(be, 1, 1))), 1.0, 0.0)
    li = lax.broadcasted_iota(jnp.int32, (LANES, LANES), 0)
    lj = lax.broadcasted_iota(jnp.int32, (LANES, LANES), 1)
    upto = jnp.where(li <= lj, 1.0, 0.0).astype(BF16)
    run = _dot(jnp.broadcast_to(first, (MOE_ID_ROWS, LANES)).astype(BF16), upto)[0:1] - 1.0
    ws = run - 2.0 * jnp.floor(run * 0.5)
    later = jnp.min(jnp.where((e_col > be) & (counts > 0), e_col, float(LANES)), axis=0, keepdims=True)
    nxt = jnp.where(later >= float(N_EXPERTS), -1.0, later)
    rows = [be, nv, first, ws, nxt] + [jnp.zeros_like(be)] * (MOE_ID_ROWS - 5)
    meta_ref[...] = jnp.concatenate(rows, axis=0).astype(jnp.int32)


def _moe_experts(tok, eid, w1, w3, w2, layer):
    n_tok, n_sub, _ = tok.shape
    d = n_sub * LANES
    assert n_tok & (n_tok - 1) == 0
    ff = w1.shape[3]
    n_assign = n_tok * MOE_TOP_K
    n_blocks = -(-(n_assign + N_EXPERTS * (MOE_ROWS - 1)) // MOE_ROWS)
    slot, be, nv, first, ws, nxt = _moe_dispatch(eid, n_blocks)
    grid_spec = pltpu.PrefetchScalarGridSpec(
        num_scalar_prefetch=5,
        grid=(n_blocks + 1,),
        in_specs=[
            pl.BlockSpec((1, 1, MOE_ROWS), lambda i, *_: (i + 1, 0, 0), memory_space=pltpu.SMEM),
            pl.BlockSpec((1, 1, MOE_ROWS), lambda i, *_: (jnp.minimum(i + 2, n_blocks + 1), 0, 0),
                         memory_space=pltpu.SMEM),
            pl.BlockSpec((1, 1, MOE_ROWS), lambda i, *_: (i, 0, 0), memory_space=pltpu.SMEM),
            pl.BlockSpec(memory_space=pl.ANY),
            pl.BlockSpec(memory_space=pl.ANY),
            pl.BlockSpec(memory_space=pl.ANY),
            pl.BlockSpec(memory_space=pl.ANY),
        ],
        out_specs=pl.BlockSpec(memory_space=pl.ANY),
        scratch_shapes=[
            pltpu.VMEM((2, MOE_ROWS, n_sub, LANES), BF16), pltpu.VMEM((2, MOE_ROWS, n_sub, LANES), BF16),
            pltpu.VMEM((MOE_ROWS, d), BF16),
            pltpu.VMEM((2, d, ff), F32), pltpu.VMEM((2, d, ff), F32), pltpu.VMEM((2, ff, d), F32),
            pltpu.VMEM((d, ff), BF16), pltpu.VMEM((d, ff), BF16), pltpu.VMEM((ff, d), BF16),
            pltpu.SemaphoreType.DMA((2,)), pltpu.SemaphoreType.DMA((2,)), pltpu.SemaphoreType.DMA((2,)),
        ],
    )
    return pl.pallas_call(
        functools.partial(_moe_kernel, n_tok=n_tok, layer=layer, n_blocks=n_blocks),
        grid_spec=grid_spec,
        out_shape=jax.ShapeDtypeStruct((MOE_TOP_K * n_tok + 2 * MOE_ROWS, n_sub, LANES), BF16),
        compiler_params=pltpu.CompilerParams(dimension_semantics=("arbitrary",),
                                             vmem_limit_bytes=MOE_VMEM_LIMIT_BYTES),
        name="moe_experts",
    )(be, nv, first, ws, nxt, slot, slot, slot, tok, w1, w3, w2)


def _rope_tables(n_seq):
    half = HEAD_DIM // 2
    n_freq = half // 2
    t = jnp.arange(n_seq)
    row = (t // GRID_W).astype(F32)
    col = (t % GRID_W).astype(F32)
    inv_freq = ROPE_BASE ** (-jnp.arange(n_freq, dtype=F32) / n_freq)
    ang_r = row[:, None] * inv_freq[None, :]
    ang_c = col[:, None] * inv_freq[None, :]
    cos = jnp.concatenate([jnp.cos(ang_r)] * 2 + [jnp.cos(ang_c)] * 2, axis=-1)
    sin = jnp.concatenate([-jnp.sin(ang_r), jnp.sin(ang_r), -jnp.sin(ang_c), jnp.sin(ang_c)], axis=-1)
    return cos, sin


def _router_weights(w_g, b_g, w_e, b_e):
    d = w_g.shape[0]
    n = w_g.shape[1] + w_e.shape[1]
    wr = jnp.concatenate([w_g, w_e, jnp.zeros((d, LANES - n), F32)], axis=1)
    rb = jnp.concatenate([b_g, b_e, jnp.zeros((LANES - n,), F32)]).reshape(1, LANES)
    hi = wr.astype(BF16)
    lo = (wr - hi.astype(F32)).astype(BF16)
    return jnp.concatenate([hi, lo], axis=1), rb


def kernel(x, c, ctx, c_ctx, ada_w, ada_b, ln_g, ln_b, mix_w_in, att_sink, hg_lb, hg_norm_g, mix_w_out, pool_w_in, pool_w_grp, pool_scale, pool_w_out, rt_group_w, rt_group_b, rt_expert_w, rt_expert_b, moe_w1, moe_w3, moe_w2):
    b, n, d = x.shape
    n_ctx = ctx.shape[1]
    t = b * n
    xf = x.reshape(t, d)
    ctxf = ctx.reshape(b * n_ctx, d)

    cond = jnp.concatenate([c, c_ctx[None, :], jnp.zeros((8 - b - 1, d), F32)], axis=0)
    mod = _ada_mod(cond, ada_w, ada_b)

    def chunk(l, j, rows=slice(0, b)):
        return mod[l, rows, j * d:(j + 1) * d][:, None, :]

    w_in = mix_w_in[0].astype(BF16)
    cos, sin = _rope_tables(n)
    q_w, kv_w = ATT_HEADS * HEAD_DIM, ATT_KV_HEADS * HEAD_DIM
    n_att = q_w + 2 * kv_w
    hk = HG_HEADS * HG_KEY
    assert n_att == 2 * PROJ_TN and hk == PROJ_TN
    a_lat, f_lat = _mod_matmul(xf, chunk(0, 1), chunk(0, 0), w_in, cos, sin,
                               lambda j: jnp.where(j < 3, j, jnp.where(j < 5, j + 2, j - 2)), n_att + 3 * hk, 2 * hk,
                               n_q=q_w, n_rope=q_w + kv_w, n_seq=n, tm=1024, tn=PROJ_TN)
    ctx_rows = slice(b, b + 1)
    a_ctx, f_ctx = _mod_matmul(ctxf, chunk(0, 1, ctx_rows), chunk(0, 0, ctx_rows), w_in, cos, sin,
                               lambda j: jnp.where(j < 1, 1, jnp.where(j < 2, 5, j + 1)), 2 * kv_w + hk, 2 * hk,
                               n_q=0, n_rope=0, n_seq=n, tm=b * n_ctx, tn=PROJ_TN)
    att = _window_attention(a_lat, a_ctx, att_sink[0], b, n, n_ctx)
    o_f, o_b = _hgrn2_scan(a_lat, f_lat, a_ctx, f_ctx, hg_lb, b, n, n_ctx)
    wr, rb = _router_weights(rt_group_w[0], rt_group_b[0], rt_expert_w[0], rt_expert_b[0])
    x1, tok, eid, gate = _even_out(
        att, o_f, o_b, a_lat, hg_norm_g[0][None, :], mix_w_out[0].astype(BF16), xf,
        chunk(0, 2), chunk(0, 4), chunk(0, 3), ln_g[0, 0][None, :], ln_b[0, 0][None, :], wr, rb, n)
    y2 = _moe_experts(tok, eid, moe_w1, moe_w3, moe_w2, 0)

    x2, u = _combine_call(x1, y2, gate, chunk(0, 5), ln_g[0, 1][None, :], ln_b[0, 1][None, :], n,
                          proj=(chunk(1, 1), chunk(1, 0), pool_w_in[0].astype(BF16)))
    wr, rb = _router_weights(rt_group_w[1], rt_group_b[1], rt_expert_w[1], rt_expert_b[1])
    x3, tok, eid, gate = _pool_out(
        u, pool_w_grp[0].astype(BF16), pool_scale[0][None, :], pool_w_out[0].astype(BF16), x2,
        chunk(1, 2), chunk(1, 4), chunk(1, 3), ln_g[1, 0][None, :], ln_b[1, 0][None, :], wr, rb, n)
    y2 = _moe_experts(tok, eid, moe_w1, moe_w3, moe_w2, 1)
    out = _combine_call(x3, y2, gate, chunk(1, 5), ln_g[1, 1][None, :], ln_b[1, 1][None, :], n)
    return out.reshape(b, n, d)
```

```python
import functools

import jax
import jax.numpy as jnp
from jax import lax
from jax.experimental import pallas as pl
from jax.experimental.pallas import tpu as pltpu

F32 = jnp.float32
BF16 = jnp.bfloat16

LANES = 128
SUBLANES = 8
MXU_TILE = 256
VMEM_LIMIT_BYTES = 56 * 1024 * 1024
MOE_VMEM_LIMIT_BYTES = 60 * 1024 * 1024

GRID_W = 64
ATT_HEADS = 8
ATT_KV_HEADS = 4
ATT_GROUP = ATT_HEADS // ATT_KV_HEADS
HEAD_DIM = 128
ROPE_FREQS = HEAD_DIM // 4
WINDOW = 128
ATT_BLOCK = 128
ROPE_BASE = 10000.0
HG_HEADS = 8
HG_KEY = 128
HG_CHUNK = 64
HG_STEP_CHUNKS = 4
HG_SUB = 16
HG_FAST_RANGE = 80.0
NORM_EPS = 1e-6
POOL_WINDOWS = (2, 4, 8, 16)
POOL_HALO = 8
MOE_GROUPS = 4
MOE_EXPERTS_PER_GROUP = 8
N_EXPERTS = MOE_GROUPS * MOE_EXPERTS_PER_GROUP
MOE_TOP_K = 2
MOE_ROWS = 256
ROW_TILE = 512
SUB_ROWS = 256
PROJ_TM = 1024
PROJ_TN = 1024
MOE_ID_ROWS = 8
SMEM_1D_TILE = 1024
WEIGHT_DMA_PRIORITY = 1
LN_EPS = 1e-5
DEPTH = 2
DEEPNORM_ALPHA = (2 * DEPTH) ** 0.25


def _dot(a, b):
    return jnp.dot(a, b, preferred_element_type=F32)


def _dot_nt(a, b):
    return lax.dot_general(a, b, (((1,), (1,)), ((), ())), preferred_element_type=F32)


def _dot_tn(a, b):
    return lax.dot_general(a, b, (((0,), (0,)), ((), ())), preferred_element_type=F32)


def _sigmoid(x):
    return 1.0 / (1.0 + jnp.exp(-x))


def _silu(x):
    return x * _sigmoid(x)


def _params(*sem):
    return pltpu.CompilerParams(dimension_semantics=sem, vmem_limit_bytes=VMEM_LIMIT_BYTES)


def _tiles_from_rows(x):
    n = x.shape[1] // LANES
    return jnp.swapaxes(jnp.stack([x[:, s * LANES:(s + 1) * LANES] for s in range(n)], axis=0), 0, 1)


def _rows_from_tiles(x3):
    xt = jnp.swapaxes(x3, 0, 1)
    return jnp.concatenate([xt[s] for s in range(xt.shape[0])], axis=-1)


def _layer_norm(z, g, b):
    mu = jnp.mean(z, axis=-1, keepdims=True)
    zc = z - mu
    var = jnp.mean(zc * zc, axis=-1, keepdims=True)
    return zc * lax.rsqrt(var + LN_EPS) * g + b


def _ada_kernel(s_ref, w_ref, b_ref, o_ref):
    s = _silu(s_ref[...]).astype(BF16)
    o_ref[0] = _dot(s, w_ref[0].astype(BF16)) + b_ref[0]


def _ada_mod(s, ada_w, ada_b, tn=PROJ_TN):
    n_l, d, n = ada_w.shape
    return pl.pallas_call(
        _ada_kernel,
        grid=(n_l, n // tn),
        in_specs=[
            pl.BlockSpec((SUBLANES, d), lambda l, j: (0, 0)),
            pl.BlockSpec((1, d, tn), lambda l, j: (l, 0, j)),
            pl.BlockSpec((1, 1, tn), lambda l, j: (l, 0, j)),
        ],
        out_specs=pl.BlockSpec((1, SUBLANES, tn), lambda l, j: (l, 0, j)),
        out_shape=jax.ShapeDtypeStruct((n_l, SUBLANES, n), F32),
        compiler_params=_params("parallel", "parallel"),
        name="ada_mod",
    )(s, ada_w, ada_b.reshape(n_l, 1, n))


def _rope(t, cos, sin_signed, first_half):
    partner = jnp.where(first_half, pltpu.roll(t, HEAD_DIM - ROPE_FREQS, 1), pltpu.roll(t, ROPE_FREQS, 1))
    return t * cos + partner * sin_signed


def _modmm_kernel(x_ref, sc_ref, sh_ref, w_ref, cos_ref, sin_ref, oa_ref, ob_ref, xs_ref, *,
                  n_q, n_rope, n_a_tiles):
    j = pl.program_id(1)
    tm, tn = x_ref.shape[0], w_ref.shape[1]
    halves = [slice(0, tm // 2), slice(tm // 2, tm)]

    def dots():
        return [_dot(xs_ref[r], w_ref[...]) for r in halves]

    def store_bf16(jt, r, acc):
        lane = lax.broadcasted_iota(jnp.int32, (1, HEAD_DIM), 1)
        first_half = (lane % (2 * ROPE_FREQS)) < ROPE_FREQS
        for h in range(tn // HEAD_DIM):
            sl = slice(h * HEAD_DIM, (h + 1) * HEAD_DIM)
            col = jt * tn + h * HEAD_DIM
            if col < n_rope:
                scale = HEAD_DIM ** -0.5 if col < n_q else 1.0
                oa_ref[r, sl] = _rope(acc[:, sl], cos_ref[r] * scale, sin_ref[r] * scale, first_half).astype(BF16)
            else:
                oa_ref[r, sl] = acc[:, sl].astype(BF16)

    for jt in range(n_a_tiles):
        @pl.when(j == jt)
        def _(jt=jt):
            if jt == 0:
                for r in halves:
                    xs_ref[r] = (x_ref[r] * (1.0 + sc_ref[0]) + sh_ref[0]).astype(BF16)
            for r, acc in zip(halves, dots()):
                store_bf16(jt, r, acc)

    @pl.when(j >= n_a_tiles)
    def _():
        for r, acc in zip(halves, dots()):
            ob_ref[r] = acc


def _mod_matmul(x, sc, sh, w, cos, sin, col_map, n_a, n_b, n_q, n_rope, n_seq, tm, tn):
    m, k = x.shape
    rows_per_mod = m // sc.shape[0]
    ta, tb = n_a // tn, n_b // tn
    tab = pl.BlockSpec((tm, HEAD_DIM), lambda i, j: ((i * tm % n_seq) // tm, 0))
    return pl.pallas_call(
        functools.partial(_modmm_kernel, n_q=n_q, n_rope=n_rope, n_a_tiles=ta),
        grid=(m // tm, ta + tb),
        in_specs=[
            pl.BlockSpec((tm, k), lambda i, j: (i, 0)),
            pl.BlockSpec((1, 1, k), lambda i, j: (i * tm // rows_per_mod, 0, 0)),
            pl.BlockSpec((1, 1, k), lambda i, j: (i * tm // rows_per_mod, 0, 0)),
            pl.BlockSpec((k, tn), lambda i, j: (0, col_map(j))),
            tab, tab,
        ],
        out_specs=[pl.BlockSpec((tm, tn), lambda i, j: (i, jnp.minimum(j, ta - 1))),
                   pl.BlockSpec((tm, tn), lambda i, j: (i, jnp.maximum(j - ta, 0)))],
        out_shape=[jax.ShapeDtypeStruct((m, n_a), BF16), jax.ShapeDtypeStruct((m, n_b), F32)],
        scratch_shapes=[pltpu.VMEM((tm, k), BF16)],
        compiler_params=_params("parallel", "arbitrary"),
        name="mod_matmul",
    )(x, sc, sh, w, cos, sin)


def _attn_kernel(sink_ref, q_ref, kp_ref, kc_ref, kn_ref, vp_ref, vc_ref, vn_ref, kx_ref, vx_ref,
                 mp_ref, mn_ref, o_ref, *, n_blocks):
    s_idx = pl.program_id(1)
    blk = ATT_BLOCK
    row1 = lax.broadcasted_iota(jnp.int32, (ATT_GROUP * blk, 1), 0)
    lo, hi = slice(0, blk), slice(blk, 2 * blk)
    subs = [(lo, (kp_ref, vp_ref, lo), (kc_ref, vc_ref, lo), (kc_ref, vc_ref, hi), s_idx > 0, True),
            (hi, (kc_ref, vc_ref, lo), (kc_ref, vc_ref, hi), (kn_ref, vn_ref, lo), True, s_idx < n_blocks // 2 - 1)]
    jobs = [(sub, h) for sub in subs for h in range(ATT_KV_HEADS)]

    def kv(h):
        return slice(h * HEAD_DIM, (h + 1) * HEAD_DIM)

    def gather(sub, h, which):
        blocks = [t[which][t[2], kv(h)] for t in sub[1:4]]
        return jnp.concatenate(blocks + [(kx_ref, vx_ref)[which][:, kv(h)]], axis=0)

    scores = []
    for sub, h in jobs:
        q2 = jnp.concatenate([q_ref[sub[0], (ATT_GROUP * h + g) * HEAD_DIM:(ATT_GROUP * h + g + 1) * HEAD_DIM]
                              for g in range(ATT_GROUP)], axis=0)
        scores.append(_dot_nt(q2, gather(sub, h, 0)))
    probs, dens = [], []
    for (sub, h), s in zip(jobs, scores):
        has_prev, has_next = sub[4], sub[5]
        parts = [jnp.where(has_prev, s[:, :blk] + mp_ref[...], -jnp.inf), s[:, blk:2 * blk],
                 jnp.where(has_next, s[:, 2 * blk:3 * blk] + mn_ref[...], -jnp.inf)]
        parts += [s[:, c:c + blk] for c in range(3 * blk, s.shape[1], blk)]
        sink = jnp.where(row1 < blk, sink_ref[ATT_GROUP * h], sink_ref[ATT_GROUP * h + 1])
        m = jnp.maximum(jnp.max(functools.reduce(jnp.maximum, parts), axis=-1, keepdims=True), sink)
        p = [jnp.exp(x - m) for x in parts]
        dens.append(jnp.sum(functools.reduce(jnp.add, p), axis=-1, keepdims=True) + jnp.exp(sink - m))
        probs.append(jnp.concatenate(p, axis=-1).astype(BF16))
    for (sub, h), p, den in zip(jobs, probs, dens):
        o = _dot(p, gather(sub, h, 1)) / den
        for g in range(ATT_GROUP):
            col = (ATT_GROUP * h + g) * HEAD_DIM
            o_ref[sub[0], col:col + HEAD_DIM] = o[g * blk:(g + 1) * blk].astype(o_ref.dtype)


def _window_attention(qkv, kv_ctx, sink, batch, n_seq, n_ctx):
    assert ATT_GROUP == 2 and WINDOW == ATT_BLOCK
    nb = n_seq // ATT_BLOCK
    qw, kw = ATT_HEADS * HEAD_DIM, ATT_KV_HEADS * HEAD_DIM
    kcol, vcol = qw // kw, qw // kw + 1

    assert nb % 2 == 0
    pair = 2 * ATT_BLOCK

    def before(b, s):
        return b * nb + jnp.maximum(2 * s - 1, 0)

    def after(b, s):
        return b * nb + jnp.minimum(2 * s + 2, nb - 1)

    def own(b, s):
        return b * (nb // 2) + s

    def band(col):
        return [pl.BlockSpec((ATT_BLOCK, kw), lambda b, s: (before(b, s), col)),
                pl.BlockSpec((pair, kw), lambda b, s: (own(b, s), col)),
                pl.BlockSpec((ATT_BLOCK, kw), lambda b, s: (after(b, s), col))]

    kspec, vspec = band(kcol), band(vcol)
    r = jnp.arange(ATT_GROUP * ATT_BLOCK)[:, None] % ATT_BLOCK
    c = jnp.arange(ATT_BLOCK)[None, :]
    mask_prev = jnp.where(c >= r, 0.0, -jnp.inf).astype(F32)
    mask_next = jnp.where(c <= r, 0.0, -jnp.inf).astype(F32)
    mspec = pl.BlockSpec(mask_prev.shape, lambda b, s: (0, 0))
    return pl.pallas_call(
        functools.partial(_attn_kernel, n_blocks=nb),
        grid=(batch, nb // 2),
        in_specs=[pl.BlockSpec(memory_space=pltpu.SMEM),
                  pl.BlockSpec((pair, qw), lambda b, s: (own(b, s), 0))]
        + kspec + vspec
        + [pl.BlockSpec((n_ctx, kw), lambda b, s: (b, 0)), pl.BlockSpec((n_ctx, kw), lambda b, s: (b, 1)),
           mspec, mspec],
        out_specs=pl.BlockSpec((pair, qw), lambda b, s: (own(b, s), 0)),
        out_shape=jax.ShapeDtypeStruct((batch * n_seq, qw), BF16),
        compiler_params=_params("parallel", "parallel"),
        name="window_attention",
    )(sink, qkv, qkv, qkv, qkv, qkv, qkv, qkv, kv_ctx, kv_ctx, mask_prev, mask_next)


def _gla_step(zf, q_raw, v, lb, st_ref, o_ref, rev):
    c_len = HG_CHUNK
    n_rows = zf.shape[0]
    n_sub = n_rows // c_len
    shift = c_len.bit_length() - 1
    order = range(n_sub - 1, -1, -1) if rev else range(n_sub)

    def head(h):
        return slice(h * HG_KEY, (h + 1) * HG_KEY)

    def chunk(i):
        return slice(i * c_len, (i + 1) * c_len)

    def seen(n):
        ri = lax.broadcasted_iota(jnp.int32, (n, n), 0)
        ci = lax.broadcasted_iota(jnp.int32, (n, n), 1)
        return ((ri >> shift) == (ci >> shift)) & ((ci >= ri) if rev else (ci <= ri))

    f = lb + (1.0 - lb) * _sigmoid(zf)
    k = 1.0 - f
    g = jnp.log(f)
    tri = jnp.where(seen(n_rows), 1.0, 0.0).astype(BF16)
    g1 = g.astype(BF16)
    r1 = g - g1.astype(F32)
    g2 = r1.astype(BF16)
    g3 = (r1 - g2.astype(F32)).astype(BF16)
    c = _dot(tri, g1) + _dot(tri, g2) + _dot(tri, g3)
    c_end = [c[i * c_len:i * c_len + 1] if rev else c[(i + 1) * c_len - 1:(i + 1) * c_len] for i in range(n_sub)]
    c_end_rows = jnp.concatenate([jnp.broadcast_to(ce, (c_len, ce.shape[1])) for ce in c_end], axis=0)
    k_end = (k * jnp.exp(c_end_rows - c)).astype(BF16)
    dec = [jnp.exp(ce) for ce in c_end]
    vb = v.astype(BF16)

    def advance(states, i):
        new = []
        for p in range(0, HG_HEADS, 2):
            lanes = slice(p * HG_KEY, (p + 2) * HG_KEY)
            inc = _dot_tn(vb[chunk(i), lanes], k_end[chunk(i), lanes])
            for j in range(2):
                blk = slice(j * HG_KEY, (j + 1) * HG_KEY)
                new.append(states[p + j] * dec[i][:, head(p + j)] + inc[blk, blk])
        return new

    if o_ref is None:
        states = [st_ref[h] for h in range(HG_HEADS)]
        for i in order:
            states = advance(states, i)
        for h in range(HG_HEADS):
            st_ref[h] = states[h]
        return

    q = _silu(q_raw.astype(F32))
    q_in = (q * jnp.exp(c)).astype(BF16)
    lowest = functools.reduce(jnp.minimum, c_end)

    def stack(x, i, h0, n):
        return jnp.concatenate([x[chunk(i), head(h)] for h in range(h0, h0 + n)], axis=0)

    n_qk = MXU_TILE // c_len
    groups = [(i, h0) for i in order for h0 in range(0, HG_HEADS, n_qk)]
    work = {}

    def fast_scores():
        k_in = (k * jnp.exp(-c)).astype(BF16)
        work['raw'] = [_dot_nt(stack(q_in, i, h0, n_qk), stack(k_in, i, h0, n_qk)) for i, h0 in groups]

    def fast_mask():
        same = seen(n_qk * c_len)
        work['masked'] = [jnp.where(same, s, 0.0).astype(BF16) for s in work['raw']]

    def fast_values():
        intra = {}
        for (i, h0), sc in zip(groups, work['masked']):
            pv = _dot(sc, stack(vb, i, h0, n_qk))
            for j in range(n_qk):
                intra[i, h0 + j] = pv[j * c_len:(j + 1) * c_len]
        work['intra'] = intra

    def fast_chain():
        states = [st_ref[h] for h in range(HG_HEADS)]
        for i in order:
            out = []
            for p in range(0, HG_HEADS, 2):
                st_pair = jnp.concatenate([states[p].astype(BF16), states[p + 1].astype(BF16)], axis=0)
                inter = _dot_nt(stack(q_in, i, p, 2), st_pair)
                for j in range(2):
                    out.append(inter[j * c_len:(j + 1) * c_len, j * HG_KEY:(j + 1) * HG_KEY]
                               + work['intra'][i, p + j])
            o_ref[chunk(i), :] = jnp.concatenate(out, axis=1)
            states = advance(states, i)
        for h in range(HG_HEADS):
            st_ref[h] = states[h]

    def exact():
        for i in order:
            r = chunk(i)
            _gla_intra_exact(q[r], k[r], v[r].astype(F32), vb[r], c[r], q_in[r], st_ref, o_ref, r, rev)
            states = advance([st_ref[h] for h in range(HG_HEADS)], i)
            for h in range(HG_HEADS):
                st_ref[h] = states[h]

    return lowest, (fast_scores, fast_mask, fast_values, fast_chain), exact


def _gla_run(dirs):
    parts = [_gla_step(*d) for d in dirs]
    if parts[0] is None:
        return
    lowest = functools.reduce(jnp.minimum, [p[0] for p in parts])
    in_range = jnp.min(lowest) >= -HG_FAST_RANGE

    @pl.when(in_range)
    def _():
        for phase in zip(*[p[1] for p in parts]):
            for fn in phase:
                fn()

    @pl.when(jnp.logical_not(in_range))
    def _():
        for p in parts:
            p[2]()


def _gla_intra_exact(q, k, v, vb, c, q_in, st_ref, o_ref, rows, rev):
    c_len = q.shape[0]
    pairs = []
    size = c_len // 2
    while size >= HG_SUB:
        for lo in range(0, c_len, 2 * size):
            pairs.append((lo, lo + size, lo + 2 * size))
        size //= 2
    scaled = []
    for lo, mid, hi in pairs:
        if rev:
            late, early, bnd = slice(lo, mid), slice(mid, hi), mid
        else:
            late, early, bnd = slice(mid, hi), slice(lo, mid), mid - 1
        cb = c[bnd:bnd + 1]
        q_l = (q[late] * jnp.exp(c[late] - cb)).astype(BF16)
        k_e = (k[early] * jnp.exp(cb - c[early])).astype(BF16)
        scaled.append((late, early, q_l, k_e))
    n_sub = c_len // HG_SUB
    t_idx = lax.broadcasted_iota(jnp.int32, (HG_SUB, 1), 0)
    diag = [[None] * HG_HEADS for _ in range(n_sub)]
    for b in range(n_sub):
        r0 = b * HG_SUB
        qb, cb = q[r0:r0 + HG_SUB], c[r0:r0 + HG_SUB]
        for s in range(HG_SUB):
            row = r0 + s
            ok = (t_idx <= s) if rev else (t_idx >= s)
            w = qb * k[row:row + 1] * jnp.exp(jnp.where(ok, cb - c[row:row + 1], -jnp.inf))
            for h in range(HG_HEADS):
                sl = slice(h * HG_KEY, (h + 1) * HG_KEY)
                contrib = jnp.sum(w[:, sl], axis=-1, keepdims=True) * v[row:row + 1, sl]
                diag[b][h] = contrib if diag[b][h] is None else diag[b][h] + contrib

    for h in range(HG_HEADS):
        sl = slice(h * HG_KEY, (h + 1) * HG_KEY)
        o_h = _dot_nt(q_in[:, sl], st_ref[h].astype(BF16))
        parts = [diag[b][h] for b in range(n_sub)]
        for late, early, q_l, k_e in scaled:
            sc = _dot_nt(q_l[:, sl], k_e[:, sl]).astype(BF16)
            add = _dot(sc, vb[early, sl])
            b0 = late.start // HG_SUB
            for j in range((late.stop - late.start) // HG_SUB):
                parts[b0 + j] = parts[b0 + j] + add[j * HG_SUB:(j + 1) * HG_SUB]
        o_ref[rows, sl] = o_h + jnp.concatenate(parts, axis=0)


def _gla_kernel(lb_ref, *refs, n_ctx_steps):
    lat = [refs[0:3], refs[3:6]]
    ctx = [refs[6:8], refs[8:10]]
    outs, states = refs[10:12], refs[12:14]
    s = pl.program_id(1)

    @pl.when(s == 0)
    def _():
        for st_ref in states:
            st_ref[...] = jnp.zeros_like(st_ref)

    lbs = []
    for d in range(2):
        x = lb_ref[d]
        e = jnp.exp(x - jnp.max(x, axis=0, keepdims=True))
        lbs.append(e[0:1] / jnp.sum(e, axis=0, keepdims=True))

    @pl.when(s < n_ctx_steps)
    def _():
        _gla_run([(ctx[d][0][...], None, ctx[d][1][...], lbs[d], states[d], None, d == 1) for d in range(2)])

    @pl.when(s >= n_ctx_steps)
    def _():
        _gla_run([(lat[d][0][...], lat[d][1][...], lat[d][2][...], lbs[d], states[d], outs[d], d == 1)
                  for d in range(2)])


def _hgrn2_scan(a_lat, f_lat, a_ctx, f_ctx, hg_lb, batch, n_seq, n_ctx):
    hk = HG_HEADS * HG_KEY
    rows = HG_CHUNK * HG_STEP_CHUNKS
    assert n_seq % rows == 0 and n_ctx % rows == 0
    nc, ncc = n_seq // rows, n_ctx // rows

    def lat(rev):
        def index(b, s):
            j = jnp.maximum(s - ncc, 0)
            return b * nc + (nc - 1 - j if rev else j)
        return index

    def ctx(rev):
        def index(b, s):
            j = jnp.minimum(s, ncc - 1)
            return b * ncc + (ncc - 1 - j if rev else j)
        return index

    def blk(index, col):
        return pl.BlockSpec((rows, hk), lambda b, s: (index(b, s), col))

    lat_specs, ctx_specs, out_specs = [], [], []
    for d, rev in enumerate((False, True)):
        lat_specs += [blk(lat(rev), d), blk(lat(rev), 2), blk(lat(rev), 3)]
        ctx_specs += [blk(ctx(rev), d), blk(ctx(rev), 1)]
        out_specs.append(blk(lat(rev), 0))
    return pl.pallas_call(
        functools.partial(_gla_kernel, n_ctx_steps=ncc),
        grid=(batch, ncc + nc),
        in_specs=[pl.BlockSpec((2, hg_lb.shape[1], hk), lambda b, s: (0, 0, 0))] + lat_specs + ctx_specs,
        out_specs=out_specs,
        out_shape=[jax.ShapeDtypeStruct((batch * n_seq, hk), F32)] * 2,
        scratch_shapes=[pltpu.VMEM((HG_HEADS, HG_KEY, HG_KEY), F32)] * 2,
        compiler_params=_params("parallel", "arbitrary"),
        name="hgrn2",
    )(hg_lb, f_lat, a_lat, a_lat, f_lat, a_lat, a_lat, f_ctx, a_ctx, f_ctx, a_ctx)


def _route(tok, wr_ref, rb):
    t_hi = tok.astype(BF16)
    t_lo = (tok - t_hi.astype(F32)).astype(BF16)
    hi_both = _dot(t_hi, wr_ref[...])
    lg = hi_both[:, :LANES] + hi_both[:, LANES:] + _dot(t_lo, wr_ref[:, :LANES]) + rb
    lane = lax.broadcasted_iota(jnp.int32, lg.shape, 1)
    lane_f = lane.astype(F32)
    ninf = -jnp.inf
    gl = jnp.where(lane < MOE_GROUPS, lg, ninf)
    gmax = jnp.max(gl, axis=-1, keepdims=True)
    g_idx = jnp.min(jnp.where(gl == gmax, lane_f, float(LANES)), axis=-1, keepdims=True)
    g_val = 1.0 / jnp.sum(jnp.exp(gl - gmax), axis=-1, keepdims=True)
    e_lane = lane_f - float(MOE_GROUPS)
    lo = g_idx * float(MOE_EXPERTS_PER_GROUP)
    in_grp = (e_lane >= lo) & (e_lane < lo + float(MOE_EXPERTS_PER_GROUP))
    el = jnp.where(in_grp, lg, ninf)
    l1 = jnp.max(el, axis=-1, keepdims=True)
    i1 = jnp.min(jnp.where(el == l1, e_lane, float(LANES)), axis=-1, keepdims=True)
    el2 = jnp.where(e_lane == i1, ninf, el)
    l2 = jnp.max(el2, axis=-1, keepdims=True)
    i2 = jnp.min(jnp.where(el2 == l2, e_lane, float(LANES)), axis=-1, keepdims=True)
    r = jnp.exp(l2 - l1)
    w1 = g_val / (1.0 + r)
    w2 = w1 * r
    eid = jnp.where(lane == 0, i1, jnp.where(lane == 1, i2, 0.0))
    gate = jnp.where(lane == 0, w1, jnp.where(lane == 1, w2, 0.0))
    eid_t = jnp.transpose(eid)[:MOE_ID_ROWS].astype(jnp.int32)
    return eid_t, gate


def _post_mix(y, x_ref, g1_ref, sc2_ref, sh2_ref, lng_ref, lnb_ref, wr_ref, rb_ref,
              x1_ref, tok_ref, eid_ref, gate_ref, r):
    x1 = _layer_norm(DEEPNORM_ALPHA * x_ref[r] + g1_ref[0] * y, lng_ref[...], lnb_ref[...])
    x1_ref[r] = x1
    tok = x1 * (1.0 + sc2_ref[0]) + sh2_ref[0]
    tok_ref[r] = _tiles_from_rows(tok).astype(BF16)
    eid, gate = _route(tok, wr_ref, rb_ref[...])
    eid_ref[:, r] = eid
    gate_ref[r] = gate


def _sub_tiles(n_rows):
    return [slice(s, s + SUB_ROWS) for s in range(0, n_rows, SUB_ROWS)]


def _even_out_kernel(att_ref, of_ref, ob_ref, gt_ref, ng_ref, wo_ref, *rest):
    tiles = _sub_tiles(att_ref.shape[0])
    lhs = []
    for r in tiles:
        o = of_ref[r] + ob_ref[r]
        pieces = []
        for h in range(HG_HEADS):
            oh = o[:, h * HG_KEY:(h + 1) * HG_KEY]
            pieces.append(oh * lax.rsqrt(jnp.mean(oh * oh, axis=-1, keepdims=True) + NORM_EPS))
        hg = (jnp.concatenate(pieces, axis=-1) * ng_ref[...] * _silu(gt_ref[r].astype(F32))).astype(BF16)
        lhs.append(jnp.concatenate([att_ref[r], hg], axis=-1))
    ys = [_dot(a, wo_ref[...]) for a in lhs]
    for r, y in zip(tiles, ys):
        _post_mix(y, *rest, r)


def _post_specs(d, tm, rows_per_batch):
    def bmap(i):
        return (i * tm // rows_per_batch, 0, 0)

    row = pl.BlockSpec((tm, d), lambda i: (i, 0))
    mod = pl.BlockSpec((1, 1, d), bmap)
    vec = pl.BlockSpec((1, d), lambda i: (0, 0))
    rw = pl.BlockSpec((d, 2 * LANES), lambda i: (0, 0))
    in_specs = [row, mod, mod, mod, vec, vec, rw, pl.BlockSpec((1, LANES), lambda i: (0, 0))]
    lane_blk = pl.BlockSpec((tm, LANES), lambda i: (i, 0))
    tiles = pl.BlockSpec((tm, d // LANES, LANES), lambda i: (i, 0, 0))
    out_specs = [row, tiles, pl.BlockSpec((MOE_ID_ROWS, tm), lambda i: (0, i)), lane_blk]
    return in_specs, out_specs


def _post_out_shapes(t, d):
    return [jax.ShapeDtypeStruct((t, d), F32), jax.ShapeDtypeStruct((t, d // LANES, LANES), BF16),
            jax.ShapeDtypeStruct((MOE_ID_ROWS, t), jnp.int32), jax.ShapeDtypeStruct((t, LANES), F32)]


def _resident(shape):
    return pl.BlockSpec(shape, lambda i: (0,) * len(shape), pipeline_mode=pl.Buffered(1))


def _even_out(att, o_f, o_b, p, norm_g, w_out, x, g1, sc2, sh2, lng, lnb, wr, rb, rows_per_batch,
              tm=ROW_TILE):
    t, d = x.shape
    hv = o_f.shape[1]
    post_in, post_out = _post_specs(d, tm, rows_per_batch)
    return pl.pallas_call(
        _even_out_kernel,
        grid=(t // tm,),
        in_specs=[
            pl.BlockSpec((tm, att.shape[1]), lambda i: (i, 0)),
            pl.BlockSpec((tm, hv), lambda i: (i, 0)),
            pl.BlockSpec((tm, hv), lambda i: (i, 0)),
            pl.BlockSpec((tm, hv), lambda i: (i, 4)),
            pl.BlockSpec((1, hv), lambda i: (0, 0)),
            _resident(w_out.shape),
        ] + post_in,
        out_specs=post_out,
        out_shape=_post_out_shapes(t, d),
        compiler_params=_params("parallel"),
        name="even_out",
    )(att, o_f, o_b, p, norm_g, w_out, x, g1, sc2, sh2, lng, lnb, wr, rb)


def _combine(x_ref, ya_ref, yb_ref, gate_ref, g2_ref, lng_ref, lnb_ref, r):
    gate = gate_ref[r]
    y = (gate[:, 0:1] * _rows_from_tiles(ya_ref[r].astype(F32))
         + gate[:, 1:2] * _rows_from_tiles(yb_ref[r].astype(F32)))
    return _layer_norm(DEEPNORM_ALPHA * x_ref[r] + g2_ref[0] * y, lng_ref[...], lnb_ref[...])


def _combine_proj_kernel(x_ref, ya_ref, yb_ref, gate_ref, g2_ref, lng_ref, lnb_ref, sc_ref, sh_ref, w_ref,
                         x2_ref, u_ref):
    tiles = _sub_tiles(x_ref.shape[0])
    lhs = []
    for r in tiles:
        x2 = _combine(x_ref, ya_ref, yb_ref, gate_ref, g2_ref, lng_ref, lnb_ref, r)
        x2_ref[r] = x2
        lhs.append((x2 * (1.0 + sc_ref[0]) + sh_ref[0]).astype(BF16))
    for r, a in zip(tiles, lhs):
        u_ref[r] = _dot(a, w_ref[...])


def _combine_kernel(x_ref, ya_ref, yb_ref, gate_ref, g2_ref, lng_ref, lnb_ref, x2_ref):
    for r in _sub_tiles(x_ref.shape[0]):
        x2_ref[r] = _combine(x_ref, ya_ref, yb_ref, gate_ref, g2_ref, lng_ref, lnb_ref, r)


def _combine_call(x1, y2, gate, g2, lng, lnb, rows_per_batch, proj=None, tm=ROW_TILE):
    t, d = x1.shape
    nt = t // tm

    def bmap(i):
        return (i * tm // rows_per_batch, 0, 0)

    row = pl.BlockSpec((tm, d), lambda i: (i, 0))
    mod = pl.BlockSpec((1, 1, d), bmap)
    vec = pl.BlockSpec((1, d), lambda i: (0, 0))
    in_specs = [row, pl.BlockSpec((tm, d // LANES, LANES), lambda i: (i, 0, 0)),
                pl.BlockSpec((tm, d // LANES, LANES), lambda i: (nt + i, 0, 0)),
                pl.BlockSpec((tm, LANES), lambda i: (i, 0)), mod, vec, vec]
    args = [x1, y2, y2, gate, g2, lng, lnb]
    if proj is None:
        return pl.pallas_call(
            _combine_kernel, grid=(nt,), in_specs=in_specs, out_specs=row,
            out_shape=jax.ShapeDtypeStruct((t, d), F32),
            compiler_params=_params("parallel"), name="combine_ln")(*args)
    sc, sh, w = proj
    return pl.pallas_call(
        _combine_proj_kernel, grid=(nt,),
        in_specs=in_specs + [mod, mod, _resident(w.shape)],
        out_specs=[row, pl.BlockSpec((tm, w.shape[1]), lambda i: (i, 0))],
        out_shape=[jax.ShapeDtypeStruct((t, d), F32), jax.ShapeDtypeStruct((t, w.shape[1]), F32)],
        compiler_params=_params("parallel"), name="combine_ln_proj")(*args, sc, sh, w)


def _pool_out_kernel(up_ref, uc_ref, un_ref, wg_ref, ps_ref, wo_ref, *rest, n_seq):
    tm, d = uc_ref.shape
    n_grp = len(POOL_WINDOWS)
    ch = d // n_grp
    halo = POOL_HALO
    tiles = _sub_tiles(tm)
    lhs = []
    for r in tiles:
        n_r = r.stop - r.start
        pos0 = (pl.program_id(0) * tm + r.start) % n_seq
        e_pos = pos0 - halo + lax.broadcasted_iota(jnp.int32, (n_r + 2 * halo, 1), 0)
        e_ok = (e_pos >= 0) & (e_pos < n_seq)
        t_pos = pos0 + lax.broadcasted_iota(jnp.int32, (n_r, 1), 0)
        z = []
        for gi, w in enumerate(POOL_WINDOWS):
            cs = slice(gi * ch, (gi + 1) * ch)
            u = uc_ref[r, cs]
            before = up_ref[:, cs] if r.start == 0 else uc_ref[r.start - halo:r.start, cs]
            after = un_ref[:, cs] if r.stop == tm else uc_ref[r.stop:r.stop + halo, cs]
            ext = jnp.where(e_ok, jnp.concatenate([before, u, after], axis=0), 0.0)
            a, span = ext, 1
            while span < w:
                a = a[:a.shape[0] - span] + a[span:]
                span *= 2
            start = halo - w // 2
            win = a[start:start + n_r]
            cnt = (jnp.minimum(t_pos + (w - w // 2), n_seq) - jnp.maximum(t_pos - w // 2, 0)).astype(F32)
            mixed = (win / cnt - u).astype(BF16)
            z.append((_dot(mixed, wg_ref[gi]) * ps_ref[:, cs]).astype(BF16))
        lhs.append(jnp.concatenate(z, axis=-1))
    ys = [_dot(a, wo_ref[...]) for a in lhs]
    for r, y in zip(tiles, ys):
        _post_mix(y, *rest, r)


def _pool_out(u, w_grp, scale, w_out, x, g1, sc2, sh2, lng, lnb, wr, rb, n_seq, tm=ROW_TILE):
    t, d = x.shape
    hb = tm // POOL_HALO
    n_hb = t // POOL_HALO
    post_in, post_out = _post_specs(d, tm, n_seq)
    return pl.pallas_call(
        functools.partial(_pool_out_kernel, n_seq=n_seq),
        grid=(t // tm,),
        in_specs=[
            pl.BlockSpec((POOL_HALO, d), lambda i: (jnp.maximum(i * hb - 1, 0), 0)),
            pl.BlockSpec((tm, d), lambda i: (i, 0)),
            pl.BlockSpec((POOL_HALO, d), lambda i: (jnp.minimum((i + 1) * hb, n_hb - 1), 0)),
            _resident(w_grp.shape),
            pl.BlockSpec((1, d), lambda i: (0, 0)),
            _resident(w_out.shape),
        ] + post_in,
        out_specs=post_out,
        out_shape=_post_out_shapes(t, d),
        compiler_params=_params("parallel"),
        name="pool_out",
    )(u, u, u, w_grp, scale, w_out, x, g1, sc2, sh2, lng, lnb, wr, rb)


def _moe_kernel(be_ref, nv_ref, first_ref, ws_ref, nxt_ref, idx_ref, idxn_ref, idxp_ref, tok_hbm, w1_hbm, w3_hbm, w2_hbm,
                y_hbm, xbuf, ybuf, xb_ref, wf1, wf3, wf2, w1b, w3b, w2b, gsem, ssem, wsem, *, n_tok, layer, n_blocks):
    i = pl.program_id(0)
    used = nv_ref[jnp.minimum(i, n_blocks - 1)] > 0
    used = used & (i < n_blocks)
    prev_used = (i > 0) & (nv_ref[jnp.maximum(i - 1, 0)] > 0)
    xs = i % 2

    def weight_copies(e, ws):
        return (pltpu.make_async_copy(w1_hbm.at[layer, e], wf1.at[ws], wsem.at[ws]),
                pltpu.make_async_copy(w3_hbm.at[layer, e], wf3.at[ws], wsem.at[ws]),
                pltpu.make_async_copy(w2_hbm.at[layer, e], wf2.at[ws], wsem.at[ws]))

    def gather_start(idx, slot):
        for r in range(MOE_ROWS):
            tok = idx[0, 0, r] & (n_tok - 1)
            pltpu.make_async_copy(tok_hbm.at[tok], xbuf.at[slot, r], gsem.at[slot]).start()

    def gather_wait(slot):
        pltpu.make_async_copy(tok_hbm.at[pl.ds(0, MOE_ROWS)], xbuf.at[slot], gsem.at[slot]).wait()

    def scatter_start(idx, slot):
        for r in range(MOE_ROWS):
            pltpu.make_async_copy(ybuf.at[slot, r], y_hbm.at[idx[0, 0, r]], ssem.at[slot]).start(priority=r % 2)

    def scatter_wait(slot):
        pltpu.make_async_copy(ybuf.at[slot], y_hbm.at[pl.ds(0, MOE_ROWS)], ssem.at[slot]).wait()

    @pl.when(i == 0)
    def _():
        xbuf[...] = jnp.zeros_like(xbuf)
        ybuf[...] = jnp.zeros_like(ybuf)
        spare0 = pltpu.make_async_copy(
            ybuf.at[0], y_hbm.at[pl.ds(MOE_TOP_K * n_tok, MOE_ROWS)], ssem.at[0])
        spare0.start()
        for cp in weight_copies(be_ref[0], 0):
            cp.start(priority=WEIGHT_DMA_PRIORITY)
        gather_start(idx_ref, 0)

    @pl.when(used)
    def _():
        ws = ws_ref[i]

        @pl.when(first_ref[i] == 1)
        def _():
            for cp in weight_copies(be_ref[i], ws):
                cp.wait()
            nxt = nxt_ref[i]

            @pl.when(nxt >= 0)
            def _():
                for cp in weight_copies(nxt, 1 - ws):
                    cp.start(priority=WEIGHT_DMA_PRIORITY)

            w1b[...] = wf1[ws].astype(BF16)
            w3b[...] = wf3[ws].astype(BF16)
            w2b[...] = wf2[ws].astype(BF16)

        def compute(n_rows):
            rows = slice(0, n_rows)
            gather_wait(xs)
            xb_ref[rows] = _rows_from_tiles(xbuf[xs, rows].astype(F32)).astype(BF16)
            gather_start(idxn_ref, 1 - xs)
            scatter_start(idxp_ref, 1 - xs)
            xb = xb_ref[rows]
            h = (_silu(_dot(xb, w1b[...])) * _dot(xb, w3b[...])).astype(BF16)
            y = _tiles_from_rows(_dot(h, w2b[...])).astype(BF16)
            scatter_wait(xs)
            ybuf[xs, rows] = y

        half = MOE_ROWS // 2
        nv = nv_ref[i]

        @pl.when(nv > half)
        def _():
            compute(MOE_ROWS)

        @pl.when(nv <= half)
        def _():
            compute(half)

    @pl.when(jnp.logical_not(used) & prev_used)
    def _():
        gather_wait(xs)
        scatter_start(idxp_ref, 1 - xs)
        scatter_wait(1 - xs)
        scatter_wait(xs)


def _moe_dispatch(eid_t, n_blocks):
    n_tok = eid_t.shape[1]
    n_slots = (n_blocks + 2) * MOE_ROWS
    slot_rows = -(-n_slots // SMEM_1D_TILE)
    assert n_blocks <= LANES and n_tok % SMEM_1D_TILE == 0 and SMEM_1D_TILE % MOE_ROWS == 0
    slot, meta = pl.pallas_call(
        functools.partial(_dispatch_kernel, n_tok=n_tok),
        in_specs=[pl.BlockSpec(memory_space=pltpu.VMEM)],
        out_specs=[pl.BlockSpec(memory_space=pltpu.SMEM), pl.BlockSpec(memory_space=pltpu.VMEM)],
        out_shape=[jax.ShapeDtypeStruct((slot_rows * SMEM_1D_TILE,), jnp.int32),
                   jax.ShapeDtypeStruct((MOE_ID_ROWS, LANES), jnp.int32)],
        scratch_shapes=[pltpu.VMEM((MOE_TOP_K * n_tok,), jnp.int32),
                        pltpu.VMEM((slot_rows * SMEM_1D_TILE,), jnp.int32),
                        pltpu.SMEM((MOE_TOP_K * n_tok,), jnp.int32),
                        pltpu.SemaphoreType.DMA(())],
        name="moe_dispatch",
    )(eid_t)
    return (slot[:n_slots].reshape(n_blocks + 2, 1, MOE_ROWS),) + tuple(meta[r, :n_blocks] for r in range(5))


def _dispatch_kernel(eid_ref, slot_ref, meta_ref, dest_vmem, init_vmem, dest_smem, sem, *, n_tok):
    tile = MOE_ROWS
    n_tiles = n_tok // tile
    sub = lax.broadcasted_iota(jnp.int32, (N_EXPERTS, tile), 0)
    si = lax.broadcasted_iota(jnp.int32, (tile, tile), 0)
    ti = lax.broadcasted_iota(jnp.int32, (tile, tile), 1)
    before = jnp.where(si < ti, 1.0, 0.0).astype(BF16)

    def one_hots(j):
        ids = eid_ref[:, j * tile:(j + 1) * tile]
        return [jnp.where(sub == ids[k:k + 1], 1.0, 0.0) for k in range(MOE_TOP_K)]

    carry = jnp.zeros((N_EXPERTS, 1), F32)
    ranks = []
    for j in range(n_tiles):
        oh = one_hots(j)
        both = oh[0] + oh[1]
        seen = carry + _dot(both.astype(BF16), before)
        ranks.append([jnp.sum(seen * o, axis=0, keepdims=True) for o in oh])
        carry = carry + jnp.sum(both, axis=1, keepdims=True)

    counts = carry
    nblk = jnp.floor((counts + float(MOE_ROWS - 1)) * (1.0 / MOE_ROWS))
    ei = lax.broadcasted_iota(jnp.int32, (N_EXPERTS, N_EXPERTS), 0)
    ej = lax.broadcasted_iota(jnp.int32, (N_EXPERTS, N_EXPERTS), 1)
    lower = jnp.where(ej < ei, 1.0, 0.0).astype(BF16)
    first_blk = _dot(lower, jnp.broadcast_to(nblk, (N_EXPERTS, LANES)).astype(BF16))[:, 0:1]
    first_slot = first_blk * float(MOE_ROWS)

    per_row = SMEM_1D_TILE // tile
    for k in range(MOE_TOP_K):
        for q in range(n_tiles // per_row):
            parts = []
            for j in range(q * per_row, (q + 1) * per_row):
                parts.append(jnp.sum(first_slot * one_hots(j)[k], axis=0, keepdims=True) + ranks[j][k])
            dest = jnp.concatenate(parts, axis=1).astype(jnp.int32)
            dest_vmem[pl.ds(k * n_tok + q * SMEM_1D_TILE, SMEM_1D_TILE)] = dest.reshape(SMEM_1D_TILE)

    lane = lax.broadcasted_iota(jnp.int32, (1, SMEM_1D_TILE), 1)
    for q in range(init_vmem.shape[0] // SMEM_1D_TILE):
        pos = q * SMEM_1D_TILE + lane
        spare = MOE_TOP_K * n_tok + ((pos // MOE_ROWS + 1) % 2) * MOE_ROWS + pos % MOE_ROWS
        init_vmem[pl.ds(q * SMEM_1D_TILE, SMEM_1D_TILE)] = spare.reshape(SMEM_1D_TILE)
    copies = [pltpu.make_async_copy(dest_vmem, dest_smem, sem), pltpu.make_async_copy(init_vmem, slot_ref, sem)]
    for cp in copies:
        cp.start()
    for cp in copies:
        cp.wait()

    def place(t, carry_):
        for k in range(MOE_TOP_K):
            slot_ref[dest_smem[k * n_tok + t] + MOE_ROWS] = k * n_tok + t
        return carry_

    lax.fori_loop(0, n_tok, place, 0, unroll=8)

    b = lax.broadcasted_iota(jnp.int32, (N_EXPERTS, LANES), 1).astype(F32)
    e_col = lax.broadcasted_iota(jnp.int32, (N_EXPERTS, LANES), 0).astype(F32)
    b_row = b[0:1]
    be = jnp.minimum(jnp.sum(jnp.where(first_blk + nblk <= b, 1.0, 0.0), axis=0, keepdims=True), N_EXPERTS - 1.0)
    mine = e_col == be
    cnt_b = jnp.sum(jnp.where(mine, counts, 0.0), axis=0, keepdims=True)
    start_b = jnp.sum(jnp.where(mine, first_blk, 0.0), axis=0, keepdims=True)
    nv = jnp.clip(cnt_b - (b_row - start_b) * MOE_ROWS, 0.0, float(MOE_ROWS))
    nv = jnp.where(b_row < jnp.sum(nblk, axis=0, keepdims=True), nv, 0.0)
    first = jnp.where((nv > 0) & ((b_row == 0) | (be != pltpu.roll(be, 1, 1))), 1.0, 0.0)
    li = lax.broadcasted_iota(jnp.int32, (LANES, LANES), 0)
    lj = lax.broadcasted_iota(jnp.int32, (LANES, LANES), 1)
    upto = jnp.where(li <= lj, 1.0, 0.0).astype(BF16)
    run = _dot(jnp.broadcast_to(first, (MOE_ID_ROWS, LANES)).astype(BF16), upto)[0:1] - 1.0
    ws = run - 2.0 * jnp.floor(run * 0.5)
    later = jnp.min(jnp.where((e_col > be) & (counts > 0), e_col, float(LANES)), axis=0, keepdims=True)
    nxt = jnp.where(later >= float(N_EXPERTS), -1.0, later)
    rows = [be, nv, first, ws, nxt] + [jnp.zeros_like(be)] * (MOE_ID_ROWS - 5)
    meta_ref[...] = jnp.concatenate(rows, axis=0).astype(jnp.int32)


def _moe_experts(tok, eid, w1, w3, w2, layer):
    n_tok, n_sub, _ = tok.shape
    d = n_sub * LANES
    assert n_tok & (n_tok - 1) == 0
    ff = w1.shape[3]
    n_assign = n_tok * MOE_TOP_K
    n_blocks = -(-(n_assign + N_EXPERTS * (MOE_ROWS - 1)) // MOE_ROWS)
    slot, be, nv, first, ws, nxt = _moe_dispatch(eid, n_blocks)
    grid_spec = pltpu.PrefetchScalarGridSpec(
        num_scalar_prefetch=5,
        grid=(n_blocks + 1,),
        in_specs=[
            pl.BlockSpec((1, 1, MOE_ROWS), lambda i, *_: (i + 1, 0, 0), memory_space=pltpu.SMEM),
            pl.BlockSpec((1, 1, MOE_ROWS), lambda i, *_: (jnp.minimum(i + 2, n_blocks + 1), 0, 0),
                         memory_space=pltpu.SMEM),
            pl.BlockSpec((1, 1, MOE_ROWS), lambda i, *_: (i, 0, 0), memory_space=pltpu.SMEM),
            pl.BlockSpec(memory_space=pl.ANY),
            pl.BlockSpec(memory_space=pl.ANY),
            pl.BlockSpec(memory_space=pl.ANY),
            pl.BlockSpec(memory_space=pl.ANY),
        ],
        out_specs=pl.BlockSpec(memory_space=pl.ANY),
        scratch_shapes=[
            pltpu.VMEM((2, MOE_ROWS, n_sub, LANES), BF16), pltpu.VMEM((2, MOE_ROWS, n_sub, LANES), BF16),
            pltpu.VMEM((MOE_ROWS, d), BF16),
            pltpu.VMEM((2, d, ff), F32), pltpu.VMEM((2, d, ff), F32), pltpu.VMEM((2, ff, d), F32),
            pltpu.VMEM((d, ff), BF16), pltpu.VMEM((d, ff), BF16), pltpu.VMEM((ff, d), BF16),
            pltpu.SemaphoreType.DMA((2,)), pltpu.SemaphoreType.DMA((2,)), pltpu.SemaphoreType.DMA((2,)),
        ],
    )
    return pl.pallas_call(
        functools.partial(_moe_kernel, n_tok=n_tok, layer=layer, n_blocks=n_blocks),
        grid_spec=grid_spec,
        out_shape=jax.ShapeDtypeStruct((MOE_TOP_K * n_tok + 2 * MOE_ROWS, n_sub, LANES), BF16),
        compiler_params=pltpu.CompilerParams(dimension_semantics=("arbitrary",),
                                             vmem_limit_bytes=MOE_VMEM_LIMIT_BYTES),
        name="moe_experts",
    )(be, nv, first, ws, nxt, slot, slot, slot, tok, w1, w3, w2)


def _rope_tables(n_seq):
    half = HEAD_DIM // 2
    n_freq = half // 2
    t = jnp.arange(n_seq)
    row = (t // GRID_W).astype(F32)
    col = (t % GRID_W).astype(F32)
    inv_freq = ROPE_BASE ** (-jnp.arange(n_freq, dtype=F32) / n_freq)
    ang_r = row[:, None] * inv_freq[None, :]
    ang_c = col[:, None] * inv_freq[None, :]
    cos = jnp.concatenate([jnp.cos(ang_r)] * 2 + [jnp.cos(ang_c)] * 2, axis=-1)
    sin = jnp.concatenate([-jnp.sin(ang_r), jnp.sin(ang_r), -jnp.sin(ang_c), jnp.sin(ang_c)], axis=-1)
    return cos, sin


def _router_weights(w_g, b_g, w_e, b_e):
    d = w_g.shape[0]
    n = w_g.shape[1] + w_e.shape[1]
    wr = jnp.concatenate([w_g, w_e, jnp.zeros((d, LANES - n), F32)], axis=1)
    rb = jnp.concatenate([b_g, b_e, jnp.zeros((LANES - n,), F32)]).reshape(1, LANES)
    hi = wr.astype(BF16)
    lo = (wr - hi.astype(F32)).astype(BF16)
    return jnp.concatenate([hi, lo], axis=1), rb


def kernel(x, c, ctx, c_ctx, ada_w, ada_b, ln_g, ln_b, mix_w_in, att_sink, hg_lb, hg_norm_g, mix_w_out, pool_w_in, pool_w_grp, pool_scale, pool_w_out, rt_group_w, rt_group_b, rt_expert_w, rt_expert_b, moe_w1, moe_w3, moe_w2):
    b, n, d = x.shape
    n_ctx = ctx.shape[1]
    t = b * n
    xf = x.reshape(t, d)
    ctxf = ctx.reshape(b * n_ctx, d)

    cond = jnp.concatenate([c, c_ctx[None, :], jnp.zeros((SUBLANES - b - 1, d), F32)], axis=0)
    mod = _ada_mod(cond, ada_w, ada_b)

    def chunk(l, j, rows=slice(0, b)):
        return mod[l, rows, j * d:(j + 1) * d][:, None, :]

    w_in = mix_w_in[0].astype(BF16)
    cos, sin = _rope_tables(n)
    q_w, kv_w = ATT_HEADS * HEAD_DIM, ATT_KV_HEADS * HEAD_DIM
    n_att = q_w + 2 * kv_w
    hk = HG_HEADS * HG_KEY
    assert n_att == 2 * PROJ_TN and hk == PROJ_TN
    a_lat, f_lat = _mod_matmul(xf, chunk(0, 1), chunk(0, 0), w_in, cos, sin,
                               lambda j: jnp.where(j < 3, j, jnp.where(j < 5, j + 2, j - 2)), n_att + 3 * hk, 2 * hk,
                               n_q=q_w, n_rope=q_w + kv_w, n_seq=n, tm=PROJ_TM, tn=PROJ_TN)
    ctx_rows = slice(b, b + 1)
    a_ctx, f_ctx = _mod_matmul(ctxf, chunk(0, 1, ctx_rows), chunk(0, 0, ctx_rows), w_in, cos, sin,
                               lambda j: jnp.where(j < 1, 1, jnp.where(j < 2, 5, j + 1)), 2 * kv_w + hk, 2 * hk,
                               n_q=0, n_rope=0, n_seq=n, tm=b * n_ctx, tn=PROJ_TN)
    att = _window_attention(a_lat, a_ctx, att_sink[0], b, n, n_ctx)
    o_f, o_b = _hgrn2_scan(a_lat, f_lat, a_ctx, f_ctx, hg_lb, b, n, n_ctx)
    wr, rb = _router_weights(rt_group_w[0], rt_group_b[0], rt_expert_w[0], rt_expert_b[0])
    x1, tok, eid, gate = _even_out(
        att, o_f, o_b, a_lat, hg_norm_g[0][None, :], mix_w_out[0].astype(BF16), xf,
        chunk(0, 2), chunk(0, 4), chunk(0, 3), ln_g[0, 0][None, :], ln_b[0, 0][None, :], wr, rb, n)
    y2 = _moe_experts(tok, eid, moe_w1, moe_w3, moe_w2, 0)

    x2, u = _combine_call(x1, y2, gate, chunk(0, 5), ln_g[0, 1][None, :], ln_b[0, 1][None, :], n,
                          proj=(chunk(1, 1), chunk(1, 0), pool_w_in[0].astype(BF16)))
    wr, rb = _router_weights(rt_group_w[1], rt_group_b[1], rt_expert_w[1], rt_expert_b[1])
    x3, tok, eid, gate = _pool_out(
        u, pool_w_grp[0].astype(BF16), pool_scale[0][None, :], pool_w_out[0].astype(BF16), x2,
        chunk(1, 2), chunk(1, 4), chunk(1, 3), ln_g[1, 0][None, :], ln_b[1, 0][None, :], wr, rb, n)
    y2 = _moe_experts(tok, eid, moe_w1, moe_w3, moe_w2, 1)
    out = _combine_call(x3, y2, gate, chunk(1, 5), ln_g[1, 1][None, :], ln_b[1, 1][None, :], n)
    return out.reshape(b, n, d)
```

```python
import functools

import jax
import jax.numpy as jnp
import numpy as np
from jax import lax
from jax.experimental import pallas as pl
from jax.experimental.pallas import tpu as pltpu

F32 = jnp.float32
BF16 = jnp.bfloat16

LANES = 128
SUBLANES = 8
MXU_TILE = 256
VMEM_LIMIT_BYTES = 56 * 1024 * 1024
MOE_VMEM_LIMIT_BYTES = 60 * 1024 * 1024

GRID_W = 64
ATT_HEADS = 8
ATT_KV_HEADS = 4
ATT_GROUP = ATT_HEADS // ATT_KV_HEADS
HEAD_DIM = 128
ROPE_FREQS = HEAD_DIM // 4
WINDOW = 128
ATT_BLOCK = 128
ROPE_BASE = 10000.0
HG_HEADS = 8
HG_KEY = 128
HG_CHUNK = 64
HG_STEP_CHUNKS = 4
HG_SUB = 16
HG_FAST_RANGE = 80.0
NORM_EPS = 1e-6
POOL_WINDOWS = (2, 4, 8, 16)
POOL_HALO = 8
MOE_GROUPS = 4
MOE_EXPERTS_PER_GROUP = 8
N_EXPERTS = MOE_GROUPS * MOE_EXPERTS_PER_GROUP
MOE_TOP_K = 2
MOE_ROWS = 256
ROW_TILE = 512
SUB_ROWS = 256
PROJ_TM = 1024
PROJ_TN = 1024
MOE_ID_ROWS = 8
SMEM_1D_TILE = 1024
WEIGHT_DMA_PRIORITY = 1
LN_EPS = 1e-5
DEPTH = 2
DEEPNORM_ALPHA = (2 * DEPTH) ** 0.25


def _dot(a, b):
    return jnp.dot(a, b, preferred_element_type=F32)


def _dot_nt(a, b):
    return lax.dot_general(a, b, (((1,), (1,)), ((), ())), preferred_element_type=F32)


def _dot_tn(a, b):
    return lax.dot_general(a, b, (((0,), (0,)), ((), ())), preferred_element_type=F32)


def _sigmoid(x):
    return 1.0 / (1.0 + jnp.exp(-x))


def _silu(x):
    return x * _sigmoid(x)


def _params(*sem):
    return pltpu.CompilerParams(dimension_semantics=sem, vmem_limit_bytes=VMEM_LIMIT_BYTES)


def _tiles_from_rows(x):
    n = x.shape[1] // LANES
    return jnp.swapaxes(jnp.stack([x[:, s * LANES:(s + 1) * LANES] for s in range(n)], axis=0), 0, 1)


def _rows_from_tiles(x3):
    xt = jnp.swapaxes(x3, 0, 1)
    return jnp.concatenate([xt[s] for s in range(xt.shape[0])], axis=-1)


def _layer_norm(z, g, b):
    mu = jnp.mean(z, axis=-1, keepdims=True)
    zc = z - mu
    var = jnp.mean(zc * zc, axis=-1, keepdims=True)
    return zc * lax.rsqrt(var + LN_EPS) * g + b


def _ada_kernel(s_ref, w_ref, b_ref, o_ref):
    s = _silu(s_ref[...]).astype(BF16)
    o_ref[0] = _dot(s, w_ref[0].astype(BF16)) + b_ref[0]


def _ada_mod(s, ada_w, ada_b, tn=PROJ_TN):
    n_l, d, n = ada_w.shape
    return pl.pallas_call(
        _ada_kernel,
        grid=(n_l, n // tn),
        in_specs=[
            pl.BlockSpec((SUBLANES, d), lambda l, j: (0, 0)),
            pl.BlockSpec((1, d, tn), lambda l, j: (l, 0, j)),
            pl.BlockSpec((1, 1, tn), lambda l, j: (l, 0, j)),
        ],
        out_specs=pl.BlockSpec((1, SUBLANES, tn), lambda l, j: (l, 0, j)),
        out_shape=jax.ShapeDtypeStruct((n_l, SUBLANES, n), F32),
        compiler_params=_params("parallel", "parallel"),
        name="ada_mod",
    )(s, ada_w, ada_b.reshape(n_l, 1, n))


def _rope(t, cos, sin_signed, first_half):
    partner = jnp.where(first_half, pltpu.roll(t, HEAD_DIM - ROPE_FREQS, 1), pltpu.roll(t, ROPE_FREQS, 1))
    return t * cos + partner * sin_signed


def _modmm_kernel(x_ref, sc_ref, sh_ref, w_ref, cos_ref, sin_ref, oa_ref, ob_ref, xs_ref, *,
                  n_q, n_rope, n_a_tiles):
    j = pl.program_id(1)
    tm, tn = x_ref.shape[0], w_ref.shape[1]
    halves = [slice(0, tm // 2), slice(tm // 2, tm)]

    def dots():
        return [_dot(xs_ref[r], w_ref[...]) for r in halves]

    def store_bf16(jt, r, acc):
        lane = lax.broadcasted_iota(jnp.int32, (1, HEAD_DIM), 1)
        first_half = (lane % (2 * ROPE_FREQS)) < ROPE_FREQS
        for h in range(tn // HEAD_DIM):
            sl = slice(h * HEAD_DIM, (h + 1) * HEAD_DIM)
            col = jt * tn + h * HEAD_DIM
            if col < n_rope:
                scale = HEAD_DIM ** -0.5 if col < n_q else 1.0
                oa_ref[r, sl] = _rope(acc[:, sl], cos_ref[r] * scale, sin_ref[r] * scale, first_half).astype(BF16)
            else:
                oa_ref[r, sl] = acc[:, sl].astype(BF16)

    for jt in range(n_a_tiles):
        @pl.when(j == jt)
        def _(jt=jt):
            if jt == 0:
                for r in halves:
                    xs_ref[r] = (x_ref[r] * (1.0 + sc_ref[0]) + sh_ref[0]).astype(BF16)
            for r, acc in zip(halves, dots()):
                store_bf16(jt, r, acc)

    @pl.when(j >= n_a_tiles)
    def _():
        for r, acc in zip(halves, dots()):
            ob_ref[r] = acc


def _mod_matmul(x, sc, sh, w, cos, sin, col_map, n_a, n_b, n_q, n_rope, n_seq, tm, tn):
    m, k = x.shape
    rows_per_mod = m // sc.shape[0]
    ta, tb = n_a // tn, n_b // tn
    tab = pl.BlockSpec((tm, HEAD_DIM), lambda i, j: ((i * tm % n_seq) // tm, 0))
    return pl.pallas_call(
        functools.partial(_modmm_kernel, n_q=n_q, n_rope=n_rope, n_a_tiles=ta),
        grid=(m // tm, ta + tb),
        in_specs=[
            pl.BlockSpec((tm, k), lambda i, j: (i, 0)),
            pl.BlockSpec((1, 1, k), lambda i, j: (i * tm // rows_per_mod, 0, 0)),
            pl.BlockSpec((1, 1, k), lambda i, j: (i * tm // rows_per_mod, 0, 0)),
            pl.BlockSpec((k, tn), lambda i, j: (0, col_map(j))),
            tab, tab,
        ],
        out_specs=[pl.BlockSpec((tm, tn), lambda i, j: (i, jnp.minimum(j, ta - 1))),
                   pl.BlockSpec((tm, tn), lambda i, j: (i, jnp.maximum(j - ta, 0)))],
        out_shape=[jax.ShapeDtypeStruct((m, n_a), BF16), jax.ShapeDtypeStruct((m, n_b), F32)],
        scratch_shapes=[pltpu.VMEM((tm, k), BF16)],
        compiler_params=_params("parallel", "arbitrary"),
        name="mod_matmul",
    )(x, sc, sh, w, cos, sin)


def _attn_kernel(sink_ref, q_ref, kp_ref, kc_ref, kn_ref, vp_ref, vc_ref, vn_ref, kx_ref, vx_ref,
                 mp_ref, mn_ref, o_ref, *, n_blocks):
    s_idx = pl.program_id(1)
    blk = ATT_BLOCK
    row1 = lax.broadcasted_iota(jnp.int32, (ATT_GROUP * blk, 1), 0)
    lo, hi = slice(0, blk), slice(blk, 2 * blk)
    subs = [(lo, (kp_ref, vp_ref, lo), (kc_ref, vc_ref, lo), (kc_ref, vc_ref, hi), s_idx > 0, True),
            (hi, (kc_ref, vc_ref, lo), (kc_ref, vc_ref, hi), (kn_ref, vn_ref, lo), True, s_idx < n_blocks // 2 - 1)]
    jobs = [(sub, h) for sub in subs for h in range(ATT_KV_HEADS)]

    def kv(h):
        return slice(h * HEAD_DIM, (h + 1) * HEAD_DIM)

    def gather(sub, h, which):
        blocks = [t[which][t[2], kv(h)] for t in sub[1:4]]
        return jnp.concatenate(blocks + [(kx_ref, vx_ref)[which][:, kv(h)]], axis=0)

    scores = []
    for sub, h in jobs:
        q2 = jnp.concatenate([q_ref[sub[0], (ATT_GROUP * h + g) * HEAD_DIM:(ATT_GROUP * h + g + 1) * HEAD_DIM]
                              for g in range(ATT_GROUP)], axis=0)
        scores.append(_dot_nt(q2, gather(sub, h, 0)))
    probs, dens = [], []
    for (sub, h), s in zip(jobs, scores):
        has_prev, has_next = sub[4], sub[5]
        parts = [jnp.where(has_prev, s[:, :blk] + mp_ref[...], -jnp.inf), s[:, blk:2 * blk],
                 jnp.where(has_next, s[:, 2 * blk:3 * blk] + mn_ref[...], -jnp.inf)]
        parts += [s[:, c:c + blk] for c in range(3 * blk, s.shape[1], blk)]
        sink = jnp.where(row1 < blk, sink_ref[ATT_GROUP * h], sink_ref[ATT_GROUP * h + 1])
        m = jnp.maximum(jnp.max(functools.reduce(jnp.maximum, parts), axis=-1, keepdims=True), sink)
        p = [jnp.exp(x - m) for x in parts]
        dens.append(jnp.sum(functools.reduce(jnp.add, p), axis=-1, keepdims=True) + jnp.exp(sink - m))
        probs.append(jnp.concatenate(p, axis=-1).astype(BF16))
    for (sub, h), p, den in zip(jobs, probs, dens):
        o = _dot(p, gather(sub, h, 1)) / den
        for g in range(ATT_GROUP):
            col = (ATT_GROUP * h + g) * HEAD_DIM
            o_ref[sub[0], col:col + HEAD_DIM] = o[g * blk:(g + 1) * blk].astype(o_ref.dtype)


def _window_attention(qkv, kv_ctx, sink, batch, n_seq, n_ctx):
    assert ATT_GROUP == 2 and WINDOW == ATT_BLOCK
    nb = n_seq // ATT_BLOCK
    qw, kw = ATT_HEADS * HEAD_DIM, ATT_KV_HEADS * HEAD_DIM
    kcol, vcol = qw // kw, qw // kw + 1

    assert nb % 2 == 0
    pair = 2 * ATT_BLOCK

    def before(b, s):
        return b * nb + jnp.maximum(2 * s - 1, 0)

    def after(b, s):
        return b * nb + jnp.minimum(2 * s + 2, nb - 1)

    def own(b, s):
        return b * (nb // 2) + s

    def band(col):
        return [pl.BlockSpec((ATT_BLOCK, kw), lambda b, s: (before(b, s), col)),
                pl.BlockSpec((pair, kw), lambda b, s: (own(b, s), col)),
                pl.BlockSpec((ATT_BLOCK, kw), lambda b, s: (after(b, s), col))]

    kspec, vspec = band(kcol), band(vcol)
    r = jnp.arange(ATT_GROUP * ATT_BLOCK)[:, None] % ATT_BLOCK
    c = jnp.arange(ATT_BLOCK)[None, :]
    mask_prev = jnp.where(c >= r, 0.0, -jnp.inf).astype(F32)
    mask_next = jnp.where(c <= r, 0.0, -jnp.inf).astype(F32)
    mspec = pl.BlockSpec(mask_prev.shape, lambda b, s: (0, 0))
    return pl.pallas_call(
        functools.partial(_attn_kernel, n_blocks=nb),
        grid=(batch, nb // 2),
        in_specs=[pl.BlockSpec(memory_space=pltpu.SMEM),
                  pl.BlockSpec((pair, qw), lambda b, s: (own(b, s), 0))]
        + kspec + vspec
        + [pl.BlockSpec((n_ctx, kw), lambda b, s: (b, 0)), pl.BlockSpec((n_ctx, kw), lambda b, s: (b, 1)),
           mspec, mspec],
        out_specs=pl.BlockSpec((pair, qw), lambda b, s: (own(b, s), 0)),
        out_shape=jax.ShapeDtypeStruct((batch * n_seq, qw), BF16),
        compiler_params=_params("parallel", "parallel"),
        name="window_attention",
    )(sink, qkv, qkv, qkv, qkv, qkv, qkv, qkv, kv_ctx, kv_ctx, mask_prev, mask_next)


def _gla_step(zf, q_raw, v, lb, st_ref, o_ref, rev):
    c_len = HG_CHUNK
    n_rows = zf.shape[0]
    n_sub = n_rows // c_len
    shift = c_len.bit_length() - 1
    order = range(n_sub - 1, -1, -1) if rev else range(n_sub)

    def head(h):
        return slice(h * HG_KEY, (h + 1) * HG_KEY)

    def chunk(i):
        return slice(i * c_len, (i + 1) * c_len)

    def seen(n):
        ri = lax.broadcasted_iota(jnp.int32, (n, n), 0)
        ci = lax.broadcasted_iota(jnp.int32, (n, n), 1)
        return ((ri >> shift) == (ci >> shift)) & ((ci >= ri) if rev else (ci <= ri))

    f = lb + (1.0 - lb) * _sigmoid(zf)
    k = 1.0 - f
    g = jnp.log(f)
    tri = jnp.where(seen(n_rows), 1.0, 0.0).astype(BF16)
    g1 = g.astype(BF16)
    r1 = g - g1.astype(F32)
    g2 = r1.astype(BF16)
    g3 = (r1 - g2.astype(F32)).astype(BF16)
    c = _dot(tri, g1) + _dot(tri, g2) + _dot(tri, g3)
    c_end = [c[i * c_len:i * c_len + 1] if rev else c[(i + 1) * c_len - 1:(i + 1) * c_len] for i in range(n_sub)]
    c_end_rows = jnp.concatenate([jnp.broadcast_to(ce, (c_len, ce.shape[1])) for ce in c_end], axis=0)
    k_end = (k * jnp.exp(c_end_rows - c)).astype(BF16)
    dec = [jnp.exp(ce) for ce in c_end]
    vb = v.astype(BF16)

    def advance(states, i):
        new = []
        for p in range(0, HG_HEADS, 2):
            lanes = slice(p * HG_KEY, (p + 2) * HG_KEY)
            inc = _dot_tn(vb[chunk(i), lanes], k_end[chunk(i), lanes])
            for j in range(2):
                blk = slice(j * HG_KEY, (j + 1) * HG_KEY)
                new.append(states[p + j] * dec[i][:, head(p + j)] + inc[blk, blk])
        return new

    if o_ref is None:
        states = [st_ref[h] for h in range(HG_HEADS)]
        for i in order:
            states = advance(states, i)
        for h in range(HG_HEADS):
            st_ref[h] = states[h]
        return

    q = _silu(q_raw.astype(F32))
    q_in = (q * jnp.exp(c)).astype(BF16)
    lowest = functools.reduce(jnp.minimum, c_end)

    def stack(x, i, h0, n):
        return jnp.concatenate([x[chunk(i), head(h)] for h in range(h0, h0 + n)], axis=0)

    n_qk = MXU_TILE // c_len
    groups = [(i, h0) for i in order for h0 in range(0, HG_HEADS, n_qk)]
    work = {}

    def fast_scores():
        k_in = (k * jnp.exp(-c)).astype(BF16)
        work['raw'] = [_dot_nt(stack(q_in, i, h0, n_qk), stack(k_in, i, h0, n_qk)) for i, h0 in groups]

    def fast_mask():
        same = seen(n_qk * c_len)
        work['masked'] = [jnp.where(same, s, 0.0).astype(BF16) for s in work['raw']]

    def fast_values():
        intra = {}
        for (i, h0), sc in zip(groups, work['masked']):
            pv = _dot(sc, stack(vb, i, h0, n_qk))
            for j in range(n_qk):
                intra[i, h0 + j] = pv[j * c_len:(j + 1) * c_len]
        work['intra'] = intra

    def fast_chain():
        states = [st_ref[h] for h in range(HG_HEADS)]
        for i in order:
            out = []
            for p in range(0, HG_HEADS, 2):
                st_pair = jnp.concatenate([states[p].astype(BF16), states[p + 1].astype(BF16)], axis=0)
                inter = _dot_nt(stack(q_in, i, p, 2), st_pair)
                for j in range(2):
                    out.append(inter[j * c_len:(j + 1) * c_len, j * HG_KEY:(j + 1) * HG_KEY]
                               + work['intra'][i, p + j])
            o_ref[chunk(i), :] = jnp.concatenate(out, axis=1)
            states = advance(states, i)
        for h in range(HG_HEADS):
            st_ref[h] = states[h]

    def exact():
        for i in order:
            r = chunk(i)
            _gla_intra_exact(q[r], k[r], v[r].astype(F32), vb[r], c[r], q_in[r], st_ref, o_ref, r, rev)
            states = advance([st_ref[h] for h in range(HG_HEADS)], i)
            for h in range(HG_HEADS):
                st_ref[h] = states[h]

    return lowest, (fast_scores, fast_mask, fast_values, fast_chain), exact


def _gla_run(dirs):
    parts = [_gla_step(*d) for d in dirs]
    if parts[0] is None:
        return
    lowest = functools.reduce(jnp.minimum, [p[0] for p in parts])
    in_range = jnp.min(lowest) >= -HG_FAST_RANGE

    @pl.when(in_range)
    def _():
        for phase in zip(*[p[1] for p in parts]):
            for fn in phase:
                fn()

    @pl.when(jnp.logical_not(in_range))
    def _():
        for p in parts:
            p[2]()


def _gla_intra_exact(q, k, v, vb, c, q_in, st_ref, o_ref, rows, rev):
    c_len = q.shape[0]
    pairs = []
    size = c_len // 2
    while size >= HG_SUB:
        for lo in range(0, c_len, 2 * size):
            pairs.append((lo, lo + size, lo + 2 * size))
        size //= 2
    scaled = []
    for lo, mid, hi in pairs:
        if rev:
            late, early, bnd = slice(lo, mid), slice(mid, hi), mid
        else:
            late, early, bnd = slice(mid, hi), slice(lo, mid), mid - 1
        cb = c[bnd:bnd + 1]
        q_l = (q[late] * jnp.exp(c[late] - cb)).astype(BF16)
        k_e = (k[early] * jnp.exp(cb - c[early])).astype(BF16)
        scaled.append((late, early, q_l, k_e))
    n_sub = c_len // HG_SUB
    t_idx = lax.broadcasted_iota(jnp.int32, (HG_SUB, 1), 0)
    diag = [[None] * HG_HEADS for _ in range(n_sub)]
    for b in range(n_sub):
        r0 = b * HG_SUB
        qb, cb = q[r0:r0 + HG_SUB], c[r0:r0 + HG_SUB]
        for s in range(HG_SUB):
            row = r0 + s
            ok = (t_idx <= s) if rev else (t_idx >= s)
            w = qb * k[row:row + 1] * jnp.exp(jnp.where(ok, cb - c[row:row + 1], -jnp.inf))
            for h in range(HG_HEADS):
                sl = slice(h * HG_KEY, (h + 1) * HG_KEY)
                contrib = jnp.sum(w[:, sl], axis=-1, keepdims=True) * v[row:row + 1, sl]
                diag[b][h] = contrib if diag[b][h] is None else diag[b][h] + contrib

    for h in range(HG_HEADS):
        sl = slice(h * HG_KEY, (h + 1) * HG_KEY)
        o_h = _dot_nt(q_in[:, sl], st_ref[h].astype(BF16))
        parts = [diag[b][h] for b in range(n_sub)]
        for late, early, q_l, k_e in scaled:
            sc = _dot_nt(q_l[:, sl], k_e[:, sl]).astype(BF16)
            add = _dot(sc, vb[early, sl])
            b0 = late.start // HG_SUB
            for j in range((late.stop - late.start) // HG_SUB):
                parts[b0 + j] = parts[b0 + j] + add[j * HG_SUB:(j + 1) * HG_SUB]
        o_ref[rows, sl] = o_h + jnp.concatenate(parts, axis=0)


def _gla_kernel(lb_ref, *refs, n_ctx_steps):
    lat = [refs[0:3], refs[3:6]]
    ctx = [refs[6:8], refs[8:10]]
    outs, states = refs[10:12], refs[12:14]
    s = pl.program_id(1)

    @pl.when(s == 0)
    def _():
        for st_ref in states:
            st_ref[...] = jnp.zeros_like(st_ref)

    lbs = []
    for d in range(2):
        x = lb_ref[d]
        e = jnp.exp(x - jnp.max(x, axis=0, keepdims=True))
        lbs.append(e[0:1] / jnp.sum(e, axis=0, keepdims=True))

    @pl.when(s < n_ctx_steps)
    def _():
        _gla_run([(ctx[d][0][...], None, ctx[d][1][...], lbs[d], states[d], None, d == 1) for d in range(2)])

    @pl.when(s >= n_ctx_steps)
    def _():
        _gla_run([(lat[d][0][...], lat[d][1][...], lat[d][2][...], lbs[d], states[d], outs[d], d == 1)
                  for d in range(2)])


def _hgrn2_scan(a_lat, f_lat, a_ctx, f_ctx, hg_lb, batch, n_seq, n_ctx):
    hk = HG_HEADS * HG_KEY
    rows = HG_CHUNK * HG_STEP_CHUNKS
    assert n_seq % rows == 0 and n_ctx % rows == 0
    nc, ncc = n_seq // rows, n_ctx // rows

    def lat(rev):
        def index(b, s):
            j = jnp.maximum(s - ncc, 0)
            return b * nc + (nc - 1 - j if rev else j)
        return index

    def ctx(rev):
        def index(b, s):
            j = jnp.minimum(s, ncc - 1)
            return b * ncc + (ncc - 1 - j if rev else j)
        return index

    def blk(index, col):
        return pl.BlockSpec((rows, hk), lambda b, s: (index(b, s), col))

    lat_specs, ctx_specs, out_specs = [], [], []
    for d, rev in enumerate((False, True)):
        lat_specs += [blk(lat(rev), d), blk(lat(rev), 2), blk(lat(rev), 3)]
        ctx_specs += [blk(ctx(rev), d), blk(ctx(rev), 1)]
        out_specs.append(blk(lat(rev), 0))
    return pl.pallas_call(
        functools.partial(_gla_kernel, n_ctx_steps=ncc),
        grid=(batch, ncc + nc),
        in_specs=[pl.BlockSpec((2, hg_lb.shape[1], hk), lambda b, s: (0, 0, 0))] + lat_specs + ctx_specs,
        out_specs=out_specs,
        out_shape=[jax.ShapeDtypeStruct((batch * n_seq, hk), F32)] * 2,
        scratch_shapes=[pltpu.VMEM((HG_HEADS, HG_KEY, HG_KEY), F32)] * 2,
        compiler_params=_params("parallel", "arbitrary"),
        name="hgrn2",
    )(hg_lb, f_lat, a_lat, a_lat, f_lat, a_lat, a_lat, f_ctx, a_ctx, f_ctx, a_ctx)


def _route(tok, wr_ref, rb):
    t_hi = tok.astype(BF16)
    t_lo = (tok - t_hi.astype(F32)).astype(BF16)
    hi_both = _dot(t_hi, wr_ref[...])
    lg = hi_both[:, :LANES] + hi_both[:, LANES:] + _dot(t_lo, wr_ref[:, :LANES]) + rb
    lane = lax.broadcasted_iota(jnp.int32, lg.shape, 1)
    lane_f = lane.astype(F32)
    ninf = -jnp.inf
    gl = jnp.where(lane < MOE_GROUPS, lg, ninf)
    gmax = jnp.max(gl, axis=-1, keepdims=True)
    g_idx = jnp.min(jnp.where(gl == gmax, lane_f, float(LANES)), axis=-1, keepdims=True)
    g_val = 1.0 / jnp.sum(jnp.exp(gl - gmax), axis=-1, keepdims=True)
    e_lane = lane_f - float(MOE_GROUPS)
    lo = g_idx * float(MOE_EXPERTS_PER_GROUP)
    in_grp = (e_lane >= lo) & (e_lane < lo + float(MOE_EXPERTS_PER_GROUP))
    el = jnp.where(in_grp, lg, ninf)
    l1 = jnp.max(el, axis=-1, keepdims=True)
    i1 = jnp.min(jnp.where(el == l1, e_lane, float(LANES)), axis=-1, keepdims=True)
    el2 = jnp.where(e_lane == i1, ninf, el)
    l2 = jnp.max(el2, axis=-1, keepdims=True)
    i2 = jnp.min(jnp.where(el2 == l2, e_lane, float(LANES)), axis=-1, keepdims=True)
    r = jnp.exp(l2 - l1)
    w1 = g_val / (1.0 + r)
    w2 = w1 * r
    eid = jnp.where(lane == 0, i1, jnp.where(lane == 1, i2, 0.0))
    gate = jnp.where(lane == 0, w1, jnp.where(lane == 1, w2, 0.0))
    eid_t = jnp.transpose(eid)[:MOE_ID_ROWS].astype(jnp.int32)
    return eid_t, gate


def _post_mix(y, x_ref, g1_ref, sc2_ref, sh2_ref, lng_ref, lnb_ref, wr_ref, rb_ref,
              x1_ref, tok_ref, eid_ref, gate_ref, r):
    x1 = _layer_norm(DEEPNORM_ALPHA * x_ref[r] + g1_ref[0] * y, lng_ref[...], lnb_ref[...])
    x1_ref[r] = x1
    tok = x1 * (1.0 + sc2_ref[0]) + sh2_ref[0]
    tok_ref[r] = _tiles_from_rows(tok).astype(BF16)
    eid, gate = _route(tok, wr_ref, rb_ref[...])
    eid_ref[:, r] = eid
    gate_ref[r] = gate


def _sub_tiles(n_rows):
    return [slice(s, s + SUB_ROWS) for s in range(0, n_rows, SUB_ROWS)]


def _even_out_kernel(att_ref, of_ref, ob_ref, gt_ref, ng_ref, wo_ref, *rest):
    tiles = _sub_tiles(att_ref.shape[0])
    lhs = []
    for r in tiles:
        o = of_ref[r] + ob_ref[r]
        pieces = []
        for h in range(HG_HEADS):
            oh = o[:, h * HG_KEY:(h + 1) * HG_KEY]
            pieces.append(oh * lax.rsqrt(jnp.mean(oh * oh, axis=-1, keepdims=True) + NORM_EPS))
        hg = (jnp.concatenate(pieces, axis=-1) * ng_ref[...] * _silu(gt_ref[r].astype(F32))).astype(BF16)
        lhs.append(jnp.concatenate([att_ref[r], hg], axis=-1))
    ys = [_dot(a, wo_ref[...]) for a in lhs]
    for r, y in zip(tiles, ys):
        _post_mix(y, *rest, r)


def _post_specs(d, tm, rows_per_batch):
    def bmap(i):
        return (i * tm // rows_per_batch, 0, 0)

    row = pl.BlockSpec((tm, d), lambda i: (i, 0))
    mod = pl.BlockSpec((1, 1, d), bmap)
    vec = pl.BlockSpec((1, d), lambda i: (0, 0))
    rw = pl.BlockSpec((d, 2 * LANES), lambda i: (0, 0))
    in_specs = [row, mod, mod, mod, vec, vec, rw, pl.BlockSpec((1, LANES), lambda i: (0, 0))]
    lane_blk = pl.BlockSpec((tm, LANES), lambda i: (i, 0))
    tiles = pl.BlockSpec((tm, d // LANES, LANES), lambda i: (i, 0, 0))
    out_specs = [row, tiles, pl.BlockSpec((MOE_ID_ROWS, tm), lambda i: (0, i)), lane_blk]
    return in_specs, out_specs


def _post_out_shapes(t, d):
    return [jax.ShapeDtypeStruct((t, d), F32), jax.ShapeDtypeStruct((t, d // LANES, LANES), BF16),
            jax.ShapeDtypeStruct((MOE_ID_ROWS, t), jnp.int32), jax.ShapeDtypeStruct((t, LANES), F32)]


def _resident(shape):
    return pl.BlockSpec(shape, lambda i: (0,) * len(shape), pipeline_mode=pl.Buffered(1))


def _even_out(att, o_f, o_b, p, norm_g, w_out, x, g1, sc2, sh2, lng, lnb, wr, rb, rows_per_batch,
              tm=ROW_TILE):
    t, d = x.shape
    hv = o_f.shape[1]
    post_in, post_out = _post_specs(d, tm, rows_per_batch)
    return pl.pallas_call(
        _even_out_kernel,
        grid=(t // tm,),
        in_specs=[
            pl.BlockSpec((tm, att.shape[1]), lambda i: (i, 0)),
            pl.BlockSpec((tm, hv), lambda i: (i, 0)),
            pl.BlockSpec((tm, hv), lambda i: (i, 0)),
            pl.BlockSpec((tm, hv), lambda i: (i, 4)),
            pl.BlockSpec((1, hv), lambda i: (0, 0)),
            _resident(w_out.shape),
        ] + post_in,
        out_specs=post_out,
        out_shape=_post_out_shapes(t, d),
        compiler_params=_params("parallel"),
        name="even_out",
    )(att, o_f, o_b, p, norm_g, w_out, x, g1, sc2, sh2, lng, lnb, wr, rb)


def _combine(x_ref, ya_ref, yb_ref, gate_ref, g2_ref, lng_ref, lnb_ref, r):
    gate = gate_ref[r]
    y = (gate[:, 0:1] * _rows_from_tiles(ya_ref[r].astype(F32))
         + gate[:, 1:2] * _rows_from_tiles(yb_ref[r].astype(F32)))
    return _layer_norm(DEEPNORM_ALPHA * x_ref[r] + g2_ref[0] * y, lng_ref[...], lnb_ref[...])


def _combine_proj_kernel(x_ref, ya_ref, yb_ref, gate_ref, g2_ref, lng_ref, lnb_ref, sc_ref, sh_ref, w_ref,
                         x2_ref, u_ref):
    tiles = _sub_tiles(x_ref.shape[0])
    lhs = []
    for r in tiles:
        x2 = _combine(x_ref, ya_ref, yb_ref, gate_ref, g2_ref, lng_ref, lnb_ref, r)
        x2_ref[r] = x2
        lhs.append((x2 * (1.0 + sc_ref[0]) + sh_ref[0]).astype(BF16))
    for r, a in zip(tiles, lhs):
        u_ref[r] = _dot(a, w_ref[...])


def _combine_kernel(x_ref, ya_ref, yb_ref, gate_ref, g2_ref, lng_ref, lnb_ref, x2_ref):
    for r in _sub_tiles(x_ref.shape[0]):
        x2_ref[r] = _combine(x_ref, ya_ref, yb_ref, gate_ref, g2_ref, lng_ref, lnb_ref, r)


def _combine_call(x1, y2, gate, g2, lng, lnb, rows_per_batch, proj=None, tm=ROW_TILE):
    t, d = x1.shape
    nt = t // tm

    def bmap(i):
        return (i * tm // rows_per_batch, 0, 0)

    row = pl.BlockSpec((tm, d), lambda i: (i, 0))
    mod = pl.BlockSpec((1, 1, d), bmap)
    vec = pl.BlockSpec((1, d), lambda i: (0, 0))
    in_specs = [row, pl.BlockSpec((tm, d // LANES, LANES), lambda i: (i, 0, 0)),
                pl.BlockSpec((tm, d // LANES, LANES), lambda i: (nt + i, 0, 0)),
                pl.BlockSpec((tm, LANES), lambda i: (i, 0)), mod, vec, vec]
    args = [x1, y2, y2, gate, g2, lng, lnb]
    if proj is None:
        return pl.pallas_call(
            _combine_kernel, grid=(nt,), in_specs=in_specs, out_specs=row,
            out_shape=jax.ShapeDtypeStruct((t, d), F32),
            compiler_params=_params("parallel"), name="combine_ln")(*args)
    sc, sh, w = proj
    return pl.pallas_call(
        _combine_proj_kernel, grid=(nt,),
        in_specs=in_specs + [mod, mod, _resident(w.shape)],
        out_specs=[row, pl.BlockSpec((tm, w.shape[1]), lambda i: (i, 0))],
        out_shape=[jax.ShapeDtypeStruct((t, d), F32), jax.ShapeDtypeStruct((t, w.shape[1]), F32)],
        compiler_params=_params("parallel"), name="combine_ln_proj")(*args, sc, sh, w)


def _pool_out_kernel(up_ref, uc_ref, un_ref, wg_ref, ps_ref, wo_ref, *rest, n_seq):
    tm, d = uc_ref.shape
    n_grp = len(POOL_WINDOWS)
    ch = d // n_grp
    halo = POOL_HALO
    tiles = _sub_tiles(tm)
    lhs = []
    for r in tiles:
        n_r = r.stop - r.start
        pos0 = (pl.program_id(0) * tm + r.start) % n_seq
        e_pos = pos0 - halo + lax.broadcasted_iota(jnp.int32, (n_r + 2 * halo, 1), 0)
        e_ok = (e_pos >= 0) & (e_pos < n_seq)
        t_pos = pos0 + lax.broadcasted_iota(jnp.int32, (n_r, 1), 0)
        z = []
        for gi, w in enumerate(POOL_WINDOWS):
            cs = slice(gi * ch, (gi + 1) * ch)
            u = uc_ref[r, cs]
            before = up_ref[:, cs] if r.start == 0 else uc_ref[r.start - halo:r.start, cs]
            after = un_ref[:, cs] if r.stop == tm else uc_ref[r.stop:r.stop + halo, cs]
            ext = jnp.where(e_ok, jnp.concatenate([before, u, after], axis=0), 0.0)
            a, span = ext, 1
            while span < w:
                a = a[:a.shape[0] - span] + a[span:]
                span *= 2
            start = halo - w // 2
            win = a[start:start + n_r]
            cnt = (jnp.minimum(t_pos + (w - w // 2), n_seq) - jnp.maximum(t_pos - w // 2, 0)).astype(F32)
            mixed = (win / cnt - u).astype(BF16)
            z.append((_dot(mixed, wg_ref[gi]) * ps_ref[:, cs]).astype(BF16))
        lhs.append(jnp.concatenate(z, axis=-1))
    ys = [_dot(a, wo_ref[...]) for a in lhs]
    for r, y in zip(tiles, ys):
        _post_mix(y, *rest, r)


def _pool_out(u, w_grp, scale, w_out, x, g1, sc2, sh2, lng, lnb, wr, rb, n_seq, tm=ROW_TILE):
    t, d = x.shape
    hb = tm // POOL_HALO
    n_hb = t // POOL_HALO
    post_in, post_out = _post_specs(d, tm, n_seq)
    return pl.pallas_call(
        functools.partial(_pool_out_kernel, n_seq=n_seq),
        grid=(t // tm,),
        in_specs=[
            pl.BlockSpec((POOL_HALO, d), lambda i: (jnp.maximum(i * hb - 1, 0), 0)),
            pl.BlockSpec((tm, d), lambda i: (i, 0)),
            pl.BlockSpec((POOL_HALO, d), lambda i: (jnp.minimum((i + 1) * hb, n_hb - 1), 0)),
            _resident(w_grp.shape),
            pl.BlockSpec((1, d), lambda i: (0, 0)),
            _resident(w_out.shape),
        ] + post_in,
        out_specs=post_out,
        out_shape=_post_out_shapes(t, d),
        compiler_params=_params("parallel"),
        name="pool_out",
    )(u, u, u, w_grp, scale, w_out, x, g1, sc2, sh2, lng, lnb, wr, rb)


def _moe_kernel(be_ref, nv_ref, first_ref, ws_ref, nxt_ref, idx_ref, idxn_ref, idxp_ref, tok_hbm, w1_hbm, w3_hbm, w2_hbm,
                y_hbm, xbuf, ybuf, xb_ref, wf1, wf3, wf2, w1b, w3b, w2b, gsem, ssem, wsem, *, n_tok, layer, n_blocks):
    i = pl.program_id(0)
    used = nv_ref[jnp.minimum(i, n_blocks - 1)] > 0
    used = used & (i < n_blocks)
    prev_used = (i > 0) & (nv_ref[jnp.maximum(i - 1, 0)] > 0)
    xs = i % 2

    def weight_copies(e, ws):
        return (pltpu.make_async_copy(w1_hbm.at[layer, e], wf1.at[ws], wsem.at[ws]),
                pltpu.make_async_copy(w3_hbm.at[layer, e], wf3.at[ws], wsem.at[ws]),
                pltpu.make_async_copy(w2_hbm.at[layer, e], wf2.at[ws], wsem.at[ws]))

    def gather_start(idx, slot):
        for r in range(MOE_ROWS):
            tok = idx[0, 0, r] & (n_tok - 1)
            pltpu.make_async_copy(tok_hbm.at[tok], xbuf.at[slot, r], gsem.at[slot]).start()

    def gather_wait(slot):
        pltpu.make_async_copy(tok_hbm.at[pl.ds(0, MOE_ROWS)], xbuf.at[slot], gsem.at[slot]).wait()

    def scatter_start(idx, slot):
        for r in range(MOE_ROWS):
            pltpu.make_async_copy(ybuf.at[slot, r], y_hbm.at[idx[0, 0, r]], ssem.at[slot]).start(priority=r % 2)

    def scatter_wait(slot):
        pltpu.make_async_copy(ybuf.at[slot], y_hbm.at[pl.ds(0, MOE_ROWS)], ssem.at[slot]).wait()

    @pl.when(i == 0)
    def _():
        xbuf[...] = jnp.zeros_like(xbuf)
        ybuf[...] = jnp.zeros_like(ybuf)
        spare0 = pltpu.make_async_copy(
            ybuf.at[0], y_hbm.at[pl.ds(MOE_TOP_K * n_tok, MOE_ROWS)], ssem.at[0])
        spare0.start()
        for cp in weight_copies(be_ref[0], 0):
            cp.start(priority=WEIGHT_DMA_PRIORITY)
        gather_start(idx_ref, 0)

    @pl.when(used)
    def _():
        ws = ws_ref[i]

        @pl.when(first_ref[i] == 1)
        def _():
            for cp in weight_copies(be_ref[i], ws):
                cp.wait()
            nxt = nxt_ref[i]

            @pl.when(nxt >= 0)
            def _():
                for cp in weight_copies(nxt, 1 - ws):
                    cp.start(priority=WEIGHT_DMA_PRIORITY)

            w1b[...] = wf1[ws].astype(BF16)
            w3b[...] = wf3[ws].astype(BF16)
            w2b[...] = wf2[ws].astype(BF16)

        def compute(n_rows):
            rows = slice(0, n_rows)
            gather_wait(xs)
            xb_ref[rows] = _rows_from_tiles(xbuf[xs, rows].astype(F32)).astype(BF16)
            gather_start(idxn_ref, 1 - xs)
            scatter_start(idxp_ref, 1 - xs)
            xb = xb_ref[rows]
            h = (_silu(_dot(xb, w1b[...])) * _dot(xb, w3b[...])).astype(BF16)
            y = _tiles_from_rows(_dot(h, w2b[...])).astype(BF16)
            scatter_wait(xs)
            ybuf[xs, rows] = y

        half = MOE_ROWS // 2
        nv = nv_ref[i]

        @pl.when(nv > half)
        def _():
            compute(MOE_ROWS)

        @pl.when(nv <= half)
        def _():
            compute(half)

    @pl.when(jnp.logical_not(used) & prev_used)
    def _():
        gather_wait(xs)
        scatter_start(idxp_ref, 1 - xs)
        scatter_wait(1 - xs)
        scatter_wait(xs)


def _moe_dispatch(eid_t, n_blocks):
    n_tok = eid_t.shape[1]
    n_slots = (n_blocks + 2) * MOE_ROWS
    slot_rows = -(-n_slots // SMEM_1D_TILE)
    assert n_blocks <= LANES and n_tok % SMEM_1D_TILE == 0 and SMEM_1D_TILE % MOE_ROWS == 0
    slot, meta = pl.pallas_call(
        functools.partial(_dispatch_kernel, n_tok=n_tok),
        in_specs=[pl.BlockSpec(memory_space=pltpu.VMEM)],
        out_specs=[pl.BlockSpec(memory_space=pltpu.SMEM), pl.BlockSpec(memory_space=pltpu.VMEM)],
        out_shape=[jax.ShapeDtypeStruct((slot_rows * SMEM_1D_TILE,), jnp.int32),
                   jax.ShapeDtypeStruct((MOE_ID_ROWS, LANES), jnp.int32)],
        scratch_shapes=[pltpu.VMEM((MOE_TOP_K * n_tok,), jnp.int32),
                        pltpu.VMEM((slot_rows * SMEM_1D_TILE,), jnp.int32),
                        pltpu.SMEM((MOE_TOP_K * n_tok,), jnp.int32),
                        pltpu.SemaphoreType.DMA(())],
        name="moe_dispatch",
    )(eid_t)
    return (slot[:n_slots].reshape(n_blocks + 2, 1, MOE_ROWS),) + tuple(meta[r, :n_blocks] for r in range(5))


def _dispatch_kernel(eid_ref, slot_ref, meta_ref, dest_vmem, init_vmem, dest_smem, sem, *, n_tok):
    tile = MOE_ROWS
    n_tiles = n_tok // tile
    sub = lax.broadcasted_iota(jnp.int32, (N_EXPERTS, tile), 0)
    si = lax.broadcasted_iota(jnp.int32, (tile, tile), 0)
    ti = lax.broadcasted_iota(jnp.int32, (tile, tile), 1)
    before = jnp.where(si < ti, 1.0, 0.0).astype(BF16)

    def one_hots(j):
        ids = eid_ref[:, j * tile:(j + 1) * tile]
        return [jnp.where(sub == ids[k:k + 1], 1.0, 0.0) for k in range(MOE_TOP_K)]

    carry = jnp.zeros((N_EXPERTS, 1), F32)
    ranks = []
    for j in range(n_tiles):
        oh = one_hots(j)
        both = oh[0] + oh[1]
        seen = carry + _dot(both.astype(BF16), before)
        ranks.append([jnp.sum(seen * o, axis=0, keepdims=True) for o in oh])
        carry = carry + jnp.sum(both, axis=1, keepdims=True)

    counts = carry
    nblk = jnp.floor((counts + float(MOE_ROWS - 1)) * (1.0 / MOE_ROWS))
    ei = lax.broadcasted_iota(jnp.int32, (N_EXPERTS, N_EXPERTS), 0)
    ej = lax.broadcasted_iota(jnp.int32, (N_EXPERTS, N_EXPERTS), 1)
    lower = jnp.where(ej < ei, 1.0, 0.0).astype(BF16)
    first_blk = _dot(lower, jnp.broadcast_to(nblk, (N_EXPERTS, LANES)).astype(BF16))[:, 0:1]
    first_slot = first_blk * float(MOE_ROWS)

    per_row = SMEM_1D_TILE // tile
    for k in range(MOE_TOP_K):
        for q in range(n_tiles // per_row):
            parts = []
            for j in range(q * per_row, (q + 1) * per_row):
                parts.append(jnp.sum(first_slot * one_hots(j)[k], axis=0, keepdims=True) + ranks[j][k])
            dest = jnp.concatenate(parts, axis=1).astype(jnp.int32) + MOE_ROWS
            dest_vmem[pl.ds(k * n_tok + q * SMEM_1D_TILE, SMEM_1D_TILE)] = dest.reshape(SMEM_1D_TILE)

    lane = lax.broadcasted_iota(jnp.int32, (1, SMEM_1D_TILE), 1)
    for q in range(init_vmem.shape[0] // SMEM_1D_TILE):
        pos = q * SMEM_1D_TILE + lane
        spare = MOE_TOP_K * n_tok + ((pos // MOE_ROWS + 1) % 2) * MOE_ROWS + pos % MOE_ROWS
        init_vmem[pl.ds(q * SMEM_1D_TILE, SMEM_1D_TILE)] = spare.reshape(SMEM_1D_TILE)
    copies = [pltpu.make_async_copy(dest_vmem, dest_smem, sem), pltpu.make_async_copy(init_vmem, slot_ref, sem)]
    for cp in copies:
        cp.start()
    for cp in copies:
        cp.wait()

    def place(t, carry_):
        for k in range(MOE_TOP_K):
            slot_ref[dest_smem[k * n_tok + t]] = k * n_tok + t
        return carry_

    lax.fori_loop(0, n_tok, place, 0, unroll=8)

    b = lax.broadcasted_iota(jnp.int32, (N_EXPERTS, LANES), 1).astype(F32)
    e_col = lax.broadcasted_iota(jnp.int32, (N_EXPERTS, LANES), 0).astype(F32)
    b_row = b[0:1]
    be = jnp.minimum(jnp.sum(jnp.where(first_blk + nblk <= b, 1.0, 0.0), axis=0, keepdims=True), N_EXPERTS - 1.0)
    mine = e_col == be
    cnt_b = jnp.sum(jnp.where(mine, counts, 0.0), axis=0, keepdims=True)
    start_b = jnp.sum(jnp.where(mine, first_blk, 0.0), axis=0, keepdims=True)
    nv = jnp.clip(cnt_b - (b_row - start_b) * MOE_ROWS, 0.0, float(MOE_ROWS))
    nv = jnp.where(b_row < jnp.sum(nblk, axis=0, keepdims=True), nv, 0.0)
    first = jnp.where((nv > 0) & ((b_row == 0) | (be != pltpu.roll(be, 1, 1))), 1.0, 0.0)
    li = lax.broadcasted_iota(jnp.int32, (LANES, LANES), 0)
    lj = lax.broadcasted_iota(jnp.int32, (LANES, LANES), 1)
    upto = jnp.where(li <= lj, 1.0, 0.0).astype(BF16)
    run = _dot(jnp.broadcast_to(first, (MOE_ID_ROWS, LANES)).astype(BF16), upto)[0:1] - 1.0
    ws = run - 2.0 * jnp.floor(run * 0.5)
    later = jnp.min(jnp.where((e_col > be) & (counts > 0), e_col, float(LANES)), axis=0, keepdims=True)
    nxt = jnp.where(later >= float(N_EXPERTS), -1.0, later)
    rows = [be, nv, first, ws, nxt] + [jnp.zeros_like(be)] * (MOE_ID_ROWS - 5)
    meta_ref[...] = jnp.concatenate(rows, axis=0).astype(jnp.int32)


def _moe_experts(tok, eid, w1, w3, w2, layer):
    n_tok, n_sub, _ = tok.shape
    d = n_sub * LANES
    assert n_tok & (n_tok - 1) == 0
    ff = w1.shape[3]
    n_assign = n_tok * MOE_TOP_K
    n_blocks = -(-(n_assign + N_EXPERTS * (MOE_ROWS - 1)) // MOE_ROWS)
    slot, be, nv, first, ws, nxt = _moe_dispatch(eid, n_blocks)
    grid_spec = pltpu.PrefetchScalarGridSpec(
        num_scalar_prefetch=5,
        grid=(n_blocks + 1,),
        in_specs=[
            pl.BlockSpec((1, 1, MOE_ROWS), lambda i, *_: (i + 1, 0, 0), memory_space=pltpu.SMEM),
            pl.BlockSpec((1, 1, MOE_ROWS), lambda i, *_: (jnp.minimum(i + 2, n_blocks + 1), 0, 0),
                         memory_space=pltpu.SMEM),
            pl.BlockSpec((1, 1, MOE_ROWS), lambda i, *_: (i, 0, 0), memory_space=pltpu.SMEM),
            pl.BlockSpec(memory_space=pl.ANY),
            pl.BlockSpec(memory_space=pl.ANY),
            pl.BlockSpec(memory_space=pl.ANY),
            pl.BlockSpec(memory_space=pl.ANY),
        ],
        out_specs=pl.BlockSpec(memory_space=pl.ANY),
        scratch_shapes=[
            pltpu.VMEM((2, MOE_ROWS, n_sub, LANES), BF16), pltpu.VMEM((2, MOE_ROWS, n_sub, LANES), BF16),
            pltpu.VMEM((MOE_ROWS, d), BF16),
            pltpu.VMEM((2, d, ff), F32), pltpu.VMEM((2, d, ff), F32), pltpu.VMEM((2, ff, d), F32),
            pltpu.VMEM((d, ff), BF16), pltpu.VMEM((d, ff), BF16), pltpu.VMEM((ff, d), BF16),
            pltpu.SemaphoreType.DMA((2,)), pltpu.SemaphoreType.DMA((2,)), pltpu.SemaphoreType.DMA((2,)),
        ],
    )
    return pl.pallas_call(
        functools.partial(_moe_kernel, n_tok=n_tok, layer=layer, n_blocks=n_blocks),
        grid_spec=grid_spec,
        out_shape=jax.ShapeDtypeStruct((MOE_TOP_K * n_tok + 2 * MOE_ROWS, n_sub, LANES), BF16),
        compiler_params=pltpu.CompilerParams(dimension_semantics=("arbitrary",),
                                             vmem_limit_bytes=MOE_VMEM_LIMIT_BYTES),
        name="moe_experts",
    )(be, nv, first, ws, nxt, slot, slot, slot, tok, w1, w3, w2)


def _rope_tables(n_seq):
    t = np.arange(n_seq)
    row = (t // GRID_W).astype(np.float32)
    col = (t % GRID_W).astype(np.float32)
    inv_freq = np.float32(ROPE_BASE) ** (-np.arange(ROPE_FREQS, dtype=np.float32) / np.float32(ROPE_FREQS))
    ang_r = row[:, None] * inv_freq[None, :]
    ang_c = col[:, None] * inv_freq[None, :]
    cos = np.concatenate([np.cos(ang_r)] * 2 + [np.cos(ang_c)] * 2, axis=-1)
    sin = np.concatenate([-np.sin(ang_r), np.sin(ang_r), -np.sin(ang_c), np.sin(ang_c)], axis=-1)
    return jnp.asarray(cos, F32), jnp.asarray(sin, F32)


def _router_weights(w_g, b_g, w_e, b_e):
    d = w_g.shape[0]
    n = w_g.shape[1] + w_e.shape[1]
    wr = jnp.concatenate([w_g, w_e, jnp.zeros((d, LANES - n), F32)], axis=1)
    rb = jnp.concatenate([b_g, b_e, jnp.zeros((LANES - n,), F32)]).reshape(1, LANES)
    hi = wr.astype(BF16)
    lo = (wr - hi.astype(F32)).astype(BF16)
    return jnp.concatenate([hi, lo], axis=1), rb


def kernel(x, c, ctx, c_ctx, ada_w, ada_b, ln_g, ln_b, mix_w_in, att_sink, hg_lb, hg_norm_g, mix_w_out, pool_w_in, pool_w_grp, pool_scale, pool_w_out, rt_group_w, rt_group_b, rt_expert_w, rt_expert_b, moe_w1, moe_w3, moe_w2):
    b, n, d = x.shape
    n_ctx = ctx.shape[1]
    t = b * n
    xf = x.reshape(t, d)
    ctxf = ctx.reshape(b * n_ctx, d)

    cond = jnp.concatenate([c, c_ctx[None, :], jnp.zeros((SUBLANES - b - 1, d), F32)], axis=0)
    mod = _ada_mod(cond, ada_w, ada_b)

    def chunk(l, j, rows=slice(0, b)):
        return mod[l, rows, j * d:(j + 1) * d][:, None, :]

    w_in = mix_w_in[0].astype(BF16)
    cos, sin = _rope_tables(n)
    q_w, kv_w = ATT_HEADS * HEAD_DIM, ATT_KV_HEADS * HEAD_DIM
    n_att = q_w + 2 * kv_w
    hk = HG_HEADS * HG_KEY
    assert n_att == 2 * PROJ_TN and hk == PROJ_TN
    a_lat, f_lat = _mod_matmul(xf, chunk(0, 1), chunk(0, 0), w_in, cos, sin,
                               lambda j: jnp.where(j < 3, j, jnp.where(j < 5, j + 2, j - 2)), n_att + 3 * hk, 2 * hk,
                               n_q=q_w, n_rope=q_w + kv_w, n_seq=n, tm=PROJ_TM, tn=PROJ_TN)
    ctx_rows = slice(b, b + 1)
    a_ctx, f_ctx = _mod_matmul(ctxf, chunk(0, 1, ctx_rows), chunk(0, 0, ctx_rows), w_in, cos, sin,
                               lambda j: jnp.where(j < 1, 1, jnp.where(j < 2, 5, j + 1)), 2 * kv_w + hk, 2 * hk,
                               n_q=0, n_rope=0, n_seq=n, tm=b * n_ctx, tn=PROJ_TN)
    att = _window_attention(a_lat, a_ctx, att_sink[0], b, n, n_ctx)
    o_f, o_b = _hgrn2_scan(a_lat, f_lat, a_ctx, f_ctx, hg_lb, b, n, n_ctx)
    wr, rb = _router_weights(rt_group_w[0], rt_group_b[0], rt_expert_w[0], rt_expert_b[0])
    x1, tok, eid, gate = _even_out(
        att, o_f, o_b, a_lat, hg_norm_g[0][None, :], mix_w_out[0].astype(BF16), xf,
        chunk(0, 2), chunk(0, 4), chunk(0, 3), ln_g[0, 0][None, :], ln_b[0, 0][None, :], wr, rb, n)
    y2 = _moe_experts(tok, eid, moe_w1, moe_w3, moe_w2, 0)

    x2, u = _combine_call(x1, y2, gate, chunk(0, 5), ln_g[0, 1][None, :], ln_b[0, 1][None, :], n,
                          proj=(chunk(1, 1), chunk(1, 0), pool_w_in[0].astype(BF16)))
    wr, rb = _router_weights(rt_group_w[1], rt_group_b[1], rt_expert_w[1], rt_expert_b[1])
    x3, tok, eid, gate = _pool_out(
        u, pool_w_grp[0].astype(BF16), pool_scale[0][None, :], pool_w_out[0].astype(BF16), x2,
        chunk(1, 2), chunk(1, 4), chunk(1, 3), ln_g[1, 0][None, :], ln_b[1, 0][None, :], wr, rb, n)
    y2 = _moe_experts(tok, eid, moe_w1, moe_w3, moe_w2, 1)
    out = _combine_call(x3, y2, gate, chunk(1, 5), ln_g[1, 1][None, :], ln_b[1, 1][None, :], n)
    return out.reshape(b, n, d)
```

```python
import functools

import jax
import jax.numpy as jnp
import numpy as np
from jax import lax
from jax.experimental import pallas as pl
from jax.experimental.pallas import tpu as pltpu

F32 = jnp.float32
BF16 = jnp.bfloat16

LANES = 128
SUBLANES = 8
MXU_TILE = 256
VMEM_LIMIT_BYTES = 56 * 1024 * 1024
MOE_VMEM_LIMIT_BYTES = 60 * 1024 * 1024

GRID_W = 64
ATT_HEADS = 8
ATT_KV_HEADS = 4
ATT_GROUP = ATT_HEADS // ATT_KV_HEADS
HEAD_DIM = 128
ROPE_FREQS = HEAD_DIM // 4
WINDOW = 128
ATT_BLOCK = 128
ROPE_BASE = 10000.0
HG_HEADS = 8
HG_KEY = 128
HG_CHUNK = 64
HG_STEP_CHUNKS = 4
HG_SUB = 16
HG_FAST_RANGE = 80.0
NORM_EPS = 1e-6
POOL_WINDOWS = (2, 4, 8, 16)
POOL_HALO = 8
MOE_GROUPS = 4
MOE_EXPERTS_PER_GROUP = 8
N_EXPERTS = MOE_GROUPS * MOE_EXPERTS_PER_GROUP
MOE_TOP_K = 2
MOE_ROWS = 256
ROW_TILE = 512
SUB_ROWS = 256
PROJ_TM = 1024
PROJ_TN = 1024
MOE_ID_ROWS = 8
SMEM_1D_TILE = 1024
WEIGHT_DMA_PRIORITY = 1
LN_EPS = 1e-5
DEPTH = 2
DEEPNORM_ALPHA = (2 * DEPTH) ** 0.25


def _dot(a, b):
    return jnp.dot(a, b, preferred_element_type=F32)


def _dot_nt(a, b):
    return lax.dot_general(a, b, (((1,), (1,)), ((), ())), preferred_element_type=F32)


def _dot_tn(a, b):
    return lax.dot_general(a, b, (((0,), (0,)), ((), ())), preferred_element_type=F32)


def _sigmoid(x):
    return 1.0 / (1.0 + jnp.exp(-x))


def _silu(x):
    return x * _sigmoid(x)


def _params(*sem):
    return pltpu.CompilerParams(dimension_semantics=sem, vmem_limit_bytes=VMEM_LIMIT_BYTES)


def _tiles_from_rows(x):
    n = x.shape[1] // LANES
    return jnp.swapaxes(jnp.stack([x[:, s * LANES:(s + 1) * LANES] for s in range(n)], axis=0), 0, 1)


def _rows_from_tiles(x3):
    xt = jnp.swapaxes(x3, 0, 1)
    return jnp.concatenate([xt[s] for s in range(xt.shape[0])], axis=-1)


def _layer_norm(z, g, b):
    mu = jnp.mean(z, axis=-1, keepdims=True)
    zc = z - mu
    var = jnp.mean(zc * zc, axis=-1, keepdims=True)
    return zc * lax.rsqrt(var + LN_EPS) * g + b


def _ada_kernel(s_ref, w_ref, b_ref, o_ref):
    s = _silu(s_ref[...]).astype(BF16)
    o_ref[0] = _dot(s, w_ref[0].astype(BF16)) + b_ref[0]


def _ada_mod(s, ada_w, ada_b, tn=PROJ_TN):
    n_l, d, n = ada_w.shape
    return pl.pallas_call(
        _ada_kernel,
        grid=(n_l, n // tn),
        in_specs=[
            pl.BlockSpec((SUBLANES, d), lambda l, j: (0, 0)),
            pl.BlockSpec((1, d, tn), lambda l, j: (l, 0, j)),
            pl.BlockSpec((1, 1, tn), lambda l, j: (l, 0, j)),
        ],
        out_specs=pl.BlockSpec((1, SUBLANES, tn), lambda l, j: (l, 0, j)),
        out_shape=jax.ShapeDtypeStruct((n_l, SUBLANES, n), F32),
        compiler_params=_params("parallel", "parallel"),
        name="ada_mod",
    )(s, ada_w, ada_b.reshape(n_l, 1, n))


def _rope(t, cos, sin_signed, first_half):
    partner = jnp.where(first_half, pltpu.roll(t, HEAD_DIM - ROPE_FREQS, 1), pltpu.roll(t, ROPE_FREQS, 1))
    return t * cos + partner * sin_signed


def _modmm_kernel(x_ref, sc_ref, sh_ref, w_ref, cos_ref, sin_ref, oa_ref, ob_ref, xs_ref, *,
                  n_q, n_rope, n_a_tiles):
    j = pl.program_id(1)
    tm, tn = x_ref.shape[0], w_ref.shape[1]
    halves = [slice(0, tm // 2), slice(tm // 2, tm)]

    def dots():
        return [_dot(xs_ref[r], w_ref[...]) for r in halves]

    def store_bf16(jt, r, acc):
        lane = lax.broadcasted_iota(jnp.int32, (1, HEAD_DIM), 1)
        first_half = (lane % (2 * ROPE_FREQS)) < ROPE_FREQS
        for h in range(tn // HEAD_DIM):
            sl = slice(h * HEAD_DIM, (h + 1) * HEAD_DIM)
            col = jt * tn + h * HEAD_DIM
            if col < n_rope:
                scale = HEAD_DIM ** -0.5 if col < n_q else 1.0
                oa_ref[r, sl] = _rope(acc[:, sl], cos_ref[r] * scale, sin_ref[r] * scale, first_half).astype(BF16)
            else:
                oa_ref[r, sl] = acc[:, sl].astype(BF16)

    for jt in range(n_a_tiles):
        @pl.when(j == jt)
        def _(jt=jt):
            if jt == 0:
                for r in halves:
                    xs_ref[r] = (x_ref[r] * (1.0 + sc_ref[0]) + sh_ref[0]).astype(BF16)
            for r, acc in zip(halves, dots()):
                store_bf16(jt, r, acc)

    @pl.when(j >= n_a_tiles)
    def _():
        for r, acc in zip(halves, dots()):
            ob_ref[r] = acc


def _mod_matmul(x, sc, sh, w, cos, sin, col_map, n_a, n_b, n_q, n_rope, n_seq, tm, tn):
    m, k = x.shape
    rows_per_mod = m // sc.shape[0]
    ta, tb = n_a // tn, n_b // tn
    tab = pl.BlockSpec((tm, HEAD_DIM), lambda i, j: ((i * tm % n_seq) // tm, 0))
    return pl.pallas_call(
        functools.partial(_modmm_kernel, n_q=n_q, n_rope=n_rope, n_a_tiles=ta),
        grid=(m // tm, ta + tb),
        in_specs=[
            pl.BlockSpec((tm, k), lambda i, j: (i, 0)),
            pl.BlockSpec((1, 1, k), lambda i, j: (i * tm // rows_per_mod, 0, 0)),
            pl.BlockSpec((1, 1, k), lambda i, j: (i * tm // rows_per_mod, 0, 0)),
            pl.BlockSpec((k, tn), lambda i, j: (0, col_map(j))),
            tab, tab,
        ],
        out_specs=[pl.BlockSpec((tm, tn), lambda i, j: (i, jnp.minimum(j, ta - 1))),
                   pl.BlockSpec((tm, tn), lambda i, j: (i, jnp.maximum(j - ta, 0)))],
        out_shape=[jax.ShapeDtypeStruct((m, n_a), BF16), jax.ShapeDtypeStruct((m, n_b), F32)],
        scratch_shapes=[pltpu.VMEM((tm, k), BF16)],
        compiler_params=_params("parallel", "arbitrary"),
        name="mod_matmul",
    )(x, sc, sh, w, cos, sin)


def _attn_kernel(sink_ref, q_ref, kp_ref, kc_ref, kn_ref, vp_ref, vc_ref, vn_ref, kx_ref, vx_ref,
                 mp_ref, mn_ref, o_ref, *, n_blocks):
    s_idx = pl.program_id(1)
    blk = ATT_BLOCK
    row1 = lax.broadcasted_iota(jnp.int32, (ATT_GROUP * blk, 1), 0)
    lo, hi = slice(0, blk), slice(blk, 2 * blk)
    subs = [(lo, (kp_ref, vp_ref, lo), (kc_ref, vc_ref, lo), (kc_ref, vc_ref, hi), s_idx > 0, True),
            (hi, (kc_ref, vc_ref, lo), (kc_ref, vc_ref, hi), (kn_ref, vn_ref, lo), True, s_idx < n_blocks // 2 - 1)]
    jobs = [(sub, h) for sub in subs for h in range(ATT_KV_HEADS)]

    def kv(h):
        return slice(h * HEAD_DIM, (h + 1) * HEAD_DIM)

    def gather(sub, h, which):
        blocks = [t[which][t[2], kv(h)] for t in sub[1:4]]
        return jnp.concatenate(blocks + [(kx_ref, vx_ref)[which][:, kv(h)]], axis=0)

    scores = []
    for sub, h in jobs:
        q2 = jnp.concatenate([q_ref[sub[0], (ATT_GROUP * h + g) * HEAD_DIM:(ATT_GROUP * h + g + 1) * HEAD_DIM]
                              for g in range(ATT_GROUP)], axis=0)
        scores.append(_dot_nt(q2, gather(sub, h, 0)))
    probs, dens = [], []
    for (sub, h), s in zip(jobs, scores):
        has_prev, has_next = sub[4], sub[5]
        parts = [jnp.where(has_prev, s[:, :blk] + mp_ref[...], -jnp.inf), s[:, blk:2 * blk],
                 jnp.where(has_next, s[:, 2 * blk:3 * blk] + mn_ref[...], -jnp.inf)]
        parts += [s[:, c:c + blk] for c in range(3 * blk, s.shape[1], blk)]
        sink = jnp.where(row1 < blk, sink_ref[ATT_GROUP * h], sink_ref[ATT_GROUP * h + 1])
        m = jnp.maximum(jnp.max(functools.reduce(jnp.maximum, parts), axis=-1, keepdims=True), sink)
        p = [jnp.exp(x - m) for x in parts]
        dens.append(jnp.sum(functools.reduce(jnp.add, p), axis=-1, keepdims=True) + jnp.exp(sink - m))
        probs.append(jnp.concatenate(p, axis=-1).astype(BF16))
    for (sub, h), p, den in zip(jobs, probs, dens):
        o = _dot(p, gather(sub, h, 1)) / den
        for g in range(ATT_GROUP):
            col = (ATT_GROUP * h + g) * HEAD_DIM
            o_ref[sub[0], col:col + HEAD_DIM] = o[g * blk:(g + 1) * blk].astype(o_ref.dtype)


def _window_attention(qkv, kv_ctx, sink, batch, n_seq, n_ctx):
    assert ATT_GROUP == 2 and WINDOW == ATT_BLOCK
    nb = n_seq // ATT_BLOCK
    qw, kw = ATT_HEADS * HEAD_DIM, ATT_KV_HEADS * HEAD_DIM
    kcol, vcol = qw // kw, qw // kw + 1

    assert nb % 2 == 0
    pair = 2 * ATT_BLOCK

    def before(b, s):
        return b * nb + jnp.maximum(2 * s - 1, 0)

    def after(b, s):
        return b * nb + jnp.minimum(2 * s + 2, nb - 1)

    def own(b, s):
        return b * (nb // 2) + s

    def band(col):
        return [pl.BlockSpec((ATT_BLOCK, kw), lambda b, s: (before(b, s), col)),
                pl.BlockSpec((pair, kw), lambda b, s: (own(b, s), col)),
                pl.BlockSpec((ATT_BLOCK, kw), lambda b, s: (after(b, s), col))]

    kspec, vspec = band(kcol), band(vcol)
    r = jnp.arange(ATT_GROUP * ATT_BLOCK)[:, None] % ATT_BLOCK
    c = jnp.arange(ATT_BLOCK)[None, :]
    mask_prev = jnp.where(c >= r, 0.0, -jnp.inf).astype(F32)
    mask_next = jnp.where(c <= r, 0.0, -jnp.inf).astype(F32)
    mspec = pl.BlockSpec(mask_prev.shape, lambda b, s: (0, 0))
    return pl.pallas_call(
        functools.partial(_attn_kernel, n_blocks=nb),
        grid=(batch, nb // 2),
        in_specs=[pl.BlockSpec(memory_space=pltpu.SMEM),
                  pl.BlockSpec((pair, qw), lambda b, s: (own(b, s), 0))]
        + kspec + vspec
        + [pl.BlockSpec((n_ctx, kw), lambda b, s: (b, 0)), pl.BlockSpec((n_ctx, kw), lambda b, s: (b, 1)),
           mspec, mspec],
        out_specs=pl.BlockSpec((pair, qw), lambda b, s: (own(b, s), 0)),
        out_shape=jax.ShapeDtypeStruct((batch * n_seq, qw), BF16),
        compiler_params=_params("parallel", "parallel"),
        name="window_attention",
    )(sink, qkv, qkv, qkv, qkv, qkv, qkv, qkv, kv_ctx, kv_ctx, mask_prev, mask_next)


def _gla_step(zf, q_raw, v, lb, st_ref, o_ref, rev):
    c_len = HG_CHUNK
    n_rows = zf.shape[0]
    n_sub = n_rows // c_len
    shift = c_len.bit_length() - 1
    order = range(n_sub - 1, -1, -1) if rev else range(n_sub)

    def head(h):
        return slice(h * HG_KEY, (h + 1) * HG_KEY)

    def chunk(i):
        return slice(i * c_len, (i + 1) * c_len)

    def seen(n):
        ri = lax.broadcasted_iota(jnp.int32, (n, n), 0)
        ci = lax.broadcasted_iota(jnp.int32, (n, n), 1)
        return ((ri >> shift) == (ci >> shift)) & ((ci >= ri) if rev else (ci <= ri))

    f = lb + (1.0 - lb) * _sigmoid(zf)
    k = 1.0 - f
    g = jnp.log(f)
    tri = jnp.where(seen(n_rows), 1.0, 0.0).astype(BF16)
    g1 = g.astype(BF16)
    r1 = g - g1.astype(F32)
    g2 = r1.astype(BF16)
    g3 = (r1 - g2.astype(F32)).astype(BF16)
    c = _dot(tri, g1) + _dot(tri, g2) + _dot(tri, g3)
    c_end = [c[i * c_len:i * c_len + 1] if rev else c[(i + 1) * c_len - 1:(i + 1) * c_len] for i in range(n_sub)]
    c_end_rows = jnp.concatenate([jnp.broadcast_to(ce, (c_len, ce.shape[1])) for ce in c_end], axis=0)
    k_end = (k * jnp.exp(c_end_rows - c)).astype(BF16)
    dec = [jnp.exp(ce) for ce in c_end]
    vb = v.astype(BF16)

    def advance(states, i):
        new = []
        for p in range(0, HG_HEADS, 2):
            lanes = slice(p * HG_KEY, (p + 2) * HG_KEY)
            inc = _dot_tn(vb[chunk(i), lanes], k_end[chunk(i), lanes])
            for j in range(2):
                blk = slice(j * HG_KEY, (j + 1) * HG_KEY)
                new.append(states[p + j] * dec[i][:, head(p + j)] + inc[blk, blk])
        return new

    if o_ref is None:
        states = [st_ref[h] for h in range(HG_HEADS)]
        for i in order:
            states = advance(states, i)
        for h in range(HG_HEADS):
            st_ref[h] = states[h]
        return

    q = _silu(q_raw.astype(F32))
    q_in = (q * jnp.exp(c)).astype(BF16)
    lowest = functools.reduce(jnp.minimum, c_end)

    def stack(x, i, h0, n):
        return jnp.concatenate([x[chunk(i), head(h)] for h in range(h0, h0 + n)], axis=0)

    n_qk = MXU_TILE // c_len
    groups = [(i, h0) for i in order for h0 in range(0, HG_HEADS, n_qk)]
    work = {}

    def fast_scores():
        k_in = (k * jnp.exp(-c)).astype(BF16)
        work['raw'] = [_dot_nt(stack(q_in, i, h0, n_qk), stack(k_in, i, h0, n_qk)) for i, h0 in groups]

    def fast_mask():
        same = seen(n_qk * c_len)
        work['masked'] = [jnp.where(same, s, 0.0).astype(BF16) for s in work['raw']]

    def fast_values():
        intra = {}
        for (i, h0), sc in zip(groups, work['masked']):
            pv = _dot(sc, stack(vb, i, h0, n_qk))
            for j in range(n_qk):
                intra[i, h0 + j] = pv[j * c_len:(j + 1) * c_len]
        work['intra'] = intra

    def fast_chain():
        states = [st_ref[h] for h in range(HG_HEADS)]
        for i in order:
            out = []
            for p in range(0, HG_HEADS, 2):
                st_pair = jnp.concatenate([states[p].astype(BF16), states[p + 1].astype(BF16)], axis=0)
                inter = _dot_nt(stack(q_in, i, p, 2), st_pair)
                for j in range(2):
                    out.append(inter[j * c_len:(j + 1) * c_len, j * HG_KEY:(j + 1) * HG_KEY]
                               + work['intra'][i, p + j])
            o_ref[chunk(i), :] = jnp.concatenate(out, axis=1)
            states = advance(states, i)
        for h in range(HG_HEADS):
            st_ref[h] = states[h]

    def exact():
        for i in order:
            r = chunk(i)
            _gla_intra_exact(q[r], k[r], v[r].astype(F32), vb[r], c[r], q_in[r], st_ref, o_ref, r, rev)
            states = advance([st_ref[h] for h in range(HG_HEADS)], i)
            for h in range(HG_HEADS):
                st_ref[h] = states[h]

    return lowest, (fast_scores, fast_mask, fast_values, fast_chain), exact


def _gla_run(dirs):
    parts = [_gla_step(*d) for d in dirs]
    if parts[0] is None:
        return
    lowest = functools.reduce(jnp.minimum, [p[0] for p in parts])
    in_range = jnp.min(lowest) >= -HG_FAST_RANGE

    @pl.when(in_range)
    def _():
        for phase in zip(*[p[1] for p in parts]):
            for fn in phase:
                fn()

    @pl.when(jnp.logical_not(in_range))
    def _():
        for p in parts:
            p[2]()


def _gla_intra_exact(q, k, v, vb, c, q_in, st_ref, o_ref, rows, rev):
    c_len = q.shape[0]
    pairs = []
    size = c_len // 2
    while size >= HG_SUB:
        for lo in range(0, c_len, 2 * size):
            pairs.append((lo, lo + size, lo + 2 * size))
        size //= 2
    scaled = []
    for lo, mid, hi in pairs:
        if rev:
            late, early, bnd = slice(lo, mid), slice(mid, hi), mid
        else:
            late, early, bnd = slice(mid, hi), slice(lo, mid), mid - 1
        cb = c[bnd:bnd + 1]
        q_l = (q[late] * jnp.exp(c[late] - cb)).astype(BF16)
        k_e = (k[early] * jnp.exp(cb - c[early])).astype(BF16)
        scaled.append((late, early, q_l, k_e))
    n_sub = c_len // HG_SUB
    t_idx = lax.broadcasted_iota(jnp.int32, (HG_SUB, 1), 0)
    diag = [[None] * HG_HEADS for _ in range(n_sub)]
    for b in range(n_sub):
        r0 = b * HG_SUB
        qb, cb = q[r0:r0 + HG_SUB], c[r0:r0 + HG_SUB]
        for s in range(HG_SUB):
            row = r0 + s
            ok = (t_idx <= s) if rev else (t_idx >= s)
            w = qb * k[row:row + 1] * jnp.exp(jnp.where(ok, cb - c[row:row + 1], -jnp.inf))
            for h in range(HG_HEADS):
                sl = slice(h * HG_KEY, (h + 1) * HG_KEY)
                contrib = jnp.sum(w[:, sl], axis=-1, keepdims=True) * v[row:row + 1, sl]
                diag[b][h] = contrib if diag[b][h] is None else diag[b][h] + contrib

    for h in range(HG_HEADS):
        sl = slice(h * HG_KEY, (h + 1) * HG_KEY)
        o_h = _dot_nt(q_in[:, sl], st_ref[h].astype(BF16))
        parts = [diag[b][h] for b in range(n_sub)]
        for late, early, q_l, k_e in scaled:
            sc = _dot_nt(q_l[:, sl], k_e[:, sl]).astype(BF16)
            add = _dot(sc, vb[early, sl])
            b0 = late.start // HG_SUB
            for j in range((late.stop - late.start) // HG_SUB):
                parts[b0 + j] = parts[b0 + j] + add[j * HG_SUB:(j + 1) * HG_SUB]
        o_ref[rows, sl] = o_h + jnp.concatenate(parts, axis=0)


def _gla_kernel(lb_ref, *refs, n_ctx_steps):
    lat = [refs[0:3], refs[3:6]]
    ctx = [refs[6:8], refs[8:10]]
    outs, states = refs[10:12], refs[12:14]
    s = pl.program_id(1)

    @pl.when(s == 0)
    def _():
        for st_ref in states:
            st_ref[...] = jnp.zeros_like(st_ref)

    lbs = []
    for d in range(2):
        x = lb_ref[d]
        e = jnp.exp(x - jnp.max(x, axis=0, keepdims=True))
        lbs.append(e[0:1] / jnp.sum(e, axis=0, keepdims=True))

    @pl.when(s < n_ctx_steps)
    def _():
        _gla_run([(ctx[d][0][...], None, ctx[d][1][...], lbs[d], states[d], None, d == 1) for d in range(2)])

    @pl.when(s >= n_ctx_steps)
    def _():
        _gla_run([(lat[d][0][...], lat[d][1][...], lat[d][2][...], lbs[d], states[d], outs[d], d == 1)
                  for d in range(2)])


def _hgrn2_scan(a_lat, f_lat, a_ctx, f_ctx, hg_lb, batch, n_seq, n_ctx):
    hk = HG_HEADS * HG_KEY
    rows = HG_CHUNK * HG_STEP_CHUNKS
    assert n_seq % rows == 0 and n_ctx % rows == 0
    nc, ncc = n_seq // rows, n_ctx // rows

    def lat(rev):
        def index(b, s):
            j = jnp.maximum(s - ncc, 0)
            return b * nc + (nc - 1 - j if rev else j)
        return index

    def ctx(rev):
        def index(b, s):
            j = jnp.minimum(s, ncc - 1)
            return b * ncc + (ncc - 1 - j if rev else j)
        return index

    def blk(index, col):
        return pl.BlockSpec((rows, hk), lambda b, s: (index(b, s), col))

    lat_specs, ctx_specs, out_specs = [], [], []
    for d, rev in enumerate((False, True)):
        lat_specs += [blk(lat(rev), d), blk(lat(rev), 2), blk(lat(rev), 3)]
        ctx_specs += [blk(ctx(rev), d), blk(ctx(rev), 1)]
        out_specs.append(blk(lat(rev), 0))
    return pl.pallas_call(
        functools.partial(_gla_kernel, n_ctx_steps=ncc),
        grid=(batch, ncc + nc),
        in_specs=[pl.BlockSpec((2, hg_lb.shape[1], hk), lambda b, s: (0, 0, 0))] + lat_specs + ctx_specs,
        out_specs=out_specs,
        out_shape=[jax.ShapeDtypeStruct((batch * n_seq, hk), F32)] * 2,
        scratch_shapes=[pltpu.VMEM((HG_HEADS, HG_KEY, HG_KEY), F32)] * 2,
        compiler_params=_params("parallel", "arbitrary"),
        name="hgrn2",
    )(hg_lb, f_lat, a_lat, a_lat, f_lat, a_lat, a_lat, f_ctx, a_ctx, f_ctx, a_ctx)


def _route(tok, wr_ref, rb):
    t_hi = tok.astype(BF16)
    t_lo = (tok - t_hi.astype(F32)).astype(BF16)
    hi_both = _dot(t_hi, wr_ref[...])
    lg = hi_both[:, :LANES] + hi_both[:, LANES:] + _dot(t_lo, wr_ref[:, :LANES]) + rb
    lane = lax.broadcasted_iota(jnp.int32, lg.shape, 1)
    lane_f = lane.astype(F32)
    ninf = -jnp.inf
    gl = jnp.where(lane < MOE_GROUPS, lg, ninf)
    gmax = jnp.max(gl, axis=-1, keepdims=True)
    g_idx = jnp.min(jnp.where(gl == gmax, lane_f, float(LANES)), axis=-1, keepdims=True)
    g_val = 1.0 / jnp.sum(jnp.exp(gl - gmax), axis=-1, keepdims=True)
    e_lane = lane_f - float(MOE_GROUPS)
    lo = g_idx * float(MOE_EXPERTS_PER_GROUP)
    in_grp = (e_lane >= lo) & (e_lane < lo + float(MOE_EXPERTS_PER_GROUP))
    el = jnp.where(in_grp, lg, ninf)
    l1 = jnp.max(el, axis=-1, keepdims=True)
    i1 = jnp.min(jnp.where(el == l1, e_lane, float(LANES)), axis=-1, keepdims=True)
    el2 = jnp.where(e_lane == i1, ninf, el)
    l2 = jnp.max(el2, axis=-1, keepdims=True)
    i2 = jnp.min(jnp.where(el2 == l2, e_lane, float(LANES)), axis=-1, keepdims=True)
    r = jnp.exp(l2 - l1)
    w1 = g_val / (1.0 + r)
    w2 = w1 * r
    eid = jnp.where(lane == 0, i1, jnp.where(lane == 1, i2, 0.0))
    gate = jnp.where(lane == 0, w1, jnp.where(lane == 1, w2, 0.0))
    eid_t = jnp.transpose(eid)[:MOE_ID_ROWS].astype(jnp.int32)
    return eid_t, gate


def _post_mix(y, x_ref, g1_ref, sc2_ref, sh2_ref, lng_ref, lnb_ref, wr_ref, rb_ref,
              x1_ref, tok_ref, eid_ref, gate_ref, r):
    x1 = _layer_norm(DEEPNORM_ALPHA * x_ref[r] + g1_ref[0] * y, lng_ref[...], lnb_ref[...])
    x1_ref[r] = x1
    tok = x1 * (1.0 + sc2_ref[0]) + sh2_ref[0]
    tok_ref[r] = _tiles_from_rows(tok).astype(BF16)
    eid, gate = _route(tok, wr_ref, rb_ref[...])
    eid_ref[:, r] = eid
    gate_ref[r] = gate


def _sub_tiles(n_rows):
    return [slice(s, s + SUB_ROWS) for s in range(0, n_rows, SUB_ROWS)]


def _even_out_kernel(att_ref, of_ref, ob_ref, gt_ref, ng_ref, wo_ref, *rest):
    tiles = _sub_tiles(att_ref.shape[0])
    lhs = []
    for r in tiles:
        o = of_ref[r] + ob_ref[r]
        pieces = []
        for h in range(HG_HEADS):
            oh = o[:, h * HG_KEY:(h + 1) * HG_KEY]
            pieces.append(oh * lax.rsqrt(jnp.mean(oh * oh, axis=-1, keepdims=True) + NORM_EPS))
        hg = (jnp.concatenate(pieces, axis=-1) * ng_ref[...] * _silu(gt_ref[r].astype(F32))).astype(BF16)
        lhs.append(jnp.concatenate([att_ref[r], hg], axis=-1))
    ys = [_dot(a, wo_ref[...]) for a in lhs]
    for r, y in zip(tiles, ys):
        _post_mix(y, *rest, r)


def _post_specs(d, tm, rows_per_batch):
    def bmap(i):
        return (i * tm // rows_per_batch, 0, 0)

    row = pl.BlockSpec((tm, d), lambda i: (i, 0))
    mod = pl.BlockSpec((1, 1, d), bmap)
    vec = pl.BlockSpec((1, d), lambda i: (0, 0))
    rw = pl.BlockSpec((d, 2 * LANES), lambda i: (0, 0))
    in_specs = [row, mod, mod, mod, vec, vec, rw, pl.BlockSpec((1, LANES), lambda i: (0, 0))]
    lane_blk = pl.BlockSpec((tm, LANES), lambda i: (i, 0))
    tiles = pl.BlockSpec((tm, d // LANES, LANES), lambda i: (i, 0, 0))
    out_specs = [row, tiles, pl.BlockSpec((MOE_ID_ROWS, tm), lambda i: (0, i)), lane_blk]
    return in_specs, out_specs


def _post_out_shapes(t, d):
    return [jax.ShapeDtypeStruct((t, d), F32), jax.ShapeDtypeStruct((t, d // LANES, LANES), BF16),
            jax.ShapeDtypeStruct((MOE_ID_ROWS, t), jnp.int32), jax.ShapeDtypeStruct((t, LANES), F32)]


def _resident(shape):
    return pl.BlockSpec(shape, lambda i: (0,) * len(shape), pipeline_mode=pl.Buffered(1))


def _even_out(att, o_f, o_b, p, norm_g, w_out, x, g1, sc2, sh2, lng, lnb, wr, rb, rows_per_batch,
              tm=ROW_TILE):
    t, d = x.shape
    hv = o_f.shape[1]
    post_in, post_out = _post_specs(d, tm, rows_per_batch)
    return pl.pallas_call(
        _even_out_kernel,
        grid=(t // tm,),
        in_specs=[
            pl.BlockSpec((tm, att.shape[1]), lambda i: (i, 0)),
            pl.BlockSpec((tm, hv), lambda i: (i, 0)),
            pl.BlockSpec((tm, hv), lambda i: (i, 0)),
            pl.BlockSpec((tm, hv), lambda i: (i, 4)),
            pl.BlockSpec((1, hv), lambda i: (0, 0)),
            _resident(w_out.shape),
        ] + post_in,
        out_specs=post_out,
        out_shape=_post_out_shapes(t, d),
        compiler_params=_params("parallel"),
        name="even_out",
    )(att, o_f, o_b, p, norm_g, w_out, x, g1, sc2, sh2, lng, lnb, wr, rb)


def _combine(x_ref, ya_ref, yb_ref, gate_ref, g2_ref, lng_ref, lnb_ref, r):
    gate = gate_ref[r]
    y = (gate[:, 0:1] * _rows_from_tiles(ya_ref[r].astype(F32))
         + gate[:, 1:2] * _rows_from_tiles(yb_ref[r].astype(F32)))
    return _layer_norm(DEEPNORM_ALPHA * x_ref[r] + g2_ref[0] * y, lng_ref[...], lnb_ref[...])


def _combine_proj_kernel(x_ref, ya_ref, yb_ref, gate_ref, g2_ref, lng_ref, lnb_ref, sc_ref, sh_ref, w_ref,
                         x2_ref, u_ref):
    tiles = _sub_tiles(x_ref.shape[0])
    lhs = []
    for r in tiles:
        x2 = _combine(x_ref, ya_ref, yb_ref, gate_ref, g2_ref, lng_ref, lnb_ref, r)
        x2_ref[r] = x2
        lhs.append((x2 * (1.0 + sc_ref[0]) + sh_ref[0]).astype(BF16))
    for r, a in zip(tiles, lhs):
        u_ref[r] = _dot(a, w_ref[...])


def _combine_kernel(x_ref, ya_ref, yb_ref, gate_ref, g2_ref, lng_ref, lnb_ref, x2_ref):
    for r in _sub_tiles(x_ref.shape[0]):
        x2_ref[r] = _combine(x_ref, ya_ref, yb_ref, gate_ref, g2_ref, lng_ref, lnb_ref, r)


def _combine_call(x1, y2, gate, g2, lng, lnb, rows_per_batch, proj=None, tm=ROW_TILE):
    t, d = x1.shape
    nt = t // tm

    def bmap(i):
        return (i * tm // rows_per_batch, 0, 0)

    row = pl.BlockSpec((tm, d), lambda i: (i, 0))
    mod = pl.BlockSpec((1, 1, d), bmap)
    vec = pl.BlockSpec((1, d), lambda i: (0, 0))
    in_specs = [row, pl.BlockSpec((tm, d // LANES, LANES), lambda i: (i, 0, 0)),
                pl.BlockSpec((tm, d // LANES, LANES), lambda i: (nt + i, 0, 0)),
                pl.BlockSpec((tm, LANES), lambda i: (i, 0)), mod, vec, vec]
    args = [x1, y2, y2, gate, g2, lng, lnb]
    if proj is None:
        return pl.pallas_call(
            _combine_kernel, grid=(nt,), in_specs=in_specs, out_specs=row,
            out_shape=jax.ShapeDtypeStruct((t, d), F32),
            compiler_params=_params("parallel"), name="combine_ln")(*args)
    sc, sh, w = proj
    return pl.pallas_call(
        _combine_proj_kernel, grid=(nt,),
        in_specs=in_specs + [mod, mod, _resident(w.shape)],
        out_specs=[row, pl.BlockSpec((tm, w.shape[1]), lambda i: (i, 0))],
        out_shape=[jax.ShapeDtypeStruct((t, d), F32), jax.ShapeDtypeStruct((t, w.shape[1]), F32)],
        compiler_params=_params("parallel"), name="combine_ln_proj")(*args, sc, sh, w)


def _pool_out_kernel(up_ref, uc_ref, un_ref, wg_ref, ps_ref, wo_ref, *rest, n_seq):
    tm, d = uc_ref.shape
    n_grp = len(POOL_WINDOWS)
    ch = d // n_grp
    halo = POOL_HALO
    tiles = _sub_tiles(tm)
    lhs = []
    for r in tiles:
        n_r = r.stop - r.start
        pos0 = (pl.program_id(0) * tm + r.start) % n_seq
        e_pos = pos0 - halo + lax.broadcasted_iota(jnp.int32, (n_r + 2 * halo, 1), 0)
        e_ok = (e_pos >= 0) & (e_pos < n_seq)
        t_pos = pos0 + lax.broadcasted_iota(jnp.int32, (n_r, 1), 0)
        z = []
        for gi, w in enumerate(POOL_WINDOWS):
            cs = slice(gi * ch, (gi + 1) * ch)
            u = uc_ref[r, cs]
            before = up_ref[:, cs] if r.start == 0 else uc_ref[r.start - halo:r.start, cs]
            after = un_ref[:, cs] if r.stop == tm else uc_ref[r.stop:r.stop + halo, cs]
            ext = jnp.where(e_ok, jnp.concatenate([before, u, after], axis=0), 0.0)
            a, span = ext, 1
            while span < w:
                a = a[:a.shape[0] - span] + a[span:]
                span *= 2
            start = halo - w // 2
            win = a[start:start + n_r]
            cnt = (jnp.minimum(t_pos + (w - w // 2), n_seq) - jnp.maximum(t_pos - w // 2, 0)).astype(F32)
            mixed = (win / cnt - u).astype(BF16)
            z.append((_dot(mixed, wg_ref[gi]) * ps_ref[:, cs]).astype(BF16))
        lhs.append(jnp.concatenate(z, axis=-1))
    ys = [_dot(a, wo_ref[...]) for a in lhs]
    for r, y in zip(tiles, ys):
        _post_mix(y, *rest, r)


def _pool_out(u, w_grp, scale, w_out, x, g1, sc2, sh2, lng, lnb, wr, rb, n_seq, tm=ROW_TILE):
    t, d = x.shape
    hb = tm // POOL_HALO
    n_hb = t // POOL_HALO
    post_in, post_out = _post_specs(d, tm, n_seq)
    return pl.pallas_call(
        functools.partial(_pool_out_kernel, n_seq=n_seq),
        grid=(t // tm,),
        in_specs=[
            pl.BlockSpec((POOL_HALO, d), lambda i: (jnp.maximum(i * hb - 1, 0), 0)),
            pl.BlockSpec((tm, d), lambda i: (i, 0)),
            pl.BlockSpec((POOL_HALO, d), lambda i: (jnp.minimum((i + 1) * hb, n_hb - 1), 0)),
            _resident(w_grp.shape),
            pl.BlockSpec((1, d), lambda i: (0, 0)),
            _resident(w_out.shape),
        ] + post_in,
        out_specs=post_out,
        out_shape=_post_out_shapes(t, d),
        compiler_params=_params("parallel"),
        name="pool_out",
    )(u, u, u, w_grp, scale, w_out, x, g1, sc2, sh2, lng, lnb, wr, rb)


def _moe_kernel(be_ref, nv_ref, first_ref, ws_ref, nxt_ref, idx_ref, idxn_ref, idxp_ref, tok_hbm, w1_hbm, w3_hbm, w2_hbm,
                y_hbm, xbuf, ybuf, xb_ref, wf1, wf3, wf2, w1b, w3b, w2b, gsem, ssem, wsem, *, n_tok, layer, n_blocks):
    i = pl.program_id(0)
    used = nv_ref[jnp.minimum(i, n_blocks - 1)] > 0
    used = used & (i < n_blocks)
    prev_used = (i > 0) & (nv_ref[jnp.maximum(i - 1, 0)] > 0)
    xs = i % 2

    def weight_copies(e, ws):
        return (pltpu.make_async_copy(w1_hbm.at[layer, e], wf1.at[ws], wsem.at[ws]),
                pltpu.make_async_copy(w3_hbm.at[layer, e], wf3.at[ws], wsem.at[ws]),
                pltpu.make_async_copy(w2_hbm.at[layer, e], wf2.at[ws], wsem.at[ws]))

    def gather_start(idx, slot):
        for r in range(MOE_ROWS):
            tok = idx[0, 0, r] & (n_tok - 1)
            pltpu.make_async_copy(tok_hbm.at[tok], xbuf.at[slot, r], gsem.at[slot]).start()

    def gather_wait(slot):
        pltpu.make_async_copy(tok_hbm.at[pl.ds(0, MOE_ROWS)], xbuf.at[slot], gsem.at[slot]).wait()

    def scatter_start(idx, slot):
        for r in range(MOE_ROWS):
            pltpu.make_async_copy(ybuf.at[slot, r], y_hbm.at[idx[0, 0, r]], ssem.at[slot]).start()

    def scatter_wait(slot):
        pltpu.make_async_copy(ybuf.at[slot], y_hbm.at[pl.ds(0, MOE_ROWS)], ssem.at[slot]).wait()

    @pl.when(i == 0)
    def _():
        xbuf[...] = jnp.zeros_like(xbuf)
        ybuf[...] = jnp.zeros_like(ybuf)
        spare0 = pltpu.make_async_copy(
            ybuf.at[0], y_hbm.at[pl.ds(MOE_TOP_K * n_tok, MOE_ROWS)], ssem.at[0])
        spare0.start()
        for cp in weight_copies(be_ref[0], 0):
            cp.start(priority=WEIGHT_DMA_PRIORITY)
        gather_start(idx_ref, 0)

    @pl.when(used)
    def _():
        ws = ws_ref[i]

        @pl.when(first_ref[i] == 1)
        def _():
            for cp in weight_copies(be_ref[i], ws):
                cp.wait()
            nxt = nxt_ref[i]

            @pl.when(nxt >= 0)
            def _():
                for cp in weight_copies(nxt, 1 - ws):
                    cp.start(priority=WEIGHT_DMA_PRIORITY)

            w1b[...] = wf1[ws].astype(BF16)
            w3b[...] = wf3[ws].astype(BF16)
            w2b[...] = wf2[ws].astype(BF16)

        def compute(n_rows):
            rows = slice(0, n_rows)
            gather_wait(xs)
            xb_ref[rows] = _rows_from_tiles(xbuf[xs, rows].astype(F32)).astype(BF16)
            gather_start(idxn_ref, 1 - xs)
            scatter_start(idxp_ref, 1 - xs)
            xb = xb_ref[rows]
            h = (_silu(_dot(xb, w1b[...])) * _dot(xb, w3b[...])).astype(BF16)
            y = _tiles_from_rows(_dot(h, w2b[...])).astype(BF16)
            scatter_wait(xs)
            ybuf[xs, rows] = y

        half = MOE_ROWS // 2
        nv = nv_ref[i]

        @pl.when(nv > half)
        def _():
            compute(MOE_ROWS)

        @pl.when(nv <= half)
        def _():
            compute(half)

    @pl.when(jnp.logical_not(used) & prev_used)
    def _():
        gather_wait(xs)
        scatter_start(idxp_ref, 1 - xs)
        scatter_wait(1 - xs)
        scatter_wait(xs)


def _moe_dispatch(eid_t, n_blocks):
    n_tok = eid_t.shape[1]
    n_slots = (n_blocks + 2) * MOE_ROWS
    slot_rows = -(-n_slots // SMEM_1D_TILE)
    assert n_blocks <= LANES and n_tok % SMEM_1D_TILE == 0 and SMEM_1D_TILE % MOE_ROWS == 0
    slot, meta = pl.pallas_call(
        functools.partial(_dispatch_kernel, n_tok=n_tok),
        in_specs=[pl.BlockSpec(memory_space=pltpu.VMEM)],
        out_specs=[pl.BlockSpec(memory_space=pltpu.SMEM), pl.BlockSpec(memory_space=pltpu.VMEM)],
        out_shape=[jax.ShapeDtypeStruct((slot_rows * SMEM_1D_TILE,), jnp.int32),
                   jax.ShapeDtypeStruct((MOE_ID_ROWS, LANES), jnp.int32)],
        scratch_shapes=[pltpu.VMEM((MOE_TOP_K * n_tok,), jnp.int32),
                        pltpu.VMEM((slot_rows * SMEM_1D_TILE,), jnp.int32),
                        pltpu.SMEM((MOE_TOP_K * n_tok,), jnp.int32),
                        pltpu.SemaphoreType.DMA(())],
        name="moe_dispatch",
    )(eid_t)
    return (slot[:n_slots].reshape(n_blocks + 2, 1, MOE_ROWS),) + tuple(meta[r, :n_blocks] for r in range(5))


def _dispatch_kernel(eid_ref, slot_ref, meta_ref, dest_vmem, init_vmem, dest_smem, sem, *, n_tok):
    tile = MOE_ROWS
    n_tiles = n_tok // tile
    sub = lax.broadcasted_iota(jnp.int32, (N_EXPERTS, tile), 0)
    si = lax.broadcasted_iota(jnp.int32, (tile, tile), 0)
    ti = lax.broadcasted_iota(jnp.int32, (tile, tile), 1)
    before = jnp.where(si < ti, 1.0, 0.0).astype(BF16)

    def one_hots(j):
        ids = eid_ref[:, j * tile:(j + 1) * tile]
        return [jnp.where(sub == ids[k:k + 1], 1.0, 0.0) for k in range(MOE_TOP_K)]

    carry = jnp.zeros((N_EXPERTS, 1), F32)
    ranks = []
    for j in range(n_tiles):
        oh = one_hots(j)
        both = oh[0] + oh[1]
        seen = carry + _dot(both.astype(BF16), before)
        ranks.append([jnp.sum(seen * o, axis=0, keepdims=True) for o in oh])
        carry = carry + jnp.sum(both, axis=1, keepdims=True)

    counts = carry
    nblk = jnp.floor((counts + float(MOE_ROWS - 1)) * (1.0 / MOE_ROWS))
    ei = lax.broadcasted_iota(jnp.int32, (N_EXPERTS, N_EXPERTS), 0)
    ej = lax.broadcasted_iota(jnp.int32, (N_EXPERTS, N_EXPERTS), 1)
    lower = jnp.where(ej < ei, 1.0, 0.0).astype(BF16)
    first_blk = _dot(lower, jnp.broadcast_to(nblk, (N_EXPERTS, LANES)).astype(BF16))[:, 0:1]
    first_slot = first_blk * float(MOE_ROWS)

    per_row = SMEM_1D_TILE // tile
    for k in range(MOE_TOP_K):
        for q in range(n_tiles // per_row):
            parts = []
            for j in range(q * per_row, (q + 1) * per_row):
                parts.append(jnp.sum(first_slot * one_hots(j)[k], axis=0, keepdims=True) + ranks[j][k])
            dest = jnp.concatenate(parts, axis=1).astype(jnp.int32) + MOE_ROWS
            dest_vmem[pl.ds(k * n_tok + q * SMEM_1D_TILE, SMEM_1D_TILE)] = dest.reshape(SMEM_1D_TILE)

    lane = lax.broadcasted_iota(jnp.int32, (1, SMEM_1D_TILE), 1)
    for q in range(init_vmem.shape[0] // SMEM_1D_TILE):
        pos = q * SMEM_1D_TILE + lane
        spare = MOE_TOP_K * n_tok + ((pos // MOE_ROWS + 1) % 2) * MOE_ROWS + pos % MOE_ROWS
        init_vmem[pl.ds(q * SMEM_1D_TILE, SMEM_1D_TILE)] = spare.reshape(SMEM_1D_TILE)
    copies = [pltpu.make_async_copy(dest_vmem, dest_smem, sem), pltpu.make_async_copy(init_vmem, slot_ref, sem)]
    for cp in copies:
        cp.start()
    for cp in copies:
        cp.wait()

    def place(t, carry_):
        for k in range(MOE_TOP_K):
            slot_ref[dest_smem[k * n_tok + t]] = k * n_tok + t
        return carry_

    lax.fori_loop(0, n_tok, place, 0, unroll=8)

    b = lax.broadcasted_iota(jnp.int32, (N_EXPERTS, LANES), 1).astype(F32)
    e_col = lax.broadcasted_iota(jnp.int32, (N_EXPERTS, LANES), 0).astype(F32)
    b_row = b[0:1]
    be = jnp.minimum(jnp.sum(jnp.where(first_blk + nblk <= b, 1.0, 0.0), axis=0, keepdims=True), N_EXPERTS - 1.0)
    mine = e_col == be
    cnt_b = jnp.sum(jnp.where(mine, counts, 0.0), axis=0, keepdims=True)
    start_b = jnp.sum(jnp.where(mine, first_blk, 0.0), axis=0, keepdims=True)
    nv = jnp.clip(cnt_b - (b_row - start_b) * MOE_ROWS, 0.0, float(MOE_ROWS))
    nv = jnp.where(b_row < jnp.sum(nblk, axis=0, keepdims=True), nv, 0.0)
    first = jnp.where((nv > 0) & ((b_row == 0) | (be != pltpu.roll(be, 1, 1))), 1.0, 0.0)
    li = lax.broadcasted_iota(jnp.int32, (LANES, LANES), 0)
    lj = lax.broadcasted_iota(jnp.int32, (LANES, LANES), 1)
    upto = jnp.where(li <= lj, 1.0, 0.0).astype(BF16)
    run = _dot(jnp.broadcast_to(first, (MOE_ID_ROWS, LANES)).astype(BF16), upto)[0:1] - 1.0
    ws = run - 2.0 * jnp.floor(run * 0.5)
    later = jnp.min(jnp.where((e_col > be) & (counts > 0), e_col, float(LANES)), axis=0, keepdims=True)
    nxt = jnp.where(later >= float(N_EXPERTS), -1.0, later)
    rows = [be, nv, first, ws, nxt] + [jnp.zeros_like(be)] * (MOE_ID_ROWS - 5)
    meta_ref[...] = jnp.concatenate(rows, axis=0).astype(jnp.int32)


def _moe_experts(tok, eid, w1, w3, w2, layer):
    n_tok, n_sub, _ = tok.shape
    d = n_sub * LANES
    assert n_tok & (n_tok - 1) == 0
    ff = w1.shape[3]
    n_assign = n_tok * MOE_TOP_K
    n_blocks = -(-(n_assign + N_EXPERTS * (MOE_ROWS - 1)) // MOE_ROWS)
    slot, be, nv, first, ws, nxt = _moe_dispatch(eid, n_blocks)
    grid_spec = pltpu.PrefetchScalarGridSpec(
        num_scalar_prefetch=5,
        grid=(n_blocks + 1,),
        in_specs=[
            pl.BlockSpec((1, 1, MOE_ROWS), lambda i, *_: (i + 1, 0, 0), memory_space=pltpu.SMEM),
            pl.BlockSpec((1, 1, MOE_ROWS), lambda i, *_: (jnp.minimum(i + 2, n_blocks + 1), 0, 0),
                         memory_space=pltpu.SMEM),
            pl.BlockSpec((1, 1, MOE_ROWS), lambda i, *_: (i, 0, 0), memory_space=pltpu.SMEM),
            pl.BlockSpec(memory_space=pl.ANY),
            pl.BlockSpec(memory_space=pl.ANY),
            pl.BlockSpec(memory_space=pl.ANY),
            pl.BlockSpec(memory_space=pl.ANY),
        ],
        out_specs=pl.BlockSpec(memory_space=pl.ANY),
        scratch_shapes=[
            pltpu.VMEM((2, MOE_ROWS, n_sub, LANES), BF16), pltpu.VMEM((2, MOE_ROWS, n_sub, LANES), BF16),
            pltpu.VMEM((MOE_ROWS, d), BF16),
            pltpu.VMEM((2, d, ff), F32), pltpu.VMEM((2, d, ff), F32), pltpu.VMEM((2, ff, d), F32),
            pltpu.VMEM((d, ff), BF16), pltpu.VMEM((d, ff), BF16), pltpu.VMEM((ff, d), BF16),
            pltpu.SemaphoreType.DMA((2,)), pltpu.SemaphoreType.DMA((2,)), pltpu.SemaphoreType.DMA((2,)),
        ],
    )
    return pl.pallas_call(
        functools.partial(_moe_kernel, n_tok=n_tok, layer=layer, n_blocks=n_blocks),
        grid_spec=grid_spec,
        out_shape=jax.ShapeDtypeStruct((MOE_TOP_K * n_tok + 2 * MOE_ROWS, n_sub, LANES), BF16),
        compiler_params=pltpu.CompilerParams(dimension_semantics=("arbitrary",),
                                             vmem_limit_bytes=MOE_VMEM_LIMIT_BYTES),
        name="moe_experts",
    )(be, nv, first, ws, nxt, slot, slot, slot, tok, w1, w3, w2)


def _rope_tables(n_seq):
    t = np.arange(n_seq)
    row = (t // GRID_W).astype(np.float32)
    col = (t % GRID_W).astype(np.float32)
    inv_freq = np.float32(ROPE_BASE) ** (-np.arange(ROPE_FREQS, dtype=np.float32) / np.float32(ROPE_FREQS))
    ang_r = row[:, None] * inv_freq[None, :]
    ang_c = col[:, None] * inv_freq[None, :]
    cos = np.concatenate([np.cos(ang_r)] * 2 + [np.cos(ang_c)] * 2, axis=-1)
    sin = np.concatenate([-np.sin(ang_r), np.sin(ang_r), -np.sin(ang_c), np.sin(ang_c)], axis=-1)
    return jnp.asarray(cos, F32), jnp.asarray(sin, F32)


def _router_weights(w_g, b_g, w_e, b_e):
    d = w_g.shape[0]
    n = w_g.shape[1] + w_e.shape[1]
    wr = jnp.concatenate([w_g, w_e, jnp.zeros((d, LANES - n), F32)], axis=1)
    rb = jnp.concatenate([b_g, b_e, jnp.zeros((LANES - n,), F32)]).reshape(1, LANES)
    hi = wr.astype(BF16)
    lo = (wr - hi.astype(F32)).astype(BF16)
    return jnp.concatenate([hi, lo], axis=1), rb


def kernel(x, c, ctx, c_ctx, ada_w, ada_b, ln_g, ln_b, mix_w_in, att_sink, hg_lb, hg_norm_g, mix_w_out, pool_w_in, pool_w_grp, pool_scale, pool_w_out, rt_group_w, rt_group_b, rt_expert_w, rt_expert_b, moe_w1, moe_w3, moe_w2):
    b, n, d = x.shape
    n_ctx = ctx.shape[1]
    t = b * n
    xf = x.reshape(t, d)
    ctxf = ctx.reshape(b * n_ctx, d)

    cond = jnp.concatenate([c, c_ctx[None, :], jnp.zeros((SUBLANES - b - 1, d), F32)], axis=0)
    mod = _ada_mod(cond, ada_w, ada_b)

    def chunk(l, j, rows=slice(0, b)):
        return mod[l, rows, j * d:(j + 1) * d][:, None, :]

    w_in = mix_w_in[0].astype(BF16)
    cos, sin = _rope_tables(n)
    q_w, kv_w = ATT_HEADS * HEAD_DIM, ATT_KV_HEADS * HEAD_DIM
    n_att = q_w + 2 * kv_w
    hk = HG_HEADS * HG_KEY
    assert n_att == 2 * PROJ_TN and hk == PROJ_TN
    a_lat, f_lat = _mod_matmul(xf, chunk(0, 1), chunk(0, 0), w_in, cos, sin,
                               lambda j: jnp.where(j < 3, j, jnp.where(j < 5, j + 2, j - 2)), n_att + 3 * hk, 2 * hk,
                               n_q=q_w, n_rope=q_w + kv_w, n_seq=n, tm=PROJ_TM, tn=PROJ_TN)
    ctx_rows = slice(b, b + 1)
    a_ctx, f_ctx = _mod_matmul(ctxf, chunk(0, 1, ctx_rows), chunk(0, 0, ctx_rows), w_in, cos, sin,
                               lambda j: jnp.where(j < 1, 1, jnp.where(j < 2, 5, j + 1)), 2 * kv_w + hk, 2 * hk,
                               n_q=0, n_rope=0, n_seq=n, tm=b * n_ctx, tn=PROJ_TN)
    att = _window_attention(a_lat, a_ctx, att_sink[0], b, n, n_ctx)
    o_f, o_b = _hgrn2_scan(a_lat, f_lat, a_ctx, f_ctx, hg_lb, b, n, n_ctx)
    wr, rb = _router_weights(rt_group_w[0], rt_group_b[0], rt_expert_w[0], rt_expert_b[0])
    x1, tok, eid, gate = _even_out(
        att, o_f, o_b, a_lat, hg_norm_g[0][None, :], mix_w_out[0].astype(BF16), xf,
        chunk(0, 2), chunk(0, 4), chunk(0, 3), ln_g[0, 0][None, :], ln_b[0, 0][None, :], wr, rb, n)
    y2 = _moe_experts(tok, eid, moe_w1, moe_w3, moe_w2, 0)

    x2, u = _combine_call(x1, y2, gate, chunk(0, 5), ln_g[0, 1][None, :], ln_b[0, 1][None, :], n,
                          proj=(chunk(1, 1), chunk(1, 0), pool_w_in[0].astype(BF16)))
    wr, rb = _router_weights(rt_group_w[1], rt_group_b[1], rt_expert_w[1], rt_expert_b[1])
    x3, tok, eid, gate = _pool_out(
        u, pool_w_grp[0].astype(BF16), pool_scale[0][None, :], pool_w_out[0].astype(BF16), x2,
        chunk(1, 2), chunk(1, 4), chunk(1, 3), ln_g[1, 0][None, :], ln_b[1, 0][None, :], wr, rb, n)
    y2 = _moe_experts(tok, eid, moe_w1, moe_w3, moe_w2, 1)
    out = _combine_call(x3, y2, gate, chunk(1, 5), ln_g[1, 1][None, :], ln_b[1, 1][None, :], n)
    return out.reshape(b, n, d)
```
